```python
import jax, jax.numpy as jnp
from jax import lax
import numpy as np

D_MODEL = 1024
BATCH = 8
SEQ = 2048
DEPTH = 1

CTX_LEN = 256
GRID_W = 64
HEAD_DIM = 64
MIX_WIDTH = D_MODEL
ATTN_HEADS = MIX_WIDTH // (2 * HEAD_DIM)
ATTN_KV_HEADS = ATTN_HEADS // 4
RET_HEADS = MIX_WIDTH // (2 * HEAD_DIM)
ATTN_W = ATTN_HEADS * HEAD_DIM
KV_W = ATTN_KV_HEADS * HEAD_DIM
RET_W = RET_HEADS * HEAD_DIM
Q_BLOCK = 128
RET_CHUNK = 128
ROPE_THETA = 10000.0
N_EXPERTS = 256
TOP_K = 8
N_GROUPS = 8
TOPK_GROUPS = 4
EXPERT_FF = D_MODEL // 4
SHARED_FF = D_MODEL // 4
ROUTED_SCALE = 2.5
MOE_BLOCK = 128
EPS = 1e-6

OFF_AK = 0
OFF_AV = OFF_AK + KV_W
OFF_RK = OFF_AV + KV_W
OFF_RV = OFF_RK + RET_W
CTX_KV_COLS = OFF_RV + RET_W
OFF_AQ = CTX_KV_COLS
OFF_RQ = OFF_AQ + ATTN_W
OFF_RG = OFF_RQ + RET_W
IN_COLS = OFF_RG + RET_W

kernel_name = "hymba_attn_retention_moe_dit_block"


def rms_norm(x, w):
    xf = x.astype(jnp.float32)
    y = xf * lax.rsqrt(jnp.mean(xf * xf, axis=-1, keepdims=True) + EPS)
    return (y * w.astype(jnp.float32)).astype(x.dtype)


def rms_unit(xf):
    return xf * lax.rsqrt(jnp.mean(xf * xf, axis=-1, keepdims=True) + EPS)


def modulate(h, shift, scale):
    return h * (1.0 + scale) + shift


def heads(z, off, n_heads):
    B, L = z.shape[:2]
    return z[..., off:off + n_heads * HEAD_DIM].reshape(B, L, n_heads, HEAD_DIM)


def flip(t):
    return jnp.flip(t, axis=1)


def axial_rope_tables(L):
    rows = L // GRID_W
    r = jnp.repeat(jnp.arange(rows), GRID_W).astype(jnp.float32)
    col = jnp.tile(jnp.arange(GRID_W), rows).astype(jnp.float32)
    n_f = HEAD_DIM // 4
    freqs = ROPE_THETA ** (-jnp.arange(n_f, dtype=jnp.float32) / n_f)
    ang = jnp.concatenate([r[:, None] * freqs, col[:, None] * freqs], axis=-1)
    return jnp.cos(ang), jnp.sin(ang)


def apply_rope(x, cos, sin):
    xf = x.astype(jnp.float32).reshape(*x.shape[:-1], HEAD_DIM // 2, 2)
    a, b = xf[..., 0], xf[..., 1]
    cs, sn = cos[None, :, None, :], sin[None, :, None, :]
    out = jnp.stack([a * cs - b * sn, a * sn + b * cs], axis=-1).reshape(x.shape)
    return out.astype(x.dtype)


def latent_attention(q, k, v, kc, vc):
    B, L = q.shape[:2]
    G = ATTN_HEADS // ATTN_KV_HEADS
    nb = L // Q_BLOCK
    scale = HEAD_DIM ** -0.5
    qb = q.reshape(B, nb, Q_BLOCK, ATTN_KV_HEADS, G, HEAD_DIM).transpose(1, 0, 2, 3, 4, 5)

    def one_block(qblk):
        s_lat = jnp.einsum('bqkgd,bskd->bkgqs', qblk, k, preferred_element_type=jnp.float32)
        s_ctx = jnp.einsum('bqkgd,bskd->bkgqs', qblk, kc, preferred_element_type=jnp.float32)
        p = jax.nn.softmax(jnp.concatenate([s_lat, s_ctx], axis=-1) * scale, axis=-1).astype(v.dtype)
        return (jnp.einsum('bkgqs,bskd->bqkgd', p[..., :L], v)
                + jnp.einsum('bkgqs,bskd->bqkgd', p[..., L:], vc))

    o = lax.map(one_block, qb)
    return o.transpose(1, 0, 2, 3, 4, 5).reshape(B, L, ATTN_W)


def context_attention(qc, kc, vc):
    B, Lc = qc.shape[:2]
    G = ATTN_HEADS // ATTN_KV_HEADS
    q = qc.reshape(B, Lc, ATTN_KV_HEADS, G, HEAD_DIM)
    s = jnp.einsum('bqkgd,bskd->bkgqs', q, kc, preferred_element_type=jnp.float32) * HEAD_DIM ** -0.5
    p = jax.nn.softmax(s, axis=-1).astype(vc.dtype)
    return jnp.einsum('bkgqs,bskd->bqkgd', p, vc).reshape(B, Lc, ATTN_W)


def retention_chunkwise(q, k, v, log_g, s0):
    B, L, H, d = q.shape
    C = RET_CHUNK
    n = L // C

    def chunks(t):
        return t.astype(jnp.float32).reshape(B, n, C, H, d).transpose(1, 0, 3, 2, 4)

    qc, kc, vc = chunks(q), chunks(k), chunks(v)
    pos = jnp.arange(C, dtype=jnp.float32)
    diff = pos[:, None] - pos[None, :]
    decay = jnp.where(diff[None] >= 0,
                      jnp.exp(jnp.maximum(diff, 0.0)[None] * log_g[:, None, None]), 0.0)
    inner = jnp.einsum('nbhcd,nbhsd->nbhcs', qc, kc) * decay[None, None]
    inner = jnp.einsum('nbhcs,nbhse->nbhce', inner, vc)
    zeta = jnp.exp((C - 1 - pos)[None, :] * log_g[:, None])
    contrib = jnp.einsum('nbhsd,hs,nbhse->nbhde', kc, zeta, vc)
    g_chunk = jnp.exp(C * log_g)[:, None, None]

    def step(s, ci):
        return g_chunk * s + ci, s

    s_final, s_prev = lax.scan(step, s0, contrib)
    xi = jnp.exp((pos + 1.0)[None, :] * log_g[:, None])
    cross = jnp.einsum('nbhcd,nbhde->nbhce', qc, s_prev) * xi[None, None, :, :, None]
    out = (inner + cross).transpose(1, 0, 3, 2, 4).reshape(B, L, H, d)
    return out, s_final


def retention_final_state(k, v, log_g):
    L = k.shape[1]
    w = jnp.exp((L - 1 - jnp.arange(L, dtype=jnp.float32))[:, None] * log_g[None, :])
    return jnp.einsum('blhd,lh,blhe->bhde', k.astype(jnp.float32), w, v.astype(jnp.float32))


def retention_readout(y, gate, dtype):
    B, L = y.shape[:2]
    y = rms_unit(y).reshape(B, L, RET_W)
    return (y * jax.nn.silu(gate.astype(jnp.float32))).astype(dtype)


def swiglu(x, wg, wu, wd):
    return (jax.nn.silu(x @ wg) * (x @ wu)) @ wd


def route(h, router_w, router_bias):
    T = h.shape[0]
    scores = jax.nn.sigmoid(jnp.dot(h.astype(jnp.float32), router_w.astype(jnp.float32)))
    grouped = (scores + router_bias.astype(jnp.float32)).reshape(T, N_GROUPS, N_EXPERTS // N_GROUPS)
    group_score = lax.top_k(grouped, 2)[0].sum(-1)
    _, top_groups = lax.top_k(group_score, TOPK_GROUPS)
    keep = jax.nn.one_hot(top_groups, N_GROUPS, dtype=jnp.float32).sum(1) > 0
    masked = jnp.where(keep[:, :, None], grouped, -jnp.inf).reshape(T, N_EXPERTS)
    _, idx = lax.top_k(masked, TOP_K)
    w = jnp.take_along_axis(scores, idx, axis=-1)
    w = w / jnp.sum(w, axis=-1, keepdims=True) * ROUTED_SCALE
    return idx, w


def routed_experts(h, idx, w, w_gate, w_up, w_down):
    T, D = h.shape
    A = T * TOP_K
    flat_e = idx.reshape(A)
    flat_tok = jnp.repeat(jnp.arange(T, dtype=jnp.int32), TOP_K)
    flat_w = w.reshape(A)
    order = jnp.argsort(flat_e)
    e_sorted = flat_e[order]
    counts = jnp.bincount(flat_e, length=N_EXPERTS)
    padded = (counts + MOE_BLOCK - 1) // MOE_BLOCK * MOE_BLOCK
    pad_end = jnp.cumsum(padded)
    pad_start = pad_end - padded
    start = jnp.cumsum(counts) - counts
    dest = pad_start[e_sorted] + jnp.arange(A) - start[e_sorted]
    n_blocks = -(-A // MOE_BLOCK) + N_EXPERTS
    n_pad = n_blocks * MOE_BLOCK
    slot_tok = jnp.full((n_pad,), T, jnp.int32).at[dest].set(flat_tok[order])
    slot_w = jnp.zeros((n_pad,), jnp.float32).at[dest].set(flat_w[order])
    block_e = jnp.minimum(jnp.searchsorted(pad_end, jnp.arange(n_blocks) * MOE_BLOCK, side='right'),
                          N_EXPERTS - 1)
    h_pad = jnp.concatenate([h, jnp.zeros((1, D), h.dtype)], axis=0)

    def one_block(args):
        tok, e = args
        return swiglu(h_pad[tok], w_gate[e], w_up[e], w_down[e])

    y = lax.map(one_block, (slot_tok.reshape(n_blocks, MOE_BLOCK), block_e)).reshape(n_pad, D)
    y = (y.astype(jnp.float32) * slot_w[:, None])
    return jax.ops.segment_sum(y, slot_tok, num_segments=T + 1)[:T].astype(h.dtype)


def channel_mixer(h, router_w, router_bias, eg, eu, ed, sg, su, sd):
    B, L, D = h.shape
    t = h.reshape(B * L, D)
    idx, w = route(t, router_w, router_bias)
    y = routed_experts(t, idx, w, eg, eu, ed) + swiglu(t, sg, su, sd)
    return y.reshape(B, L, D)


def setup_inputs(seed: int = 0) -> dict:
    key = jax.random.key(seed)
    ks = jax.random.split(key, 24)
    D = D_MODEL
    f32 = jnp.float32

    def nrm(k, shape, fan_in, s=1.0):
        return jax.random.normal(k, shape, f32) * (s * fan_in ** -0.5)

    base_decay = jnp.asarray(np.log(2.0 ** (5 + np.arange(RET_HEADS)) - 1.0).astype(np.float32))
    return {
        "x": jax.random.normal(ks[0], (BATCH, SEQ, D), f32),
        "c": jax.random.normal(ks[1], (BATCH, D), f32),
        "ctx": jax.random.normal(ks[2], (BATCH, CTX_LEN, D), f32),
        "c_ctx": jax.random.normal(ks[3], (D,), f32),
        "w_mod": nrm(ks[4], (DEPTH, D, 6 * D), D, 0.5),
        "b_mod": 0.02 * jax.random.normal(ks[5], (DEPTH, 6 * D), f32),
        "norm1_w": 1.0 + 0.02 * jax.random.normal(ks[6], (DEPTH, D), f32),
        "norm2_w": 1.0 + 0.02 * jax.random.normal(ks[7], (DEPTH, D), f32),
        "w_in": nrm(ks[8], (DEPTH, D, IN_COLS), D),
        "q_norm_w": 1.0 + 0.02 * jax.random.normal(ks[9], (DEPTH, HEAD_DIM), f32),
        "k_norm_w": 1.0 + 0.02 * jax.random.normal(ks[10], (DEPTH, HEAD_DIM), f32),
        "ret_decay_fwd": base_decay + 0.1 * jax.random.normal(ks[11], (DEPTH, RET_HEADS), f32),
        "ret_decay_bwd": base_decay + 0.1 * jax.random.normal(ks[12], (DEPTH, RET_HEADS), f32),
        "w_out": nrm(ks[13], (DEPTH, MIX_WIDTH, D), MIX_WIDTH),
        "router_w": nrm(ks[14], (DEPTH, D, N_EXPERTS), D),
        "router_bias": 0.01 * jax.random.normal(ks[15], (DEPTH, N_EXPERTS), f32),
        "exp_w_gate": nrm(ks[16], (DEPTH, N_EXPERTS, D, EXPERT_FF), D),
        "exp_w_up": nrm(ks[17], (DEPTH, N_EXPERTS, D, EXPERT_FF), D),
        "exp_w_down": nrm(ks[18], (DEPTH, N_EXPERTS, EXPERT_FF, D), EXPERT_FF),
        "sh_w_gate": nrm(ks[19], (DEPTH, D, SHARED_FF), D),
        "sh_w_up": nrm(ks[20], (DEPTH, D, SHARED_FF), D),
        "sh_w_down": nrm(ks[21], (DEPTH, SHARED_FF, D), SHARED_FF),
    }


def reference(x, c, ctx, c_ctx, w_mod, b_mod, norm1_w, norm2_w, w_in, q_norm_w, k_norm_w,
              ret_decay_fwd, ret_decay_bwd, w_out, router_w, router_bias,
              exp_w_gate, exp_w_up, exp_w_down, sh_w_gate, sh_w_up, sh_w_down):
    B, L, D = x.shape
    cos, sin = axial_rope_tables(L)
    for layer in range(DEPTH):
        last = layer == DEPTH - 1
        Wm, bm, Wi = w_mod[layer], b_mod[layer], w_in[layer]
        mod = (jax.nn.silu(c) @ Wm + bm)[:, None, :]
        sh1, sc1, g1, sh2, sc2, g2 = jnp.split(mod, 6, axis=-1)
        n_cm = 2 * D if last else 6 * D
        modc = jax.nn.silu(c_ctx) @ Wm[:, :n_cm] + bm[:n_cm]
        lg_f = jax.nn.log_sigmoid(ret_decay_fwd[layer].astype(jnp.float32))
        lg_b = jax.nn.log_sigmoid(ret_decay_bwd[layer].astype(jnp.float32))

        hc = modulate(rms_norm(ctx, norm1_w[layer]), modc[:D], modc[D:2 * D])
        zc = hc @ (Wi[:, :CTX_KV_COLS] if last else Wi)
        cak = rms_norm(heads(zc, OFF_AK, ATTN_KV_HEADS), k_norm_w[layer])
        cav = heads(zc, OFF_AV, ATTN_KV_HEADS)
        crk = heads(zc, OFF_RK, RET_HEADS) * HEAD_DIM ** -0.5
        crv = heads(zc, OFF_RV, RET_HEADS)
        if last:
            s_f = retention_final_state(crk, crv, lg_f)
            s_b = retention_final_state(flip(crk), flip(crv), lg_b)
        else:
            zero = jnp.zeros((B, RET_HEADS, HEAD_DIM, HEAD_DIM), jnp.float32)
            crq = heads(zc, OFF_RQ, RET_HEADS)
            yc_f, s_f = retention_chunkwise(crq, crk, crv, lg_f, zero)
            yc_b, s_b = retention_chunkwise(flip(crq), flip(crk), flip(crv), lg_b, zero)

        h = modulate(rms_norm(x, norm1_w[layer]), sh1, sc1)
        z = h @ Wi
        aq = apply_rope(rms_norm(heads(z, OFF_AQ, ATTN_HEADS), q_norm_w[layer]), cos, sin)
        ak = apply_rope(rms_norm(heads(z, OFF_AK, ATTN_KV_HEADS), k_norm_w[layer]), cos, sin)
        av = heads(z, OFF_AV, ATTN_KV_HEADS)
        attn = latent_attention(aq, ak, av, cak, cav)
        rq = heads(z, OFF_RQ, RET_HEADS)
        rk = heads(z, OFF_RK, RET_HEADS) * HEAD_DIM ** -0.5
        rv = heads(z, OFF_RV, RET_HEADS)
        y_f, _ = retention_chunkwise(rq, rk, rv, lg_f, s_f)
        y_b, _ = retention_chunkwise(flip(rq), flip(rk), flip(rv), lg_b, s_b)
        ret = retention_readout(y_f + flip(y_b), z[..., OFF_RG:OFF_RG + RET_W], x.dtype)
        x = x + g1 * (jnp.concatenate([attn, ret], axis=-1) @ w_out[layer])
        h2 = modulate(rms_norm(x, norm2_w[layer]), sh2, sc2)
        x = x + g2 * channel_mixer(h2, router_w[layer], router_bias[layer], exp_w_gate[layer],
                                   exp_w_up[layer], exp_w_down[layer], sh_w_gate[layer],
                                   sh_w_up[layer], sh_w_down[layer])

        if not last:
            caq = rms_norm(heads(zc, OFF_AQ, ATTN_HEADS), q_norm_w[layer])
            attn_c = context_attention(caq, cak, cav)
            ret_c = retention_readout(yc_f + flip(yc_b), zc[..., OFF_RG:OFF_RG + RET_W], ctx.dtype)
            ctx = ctx + modc[2 * D:3 * D] * (jnp.concatenate([attn_c, ret_c], axis=-1) @ w_out[layer])
            hc2 = modulate(rms_norm(ctx, norm2_w[layer]), modc[3 * D:4 * D], modc[4 * D:5 * D])
            ctx = ctx + modc[5 * D:] * channel_mixer(hc2, router_w[layer], router_bias[layer],
                                                     exp_w_gate[layer], exp_w_up[layer],
                                                     exp_w_down[layer], sh_w_gate[layer],
                                                     sh_w_up[layer], sh_w_down[layer])
    return x
```

```python
import functools

import jax
import jax.numpy as jnp
from jax import lax
from jax.experimental import pallas as pl
from jax.experimental.pallas import tpu as pltpu

F32 = jnp.float32
BF16 = jnp.bfloat16
I32 = jnp.int32
U32 = jnp.uint32

HEAD_DIM = 64
LANES = 128
SUBLANES = 8
ATTN_HEADS = 8
ATTN_KV_HEADS = 2
GQA = ATTN_HEADS // ATTN_KV_HEADS
RET_HEADS = 8
ATTN_W = ATTN_HEADS * HEAD_DIM
KV_W = ATTN_KV_HEADS * HEAD_DIM
RET_W = RET_HEADS * HEAD_DIM
RET_PAIRS = RET_W // LANES
CHUNK = 128
GRID_W = 64
ROPE_THETA = 10000.0
N_EXPERTS = 256
TOP_K = 8
N_GROUPS = 8
GROUP_SIZE = N_EXPERTS // N_GROUPS
TOPK_GROUPS = 4
ROUTED_SCALE = 2.5
MOE_BLOCK = 128
EPS = 1e-6
QK_SCALE = HEAD_DIM ** -0.5

OFF_AK = 0
OFF_AV = OFF_AK + KV_W
OFF_RK = OFF_AV + KV_W
OFF_RV = OFF_RK + RET_W
CTX_KV_COLS = OFF_RV + RET_W
OFF_AQ = CTX_KV_COLS
OFF_RQ = OFF_AQ + ATTN_W
OFF_RG = OFF_RQ + RET_W
IN_COLS = OFF_RG + RET_W

VMEM_LIMIT = 52 * 1024 * 1024
HI_MASK = 0xFFFF0000


def _cparams(*sem):
    return pltpu.CompilerParams(dimension_semantics=sem, vmem_limit_bytes=VMEM_LIMIT)


def _split(a):
    hi = a.astype(BF16)
    lo = (a - hi.astype(F32)).astype(BF16)
    return hi, lo


def _dot(a, b):
    return jnp.dot(a, b, preferred_element_type=F32)


def _dot_nt(a, b):
    return lax.dot_general(a, b, (((1,), (1,)), ((), ())), preferred_element_type=F32)


def _sigmoid(v):
    return 1.0 / (1.0 + jnp.exp(-v))


def _silu(v):
    return v * _sigmoid(v)


def _pack_halves(a, b):
    ua = lax.bitcast_convert_type(a.astype(BF16).astype(F32), U32)
    ub = lax.bitcast_convert_type(b.astype(BF16).astype(F32), U32)
    return (ua & jnp.uint32(HI_MASK)) | (ub >> 16)


def _unpack_halves(u):
    a = lax.bitcast_convert_type(u & jnp.uint32(HI_MASK), F32)
    b = lax.bitcast_convert_type(u << 16, F32)
    return a, b


def _mod_kernel(c_ref, w_ref, b_ref, o_ref):
    s_hi, s_lo = _split(_silu(c_ref[...]))
    w_hi, w_lo = _split(w_ref[...])
    o_ref[...] = _dot(s_hi, w_hi) + _dot(s_hi, w_lo) + _dot(s_lo, w_hi) + b_ref[...]


def _modulation(cc, w_mod, b_mod):
    rows, d = cc.shape
    n = w_mod.shape[1]
    tn = 768
    return pl.pallas_call(
        _mod_kernel,
        grid=(n // tn,),
        in_specs=[pl.BlockSpec((rows, d), lambda j: (0, 0)),
                  pl.BlockSpec((d, tn), lambda j: (0, j)),
                  pl.BlockSpec((1, tn), lambda j: (0, j))],
        out_specs=pl.BlockSpec((rows, tn), lambda j: (0, j)),
        out_shape=jax.ShapeDtypeStruct((rows, n), F32),
        compiler_params=_cparams("arbitrary"),
        name="mod",
    )(cc, w_mod, b_mod.reshape(1, n))


def _segment_ones():
    r = lax.broadcasted_iota(I32, (LANES, LANES), 0) // HEAD_DIM
    c = lax.broadcasted_iota(I32, (LANES, LANES), 1) // HEAD_DIM
    return jnp.where(r == c, 1.0, 0.0).astype(BF16)


def _head_mean_sq(v, seg):
    hi, lo = _split(v * v)
    return (_dot(hi, seg) + _dot(lo, seg)) * (1.0 / HEAD_DIM)


def _proj_kernel(x_ref, sh_ref, sc_ref, nw_ref, wi_ref, qnw_ref, knw_ref, cos_ref, sin_ref,
                 *out_refs, rope, with_q):
    if with_q:
        klo_ref, khi_ref, vlo_ref, vhi_ref, rk_ref, rv_ref, q_ref, rq_ref, sg_ref = out_refs
    else:
        klo_ref, khi_ref, vlo_ref, vhi_ref, rk_ref, rv_ref = out_refs
    x = x_ref[0]
    h = x * lax.rsqrt(jnp.mean(x * x, axis=-1, keepdims=True) + EPS) * nw_ref[...]
    h = h * (1.0 + sc_ref[0]) + sh_ref[0]
    z = _dot(h.astype(BF16), wi_ref[...])

    seg = _segment_ones()
    lane = lax.broadcasted_iota(I32, (x.shape[0], LANES), 1)
    low_half = lane < HEAD_DIM
    even = (lane & 1) == 0

    def norm_rope(v, w128):
        v = v * lax.rsqrt(_head_mean_sq(v, seg) + EPS) * w128
        if rope:
            swapped = jnp.where(even, pltpu.roll(v, LANES - 1, 1), pltpu.roll(v, 1, 1))
            v = v * cos_ref[...] + swapped * sin_ref[...]
        return v

    def write_lo_hi(v, lo_ref, hi_ref):
        sw = pltpu.roll(v, HEAD_DIM, 1)
        lo_ref[0, 0] = jnp.where(low_half, v, 0.0).astype(BF16)
        hi_ref[0, 0] = jnp.where(low_half, 0.0, sw).astype(BF16)
        lo_ref[0, 1] = jnp.where(low_half, sw, 0.0).astype(BF16)
        hi_ref[0, 1] = jnp.where(low_half, 0.0, v).astype(BF16)

    write_lo_hi(norm_rope(z[:, OFF_AK:OFF_AK + KV_W], knw_ref[...]), klo_ref, khi_ref)
    write_lo_hi(z[:, OFF_AV:OFF_AV + KV_W], vlo_ref, vhi_ref)
    rk_ref[0] = (z[:, OFF_RK:OFF_RK + RET_W] * QK_SCALE).astype(BF16)
    rv_ref[0] = z[:, OFF_RV:OFF_RV + RET_W].astype(BF16)
    if with_q:
        for j in range(ATTN_W // LANES):
            qj = norm_rope(z[:, OFF_AQ + j * LANES:OFF_AQ + (j + 1) * LANES], qnw_ref[...])
            q_ref[0, :, j * LANES:(j + 1) * LANES] = (qj * QK_SCALE).astype(BF16)
        rq_ref[0] = z[:, OFF_RQ:OFF_RQ + RET_W].astype(BF16)
        sg_ref[0] = _silu(z[:, OFF_RG:OFF_RG + RET_W]).astype(BF16)


def _projection(x, shift, scale, norm_w, wi_bf16, qnw, knw, cos, sin, *, rope, with_q, tm):
    b, l, d = x.shape
    tm = min(tm, l)
    ncols = IN_COLS if with_q else CTX_KV_COLS
    per_batch = shift.shape[0] > 1
    mod_idx = (lambda bi, i: (bi, 0, 0)) if per_batch else (lambda bi, i: (0, 0, 0))
    kv_shape = jax.ShapeDtypeStruct((b, ATTN_KV_HEADS, l, LANES), BF16)
    kv_spec = pl.BlockSpec((1, ATTN_KV_HEADS, tm, LANES), lambda bi, i: (bi, 0, i, 0))
    w_shape = jax.ShapeDtypeStruct((b, l, RET_W), BF16)
    w_spec = pl.BlockSpec((1, tm, RET_W), lambda bi, i: (bi, i, 0))
    out_shape = [kv_shape] * 4 + [w_shape] * 2
    out_specs = [kv_spec] * 4 + [w_spec] * 2
    if with_q:
        out_shape += [w_shape] * 3
        out_specs += [w_spec] * 3
    return pl.pallas_call(
        functools.partial(_proj_kernel, rope=rope, with_q=with_q),
        grid=(b, l // tm),
        in_specs=[pl.BlockSpec((1, tm, d), lambda bi, i: (bi, i, 0)),
                  pl.BlockSpec((1, 1, d), mod_idx),
                  pl.BlockSpec((1, 1, d), mod_idx),
                  pl.BlockSpec((1, d), lambda bi, i: (0, 0)),
                  pl.BlockSpec((d, ncols), lambda bi, i: (0, 0)),
                  pl.BlockSpec((1, LANES), lambda bi, i: (0, 0)),
                  pl.BlockSpec((1, LANES), lambda bi, i: (0, 0)),
                  pl.BlockSpec((tm, LANES), lambda bi, i: (i, 0)),
                  pl.BlockSpec((tm, LANES), lambda bi, i: (i, 0))],
        out_specs=out_specs,
        out_shape=out_shape,
        compiler_params=_cparams("arbitrary", "arbitrary"),
        name="proj_latent" if with_q else "proj_ctx",
    )(x, shift, scale, norm_w, wi_bf16, qnw, knw, cos, sin)


def _attn_kernel(q_ref, klo_ref, khi_ref, vlo_ref, vhi_ref, cklo_ref, ckhi_ref, cvlo_ref, cvhi_ref,
                 o_ref, kl_s, kh_s, va_s, *, l, lc):
    lk = l + lc

    @pl.when(pl.program_id(2) == 0)
    def _():
        kl_s[0:l] = klo_ref[0, 0]
        kl_s[l:lk] = cklo_ref[0, 0]
        kh_s[0:l] = khi_ref[0, 0]
        kh_s[l:lk] = ckhi_ref[0, 0]
        lane = lax.broadcasted_iota(I32, (lk, LANES), 1)
        ones_lo = jnp.where(lane < HEAD_DIM, 1.0, 0.0).astype(BF16)
        ones_hi = jnp.where(lane < HEAD_DIM, 0.0, 1.0).astype(BF16)
        for g, (v_ref, cv_ref, ones) in enumerate(((vlo_ref, cvlo_ref, ones_lo), (vhi_ref, cvhi_ref, ones_hi),
                                                   (vlo_ref, cvlo_ref, ones_lo), (vhi_ref, cvhi_ref, ones_hi))):
            v_col, one_col = (0, LANES) if g < 2 else (LANES, 0)
            va_s[g, 0:l, v_col:v_col + LANES] = v_ref[0, 0]
            va_s[g, l:lk, v_col:v_col + LANES] = cv_ref[0, 0]
            va_s[g, :, one_col:one_col + LANES] = ones

    q = q_ref[0]
    acc = []
    for g in range(GQA):
        qp = q[:, (g // 2) * LANES:(g // 2 + 1) * LANES]
        kk = kl_s[...] if g % 2 == 0 else kh_s[...]
        s = _dot_nt(qp, kk)
        p = jnp.exp(s - jnp.max(s, axis=-1, keepdims=True)).astype(BF16)
        acc.append(_dot(p, va_s[g]))
    out_a = acc[0] + acc[1]
    out_b = acc[2] + acc[3]
    o_ref[0, :, 0:LANES] = (out_a[:, 0:LANES] / out_a[:, LANES:2 * LANES]).astype(BF16)
    o_ref[0, :, LANES:2 * LANES] = (out_b[:, LANES:2 * LANES] / out_b[:, 0:LANES]).astype(BF16)


def _attention(q, klo, khi, vlo, vhi, cklo, ckhi, cvlo, cvhi, *, tq):
    b, l, _ = q.shape
    lc = cklo.shape[2]
    lk = l + lc
    tq = min(tq, l)
    gw = GQA * HEAD_DIM
    kv_spec = pl.BlockSpec((1, 1, l, LANES), lambda bi, h, i: (bi, h, 0, 0))
    ckv_spec = pl.BlockSpec((1, 1, lc, LANES), lambda bi, h, i: (bi, h, 0, 0))
    return pl.pallas_call(
        functools.partial(_attn_kernel, l=l, lc=lc),
        grid=(b, ATTN_KV_HEADS, l // tq),
        in_specs=[pl.BlockSpec((1, tq, gw), lambda bi, h, i: (bi, i, h))] + [kv_spec] * 4 + [ckv_spec] * 4,
        out_specs=pl.BlockSpec((1, tq, gw), lambda bi, h, i: (bi, i, h)),
        out_shape=jax.ShapeDtypeStruct((b, l, ATTN_W), BF16),
        scratch_shapes=[pltpu.VMEM((lk, LANES), BF16), pltpu.VMEM((lk, LANES), BF16),
                        pltpu.VMEM((GQA, lk, 2 * LANES), BF16)],
        compiler_params=_cparams("arbitrary", "arbitrary", "arbitrary"),
        name="attn",
    )(q, klo, khi, vlo, vhi, cklo, ckhi, cvlo, cvhi)


def _log_sigmoid(v):
    return jnp.minimum(v, 0.0) - jnp.log(1.0 + jnp.exp(-jnp.abs(v)))


def _ret_kernel(rq_ref, rk_ref, rv_ref, sg_ref, crk_ref, crv_ref, df_ref, db_ref, o_ref,
                m_s, tab_s, sb_s, sf_s, *, l, lc):
    n = l // CHUNK
    nc = lc // CHUNK
    lgf = _log_sigmoid(df_ref[...])
    lgb = _log_sigmoid(db_ref[...])
    pos = lax.broadcasted_iota(I32, (CHUNK, RET_W), 0).astype(F32)
    tab_s[0] = jnp.exp((pos + 1.0) * lgf)
    tab_s[1] = jnp.exp((CHUNK - pos) * lgb)
    tab_s[2] = jnp.exp((CHUNK - 1.0 - pos) * lgf)
    tab_s[3] = jnp.exp(pos * lgb)
    gf_c = jnp.exp(CHUNK * lgf)
    gb_c = jnp.exp(CHUNK * lgb)
    row = lax.broadcasted_iota(I32, (CHUNK, CHUNK), 0)
    col = lax.broadcasted_iota(I32, (CHUNK, CHUNK), 1)
    diff = (row - col).astype(F32)
    for h in range(RET_HEADS):
        lf = lgf[:, h * HEAD_DIM:h * HEAD_DIM + 1]
        lb = lgb[:, h * HEAD_DIM:h * HEAD_DIM + 1]
        m_s[h] = jnp.where(diff > 0, jnp.exp(diff * lf), jnp.where(diff < 0, jnp.exp(-diff * lb), 2.0))

    lane = lax.broadcasted_iota(I32, (CHUNK, LANES), 1)
    low_half = lane < HEAD_DIM
    diag = (lax.broadcasted_iota(I32, (LANES, LANES), 0) // HEAD_DIM
            == lax.broadcasted_iota(I32, (LANES, LANES), 1) // HEAD_DIM)
    seg = jnp.where(diag, 1.0, 0.0).astype(BF16)

    def contrib(k_ref, v_ref, r0, p, zeta_idx):
        cols = slice(p * LANES, (p + 1) * LANES)
        kz = k_ref[0, pl.ds(r0, CHUNK), cols].astype(F32) * tab_s[zeta_idx, :, cols]
        kv = _dot(jnp.transpose(kz).astype(BF16), v_ref[0, pl.ds(r0, CHUNK), cols])
        return jnp.where(diag, kv, 0.0)

    for p in range(RET_PAIRS):
        cols = slice(p * LANES, (p + 1) * LANES)
        sf = jnp.zeros((LANES, LANES), F32)
        sb = jnp.zeros((LANES, LANES), F32)
        for c in range(nc):
            sf = gf_c[:, cols] * sf + contrib(crk_ref, crv_ref, c * CHUNK, p, 2)
            cb = nc - 1 - c
            sb = gb_c[:, cols] * sb + contrib(crk_ref, crv_ref, cb * CHUNK, p, 3)
        sf_s[p] = sf
        sb_s[n, p] = sb

    def bwd_body(j, carry):
        c = n - 1 - j
        r0 = pl.multiple_of(c * CHUNK, CHUNK)
        for p in range(RET_PAIRS):
            cols = slice(p * LANES, (p + 1) * LANES)
            prev = sb_s[c + 1, p]
            sb_s[c, p] = gb_c[:, cols] * prev + contrib(rk_ref, rv_ref, r0, p, 3)
        return carry

    lax.fori_loop(0, n, bwd_body, 0)

    def fwd_body(c, carry):
        r0 = pl.multiple_of(c * CHUNK, CHUNK)
        for p in range(RET_PAIRS):
            cols = slice(p * LANES, (p + 1) * LANES)
            qp = rq_ref[0, pl.ds(r0, CHUNK), cols]
            kp = rk_ref[0, pl.ds(r0, CHUNK), cols]
            vp = rv_ref[0, pl.ds(r0, CHUNK), cols]
            zero = jnp.zeros_like(kp)
            s0 = _dot_nt(qp, jnp.where(low_half, kp, zero))
            s1 = _dot_nt(qp, jnp.where(low_half, zero, kp))
            a0 = (s0 * m_s[2 * p]).astype(BF16)
            a1 = (s1 * m_s[2 * p + 1]).astype(BF16)
            y = _dot(a0, jnp.where(low_half, vp, zero)) + _dot(a1, jnp.where(low_half, zero, vp))
            qf = qp.astype(F32)
            sf = sf_s[p]
            y += _dot((qf * tab_s[0, :, cols]).astype(BF16), sf.astype(BF16))
            y += _dot((qf * tab_s[1, :, cols]).astype(BF16), sb_s[c + 1, p].astype(BF16))
            hi, lo = _split(y * y)
            ms = (_dot(hi, seg) + _dot(lo, seg)) * (1.0 / HEAD_DIM)
            out = y * lax.rsqrt(ms + EPS) * sg_ref[0, pl.ds(r0, CHUNK), cols].astype(F32)
            o_ref[0, pl.ds(r0, CHUNK), cols] = out.astype(BF16)
            sf_s[p] = gf_c[:, cols] * sf + contrib(rk_ref, rv_ref, r0, p, 2)
        return carry

    lax.fori_loop(0, n, fwd_body, 0)


def _retention(rq, rk, rv, sg, crk, crv, dec_f, dec_b):
    b, l, _ = rq.shape
    lc = crk.shape[1]
    n = l // CHUNK
    spec = pl.BlockSpec((1, l, RET_W), lambda bi: (bi, 0, 0))
    cspec = pl.BlockSpec((1, lc, RET_W), lambda bi: (bi, 0, 0))
    dspec = pl.BlockSpec((1, RET_W), lambda bi: (0, 0))
    return pl.pallas_call(
        functools.partial(_ret_kernel, l=l, lc=lc),
        grid=(b,),
        in_specs=[spec, spec, spec, spec, cspec, cspec, dspec, dspec],
        out_specs=spec,
        out_shape=jax.ShapeDtypeStruct((b, l, RET_W), BF16),
        scratch_shapes=[pltpu.VMEM((RET_HEADS, CHUNK, CHUNK), F32),
                        pltpu.VMEM((4, CHUNK, RET_W), F32),
                        pltpu.VMEM((n + 1, RET_PAIRS, LANES, LANES), F32),
                        pltpu.VMEM((RET_PAIRS, LANES, LANES), F32)],
        compiler_params=_cparams("arbitrary"),
        name="ret",
    )(rq, rk, rv, sg, crk, crv, dec_f, dec_b)


def _out_kernel(attn_ref, ret_ref, x_ref, wa_ref, wr_ref, g1_ref, sh_ref, sc_ref, nw_ref, rhi_ref, rlo_ref,
                x1_ref, hp_ref, lg_ref):
    y = _dot(attn_ref[0], wa_ref[...]) + _dot(ret_ref[0], wr_ref[...])
    x1 = x_ref[0] + g1_ref[0] * y
    x1_ref[0] = x1
    h = x1 * lax.rsqrt(jnp.mean(x1 * x1, axis=-1, keepdims=True) + EPS) * nw_ref[...]
    h = h * (1.0 + sc_ref[0]) + sh_ref[0]
    half = h.shape[1] // 2
    hp_ref[...] = _pack_halves(h[:, :half], h[:, half:])
    h_hi, h_lo = _split(h)
    lg_ref[...] = _dot_nt(rhi_ref[...], h_hi) + _dot_nt(rhi_ref[...], h_lo) + _dot_nt(rlo_ref[...], h_hi)


def _out_projection(attn, ret, x, wa, wr, g1, sh2, sc2, norm_w, r_hi, r_lo, *, tm):
    b, l, d = x.shape
    tm = min(tm, l)
    nt = l // tm
    t = b * l
    mspec = pl.BlockSpec((1, 1, d), lambda bi, i: (bi, 0, 0))
    return pl.pallas_call(
        _out_kernel,
        grid=(b, nt),
        in_specs=[pl.BlockSpec((1, tm, ATTN_W), lambda bi, i: (bi, i, 0)),
                  pl.BlockSpec((1, tm, RET_W), lambda bi, i: (bi, i, 0)),
                  pl.BlockSpec((1, tm, d), lambda bi, i: (bi, i, 0)),
                  pl.BlockSpec((ATTN_W, d), lambda bi, i: (0, 0)),
                  pl.BlockSpec((RET_W, d), lambda bi, i: (0, 0)),
                  mspec, mspec, mspec,
                  pl.BlockSpec((1, d), lambda bi, i: (0, 0)),
                  pl.BlockSpec((N_EXPERTS, d), lambda bi, i: (0, 0)),
                  pl.BlockSpec((N_EXPERTS, d), lambda bi, i: (0, 0))],
        out_specs=[pl.BlockSpec((1, tm, d), lambda bi, i: (bi, i, 0)),
                   pl.BlockSpec((tm, d // 2), lambda bi, i: (bi * nt + i, 0)),
                   pl.BlockSpec((N_EXPERTS, tm), lambda bi, i: (0, bi * nt + i))],
        out_shape=[jax.ShapeDtypeStruct((b, l, d), F32),
                   jax.ShapeDtypeStruct((t, d // 2), U32),
                   jax.ShapeDtypeStruct((N_EXPERTS, t), F32)],
        compiler_params=_cparams("arbitrary", "arbitrary"),
        name="out_proj",
    )(attn, ret, x, wa, wr, g1, sh2, sc2, norm_w, r_hi, r_lo)


def _route_kernel(lg_ref, bias_ref, idx_ref, w_ref, rank_ref, cnt_col_ref, cnt_row_ref, tri_s, col_s, row_s):
    tb = lg_ref.shape[1]
    step = pl.program_id(0)

    @pl.when(step == 0)
    def _():
        r = lax.broadcasted_iota(I32, (tb, tb), 0)
        c = lax.broadcasted_iota(I32, (tb, tb), 1)
        tri_s[...] = jnp.where(r <= c, 1.0, 0.0).astype(BF16)
        col_s[...] = jnp.zeros_like(col_s)
        row_s[...] = jnp.zeros_like(row_s)

    scores = _sigmoid(lg_ref[...])
    biased = scores + bias_ref[...]
    neg = -jnp.inf
    sub = lax.broadcasted_iota(I32, (GROUP_SIZE, tb), 0).astype(F32)

    gscore = []
    for g in range(N_GROUPS):
        blk = biased[g * GROUP_SIZE:(g + 1) * GROUP_SIZE]
        m1 = jnp.max(blk, axis=0, keepdims=True)
        first = jnp.min(jnp.where(blk == m1, sub, float(GROUP_SIZE)), axis=0, keepdims=True)
        m2 = jnp.max(jnp.where(sub == first, neg, blk), axis=0, keepdims=True)
        gscore.append(m1 + m2)
    gs = jnp.concatenate(gscore, axis=0)
    gsub = lax.broadcasted_iota(I32, (N_GROUPS, tb), 0).astype(F32)
    keep = jnp.zeros((N_GROUPS, tb), F32)
    for _ in range(TOPK_GROUPS):
        m = jnp.max(gs, axis=0, keepdims=True)
        first = jnp.min(jnp.where(gs == m, gsub, float(N_GROUPS)), axis=0, keepdims=True)
        sel = gsub == first
        keep = jnp.where(sel, 1.0, keep)
        gs = jnp.where(sel, neg, gs)
    masked = jnp.concatenate(
        [jnp.where(keep[g:g + 1] > 0.0, biased[g * GROUP_SIZE:(g + 1) * GROUP_SIZE], neg)
         for g in range(N_GROUPS)], axis=0)

    esub = lax.broadcasted_iota(I32, (N_EXPERTS, tb), 0).astype(F32)
    sels, idxs, ws = [], [], []
    chosen = jnp.zeros((N_EXPERTS, tb), F32)
    for _ in range(TOP_K):
        m = jnp.max(masked, axis=0, keepdims=True)
        first = jnp.min(jnp.where(masked == m, esub, float(N_EXPERTS)), axis=0, keepdims=True)
        sel = esub == first
        sels.append(sel)
        idxs.append(first)
        ws.append(jnp.sum(jnp.where(sel, scores, 0.0), axis=0, keepdims=True))
        chosen = jnp.where(sel, 1.0, chosen)
        masked = jnp.where(sel, neg, masked)
    wsum = ws[0]
    for k in range(1, TOP_K):
        wsum = wsum + ws[k]
    idx_ref[...] = jnp.concatenate(idxs, axis=0).astype(I32)
    w_ref[...] = jnp.concatenate([wk / wsum * ROUTED_SCALE for wk in ws], axis=0)

    chosen_b = chosen.astype(BF16)
    incl = _dot(chosen_b, tri_s[...])
    before = incl - chosen + col_s[...]
    rank_ref[...] = jnp.concatenate(
        [jnp.sum(jnp.where(sel, before, 0.0), axis=0, keepdims=True) for sel in sels], axis=0).astype(I32)
    col_s[...] = col_s[...] + incl[:, tb - 1:tb]
    row_s[...] = row_s[...] + _dot_nt(jnp.ones((8, tb), BF16), chosen_b)
    cnt_col_ref[...] = col_s[...].astype(I32)
    cnt_row_ref[...] = row_s[...].astype(I32)


def _route(logits_t, bias_col, *, tb):
    e, t = logits_t.shape
    tb = min(tb, t)
    kspec = pl.BlockSpec((TOP_K, tb), lambda i: (0, i))
    return pl.pallas_call(
        _route_kernel,
        grid=(t // tb,),
        in_specs=[pl.BlockSpec((e, tb), lambda i: (0, i)),
                  pl.BlockSpec((e, 1), lambda i: (0, 0))],
        out_specs=[kspec, kspec, kspec,
                   pl.BlockSpec((e, 1), lambda i: (0, 0)),
                   pl.BlockSpec((8, e), lambda i: (0, 0))],
        out_shape=[jax.ShapeDtypeStruct((TOP_K, t), I32),
                   jax.ShapeDtypeStruct((TOP_K, t), F32),
                   jax.ShapeDtypeStruct((TOP_K, t), I32),
                   jax.ShapeDtypeStruct((e, 1), I32),
                   jax.ShapeDtypeStruct((8, e), I32)],
        scratch_shapes=[pltpu.VMEM((tb, tb), BF16), pltpu.VMEM((e, 1), F32), pltpu.VMEM((8, e), F32)],
        compiler_params=_cparams("arbitrary"),
        name="route",
    )(logits_t, bias_col)


def _pad_block(cnt):
    return (cnt + (MOE_BLOCK - 1)) // MOE_BLOCK * MOE_BLOCK


def _dest_kernel(idx_ref, rank_ref, cnt_col_ref, cnt_row_ref, dest_ref, meta_ref, blk_ref, *, n_blocks):
    tb = idx_ref.shape[1]
    nb = blk_ref.shape[1]
    pad_col = _pad_block(cnt_col_ref[...])
    pad_row = _pad_block(cnt_row_ref[0:1, :])
    er = lax.broadcasted_iota(I32, (N_EXPERTS, N_EXPERTS), 0)
    ec = lax.broadcasted_iota(I32, (N_EXPERTS, N_EXPERTS), 1)
    start_col = jnp.sum(jnp.where(ec < er, pad_row, 0), axis=1, keepdims=True)
    start_row = jnp.sum(jnp.where(er < ec, pad_col, 0), axis=0, keepdims=True)
    end_col = start_col + pad_col

    esub = lax.broadcasted_iota(I32, (N_EXPERTS, tb), 0)
    rows = []
    for k in range(TOP_K):
        onehot = esub == idx_ref[k:k + 1, :]
        rows.append(jnp.sum(jnp.where(onehot, start_col, 0), axis=0, keepdims=True) + rank_ref[k:k + 1, :])
    dest_ref[0] = jnp.concatenate(rows, axis=0)

    meta_ref[...] = jnp.concatenate(
        [cnt_row_ref[0:1, :], start_row, pad_row, jnp.zeros((5, N_EXPERTS), I32)], axis=0)
    first_row = lax.broadcasted_iota(I32, (N_EXPERTS, nb), 1) * MOE_BLOCK
    owner = jnp.sum(jnp.where(end_col <= first_row, 1, 0), axis=0, keepdims=True)
    used = jnp.sum(pad_row, axis=1, keepdims=True) // MOE_BLOCK
    lane = lax.broadcasted_iota(I32, (1, nb), 1)
    blk_ref[...] = jnp.where(lane == n_blocks, used, jnp.minimum(owner, N_EXPERTS - 1))


def _destinations(idx_t, rank_t, cnt_col, cnt_row, *, tb, n_blocks):
    _, t = idx_t.shape
    tb = min(tb, t)
    nbp = (n_blocks + 1 + LANES - 1) // LANES * LANES
    kspec = pl.BlockSpec((TOP_K, tb), lambda i: (0, i))
    return pl.pallas_call(
        functools.partial(_dest_kernel, n_blocks=n_blocks),
        grid=(t // tb,),
        in_specs=[kspec, kspec,
                  pl.BlockSpec((N_EXPERTS, 1), lambda i: (0, 0)),
                  pl.BlockSpec((8, N_EXPERTS), lambda i: (0, 0))],
        out_specs=[pl.BlockSpec((1, TOP_K, tb), lambda i: (i, 0, 0)),
                   pl.BlockSpec((8, N_EXPERTS), lambda i: (0, 0)),
                   pl.BlockSpec((1, nbp), lambda i: (0, 0))],
        out_shape=[jax.ShapeDtypeStruct((t // tb, TOP_K, tb), I32),
                   jax.ShapeDtypeStruct((8, N_EXPERTS), I32),
                   jax.ShapeDtypeStruct((1, nbp), I32)],
        compiler_params=_cparams("arbitrary"),
        name="dest",
    )(idx_t, rank_t, cnt_col, cnt_row)


_PAD_BITS = (64, 32, 16, 8)


def _dispatch_kernel(meta_ref, hp_ref, dest_hbm, xs_hbm, dest_s, zero_s, sem_d, sem_r, sem_z, *, tb, e_per_step):
    step = pl.program_id(0)
    n_assign = TOP_K * tb
    cp = pltpu.make_async_copy(dest_hbm.at[pl.ds(step * n_assign, n_assign)], dest_s, sem_d)
    cp.start()
    zero_s[...] = jnp.zeros_like(zero_s)
    cp.wait()

    def row_copy(t, k):
        d = dest_s[k * tb + t]
        return pltpu.make_async_copy(hp_ref.at[pl.ds(t, 1)], xs_hbm.at[pl.ds(d, 1)], sem_r)

    def issue(t, carry):
        for k in range(TOP_K):
            row_copy(t, k).start()
        return carry

    lax.fori_loop(0, tb, issue, 0)

    def pad_copies(e):
        cnt = meta_ref[0, e]
        off = meta_ref[1, e] + cnt
        rem = meta_ref[2, e] - cnt
        head = rem & (SUBLANES - 1)
        out = []
        for i in range(SUBLANES - 1):
            out.append((i < head,
                        pltpu.make_async_copy(zero_s.at[pl.ds(0, 1)], xs_hbm.at[pl.ds(off + i, 1)], sem_z)))
        off = off + head
        for bit in _PAD_BITS:
            out.append(((rem & bit) != 0,
                        pltpu.make_async_copy(zero_s.at[pl.ds(0, bit)],
                                              xs_hbm.at[pl.ds(pl.multiple_of(off, SUBLANES), bit)], sem_z)))
            off = off + (rem & bit)
        return out

    for j in range(e_per_step):
        for cond, c in pad_copies(step * e_per_step + j):
            pl.when(cond)(c.start)
    for j in range(e_per_step):
        for cond, c in pad_copies(step * e_per_step + j):
            pl.when(cond)(c.wait)

    def drain(t, carry):
        for k in range(TOP_K):
            row_copy(t, k).wait()
        return carry

    lax.fori_loop(0, tb, drain, 0)


def _dispatch(meta, hp, dest_flat, *, tb, n_blocks):
    t, half = hp.shape
    tb = min(tb, t)
    steps = t // tb
    e_per_step = max(N_EXPERTS // steps, 1)
    assert e_per_step * steps == N_EXPERTS, "token blocks must divide the expert count"
    return pl.pallas_call(
        functools.partial(_dispatch_kernel, tb=tb, e_per_step=e_per_step),
        grid=(steps,),
        in_specs=[pl.BlockSpec(memory_space=pltpu.SMEM),
                  pl.BlockSpec((tb, half), lambda i: (i, 0)),
                  pl.BlockSpec(memory_space=pl.ANY)],
        out_specs=pl.BlockSpec(memory_space=pl.ANY),
        out_shape=jax.ShapeDtypeStruct((n_blocks * MOE_BLOCK, half), U32),
        scratch_shapes=[pltpu.SMEM((TOP_K * tb,), I32), pltpu.VMEM((MOE_BLOCK, half), U32),
                        pltpu.SemaphoreType.DMA, pltpu.SemaphoreType.DMA, pltpu.SemaphoreType.DMA],
        compiler_params=_cparams("arbitrary"),
        name="dispatch",
    )(meta, hp, dest_flat)


def _experts_kernel(blk_ref, xs_ref, wg_ref, wu_ref, wd_ref, ys_ref, *, n_blocks):
    j = pl.program_id(0)
    used = blk_ref[n_blocks]

    @pl.when(j < used)
    def _():
        xa, xb = _unpack_halves(xs_ref[...])
        xa = xa.astype(BF16)
        xb = xb.astype(BF16)
        half = xa.shape[1]
        wg = wg_ref[0].astype(BF16)
        wu = wu_ref[0].astype(BF16)
        g = _dot(xa, wg[:half]) + _dot(xb, wg[half:])
        u = _dot(xa, wu[:half]) + _dot(xb, wu[half:])
        y = _dot((_silu(g) * u).astype(BF16), wd_ref[0].astype(BF16))
        ys_ref[...] = _pack_halves(y[:, :half], y[:, half:])

    @pl.when(j >= used)
    def _():
        ys_ref[...] = jnp.zeros_like(ys_ref)


def _experts(blk, xs, w_gate, w_up, w_down, *, n_blocks):
    half = xs.shape[1]
    e, d, f = w_gate.shape

    def x_idx(j, blk_ref):
        return (jnp.minimum(j, blk_ref[n_blocks] - 1), 0)

    def w_idx(j, blk_ref):
        return (blk_ref[jnp.minimum(j, blk_ref[n_blocks] - 1)], 0, 0)

    return pl.pallas_call(
        functools.partial(_experts_kernel, n_blocks=n_blocks),
        grid_spec=pltpu.PrefetchScalarGridSpec(
            num_scalar_prefetch=1,
            grid=(n_blocks,),
            in_specs=[pl.BlockSpec((MOE_BLOCK, half), x_idx),
                      pl.BlockSpec((1, d, f), w_idx),
                      pl.BlockSpec((1, d, f), w_idx),
                      pl.BlockSpec((1, f, d), w_idx)],
            out_specs=pl.BlockSpec((MOE_BLOCK, half), lambda j, blk_ref: (j, 0))),
        out_shape=jax.ShapeDtypeStruct((n_blocks * MOE_BLOCK, half), U32),
        compiler_params=_cparams("arbitrary"),
        name="experts",
    )(blk, xs, w_gate, w_up, w_down)


def _combine_kernel(hp_ref, x1_ref, g2_ref, w_ref, sgw_ref, suw_ref, sdw_ref, dest_hbm, ys_hbm, o_ref,
                    dest_s, buf_s, sem_d, sem_r, *, tb):
    step = pl.program_id(0)
    n_assign = TOP_K * tb
    cp = pltpu.make_async_copy(dest_hbm.at[pl.ds(step * n_assign, n_assign)], dest_s, sem_d)
    cp.start()
    cp.wait()

    def row_copy(t, k):
        d = dest_s[k * tb + t]
        return pltpu.make_async_copy(ys_hbm.at[pl.ds(d, 1)], buf_s.at[k, pl.ds(t, 1)], sem_r)

    def issue(t, carry):
        for k in range(TOP_K):
            row_copy(t, k).start()
        return carry

    lax.fori_loop(0, tb, issue, 0)

    xa, xb = _unpack_halves(hp_ref[...])
    xa = xa.astype(BF16)
    xb = xb.astype(BF16)
    half = xa.shape[1]
    g = _dot(xa, sgw_ref[0:half]) + _dot(xb, sgw_ref[half:])
    u = _dot(xa, suw_ref[0:half]) + _dot(xb, suw_ref[half:])
    shared = _dot((_silu(g) * u).astype(BF16), sdw_ref[...])

    def drain(t, carry):
        for k in range(TOP_K):
            row_copy(t, k).wait()
        return carry

    lax.fori_loop(0, tb, drain, 0)

    w = w_ref[...]
    acc_a = jnp.zeros((tb, half), F32)
    acc_b = jnp.zeros((tb, half), F32)
    for k in range(TOP_K):
        ya, yb = _unpack_halves(buf_s[k])
        acc_a += ya * w[:, k:k + 1]
        acc_b += yb * w[:, k:k + 1]
    g2 = g2_ref[0]
    o_ref[:, 0:half] = x1_ref[:, 0:half] + g2[:, 0:half] * (acc_a + shared[:, 0:half])
    o_ref[:, half:] = x1_ref[:, half:] + g2[:, half:] * (acc_b + shared[:, half:])


def _combine(hp, x1, g2, w_tok, sgw, suw, sdw, dest_flat, ys, *, tb, seq_len):
    t, half = hp.shape
    d = 2 * half
    tb = min(tb, seq_len)
    per_seq = seq_len // tb
    f = sgw.shape[1]
    return pl.pallas_call(
        functools.partial(_combine_kernel, tb=tb),
        grid=(t // tb,),
        in_specs=[pl.BlockSpec((tb, half), lambda i: (i, 0)),
                  pl.BlockSpec((tb, d), lambda i: (i, 0)),
                  pl.BlockSpec((1, 1, d), lambda i: (i // per_seq, 0, 0)),
                  pl.BlockSpec((tb, TOP_K), lambda i: (i, 0)),
                  pl.BlockSpec((d, f), lambda i: (0, 0)),
                  pl.BlockSpec((d, f), lambda i: (0, 0)),
                  pl.BlockSpec((f, d), lambda i: (0, 0)),
                  pl.BlockSpec(memory_space=pl.ANY),
                  pl.BlockSpec(memory_space=pl.ANY)],
        out_specs=pl.BlockSpec((tb, d), lambda i: (i, 0)),
        out_shape=jax.ShapeDtypeStruct((t, d), F32),
        scratch_shapes=[pltpu.SMEM((TOP_K * tb,), I32), pltpu.VMEM((TOP_K, tb, half), U32),
                        pltpu.SemaphoreType.DMA, pltpu.SemaphoreType.DMA],
        compiler_params=_cparams("arbitrary"),
        name="combine",
    )(hp, x1, g2, w_tok, sgw, suw, sdw, dest_flat, ys)


def _rope_tables(l):
    rows = l // GRID_W
    r = jnp.repeat(jnp.arange(rows), GRID_W).astype(F32)
    col = jnp.tile(jnp.arange(GRID_W), rows).astype(F32)
    n_f = HEAD_DIM // 4
    freqs = ROPE_THETA ** (-jnp.arange(n_f, dtype=F32) / n_f)
    ang = jnp.concatenate([r[:, None] * freqs, col[:, None] * freqs], axis=-1)
    ang = jnp.tile(jnp.repeat(ang, 2, axis=1), (1, LANES // HEAD_DIM))
    sign = jnp.where(jnp.arange(LANES) % 2 == 0, -1.0, 1.0).astype(F32)
    return jnp.cos(ang), jnp.sin(ang) * sign


def kernel(x, c, ctx, c_ctx, w_mod, b_mod, norm1_w, norm2_w, w_in, q_norm_w, k_norm_w, ret_decay_fwd,
           ret_decay_bwd, w_out, router_w, router_bias, exp_w_gate, exp_w_up, exp_w_down, sh_w_gate,
           sh_w_up, sh_w_down):
    b, l, d = x.shape
    lc = ctx.shape[1]
    t = b * l
    assert w_mod.shape[0] == 1, "single layer"
    assert l % CHUNK == 0 and lc % CHUNK == 0 and l % GRID_W == 0

    rows = (b + 1 + 7) // 8 * 8
    cc = jnp.zeros((rows, d), F32).at[:b].set(c).at[b].set(c_ctx)
    mod = _modulation(cc, w_mod[0], b_mod[0])
    sh1, sc1, g1, sh2, sc2, g2 = [mod[:b, i * d:(i + 1) * d].reshape(b, 1, d) for i in range(6)]
    shc = mod[b, 0:d].reshape(1, 1, d)
    scc = mod[b, d:2 * d].reshape(1, 1, d)

    wi = w_in[0].astype(BF16)
    qnw = jnp.tile(q_norm_w[0], LANES // HEAD_DIM).reshape(1, LANES)
    knw = jnp.tile(k_norm_w[0], LANES // HEAD_DIM).reshape(1, LANES)
    cos, sin = _rope_tables(l)
    n1 = norm1_w[0].reshape(1, d)

    cklo, ckhi, cvlo, cvhi, crk, crv = _projection(
        ctx, shc, scc, n1, wi, qnw, knw, cos[:lc], sin[:lc], rope=False, with_q=False, tm=256)
    klo, khi, vlo, vhi, rk, rv, q, rq, sg = _projection(
        x, sh1, sc1, n1, wi, qnw, knw, cos, sin, rope=True, with_q=True, tm=256)

    attn = _attention(q, klo, khi, vlo, vhi, cklo, ckhi, cvlo, cvhi, tq=256)
    dec_f = jnp.repeat(ret_decay_fwd[0].astype(F32), HEAD_DIM).reshape(1, RET_W)
    dec_b = jnp.repeat(ret_decay_bwd[0].astype(F32), HEAD_DIM).reshape(1, RET_W)
    ret = _retention(rq, rk, rv, sg, crk, crv, dec_f, dec_b)

    wo = w_out[0].astype(BF16)
    r_hi, r_lo = _split(router_w[0].T)
    x1, hp, logits_t = _out_projection(attn, ret, x, wo[:ATTN_W], wo[ATTN_W:], g1, sh2, sc2,
                                       norm2_w[0].reshape(1, d), r_hi, r_lo, tm=256)

    idx_t, w_t, rank_t, cnt_col, cnt_row = _route(logits_t, router_bias[0].reshape(N_EXPERTS, 1), tb=256)
    n_blocks = -(-(t * TOP_K) // MOE_BLOCK) + N_EXPERTS
    tb = 256
    dest, meta, blk = _destinations(idx_t, rank_t, cnt_col, cnt_row, tb=tb, n_blocks=n_blocks)
    dest_flat = dest.reshape(-1)
    xs = _dispatch(meta, hp, dest_flat, tb=tb, n_blocks=n_blocks)
    ys = _experts(blk.reshape(-1), xs, exp_w_gate[0], exp_w_up[0], exp_w_down[0], n_blocks=n_blocks)
    out = _combine(hp, x1.reshape(t, d), g2, w_t.T, sh_w_gate[0].astype(BF16), sh_w_up[0].astype(BF16),
                   sh_w_down[0].astype(BF16), dest_flat, ys, tb=tb, seq_len=l)
    return out.reshape(b, l, d)
```

```python
import functools

import jax
import jax.numpy as jnp
from jax import lax
from jax.experimental import pallas as pl
from jax.experimental.pallas import tpu as pltpu

F32 = jnp.float32
BF16 = jnp.bfloat16
I32 = jnp.int32
U32 = jnp.uint32

HEAD_DIM = 64
LANES = 128
SUBLANES = 8
ATTN_HEADS = 8
ATTN_KV_HEADS = 2
GQA = ATTN_HEADS // ATTN_KV_HEADS
RET_HEADS = 8
ATTN_W = ATTN_HEADS * HEAD_DIM
KV_W = ATTN_KV_HEADS * HEAD_DIM
RET_W = RET_HEADS * HEAD_DIM
RET_PAIRS = RET_W // LANES
CHUNK = 128
GRID_W = 64
ROPE_THETA = 10000.0
N_EXPERTS = 256
TOP_K = 8
N_GROUPS = 8
GROUP_SIZE = N_EXPERTS // N_GROUPS
TOPK_GROUPS = 4
ROUTED_SCALE = 2.5
MOE_BLOCK = 128
EPS = 1e-6
QK_SCALE = HEAD_DIM ** -0.5

OFF_AK = 0
OFF_AV = OFF_AK + KV_W
OFF_RK = OFF_AV + KV_W
OFF_RV = OFF_RK + RET_W
CTX_KV_COLS = OFF_RV + RET_W
OFF_AQ = CTX_KV_COLS
OFF_RQ = OFF_AQ + ATTN_W
OFF_RG = OFF_RQ + RET_W
IN_COLS = OFF_RG + RET_W

VMEM_LIMIT = 52 * 1024 * 1024

TILE_PROJ = 256
TILE_ATTN_Q = 256
TILE_OUT = 256
TILE_TOKENS = 256
ITEM_BLOCKS = 5
HI_MASK = 0xFFFF0000


def _cparams(*sem):
    return pltpu.CompilerParams(dimension_semantics=sem, vmem_limit_bytes=VMEM_LIMIT)


def _split(a):
    hi = a.astype(BF16)
    lo = (a - hi.astype(F32)).astype(BF16)
    return hi, lo


def _dot(a, b):
    return jnp.dot(a, b, preferred_element_type=F32)


def _dot_nt(a, b):
    return lax.dot_general(a, b, (((1,), (1,)), ((), ())), preferred_element_type=F32)


def _sigmoid(v):
    return 1.0 / (1.0 + jnp.exp(-v))


def _silu(v):
    return v * _sigmoid(v)


def _pack_halves(a, b):
    ua = lax.bitcast_convert_type(a.astype(BF16).astype(F32), U32)
    ub = lax.bitcast_convert_type(b.astype(BF16).astype(F32), U32)
    return (ua & jnp.uint32(HI_MASK)) | (ub >> 16)


def _unpack_halves(u):
    a = lax.bitcast_convert_type(u & jnp.uint32(HI_MASK), F32)
    b = lax.bitcast_convert_type(u << 16, F32)
    return a, b


def _mod_kernel(c_ref, w_ref, b_ref, o_ref):
    s_hi, s_lo = _split(_silu(c_ref[...]))
    w_hi, w_lo = _split(w_ref[...])
    o_ref[...] = _dot(s_hi, w_hi) + _dot(s_hi, w_lo) + _dot(s_lo, w_hi) + b_ref[...]


def _modulation(cc, w_mod, b_mod):
    rows, d = cc.shape
    n = w_mod.shape[1]
    tn = 768
    return pl.pallas_call(
        _mod_kernel,
        grid=(n // tn,),
        in_specs=[pl.BlockSpec((rows, d), lambda j: (0, 0)),
                  pl.BlockSpec((d, tn), lambda j: (0, j)),
                  pl.BlockSpec((1, tn), lambda j: (0, j))],
        out_specs=pl.BlockSpec((rows, tn), lambda j: (0, j)),
        out_shape=jax.ShapeDtypeStruct((rows, n), F32),
        compiler_params=_cparams("arbitrary"),
        name="mod",
    )(cc, w_mod, b_mod.reshape(1, n))


def _segment_ones():
    r = lax.broadcasted_iota(I32, (LANES, LANES), 0) // HEAD_DIM
    c = lax.broadcasted_iota(I32, (LANES, LANES), 1) // HEAD_DIM
    return jnp.where(r == c, 1.0, 0.0).astype(BF16)


def _head_mean_sq(v, seg):
    hi, lo = _split(v * v)
    return (_dot(hi, seg) + _dot(lo, seg)) * (1.0 / HEAD_DIM)


def _proj_kernel(x_ref, sh_ref, sc_ref, nw_ref, wi_ref, qnw_ref, knw_ref, cos_ref, sin_ref,
                 *out_refs, rope, with_q):
    if with_q:
        klo_ref, khi_ref, vlo_ref, vhi_ref, rk_ref, rv_ref, q_ref, rq_ref, sg_ref = out_refs
    else:
        klo_ref, khi_ref, vlo_ref, vhi_ref, rk_ref, rv_ref = out_refs
    x = x_ref[0]
    h = x * lax.rsqrt(jnp.mean(x * x, axis=-1, keepdims=True) + EPS) * nw_ref[...]
    h = h * (1.0 + sc_ref[0]) + sh_ref[0]
    z = _dot(h.astype(BF16), wi_ref[...])

    seg = _segment_ones()
    lane = lax.broadcasted_iota(I32, (x.shape[0], LANES), 1)
    low_half = lane < HEAD_DIM
    even = (lane & 1) == 0

    def norm_rope(v, w128):
        v = v * lax.rsqrt(_head_mean_sq(v, seg) + EPS) * w128
        if rope:
            swapped = jnp.where(even, pltpu.roll(v, LANES - 1, 1), pltpu.roll(v, 1, 1))
            v = v * cos_ref[...] + swapped * sin_ref[...]
        return v

    def write_lo_hi(v, lo_ref, hi_ref):
        sw = pltpu.roll(v, HEAD_DIM, 1)
        lo_ref[0, 0] = jnp.where(low_half, v, 0.0).astype(BF16)
        hi_ref[0, 0] = jnp.where(low_half, 0.0, sw).astype(BF16)
        lo_ref[0, 1] = jnp.where(low_half, sw, 0.0).astype(BF16)
        hi_ref[0, 1] = jnp.where(low_half, 0.0, v).astype(BF16)

    write_lo_hi(norm_rope(z[:, OFF_AK:OFF_AK + KV_W], knw_ref[...]), klo_ref, khi_ref)
    write_lo_hi(z[:, OFF_AV:OFF_AV + KV_W], vlo_ref, vhi_ref)
    rk_ref[0] = (z[:, OFF_RK:OFF_RK + RET_W] * QK_SCALE).astype(BF16)
    rv_ref[0] = z[:, OFF_RV:OFF_RV + RET_W].astype(BF16)
    if with_q:
        for j in range(ATTN_W // LANES):
            qj = norm_rope(z[:, OFF_AQ + j * LANES:OFF_AQ + (j + 1) * LANES], qnw_ref[...])
            q_ref[0, :, j * LANES:(j + 1) * LANES] = (qj * QK_SCALE).astype(BF16)
        rq_ref[0] = z[:, OFF_RQ:OFF_RQ + RET_W].astype(BF16)
        sg_ref[0] = _silu(z[:, OFF_RG:OFF_RG + RET_W]).astype(BF16)


def _projection(x, shift, scale, norm_w, wi_bf16, qnw, knw, cos, sin, *, rope, with_q, tm):
    b, l, d = x.shape
    tm = min(tm, l)
    ncols = IN_COLS if with_q else CTX_KV_COLS
    per_batch = shift.shape[0] > 1
    mod_idx = (lambda bi, i: (bi, 0, 0)) if per_batch else (lambda bi, i: (0, 0, 0))
    kv_shape = jax.ShapeDtypeStruct((b, ATTN_KV_HEADS, l, LANES), BF16)
    kv_spec = pl.BlockSpec((1, ATTN_KV_HEADS, tm, LANES), lambda bi, i: (bi, 0, i, 0))
    w_shape = jax.ShapeDtypeStruct((b, l, RET_W), BF16)
    w_spec = pl.BlockSpec((1, tm, RET_W), lambda bi, i: (bi, i, 0))
    out_shape = [kv_shape] * 4 + [w_shape] * 2
    out_specs = [kv_spec] * 4 + [w_spec] * 2
    if with_q:
        out_shape += [w_shape] * 3
        out_specs += [w_spec] * 3
    return pl.pallas_call(
        functools.partial(_proj_kernel, rope=rope, with_q=with_q),
        grid=(b, l // tm),
        in_specs=[pl.BlockSpec((1, tm, d), lambda bi, i: (bi, i, 0)),
                  pl.BlockSpec((1, 1, d), mod_idx),
                  pl.BlockSpec((1, 1, d), mod_idx),
                  pl.BlockSpec((1, d), lambda bi, i: (0, 0)),
                  pl.BlockSpec((d, ncols), lambda bi, i: (0, 0)),
                  pl.BlockSpec((1, LANES), lambda bi, i: (0, 0)),
                  pl.BlockSpec((1, LANES), lambda bi, i: (0, 0)),
                  pl.BlockSpec((tm, LANES), lambda bi, i: (i, 0)),
                  pl.BlockSpec((tm, LANES), lambda bi, i: (i, 0))],
        out_specs=out_specs,
        out_shape=out_shape,
        compiler_params=_cparams("arbitrary", "arbitrary"),
        name="proj_latent" if with_q else "proj_ctx",
    )(x, shift, scale, norm_w, wi_bf16, qnw, knw, cos, sin)


def _attn_kernel(q_ref, klo_ref, khi_ref, vlo_ref, vhi_ref, cklo_ref, ckhi_ref, cvlo_ref, cvhi_ref,
                 o_ref, kl_s, kh_s, va_s, *, l, lc):
    lk = l + lc

    @pl.when(pl.program_id(2) == 0)
    def _():
        kl_s[0:l] = klo_ref[0, 0]
        kl_s[l:lk] = cklo_ref[0, 0]
        kh_s[0:l] = khi_ref[0, 0]
        kh_s[l:lk] = ckhi_ref[0, 0]
        lane = lax.broadcasted_iota(I32, (lk, LANES), 1)
        ones_lo = jnp.where(lane < HEAD_DIM, 1.0, 0.0).astype(BF16)
        ones_hi = jnp.where(lane < HEAD_DIM, 0.0, 1.0).astype(BF16)
        for g, (v_ref, cv_ref, ones) in enumerate(((vlo_ref, cvlo_ref, ones_lo), (vhi_ref, cvhi_ref, ones_hi),
                                                   (vlo_ref, cvlo_ref, ones_lo), (vhi_ref, cvhi_ref, ones_hi))):
            v_col, one_col = (0, LANES) if g < 2 else (LANES, 0)
            va_s[g, 0:l, v_col:v_col + LANES] = v_ref[0, 0]
            va_s[g, l:lk, v_col:v_col + LANES] = cv_ref[0, 0]
            va_s[g, :, one_col:one_col + LANES] = ones

    q = q_ref[0]
    acc = []
    for g in range(GQA):
        qp = q[:, (g // 2) * LANES:(g // 2 + 1) * LANES]
        kk = kl_s[...] if g % 2 == 0 else kh_s[...]
        s = _dot_nt(qp, kk)
        p = jnp.exp(s - jnp.max(s, axis=-1, keepdims=True)).astype(BF16)
        acc.append(_dot(p, va_s[g]))
    out_a = acc[0] + acc[1]
    out_b = acc[2] + acc[3]
    o_ref[0, :, 0:LANES] = (out_a[:, 0:LANES] / out_a[:, LANES:2 * LANES]).astype(BF16)
    o_ref[0, :, LANES:2 * LANES] = (out_b[:, LANES:2 * LANES] / out_b[:, 0:LANES]).astype(BF16)


def _attention(q, klo, khi, vlo, vhi, cklo, ckhi, cvlo, cvhi, *, tq):
    b, l, _ = q.shape
    lc = cklo.shape[2]
    lk = l + lc
    tq = min(tq, l)
    gw = GQA * HEAD_DIM
    kv_spec = pl.BlockSpec((1, 1, l, LANES), lambda bi, h, i: (bi, h, 0, 0))
    ckv_spec = pl.BlockSpec((1, 1, lc, LANES), lambda bi, h, i: (bi, h, 0, 0))
    return pl.pallas_call(
        functools.partial(_attn_kernel, l=l, lc=lc),
        grid=(b, ATTN_KV_HEADS, l // tq),
        in_specs=[pl.BlockSpec((1, tq, gw), lambda bi, h, i: (bi, i, h))] + [kv_spec] * 4 + [ckv_spec] * 4,
        out_specs=pl.BlockSpec((1, tq, gw), lambda bi, h, i: (bi, i, h)),
        out_shape=jax.ShapeDtypeStruct((b, l, ATTN_W), BF16),
        scratch_shapes=[pltpu.VMEM((lk, LANES), BF16), pltpu.VMEM((lk, LANES), BF16),
                        pltpu.VMEM((GQA, lk, 2 * LANES), BF16)],
        compiler_params=_cparams("arbitrary", "arbitrary", "arbitrary"),
        name="attn",
    )(q, klo, khi, vlo, vhi, cklo, ckhi, cvlo, cvhi)


def _log_sigmoid(v):
    return jnp.minimum(v, 0.0) - jnp.log(1.0 + jnp.exp(-jnp.abs(v)))


def _ret_kernel(rq_ref, rk_ref, rv_ref, sg_ref, crk_ref, crv_ref, df_ref, db_ref, o_ref,
                m_s, tab_s, sb_s, sf_s, *, l, lc):
    n = l // CHUNK
    nc = lc // CHUNK
    lgf = _log_sigmoid(df_ref[...])
    lgb = _log_sigmoid(db_ref[...])
    pos = lax.broadcasted_iota(I32, (CHUNK, RET_W), 0).astype(F32)
    tab_s[0] = jnp.exp((pos + 1.0) * lgf)
    tab_s[1] = jnp.exp((CHUNK - pos) * lgb)
    tab_s[2] = jnp.exp((CHUNK - 1.0 - pos) * lgf)
    tab_s[3] = jnp.exp(pos * lgb)
    gf_c = jnp.exp(CHUNK * lgf)
    gb_c = jnp.exp(CHUNK * lgb)
    row = lax.broadcasted_iota(I32, (CHUNK, CHUNK), 0)
    col = lax.broadcasted_iota(I32, (CHUNK, CHUNK), 1)
    diff = (row - col).astype(F32)
    for h in range(RET_HEADS):
        lf = lgf[:, h * HEAD_DIM:h * HEAD_DIM + 1]
        lb = lgb[:, h * HEAD_DIM:h * HEAD_DIM + 1]
        m_s[h] = jnp.where(diff > 0, jnp.exp(diff * lf), jnp.where(diff < 0, jnp.exp(-diff * lb), 2.0))

    lane = lax.broadcasted_iota(I32, (CHUNK, LANES), 1)
    low_half = lane < HEAD_DIM
    diag = (lax.broadcasted_iota(I32, (LANES, LANES), 0) // HEAD_DIM
            == lax.broadcasted_iota(I32, (LANES, LANES), 1) // HEAD_DIM)
    seg = jnp.where(diag, 1.0, 0.0).astype(BF16)

    def contrib(k_ref, v_ref, r0, p, zeta_idx):
        cols = slice(p * LANES, (p + 1) * LANES)
        kz = k_ref[0, pl.ds(r0, CHUNK), cols].astype(F32) * tab_s[zeta_idx, :, cols]
        kv = _dot(jnp.transpose(kz).astype(BF16), v_ref[0, pl.ds(r0, CHUNK), cols])
        return jnp.where(diag, kv, 0.0)

    for p in range(RET_PAIRS):
        cols = slice(p * LANES, (p + 1) * LANES)
        sf = jnp.zeros((LANES, LANES), F32)
        sb = jnp.zeros((LANES, LANES), F32)
        for c in range(nc):
            sf = gf_c[:, cols] * sf + contrib(crk_ref, crv_ref, c * CHUNK, p, 2)
            cb = nc - 1 - c
            sb = gb_c[:, cols] * sb + contrib(crk_ref, crv_ref, cb * CHUNK, p, 3)
        sf_s[p] = sf
        sb_s[n, p] = sb

    def bwd_body(j, carry):
        c = n - 1 - j
        r0 = pl.multiple_of(c * CHUNK, CHUNK)
        for p in range(RET_PAIRS):
            cols = slice(p * LANES, (p + 1) * LANES)
            prev = sb_s[c + 1, p]
            sb_s[c, p] = gb_c[:, cols] * prev + contrib(rk_ref, rv_ref, r0, p, 3)
        return carry

    lax.fori_loop(0, n, bwd_body, 0)

    def fwd_body(c, carry):
        r0 = pl.multiple_of(c * CHUNK, CHUNK)
        for p in range(RET_PAIRS):
            cols = slice(p * LANES, (p + 1) * LANES)
            qp = rq_ref[0, pl.ds(r0, CHUNK), cols]
            kp = rk_ref[0, pl.ds(r0, CHUNK), cols]
            vp = rv_ref[0, pl.ds(r0, CHUNK), cols]
            zero = jnp.zeros_like(kp)
            s0 = _dot_nt(qp, jnp.where(low_half, kp, zero))
            s1 = _dot_nt(qp, jnp.where(low_half, zero, kp))
            a0 = (s0 * m_s[2 * p]).astype(BF16)
            a1 = (s1 * m_s[2 * p + 1]).astype(BF16)
            y = _dot(a0, jnp.where(low_half, vp, zero)) + _dot(a1, jnp.where(low_half, zero, vp))
            qf = qp.astype(F32)
            sf = sf_s[p]
            y += _dot((qf * tab_s[0, :, cols]).astype(BF16), sf.astype(BF16))
            y += _dot((qf * tab_s[1, :, cols]).astype(BF16), sb_s[c + 1, p].astype(BF16))
            hi, lo = _split(y * y)
            ms = (_dot(hi, seg) + _dot(lo, seg)) * (1.0 / HEAD_DIM)
            out = y * lax.rsqrt(ms + EPS) * sg_ref[0, pl.ds(r0, CHUNK), cols].astype(F32)
            o_ref[0, pl.ds(r0, CHUNK), cols] = out.astype(BF16)
            sf_s[p] = gf_c[:, cols] * sf + contrib(rk_ref, rv_ref, r0, p, 2)
        return carry

    lax.fori_loop(0, n, fwd_body, 0)


def _retention(rq, rk, rv, sg, crk, crv, dec_f, dec_b):
    b, l, _ = rq.shape
    lc = crk.shape[1]
    n = l // CHUNK
    spec = pl.BlockSpec((1, l, RET_W), lambda bi: (bi, 0, 0))
    cspec = pl.BlockSpec((1, lc, RET_W), lambda bi: (bi, 0, 0))
    dspec = pl.BlockSpec((1, RET_W), lambda bi: (0, 0))
    return pl.pallas_call(
        functools.partial(_ret_kernel, l=l, lc=lc),
        grid=(b,),
        in_specs=[spec, spec, spec, spec, cspec, cspec, dspec, dspec],
        out_specs=spec,
        out_shape=jax.ShapeDtypeStruct((b, l, RET_W), BF16),
        scratch_shapes=[pltpu.VMEM((RET_HEADS, CHUNK, CHUNK), F32),
                        pltpu.VMEM((4, CHUNK, RET_W), F32),
                        pltpu.VMEM((n + 1, RET_PAIRS, LANES, LANES), F32),
                        pltpu.VMEM((RET_PAIRS, LANES, LANES), F32)],
        compiler_params=_cparams("arbitrary"),
        name="ret",
    )(rq, rk, rv, sg, crk, crv, dec_f, dec_b)


def _out_kernel(attn_ref, ret_ref, x_ref, wa_ref, wr_ref, g1_ref, sh_ref, sc_ref, nw_ref, rhi_ref, rlo_ref,
                x1_ref, hp_ref, lg_ref):
    y = _dot(attn_ref[0], wa_ref[...]) + _dot(ret_ref[0], wr_ref[...])
    x1 = x_ref[0] + g1_ref[0] * y
    x1_ref[0] = x1
    h = x1 * lax.rsqrt(jnp.mean(x1 * x1, axis=-1, keepdims=True) + EPS) * nw_ref[...]
    h = h * (1.0 + sc_ref[0]) + sh_ref[0]
    half = h.shape[1] // 2
    hp_ref[...] = _pack_halves(h[:, :half], h[:, half:])
    h_hi, h_lo = _split(h)
    lg_ref[...] = _dot_nt(rhi_ref[...], h_hi) + _dot_nt(rhi_ref[...], h_lo) + _dot_nt(rlo_ref[...], h_hi)


def _out_projection(attn, ret, x, wa, wr, g1, sh2, sc2, norm_w, r_hi, r_lo, *, tm):
    b, l, d = x.shape
    tm = min(tm, l)
    nt = l // tm
    t = b * l
    mspec = pl.BlockSpec((1, 1, d), lambda bi, i: (bi, 0, 0))
    return pl.pallas_call(
        _out_kernel,
        grid=(b, nt),
        in_specs=[pl.BlockSpec((1, tm, ATTN_W), lambda bi, i: (bi, i, 0)),
                  pl.BlockSpec((1, tm, RET_W), lambda bi, i: (bi, i, 0)),
                  pl.BlockSpec((1, tm, d), lambda bi, i: (bi, i, 0)),
                  pl.BlockSpec((ATTN_W, d), lambda bi, i: (0, 0)),
                  pl.BlockSpec((RET_W, d), lambda bi, i: (0, 0)),
                  mspec, mspec, mspec,
                  pl.BlockSpec((1, d), lambda bi, i: (0, 0)),
                  pl.BlockSpec((N_EXPERTS, d), lambda bi, i: (0, 0)),
                  pl.BlockSpec((N_EXPERTS, d), lambda bi, i: (0, 0))],
        out_specs=[pl.BlockSpec((1, tm, d), lambda bi, i: (bi, i, 0)),
                   pl.BlockSpec((tm, d // 2), lambda bi, i: (bi * nt + i, 0)),
                   pl.BlockSpec((N_EXPERTS, tm), lambda bi, i: (0, bi * nt + i))],
        out_shape=[jax.ShapeDtypeStruct((b, l, d), F32),
                   jax.ShapeDtypeStruct((t, d // 2), U32),
                   jax.ShapeDtypeStruct((N_EXPERTS, t), F32)],
        compiler_params=_cparams("arbitrary", "arbitrary"),
        name="out_proj",
    )(attn, ret, x, wa, wr, g1, sh2, sc2, norm_w, r_hi, r_lo)


def _route_kernel(lg_ref, bias_ref, idx_ref, w_ref, rank_ref, cnt_col_ref, cnt_row_ref, tri_s, col_s, row_s):
    tb = lg_ref.shape[1]
    step = pl.program_id(0)

    @pl.when(step == 0)
    def _():
        r = lax.broadcasted_iota(I32, (tb, tb), 0)
        c = lax.broadcasted_iota(I32, (tb, tb), 1)
        tri_s[...] = jnp.where(r <= c, 1.0, 0.0).astype(BF16)
        col_s[...] = jnp.zeros_like(col_s)
        row_s[...] = jnp.zeros_like(row_s)

    scores = _sigmoid(lg_ref[...])
    biased = scores + bias_ref[...]
    neg = -jnp.inf
    sub = lax.broadcasted_iota(I32, (GROUP_SIZE, tb), 0).astype(F32)

    gscore = []
    for g in range(N_GROUPS):
        blk = biased[g * GROUP_SIZE:(g + 1) * GROUP_SIZE]
        m1 = jnp.max(blk, axis=0, keepdims=True)
        first = jnp.min(jnp.where(blk == m1, sub, float(GROUP_SIZE)), axis=0, keepdims=True)
        m2 = jnp.max(jnp.where(sub == first, neg, blk), axis=0, keepdims=True)
        gscore.append(m1 + m2)
    gs = jnp.concatenate(gscore, axis=0)
    gsub = lax.broadcasted_iota(I32, (N_GROUPS, tb), 0).astype(F32)
    keep = jnp.zeros((N_GROUPS, tb), F32)
    for _ in range(TOPK_GROUPS):
        m = jnp.max(gs, axis=0, keepdims=True)
        first = jnp.min(jnp.where(gs == m, gsub, float(N_GROUPS)), axis=0, keepdims=True)
        sel = gsub == first
        keep = jnp.where(sel, 1.0, keep)
        gs = jnp.where(sel, neg, gs)
    masked = jnp.concatenate(
        [jnp.where(keep[g:g + 1] > 0.0, biased[g * GROUP_SIZE:(g + 1) * GROUP_SIZE], neg)
         for g in range(N_GROUPS)], axis=0)

    esub = lax.broadcasted_iota(I32, (N_EXPERTS, tb), 0).astype(F32)
    sels, idxs, ws = [], [], []
    chosen = jnp.zeros((N_EXPERTS, tb), F32)
    for _ in range(TOP_K):
        m = jnp.max(masked, axis=0, keepdims=True)
        first = jnp.min(jnp.where(masked == m, esub, float(N_EXPERTS)), axis=0, keepdims=True)
        sel = esub == first
        sels.append(sel)
        idxs.append(first)
        ws.append(jnp.sum(jnp.where(sel, scores, 0.0), axis=0, keepdims=True))
        chosen = jnp.where(sel, 1.0, chosen)
        masked = jnp.where(sel, neg, masked)
    wsum = ws[0]
    for k in range(1, TOP_K):
        wsum = wsum + ws[k]
    idx_ref[...] = jnp.concatenate(idxs, axis=0).astype(I32)
    w_ref[...] = jnp.concatenate([wk / wsum * ROUTED_SCALE for wk in ws], axis=0)

    chosen_b = chosen.astype(BF16)
    incl = _dot(chosen_b, tri_s[...])
    before = incl - chosen + col_s[...]
    rank_ref[...] = jnp.concatenate(
        [jnp.sum(jnp.where(sel, before, 0.0), axis=0, keepdims=True) for sel in sels], axis=0).astype(I32)
    col_s[...] = col_s[...] + incl[:, tb - 1:tb]
    row_s[...] = row_s[...] + _dot_nt(jnp.ones((8, tb), BF16), chosen_b)
    cnt_col_ref[...] = col_s[...].astype(I32)
    cnt_row_ref[...] = row_s[...].astype(I32)


def _route(logits_t, bias_col, *, tb):
    e, t = logits_t.shape
    tb = min(tb, t)
    kspec = pl.BlockSpec((TOP_K, tb), lambda i: (0, i))
    return pl.pallas_call(
        _route_kernel,
        grid=(t // tb,),
        in_specs=[pl.BlockSpec((e, tb), lambda i: (0, i)),
                  pl.BlockSpec((e, 1), lambda i: (0, 0))],
        out_specs=[kspec, kspec, kspec,
                   pl.BlockSpec((e, 1), lambda i: (0, 0)),
                   pl.BlockSpec((8, e), lambda i: (0, 0))],
        out_shape=[jax.ShapeDtypeStruct((TOP_K, t), I32),
                   jax.ShapeDtypeStruct((TOP_K, t), F32),
                   jax.ShapeDtypeStruct((TOP_K, t), I32),
                   jax.ShapeDtypeStruct((e, 1), I32),
                   jax.ShapeDtypeStruct((8, e), I32)],
        scratch_shapes=[pltpu.VMEM((tb, tb), BF16), pltpu.VMEM((e, 1), F32), pltpu.VMEM((8, e), F32)],
        compiler_params=_cparams("arbitrary"),
        name="route",
    )(logits_t, bias_col)


def _pad_block(cnt):
    return (cnt + (MOE_BLOCK - 1)) // MOE_BLOCK * MOE_BLOCK


def _max_items(n_blocks):
    return n_blocks // ITEM_BLOCKS + N_EXPERTS


def _dest_kernel(idx_ref, rank_ref, cnt_col_ref, cnt_row_ref, dest_ref, meta_ref, items_ref):
    tb = idx_ref.shape[1]
    nip = items_ref.shape[1]
    pad_col = _pad_block(cnt_col_ref[...])
    pad_row = _pad_block(cnt_row_ref[0:1, :])
    er = lax.broadcasted_iota(I32, (N_EXPERTS, N_EXPERTS), 0)
    ec = lax.broadcasted_iota(I32, (N_EXPERTS, N_EXPERTS), 1)
    start_col = jnp.sum(jnp.where(ec < er, pad_row, 0), axis=1, keepdims=True)
    start_row = jnp.sum(jnp.where(er < ec, pad_col, 0), axis=0, keepdims=True)

    esub = lax.broadcasted_iota(I32, (N_EXPERTS, tb), 0)
    rows = []
    for k in range(TOP_K):
        onehot = esub == idx_ref[k:k + 1, :]
        rows.append(jnp.sum(jnp.where(onehot, start_col, 0), axis=0, keepdims=True) + rank_ref[k:k + 1, :])
    dest_ref[0] = jnp.concatenate(rows, axis=0)

    used = jnp.sum(pad_row, axis=1, keepdims=True) // MOE_BLOCK
    meta_ref[...] = jnp.concatenate(
        [cnt_row_ref[0:1, :], start_row, pad_row, jnp.broadcast_to(used, (1, N_EXPERTS)),
         jnp.zeros((4, N_EXPERTS), I32)], axis=0)

    nb_col = pad_col // MOE_BLOCK
    it_col = (nb_col + (ITEM_BLOCKS - 1)) // ITEM_BLOCKS
    it_row = (pad_row // MOE_BLOCK + (ITEM_BLOCKS - 1)) // ITEM_BLOCKS
    it_start = jnp.sum(jnp.where(ec < er, it_row, 0), axis=1, keepdims=True)
    n_items = jnp.sum(it_row, axis=1, keepdims=True)
    lane = lax.broadcasted_iota(I32, (1, nip), 1)
    owner = jnp.sum(jnp.where(it_start + it_col <= lane, 1, 0), axis=0, keepdims=True)
    owner = jnp.minimum(owner, N_EXPERTS - 1)
    onehot = lax.broadcasted_iota(I32, (N_EXPERTS, nip), 0) == owner

    def pick(col):
        return jnp.sum(jnp.where(onehot, col, 0), axis=0, keepdims=True)

    j = lane - pick(it_start)
    block0 = pick(start_col) // MOE_BLOCK + ITEM_BLOCKS * j
    nvalid = jnp.clip(pick(nb_col) - ITEM_BLOCKS * j, 0, ITEM_BLOCKS)
    items_ref[...] = jnp.concatenate(
        [owner, block0, jnp.where(lane < n_items, nvalid, 0), jnp.broadcast_to(n_items, (1, nip)),
         jnp.zeros((4, nip), I32)], axis=0)


def _destinations(idx_t, rank_t, cnt_col, cnt_row, *, tb, n_blocks):
    _, t = idx_t.shape
    tb = min(tb, t)
    nip = (_max_items(n_blocks) + LANES - 1) // LANES * LANES
    kspec = pl.BlockSpec((TOP_K, tb), lambda i: (0, i))
    return pl.pallas_call(
        _dest_kernel,
        grid=(t // tb,),
        in_specs=[kspec, kspec,
                  pl.BlockSpec((N_EXPERTS, 1), lambda i: (0, 0)),
                  pl.BlockSpec((8, N_EXPERTS), lambda i: (0, 0))],
        out_specs=[pl.BlockSpec((1, TOP_K, tb), lambda i: (i, 0, 0)),
                   pl.BlockSpec((8, N_EXPERTS), lambda i: (0, 0)),
                   pl.BlockSpec((8, nip), lambda i: (0, 0))],
        out_shape=[jax.ShapeDtypeStruct((t // tb, TOP_K, tb), I32),
                   jax.ShapeDtypeStruct((8, N_EXPERTS), I32),
                   jax.ShapeDtypeStruct((8, nip), I32)],
        compiler_params=_cparams("arbitrary"),
        name="dest",
    )(idx_t, rank_t, cnt_col, cnt_row)


_PAD_BITS = (64, 32, 16, 8)


def _dispatch_kernel(meta_ref, hp_ref, dest_hbm, xs_hbm, dest_s, zero_s, sem_d, sem_r, sem_z, *, tb, e_per_step):
    step = pl.program_id(0)
    n_assign = TOP_K * tb
    cp = pltpu.make_async_copy(dest_hbm.at[pl.ds(step * n_assign, n_assign)], dest_s, sem_d)
    cp.start()
    zero_s[...] = jnp.zeros_like(zero_s)
    cp.wait()

    def row_copy(t, k):
        d = dest_s[k * tb + t]
        return pltpu.make_async_copy(hp_ref.at[pl.ds(t, 1)], xs_hbm.at[pl.ds(d, 1)], sem_r)

    def issue(t, carry):
        for k in range(TOP_K):
            row_copy(t, k).start(priority=k % 2)
        return carry

    lax.fori_loop(0, tb, issue, 0)

    def tail_copy(c):
        row0 = pl.multiple_of((meta_ref[3, 0] + c) * MOE_BLOCK, MOE_BLOCK)
        return pltpu.make_async_copy(zero_s, xs_hbm.at[pl.ds(row0, MOE_BLOCK)], sem_z)

    @pl.when(step == 0)
    def _():
        for c in range(ITEM_BLOCKS - 1):
            tail_copy(c).start()
        for c in range(ITEM_BLOCKS - 1):
            tail_copy(c).wait()

    def pad_copies(e):
        cnt = meta_ref[0, e]
        off = meta_ref[1, e] + cnt
        rem = meta_ref[2, e] - cnt
        head = rem & (SUBLANES - 1)
        out = []
        for i in range(SUBLANES - 1):
            out.append((i < head,
                        pltpu.make_async_copy(zero_s.at[pl.ds(0, 1)], xs_hbm.at[pl.ds(off + i, 1)], sem_z)))
        off = off + head
        for bit in _PAD_BITS:
            out.append(((rem & bit) != 0,
                        pltpu.make_async_copy(zero_s.at[pl.ds(0, bit)],
                                              xs_hbm.at[pl.ds(pl.multiple_of(off, SUBLANES), bit)], sem_z)))
            off = off + (rem & bit)
        return out

    for j in range(e_per_step):
        for cond, c in pad_copies(step * e_per_step + j):
            pl.when(cond)(c.start)
    for j in range(e_per_step):
        for cond, c in pad_copies(step * e_per_step + j):
            pl.when(cond)(c.wait)

    for k in range(TOP_K):
        pltpu.make_async_copy(hp_ref, xs_hbm.at[pl.ds(0, tb)], sem_r).wait()


def _dispatch(meta, hp, dest_flat, *, tb, n_blocks):
    t, half = hp.shape
    tb = min(tb, t)
    steps = t // tb
    e_per_step = max(N_EXPERTS // steps, 1)
    assert e_per_step * steps == N_EXPERTS, "token blocks must divide the expert count"
    return pl.pallas_call(
        functools.partial(_dispatch_kernel, tb=tb, e_per_step=e_per_step),
        grid=(steps,),
        in_specs=[pl.BlockSpec(memory_space=pltpu.SMEM),
                  pl.BlockSpec((tb, half), lambda i: (i, 0)),
                  pl.BlockSpec(memory_space=pl.ANY)],
        out_specs=pl.BlockSpec(memory_space=pl.ANY),
        out_shape=jax.ShapeDtypeStruct(((n_blocks + ITEM_BLOCKS - 1) * MOE_BLOCK, half), U32),
        scratch_shapes=[pltpu.SMEM((TOP_K * tb,), I32), pltpu.VMEM((MOE_BLOCK, half), U32),
                        pltpu.SemaphoreType.DMA, pltpu.SemaphoreType.DMA, pltpu.SemaphoreType.DMA],
        compiler_params=_cparams("arbitrary"),
        name="dispatch",
    )(meta, hp, dest_flat)


def _experts_kernel(items_ref, xs_hbm, wg_ref, wu_ref, wd_ref, ys_hbm,
                    xbuf, ybuf, wg_s, wu_s, wd_s, sem_x, sem_y):
    i = pl.program_id(0)
    n_items = items_ref[3, 0]
    slot = i % 2
    rows = ITEM_BLOCKS * MOE_BLOCK

    def x_copy(item, s):
        row0 = pl.multiple_of(items_ref[1, item] * MOE_BLOCK, MOE_BLOCK)
        return pltpu.make_async_copy(xs_hbm.at[pl.ds(row0, rows)], xbuf.at[s], sem_x.at[s])

    def y_copies(item, s, fn):
        for k in range(ITEM_BLOCKS):
            row0 = pl.multiple_of((items_ref[1, item] + k) * MOE_BLOCK, MOE_BLOCK)
            cp = pltpu.make_async_copy(ybuf.at[s, pl.ds(k * MOE_BLOCK, MOE_BLOCK)],
                                       ys_hbm.at[pl.ds(row0, MOE_BLOCK)], sem_y.at[s])
            pl.when(k < items_ref[2, item])(functools.partial(fn, cp))

    @pl.when(i == 0)
    def _():
        x_copy(0, 0).start()

    @pl.when(i + 1 < n_items)
    def _():
        x_copy(i + 1, 1 - slot).start()

    @pl.when(i < n_items)
    def _():
        prev = jnp.maximum(i - 1, 0)

        @pl.when((i == 0) | (items_ref[0, i] != items_ref[0, prev]))
        def _():
            wg_s[...] = wg_ref[0].astype(BF16)
            wu_s[...] = wu_ref[0].astype(BF16)
            wd_s[...] = wd_ref[0].astype(BF16)

        x_copy(i, slot).wait()
        xa, xb = _unpack_halves(xbuf[slot])
        xa = xa.astype(BF16)
        xb = xb.astype(BF16)
        half = xa.shape[1]
        g = _dot(xa, wg_s[0:half]) + _dot(xb, wg_s[half:])
        u = _dot(xa, wu_s[0:half]) + _dot(xb, wu_s[half:])
        y = _dot((_silu(g) * u).astype(BF16), wd_s[...])
        ybuf[slot] = _pack_halves(y[:, :half], y[:, half:])
        y_copies(i, slot, lambda cp: cp.start())

        @pl.when(i > 0)
        def _():
            y_copies(prev, 1 - slot, lambda cp: cp.wait())

        @pl.when(i == n_items - 1)
        def _():
            y_copies(i, slot, lambda cp: cp.wait())


def _experts(items, xs, w_gate, w_up, w_down, *, n_blocks):
    half = xs.shape[1]
    e, d, f = w_gate.shape
    rows = ITEM_BLOCKS * MOE_BLOCK

    def w_idx(i, items_ref):
        return (items_ref[0, jnp.minimum(i, items_ref[3, 0] - 1)], 0, 0)

    return pl.pallas_call(
        _experts_kernel,
        grid_spec=pltpu.PrefetchScalarGridSpec(
            num_scalar_prefetch=1,
            grid=(_max_items(n_blocks),),
            in_specs=[pl.BlockSpec(memory_space=pl.ANY),
                      pl.BlockSpec((1, d, f), w_idx),
                      pl.BlockSpec((1, d, f), w_idx),
                      pl.BlockSpec((1, f, d), w_idx)],
            out_specs=pl.BlockSpec(memory_space=pl.ANY),
            scratch_shapes=[pltpu.VMEM((2, rows, half), U32), pltpu.VMEM((2, rows, half), U32),
                            pltpu.VMEM((d, f), BF16), pltpu.VMEM((d, f), BF16), pltpu.VMEM((f, d), BF16),
                            pltpu.SemaphoreType.DMA((2,)), pltpu.SemaphoreType.DMA((2,))]),
        out_shape=jax.ShapeDtypeStruct((n_blocks * MOE_BLOCK, half), U32),
        compiler_params=_cparams("arbitrary"),
        name="experts",
    )(items, xs, w_gate, w_up, w_down)


def _combine_kernel(hp_ref, x1_ref, g2_ref, w_ref, sgw_ref, suw_ref, sdw_ref, dest_hbm, ys_hbm, o_ref,
                    dest_s, buf_s, sem_d, sem_r, *, tb):
    step = pl.program_id(0)
    slot = step % 2
    n_assign = TOP_K * tb

    def fetch(s, b):
        cp = pltpu.make_async_copy(dest_hbm.at[pl.ds(s * n_assign, n_assign)], dest_s.at[b], sem_d)
        cp.start()
        cp.wait()

        def issue(t, carry):
            for k in range(TOP_K):
                d = dest_s[b, k * tb + t]
                pltpu.make_async_copy(ys_hbm.at[pl.ds(d, 1)], buf_s.at[b, k, pl.ds(t, 1)],
                                      sem_r.at[b]).start(priority=k % 2)
            return carry

        lax.fori_loop(0, tb, issue, 0)

    @pl.when(step == 0)
    def _():
        fetch(0, 0)

    @pl.when(step + 1 < pl.num_programs(0))
    def _():
        fetch(step + 1, 1 - slot)

    xa, xb = _unpack_halves(hp_ref[...])
    xa = xa.astype(BF16)
    xb = xb.astype(BF16)
    half = xa.shape[1]
    g = _dot(xa, sgw_ref[0:half]) + _dot(xb, sgw_ref[half:])
    u = _dot(xa, suw_ref[0:half]) + _dot(xb, suw_ref[half:])
    shared = _dot((_silu(g) * u).astype(BF16), sdw_ref[...])

    for k in range(TOP_K):
        pltpu.make_async_copy(ys_hbm.at[pl.ds(0, tb)], buf_s.at[slot, k], sem_r.at[slot]).wait()

    w = w_ref[...]
    acc_a = jnp.zeros((tb, half), F32)
    acc_b = jnp.zeros((tb, half), F32)
    for k in range(TOP_K):
        ya, yb = _unpack_halves(buf_s[slot, k])
        acc_a += ya * w[:, k:k + 1]
        acc_b += yb * w[:, k:k + 1]
    g2 = g2_ref[0]
    o_ref[:, 0:half] = x1_ref[:, 0:half] + g2[:, 0:half] * (acc_a + shared[:, 0:half])
    o_ref[:, half:] = x1_ref[:, half:] + g2[:, half:] * (acc_b + shared[:, half:])


def _combine(hp, x1, g2, w_tok, sgw, suw, sdw, dest_flat, ys, *, tb, seq_len):
    t, half = hp.shape
    d = 2 * half
    tb = min(tb, seq_len)
    per_seq = seq_len // tb
    f = sgw.shape[1]
    return pl.pallas_call(
        functools.partial(_combine_kernel, tb=tb),
        grid=(t // tb,),
        in_specs=[pl.BlockSpec((tb, half), lambda i: (i, 0)),
                  pl.BlockSpec((tb, d), lambda i: (i, 0)),
                  pl.BlockSpec((1, 1, d), lambda i: (i // per_seq, 0, 0)),
                  pl.BlockSpec((tb, TOP_K), lambda i: (i, 0)),
                  pl.BlockSpec((d, f), lambda i: (0, 0)),
                  pl.BlockSpec((d, f), lambda i: (0, 0)),
                  pl.BlockSpec((f, d), lambda i: (0, 0)),
                  pl.BlockSpec(memory_space=pl.ANY),
                  pl.BlockSpec(memory_space=pl.ANY)],
        out_specs=pl.BlockSpec((tb, d), lambda i: (i, 0)),
        out_shape=jax.ShapeDtypeStruct((t, d), F32),
        scratch_shapes=[pltpu.SMEM((2, TOP_K * tb), I32), pltpu.VMEM((2, TOP_K, tb, half), U32),
                        pltpu.SemaphoreType.DMA, pltpu.SemaphoreType.DMA((2,))],
        compiler_params=_cparams("arbitrary"),
        name="combine",
    )(hp, x1, g2, w_tok, sgw, suw, sdw, dest_flat, ys)


def _rope_tables(l):
    rows = l // GRID_W
    r = jnp.repeat(jnp.arange(rows), GRID_W).astype(F32)
    col = jnp.tile(jnp.arange(GRID_W), rows).astype(F32)
    n_f = HEAD_DIM // 4
    freqs = ROPE_THETA ** (-jnp.arange(n_f, dtype=F32) / n_f)
    ang = jnp.concatenate([r[:, None] * freqs, col[:, None] * freqs], axis=-1)
    ang = jnp.tile(jnp.repeat(ang, 2, axis=1), (1, LANES // HEAD_DIM))
    sign = jnp.where(jnp.arange(LANES) % 2 == 0, -1.0, 1.0).astype(F32)
    return jnp.cos(ang), jnp.sin(ang) * sign


def kernel(x, c, ctx, c_ctx, w_mod, b_mod, norm1_w, norm2_w, w_in, q_norm_w, k_norm_w, ret_decay_fwd,
           ret_decay_bwd, w_out, router_w, router_bias, exp_w_gate, exp_w_up, exp_w_down, sh_w_gate,
           sh_w_up, sh_w_down):
    b, l, d = x.shape
    lc = ctx.shape[1]
    t = b * l
    assert w_mod.shape[0] == 1, "single layer"
    assert l % CHUNK == 0 and lc % CHUNK == 0 and l % GRID_W == 0

    rows = (b + 1 + 7) // 8 * 8
    cc = jnp.zeros((rows, d), F32).at[:b].set(c).at[b].set(c_ctx)
    mod = _modulation(cc, w_mod[0], b_mod[0])
    sh1, sc1, g1, sh2, sc2, g2 = [mod[:b, i * d:(i + 1) * d].reshape(b, 1, d) for i in range(6)]
    shc = mod[b, 0:d].reshape(1, 1, d)
    scc = mod[b, d:2 * d].reshape(1, 1, d)

    wi = w_in[0].astype(BF16)
    qnw = jnp.tile(q_norm_w[0], LANES // HEAD_DIM).reshape(1, LANES)
    knw = jnp.tile(k_norm_w[0], LANES // HEAD_DIM).reshape(1, LANES)
    cos, sin = _rope_tables(l)
    n1 = norm1_w[0].reshape(1, d)

    cklo, ckhi, cvlo, cvhi, crk, crv = _projection(
        ctx, shc, scc, n1, wi, qnw, knw, cos[:lc], sin[:lc], rope=False, with_q=False, tm=TILE_PROJ)
    klo, khi, vlo, vhi, rk, rv, q, rq, sg = _projection(
        x, sh1, sc1, n1, wi, qnw, knw, cos, sin, rope=True, with_q=True, tm=TILE_PROJ)

    attn = _attention(q, klo, khi, vlo, vhi, cklo, ckhi, cvlo, cvhi, tq=TILE_ATTN_Q)
    dec_f = jnp.repeat(ret_decay_fwd[0].astype(F32), HEAD_DIM).reshape(1, RET_W)
    dec_b = jnp.repeat(ret_decay_bwd[0].astype(F32), HEAD_DIM).reshape(1, RET_W)
    ret = _retention(rq, rk, rv, sg, crk, crv, dec_f, dec_b)

    wo = w_out[0].astype(BF16)
    r_hi, r_lo = _split(router_w[0].T)
    x1, hp, logits_t = _out_projection(attn, ret, x, wo[:ATTN_W], wo[ATTN_W:], g1, sh2, sc2,
                                       norm2_w[0].reshape(1, d), r_hi, r_lo, tm=TILE_OUT)

    idx_t, w_t, rank_t, cnt_col, cnt_row = _route(logits_t, router_bias[0].reshape(N_EXPERTS, 1), tb=TILE_TOKENS)
    n_blocks = -(-(t * TOP_K) // MOE_BLOCK) + N_EXPERTS
    tb = TILE_TOKENS
    dest, meta, items = _destinations(idx_t, rank_t, cnt_col, cnt_row, tb=tb, n_blocks=n_blocks)
    dest_flat = dest.reshape(-1)
    xs = _dispatch(meta, hp, dest_flat, tb=tb, n_blocks=n_blocks)
    ys = _experts(items, xs, exp_w_gate[0], exp_w_up[0], exp_w_down[0], n_blocks=n_blocks)
    out = _combine(hp, x1.reshape(t, d), g2, w_t.T, sh_w_gate[0].astype(BF16), sh_w_up[0].astype(BF16),
                   sh_w_down[0].astype(BF16), dest_flat, ys, tb=tb, seq_len=l)
    return out.reshape(b, l, d)
```

```python
import functools

import jax
import jax.numpy as jnp
from jax import lax
from jax.experimental import pallas as pl
from jax.experimental.pallas import tpu as pltpu
from jax.experimental.pallas import tpu_sc as plsc

F32 = jnp.float32
BF16 = jnp.bfloat16
I32 = jnp.int32
U32 = jnp.uint32

HEAD_DIM = 64
LANES = 128
SUBLANES = 8
ATTN_HEADS = 8
ATTN_KV_HEADS = 2
GQA = ATTN_HEADS // ATTN_KV_HEADS
RET_HEADS = 8
ATTN_W = ATTN_HEADS * HEAD_DIM
KV_W = ATTN_KV_HEADS * HEAD_DIM
RET_W = RET_HEADS * HEAD_DIM
RET_PAIRS = RET_W // LANES
CHUNK = 128
GRID_W = 64
ROPE_THETA = 10000.0
N_EXPERTS = 256
TOP_K = 8
N_GROUPS = 8
GROUP_SIZE = N_EXPERTS // N_GROUPS
TOPK_GROUPS = 4
ROUTED_SCALE = 2.5
MOE_BLOCK = 128
EPS = 1e-6
QK_SCALE = HEAD_DIM ** -0.5

OFF_AK = 0
OFF_AV = OFF_AK + KV_W
OFF_RK = OFF_AV + KV_W
OFF_RV = OFF_RK + RET_W
CTX_KV_COLS = OFF_RV + RET_W
OFF_AQ = CTX_KV_COLS
OFF_RQ = OFF_AQ + ATTN_W
OFF_RG = OFF_RQ + RET_W
IN_COLS = OFF_RG + RET_W

VMEM_LIMIT = 52 * 1024 * 1024

TILE_PROJ = 256
TILE_ATTN_Q = 256
TILE_OUT = 256
TILE_TOKENS = 256
ITEM_BLOCKS = 5
HI_MASK = 0xFFFF0000


def _cparams(*sem):
    return pltpu.CompilerParams(dimension_semantics=sem, vmem_limit_bytes=VMEM_LIMIT)


def _split(a):
    hi = a.astype(BF16)
    lo = (a - hi.astype(F32)).astype(BF16)
    return hi, lo


def _dot(a, b):
    return jnp.dot(a, b, preferred_element_type=F32)


def _dot_nt(a, b):
    return lax.dot_general(a, b, (((1,), (1,)), ((), ())), preferred_element_type=F32)


def _sigmoid(v):
    return 1.0 / (1.0 + jnp.exp(-v))


def _silu(v):
    return v * _sigmoid(v)


def _pack_halves(a, b):
    ua = lax.bitcast_convert_type(a.astype(BF16).astype(F32), U32)
    ub = lax.bitcast_convert_type(b.astype(BF16).astype(F32), U32)
    return (ua & jnp.uint32(HI_MASK)) | (ub >> 16)


def _unpack_halves(u):
    a = lax.bitcast_convert_type(u & jnp.uint32(HI_MASK), F32)
    b = lax.bitcast_convert_type(u << 16, F32)
    return a, b


def _mod_kernel(c_ref, w_ref, b_ref, o_ref):
    s_hi, s_lo = _split(_silu(c_ref[...]))
    w_hi, w_lo = _split(w_ref[...])
    o_ref[...] = _dot(s_hi, w_hi) + _dot(s_hi, w_lo) + _dot(s_lo, w_hi) + b_ref[...]


def _modulation(cc, w_mod, b_mod):
    rows, d = cc.shape
    n = w_mod.shape[1]
    tn = 768
    return pl.pallas_call(
        _mod_kernel,
        grid=(n // tn,),
        in_specs=[pl.BlockSpec((rows, d), lambda j: (0, 0)),
                  pl.BlockSpec((d, tn), lambda j: (0, j)),
                  pl.BlockSpec((1, tn), lambda j: (0, j))],
        out_specs=pl.BlockSpec((rows, tn), lambda j: (0, j)),
        out_shape=jax.ShapeDtypeStruct((rows, n), F32),
        compiler_params=_cparams("arbitrary"),
        name="mod",
    )(cc, w_mod, b_mod.reshape(1, n))


def _segment_ones():
    r = lax.broadcasted_iota(I32, (LANES, LANES), 0) // HEAD_DIM
    c = lax.broadcasted_iota(I32, (LANES, LANES), 1) // HEAD_DIM
    return jnp.where(r == c, 1.0, 0.0).astype(BF16)


def _head_mean_sq(v, seg):
    hi, lo = _split(v * v)
    return (_dot(hi, seg) + _dot(lo, seg)) * (1.0 / HEAD_DIM)


def _proj_kernel(x_ref, sh_ref, sc_ref, nw_ref, wi_ref, qnw_ref, knw_ref, cos_ref, sin_ref,
                 *out_refs, rope, with_q):
    if with_q:
        klo_ref, khi_ref, vlo_ref, vhi_ref, rk_ref, rv_ref, q_ref, rq_ref, sg_ref = out_refs
    else:
        klo_ref, khi_ref, vlo_ref, vhi_ref, rk_ref, rv_ref = out_refs
    x = x_ref[0]
    h = x * lax.rsqrt(jnp.mean(x * x, axis=-1, keepdims=True) + EPS) * nw_ref[...]
    h = h * (1.0 + sc_ref[0]) + sh_ref[0]
    z = _dot(h.astype(BF16), wi_ref[...])

    seg = _segment_ones()
    lane = lax.broadcasted_iota(I32, (x.shape[0], LANES), 1)
    low_half = lane < HEAD_DIM
    even = (lane & 1) == 0

    def norm_rope(v, w128):
        v = v * lax.rsqrt(_head_mean_sq(v, seg) + EPS) * w128
        if rope:
            swapped = jnp.where(even, pltpu.roll(v, LANES - 1, 1), pltpu.roll(v, 1, 1))
            v = v * cos_ref[...] + swapped * sin_ref[...]
        return v

    def write_lo_hi(v, lo_ref, hi_ref):
        sw = pltpu.roll(v, HEAD_DIM, 1)
        lo_ref[0, 0] = jnp.where(low_half, v, 0.0).astype(BF16)
        hi_ref[0, 0] = jnp.where(low_half, 0.0, sw).astype(BF16)
        lo_ref[0, 1] = jnp.where(low_half, sw, 0.0).astype(BF16)
        hi_ref[0, 1] = jnp.where(low_half, 0.0, v).astype(BF16)

    write_lo_hi(norm_rope(z[:, OFF_AK:OFF_AK + KV_W], knw_ref[...]), klo_ref, khi_ref)
    write_lo_hi(z[:, OFF_AV:OFF_AV + KV_W], vlo_ref, vhi_ref)
    rk_ref[0] = (z[:, OFF_RK:OFF_RK + RET_W] * QK_SCALE).astype(BF16)
    rv_ref[0] = z[:, OFF_RV:OFF_RV + RET_W].astype(BF16)
    if with_q:
        for j in range(ATTN_W // LANES):
            qj = norm_rope(z[:, OFF_AQ + j * LANES:OFF_AQ + (j + 1) * LANES], qnw_ref[...])
            q_ref[0, :, j * LANES:(j + 1) * LANES] = (qj * QK_SCALE).astype(BF16)
        rq_ref[0] = z[:, OFF_RQ:OFF_RQ + RET_W].astype(BF16)
        sg_ref[0] = _silu(z[:, OFF_RG:OFF_RG + RET_W]).astype(BF16)


def _projection(x, shift, scale, norm_w, wi_bf16, qnw, knw, cos, sin, *, rope, with_q, tm):
    b, l, d = x.shape
    tm = min(tm, l)
    ncols = IN_COLS if with_q else CTX_KV_COLS
    per_batch = shift.shape[0] > 1
    mod_idx = (lambda bi, i: (bi, 0, 0)) if per_batch else (lambda bi, i: (0, 0, 0))
    kv_shape = jax.ShapeDtypeStruct((b, ATTN_KV_HEADS, l, LANES), BF16)
    kv_spec = pl.BlockSpec((1, ATTN_KV_HEADS, tm, LANES), lambda bi, i: (bi, 0, i, 0))
    w_shape = jax.ShapeDtypeStruct((b, l, RET_W), BF16)
    w_spec = pl.BlockSpec((1, tm, RET_W), lambda bi, i: (bi, i, 0))
    out_shape = [kv_shape] * 4 + [w_shape] * 2
    out_specs = [kv_spec] * 4 + [w_spec] * 2
    if with_q:
        out_shape += [w_shape] * 3
        out_specs += [w_spec] * 3
    return pl.pallas_call(
        functools.partial(_proj_kernel, rope=rope, with_q=with_q),
        grid=(b, l // tm),
        in_specs=[pl.BlockSpec((1, tm, d), lambda bi, i: (bi, i, 0)),
                  pl.BlockSpec((1, 1, d), mod_idx),
                  pl.BlockSpec((1, 1, d), mod_idx),
                  pl.BlockSpec((1, d), lambda bi, i: (0, 0)),
                  pl.BlockSpec((d, ncols), lambda bi, i: (0, 0)),
                  pl.BlockSpec((1, LANES), lambda bi, i: (0, 0)),
                  pl.BlockSpec((1, LANES), lambda bi, i: (0, 0)),
                  pl.BlockSpec((tm, LANES), lambda bi, i: (i, 0)),
                  pl.BlockSpec((tm, LANES), lambda bi, i: (i, 0))],
        out_specs=out_specs,
        out_shape=out_shape,
        compiler_params=_cparams("arbitrary", "arbitrary"),
        name="proj_latent" if with_q else "proj_ctx",
    )(x, shift, scale, norm_w, wi_bf16, qnw, knw, cos, sin)


def _attn_kernel(q_ref, klo_ref, khi_ref, vlo_ref, vhi_ref, cklo_ref, ckhi_ref, cvlo_ref, cvhi_ref,
                 o_ref, kl_s, kh_s, va_s, *, l, lc):
    lk = l + lc

    @pl.when(pl.program_id(2) == 0)
    def _():
        kl_s[0:l] = klo_ref[0, 0]
        kl_s[l:lk] = cklo_ref[0, 0]
        kh_s[0:l] = khi_ref[0, 0]
        kh_s[l:lk] = ckhi_ref[0, 0]
        lane = lax.broadcasted_iota(I32, (lk, LANES), 1)
        ones_lo = jnp.where(lane < HEAD_DIM, 1.0, 0.0).astype(BF16)
        ones_hi = jnp.where(lane < HEAD_DIM, 0.0, 1.0).astype(BF16)
        for g, (v_ref, cv_ref, ones) in enumerate(((vlo_ref, cvlo_ref, ones_lo), (vhi_ref, cvhi_ref, ones_hi),
                                                   (vlo_ref, cvlo_ref, ones_lo), (vhi_ref, cvhi_ref, ones_hi))):
            v_col, one_col = (0, LANES) if g < 2 else (LANES, 0)
            va_s[g, 0:l, v_col:v_col + LANES] = v_ref[0, 0]
            va_s[g, l:lk, v_col:v_col + LANES] = cv_ref[0, 0]
            va_s[g, :, one_col:one_col + LANES] = ones

    q = q_ref[0]
    acc = []
    for g in range(GQA):
        qp = q[:, (g // 2) * LANES:(g // 2 + 1) * LANES]
        kk = kl_s[...] if g % 2 == 0 else kh_s[...]
        s = _dot_nt(qp, kk)
        p = jnp.exp(s - jnp.max(s, axis=-1, keepdims=True)).astype(BF16)
        acc.append(_dot(p, va_s[g]))
    out_a = acc[0] + acc[1]
    out_b = acc[2] + acc[3]
    o_ref[0, :, 0:LANES] = (out_a[:, 0:LANES] / out_a[:, LANES:2 * LANES]).astype(BF16)
    o_ref[0, :, LANES:2 * LANES] = (out_b[:, LANES:2 * LANES] / out_b[:, 0:LANES]).astype(BF16)


def _attention(q, klo, khi, vlo, vhi, cklo, ckhi, cvlo, cvhi, *, tq):
    b, l, _ = q.shape
    lc = cklo.shape[2]
    lk = l + lc
    tq = min(tq, l)
    gw = GQA * HEAD_DIM
    kv_spec = pl.BlockSpec((1, 1, l, LANES), lambda bi, h, i: (bi, h, 0, 0))
    ckv_spec = pl.BlockSpec((1, 1, lc, LANES), lambda bi, h, i: (bi, h, 0, 0))
    return pl.pallas_call(
        functools.partial(_attn_kernel, l=l, lc=lc),
        grid=(b, ATTN_KV_HEADS, l // tq),
        in_specs=[pl.BlockSpec((1, tq, gw), lambda bi, h, i: (bi, i, h))] + [kv_spec] * 4 + [ckv_spec] * 4,
        out_specs=pl.BlockSpec((1, tq, gw), lambda bi, h, i: (bi, i, h)),
        out_shape=jax.ShapeDtypeStruct((b, l, ATTN_W), BF16),
        scratch_shapes=[pltpu.VMEM((lk, LANES), BF16), pltpu.VMEM((lk, LANES), BF16),
                        pltpu.VMEM((GQA, lk, 2 * LANES), BF16)],
        compiler_params=_cparams("arbitrary", "arbitrary", "arbitrary"),
        name="attn",
    )(q, klo, khi, vlo, vhi, cklo, ckhi, cvlo, cvhi)


def _log_sigmoid(v):
    return jnp.minimum(v, 0.0) - jnp.log(1.0 + jnp.exp(-jnp.abs(v)))


def _ret_kernel(rq_ref, rk_ref, rv_ref, sg_ref, crk_ref, crv_ref, df_ref, db_ref, o_ref,
                m_s, tab_s, sb_s, sf_s, *, l, lc):
    n = l // CHUNK
    nc = lc // CHUNK
    lgf = _log_sigmoid(df_ref[...])
    lgb = _log_sigmoid(db_ref[...])
    pos = lax.broadcasted_iota(I32, (CHUNK, RET_W), 0).astype(F32)
    tab_s[0] = jnp.exp((pos + 1.0) * lgf)
    tab_s[1] = jnp.exp((CHUNK - pos) * lgb)
    tab_s[2] = jnp.exp((CHUNK - 1.0 - pos) * lgf)
    tab_s[3] = jnp.exp(pos * lgb)
    gf_c = jnp.exp(CHUNK * lgf)
    gb_c = jnp.exp(CHUNK * lgb)
    row = lax.broadcasted_iota(I32, (CHUNK, CHUNK), 0)
    col = lax.broadcasted_iota(I32, (CHUNK, CHUNK), 1)
    diff = (row - col).astype(F32)
    for h in range(RET_HEADS):
        lf = lgf[:, h * HEAD_DIM:h * HEAD_DIM + 1]
        lb = lgb[:, h * HEAD_DIM:h * HEAD_DIM + 1]
        m_s[h] = jnp.where(diff > 0, jnp.exp(diff * lf), jnp.where(diff < 0, jnp.exp(-diff * lb), 2.0))

    lane = lax.broadcasted_iota(I32, (CHUNK, LANES), 1)
    low_half = lane < HEAD_DIM
    diag = (lax.broadcasted_iota(I32, (LANES, LANES), 0) // HEAD_DIM
            == lax.broadcasted_iota(I32, (LANES, LANES), 1) // HEAD_DIM)
    seg = jnp.where(diag, 1.0, 0.0).astype(BF16)

    def contrib(k_ref, v_ref, r0, p, zeta_idx):
        cols = slice(p * LANES, (p + 1) * LANES)
        kz = k_ref[0, pl.ds(r0, CHUNK), cols].astype(F32) * tab_s[zeta_idx, :, cols]
        kv = _dot(jnp.transpose(kz).astype(BF16), v_ref[0, pl.ds(r0, CHUNK), cols])
        return jnp.where(diag, kv, 0.0)

    for p in range(RET_PAIRS):
        cols = slice(p * LANES, (p + 1) * LANES)
        sf = jnp.zeros((LANES, LANES), F32)
        sb = jnp.zeros((LANES, LANES), F32)
        for c in range(nc):
            sf = gf_c[:, cols] * sf + contrib(crk_ref, crv_ref, c * CHUNK, p, 2)
            cb = nc - 1 - c
            sb = gb_c[:, cols] * sb + contrib(crk_ref, crv_ref, cb * CHUNK, p, 3)
        sf_s[p] = sf
        sb_s[n, p] = sb

    def bwd_body(j, carry):
        c = n - 1 - j
        r0 = pl.multiple_of(c * CHUNK, CHUNK)
        for p in range(RET_PAIRS):
            cols = slice(p * LANES, (p + 1) * LANES)
            prev = sb_s[c + 1, p]
            sb_s[c, p] = gb_c[:, cols] * prev + contrib(rk_ref, rv_ref, r0, p, 3)
        return carry

    lax.fori_loop(0, n, bwd_body, 0)

    def fwd_body(c, carry):
        r0 = pl.multiple_of(c * CHUNK, CHUNK)
        for p in range(RET_PAIRS):
            cols = slice(p * LANES, (p + 1) * LANES)
            qp = rq_ref[0, pl.ds(r0, CHUNK), cols]
            kp = rk_ref[0, pl.ds(r0, CHUNK), cols]
            vp = rv_ref[0, pl.ds(r0, CHUNK), cols]
            zero = jnp.zeros_like(kp)
            s0 = _dot_nt(qp, jnp.where(low_half, kp, zero))
            s1 = _dot_nt(qp, jnp.where(low_half, zero, kp))
            a0 = (s0 * m_s[2 * p]).astype(BF16)
            a1 = (s1 * m_s[2 * p + 1]).astype(BF16)
            y = _dot(a0, jnp.where(low_half, vp, zero)) + _dot(a1, jnp.where(low_half, zero, vp))
            qf = qp.astype(F32)
            sf = sf_s[p]
            y += _dot((qf * tab_s[0, :, cols]).astype(BF16), sf.astype(BF16))
            y += _dot((qf * tab_s[1, :, cols]).astype(BF16), sb_s[c + 1, p].astype(BF16))
            hi, lo = _split(y * y)
            ms = (_dot(hi, seg) + _dot(lo, seg)) * (1.0 / HEAD_DIM)
            out = y * lax.rsqrt(ms + EPS) * sg_ref[0, pl.ds(r0, CHUNK), cols].astype(F32)
            o_ref[0, pl.ds(r0, CHUNK), cols] = out.astype(BF16)
            sf_s[p] = gf_c[:, cols] * sf + contrib(rk_ref, rv_ref, r0, p, 2)
        return carry

    lax.fori_loop(0, n, fwd_body, 0)


def _retention(rq, rk, rv, sg, crk, crv, dec_f, dec_b):
    b, l, _ = rq.shape
    lc = crk.shape[1]
    n = l // CHUNK
    spec = pl.BlockSpec((1, l, RET_W), lambda bi: (bi, 0, 0))
    cspec = pl.BlockSpec((1, lc, RET_W), lambda bi: (bi, 0, 0))
    dspec = pl.BlockSpec((1, RET_W), lambda bi: (0, 0))
    return pl.pallas_call(
        functools.partial(_ret_kernel, l=l, lc=lc),
        grid=(b,),
        in_specs=[spec, spec, spec, spec, cspec, cspec, dspec, dspec],
        out_specs=spec,
        out_shape=jax.ShapeDtypeStruct((b, l, RET_W), BF16),
        scratch_shapes=[pltpu.VMEM((RET_HEADS, CHUNK, CHUNK), F32),
                        pltpu.VMEM((4, CHUNK, RET_W), F32),
                        pltpu.VMEM((n + 1, RET_PAIRS, LANES, LANES), F32),
                        pltpu.VMEM((RET_PAIRS, LANES, LANES), F32)],
        compiler_params=_cparams("arbitrary"),
        name="ret",
    )(rq, rk, rv, sg, crk, crv, dec_f, dec_b)


def _out_kernel(attn_ref, ret_ref, x_ref, wa_ref, wr_ref, g1_ref, sh_ref, sc_ref, nw_ref, rhi_ref, rlo_ref,
                x1_ref, hp_ref, lg_ref):
    y = _dot(attn_ref[0], wa_ref[...]) + _dot(ret_ref[0], wr_ref[...])
    x1 = x_ref[0] + g1_ref[0] * y
    x1_ref[0] = x1
    h = x1 * lax.rsqrt(jnp.mean(x1 * x1, axis=-1, keepdims=True) + EPS) * nw_ref[...]
    h = h * (1.0 + sc_ref[0]) + sh_ref[0]
    half = h.shape[1] // 2
    hp_ref[...] = _pack_halves(h[:, :half], h[:, half:])
    h_hi, h_lo = _split(h)
    lg_ref[...] = _dot_nt(rhi_ref[...], h_hi) + _dot_nt(rhi_ref[...], h_lo) + _dot_nt(rlo_ref[...], h_hi)


def _out_projection(attn, ret, x, wa, wr, g1, sh2, sc2, norm_w, r_hi, r_lo, *, tm):
    b, l, d = x.shape
    tm = min(tm, l)
    nt = l // tm
    t = b * l
    mspec = pl.BlockSpec((1, 1, d), lambda bi, i: (bi, 0, 0))
    return pl.pallas_call(
        _out_kernel,
        grid=(b, nt),
        in_specs=[pl.BlockSpec((1, tm, ATTN_W), lambda bi, i: (bi, i, 0)),
                  pl.BlockSpec((1, tm, RET_W), lambda bi, i: (bi, i, 0)),
                  pl.BlockSpec((1, tm, d), lambda bi, i: (bi, i, 0)),
                  pl.BlockSpec((ATTN_W, d), lambda bi, i: (0, 0)),
                  pl.BlockSpec((RET_W, d), lambda bi, i: (0, 0)),
                  mspec, mspec, mspec,
                  pl.BlockSpec((1, d), lambda bi, i: (0, 0)),
                  pl.BlockSpec((N_EXPERTS, d), lambda bi, i: (0, 0)),
                  pl.BlockSpec((N_EXPERTS, d), lambda bi, i: (0, 0))],
        out_specs=[pl.BlockSpec((1, tm, d), lambda bi, i: (bi, i, 0)),
                   pl.BlockSpec((tm, d // 2), lambda bi, i: (bi * nt + i, 0)),
                   pl.BlockSpec((N_EXPERTS, tm), lambda bi, i: (0, bi * nt + i))],
        out_shape=[jax.ShapeDtypeStruct((b, l, d), F32),
                   jax.ShapeDtypeStruct((t, d // 2), U32),
                   jax.ShapeDtypeStruct((N_EXPERTS, t), F32)],
        compiler_params=_cparams("arbitrary", "arbitrary"),
        name="out_proj",
    )(attn, ret, x, wa, wr, g1, sh2, sc2, norm_w, r_hi, r_lo)


def _route_kernel(lg_ref, bias_ref, idx_ref, w_ref, rank_ref, cnt_col_ref, cnt_row_ref, tri_s, col_s, row_s):
    tb = lg_ref.shape[1]
    step = pl.program_id(0)

    @pl.when(step == 0)
    def _():
        r = lax.broadcasted_iota(I32, (tb, tb), 0)
        c = lax.broadcasted_iota(I32, (tb, tb), 1)
        tri_s[...] = jnp.where(r <= c, 1.0, 0.0).astype(BF16)
        col_s[...] = jnp.zeros_like(col_s)
        row_s[...] = jnp.zeros_like(row_s)

    scores = _sigmoid(lg_ref[...])
    biased = scores + bias_ref[...]
    neg = -jnp.inf
    sub = lax.broadcasted_iota(I32, (GROUP_SIZE, tb), 0).astype(F32)

    gscore = []
    for g in range(N_GROUPS):
        blk = biased[g * GROUP_SIZE:(g + 1) * GROUP_SIZE]
        m1 = jnp.max(blk, axis=0, keepdims=True)
        first = jnp.min(jnp.where(blk == m1, sub, float(GROUP_SIZE)), axis=0, keepdims=True)
        m2 = jnp.max(jnp.where(sub == first, neg, blk), axis=0, keepdims=True)
        gscore.append(m1 + m2)
    gs = jnp.concatenate(gscore, axis=0)
    gsub = lax.broadcasted_iota(I32, (N_GROUPS, tb), 0).astype(F32)
    keep = jnp.zeros((N_GROUPS, tb), F32)
    for _ in range(TOPK_GROUPS):
        m = jnp.max(gs, axis=0, keepdims=True)
        first = jnp.min(jnp.where(gs == m, gsub, float(N_GROUPS)), axis=0, keepdims=True)
        sel = gsub == first
        keep = jnp.where(sel, 1.0, keep)
        gs = jnp.where(sel, neg, gs)
    masked = jnp.concatenate(
        [jnp.where(keep[g:g + 1] > 0.0, biased[g * GROUP_SIZE:(g + 1) * GROUP_SIZE], neg)
         for g in range(N_GROUPS)], axis=0)

    esub = lax.broadcasted_iota(I32, (N_EXPERTS, tb), 0).astype(F32)
    sels, idxs, ws = [], [], []
    chosen = jnp.zeros((N_EXPERTS, tb), F32)
    for _ in range(TOP_K):
        m = jnp.max(masked, axis=0, keepdims=True)
        first = jnp.min(jnp.where(masked == m, esub, float(N_EXPERTS)), axis=0, keepdims=True)
        sel = esub == first
        sels.append(sel)
        idxs.append(first)
        ws.append(jnp.sum(jnp.where(sel, scores, 0.0), axis=0, keepdims=True))
        chosen = jnp.where(sel, 1.0, chosen)
        masked = jnp.where(sel, neg, masked)
    wsum = ws[0]
    for k in range(1, TOP_K):
        wsum = wsum + ws[k]
    idx_ref[...] = jnp.concatenate(idxs, axis=0).astype(I32)
    w_ref[...] = jnp.concatenate([wk / wsum * ROUTED_SCALE for wk in ws], axis=0)

    chosen_b = chosen.astype(BF16)
    incl = _dot(chosen_b, tri_s[...])
    before = incl - chosen + col_s[...]
    rank_ref[...] = jnp.concatenate(
        [jnp.sum(jnp.where(sel, before, 0.0), axis=0, keepdims=True) for sel in sels], axis=0).astype(I32)
    col_s[...] = col_s[...] + incl[:, tb - 1:tb]
    row_s[...] = row_s[...] + _dot_nt(jnp.ones((8, tb), BF16), chosen_b)
    cnt_col_ref[...] = col_s[...].astype(I32)
    cnt_row_ref[...] = row_s[...].astype(I32)


def _route(logits_t, bias_col, *, tb):
    e, t = logits_t.shape
    tb = min(tb, t)
    kspec = pl.BlockSpec((TOP_K, tb), lambda i: (0, i))
    return pl.pallas_call(
        _route_kernel,
        grid=(t // tb,),
        in_specs=[pl.BlockSpec((e, tb), lambda i: (0, i)),
                  pl.BlockSpec((e, 1), lambda i: (0, 0))],
        out_specs=[kspec, kspec, kspec,
                   pl.BlockSpec((e, 1), lambda i: (0, 0)),
                   pl.BlockSpec((8, e), lambda i: (0, 0))],
        out_shape=[jax.ShapeDtypeStruct((TOP_K, t), I32),
                   jax.ShapeDtypeStruct((TOP_K, t), F32),
                   jax.ShapeDtypeStruct((TOP_K, t), I32),
                   jax.ShapeDtypeStruct((e, 1), I32),
                   jax.ShapeDtypeStruct((8, e), I32)],
        scratch_shapes=[pltpu.VMEM((tb, tb), BF16), pltpu.VMEM((e, 1), F32), pltpu.VMEM((8, e), F32)],
        compiler_params=_cparams("arbitrary"),
        name="route",
    )(logits_t, bias_col)


def _pad_block(cnt):
    return (cnt + (MOE_BLOCK - 1)) // MOE_BLOCK * MOE_BLOCK


def _max_items(n_blocks):
    return n_blocks // ITEM_BLOCKS + N_EXPERTS


def _dest_kernel(idx_ref, rank_ref, cnt_col_ref, cnt_row_ref, dest_ref, meta_ref, items_ref):
    tb = idx_ref.shape[1]
    nip = items_ref.shape[1]
    pad_col = _pad_block(cnt_col_ref[...])
    pad_row = _pad_block(cnt_row_ref[0:1, :])
    er = lax.broadcasted_iota(I32, (N_EXPERTS, N_EXPERTS), 0)
    ec = lax.broadcasted_iota(I32, (N_EXPERTS, N_EXPERTS), 1)
    start_col = jnp.sum(jnp.where(ec < er, pad_row, 0), axis=1, keepdims=True)
    start_row = jnp.sum(jnp.where(er < ec, pad_col, 0), axis=0, keepdims=True)

    esub = lax.broadcasted_iota(I32, (N_EXPERTS, tb), 0)
    rows = []
    for k in range(TOP_K):
        onehot = esub == idx_ref[k:k + 1, :]
        rows.append(jnp.sum(jnp.where(onehot, start_col, 0), axis=0, keepdims=True) + rank_ref[k:k + 1, :])
    dest_ref[0] = jnp.concatenate(rows, axis=0)

    used = jnp.sum(pad_row, axis=1, keepdims=True) // MOE_BLOCK
    meta_ref[...] = jnp.concatenate(
        [cnt_row_ref[0:1, :], start_row, pad_row, jnp.broadcast_to(used, (1, N_EXPERTS)),
         jnp.zeros((4, N_EXPERTS), I32)], axis=0)

    nb_col = pad_col // MOE_BLOCK
    it_col = (nb_col + (ITEM_BLOCKS - 1)) // ITEM_BLOCKS
    it_row = (pad_row // MOE_BLOCK + (ITEM_BLOCKS - 1)) // ITEM_BLOCKS
    it_start = jnp.sum(jnp.where(ec < er, it_row, 0), axis=1, keepdims=True)
    n_items = jnp.sum(it_row, axis=1, keepdims=True)
    lane = lax.broadcasted_iota(I32, (1, nip), 1)
    owner = jnp.sum(jnp.where(it_start + it_col <= lane, 1, 0), axis=0, keepdims=True)
    owner = jnp.minimum(owner, N_EXPERTS - 1)
    onehot = lax.broadcasted_iota(I32, (N_EXPERTS, nip), 0) == owner

    def pick(col):
        return jnp.sum(jnp.where(onehot, col, 0), axis=0, keepdims=True)

    j = lane - pick(it_start)
    block0 = pick(start_col) // MOE_BLOCK + ITEM_BLOCKS * j
    nvalid = jnp.clip(pick(nb_col) - ITEM_BLOCKS * j, 0, ITEM_BLOCKS)
    items_ref[...] = jnp.concatenate(
        [owner, block0, jnp.where(lane < n_items, nvalid, 0), jnp.broadcast_to(n_items, (1, nip)),
         jnp.zeros((4, nip), I32)], axis=0)


def _destinations(idx_t, rank_t, cnt_col, cnt_row, *, tb, n_blocks):
    _, t = idx_t.shape
    tb = min(tb, t)
    nip = (_max_items(n_blocks) + LANES - 1) // LANES * LANES
    kspec = pl.BlockSpec((TOP_K, tb), lambda i: (0, i))
    return pl.pallas_call(
        _dest_kernel,
        grid=(t // tb,),
        in_specs=[kspec, kspec,
                  pl.BlockSpec((N_EXPERTS, 1), lambda i: (0, 0)),
                  pl.BlockSpec((8, N_EXPERTS), lambda i: (0, 0))],
        out_specs=[pl.BlockSpec((1, TOP_K, tb), lambda i: (i, 0, 0)),
                   pl.BlockSpec((8, N_EXPERTS), lambda i: (0, 0)),
                   pl.BlockSpec((8, nip), lambda i: (0, 0))],
        out_shape=[jax.ShapeDtypeStruct((t // tb, TOP_K, tb), I32),
                   jax.ShapeDtypeStruct((8, N_EXPERTS), I32),
                   jax.ShapeDtypeStruct((8, nip), I32)],
        compiler_params=_cparams("arbitrary"),
        name="dest",
    )(idx_t, rank_t, cnt_col, cnt_row)


_PAD_BITS = (64, 32, 16, 8)


def _dispatch_kernel(meta_ref, hp_ref, dest_hbm, xs_hbm, dest_s, zero_s, sem_d, sem_r, sem_z, *, tb, e_per_step):
    step = pl.program_id(0)
    n_assign = TOP_K * tb
    cp = pltpu.make_async_copy(dest_hbm.at[pl.ds(step * n_assign, n_assign)], dest_s, sem_d)
    cp.start()
    zero_s[...] = jnp.zeros_like(zero_s)
    cp.wait()

    def row_copy(t, k):
        d = dest_s[k * tb + t]
        return pltpu.make_async_copy(hp_ref.at[pl.ds(t, 1)], xs_hbm.at[pl.ds(d, 1)], sem_r)

    def issue(t, carry):
        for k in range(TOP_K):
            row_copy(t, k).start(priority=k % 2)
        return carry

    lax.fori_loop(0, tb, issue, 0)

    def tail_copy(c):
        row0 = pl.multiple_of((meta_ref[3, 0] + c) * MOE_BLOCK, MOE_BLOCK)
        return pltpu.make_async_copy(zero_s, xs_hbm.at[pl.ds(row0, MOE_BLOCK)], sem_z)

    @pl.when(step == 0)
    def _():
        for c in range(ITEM_BLOCKS - 1):
            tail_copy(c).start()
        for c in range(ITEM_BLOCKS - 1):
            tail_copy(c).wait()

    def pad_copies(e):
        cnt = meta_ref[0, e]
        off = meta_ref[1, e] + cnt
        rem = meta_ref[2, e] - cnt
        head = rem & (SUBLANES - 1)
        out = []
        for i in range(SUBLANES - 1):
            out.append((i < head,
                        pltpu.make_async_copy(zero_s.at[pl.ds(0, 1)], xs_hbm.at[pl.ds(off + i, 1)], sem_z)))
        off = off + head
        for bit in _PAD_BITS:
            out.append(((rem & bit) != 0,
                        pltpu.make_async_copy(zero_s.at[pl.ds(0, bit)],
                                              xs_hbm.at[pl.ds(pl.multiple_of(off, SUBLANES), bit)], sem_z)))
            off = off + (rem & bit)
        return out

    for j in range(e_per_step):
        for cond, c in pad_copies(step * e_per_step + j):
            pl.when(cond)(c.start)
    for j in range(e_per_step):
        for cond, c in pad_copies(step * e_per_step + j):
            pl.when(cond)(c.wait)

    for k in range(TOP_K):
        pltpu.make_async_copy(hp_ref, xs_hbm.at[pl.ds(0, tb)], sem_r).wait()


def _dispatch(meta, hp, dest_flat, *, tb, n_blocks):
    t, half = hp.shape
    tb = min(tb, t)
    steps = t // tb
    e_per_step = max(N_EXPERTS // steps, 1)
    assert e_per_step * steps == N_EXPERTS, "token blocks must divide the expert count"
    return pl.pallas_call(
        functools.partial(_dispatch_kernel, tb=tb, e_per_step=e_per_step),
        grid=(steps,),
        in_specs=[pl.BlockSpec(memory_space=pltpu.SMEM),
                  pl.BlockSpec((tb, half), lambda i: (i, 0)),
                  pl.BlockSpec(memory_space=pl.ANY)],
        out_specs=pl.BlockSpec(memory_space=pl.ANY),
        out_shape=jax.ShapeDtypeStruct(((n_blocks + ITEM_BLOCKS - 1) * MOE_BLOCK, half), U32),
        scratch_shapes=[pltpu.SMEM((TOP_K * tb,), I32), pltpu.VMEM((MOE_BLOCK, half), U32),
                        pltpu.SemaphoreType.DMA, pltpu.SemaphoreType.DMA, pltpu.SemaphoreType.DMA],
        compiler_params=_cparams("arbitrary"),
        name="dispatch",
    )(meta, hp, dest_flat)


def _experts_kernel(items_ref, xs_hbm, wg_ref, wu_ref, wd_ref, ys_hbm,
                    xbuf, ybuf, wg_s, wu_s, wd_s, sem_x, sem_y):
    i = pl.program_id(0)
    n_items = items_ref[3, 0]
    slot = i % 2
    rows = ITEM_BLOCKS * MOE_BLOCK

    def x_copy(item, s):
        row0 = pl.multiple_of(items_ref[1, item] * MOE_BLOCK, MOE_BLOCK)
        return pltpu.make_async_copy(xs_hbm.at[pl.ds(row0, rows)], xbuf.at[s], sem_x.at[s])

    def y_copies(item, s, fn):
        for k in range(ITEM_BLOCKS):
            row0 = pl.multiple_of((items_ref[1, item] + k) * MOE_BLOCK, MOE_BLOCK)
            cp = pltpu.make_async_copy(ybuf.at[s, pl.ds(k * MOE_BLOCK, MOE_BLOCK)],
                                       ys_hbm.at[pl.ds(row0, MOE_BLOCK)], sem_y.at[s])
            pl.when(k < items_ref[2, item])(functools.partial(fn, cp))

    @pl.when(i == 0)
    def _():
        x_copy(0, 0).start()

    @pl.when(i + 1 < n_items)
    def _():
        x_copy(i + 1, 1 - slot).start()

    @pl.when(i < n_items)
    def _():
        prev = jnp.maximum(i - 1, 0)

        @pl.when((i == 0) | (items_ref[0, i] != items_ref[0, prev]))
        def _():
            wg_s[...] = wg_ref[0].astype(BF16)
            wu_s[...] = wu_ref[0].astype(BF16)
            wd_s[...] = wd_ref[0].astype(BF16)

        x_copy(i, slot).wait()
        xa, xb = _unpack_halves(xbuf[slot])
        xa = xa.astype(BF16)
        xb = xb.astype(BF16)
        half = xa.shape[1]
        g = _dot(xa, wg_s[0:half]) + _dot(xb, wg_s[half:])
        u = _dot(xa, wu_s[0:half]) + _dot(xb, wu_s[half:])
        y = _dot((_silu(g) * u).astype(BF16), wd_s[...])
        ybuf[slot] = _pack_halves(y[:, :half], y[:, half:])
        y_copies(i, slot, lambda cp: cp.start())

        @pl.when(i > 0)
        def _():
            y_copies(prev, 1 - slot, lambda cp: cp.wait())

        @pl.when(i == n_items - 1)
        def _():
            y_copies(i, slot, lambda cp: cp.wait())


def _experts(items, xs, w_gate, w_up, w_down, *, n_blocks):
    half = xs.shape[1]
    e, d, f = w_gate.shape
    rows = ITEM_BLOCKS * MOE_BLOCK

    def w_idx(i, items_ref):
        return (items_ref[0, jnp.minimum(i, items_ref[3, 0] - 1)], 0, 0)

    return pl.pallas_call(
        _experts_kernel,
        grid_spec=pltpu.PrefetchScalarGridSpec(
            num_scalar_prefetch=1,
            grid=(_max_items(n_blocks),),
            in_specs=[pl.BlockSpec(memory_space=pl.ANY),
                      pl.BlockSpec((1, d, f), w_idx),
                      pl.BlockSpec((1, d, f), w_idx),
                      pl.BlockSpec((1, f, d), w_idx)],
            out_specs=pl.BlockSpec(memory_space=pl.ANY),
            scratch_shapes=[pltpu.VMEM((2, rows, half), U32), pltpu.VMEM((2, rows, half), U32),
                            pltpu.VMEM((d, f), BF16), pltpu.VMEM((d, f), BF16), pltpu.VMEM((f, d), BF16),
                            pltpu.SemaphoreType.DMA((2,)), pltpu.SemaphoreType.DMA((2,))]),
        out_shape=jax.ShapeDtypeStruct((n_blocks * MOE_BLOCK, half), U32),
        compiler_params=_cparams("arbitrary"),
        name="experts",
    )(items, xs, w_gate, w_up, w_down)


SC_WINDOW = 64


def _sc_gather_rows(table, idx):
    n_idx = idx.shape[0]
    width = table.shape[1]
    info = plsc.get_sparse_core_info()
    n_workers = info.num_cores * info.num_subcores
    per_worker = n_idx // n_workers
    assert per_worker * n_workers == n_idx and per_worker % SC_WINDOW == 0
    mesh = plsc.VectorSubcoreMesh(core_axis_name="c", subcore_axis_name="s")

    def body(table_hbm, idx_hbm, out_hbm, idx_v, rows_v, sem):
        wid = lax.axis_index("s") * info.num_cores + lax.axis_index("c")
        base = wid * per_worker

        @pl.loop(0, per_worker // SC_WINDOW)
        def _(j):
            off = base + j * SC_WINDOW
            pltpu.sync_copy(idx_hbm.at[pl.ds(off, SC_WINDOW)], idx_v)
            pltpu.async_copy(table_hbm.at[idx_v], rows_v, sem).wait()
            pltpu.sync_copy(rows_v, out_hbm.at[pl.ds(off, SC_WINDOW)])

    return pl.kernel(
        body,
        out_type=jax.ShapeDtypeStruct((n_idx, width), table.dtype),
        mesh=mesh,
        scratch_types=[pltpu.VMEM((SC_WINDOW,), I32), pltpu.VMEM((SC_WINDOW, width), table.dtype),
                       pltpu.SemaphoreType.DMA],
        name="sc_gather",
    )(table, idx)


def _combine_kernel(hp_ref, x1_ref, g2_ref, w_ref, sgw_ref, suw_ref, sdw_ref, yg_ref, o_ref):
    xa, xb = _unpack_halves(hp_ref[...])
    xa = xa.astype(BF16)
    xb = xb.astype(BF16)
    half = xa.shape[1]
    tb = xa.shape[0]
    g = _dot(xa, sgw_ref[0:half]) + _dot(xb, sgw_ref[half:])
    u = _dot(xa, suw_ref[0:half]) + _dot(xb, suw_ref[half:])
    shared = _dot((_silu(g) * u).astype(BF16), sdw_ref[...])

    w = w_ref[...]
    acc_a = jnp.zeros((tb, half), F32)
    acc_b = jnp.zeros((tb, half), F32)
    for k in range(TOP_K):
        ya, yb = _unpack_halves(yg_ref[0, k])
        acc_a += ya * w[:, k:k + 1]
        acc_b += yb * w[:, k:k + 1]
    g2 = g2_ref[0]
    o_ref[:, 0:half] = x1_ref[:, 0:half] + g2[:, 0:half] * (acc_a + shared[:, 0:half])
    o_ref[:, half:] = x1_ref[:, half:] + g2[:, half:] * (acc_b + shared[:, half:])


def _combine(hp, x1, g2, w_tok, sgw, suw, sdw, yg, *, tb, seq_len):
    t, half = hp.shape
    d = 2 * half
    tb = min(tb, seq_len)
    per_seq = seq_len // tb
    f = sgw.shape[1]
    return pl.pallas_call(
        _combine_kernel,
        grid=(t // tb,),
        in_specs=[pl.BlockSpec((tb, half), lambda i: (i, 0)),
                  pl.BlockSpec((tb, d), lambda i: (i, 0)),
                  pl.BlockSpec((1, 1, d), lambda i: (i // per_seq, 0, 0)),
                  pl.BlockSpec((tb, TOP_K), lambda i: (i, 0)),
                  pl.BlockSpec((d, f), lambda i: (0, 0)),
                  pl.BlockSpec((d, f), lambda i: (0, 0)),
                  pl.BlockSpec((f, d), lambda i: (0, 0)),
                  pl.BlockSpec((1, TOP_K, tb, half), lambda i: (i, 0, 0, 0))],
        out_specs=pl.BlockSpec((tb, d), lambda i: (i, 0)),
        out_shape=jax.ShapeDtypeStruct((t, d), F32),
        compiler_params=_cparams("arbitrary"),
        name="combine",
    )(hp, x1, g2, w_tok, sgw, suw, sdw, yg)


def _rope_tables(l):
    rows = l // GRID_W
    r = jnp.repeat(jnp.arange(rows), GRID_W).astype(F32)
    col = jnp.tile(jnp.arange(GRID_W), rows).astype(F32)
    n_f = HEAD_DIM // 4
    freqs = ROPE_THETA ** (-jnp.arange(n_f, dtype=F32) / n_f)
    ang = jnp.concatenate([r[:, None] * freqs, col[:, None] * freqs], axis=-1)
    ang = jnp.tile(jnp.repeat(ang, 2, axis=1), (1, LANES // HEAD_DIM))
    sign = jnp.where(jnp.arange(LANES) % 2 == 0, -1.0, 1.0).astype(F32)
    return jnp.cos(ang), jnp.sin(ang) * sign


def kernel(x, c, ctx, c_ctx, w_mod, b_mod, norm1_w, norm2_w, w_in, q_norm_w, k_norm_w, ret_decay_fwd,
           ret_decay_bwd, w_out, router_w, router_bias, exp_w_gate, exp_w_up, exp_w_down, sh_w_gate,
           sh_w_up, sh_w_down):
    b, l, d = x.shape
    lc = ctx.shape[1]
    t = b * l
    assert w_mod.shape[0] == 1, "single layer"
    assert l % CHUNK == 0 and lc % CHUNK == 0 and l % GRID_W == 0

    rows = (b + 1 + 7) // 8 * 8
    cc = jnp.zeros((rows, d), F32).at[:b].set(c).at[b].set(c_ctx)
    mod = _modulation(cc, w_mod[0], b_mod[0])
    sh1, sc1, g1, sh2, sc2, g2 = [mod[:b, i * d:(i + 1) * d].reshape(b, 1, d) for i in range(6)]
    shc = mod[b, 0:d].reshape(1, 1, d)
    scc = mod[b, d:2 * d].reshape(1, 1, d)

    wi = w_in[0].astype(BF16)
    qnw = jnp.tile(q_norm_w[0], LANES // HEAD_DIM).reshape(1, LANES)
    knw = jnp.tile(k_norm_w[0], LANES // HEAD_DIM).reshape(1, LANES)
    cos, sin = _rope_tables(l)
    n1 = norm1_w[0].reshape(1, d)

    cklo, ckhi, cvlo, cvhi, crk, crv = _projection(
        ctx, shc, scc, n1, wi, qnw, knw, cos[:lc], sin[:lc], rope=False, with_q=False, tm=TILE_PROJ)
    klo, khi, vlo, vhi, rk, rv, q, rq, sg = _projection(
        x, sh1, sc1, n1, wi, qnw, knw, cos, sin, rope=True, with_q=True, tm=TILE_PROJ)

    attn = _attention(q, klo, khi, vlo, vhi, cklo, ckhi, cvlo, cvhi, tq=TILE_ATTN_Q)
    dec_f = jnp.repeat(ret_decay_fwd[0].astype(F32), HEAD_DIM).reshape(1, RET_W)
    dec_b = jnp.repeat(ret_decay_bwd[0].astype(F32), HEAD_DIM).reshape(1, RET_W)
    ret = _retention(rq, rk, rv, sg, crk, crv, dec_f, dec_b)

    wo = w_out[0].astype(BF16)
    r_hi, r_lo = _split(router_w[0].T)
    x1, hp, logits_t = _out_projection(attn, ret, x, wo[:ATTN_W], wo[ATTN_W:], g1, sh2, sc2,
                                       norm2_w[0].reshape(1, d), r_hi, r_lo, tm=TILE_OUT)

    idx_t, w_t, rank_t, cnt_col, cnt_row = _route(logits_t, router_bias[0].reshape(N_EXPERTS, 1), tb=TILE_TOKENS)
    n_blocks = -(-(t * TOP_K) // MOE_BLOCK) + N_EXPERTS
    tb = TILE_TOKENS
    dest, meta, items = _destinations(idx_t, rank_t, cnt_col, cnt_row, tb=tb, n_blocks=n_blocks)
    dest_flat = dest.reshape(-1)
    xs = _dispatch(meta, hp, dest_flat, tb=tb, n_blocks=n_blocks)
    ys = _experts(items, xs, exp_w_gate[0], exp_w_up[0], exp_w_down[0], n_blocks=n_blocks)
    yg = _sc_gather_rows(lax.bitcast_convert_type(ys, I32), dest_flat)
    yg = lax.bitcast_convert_type(yg, U32).reshape(dest.shape + (d // 2,))
    out = _combine(hp, x1.reshape(t, d), g2, w_t.T, sh_w_gate[0].astype(BF16), sh_w_up[0].astype(BF16),
                   sh_w_down[0].astype(BF16), yg, tb=tb, seq_len=l)
    return out.reshape(b, l, d)
```

```python
import functools

import jax
import jax.numpy as jnp
from jax import lax
from jax.experimental import pallas as pl
from jax.experimental.pallas import tpu as pltpu
from jax.experimental.pallas import tpu_sc as plsc

F32 = jnp.float32
BF16 = jnp.bfloat16
I32 = jnp.int32
U32 = jnp.uint32
PACKED = jnp.int32

HEAD_DIM = 64
LANES = 128
SUBLANES = 8
ATTN_HEADS = 8
ATTN_KV_HEADS = 2
GQA = ATTN_HEADS // ATTN_KV_HEADS
RET_HEADS = 8
ATTN_W = ATTN_HEADS * HEAD_DIM
KV_W = ATTN_KV_HEADS * HEAD_DIM
RET_W = RET_HEADS * HEAD_DIM
RET_PAIRS = RET_W // LANES
CHUNK = 128
GRID_W = 64
ROPE_THETA = 10000.0
N_EXPERTS = 256
TOP_K = 8
N_GROUPS = 8
GROUP_SIZE = N_EXPERTS // N_GROUPS
TOPK_GROUPS = 4
ROUTED_SCALE = 2.5
MOE_BLOCK = 128
EPS = 1e-6
QK_SCALE = HEAD_DIM ** -0.5

OFF_AK = 0
OFF_AV = OFF_AK + KV_W
OFF_RK = OFF_AV + KV_W
OFF_RV = OFF_RK + RET_W
CTX_KV_COLS = OFF_RV + RET_W
OFF_AQ = CTX_KV_COLS
OFF_RQ = OFF_AQ + ATTN_W
OFF_RG = OFF_RQ + RET_W
IN_COLS = OFF_RG + RET_W

VMEM_LIMIT = 52 * 1024 * 1024

TILE_PROJ = 256
TILE_ATTN_Q = 256
TILE_OUT = 256
TILE_TOKENS = 256
ITEM_BLOCKS = 5
HI_MASK = 0xFFFF0000


def _cparams(*sem):
    return pltpu.CompilerParams(dimension_semantics=sem, vmem_limit_bytes=VMEM_LIMIT)


def _split(a):
    hi = a.astype(BF16)
    lo = (a - hi.astype(F32)).astype(BF16)
    return hi, lo


def _dot(a, b):
    return jnp.dot(a, b, preferred_element_type=F32)


def _dot_nt(a, b):
    return lax.dot_general(a, b, (((1,), (1,)), ((), ())), preferred_element_type=F32)


def _sigmoid(v):
    return 1.0 / (1.0 + jnp.exp(-v))


def _silu(v):
    return v * _sigmoid(v)


def _pack_halves(a, b):
    ua = lax.bitcast_convert_type(a.astype(BF16).astype(F32), U32)
    ub = lax.bitcast_convert_type(b.astype(BF16).astype(F32), U32)
    return lax.bitcast_convert_type((ua & jnp.uint32(HI_MASK)) | (ub >> 16), PACKED)


def _unpack_halves(p):
    u = lax.bitcast_convert_type(p, U32)
    a = lax.bitcast_convert_type(u & jnp.uint32(HI_MASK), F32)
    b = lax.bitcast_convert_type(u << 16, F32)
    return a, b


def _mod_kernel(c_ref, w_ref, b_ref, o_ref):
    s_hi, s_lo = _split(_silu(c_ref[...]))
    w_hi, w_lo = _split(w_ref[...])
    o_ref[...] = _dot(s_hi, w_hi) + _dot(s_hi, w_lo) + _dot(s_lo, w_hi) + b_ref[...]


def _modulation(cc, w_mod, b_mod):
    rows, d = cc.shape
    n = w_mod.shape[1]
    tn = 768
    return pl.pallas_call(
        _mod_kernel,
        grid=(n // tn,),
        in_specs=[pl.BlockSpec((rows, d), lambda j: (0, 0)),
                  pl.BlockSpec((d, tn), lambda j: (0, j)),
                  pl.BlockSpec((1, tn), lambda j: (0, j))],
        out_specs=pl.BlockSpec((rows, tn), lambda j: (0, j)),
        out_shape=jax.ShapeDtypeStruct((rows, n), F32),
        compiler_params=_cparams("arbitrary"),
        name="mod",
    )(cc, w_mod, b_mod.reshape(1, n))


def _segment_ones():
    r = lax.broadcasted_iota(I32, (LANES, LANES), 0) // HEAD_DIM
    c = lax.broadcasted_iota(I32, (LANES, LANES), 1) // HEAD_DIM
    return jnp.where(r == c, 1.0, 0.0).astype(BF16)


def _head_mean_sq(v, seg):
    hi, lo = _split(v * v)
    return (_dot(hi, seg) + _dot(lo, seg)) * (1.0 / HEAD_DIM)


def _proj_kernel(x_ref, sh_ref, sc_ref, nw_ref, wi_ref, qnw_ref, knw_ref, cos_ref, sin_ref,
                 *out_refs, rope, with_q):
    if with_q:
        klo_ref, khi_ref, vlo_ref, vhi_ref, rk_ref, rv_ref, q_ref, rq_ref, sg_ref = out_refs
    else:
        klo_ref, khi_ref, vlo_ref, vhi_ref, rk_ref, rv_ref = out_refs
    x = x_ref[0]
    h = x * lax.rsqrt(jnp.mean(x * x, axis=-1, keepdims=True) + EPS) * nw_ref[...]
    h = h * (1.0 + sc_ref[0]) + sh_ref[0]
    z = _dot(h.astype(BF16), wi_ref[...])

    seg = _segment_ones()
    lane = lax.broadcasted_iota(I32, (x.shape[0], LANES), 1)
    low_half = lane < HEAD_DIM
    even = (lane & 1) == 0

    def norm_rope(v, w128):
        v = v * lax.rsqrt(_head_mean_sq(v, seg) + EPS) * w128
        if rope:
            swapped = jnp.where(even, pltpu.roll(v, LANES - 1, 1), pltpu.roll(v, 1, 1))
            v = v * cos_ref[...] + swapped * sin_ref[...]
        return v

    def write_lo_hi(v, lo_ref, hi_ref):
        sw = pltpu.roll(v, HEAD_DIM, 1)
        lo_ref[0, 0] = jnp.where(low_half, v, 0.0).astype(BF16)
        hi_ref[0, 0] = jnp.where(low_half, 0.0, sw).astype(BF16)
        lo_ref[0, 1] = jnp.where(low_half, sw, 0.0).astype(BF16)
        hi_ref[0, 1] = jnp.where(low_half, 0.0, v).astype(BF16)

    write_lo_hi(norm_rope(z[:, OFF_AK:OFF_AK + KV_W], knw_ref[...]), klo_ref, khi_ref)
    write_lo_hi(z[:, OFF_AV:OFF_AV + KV_W], vlo_ref, vhi_ref)
    rk_ref[0] = (z[:, OFF_RK:OFF_RK + RET_W] * QK_SCALE).astype(BF16)
    rv_ref[0] = z[:, OFF_RV:OFF_RV + RET_W].astype(BF16)
    if with_q:
        for j in range(ATTN_W // LANES):
            qj = norm_rope(z[:, OFF_AQ + j * LANES:OFF_AQ + (j + 1) * LANES], qnw_ref[...])
            q_ref[0, :, j * LANES:(j + 1) * LANES] = (qj * QK_SCALE).astype(BF16)
        rq_ref[0] = z[:, OFF_RQ:OFF_RQ + RET_W].astype(BF16)
        sg_ref[0] = _silu(z[:, OFF_RG:OFF_RG + RET_W]).astype(BF16)


def _projection(x, shift, scale, norm_w, wi_bf16, qnw, knw, cos, sin, *, rope, with_q, tm):
    b, l, d = x.shape
    tm = min(tm, l)
    ncols = IN_COLS if with_q else CTX_KV_COLS
    per_batch = shift.shape[0] > 1
    mod_idx = (lambda bi, i: (bi, 0, 0)) if per_batch else (lambda bi, i: (0, 0, 0))
    kv_shape = jax.ShapeDtypeStruct((b, ATTN_KV_HEADS, l, LANES), BF16)
    kv_spec = pl.BlockSpec((1, ATTN_KV_HEADS, tm, LANES), lambda bi, i: (bi, 0, i, 0))
    w_shape = jax.ShapeDtypeStruct((b, l, RET_W), BF16)
    w_spec = pl.BlockSpec((1, tm, RET_W), lambda bi, i: (bi, i, 0))
    out_shape = [kv_shape] * 4 + [w_shape] * 2
    out_specs = [kv_spec] * 4 + [w_spec] * 2
    if with_q:
        out_shape += [w_shape] * 3
        out_specs += [w_spec] * 3
    return pl.pallas_call(
        functools.partial(_proj_kernel, rope=rope, with_q=with_q),
        grid=(b, l // tm),
        in_specs=[pl.BlockSpec((1, tm, d), lambda bi, i: (bi, i, 0)),
                  pl.BlockSpec((1, 1, d), mod_idx),
                  pl.BlockSpec((1, 1, d), mod_idx),
                  pl.BlockSpec((1, d), lambda bi, i: (0, 0)),
                  pl.BlockSpec((d, ncols), lambda bi, i: (0, 0)),
                  pl.BlockSpec((1, LANES), lambda bi, i: (0, 0)),
                  pl.BlockSpec((1, LANES), lambda bi, i: (0, 0)),
                  pl.BlockSpec((tm, LANES), lambda bi, i: (i, 0)),
                  pl.BlockSpec((tm, LANES), lambda bi, i: (i, 0))],
        out_specs=out_specs,
        out_shape=out_shape,
        compiler_params=_cparams("arbitrary", "arbitrary"),
        name="proj_latent" if with_q else "proj_ctx",
    )(x, shift, scale, norm_w, wi_bf16, qnw, knw, cos, sin)


def _attn_kernel(q_ref, klo_ref, khi_ref, vlo_ref, vhi_ref, cklo_ref, ckhi_ref, cvlo_ref, cvhi_ref,
                 o_ref, kl_s, kh_s, va_s, *, l, lc):
    lk = l + lc

    @pl.when(pl.program_id(2) == 0)
    def _():
        kl_s[0:l] = klo_ref[0, 0]
        kl_s[l:lk] = cklo_ref[0, 0]
        kh_s[0:l] = khi_ref[0, 0]
        kh_s[l:lk] = ckhi_ref[0, 0]
        lane = lax.broadcasted_iota(I32, (lk, LANES), 1)
        ones_lo = jnp.where(lane < HEAD_DIM, 1.0, 0.0).astype(BF16)
        ones_hi = jnp.where(lane < HEAD_DIM, 0.0, 1.0).astype(BF16)
        for g, (v_ref, cv_ref, ones) in enumerate(((vlo_ref, cvlo_ref, ones_lo), (vhi_ref, cvhi_ref, ones_hi),
                                                   (vlo_ref, cvlo_ref, ones_lo), (vhi_ref, cvhi_ref, ones_hi))):
            v_col, one_col = (0, LANES) if g < 2 else (LANES, 0)
            va_s[g, 0:l, v_col:v_col + LANES] = v_ref[0, 0]
            va_s[g, l:lk, v_col:v_col + LANES] = cv_ref[0, 0]
            va_s[g, :, one_col:one_col + LANES] = ones

    q = q_ref[0]
    acc = []
    for g in range(GQA):
        qp = q[:, (g // 2) * LANES:(g // 2 + 1) * LANES]
        kk = kl_s[...] if g % 2 == 0 else kh_s[...]
        s = _dot_nt(qp, kk)
        p = jnp.exp(s - jnp.max(s, axis=-1, keepdims=True)).astype(BF16)
        acc.append(_dot(p, va_s[g]))
    out_a = acc[0] + acc[1]
    out_b = acc[2] + acc[3]
    o_ref[0, :, 0:LANES] = (out_a[:, 0:LANES] / out_a[:, LANES:2 * LANES]).astype(BF16)
    o_ref[0, :, LANES:2 * LANES] = (out_b[:, LANES:2 * LANES] / out_b[:, 0:LANES]).astype(BF16)


def _attention(q, klo, khi, vlo, vhi, cklo, ckhi, cvlo, cvhi, *, tq):
    b, l, _ = q.shape
    lc = cklo.shape[2]
    lk = l + lc
    tq = min(tq, l)
    gw = GQA * HEAD_DIM
    kv_spec = pl.BlockSpec((1, 1, l, LANES), lambda bi, h, i: (bi, h, 0, 0))
    ckv_spec = pl.BlockSpec((1, 1, lc, LANES), lambda bi, h, i: (bi, h, 0, 0))
    return pl.pallas_call(
        functools.partial(_attn_kernel, l=l, lc=lc),
        grid=(b, ATTN_KV_HEADS, l // tq),
        in_specs=[pl.BlockSpec((1, tq, gw), lambda bi, h, i: (bi, i, h))] + [kv_spec] * 4 + [ckv_spec] * 4,
        out_specs=pl.BlockSpec((1, tq, gw), lambda bi, h, i: (bi, i, h)),
        out_shape=jax.ShapeDtypeStruct((b, l, ATTN_W), BF16),
        scratch_shapes=[pltpu.VMEM((lk, LANES), BF16), pltpu.VMEM((lk, LANES), BF16),
                        pltpu.VMEM((GQA, lk, 2 * LANES), BF16)],
        compiler_params=_cparams("arbitrary", "arbitrary", "arbitrary"),
        name="attn",
    )(q, klo, khi, vlo, vhi, cklo, ckhi, cvlo, cvhi)


def _log_sigmoid(v):
    return jnp.minimum(v, 0.0) - jnp.log(1.0 + jnp.exp(-jnp.abs(v)))


def _ret_kernel(rq_ref, rk_ref, rv_ref, sg_ref, crk_ref, crv_ref, df_ref, db_ref, o_ref,
                m_s, tab_s, sb_s, sf_s, *, l, lc):
    n = l // CHUNK
    nc = lc // CHUNK
    lgf = _log_sigmoid(df_ref[...])
    lgb = _log_sigmoid(db_ref[...])
    pos = lax.broadcasted_iota(I32, (CHUNK, RET_W), 0).astype(F32)
    tab_s[0] = jnp.exp((pos + 1.0) * lgf)
    tab_s[1] = jnp.exp((CHUNK - pos) * lgb)
    tab_s[2] = jnp.exp((CHUNK - 1.0 - pos) * lgf)
    tab_s[3] = jnp.exp(pos * lgb)
    gf_c = jnp.exp(CHUNK * lgf)
    gb_c = jnp.exp(CHUNK * lgb)
    row = lax.broadcasted_iota(I32, (CHUNK, CHUNK), 0)
    col = lax.broadcasted_iota(I32, (CHUNK, CHUNK), 1)
    diff = (row - col).astype(F32)
    for h in range(RET_HEADS):
        lf = lgf[:, h * HEAD_DIM:h * HEAD_DIM + 1]
        lb = lgb[:, h * HEAD_DIM:h * HEAD_DIM + 1]
        m_s[h] = jnp.where(diff > 0, jnp.exp(diff * lf), jnp.where(diff < 0, jnp.exp(-diff * lb), 2.0))

    lane = lax.broadcasted_iota(I32, (CHUNK, LANES), 1)
    low_half = lane < HEAD_DIM
    diag = (lax.broadcasted_iota(I32, (LANES, LANES), 0) // HEAD_DIM
            == lax.broadcasted_iota(I32, (LANES, LANES), 1) // HEAD_DIM)
    seg = jnp.where(diag, 1.0, 0.0).astype(BF16)

    def contrib(k_ref, v_ref, r0, p, zeta_idx):
        cols = slice(p * LANES, (p + 1) * LANES)
        kz = k_ref[0, pl.ds(r0, CHUNK), cols].astype(F32) * tab_s[zeta_idx, :, cols]
        kv = _dot(jnp.transpose(kz).astype(BF16), v_ref[0, pl.ds(r0, CHUNK), cols])
        return jnp.where(diag, kv, 0.0)

    for p in range(RET_PAIRS):
        cols = slice(p * LANES, (p + 1) * LANES)
        sf = jnp.zeros((LANES, LANES), F32)
        sb = jnp.zeros((LANES, LANES), F32)
        for c in range(nc):
            sf = gf_c[:, cols] * sf + contrib(crk_ref, crv_ref, c * CHUNK, p, 2)
            cb = nc - 1 - c
            sb = gb_c[:, cols] * sb + contrib(crk_ref, crv_ref, cb * CHUNK, p, 3)
        sf_s[p] = sf
        sb_s[n, p] = sb

    def bwd_body(j, carry):
        c = n - 1 - j
        r0 = pl.multiple_of(c * CHUNK, CHUNK)
        for p in range(RET_PAIRS):
            cols = slice(p * LANES, (p + 1) * LANES)
            prev = sb_s[c + 1, p]
            sb_s[c, p] = gb_c[:, cols] * prev + contrib(rk_ref, rv_ref, r0, p, 3)
        return carry

    lax.fori_loop(0, n, bwd_body, 0)

    def fwd_body(c, carry):
        r0 = pl.multiple_of(c * CHUNK, CHUNK)
        for p in range(RET_PAIRS):
            cols = slice(p * LANES, (p + 1) * LANES)
            qp = rq_ref[0, pl.ds(r0, CHUNK), cols]
            kp = rk_ref[0, pl.ds(r0, CHUNK), cols]
            vp = rv_ref[0, pl.ds(r0, CHUNK), cols]
            zero = jnp.zeros_like(kp)
            s0 = _dot_nt(qp, jnp.where(low_half, kp, zero))
            s1 = _dot_nt(qp, jnp.where(low_half, zero, kp))
            a0 = (s0 * m_s[2 * p]).astype(BF16)
            a1 = (s1 * m_s[2 * p + 1]).astype(BF16)
            y = _dot(a0, jnp.where(low_half, vp, zero)) + _dot(a1, jnp.where(low_half, zero, vp))
            qf = qp.astype(F32)
            sf = sf_s[p]
            y += _dot((qf * tab_s[0, :, cols]).astype(BF16), sf.astype(BF16))
            y += _dot((qf * tab_s[1, :, cols]).astype(BF16), sb_s[c + 1, p].astype(BF16))
            hi, lo = _split(y * y)
            ms = (_dot(hi, seg) + _dot(lo, seg)) * (1.0 / HEAD_DIM)
            out = y * lax.rsqrt(ms + EPS) * sg_ref[0, pl.ds(r0, CHUNK), cols].astype(F32)
            o_ref[0, pl.ds(r0, CHUNK), cols] = out.astype(BF16)
            sf_s[p] = gf_c[:, cols] * sf + contrib(rk_ref, rv_ref, r0, p, 2)
        return carry

    lax.fori_loop(0, n, fwd_body, 0)


def _retention(rq, rk, rv, sg, crk, crv, dec_f, dec_b):
    b, l, _ = rq.shape
    lc = crk.shape[1]
    n = l // CHUNK
    spec = pl.BlockSpec((1, l, RET_W), lambda bi: (bi, 0, 0))
    cspec = pl.BlockSpec((1, lc, RET_W), lambda bi: (bi, 0, 0))
    dspec = pl.BlockSpec((1, RET_W), lambda bi: (0, 0))
    return pl.pallas_call(
        functools.partial(_ret_kernel, l=l, lc=lc),
        grid=(b,),
        in_specs=[spec, spec, spec, spec, cspec, cspec, dspec, dspec],
        out_specs=spec,
        out_shape=jax.ShapeDtypeStruct((b, l, RET_W), BF16),
        scratch_shapes=[pltpu.VMEM((RET_HEADS, CHUNK, CHUNK), F32),
                        pltpu.VMEM((4, CHUNK, RET_W), F32),
                        pltpu.VMEM((n + 1, RET_PAIRS, LANES, LANES), F32),
                        pltpu.VMEM((RET_PAIRS, LANES, LANES), F32)],
        compiler_params=_cparams("arbitrary"),
        name="ret",
    )(rq, rk, rv, sg, crk, crv, dec_f, dec_b)


def _out_kernel(attn_ref, ret_ref, x_ref, wa_ref, wr_ref, g1_ref, sh_ref, sc_ref, nw_ref, rhi_ref, rlo_ref,
                x1_ref, hp_ref, lg_ref):
    y = _dot(attn_ref[0], wa_ref[...]) + _dot(ret_ref[0], wr_ref[...])
    x1 = x_ref[0] + g1_ref[0] * y
    x1_ref[0] = x1
    h = x1 * lax.rsqrt(jnp.mean(x1 * x1, axis=-1, keepdims=True) + EPS) * nw_ref[...]
    h = h * (1.0 + sc_ref[0]) + sh_ref[0]
    half = h.shape[1] // 2
    hp_ref[...] = _pack_halves(h[:, :half], h[:, half:])
    h_hi, h_lo = _split(h)
    lg_ref[...] = _dot_nt(rhi_ref[...], h_hi) + _dot_nt(rhi_ref[...], h_lo) + _dot_nt(rlo_ref[...], h_hi)


def _out_projection(attn, ret, x, wa, wr, g1, sh2, sc2, norm_w, r_hi, r_lo, *, tm):
    b, l, d = x.shape
    tm = min(tm, l)
    nt = l // tm
    t = b * l
    mspec = pl.BlockSpec((1, 1, d), lambda bi, i: (bi, 0, 0))
    return pl.pallas_call(
        _out_kernel,
        grid=(b, nt),
        in_specs=[pl.BlockSpec((1, tm, ATTN_W), lambda bi, i: (bi, i, 0)),
                  pl.BlockSpec((1, tm, RET_W), lambda bi, i: (bi, i, 0)),
                  pl.BlockSpec((1, tm, d), lambda bi, i: (bi, i, 0)),
                  pl.BlockSpec((ATTN_W, d), lambda bi, i: (0, 0)),
                  pl.BlockSpec((RET_W, d), lambda bi, i: (0, 0)),
                  mspec, mspec, mspec,
                  pl.BlockSpec((1, d), lambda bi, i: (0, 0)),
                  pl.BlockSpec((N_EXPERTS, d), lambda bi, i: (0, 0)),
                  pl.BlockSpec((N_EXPERTS, d), lambda bi, i: (0, 0))],
        out_specs=[pl.BlockSpec((1, tm, d), lambda bi, i: (bi, i, 0)),
                   pl.BlockSpec((tm, d // 2), lambda bi, i: (bi * nt + i, 0)),
                   pl.BlockSpec((N_EXPERTS, tm), lambda bi, i: (0, bi * nt + i))],
        out_shape=[jax.ShapeDtypeStruct((b, l, d), F32),
                   jax.ShapeDtypeStruct((t, d // 2), PACKED),
                   jax.ShapeDtypeStruct((N_EXPERTS, t), F32)],
        compiler_params=_cparams("arbitrary", "arbitrary"),
        name="out_proj",
    )(attn, ret, x, wa, wr, g1, sh2, sc2, norm_w, r_hi, r_lo)


def _route_kernel(lg_ref, bias_ref, idx_ref, w_ref, rank_ref, cnt_col_ref, cnt_row_ref, tri_s, col_s, row_s):
    tb = lg_ref.shape[1]
    step = pl.program_id(0)

    @pl.when(step == 0)
    def _():
        r = lax.broadcasted_iota(I32, (tb, tb), 0)
        c = lax.broadcasted_iota(I32, (tb, tb), 1)
        tri_s[...] = jnp.where(r <= c, 1.0, 0.0).astype(BF16)
        col_s[...] = jnp.zeros_like(col_s)
        row_s[...] = jnp.zeros_like(row_s)

    scores = _sigmoid(lg_ref[...])
    biased = scores + bias_ref[...]
    neg = -jnp.inf
    sub = lax.broadcasted_iota(I32, (GROUP_SIZE, tb), 0).astype(F32)

    gscore = []
    for g in range(N_GROUPS):
        blk = biased[g * GROUP_SIZE:(g + 1) * GROUP_SIZE]
        m1 = jnp.max(blk, axis=0, keepdims=True)
        first = jnp.min(jnp.where(blk == m1, sub, float(GROUP_SIZE)), axis=0, keepdims=True)
        m2 = jnp.max(jnp.where(sub == first, neg, blk), axis=0, keepdims=True)
        gscore.append(m1 + m2)
    gs = jnp.concatenate(gscore, axis=0)
    gsub = lax.broadcasted_iota(I32, (N_GROUPS, tb), 0).astype(F32)
    keep = jnp.zeros((N_GROUPS, tb), F32)
    for _ in range(TOPK_GROUPS):
        m = jnp.max(gs, axis=0, keepdims=True)
        first = jnp.min(jnp.where(gs == m, gsub, float(N_GROUPS)), axis=0, keepdims=True)
        sel = gsub == first
        keep = jnp.where(sel, 1.0, keep)
        gs = jnp.where(sel, neg, gs)
    masked = jnp.concatenate(
        [jnp.where(keep[g:g + 1] > 0.0, biased[g * GROUP_SIZE:(g + 1) * GROUP_SIZE], neg)
         for g in range(N_GROUPS)], axis=0)

    esub = lax.broadcasted_iota(I32, (N_EXPERTS, tb), 0).astype(F32)
    sels, idxs, ws = [], [], []
    chosen = jnp.zeros((N_EXPERTS, tb), F32)
    for _ in range(TOP_K):
        m = jnp.max(masked, axis=0, keepdims=True)
        first = jnp.min(jnp.where(masked == m, esub, float(N_EXPERTS)), axis=0, keepdims=True)
        sel = esub == first
        sels.append(sel)
        idxs.append(first)
        ws.append(jnp.sum(jnp.where(sel, scores, 0.0), axis=0, keepdims=True))
        chosen = jnp.where(sel, 1.0, chosen)
        masked = jnp.where(sel, neg, masked)
    wsum = ws[0]
    for k in range(1, TOP_K):
        wsum = wsum + ws[k]
    idx_ref[...] = jnp.concatenate(idxs, axis=0).astype(I32)
    w_ref[...] = jnp.concatenate([wk / wsum * ROUTED_SCALE for wk in ws], axis=0)

    chosen_b = chosen.astype(BF16)
    incl = _dot(chosen_b, tri_s[...])
    before = incl - chosen + col_s[...]
    rank_ref[...] = jnp.concatenate(
        [jnp.sum(jnp.where(sel, before, 0.0), axis=0, keepdims=True) for sel in sels], axis=0).astype(I32)
    col_s[...] = col_s[...] + incl[:, tb - 1:tb]
    row_s[...] = row_s[...] + _dot_nt(jnp.ones((8, tb), BF16), chosen_b)
    cnt_col_ref[...] = col_s[...].astype(I32)
    cnt_row_ref[...] = row_s[...].astype(I32)


def _route(logits_t, bias_col, *, tb):
    e, t = logits_t.shape
    tb = min(tb, t)
    kspec = pl.BlockSpec((TOP_K, tb), lambda i: (0, i))
    return pl.pallas_call(
        _route_kernel,
        grid=(t // tb,),
        in_specs=[pl.BlockSpec((e, tb), lambda i: (0, i)),
                  pl.BlockSpec((e, 1), lambda i: (0, 0))],
        out_specs=[kspec, kspec, kspec,
                   pl.BlockSpec((e, 1), lambda i: (0, 0)),
                   pl.BlockSpec((8, e), lambda i: (0, 0))],
        out_shape=[jax.ShapeDtypeStruct((TOP_K, t), I32),
                   jax.ShapeDtypeStruct((TOP_K, t), F32),
                   jax.ShapeDtypeStruct((TOP_K, t), I32),
                   jax.ShapeDtypeStruct((e, 1), I32),
                   jax.ShapeDtypeStruct((8, e), I32)],
        scratch_shapes=[pltpu.VMEM((tb, tb), BF16), pltpu.VMEM((e, 1), F32), pltpu.VMEM((8, e), F32)],
        compiler_params=_cparams("arbitrary"),
        name="route",
    )(logits_t, bias_col)


def _pad_block(cnt):
    return (cnt + (MOE_BLOCK - 1)) // MOE_BLOCK * MOE_BLOCK


def _max_items(n_blocks):
    return n_blocks // ITEM_BLOCKS + N_EXPERTS


def _dest_kernel(idx_ref, rank_ref, cnt_col_ref, cnt_row_ref, dest_ref, meta_ref, items_ref):
    tb = idx_ref.shape[1]
    nip = items_ref.shape[1]
    pad_col = _pad_block(cnt_col_ref[...])
    pad_row = _pad_block(cnt_row_ref[0:1, :])
    er = lax.broadcasted_iota(I32, (N_EXPERTS, N_EXPERTS), 0)
    ec = lax.broadcasted_iota(I32, (N_EXPERTS, N_EXPERTS), 1)
    start_col = jnp.sum(jnp.where(ec < er, pad_row, 0), axis=1, keepdims=True)
    start_row = jnp.sum(jnp.where(er < ec, pad_col, 0), axis=0, keepdims=True)

    esub = lax.broadcasted_iota(I32, (N_EXPERTS, tb), 0)
    rows = []
    for k in range(TOP_K):
        onehot = esub == idx_ref[k:k + 1, :]
        rows.append(jnp.sum(jnp.where(onehot, start_col, 0), axis=0, keepdims=True) + rank_ref[k:k + 1, :])
    dest_ref[0] = jnp.concatenate(rows, axis=0)

    used = jnp.sum(pad_row, axis=1, keepdims=True) // MOE_BLOCK
    meta_ref[...] = jnp.concatenate(
        [cnt_row_ref[0:1, :], start_row, pad_row, jnp.broadcast_to(used, (1, N_EXPERTS)),
         jnp.zeros((4, N_EXPERTS), I32)], axis=0)

    nb_col = pad_col // MOE_BLOCK
    it_col = (nb_col + (ITEM_BLOCKS - 1)) // ITEM_BLOCKS
    it_row = (pad_row // MOE_BLOCK + (ITEM_BLOCKS - 1)) // ITEM_BLOCKS
    it_start = jnp.sum(jnp.where(ec < er, it_row, 0), axis=1, keepdims=True)
    n_items = jnp.sum(it_row, axis=1, keepdims=True)
    lane = lax.broadcasted_iota(I32, (1, nip), 1)
    owner = jnp.sum(jnp.where(it_start + it_col <= lane, 1, 0), axis=0, keepdims=True)
    owner = jnp.minimum(owner, N_EXPERTS - 1)
    onehot = lax.broadcasted_iota(I32, (N_EXPERTS, nip), 0) == owner

    def pick(col):
        return jnp.sum(jnp.where(onehot, col, 0), axis=0, keepdims=True)

    j = lane - pick(it_start)
    block0 = pick(start_col) // MOE_BLOCK + ITEM_BLOCKS * j
    nvalid = jnp.clip(pick(nb_col) - ITEM_BLOCKS * j, 0, ITEM_BLOCKS)
    items_ref[...] = jnp.concatenate(
        [owner, block0, jnp.where(lane < n_items, nvalid, 0), jnp.broadcast_to(n_items, (1, nip)),
         jnp.zeros((4, nip), I32)], axis=0)


def _destinations(idx_t, rank_t, cnt_col, cnt_row, *, tb, n_blocks):
    _, t = idx_t.shape
    tb = min(tb, t)
    nip = (_max_items(n_blocks) + LANES - 1) // LANES * LANES
    kspec = pl.BlockSpec((TOP_K, tb), lambda i: (0, i))
    return pl.pallas_call(
        _dest_kernel,
        grid=(t // tb,),
        in_specs=[kspec, kspec,
                  pl.BlockSpec((N_EXPERTS, 1), lambda i: (0, 0)),
                  pl.BlockSpec((8, N_EXPERTS), lambda i: (0, 0))],
        out_specs=[pl.BlockSpec((1, TOP_K, tb), lambda i: (i, 0, 0)),
                   pl.BlockSpec((8, N_EXPERTS), lambda i: (0, 0)),
                   pl.BlockSpec((8, nip), lambda i: (0, 0))],
        out_shape=[jax.ShapeDtypeStruct((t // tb, TOP_K, tb), I32),
                   jax.ShapeDtypeStruct((8, N_EXPERTS), I32),
                   jax.ShapeDtypeStruct((8, nip), I32)],
        compiler_params=_cparams("arbitrary"),
        name="dest",
    )(idx_t, rank_t, cnt_col, cnt_row)


_PAD_BITS = (64, 32, 16, 8)


def _dispatch_kernel(meta_ref, hp_ref, dest_hbm, xs_hbm, dest_s, zero_s, sem_d, sem_r, sem_z, *, tb, e_per_step):
    step = pl.program_id(0)
    n_assign = TOP_K * tb
    cp = pltpu.make_async_copy(dest_hbm.at[pl.ds(step * n_assign, n_assign)], dest_s, sem_d)
    cp.start()
    zero_s[...] = jnp.zeros_like(zero_s)
    cp.wait()

    def row_copy(t, k):
        d = dest_s[k * tb + t]
        return pltpu.make_async_copy(hp_ref.at[pl.ds(t, 1)], xs_hbm.at[pl.ds(d, 1)], sem_r)

    def issue(t, carry):
        for k in range(TOP_K):
            row_copy(t, k).start(priority=k % 2)
        return carry

    lax.fori_loop(0, tb, issue, 0)

    def tail_copy(c):
        row0 = pl.multiple_of((meta_ref[3, 0] + c) * MOE_BLOCK, MOE_BLOCK)
        return pltpu.make_async_copy(zero_s, xs_hbm.at[pl.ds(row0, MOE_BLOCK)], sem_z)

    @pl.when(step == 0)
    def _():
        for c in range(ITEM_BLOCKS - 1):
            tail_copy(c).start()
        for c in range(ITEM_BLOCKS - 1):
            tail_copy(c).wait()

    def pad_copies(e):
        cnt = meta_ref[0, e]
        off = meta_ref[1, e] + cnt
        rem = meta_ref[2, e] - cnt
        head = rem & (SUBLANES - 1)
        out = []
        for i in range(SUBLANES - 1):
            out.append((i < head,
                        pltpu.make_async_copy(zero_s.at[pl.ds(0, 1)], xs_hbm.at[pl.ds(off + i, 1)], sem_z)))
        off = off + head
        for bit in _PAD_BITS:
            out.append(((rem & bit) != 0,
                        pltpu.make_async_copy(zero_s.at[pl.ds(0, bit)],
                                              xs_hbm.at[pl.ds(pl.multiple_of(off, SUBLANES), bit)], sem_z)))
            off = off + (rem & bit)
        return out

    for j in range(e_per_step):
        for cond, c in pad_copies(step * e_per_step + j):
            pl.when(cond)(c.start)
    for j in range(e_per_step):
        for cond, c in pad_copies(step * e_per_step + j):
            pl.when(cond)(c.wait)

    for k in range(TOP_K):
        pltpu.make_async_copy(hp_ref, xs_hbm.at[pl.ds(0, tb)], sem_r).wait()


def _dispatch(meta, hp, dest_flat, *, tb, n_blocks):
    t, half = hp.shape
    tb = min(tb, t)
    steps = t // tb
    e_per_step = max(N_EXPERTS // steps, 1)
    assert e_per_step * steps == N_EXPERTS, "token blocks must divide the expert count"
    return pl.pallas_call(
        functools.partial(_dispatch_kernel, tb=tb, e_per_step=e_per_step),
        grid=(steps,),
        in_specs=[pl.BlockSpec(memory_space=pltpu.SMEM),
                  pl.BlockSpec((tb, half), lambda i: (i, 0)),
                  pl.BlockSpec(memory_space=pl.ANY)],
        out_specs=pl.BlockSpec(memory_space=pl.ANY),
        out_shape=jax.ShapeDtypeStruct(((n_blocks + ITEM_BLOCKS - 1) * MOE_BLOCK, half), PACKED),
        scratch_shapes=[pltpu.SMEM((TOP_K * tb,), I32), pltpu.VMEM((MOE_BLOCK, half), PACKED),
                        pltpu.SemaphoreType.DMA, pltpu.SemaphoreType.DMA, pltpu.SemaphoreType.DMA],
        compiler_params=_cparams("arbitrary"),
        name="dispatch",
    )(meta, hp, dest_flat)


def _experts_kernel(items_ref, xs_hbm, wg_ref, wu_ref, wd_ref, ys_hbm,
                    xbuf, ybuf, wg_s, wu_s, wd_s, sem_x, sem_y):
    i = pl.program_id(0)
    n_items = items_ref[3, 0]
    slot = i % 2
    rows = ITEM_BLOCKS * MOE_BLOCK

    def x_copy(item, s):
        row0 = pl.multiple_of(items_ref[1, item] * MOE_BLOCK, MOE_BLOCK)
        return pltpu.make_async_copy(xs_hbm.at[pl.ds(row0, rows)], xbuf.at[s], sem_x.at[s])

    def y_copies(item, s, fn):
        for k in range(ITEM_BLOCKS):
            row0 = pl.multiple_of((items_ref[1, item] + k) * MOE_BLOCK, MOE_BLOCK)
            cp = pltpu.make_async_copy(ybuf.at[s, pl.ds(k * MOE_BLOCK, MOE_BLOCK)],
                                       ys_hbm.at[pl.ds(row0, MOE_BLOCK)], sem_y.at[s])
            pl.when(k < items_ref[2, item])(functools.partial(fn, cp))

    @pl.when(i == 0)
    def _():
        x_copy(0, 0).start()

    @pl.when(i + 1 < n_items)
    def _():
        x_copy(i + 1, 1 - slot).start()

    @pl.when(i < n_items)
    def _():
        prev = jnp.maximum(i - 1, 0)

        @pl.when((i == 0) | (items_ref[0, i] != items_ref[0, prev]))
        def _():
            wg_s[...] = wg_ref[0].astype(BF16)
            wu_s[...] = wu_ref[0].astype(BF16)
            wd_s[...] = wd_ref[0].astype(BF16)

        x_copy(i, slot).wait()
        xa, xb = _unpack_halves(xbuf[slot])
        xa = xa.astype(BF16)
        xb = xb.astype(BF16)
        half = xa.shape[1]
        g = _dot(xa, wg_s[0:half]) + _dot(xb, wg_s[half:])
        u = _dot(xa, wu_s[0:half]) + _dot(xb, wu_s[half:])
        y = _dot((_silu(g) * u).astype(BF16), wd_s[...])
        ybuf[slot] = _pack_halves(y[:, :half], y[:, half:])
        y_copies(i, slot, lambda cp: cp.start())

        @pl.when(i > 0)
        def _():
            y_copies(prev, 1 - slot, lambda cp: cp.wait())

        @pl.when(i == n_items - 1)
        def _():
            y_copies(i, slot, lambda cp: cp.wait())


def _experts(items, xs, w_gate, w_up, w_down, *, n_blocks):
    half = xs.shape[1]
    e, d, f = w_gate.shape
    rows = ITEM_BLOCKS * MOE_BLOCK

    def w_idx(i, items_ref):
        return (items_ref[0, jnp.minimum(i, items_ref[3, 0] - 1)], 0, 0)

    return pl.pallas_call(
        _experts_kernel,
        grid_spec=pltpu.PrefetchScalarGridSpec(
            num_scalar_prefetch=1,
            grid=(_max_items(n_blocks),),
            in_specs=[pl.BlockSpec(memory_space=pl.ANY),
                      pl.BlockSpec((1, d, f), w_idx),
                      pl.BlockSpec((1, d, f), w_idx),
                      pl.BlockSpec((1, f, d), w_idx)],
            out_specs=pl.BlockSpec(memory_space=pl.ANY),
            scratch_shapes=[pltpu.VMEM((2, rows, half), PACKED), pltpu.VMEM((2, rows, half), PACKED),
                            pltpu.VMEM((d, f), BF16), pltpu.VMEM((d, f), BF16), pltpu.VMEM((f, d), BF16),
                            pltpu.SemaphoreType.DMA((2,)), pltpu.SemaphoreType.DMA((2,))]),
        out_shape=jax.ShapeDtypeStruct((n_blocks * MOE_BLOCK, half), PACKED),
        compiler_params=_cparams("arbitrary"),
        name="experts",
    )(items, xs, w_gate, w_up, w_down)


SC_WINDOW = 64


def _sc_gather_rows(table, idx):
    n_idx = idx.shape[0]
    width = table.shape[1]
    info = plsc.get_sparse_core_info()
    n_workers = info.num_cores * info.num_subcores
    per_worker = n_idx // n_workers
    assert per_worker * n_workers == n_idx and per_worker % SC_WINDOW == 0
    mesh = plsc.VectorSubcoreMesh(core_axis_name="c", subcore_axis_name="s")

    def body(table_hbm, idx_hbm, out_hbm, idx_v, rows_v, sem):
        wid = lax.axis_index("s") * info.num_cores + lax.axis_index("c")
        base = wid * per_worker

        @pl.loop(0, per_worker // SC_WINDOW)
        def _(j):
            off = base + j * SC_WINDOW
            pltpu.sync_copy(idx_hbm.at[pl.ds(off, SC_WINDOW)], idx_v)
            pltpu.async_copy(table_hbm.at[idx_v], rows_v, sem).wait()
            pltpu.sync_copy(rows_v, out_hbm.at[pl.ds(off, SC_WINDOW)])

    return pl.kernel(
        body,
        out_type=jax.ShapeDtypeStruct((n_idx, width), table.dtype),
        mesh=mesh,
        scratch_types=[pltpu.VMEM((SC_WINDOW,), I32), pltpu.VMEM((SC_WINDOW, width), table.dtype),
                       pltpu.SemaphoreType.DMA],
        name="sc_gather",
    )(table, idx)


def _combine_kernel(hp_ref, x1_ref, g2_ref, w_ref, sgw_ref, suw_ref, sdw_ref, yg_ref, o_ref):
    xa, xb = _unpack_halves(hp_ref[...])
    xa = xa.astype(BF16)
    xb = xb.astype(BF16)
    half = xa.shape[1]
    tb = xa.shape[0]
    g = _dot(xa, sgw_ref[0:half]) + _dot(xb, sgw_ref[half:])
    u = _dot(xa, suw_ref[0:half]) + _dot(xb, suw_ref[half:])
    shared = _dot((_silu(g) * u).astype(BF16), sdw_ref[...])

    w = w_ref[...]
    acc_a = jnp.zeros((tb, half), F32)
    acc_b = jnp.zeros((tb, half), F32)
    for k in range(TOP_K):
        ya, yb = _unpack_halves(yg_ref[0, k])
        acc_a += ya * w[:, k:k + 1]
        acc_b += yb * w[:, k:k + 1]
    g2 = g2_ref[0]
    o_ref[:, 0:half] = x1_ref[:, 0:half] + g2[:, 0:half] * (acc_a + shared[:, 0:half])
    o_ref[:, half:] = x1_ref[:, half:] + g2[:, half:] * (acc_b + shared[:, half:])


def _combine(hp, x1, g2, w_tok, sgw, suw, sdw, yg, *, tb, seq_len):
    t, half = hp.shape
    d = 2 * half
    tb = min(tb, seq_len)
    per_seq = seq_len // tb
    f = sgw.shape[1]
    return pl.pallas_call(
        _combine_kernel,
        grid=(t // tb,),
        in_specs=[pl.BlockSpec((tb, half), lambda i: (i, 0)),
                  pl.BlockSpec((tb, d), lambda i: (i, 0)),
                  pl.BlockSpec((1, 1, d), lambda i: (i // per_seq, 0, 0)),
                  pl.BlockSpec((tb, TOP_K), lambda i: (i, 0)),
                  pl.BlockSpec((d, f), lambda i: (0, 0)),
                  pl.BlockSpec((d, f), lambda i: (0, 0)),
                  pl.BlockSpec((f, d), lambda i: (0, 0)),
                  pl.BlockSpec((1, TOP_K, tb, half), lambda i: (i, 0, 0, 0))],
        out_specs=pl.BlockSpec((tb, d), lambda i: (i, 0)),
        out_shape=jax.ShapeDtypeStruct((t, d), F32),
        compiler_params=_cparams("arbitrary"),
        name="combine",
    )(hp, x1, g2, w_tok, sgw, suw, sdw, yg)


def _rope_tables(l):
    rows = l // GRID_W
    r = jnp.repeat(jnp.arange(rows), GRID_W).astype(F32)
    col = jnp.tile(jnp.arange(GRID_W), rows).astype(F32)
    n_f = HEAD_DIM // 4
    freqs = ROPE_THETA ** (-jnp.arange(n_f, dtype=F32) / n_f)
    ang = jnp.concatenate([r[:, None] * freqs, col[:, None] * freqs], axis=-1)
    ang = jnp.tile(jnp.repeat(ang, 2, axis=1), (1, LANES // HEAD_DIM))
    sign = jnp.where(jnp.arange(LANES) % 2 == 0, -1.0, 1.0).astype(F32)
    return jnp.cos(ang), jnp.sin(ang) * sign


def kernel(x, c, ctx, c_ctx, w_mod, b_mod, norm1_w, norm2_w, w_in, q_norm_w, k_norm_w, ret_decay_fwd,
           ret_decay_bwd, w_out, router_w, router_bias, exp_w_gate, exp_w_up, exp_w_down, sh_w_gate,
           sh_w_up, sh_w_down):
    b, l, d = x.shape
    lc = ctx.shape[1]
    t = b * l
    assert w_mod.shape[0] == 1, "single layer"
    assert l % CHUNK == 0 and lc % CHUNK == 0 and l % GRID_W == 0

    rows = (b + 1 + 7) // 8 * 8
    cc = jnp.zeros((rows, d), F32).at[:b].set(c).at[b].set(c_ctx)
    mod = _modulation(cc, w_mod[0], b_mod[0])
    sh1, sc1, g1, sh2, sc2, g2 = [mod[:b, i * d:(i + 1) * d].reshape(b, 1, d) for i in range(6)]
    shc = mod[b, 0:d].reshape(1, 1, d)
    scc = mod[b, d:2 * d].reshape(1, 1, d)

    wi = w_in[0].astype(BF16)
    qnw = jnp.tile(q_norm_w[0], LANES // HEAD_DIM).reshape(1, LANES)
    knw = jnp.tile(k_norm_w[0], LANES // HEAD_DIM).reshape(1, LANES)
    cos, sin = _rope_tables(l)
    n1 = norm1_w[0].reshape(1, d)

    cklo, ckhi, cvlo, cvhi, crk, crv = _projection(
        ctx, shc, scc, n1, wi, qnw, knw, cos[:lc], sin[:lc], rope=False, with_q=False, tm=TILE_PROJ)
    klo, khi, vlo, vhi, rk, rv, q, rq, sg = _projection(
        x, sh1, sc1, n1, wi, qnw, knw, cos, sin, rope=True, with_q=True, tm=TILE_PROJ)

    attn = _attention(q, klo, khi, vlo, vhi, cklo, ckhi, cvlo, cvhi, tq=TILE_ATTN_Q)
    dec_f = jnp.repeat(ret_decay_fwd[0].astype(F32), HEAD_DIM).reshape(1, RET_W)
    dec_b = jnp.repeat(ret_decay_bwd[0].astype(F32), HEAD_DIM).reshape(1, RET_W)
    ret = _retention(rq, rk, rv, sg, crk, crv, dec_f, dec_b)

    wo = w_out[0].astype(BF16)
    r_hi, r_lo = _split(router_w[0].T)
    x1, hp, logits_t = _out_projection(attn, ret, x, wo[:ATTN_W], wo[ATTN_W:], g1, sh2, sc2,
                                       norm2_w[0].reshape(1, d), r_hi, r_lo, tm=TILE_OUT)

    idx_t, w_t, rank_t, cnt_col, cnt_row = _route(logits_t, router_bias[0].reshape(N_EXPERTS, 1), tb=TILE_TOKENS)
    n_blocks = -(-(t * TOP_K) // MOE_BLOCK) + N_EXPERTS
    tb = TILE_TOKENS
    dest, meta, items = _destinations(idx_t, rank_t, cnt_col, cnt_row, tb=tb, n_blocks=n_blocks)
    dest_flat = dest.reshape(-1)
    xs = _dispatch(meta, hp, dest_flat, tb=tb, n_blocks=n_blocks)
    ys = _experts(items, xs, exp_w_gate[0], exp_w_up[0], exp_w_down[0], n_blocks=n_blocks)
    yg = _sc_gather_rows(ys, dest_flat).reshape(dest.shape + (d // 2,))
    out = _combine(hp, x1.reshape(t, d), g2, w_t.T, sh_w_gate[0].astype(BF16), sh_w_up[0].astype(BF16),
                   sh_w_down[0].astype(BF16), yg, tb=tb, seq_len=l)
    return out.reshape(b, l, d)
```

```python
import functools

import jax
import jax.numpy as jnp
from jax import lax
from jax.experimental import pallas as pl
from jax.experimental.pallas import tpu as pltpu
from jax.experimental.pallas import tpu_sc as plsc

F32 = jnp.float32
BF16 = jnp.bfloat16
I32 = jnp.int32
U32 = jnp.uint32
PACKED = jnp.int32

HEAD_DIM = 64
LANES = 128
SUBLANES = 8
ATTN_HEADS = 8
ATTN_KV_HEADS = 2
GQA = ATTN_HEADS // ATTN_KV_HEADS
RET_HEADS = 8
ATTN_W = ATTN_HEADS * HEAD_DIM
KV_W = ATTN_KV_HEADS * HEAD_DIM
RET_W = RET_HEADS * HEAD_DIM
RET_PAIRS = RET_W // LANES
CHUNK = 128
GRID_W = 64
ROPE_THETA = 10000.0
N_EXPERTS = 256
TOP_K = 8
N_GROUPS = 8
GROUP_SIZE = N_EXPERTS // N_GROUPS
TOPK_GROUPS = 4
ROUTED_SCALE = 2.5
MOE_BLOCK = 128
EPS = 1e-6
QK_SCALE = HEAD_DIM ** -0.5

OFF_AK = 0
OFF_AV = OFF_AK + KV_W
OFF_RK = OFF_AV + KV_W
OFF_RV = OFF_RK + RET_W
CTX_KV_COLS = OFF_RV + RET_W
OFF_AQ = CTX_KV_COLS
OFF_RQ = OFF_AQ + ATTN_W
OFF_RG = OFF_RQ + RET_W
IN_COLS = OFF_RG + RET_W

VMEM_LIMIT = 52 * 1024 * 1024

TILE_PROJ = 256
TILE_ATTN_Q = 256
TILE_OUT = 256
TILE_TOKENS = 256
ITEM_BLOCKS = 5
SC_WINDOW = 64
HI_MASK = 0xFFFF0000


def _cparams(*sem):
    return pltpu.CompilerParams(dimension_semantics=sem, vmem_limit_bytes=VMEM_LIMIT)


def _split(a):
    hi = a.astype(BF16)
    lo = (a - hi.astype(F32)).astype(BF16)
    return hi, lo


def _dot(a, b):
    return jnp.dot(a, b, preferred_element_type=F32)


def _dot_nt(a, b):
    return lax.dot_general(a, b, (((1,), (1,)), ((), ())), preferred_element_type=F32)


def _sigmoid(v):
    return 1.0 / (1.0 + jnp.exp(-v))


def _silu(v):
    return v * _sigmoid(v)


def _pack_halves(a, b):
    ua = lax.bitcast_convert_type(a.astype(BF16).astype(F32), U32)
    ub = lax.bitcast_convert_type(b.astype(BF16).astype(F32), U32)
    return lax.bitcast_convert_type((ua & jnp.uint32(HI_MASK)) | (ub >> 16), PACKED)


def _unpack_halves(p):
    u = lax.bitcast_convert_type(p, U32)
    a = lax.bitcast_convert_type(u & jnp.uint32(HI_MASK), F32)
    b = lax.bitcast_convert_type(u << 16, F32)
    return a, b


def _mod_kernel(c_ref, w_ref, b_ref, o_ref):
    s_hi, s_lo = _split(_silu(c_ref[...]))
    w_hi, w_lo = _split(w_ref[...])
    o_ref[...] = _dot(s_hi, w_hi) + _dot(s_hi, w_lo) + _dot(s_lo, w_hi) + b_ref[...]


def _modulation(cc, w_mod, b_mod):
    rows, d = cc.shape
    n = w_mod.shape[1]
    tn = 768
    return pl.pallas_call(
        _mod_kernel,
        grid=(n // tn,),
        in_specs=[pl.BlockSpec((rows, d), lambda j: (0, 0)),
                  pl.BlockSpec((d, tn), lambda j: (0, j)),
                  pl.BlockSpec((1, tn), lambda j: (0, j))],
        out_specs=pl.BlockSpec((rows, tn), lambda j: (0, j)),
        out_shape=jax.ShapeDtypeStruct((rows, n), F32),
        compiler_params=_cparams("arbitrary"),
        name="mod",
    )(cc, w_mod, b_mod.reshape(1, n))


def _segment_ones():
    r = lax.broadcasted_iota(I32, (LANES, LANES), 0) // HEAD_DIM
    c = lax.broadcasted_iota(I32, (LANES, LANES), 1) // HEAD_DIM
    return jnp.where(r == c, 1.0, 0.0).astype(BF16)


def _head_mean_sq(v, seg):
    hi, lo = _split(v * v)
    return (_dot(hi, seg) + _dot(lo, seg)) * (1.0 / HEAD_DIM)


def _proj_kernel(x_ref, sh_ref, sc_ref, nw_ref, wi_ref, qnw_ref, knw_ref, cos_ref, sin_ref,
                 *out_refs, rope, with_q):
    if with_q:
        klo_ref, khi_ref, vlo_ref, vhi_ref, rk_ref, rv_ref, q_ref, rq_ref, sg_ref = out_refs
    else:
        klo_ref, khi_ref, vlo_ref, vhi_ref, rk_ref, rv_ref = out_refs
    x = x_ref[0]
    h = x * lax.rsqrt(jnp.mean(x * x, axis=-1, keepdims=True) + EPS) * nw_ref[...]
    h = h * (1.0 + sc_ref[0]) + sh_ref[0]
    z = _dot(h.astype(BF16), wi_ref[...])

    seg = _segment_ones()
    lane = lax.broadcasted_iota(I32, (x.shape[0], LANES), 1)
    low_half = lane < HEAD_DIM
    even = (lane & 1) == 0

    def norm_rope(v, w128):
        v = v * lax.rsqrt(_head_mean_sq(v, seg) + EPS) * w128
        if rope:
            swapped = jnp.where(even, pltpu.roll(v, LANES - 1, 1), pltpu.roll(v, 1, 1))
            v = v * cos_ref[...] + swapped * sin_ref[...]
        return v

    def write_lo_hi(v, lo_ref, hi_ref):
        sw = pltpu.roll(v, HEAD_DIM, 1)
        lo_ref[0, 0] = jnp.where(low_half, v, 0.0).astype(BF16)
        hi_ref[0, 0] = jnp.where(low_half, 0.0, sw).astype(BF16)
        lo_ref[0, 1] = jnp.where(low_half, sw, 0.0).astype(BF16)
        hi_ref[0, 1] = jnp.where(low_half, 0.0, v).astype(BF16)

    write_lo_hi(norm_rope(z[:, OFF_AK:OFF_AK + KV_W], knw_ref[...]), klo_ref, khi_ref)
    write_lo_hi(z[:, OFF_AV:OFF_AV + KV_W], vlo_ref, vhi_ref)
    rk_ref[0] = (z[:, OFF_RK:OFF_RK + RET_W] * QK_SCALE).astype(BF16)
    rv_ref[0] = z[:, OFF_RV:OFF_RV + RET_W].astype(BF16)
    if with_q:
        for j in range(ATTN_W // LANES):
            qj = norm_rope(z[:, OFF_AQ + j * LANES:OFF_AQ + (j + 1) * LANES], qnw_ref[...])
            q_ref[0, :, j * LANES:(j + 1) * LANES] = (qj * QK_SCALE).astype(BF16)
        rq_ref[0] = z[:, OFF_RQ:OFF_RQ + RET_W].astype(BF16)
        sg_ref[0] = _silu(z[:, OFF_RG:OFF_RG + RET_W]).astype(BF16)


def _projection(x, shift, scale, norm_w, wi_bf16, qnw, knw, cos, sin, *, rope, with_q, tm):
    b, l, d = x.shape
    tm = min(tm, l)
    ncols = IN_COLS if with_q else CTX_KV_COLS
    per_batch = shift.shape[0] > 1
    mod_idx = (lambda bi, i: (bi, 0, 0)) if per_batch else (lambda bi, i: (0, 0, 0))
    kv_shape = jax.ShapeDtypeStruct((b, ATTN_KV_HEADS, l, LANES), BF16)
    kv_spec = pl.BlockSpec((1, ATTN_KV_HEADS, tm, LANES), lambda bi, i: (bi, 0, i, 0))
    w_shape = jax.ShapeDtypeStruct((b, l, RET_W), BF16)
    w_spec = pl.BlockSpec((1, tm, RET_W), lambda bi, i: (bi, i, 0))
    out_shape = [kv_shape] * 4 + [w_shape] * 2
    out_specs = [kv_spec] * 4 + [w_spec] * 2
    if with_q:
        out_shape += [w_shape] * 3
        out_specs += [w_spec] * 3
    return pl.pallas_call(
        functools.partial(_proj_kernel, rope=rope, with_q=with_q),
        grid=(b, l // tm),
        in_specs=[pl.BlockSpec((1, tm, d), lambda bi, i: (bi, i, 0)),
                  pl.BlockSpec((1, 1, d), mod_idx),
                  pl.BlockSpec((1, 1, d), mod_idx),
                  pl.BlockSpec((1, d), lambda bi, i: (0, 0)),
                  pl.BlockSpec((d, ncols), lambda bi, i: (0, 0)),
                  pl.BlockSpec((1, LANES), lambda bi, i: (0, 0)),
                  pl.BlockSpec((1, LANES), lambda bi, i: (0, 0)),
                  pl.BlockSpec((tm, LANES), lambda bi, i: (i, 0)),
                  pl.BlockSpec((tm, LANES), lambda bi, i: (i, 0))],
        out_specs=out_specs,
        out_shape=out_shape,
        compiler_params=_cparams("arbitrary", "arbitrary"),
        name="proj_latent" if with_q else "proj_ctx",
    )(x, shift, scale, norm_w, wi_bf16, qnw, knw, cos, sin)


def _attn_kernel(q_ref, klo_ref, khi_ref, vlo_ref, vhi_ref, cklo_ref, ckhi_ref, cvlo_ref, cvhi_ref,
                 o_ref, kl_s, kh_s, va_s, *, l, lc):
    lk = l + lc

    @pl.when(pl.program_id(2) == 0)
    def _():
        kl_s[0:l] = klo_ref[0, 0]
        kl_s[l:lk] = cklo_ref[0, 0]
        kh_s[0:l] = khi_ref[0, 0]
        kh_s[l:lk] = ckhi_ref[0, 0]
        lane = lax.broadcasted_iota(I32, (lk, LANES), 1)
        ones_lo = jnp.where(lane < HEAD_DIM, 1.0, 0.0).astype(BF16)
        ones_hi = jnp.where(lane < HEAD_DIM, 0.0, 1.0).astype(BF16)
        for g, (v_ref, cv_ref, ones) in enumerate(((vlo_ref, cvlo_ref, ones_lo), (vhi_ref, cvhi_ref, ones_hi),
                                                   (vlo_ref, cvlo_ref, ones_lo), (vhi_ref, cvhi_ref, ones_hi))):
            v_col, one_col = (0, LANES) if g < 2 else (LANES, 0)
            va_s[g, 0:l, v_col:v_col + LANES] = v_ref[0, 0]
            va_s[g, l:lk, v_col:v_col + LANES] = cv_ref[0, 0]
            va_s[g, :, one_col:one_col + LANES] = ones

    q = q_ref[0]
    acc = []
    for g in range(GQA):
        qp = q[:, (g // 2) * LANES:(g // 2 + 1) * LANES]
        kk = kl_s[...] if g % 2 == 0 else kh_s[...]
        s = _dot_nt(qp, kk)
        p = jnp.exp(s - jnp.max(s, axis=-1, keepdims=True)).astype(BF16)
        acc.append(_dot(p, va_s[g]))
    out_a = acc[0] + acc[1]
    out_b = acc[2] + acc[3]
    o_ref[0, :, 0:LANES] = (out_a[:, 0:LANES] / out_a[:, LANES:2 * LANES]).astype(BF16)
    o_ref[0, :, LANES:2 * LANES] = (out_b[:, LANES:2 * LANES] / out_b[:, 0:LANES]).astype(BF16)


def _attention(q, klo, khi, vlo, vhi, cklo, ckhi, cvlo, cvhi, *, tq):
    b, l, _ = q.shape
    lc = cklo.shape[2]
    lk = l + lc
    tq = min(tq, l)
    gw = GQA * HEAD_DIM
    kv_spec = pl.BlockSpec((1, 1, l, LANES), lambda bi, h, i: (bi, h, 0, 0))
    ckv_spec = pl.BlockSpec((1, 1, lc, LANES), lambda bi, h, i: (bi, h, 0, 0))
    return pl.pallas_call(
        functools.partial(_attn_kernel, l=l, lc=lc),
        grid=(b, ATTN_KV_HEADS, l // tq),
        in_specs=[pl.BlockSpec((1, tq, gw), lambda bi, h, i: (bi, i, h))] + [kv_spec] * 4 + [ckv_spec] * 4,
        out_specs=pl.BlockSpec((1, tq, gw), lambda bi, h, i: (bi, i, h)),
        out_shape=jax.ShapeDtypeStruct((b, l, ATTN_W), BF16),
        scratch_shapes=[pltpu.VMEM((lk, LANES), BF16), pltpu.VMEM((lk, LANES), BF16),
                        pltpu.VMEM((GQA, lk, 2 * LANES), BF16)],
        compiler_params=_cparams("arbitrary", "arbitrary", "arbitrary"),
        name="attn",
    )(q, klo, khi, vlo, vhi, cklo, ckhi, cvlo, cvhi)


def _log_sigmoid(v):
    return jnp.minimum(v, 0.0) - jnp.log(1.0 + jnp.exp(-jnp.abs(v)))


def _ret_kernel(rq_ref, rk_ref, rv_ref, sg_ref, crk_ref, crv_ref, df_ref, db_ref, o_ref,
                m_s, tab_s, sb_s, sf_s, *, l, lc):
    n = l // CHUNK
    nc = lc // CHUNK
    lgf = _log_sigmoid(df_ref[...])
    lgb = _log_sigmoid(db_ref[...])
    pos = lax.broadcasted_iota(I32, (CHUNK, RET_W), 0).astype(F32)
    tab_s[0] = jnp.exp((pos + 1.0) * lgf)
    tab_s[1] = jnp.exp((CHUNK - pos) * lgb)
    tab_s[2] = jnp.exp((CHUNK - 1.0 - pos) * lgf)
    tab_s[3] = jnp.exp(pos * lgb)
    gf_c = jnp.exp(CHUNK * lgf)
    gb_c = jnp.exp(CHUNK * lgb)
    row = lax.broadcasted_iota(I32, (CHUNK, CHUNK), 0)
    col = lax.broadcasted_iota(I32, (CHUNK, CHUNK), 1)
    diff = (row - col).astype(F32)
    for h in range(RET_HEADS):
        lf = lgf[:, h * HEAD_DIM:h * HEAD_DIM + 1]
        lb = lgb[:, h * HEAD_DIM:h * HEAD_DIM + 1]
        m_s[h] = jnp.where(diff > 0, jnp.exp(diff * lf), jnp.where(diff < 0, jnp.exp(-diff * lb), 2.0))

    lane = lax.broadcasted_iota(I32, (CHUNK, LANES), 1)
    low_half = lane < HEAD_DIM
    diag = (lax.broadcasted_iota(I32, (LANES, LANES), 0) // HEAD_DIM
            == lax.broadcasted_iota(I32, (LANES, LANES), 1) // HEAD_DIM)
    seg = jnp.where(diag, 1.0, 0.0).astype(BF16)

    def contrib(k_ref, v_ref, r0, p, zeta_idx):
        cols = slice(p * LANES, (p + 1) * LANES)
        kz = k_ref[0, pl.ds(r0, CHUNK), cols].astype(F32) * tab_s[zeta_idx, :, cols]
        kv = _dot(jnp.transpose(kz).astype(BF16), v_ref[0, pl.ds(r0, CHUNK), cols])
        return jnp.where(diag, kv, 0.0)

    for p in range(RET_PAIRS):
        cols = slice(p * LANES, (p + 1) * LANES)
        sf = jnp.zeros((LANES, LANES), F32)
        sb = jnp.zeros((LANES, LANES), F32)
        for c in range(nc):
            sf = gf_c[:, cols] * sf + contrib(crk_ref, crv_ref, c * CHUNK, p, 2)
            cb = nc - 1 - c
            sb = gb_c[:, cols] * sb + contrib(crk_ref, crv_ref, cb * CHUNK, p, 3)
        sf_s[p] = sf
        sb_s[n, p] = sb

    def bwd_body(j, carry):
        c = n - 1 - j
        r0 = pl.multiple_of(c * CHUNK, CHUNK)
        for p in range(RET_PAIRS):
            cols = slice(p * LANES, (p + 1) * LANES)
            prev = sb_s[c + 1, p]
            sb_s[c, p] = gb_c[:, cols] * prev + contrib(rk_ref, rv_ref, r0, p, 3)
        return carry

    lax.fori_loop(0, n, bwd_body, 0)

    def fwd_body(c, carry):
        r0 = pl.multiple_of(c * CHUNK, CHUNK)
        for p in range(RET_PAIRS):
            cols = slice(p * LANES, (p + 1) * LANES)
            qp = rq_ref[0, pl.ds(r0, CHUNK), cols]
            kp = rk_ref[0, pl.ds(r0, CHUNK), cols]
            vp = rv_ref[0, pl.ds(r0, CHUNK), cols]
            zero = jnp.zeros_like(kp)
            s0 = _dot_nt(qp, jnp.where(low_half, kp, zero))
            s1 = _dot_nt(qp, jnp.where(low_half, zero, kp))
            a0 = (s0 * m_s[2 * p]).astype(BF16)
            a1 = (s1 * m_s[2 * p + 1]).astype(BF16)
            y = _dot(a0, jnp.where(low_half, vp, zero)) + _dot(a1, jnp.where(low_half, zero, vp))
            qf = qp.astype(F32)
            sf = sf_s[p]
            y += _dot((qf * tab_s[0, :, cols]).astype(BF16), sf.astype(BF16))
            y += _dot((qf * tab_s[1, :, cols]).astype(BF16), sb_s[c + 1, p].astype(BF16))
            hi, lo = _split(y * y)
            ms = (_dot(hi, seg) + _dot(lo, seg)) * (1.0 / HEAD_DIM)
            out = y * lax.rsqrt(ms + EPS) * sg_ref[0, pl.ds(r0, CHUNK), cols].astype(F32)
            o_ref[0, pl.ds(r0, CHUNK), cols] = out.astype(BF16)
            sf_s[p] = gf_c[:, cols] * sf + contrib(rk_ref, rv_ref, r0, p, 2)
        return carry

    lax.fori_loop(0, n, fwd_body, 0)


def _retention(rq, rk, rv, sg, crk, crv, dec_f, dec_b):
    b, l, _ = rq.shape
    lc = crk.shape[1]
    n = l // CHUNK
    spec = pl.BlockSpec((1, l, RET_W), lambda bi: (bi, 0, 0))
    cspec = pl.BlockSpec((1, lc, RET_W), lambda bi: (bi, 0, 0))
    dspec = pl.BlockSpec((1, RET_W), lambda bi: (0, 0))
    return pl.pallas_call(
        functools.partial(_ret_kernel, l=l, lc=lc),
        grid=(b,),
        in_specs=[spec, spec, spec, spec, cspec, cspec, dspec, dspec],
        out_specs=spec,
        out_shape=jax.ShapeDtypeStruct((b, l, RET_W), BF16),
        scratch_shapes=[pltpu.VMEM((RET_HEADS, CHUNK, CHUNK), F32),
                        pltpu.VMEM((4, CHUNK, RET_W), F32),
                        pltpu.VMEM((n + 1, RET_PAIRS, LANES, LANES), F32),
                        pltpu.VMEM((RET_PAIRS, LANES, LANES), F32)],
        compiler_params=_cparams("arbitrary"),
        name="ret",
    )(rq, rk, rv, sg, crk, crv, dec_f, dec_b)


def _out_kernel(attn_ref, ret_ref, x_ref, wa_ref, wr_ref, g1_ref, sh_ref, sc_ref, nw_ref, rhi_ref, rlo_ref,
                x1_ref, hp_ref, lg_ref):
    y = _dot(attn_ref[0], wa_ref[...]) + _dot(ret_ref[0], wr_ref[...])
    x1 = x_ref[0] + g1_ref[0] * y
    x1_ref[0] = x1
    h = x1 * lax.rsqrt(jnp.mean(x1 * x1, axis=-1, keepdims=True) + EPS) * nw_ref[...]
    h = h * (1.0 + sc_ref[0]) + sh_ref[0]
    half = h.shape[1] // 2
    hp_ref[...] = _pack_halves(h[:, :half], h[:, half:])
    h_hi, h_lo = _split(h)
    lg_ref[...] = _dot_nt(rhi_ref[...], h_hi) + _dot_nt(rhi_ref[...], h_lo) + _dot_nt(rlo_ref[...], h_hi)


def _out_projection(attn, ret, x, wa, wr, g1, sh2, sc2, norm_w, r_hi, r_lo, *, tm):
    b, l, d = x.shape
    tm = min(tm, l)
    nt = l // tm
    t = b * l
    mspec = pl.BlockSpec((1, 1, d), lambda bi, i: (bi, 0, 0))
    return pl.pallas_call(
        _out_kernel,
        grid=(b, nt),
        in_specs=[pl.BlockSpec((1, tm, ATTN_W), lambda bi, i: (bi, i, 0)),
                  pl.BlockSpec((1, tm, RET_W), lambda bi, i: (bi, i, 0)),
                  pl.BlockSpec((1, tm, d), lambda bi, i: (bi, i, 0)),
                  pl.BlockSpec((ATTN_W, d), lambda bi, i: (0, 0)),
                  pl.BlockSpec((RET_W, d), lambda bi, i: (0, 0)),
                  mspec, mspec, mspec,
                  pl.BlockSpec((1, d), lambda bi, i: (0, 0)),
                  pl.BlockSpec((N_EXPERTS, d), lambda bi, i: (0, 0)),
                  pl.BlockSpec((N_EXPERTS, d), lambda bi, i: (0, 0))],
        out_specs=[pl.BlockSpec((1, tm, d), lambda bi, i: (bi, i, 0)),
                   pl.BlockSpec((tm, d // 2), lambda bi, i: (bi * nt + i, 0)),
                   pl.BlockSpec((N_EXPERTS, tm), lambda bi, i: (0, bi * nt + i))],
        out_shape=[jax.ShapeDtypeStruct((b, l, d), F32),
                   jax.ShapeDtypeStruct((t, d // 2), PACKED),
                   jax.ShapeDtypeStruct((N_EXPERTS, t), F32)],
        compiler_params=_cparams("arbitrary", "arbitrary"),
        name="out_proj",
    )(attn, ret, x, wa, wr, g1, sh2, sc2, norm_w, r_hi, r_lo)


def _route_kernel(lg_ref, bias_ref, idx_ref, w_ref, rank_ref, cnt_col_ref, cnt_row_ref, tri_s, col_s, row_s):
    tb = lg_ref.shape[1]
    step = pl.program_id(0)

    @pl.when(step == 0)
    def _():
        r = lax.broadcasted_iota(I32, (tb, tb), 0)
        c = lax.broadcasted_iota(I32, (tb, tb), 1)
        tri_s[...] = jnp.where(r <= c, 1.0, 0.0).astype(BF16)
        col_s[...] = jnp.zeros_like(col_s)
        row_s[...] = jnp.zeros_like(row_s)

    scores = _sigmoid(lg_ref[...])
    biased = scores + bias_ref[...]
    neg = -jnp.inf
    sub = lax.broadcasted_iota(I32, (GROUP_SIZE, tb), 0).astype(F32)

    gscore = []
    for g in range(N_GROUPS):
        blk = biased[g * GROUP_SIZE:(g + 1) * GROUP_SIZE]
        m1 = jnp.max(blk, axis=0, keepdims=True)
        first = jnp.min(jnp.where(blk == m1, sub, float(GROUP_SIZE)), axis=0, keepdims=True)
        m2 = jnp.max(jnp.where(sub == first, neg, blk), axis=0, keepdims=True)
        gscore.append(m1 + m2)
    gs = jnp.concatenate(gscore, axis=0)
    gsub = lax.broadcasted_iota(I32, (N_GROUPS, tb), 0).astype(F32)
    keep = jnp.zeros((N_GROUPS, tb), F32)
    for _ in range(TOPK_GROUPS):
        m = jnp.max(gs, axis=0, keepdims=True)
        first = jnp.min(jnp.where(gs == m, gsub, float(N_GROUPS)), axis=0, keepdims=True)
        sel = gsub == first
        keep = jnp.where(sel, 1.0, keep)
        gs = jnp.where(sel, neg, gs)
    masked = jnp.concatenate(
        [jnp.where(keep[g:g + 1] > 0.0, biased[g * GROUP_SIZE:(g + 1) * GROUP_SIZE], neg)
         for g in range(N_GROUPS)], axis=0)

    esub = lax.broadcasted_iota(I32, (N_EXPERTS, tb), 0).astype(F32)
    sels, idxs, ws = [], [], []
    chosen = jnp.zeros((N_EXPERTS, tb), F32)
    for _ in range(TOP_K):
        m = jnp.max(masked, axis=0, keepdims=True)
        first = jnp.min(jnp.where(masked == m, esub, float(N_EXPERTS)), axis=0, keepdims=True)
        sel = esub == first
        sels.append(sel)
        idxs.append(first)
        ws.append(jnp.sum(jnp.where(sel, scores, 0.0), axis=0, keepdims=True))
        chosen = jnp.where(sel, 1.0, chosen)
        masked = jnp.where(sel, neg, masked)
    wsum = ws[0]
    for k in range(1, TOP_K):
        wsum = wsum + ws[k]
    idx_ref[...] = jnp.concatenate(idxs, axis=0).astype(I32)
    w_ref[...] = jnp.concatenate([wk / wsum * ROUTED_SCALE for wk in ws], axis=0)

    chosen_b = chosen.astype(BF16)
    incl = _dot(chosen_b, tri_s[...])
    before = incl - chosen + col_s[...]
    rank_ref[...] = jnp.concatenate(
        [jnp.sum(jnp.where(sel, before, 0.0), axis=0, keepdims=True) for sel in sels], axis=0).astype(I32)
    col_s[...] = col_s[...] + incl[:, tb - 1:tb]
    row_s[...] = row_s[...] + _dot_nt(jnp.ones((8, tb), BF16), chosen_b)
    cnt_col_ref[...] = col_s[...].astype(I32)
    cnt_row_ref[...] = row_s[...].astype(I32)


def _route(logits_t, bias_col, *, tb):
    e, t = logits_t.shape
    tb = min(tb, t)
    kspec = pl.BlockSpec((TOP_K, tb), lambda i: (0, i))
    return pl.pallas_call(
        _route_kernel,
        grid=(t // tb,),
        in_specs=[pl.BlockSpec((e, tb), lambda i: (0, i)),
                  pl.BlockSpec((e, 1), lambda i: (0, 0))],
        out_specs=[kspec, kspec, kspec,
                   pl.BlockSpec((e, 1), lambda i: (0, 0)),
                   pl.BlockSpec((8, e), lambda i: (0, 0))],
        out_shape=[jax.ShapeDtypeStruct((TOP_K, t), I32),
                   jax.ShapeDtypeStruct((TOP_K, t), F32),
                   jax.ShapeDtypeStruct((TOP_K, t), I32),
                   jax.ShapeDtypeStruct((e, 1), I32),
                   jax.ShapeDtypeStruct((8, e), I32)],
        scratch_shapes=[pltpu.VMEM((tb, tb), BF16), pltpu.VMEM((e, 1), F32), pltpu.VMEM((8, e), F32)],
        compiler_params=_cparams("arbitrary"),
        name="route",
    )(logits_t, bias_col)


def _pad_block(cnt):
    return (cnt + (MOE_BLOCK - 1)) // MOE_BLOCK * MOE_BLOCK


def _max_items(n_blocks):
    return n_blocks // ITEM_BLOCKS + N_EXPERTS


def _dest_kernel(idx_ref, rank_ref, cnt_col_ref, cnt_row_ref, dest_ref, meta_ref, items_ref):
    tb = idx_ref.shape[1]
    nip = items_ref.shape[1]
    pad_col = _pad_block(cnt_col_ref[...])
    pad_row = _pad_block(cnt_row_ref[0:1, :])
    er = lax.broadcasted_iota(I32, (N_EXPERTS, N_EXPERTS), 0)
    ec = lax.broadcasted_iota(I32, (N_EXPERTS, N_EXPERTS), 1)
    start_col = jnp.sum(jnp.where(ec < er, pad_row, 0), axis=1, keepdims=True)
    start_row = jnp.sum(jnp.where(er < ec, pad_col, 0), axis=0, keepdims=True)

    esub = lax.broadcasted_iota(I32, (N_EXPERTS, tb), 0)
    rows = []
    for k in range(TOP_K):
        onehot = esub == idx_ref[k:k + 1, :]
        rows.append(jnp.sum(jnp.where(onehot, start_col, 0), axis=0, keepdims=True) + rank_ref[k:k + 1, :])
    dest_ref[0] = jnp.concatenate(rows, axis=0)

    used = jnp.sum(pad_row, axis=1, keepdims=True) // MOE_BLOCK
    meta_ref[...] = jnp.concatenate(
        [cnt_row_ref[0:1, :], start_row, pad_row, jnp.broadcast_to(used, (1, N_EXPERTS)),
         jnp.zeros((4, N_EXPERTS), I32)], axis=0)

    nb_col = pad_col // MOE_BLOCK
    it_col = (nb_col + (ITEM_BLOCKS - 1)) // ITEM_BLOCKS
    it_row = (pad_row // MOE_BLOCK + (ITEM_BLOCKS - 1)) // ITEM_BLOCKS
    it_start = jnp.sum(jnp.where(ec < er, it_row, 0), axis=1, keepdims=True)
    n_items = jnp.sum(it_row, axis=1, keepdims=True)
    lane = lax.broadcasted_iota(I32, (1, nip), 1)
    owner = jnp.sum(jnp.where(it_start + it_col <= lane, 1, 0), axis=0, keepdims=True)
    owner = jnp.minimum(owner, N_EXPERTS - 1)
    onehot = lax.broadcasted_iota(I32, (N_EXPERTS, nip), 0) == owner

    def pick(col):
        return jnp.sum(jnp.where(onehot, col, 0), axis=0, keepdims=True)

    j = lane - pick(it_start)
    block0 = pick(start_col) // MOE_BLOCK + ITEM_BLOCKS * j
    nvalid = jnp.clip(pick(nb_col) - ITEM_BLOCKS * j, 0, ITEM_BLOCKS)
    items_ref[...] = jnp.concatenate(
        [owner, block0, jnp.where(lane < n_items, nvalid, 0), jnp.broadcast_to(n_items, (1, nip)),
         jnp.zeros((4, nip), I32)], axis=0)


def _destinations(idx_t, rank_t, cnt_col, cnt_row, *, tb, n_blocks):
    _, t = idx_t.shape
    tb = min(tb, t)
    nip = (_max_items(n_blocks) + LANES - 1) // LANES * LANES
    kspec = pl.BlockSpec((TOP_K, tb), lambda i: (0, i))
    return pl.pallas_call(
        _dest_kernel,
        grid=(t // tb,),
        in_specs=[kspec, kspec,
                  pl.BlockSpec((N_EXPERTS, 1), lambda i: (0, 0)),
                  pl.BlockSpec((8, N_EXPERTS), lambda i: (0, 0))],
        out_specs=[pl.BlockSpec((1, TOP_K, tb), lambda i: (i, 0, 0)),
                   pl.BlockSpec((8, N_EXPERTS), lambda i: (0, 0)),
                   pl.BlockSpec((8, nip), lambda i: (0, 0))],
        out_shape=[jax.ShapeDtypeStruct((t // tb, TOP_K, tb), I32),
                   jax.ShapeDtypeStruct((8, N_EXPERTS), I32),
                   jax.ShapeDtypeStruct((8, nip), I32)],
        compiler_params=_cparams("arbitrary"),
        name="dest",
    )(idx_t, rank_t, cnt_col, cnt_row)


_PAD_BITS = (64, 32, 16, 8)


def _sc_scatter_rows(rows, dest_win, n_out_rows):
    n_win, n_slots, win = dest_win.shape
    width = rows.shape[1]
    info = plsc.get_sparse_core_info()
    n_workers = info.num_cores * info.num_subcores
    per_worker = n_win // n_workers
    assert per_worker * n_workers == n_win and win <= LANES
    mesh = plsc.VectorSubcoreMesh(core_axis_name="c", subcore_axis_name="s")

    def body(rows_hbm, dest_hbm, out_hbm, idx_v, rows_v, sem):
        wid = lax.axis_index("s") * info.num_cores + lax.axis_index("c")

        @pl.loop(0, per_worker)
        def _(j):
            w = wid * per_worker + j
            pltpu.sync_copy(dest_hbm.at[w], idx_v)
            pltpu.sync_copy(rows_hbm.at[pl.ds(w * win, win)], rows_v)
            copies = [pltpu.async_copy(rows_v, out_hbm.at[idx_v.at[k]], sem) for k in range(n_slots)]
            for cp in copies:
                cp.wait()

    return pl.kernel(
        body,
        out_type=jax.ShapeDtypeStruct((n_out_rows, width), rows.dtype),
        mesh=mesh,
        scratch_types=[pltpu.VMEM((n_slots, win), I32), pltpu.VMEM((win, width), rows.dtype),
                       pltpu.SemaphoreType.DMA],
        name="sc_scatter",
    )(rows, dest_win)


def _pad_fill_kernel(meta_ref, xs_in, xs_hbm, zero_s, sem_z, *, e_per_step):
    del xs_in
    step = pl.program_id(0)
    zero_s[...] = jnp.zeros_like(zero_s)

    def tail_copy(c):
        row0 = pl.multiple_of((meta_ref[3, 0] + c) * MOE_BLOCK, MOE_BLOCK)
        return pltpu.make_async_copy(zero_s, xs_hbm.at[pl.ds(row0, MOE_BLOCK)], sem_z)

    @pl.when(step == 0)
    def _():
        for c in range(ITEM_BLOCKS - 1):
            tail_copy(c).start()
        for c in range(ITEM_BLOCKS - 1):
            tail_copy(c).wait()

    def pad_copies(e):
        cnt = meta_ref[0, e]
        off = meta_ref[1, e] + cnt
        rem = meta_ref[2, e] - cnt
        head = rem & (SUBLANES - 1)
        out = []
        for i in range(SUBLANES - 1):
            out.append((i < head,
                        pltpu.make_async_copy(zero_s.at[pl.ds(0, 1)], xs_hbm.at[pl.ds(off + i, 1)], sem_z)))
        off = off + head
        for bit in _PAD_BITS:
            out.append(((rem & bit) != 0,
                        pltpu.make_async_copy(zero_s.at[pl.ds(0, bit)],
                                              xs_hbm.at[pl.ds(pl.multiple_of(off, SUBLANES), bit)], sem_z)))
            off = off + (rem & bit)
        return out

    for j in range(e_per_step):
        for cond, c in pad_copies(step * e_per_step + j):
            pl.when(cond)(c.start)
    for j in range(e_per_step):
        for cond, c in pad_copies(step * e_per_step + j):
            pl.when(cond)(c.wait)


PAD_FILL_EXPERTS_PER_STEP = 8


def _pad_fill(meta, xs):
    half = xs.shape[1]
    return pl.pallas_call(
        functools.partial(_pad_fill_kernel, e_per_step=PAD_FILL_EXPERTS_PER_STEP),
        grid=(N_EXPERTS // PAD_FILL_EXPERTS_PER_STEP,),
        in_specs=[pl.BlockSpec(memory_space=pltpu.SMEM),
                  pl.BlockSpec(memory_space=pl.ANY)],
        out_specs=pl.BlockSpec(memory_space=pl.ANY),
        out_shape=jax.ShapeDtypeStruct(xs.shape, xs.dtype),
        input_output_aliases={1: 0},
        scratch_shapes=[pltpu.VMEM((MOE_BLOCK, half), PACKED), pltpu.SemaphoreType.DMA],
        compiler_params=_cparams("arbitrary"),
        name="pad_fill",
    )(meta, xs)


def _experts_kernel(items_ref, xs_hbm, wg_ref, wu_ref, wd_ref, ys_hbm,
                    xbuf, ybuf, wg_s, wu_s, wd_s, sem_x, sem_y):
    i = pl.program_id(0)
    n_items = items_ref[3, 0]
    slot = i % 2
    rows = ITEM_BLOCKS * MOE_BLOCK

    def x_copy(item, s):
        row0 = pl.multiple_of(items_ref[1, item] * MOE_BLOCK, MOE_BLOCK)
        return pltpu.make_async_copy(xs_hbm.at[pl.ds(row0, rows)], xbuf.at[s], sem_x.at[s])

    def y_copies(item, s, fn):
        for k in range(ITEM_BLOCKS):
            row0 = pl.multiple_of((items_ref[1, item] + k) * MOE_BLOCK, MOE_BLOCK)
            cp = pltpu.make_async_copy(ybuf.at[s, pl.ds(k * MOE_BLOCK, MOE_BLOCK)],
                                       ys_hbm.at[pl.ds(row0, MOE_BLOCK)], sem_y.at[s])
            pl.when(k < items_ref[2, item])(functools.partial(fn, cp))

    @pl.when(i == 0)
    def _():
        x_copy(0, 0).start()

    @pl.when(i + 1 < n_items)
    def _():
        x_copy(i + 1, 1 - slot).start()

    @pl.when(i < n_items)
    def _():
        prev = jnp.maximum(i - 1, 0)

        @pl.when((i == 0) | (items_ref[0, i] != items_ref[0, prev]))
        def _():
            wg_s[...] = wg_ref[0].astype(BF16)
            wu_s[...] = wu_ref[0].astype(BF16)
            wd_s[...] = wd_ref[0].astype(BF16)

        x_copy(i, slot).wait()
        xa, xb = _unpack_halves(xbuf[slot])
        xa = xa.astype(BF16)
        xb = xb.astype(BF16)
        half = xa.shape[1]
        g = _dot(xa, wg_s[0:half]) + _dot(xb, wg_s[half:])
        u = _dot(xa, wu_s[0:half]) + _dot(xb, wu_s[half:])
        y = _dot((_silu(g) * u).astype(BF16), wd_s[...])
        ybuf[slot] = _pack_halves(y[:, :half], y[:, half:])
        y_copies(i, slot, lambda cp: cp.start())

        @pl.when(i > 0)
        def _():
            y_copies(prev, 1 - slot, lambda cp: cp.wait())

        @pl.when(i == n_items - 1)
        def _():
            y_copies(i, slot, lambda cp: cp.wait())


def _experts(items, xs, w_gate, w_up, w_down, *, n_blocks):
    half = xs.shape[1]
    e, d, f = w_gate.shape
    rows = ITEM_BLOCKS * MOE_BLOCK

    def w_idx(i, items_ref):
        return (items_ref[0, jnp.minimum(i, items_ref[3, 0] - 1)], 0, 0)

    return pl.pallas_call(
        _experts_kernel,
        grid_spec=pltpu.PrefetchScalarGridSpec(
            num_scalar_prefetch=1,
            grid=(_max_items(n_blocks),),
            in_specs=[pl.BlockSpec(memory_space=pl.ANY),
                      pl.BlockSpec((1, d, f), w_idx),
                      pl.BlockSpec((1, d, f), w_idx),
                      pl.BlockSpec((1, f, d), w_idx)],
            out_specs=pl.BlockSpec(memory_space=pl.ANY),
            scratch_shapes=[pltpu.VMEM((2, rows, half), PACKED), pltpu.VMEM((2, rows, half), PACKED),
                            pltpu.VMEM((d, f), BF16), pltpu.VMEM((d, f), BF16), pltpu.VMEM((f, d), BF16),
                            pltpu.SemaphoreType.DMA((2,)), pltpu.SemaphoreType.DMA((2,))]),
        out_shape=jax.ShapeDtypeStruct((n_blocks * MOE_BLOCK, half), PACKED),
        compiler_params=_cparams("arbitrary"),
        name="experts",
    )(items, xs, w_gate, w_up, w_down)


def _sc_gather_rows(table, idx):
    n_idx = idx.shape[0]
    width = table.shape[1]
    info = plsc.get_sparse_core_info()
    n_workers = info.num_cores * info.num_subcores
    per_worker = n_idx // n_workers
    assert per_worker * n_workers == n_idx and per_worker % SC_WINDOW == 0
    mesh = plsc.VectorSubcoreMesh(core_axis_name="c", subcore_axis_name="s")

    def body(table_hbm, idx_hbm, out_hbm, idx_v, rows_v, sem):
        wid = lax.axis_index("s") * info.num_cores + lax.axis_index("c")
        base = wid * per_worker

        @pl.loop(0, per_worker // SC_WINDOW)
        def _(j):
            off = base + j * SC_WINDOW
            pltpu.sync_copy(idx_hbm.at[pl.ds(off, SC_WINDOW)], idx_v)
            pltpu.async_copy(table_hbm.at[idx_v], rows_v, sem).wait()
            pltpu.sync_copy(rows_v, out_hbm.at[pl.ds(off, SC_WINDOW)])

    return pl.kernel(
        body,
        out_type=jax.ShapeDtypeStruct((n_idx, width), table.dtype),
        mesh=mesh,
        scratch_types=[pltpu.VMEM((SC_WINDOW,), I32), pltpu.VMEM((SC_WINDOW, width), table.dtype),
                       pltpu.SemaphoreType.DMA],
        name="sc_gather",
    )(table, idx)


def _combine_kernel(hp_ref, x1_ref, g2_ref, w_ref, sgw_ref, suw_ref, sdw_ref, yg_ref, o_ref):
    xa, xb = _unpack_halves(hp_ref[...])
    xa = xa.astype(BF16)
    xb = xb.astype(BF16)
    half = xa.shape[1]
    tb = xa.shape[0]
    g = _dot(xa, sgw_ref[0:half]) + _dot(xb, sgw_ref[half:])
    u = _dot(xa, suw_ref[0:half]) + _dot(xb, suw_ref[half:])
    shared = _dot((_silu(g) * u).astype(BF16), sdw_ref[...])

    w = w_ref[...]
    acc_a = jnp.zeros((tb, half), F32)
    acc_b = jnp.zeros((tb, half), F32)
    for k in range(TOP_K):
        ya, yb = _unpack_halves(yg_ref[0, k])
        acc_a += ya * w[:, k:k + 1]
        acc_b += yb * w[:, k:k + 1]
    g2 = g2_ref[0]
    o_ref[:, 0:half] = x1_ref[:, 0:half] + g2[:, 0:half] * (acc_a + shared[:, 0:half])
    o_ref[:, half:] = x1_ref[:, half:] + g2[:, half:] * (acc_b + shared[:, half:])


def _combine(hp, x1, g2, w_tok, sgw, suw, sdw, yg, *, tb, seq_len):
    t, half = hp.shape
    d = 2 * half
    tb = min(tb, seq_len)
    per_seq = seq_len // tb
    f = sgw.shape[1]
    return pl.pallas_call(
        _combine_kernel,
        grid=(t // tb,),
        in_specs=[pl.BlockSpec((tb, half), lambda i: (i, 0)),
                  pl.BlockSpec((tb, d), lambda i: (i, 0)),
                  pl.BlockSpec((1, 1, d), lambda i: (i // per_seq, 0, 0)),
                  pl.BlockSpec((tb, TOP_K), lambda i: (i, 0)),
                  pl.BlockSpec((d, f), lambda i: (0, 0)),
                  pl.BlockSpec((d, f), lambda i: (0, 0)),
                  pl.BlockSpec((f, d), lambda i: (0, 0)),
                  pl.BlockSpec((1, TOP_K, tb, half), lambda i: (i, 0, 0, 0))],
        out_specs=pl.BlockSpec((tb, d), lambda i: (i, 0)),
        out_shape=jax.ShapeDtypeStruct((t, d), F32),
        compiler_params=_cparams("arbitrary"),
        name="combine",
    )(hp, x1, g2, w_tok, sgw, suw, sdw, yg)


def _rope_tables(l):
    rows = l // GRID_W
    r = jnp.repeat(jnp.arange(rows), GRID_W).astype(F32)
    col = jnp.tile(jnp.arange(GRID_W), rows).astype(F32)
    n_f = HEAD_DIM // 4
    freqs = ROPE_THETA ** (-jnp.arange(n_f, dtype=F32) / n_f)
    ang = jnp.concatenate([r[:, None] * freqs, col[:, None] * freqs], axis=-1)
    ang = jnp.tile(jnp.repeat(ang, 2, axis=1), (1, LANES // HEAD_DIM))
    sign = jnp.where(jnp.arange(LANES) % 2 == 0, -1.0, 1.0).astype(F32)
    return jnp.cos(ang), jnp.sin(ang) * sign


def kernel(x, c, ctx, c_ctx, w_mod, b_mod, norm1_w, norm2_w, w_in, q_norm_w, k_norm_w, ret_decay_fwd,
           ret_decay_bwd, w_out, router_w, router_bias, exp_w_gate, exp_w_up, exp_w_down, sh_w_gate,
           sh_w_up, sh_w_down):
    b, l, d = x.shape
    lc = ctx.shape[1]
    t = b * l
    assert w_mod.shape[0] == 1, "single layer"
    assert l % CHUNK == 0 and lc % CHUNK == 0 and l % GRID_W == 0

    rows = (b + 1 + 7) // 8 * 8
    cc = jnp.zeros((rows, d), F32).at[:b].set(c).at[b].set(c_ctx)
    mod = _modulation(cc, w_mod[0], b_mod[0])
    sh1, sc1, g1, sh2, sc2, g2 = [mod[:b, i * d:(i + 1) * d].reshape(b, 1, d) for i in range(6)]
    shc = mod[b, 0:d].reshape(1, 1, d)
    scc = mod[b, d:2 * d].reshape(1, 1, d)

    wi = w_in[0].astype(BF16)
    qnw = jnp.tile(q_norm_w[0], LANES // HEAD_DIM).reshape(1, LANES)
    knw = jnp.tile(k_norm_w[0], LANES // HEAD_DIM).reshape(1, LANES)
    cos, sin = _rope_tables(l)
    n1 = norm1_w[0].reshape(1, d)

    cklo, ckhi, cvlo, cvhi, crk, crv = _projection(
        ctx, shc, scc, n1, wi, qnw, knw, cos[:lc], sin[:lc], rope=False, with_q=False, tm=TILE_PROJ)
    klo, khi, vlo, vhi, rk, rv, q, rq, sg = _projection(
        x, sh1, sc1, n1, wi, qnw, knw, cos, sin, rope=True, with_q=True, tm=TILE_PROJ)

    attn = _attention(q, klo, khi, vlo, vhi, cklo, ckhi, cvlo, cvhi, tq=TILE_ATTN_Q)
    dec_f = jnp.repeat(ret_decay_fwd[0].astype(F32), HEAD_DIM).reshape(1, RET_W)
    dec_b = jnp.repeat(ret_decay_bwd[0].astype(F32), HEAD_DIM).reshape(1, RET_W)
    ret = _retention(rq, rk, rv, sg, crk, crv, dec_f, dec_b)

    wo = w_out[0].astype(BF16)
    r_hi, r_lo = _split(router_w[0].T)
    x1, hp, logits_t = _out_projection(attn, ret, x, wo[:ATTN_W], wo[ATTN_W:], g1, sh2, sc2,
                                       norm2_w[0].reshape(1, d), r_hi, r_lo, tm=TILE_OUT)

    idx_t, w_t, rank_t, cnt_col, cnt_row = _route(logits_t, router_bias[0].reshape(N_EXPERTS, 1), tb=TILE_TOKENS)
    n_blocks = -(-(t * TOP_K) // MOE_BLOCK) + N_EXPERTS
    tb = TILE_TOKENS
    dest, meta, items = _destinations(idx_t, rank_t, cnt_col, cnt_row, tb=tb, n_blocks=n_blocks)
    dest_flat = dest.reshape(-1)
    steps, _, tbe = dest.shape
    dest_win = dest.reshape(steps, TOP_K, tbe // SC_WINDOW, SC_WINDOW).transpose(0, 2, 1, 3)
    dest_win = dest_win.reshape(t // SC_WINDOW, TOP_K, SC_WINDOW)
    xs = _sc_scatter_rows(hp, dest_win, (n_blocks + ITEM_BLOCKS - 1) * MOE_BLOCK)
    xs = _pad_fill(meta, xs)
    ys = _experts(items, xs, exp_w_gate[0], exp_w_up[0], exp_w_down[0], n_blocks=n_blocks)
    yg = _sc_gather_rows(ys, dest_flat).reshape(dest.shape + (d // 2,))
    out = _combine(hp, x1.reshape(t, d), g2, w_t.T, sh_w_gate[0].astype(BF16), sh_w_up[0].astype(BF16),
                   sh_w_down[0].astype(BF16), yg, tb=tb, seq_len=l)
    return out.reshape(b, l, d)
```

```python
import functools

import jax
import jax.numpy as jnp
from jax import lax
from jax.experimental import pallas as pl
from jax.experimental.pallas import tpu as pltpu
from jax.experimental.pallas import tpu_sc as plsc

F32 = jnp.float32
BF16 = jnp.bfloat16
I32 = jnp.int32
U32 = jnp.uint32
PACKED = jnp.int32

HEAD_DIM = 64
LANES = 128
SUBLANES = 8
ATTN_HEADS = 8
ATTN_KV_HEADS = 2
GQA = ATTN_HEADS // ATTN_KV_HEADS
RET_HEADS = 8
ATTN_W = ATTN_HEADS * HEAD_DIM
KV_W = ATTN_KV_HEADS * HEAD_DIM
RET_W = RET_HEADS * HEAD_DIM
RET_PAIRS = RET_W // LANES
CHUNK = 128
GRID_W = 64
ROPE_THETA = 10000.0
N_EXPERTS = 256
TOP_K = 8
N_GROUPS = 8
GROUP_SIZE = N_EXPERTS // N_GROUPS
TOPK_GROUPS = 4
ROUTED_SCALE = 2.5
MOE_BLOCK = 128
EPS = 1e-6
QK_SCALE = HEAD_DIM ** -0.5
LOG2_E = 1.4426950408889634

OFF_AK = 0
OFF_AV = OFF_AK + KV_W
OFF_RK = OFF_AV + KV_W
OFF_RV = OFF_RK + RET_W
CTX_KV_COLS = OFF_RV + RET_W
OFF_AQ = CTX_KV_COLS
OFF_RQ = OFF_AQ + ATTN_W
OFF_RG = OFF_RQ + RET_W
IN_COLS = OFF_RG + RET_W

VMEM_LIMIT = 52 * 1024 * 1024

TILE_PROJ = 512
TILE_ATTN_Q = 256
TILE_OUT = 512
TILE_TOKENS = 256
ITEM_BLOCKS = 5
SC_WINDOW = 64
HI_MASK = 0xFFFF0000


def _cparams(*sem):
    return pltpu.CompilerParams(dimension_semantics=sem, vmem_limit_bytes=VMEM_LIMIT)


def _split(a):
    hi = a.astype(BF16)
    lo = (a - hi.astype(F32)).astype(BF16)
    return hi, lo


def _dot(a, b):
    return jnp.dot(a, b, preferred_element_type=F32)


def _dot_nt(a, b):
    return lax.dot_general(a, b, (((1,), (1,)), ((), ())), preferred_element_type=F32)


def _sigmoid(v):
    return 1.0 / (1.0 + jnp.exp(-v))


def _silu(v):
    return v * _sigmoid(v)


def _pack_halves(a, b):
    ua = lax.bitcast_convert_type(a.astype(BF16).astype(F32), U32)
    ub = lax.bitcast_convert_type(b.astype(BF16).astype(F32), U32)
    return lax.bitcast_convert_type((ua & jnp.uint32(HI_MASK)) | (ub >> 16), PACKED)


def _unpack_halves(p):
    u = lax.bitcast_convert_type(p, U32)
    a = lax.bitcast_convert_type(u & jnp.uint32(HI_MASK), F32)
    b = lax.bitcast_convert_type(u << 16, F32)
    return a, b


def _mod_kernel(c_ref, w_ref, b_ref, o_ref):
    s_hi, s_lo = _split(_silu(c_ref[...]))
    w_hi, w_lo = _split(w_ref[...])
    o_ref[...] = _dot(s_hi, w_hi) + _dot(s_hi, w_lo) + _dot(s_lo, w_hi) + b_ref[...]


def _modulation(cc, w_mod, b_mod):
    rows, d = cc.shape
    n = w_mod.shape[1]
    tn = 768
    return pl.pallas_call(
        _mod_kernel,
        grid=(n // tn,),
        in_specs=[pl.BlockSpec((rows, d), lambda j: (0, 0)),
                  pl.BlockSpec((d, tn), lambda j: (0, j)),
                  pl.BlockSpec((1, tn), lambda j: (0, j))],
        out_specs=pl.BlockSpec((rows, tn), lambda j: (0, j)),
        out_shape=jax.ShapeDtypeStruct((rows, n), F32),
        compiler_params=_cparams("arbitrary"),
        name="mod",
    )(cc, w_mod, b_mod.reshape(1, n))


def _segment_ones():
    r = lax.broadcasted_iota(I32, (LANES, LANES), 0) // HEAD_DIM
    c = lax.broadcasted_iota(I32, (LANES, LANES), 1) // HEAD_DIM
    return jnp.where(r == c, 1.0, 0.0).astype(BF16)


def _head_mean_sq(v, seg):
    hi, lo = _split(v * v)
    return (_dot(hi, seg) + _dot(lo, seg)) * (1.0 / HEAD_DIM)


def _proj_kernel(x_ref, sh_ref, sc_ref, nw_ref, wi_ref, qnw_ref, knw_ref, cos_ref, sin_ref,
                 *out_refs, rope, with_q):
    if with_q:
        klo_ref, khi_ref, vlo_ref, vhi_ref, rk_ref, rv_ref, q_ref, rq_ref, sg_ref = out_refs
    else:
        klo_ref, khi_ref, vlo_ref, vhi_ref, rk_ref, rv_ref = out_refs
    x = x_ref[0]
    h = x * lax.rsqrt(jnp.mean(x * x, axis=-1, keepdims=True) + EPS) * nw_ref[...]
    h = h * (1.0 + sc_ref[0]) + sh_ref[0]
    z = _dot(h.astype(BF16), wi_ref[...])

    seg = _segment_ones()
    lane = lax.broadcasted_iota(I32, (x.shape[0], LANES), 1)
    low_half = lane < HEAD_DIM
    even = (lane & 1) == 0

    def norm_rope(v, w128):
        v = v * lax.rsqrt(_head_mean_sq(v, seg) + EPS) * w128
        if rope:
            swapped = jnp.where(even, pltpu.roll(v, LANES - 1, 1), pltpu.roll(v, 1, 1))
            v = v * cos_ref[...] + swapped * sin_ref[...]
        return v

    def write_lo_hi(v, lo_ref, hi_ref, transposed):
        sw = pltpu.roll(v, HEAD_DIM, 1)
        fix = (lambda a: jnp.transpose(a).astype(BF16)) if transposed else (lambda a: a.astype(BF16))
        lo_ref[0, 0] = fix(jnp.where(low_half, v, 0.0))
        hi_ref[0, 0] = fix(jnp.where(low_half, 0.0, sw))
        lo_ref[0, 1] = fix(jnp.where(low_half, sw, 0.0))
        hi_ref[0, 1] = fix(jnp.where(low_half, 0.0, v))

    write_lo_hi(norm_rope(z[:, OFF_AK:OFF_AK + KV_W], knw_ref[...]), klo_ref, khi_ref, True)
    write_lo_hi(z[:, OFF_AV:OFF_AV + KV_W], vlo_ref, vhi_ref, False)
    rk_ref[0] = (z[:, OFF_RK:OFF_RK + RET_W] * QK_SCALE).astype(BF16)
    rv_ref[0] = z[:, OFF_RV:OFF_RV + RET_W].astype(BF16)
    if with_q:
        for j in range(ATTN_W // LANES):
            qj = norm_rope(z[:, OFF_AQ + j * LANES:OFF_AQ + (j + 1) * LANES], qnw_ref[...])
            q_ref[0, :, j * LANES:(j + 1) * LANES] = (qj * (QK_SCALE * LOG2_E)).astype(BF16)
        rq_ref[0] = z[:, OFF_RQ:OFF_RQ + RET_W].astype(BF16)
        sg_ref[0] = _silu(z[:, OFF_RG:OFF_RG + RET_W]).astype(BF16)


def _projection(x, shift, scale, norm_w, wi_bf16, qnw, knw, cos, sin, *, rope, with_q, tm):
    b, l, d = x.shape
    tm = min(tm, l)
    ncols = IN_COLS if with_q else CTX_KV_COLS
    per_batch = shift.shape[0] > 1
    mod_idx = (lambda bi, i: (bi, 0, 0)) if per_batch else (lambda bi, i: (0, 0, 0))
    kv_shape = jax.ShapeDtypeStruct((b, ATTN_KV_HEADS, l, LANES), BF16)
    kv_spec = pl.BlockSpec((1, ATTN_KV_HEADS, tm, LANES), lambda bi, i: (bi, 0, i, 0))
    kt_shape = jax.ShapeDtypeStruct((b, ATTN_KV_HEADS, LANES, l), BF16)
    kt_spec = pl.BlockSpec((1, ATTN_KV_HEADS, LANES, tm), lambda bi, i: (bi, 0, 0, i))
    w_shape = jax.ShapeDtypeStruct((b, l, RET_W), BF16)
    w_spec = pl.BlockSpec((1, tm, RET_W), lambda bi, i: (bi, i, 0))
    out_shape = [kt_shape] * 2 + [kv_shape] * 2 + [w_shape] * 2
    out_specs = [kt_spec] * 2 + [kv_spec] * 2 + [w_spec] * 2
    if with_q:
        out_shape += [w_shape] * 3
        out_specs += [w_spec] * 3
    return pl.pallas_call(
        functools.partial(_proj_kernel, rope=rope, with_q=with_q),
        grid=(b, l // tm),
        in_specs=[pl.BlockSpec((1, tm, d), lambda bi, i: (bi, i, 0)),
                  pl.BlockSpec((1, 1, d), mod_idx),
                  pl.BlockSpec((1, 1, d), mod_idx),
                  pl.BlockSpec((1, d), lambda bi, i: (0, 0)),
                  pl.BlockSpec((d, ncols), lambda bi, i: (0, 0)),
                  pl.BlockSpec((1, LANES), lambda bi, i: (0, 0)),
                  pl.BlockSpec((1, LANES), lambda bi, i: (0, 0)),
                  pl.BlockSpec((tm, LANES), lambda bi, i: (i, 0)),
                  pl.BlockSpec((tm, LANES), lambda bi, i: (i, 0))],
        out_specs=out_specs,
        out_shape=out_shape,
        compiler_params=_cparams("arbitrary", "arbitrary"),
        name="proj_latent" if with_q else "proj_ctx",
    )(x, shift, scale, norm_w, wi_bf16, qnw, knw, cos, sin)


def _attn_kernel(q_ref, klo_ref, khi_ref, vlo_ref, vhi_ref, cklo_ref, ckhi_ref, cvlo_ref, cvhi_ref,
                 o_ref, kl_s, kh_s, va_s, *, l, lc):
    lk = l + lc

    @pl.when(pl.program_id(2) == 0)
    def _():
        kl_s[:, 0:l] = klo_ref[0, 0]
        kl_s[:, l:lk] = cklo_ref[0, 0]
        kh_s[:, 0:l] = khi_ref[0, 0]
        kh_s[:, l:lk] = ckhi_ref[0, 0]
        lane = lax.broadcasted_iota(I32, (lk, LANES), 1)
        ones_lo = jnp.where(lane < HEAD_DIM, 1.0, 0.0).astype(BF16)
        ones_hi = jnp.where(lane < HEAD_DIM, 0.0, 1.0).astype(BF16)
        for g, (v_ref, cv_ref, ones) in enumerate(((vlo_ref, cvlo_ref, ones_lo), (vhi_ref, cvhi_ref, ones_hi),
                                                   (vlo_ref, cvlo_ref, ones_lo), (vhi_ref, cvhi_ref, ones_hi))):
            v_col, one_col = (0, LANES) if g < 2 else (LANES, 0)
            va_s[g, 0:l, v_col:v_col + LANES] = v_ref[0, 0]
            va_s[g, l:lk, v_col:v_col + LANES] = cv_ref[0, 0]
            va_s[g, :, one_col:one_col + LANES] = ones

    q = q_ref[0]
    acc = []
    for g in range(GQA):
        qp = q[:, (g // 2) * LANES:(g // 2 + 1) * LANES]
        kt = kl_s[...] if g % 2 == 0 else kh_s[...]
        s = _dot(qp, kt)
        p = jnp.exp2(s - jnp.max(s, axis=-1, keepdims=True)).astype(BF16)
        acc.append(_dot(p, va_s[g]))
    out_a = acc[0] + acc[1]
    out_b = acc[2] + acc[3]
    o_ref[0, :, 0:LANES] = (out_a[:, 0:LANES] / out_a[:, LANES:2 * LANES]).astype(BF16)
    o_ref[0, :, LANES:2 * LANES] = (out_b[:, LANES:2 * LANES] / out_b[:, 0:LANES]).astype(BF16)


def _attention(q, klo, khi, vlo, vhi, cklo, ckhi, cvlo, cvhi, *, tq):
    b, l, _ = q.shape
    lc = cvlo.shape[2]
    lk = l + lc
    tq = min(tq, l)
    gw = GQA * HEAD_DIM
    kt_spec = pl.BlockSpec((1, 1, LANES, l), lambda bi, h, i: (bi, h, 0, 0))
    kv_spec = pl.BlockSpec((1, 1, l, LANES), lambda bi, h, i: (bi, h, 0, 0))
    ckt_spec = pl.BlockSpec((1, 1, LANES, lc), lambda bi, h, i: (bi, h, 0, 0))
    ckv_spec = pl.BlockSpec((1, 1, lc, LANES), lambda bi, h, i: (bi, h, 0, 0))
    return pl.pallas_call(
        functools.partial(_attn_kernel, l=l, lc=lc),
        grid=(b, ATTN_KV_HEADS, l // tq),
        in_specs=([pl.BlockSpec((1, tq, gw), lambda bi, h, i: (bi, i, h))] + [kt_spec] * 2 + [kv_spec] * 2
                  + [ckt_spec] * 2 + [ckv_spec] * 2),
        out_specs=pl.BlockSpec((1, tq, gw), lambda bi, h, i: (bi, i, h)),
        out_shape=jax.ShapeDtypeStruct((b, l, ATTN_W), BF16),
        scratch_shapes=[pltpu.VMEM((LANES, lk), BF16), pltpu.VMEM((LANES, lk), BF16),
                        pltpu.VMEM((GQA, lk, 2 * LANES), BF16)],
        compiler_params=_cparams("arbitrary", "arbitrary", "arbitrary"),
        name="attn",
    )(q, klo, khi, vlo, vhi, cklo, ckhi, cvlo, cvhi)


def _log_sigmoid(v):
    return jnp.minimum(v, 0.0) - jnp.log(1.0 + jnp.exp(-jnp.abs(v)))


def _ret_kernel(rq_ref, rk_ref, rv_ref, sg_ref, crk_ref, crv_ref, df_ref, db_ref, o_ref,
                m_s, tab_s, sb_s, sf_s, *, l, lc):
    n = l // CHUNK
    nc = lc // CHUNK
    lgf = _log_sigmoid(df_ref[...])
    lgb = _log_sigmoid(db_ref[...])
    pos = lax.broadcasted_iota(I32, (CHUNK, RET_W), 0).astype(F32)
    tab_s[0] = jnp.exp((pos + 1.0) * lgf)
    tab_s[1] = jnp.exp((CHUNK - pos) * lgb)
    tab_s[2] = jnp.exp((CHUNK - 1.0 - pos) * lgf)
    tab_s[3] = jnp.exp(pos * lgb)
    gf_c = jnp.exp(CHUNK * lgf)
    gb_c = jnp.exp(CHUNK * lgb)
    row = lax.broadcasted_iota(I32, (CHUNK, CHUNK), 0)
    col = lax.broadcasted_iota(I32, (CHUNK, CHUNK), 1)
    diff = (row - col).astype(F32)
    for h in range(RET_HEADS):
        lf = lgf[:, h * HEAD_DIM:h * HEAD_DIM + 1]
        lb = lgb[:, h * HEAD_DIM:h * HEAD_DIM + 1]
        m_s[h] = jnp.where(diff > 0, jnp.exp(diff * lf), jnp.where(diff < 0, jnp.exp(-diff * lb), 2.0))

    lane = lax.broadcasted_iota(I32, (CHUNK, LANES), 1)
    low_half = lane < HEAD_DIM
    diag = (lax.broadcasted_iota(I32, (LANES, LANES), 0) // HEAD_DIM
            == lax.broadcasted_iota(I32, (LANES, LANES), 1) // HEAD_DIM)
    seg = jnp.where(diag, 1.0, 0.0).astype(BF16)

    def contrib(k_ref, v_ref, r0, p, zeta_idx):
        cols = slice(p * LANES, (p + 1) * LANES)
        kz = k_ref[0, pl.ds(r0, CHUNK), cols].astype(F32) * tab_s[zeta_idx, :, cols]
        kv = _dot(jnp.transpose(kz).astype(BF16), v_ref[0, pl.ds(r0, CHUNK), cols])
        return jnp.where(diag, kv, 0.0)

    for p in range(RET_PAIRS):
        cols = slice(p * LANES, (p + 1) * LANES)
        sf = jnp.zeros((LANES, LANES), F32)
        sb = jnp.zeros((LANES, LANES), F32)
        for c in range(nc):
            sf = gf_c[:, cols] * sf + contrib(crk_ref, crv_ref, c * CHUNK, p, 2)
            cb = nc - 1 - c
            sb = gb_c[:, cols] * sb + contrib(crk_ref, crv_ref, cb * CHUNK, p, 3)
        sf_s[p] = sf
        sb_s[n, p] = sb

    def bwd_body(j, carry):
        c = n - 1 - j
        r0 = pl.multiple_of(c * CHUNK, CHUNK)
        for p in range(RET_PAIRS):
            cols = slice(p * LANES, (p + 1) * LANES)
            prev = sb_s[c + 1, p]
            sb_s[c, p] = gb_c[:, cols] * prev + contrib(rk_ref, rv_ref, r0, p, 3)
        return carry

    lax.fori_loop(0, n, bwd_body, 0)

    def fwd_body(c, carry):
        r0 = pl.multiple_of(c * CHUNK, CHUNK)
        for p in range(RET_PAIRS):
            cols = slice(p * LANES, (p + 1) * LANES)
            qp = rq_ref[0, pl.ds(r0, CHUNK), cols]
            kp = rk_ref[0, pl.ds(r0, CHUNK), cols]
            vp = rv_ref[0, pl.ds(r0, CHUNK), cols]
            zero = jnp.zeros_like(kp)
            s0 = _dot_nt(qp, jnp.where(low_half, kp, zero))
            s1 = _dot_nt(qp, jnp.where(low_half, zero, kp))
            a0 = (s0 * m_s[2 * p]).astype(BF16)
            a1 = (s1 * m_s[2 * p + 1]).astype(BF16)
            y = _dot(a0, jnp.where(low_half, vp, zero)) + _dot(a1, jnp.where(low_half, zero, vp))
            qf = qp.astype(F32)
            sf = sf_s[p]
            y += _dot((qf * tab_s[0, :, cols]).astype(BF16), sf.astype(BF16))
            y += _dot((qf * tab_s[1, :, cols]).astype(BF16), sb_s[c + 1, p].astype(BF16))
            hi, lo = _split(y * y)
            ms = (_dot(hi, seg) + _dot(lo, seg)) * (1.0 / HEAD_DIM)
            out = y * lax.rsqrt(ms + EPS) * sg_ref[0, pl.ds(r0, CHUNK), cols].astype(F32)
            o_ref[0, pl.ds(r0, CHUNK), cols] = out.astype(BF16)
            sf_s[p] = gf_c[:, cols] * sf + contrib(rk_ref, rv_ref, r0, p, 2)
        return carry

    lax.fori_loop(0, n, fwd_body, 0)


def _retention(rq, rk, rv, sg, crk, crv, dec_f, dec_b):
    b, l, _ = rq.shape
    lc = crk.shape[1]
    n = l // CHUNK
    spec = pl.BlockSpec((1, l, RET_W), lambda bi: (bi, 0, 0))
    cspec = pl.BlockSpec((1, lc, RET_W), lambda bi: (bi, 0, 0))
    dspec = pl.BlockSpec((1, RET_W), lambda bi: (0, 0))
    return pl.pallas_call(
        functools.partial(_ret_kernel, l=l, lc=lc),
        grid=(b,),
        in_specs=[spec, spec, spec, spec, cspec, cspec, dspec, dspec],
        out_specs=spec,
        out_shape=jax.ShapeDtypeStruct((b, l, RET_W), BF16),
        scratch_shapes=[pltpu.VMEM((RET_HEADS, CHUNK, CHUNK), F32),
                        pltpu.VMEM((4, CHUNK, RET_W), F32),
                        pltpu.VMEM((n + 1, RET_PAIRS, LANES, LANES), F32),
                        pltpu.VMEM((RET_PAIRS, LANES, LANES), F32)],
        compiler_params=_cparams("arbitrary"),
        name="ret",
    )(rq, rk, rv, sg, crk, crv, dec_f, dec_b)


def _out_kernel(attn_ref, ret_ref, x_ref, wa_ref, wr_ref, g1_ref, sh_ref, sc_ref, nw_ref, rhi_ref, rlo_ref,
                x1_ref, hp_ref, lg_ref):
    y = _dot(attn_ref[0], wa_ref[...]) + _dot(ret_ref[0], wr_ref[...])
    x1 = x_ref[0] + g1_ref[0] * y
    x1_ref[0] = x1
    h = x1 * lax.rsqrt(jnp.mean(x1 * x1, axis=-1, keepdims=True) + EPS) * nw_ref[...]
    h = h * (1.0 + sc_ref[0]) + sh_ref[0]
    half = h.shape[1] // 2
    hp_ref[...] = _pack_halves(h[:, :half], h[:, half:])
    h_hi, h_lo = _split(h)
    lg_ref[...] = _dot_nt(rhi_ref[...], h_hi) + _dot_nt(rhi_ref[...], h_lo) + _dot_nt(rlo_ref[...], h_hi)


def _out_projection(attn, ret, x, wa, wr, g1, sh2, sc2, norm_w, r_hi, r_lo, *, tm):
    b, l, d = x.shape
    tm = min(tm, l)
    nt = l // tm
    t = b * l
    mspec = pl.BlockSpec((1, 1, d), lambda bi, i: (bi, 0, 0))
    return pl.pallas_call(
        _out_kernel,
        grid=(b, nt),
        in_specs=[pl.BlockSpec((1, tm, ATTN_W), lambda bi, i: (bi, i, 0)),
                  pl.BlockSpec((1, tm, RET_W), lambda bi, i: (bi, i, 0)),
                  pl.BlockSpec((1, tm, d), lambda bi, i: (bi, i, 0)),
                  pl.BlockSpec((ATTN_W, d), lambda bi, i: (0, 0)),
                  pl.BlockSpec((RET_W, d), lambda bi, i: (0, 0)),
                  mspec, mspec, mspec,
                  pl.BlockSpec((1, d), lambda bi, i: (0, 0)),
                  pl.BlockSpec((N_EXPERTS, d), lambda bi, i: (0, 0)),
                  pl.BlockSpec((N_EXPERTS, d), lambda bi, i: (0, 0))],
        out_specs=[pl.BlockSpec((1, tm, d), lambda bi, i: (bi, i, 0)),
                   pl.BlockSpec((tm, d // 2), lambda bi, i: (bi * nt + i, 0)),
                   pl.BlockSpec((N_EXPERTS, tm), lambda bi, i: (0, bi * nt + i))],
        out_shape=[jax.ShapeDtypeStruct((b, l, d), F32),
                   jax.ShapeDtypeStruct((t, d // 2), PACKED),
                   jax.ShapeDtypeStruct((N_EXPERTS, t), F32)],
        compiler_params=_cparams("arbitrary", "arbitrary"),
        name="out_proj",
    )(attn, ret, x, wa, wr, g1, sh2, sc2, norm_w, r_hi, r_lo)


def _route_kernel(lg_ref, bias_ref, idx_ref, w_ref, rank_ref, cnt_col_ref, cnt_row_ref, tri_s, col_s, row_s):
    tb = lg_ref.shape[1]
    step = pl.program_id(0)

    @pl.when(step == 0)
    def _():
        r = lax.broadcasted_iota(I32, (tb, tb), 0)
        c = lax.broadcasted_iota(I32, (tb, tb), 1)
        tri_s[...] = jnp.where(r <= c, 1.0, 0.0).astype(BF16)
        col_s[...] = jnp.zeros_like(col_s)
        row_s[...] = jnp.zeros_like(row_s)

    scores = _sigmoid(lg_ref[...])
    biased = scores + bias_ref[...]
    neg = -jnp.inf
    sub = lax.broadcasted_iota(I32, (GROUP_SIZE, tb), 0).astype(F32)

    gscore = []
    for g in range(N_GROUPS):
        blk = biased[g * GROUP_SIZE:(g + 1) * GROUP_SIZE]
        m1 = jnp.max(blk, axis=0, keepdims=True)
        first = jnp.min(jnp.where(blk == m1, sub, float(GROUP_SIZE)), axis=0, keepdims=True)
        m2 = jnp.max(jnp.where(sub == first, neg, blk), axis=0, keepdims=True)
        gscore.append(m1 + m2)
    gs = jnp.concatenate(gscore, axis=0)
    gsub = lax.broadcasted_iota(I32, (N_GROUPS, tb), 0).astype(F32)
    keep = jnp.zeros((N_GROUPS, tb), F32)
    for _ in range(TOPK_GROUPS):
        m = jnp.max(gs, axis=0, keepdims=True)
        first = jnp.min(jnp.where(gs == m, gsub, float(N_GROUPS)), axis=0, keepdims=True)
        sel = gsub == first
        keep = jnp.where(sel, 1.0, keep)
        gs = jnp.where(sel, neg, gs)
    masked = jnp.concatenate(
        [jnp.where(keep[g:g + 1] > 0.0, biased[g * GROUP_SIZE:(g + 1) * GROUP_SIZE], neg)
         for g in range(N_GROUPS)], axis=0)

    esub = lax.broadcasted_iota(I32, (N_EXPERTS, tb), 0).astype(F32)
    sels, idxs, ws = [], [], []
    chosen = jnp.zeros((N_EXPERTS, tb), F32)
    for _ in range(TOP_K):
        m = jnp.max(masked, axis=0, keepdims=True)
        first = jnp.min(jnp.where(masked == m, esub, float(N_EXPERTS)), axis=0, keepdims=True)
        sel = esub == first
        sels.append(sel)
        idxs.append(first)
        ws.append(jnp.sum(jnp.where(sel, scores, 0.0), axis=0, keepdims=True))
        chosen = jnp.where(sel, 1.0, chosen)
        masked = jnp.where(sel, neg, masked)
    wsum = ws[0]
    for k in range(1, TOP_K):
        wsum = wsum + ws[k]
    idx_ref[...] = jnp.concatenate(idxs, axis=0).astype(I32)
    w_ref[...] = jnp.concatenate([wk / wsum * ROUTED_SCALE for wk in ws], axis=0)

    chosen_b = chosen.astype(BF16)
    incl = _dot(chosen_b, tri_s[...])
    before = incl - chosen + col_s[...]
    rank_ref[...] = jnp.concatenate(
        [jnp.sum(jnp.where(sel, before, 0.0), axis=0, keepdims=True) for sel in sels], axis=0).astype(I32)
    col_s[...] = col_s[...] + incl[:, tb - 1:tb]
    row_s[...] = row_s[...] + _dot_nt(jnp.ones((8, tb), BF16), chosen_b)
    cnt_col_ref[...] = col_s[...].astype(I32)
    cnt_row_ref[...] = row_s[...].astype(I32)


def _route(logits_t, bias_col, *, tb):
    e, t = logits_t.shape
    tb = min(tb, t)
    kspec = pl.BlockSpec((TOP_K, tb), lambda i: (0, i))
    return pl.pallas_call(
        _route_kernel,
        grid=(t // tb,),
        in_specs=[pl.BlockSpec((e, tb), lambda i: (0, i)),
                  pl.BlockSpec((e, 1), lambda i: (0, 0))],
        out_specs=[kspec, kspec, kspec,
                   pl.BlockSpec((e, 1), lambda i: (0, 0)),
                   pl.BlockSpec((8, e), lambda i: (0, 0))],
        out_shape=[jax.ShapeDtypeStruct((TOP_K, t), I32),
                   jax.ShapeDtypeStruct((TOP_K, t), F32),
                   jax.ShapeDtypeStruct((TOP_K, t), I32),
                   jax.ShapeDtypeStruct((e, 1), I32),
                   jax.ShapeDtypeStruct((8, e), I32)],
        scratch_shapes=[pltpu.VMEM((tb, tb), BF16), pltpu.VMEM((e, 1), F32), pltpu.VMEM((8, e), F32)],
        compiler_params=_cparams("arbitrary"),
        name="route",
    )(logits_t, bias_col)


def _pad_block(cnt):
    return (cnt + (MOE_BLOCK - 1)) // MOE_BLOCK * MOE_BLOCK


def _max_items(n_blocks):
    return n_blocks // ITEM_BLOCKS + N_EXPERTS


def _dest_kernel(idx_ref, rank_ref, cnt_col_ref, cnt_row_ref, dest_ref, meta_ref, items_ref):
    tb = idx_ref.shape[1]
    nip = items_ref.shape[1]
    pad_col = _pad_block(cnt_col_ref[...])
    pad_row = _pad_block(cnt_row_ref[0:1, :])
    er = lax.broadcasted_iota(I32, (N_EXPERTS, N_EXPERTS), 0)
    ec = lax.broadcasted_iota(I32, (N_EXPERTS, N_EXPERTS), 1)
    start_col = jnp.sum(jnp.where(ec < er, pad_row, 0), axis=1, keepdims=True)
    start_row = jnp.sum(jnp.where(er < ec, pad_col, 0), axis=0, keepdims=True)

    esub = lax.broadcasted_iota(I32, (N_EXPERTS, tb), 0)
    rows = []
    for k in range(TOP_K):
        onehot = esub == idx_ref[k:k + 1, :]
        rows.append(jnp.sum(jnp.where(onehot, start_col, 0), axis=0, keepdims=True) + rank_ref[k:k + 1, :])
    dest_ref[0] = jnp.concatenate(rows, axis=0)

    used = jnp.sum(pad_row, axis=1, keepdims=True) // MOE_BLOCK
    meta_ref[...] = jnp.concatenate(
        [cnt_row_ref[0:1, :], start_row, pad_row, jnp.broadcast_to(used, (1, N_EXPERTS)),
         jnp.zeros((4, N_EXPERTS), I32)], axis=0)

    nb_col = pad_col // MOE_BLOCK
    it_col = (nb_col + (ITEM_BLOCKS - 1)) // ITEM_BLOCKS
    it_row = (pad_row // MOE_BLOCK + (ITEM_BLOCKS - 1)) // ITEM_BLOCKS
    it_start = jnp.sum(jnp.where(ec < er, it_row, 0), axis=1, keepdims=True)
    n_items = jnp.sum(it_row, axis=1, keepdims=True)
    lane = lax.broadcasted_iota(I32, (1, nip), 1)
    owner = jnp.sum(jnp.where(it_start + it_col <= lane, 1, 0), axis=0, keepdims=True)
    owner = jnp.minimum(owner, N_EXPERTS - 1)
    onehot = lax.broadcasted_iota(I32, (N_EXPERTS, nip), 0) == owner

    def pick(col):
        return jnp.sum(jnp.where(onehot, col, 0), axis=0, keepdims=True)

    j = lane - pick(it_start)
    block0 = pick(start_col) // MOE_BLOCK + ITEM_BLOCKS * j
    nvalid = jnp.clip(pick(nb_col) - ITEM_BLOCKS * j, 0, ITEM_BLOCKS)
    items_ref[...] = jnp.concatenate(
        [owner, block0, jnp.where(lane < n_items, nvalid, 0), jnp.broadcast_to(n_items, (1, nip)),
         jnp.zeros((4, nip), I32)], axis=0)


def _destinations(idx_t, rank_t, cnt_col, cnt_row, *, tb, n_blocks):
    _, t = idx_t.shape
    tb = min(tb, t)
    nip = (_max_items(n_blocks) + LANES - 1) // LANES * LANES
    kspec = pl.BlockSpec((TOP_K, tb), lambda i: (0, i))
    return pl.pallas_call(
        _dest_kernel,
        grid=(t // tb,),
        in_specs=[kspec, kspec,
                  pl.BlockSpec((N_EXPERTS, 1), lambda i: (0, 0)),
                  pl.BlockSpec((8, N_EXPERTS), lambda i: (0, 0))],
        out_specs=[pl.BlockSpec((1, TOP_K, tb), lambda i: (i, 0, 0)),
                   pl.BlockSpec((8, N_EXPERTS), lambda i: (0, 0)),
                   pl.BlockSpec((8, nip), lambda i: (0, 0))],
        out_shape=[jax.ShapeDtypeStruct((t // tb, TOP_K, tb), I32),
                   jax.ShapeDtypeStruct((8, N_EXPERTS), I32),
                   jax.ShapeDtypeStruct((8, nip), I32)],
        compiler_params=_cparams("arbitrary"),
        name="dest",
    )(idx_t, rank_t, cnt_col, cnt_row)


_PAD_BITS = (64, 32, 16, 8)


def _sc_scatter_rows(rows, dest_win, n_out_rows):
    n_win, n_slots, win = dest_win.shape
    width = rows.shape[1]
    info = plsc.get_sparse_core_info()
    n_workers = info.num_cores * info.num_subcores
    per_worker = n_win // n_workers
    assert per_worker * n_workers == n_win and win <= LANES
    mesh = plsc.VectorSubcoreMesh(core_axis_name="c", subcore_axis_name="s")

    def body(rows_hbm, dest_hbm, out_hbm, idx_v, rows_v, sem):
        wid = lax.axis_index("s") * info.num_cores + lax.axis_index("c")

        @pl.loop(0, per_worker)
        def _(j):
            w = wid * per_worker + j
            pltpu.sync_copy(dest_hbm.at[w], idx_v)
            pltpu.sync_copy(rows_hbm.at[pl.ds(w * win, win)], rows_v)
            copies = [pltpu.async_copy(rows_v, out_hbm.at[idx_v.at[k]], sem) for k in range(n_slots)]
            for cp in copies:
                cp.wait()

    return pl.kernel(
        body,
        out_type=jax.ShapeDtypeStruct((n_out_rows, width), rows.dtype),
        mesh=mesh,
        scratch_types=[pltpu.VMEM((n_slots, win), I32), pltpu.VMEM((win, width), rows.dtype),
                       pltpu.SemaphoreType.DMA],
        name="sc_scatter",
    )(rows, dest_win)


def _pad_fill_kernel(meta_ref, xs_in, xs_hbm, zero_s, sem_z, *, e_per_step):
    del xs_in
    step = pl.program_id(0)
    zero_s[...] = jnp.zeros_like(zero_s)

    def tail_copy(c):
        row0 = pl.multiple_of((meta_ref[3, 0] + c) * MOE_BLOCK, MOE_BLOCK)
        return pltpu.make_async_copy(zero_s, xs_hbm.at[pl.ds(row0, MOE_BLOCK)], sem_z)

    @pl.when(step == 0)
    def _():
        for c in range(ITEM_BLOCKS - 1):
            tail_copy(c).start()
        for c in range(ITEM_BLOCKS - 1):
            tail_copy(c).wait()

    def pad_copies(e):
        cnt = meta_ref[0, e]
        off = meta_ref[1, e] + cnt
        rem = meta_ref[2, e] - cnt
        head = rem & (SUBLANES - 1)
        out = []
        for i in range(SUBLANES - 1):
            out.append((i < head,
                        pltpu.make_async_copy(zero_s.at[pl.ds(0, 1)], xs_hbm.at[pl.ds(off + i, 1)], sem_z)))
        off = off + head
        for bit in _PAD_BITS:
            out.append(((rem & bit) != 0,
                        pltpu.make_async_copy(zero_s.at[pl.ds(0, bit)],
                                              xs_hbm.at[pl.ds(pl.multiple_of(off, SUBLANES), bit)], sem_z)))
            off = off + (rem & bit)
        return out

    for j in range(e_per_step):
        for cond, c in pad_copies(step * e_per_step + j):
            pl.when(cond)(c.start)
    for j in range(e_per_step):
        for cond, c in pad_copies(step * e_per_step + j):
            pl.when(cond)(c.wait)


PAD_FILL_EXPERTS_PER_STEP = 8


def _pad_fill(meta, xs):
    half = xs.shape[1]
    return pl.pallas_call(
        functools.partial(_pad_fill_kernel, e_per_step=PAD_FILL_EXPERTS_PER_STEP),
        grid=(N_EXPERTS // PAD_FILL_EXPERTS_PER_STEP,),
        in_specs=[pl.BlockSpec(memory_space=pltpu.SMEM),
                  pl.BlockSpec(memory_space=pl.ANY)],
        out_specs=pl.BlockSpec(memory_space=pl.ANY),
        out_shape=jax.ShapeDtypeStruct(xs.shape, xs.dtype),
        input_output_aliases={1: 0},
        scratch_shapes=[pltpu.VMEM((MOE_BLOCK, half), PACKED), pltpu.SemaphoreType.DMA],
        compiler_params=_cparams("arbitrary"),
        name="pad_fill",
    )(meta, xs)


def _experts_kernel(items_ref, xs_hbm, wg_hbm, wu_hbm, wd_hbm, ys_hbm,
                    xbuf, ybuf, wg_f, wu_f, wd_f, wg_s, wu_s, wd_s, sem_x, sem_y, sem_w):
    n_items = items_ref[3, 0]
    rows = ITEM_BLOCKS * MOE_BLOCK

    def w_copies(e, s):
        return [pltpu.make_async_copy(src.at[e], dst.at[s], sem_w.at[s])
                for src, dst in ((wg_hbm, wg_f), (wu_hbm, wu_f), (wd_hbm, wd_f))]

    def x_copy(item, s):
        row0 = pl.multiple_of(items_ref[1, item] * MOE_BLOCK, MOE_BLOCK)
        return pltpu.make_async_copy(xs_hbm.at[pl.ds(row0, rows)], xbuf.at[s], sem_x.at[s])

    def y_copies(item, s, fn):
        for k in range(ITEM_BLOCKS):
            row0 = pl.multiple_of((items_ref[1, item] + k) * MOE_BLOCK, MOE_BLOCK)
            cp = pltpu.make_async_copy(ybuf.at[s, pl.ds(k * MOE_BLOCK, MOE_BLOCK)],
                                       ys_hbm.at[pl.ds(row0, MOE_BLOCK)], sem_y.at[s])
            pl.when(k < items_ref[2, item])(functools.partial(fn, cp))

    @pl.when(n_items > 0)
    def _():
        x_copy(0, 0).start()
        for cp in w_copies(items_ref[0, 0], 0):
            cp.start()

    def item_body(i, wslot):
        slot = i % 2
        prev = jnp.maximum(i - 1, 0)
        nxt = jnp.minimum(i + 1, n_items - 1)
        e = items_ref[0, i]
        new_expert = (i == 0) | (e != items_ref[0, prev])
        has_next = i + 1 < n_items
        next_new = has_next & (items_ref[0, nxt] != e)

        @pl.when(has_next)
        def _():
            x_copy(nxt, 1 - slot).start()

        @pl.when(new_expert)
        def _():
            for cp in w_copies(e, wslot):
                cp.wait()
            wg_s[...] = wg_f[wslot].astype(BF16)
            wu_s[...] = wu_f[wslot].astype(BF16)
            wd_s[...] = wd_f[wslot].astype(BF16)

        @pl.when(next_new)
        def _():
            for cp in w_copies(items_ref[0, nxt], 1 - wslot):
                cp.start()

        x_copy(i, slot).wait()
        xa, xb = _unpack_halves(xbuf[slot])
        xa = xa.astype(BF16)
        xb = xb.astype(BF16)
        half = xa.shape[1]
        g = _dot(xa, wg_s[0:half]) + _dot(xb, wg_s[half:])
        u = _dot(xa, wu_s[0:half]) + _dot(xb, wu_s[half:])
        y = _dot((_silu(g) * u).astype(BF16), wd_s[...])
        ybuf[slot] = _pack_halves(y[:, :half], y[:, half:])
        y_copies(i, slot, lambda cp: cp.start())

        @pl.when(i > 0)
        def _():
            y_copies(prev, 1 - slot, lambda cp: cp.wait())

        return jnp.where(next_new, 1 - wslot, wslot)

    lax.fori_loop(0, n_items, item_body, jnp.int32(0))

    @pl.when(n_items > 0)
    def _():
        last = n_items - 1
        y_copies(last, last % 2, lambda cp: cp.wait())


def _experts(items, xs, w_gate, w_up, w_down, *, n_blocks):
    half = xs.shape[1]
    e, d, f = w_gate.shape
    rows = ITEM_BLOCKS * MOE_BLOCK
    any_spec = pl.BlockSpec(memory_space=pl.ANY)
    return pl.pallas_call(
        _experts_kernel,
        grid_spec=pltpu.PrefetchScalarGridSpec(
            num_scalar_prefetch=1,
            grid=(1,),
            in_specs=[any_spec, any_spec, any_spec, any_spec],
            out_specs=any_spec,
            scratch_shapes=[pltpu.VMEM((2, rows, half), PACKED), pltpu.VMEM((2, rows, half), PACKED),
                            pltpu.VMEM((2, d, f), F32), pltpu.VMEM((2, d, f), F32), pltpu.VMEM((2, f, d), F32),
                            pltpu.VMEM((d, f), BF16), pltpu.VMEM((d, f), BF16), pltpu.VMEM((f, d), BF16),
                            pltpu.SemaphoreType.DMA((2,)), pltpu.SemaphoreType.DMA((2,)),
                            pltpu.SemaphoreType.DMA((2,))]),
        out_shape=jax.ShapeDtypeStruct((n_blocks * MOE_BLOCK, half), PACKED),
        compiler_params=_cparams("arbitrary"),
        name="experts",
    )(items, xs, w_gate, w_up, w_down)


def _sc_gather_rows(table, idx):
    n_idx = idx.shape[0]
    width = table.shape[1]
    info = plsc.get_sparse_core_info()
    n_workers = info.num_cores * info.num_subcores
    per_worker = n_idx // n_workers
    assert per_worker * n_workers == n_idx and per_worker % SC_WINDOW == 0
    mesh = plsc.VectorSubcoreMesh(core_axis_name="c", subcore_axis_name="s")

    def body(table_hbm, idx_hbm, out_hbm, idx_v, rows_v, sem):
        wid = lax.axis_index("s") * info.num_cores + lax.axis_index("c")
        base = wid * per_worker

        @pl.loop(0, per_worker // SC_WINDOW)
        def _(j):
            off = base + j * SC_WINDOW
            pltpu.sync_copy(idx_hbm.at[pl.ds(off, SC_WINDOW)], idx_v)
            pltpu.async_copy(table_hbm.at[idx_v], rows_v, sem).wait()
            pltpu.sync_copy(rows_v, out_hbm.at[pl.ds(off, SC_WINDOW)])

    return pl.kernel(
        body,
        out_type=jax.ShapeDtypeStruct((n_idx, width), table.dtype),
        mesh=mesh,
        scratch_types=[pltpu.VMEM((SC_WINDOW,), I32), pltpu.VMEM((SC_WINDOW, width), table.dtype),
                       pltpu.SemaphoreType.DMA],
        name="sc_gather",
    )(table, idx)


def _combine_kernel(hp_ref, x1_ref, g2_ref, w_ref, sgw_ref, suw_ref, sdw_ref, yg_ref, o_ref):
    xa, xb = _unpack_halves(hp_ref[...])
    xa = xa.astype(BF16)
    xb = xb.astype(BF16)
    half = xa.shape[1]
    tb = xa.shape[0]
    g = _dot(xa, sgw_ref[0:half]) + _dot(xb, sgw_ref[half:])
    u = _dot(xa, suw_ref[0:half]) + _dot(xb, suw_ref[half:])
    shared = _dot((_silu(g) * u).astype(BF16), sdw_ref[...])

    w = w_ref[...]
    acc_a = jnp.zeros((tb, half), F32)
    acc_b = jnp.zeros((tb, half), F32)
    for k in range(TOP_K):
        ya, yb = _unpack_halves(yg_ref[0, k])
        acc_a += ya * w[:, k:k + 1]
        acc_b += yb * w[:, k:k + 1]
    g2 = g2_ref[0]
    o_ref[:, 0:half] = x1_ref[:, 0:half] + g2[:, 0:half] * (acc_a + shared[:, 0:half])
    o_ref[:, half:] = x1_ref[:, half:] + g2[:, half:] * (acc_b + shared[:, half:])


def _combine(hp, x1, g2, w_tok, sgw, suw, sdw, yg, *, tb, seq_len):
    t, half = hp.shape
    d = 2 * half
    tb = min(tb, seq_len)
    per_seq = seq_len // tb
    f = sgw.shape[1]
    return pl.pallas_call(
        _combine_kernel,
        grid=(t // tb,),
        in_specs=[pl.BlockSpec((tb, half), lambda i: (i, 0)),
                  pl.BlockSpec((tb, d), lambda i: (i, 0)),
                  pl.BlockSpec((1, 1, d), lambda i: (i // per_seq, 0, 0)),
                  pl.BlockSpec((tb, TOP_K), lambda i: (i, 0)),
                  pl.BlockSpec((d, f), lambda i: (0, 0)),
                  pl.BlockSpec((d, f), lambda i: (0, 0)),
                  pl.BlockSpec((f, d), lambda i: (0, 0)),
                  pl.BlockSpec((1, TOP_K, tb, half), lambda i: (i, 0, 0, 0))],
        out_specs=pl.BlockSpec((tb, d), lambda i: (i, 0)),
        out_shape=jax.ShapeDtypeStruct((t, d), F32),
        compiler_params=_cparams("arbitrary"),
        name="combine",
    )(hp, x1, g2, w_tok, sgw, suw, sdw, yg)


def _rope_tables(l):
    rows = l // GRID_W
    r = jnp.repeat(jnp.arange(rows), GRID_W).astype(F32)
    col = jnp.tile(jnp.arange(GRID_W), rows).astype(F32)
    n_f = HEAD_DIM // 4
    freqs = ROPE_THETA ** (-jnp.arange(n_f, dtype=F32) / n_f)
    ang = jnp.concatenate([r[:, None] * freqs, col[:, None] * freqs], axis=-1)
    ang = jnp.tile(jnp.repeat(ang, 2, axis=1), (1, LANES // HEAD_DIM))
    sign = jnp.where(jnp.arange(LANES) % 2 == 0, -1.0, 1.0).astype(F32)
    return jnp.cos(ang), jnp.sin(ang) * sign


def kernel(x, c, ctx, c_ctx, w_mod, b_mod, norm1_w, norm2_w, w_in, q_norm_w, k_norm_w, ret_decay_fwd,
           ret_decay_bwd, w_out, router_w, router_bias, exp_w_gate, exp_w_up, exp_w_down, sh_w_gate,
           sh_w_up, sh_w_down):
    b, l, d = x.shape
    lc = ctx.shape[1]
    t = b * l
    assert w_mod.shape[0] == 1, "single layer"
    assert l % CHUNK == 0 and lc % CHUNK == 0 and l % GRID_W == 0

    rows = (b + 1 + 7) // 8 * 8
    cc = jnp.zeros((rows, d), F32).at[:b].set(c).at[b].set(c_ctx)
    mod = _modulation(cc, w_mod[0], b_mod[0])
    sh1, sc1, g1, sh2, sc2, g2 = [mod[:b, i * d:(i + 1) * d].reshape(b, 1, d) for i in range(6)]
    shc = mod[b, 0:d].reshape(1, 1, d)
    scc = mod[b, d:2 * d].reshape(1, 1, d)

    wi = w_in[0].astype(BF16)
    qnw = jnp.tile(q_norm_w[0], LANES // HEAD_DIM).reshape(1, LANES)
    knw = jnp.tile(k_norm_w[0], LANES // HEAD_DIM).reshape(1, LANES)
    cos, sin = _rope_tables(l)
    n1 = norm1_w[0].reshape(1, d)

    cklo, ckhi, cvlo, cvhi, crk, crv = _projection(
        ctx, shc, scc, n1, wi, qnw, knw, cos[:lc], sin[:lc], rope=False, with_q=False, tm=TILE_PROJ)
    klo, khi, vlo, vhi, rk, rv, q, rq, sg = _projection(
        x, sh1, sc1, n1, wi, qnw, knw, cos, sin, rope=True, with_q=True, tm=TILE_PROJ)

    attn = _attention(q, klo, khi, vlo, vhi, cklo, ckhi, cvlo, cvhi, tq=TILE_ATTN_Q)
    dec_f = jnp.repeat(ret_decay_fwd[0].astype(F32), HEAD_DIM).reshape(1, RET_W)
    dec_b = jnp.repeat(ret_decay_bwd[0].astype(F32), HEAD_DIM).reshape(1, RET_W)
    ret = _retention(rq, rk, rv, sg, crk, crv, dec_f, dec_b)

    wo = w_out[0].astype(BF16)
    r_hi, r_lo = _split(router_w[0].T)
    x1, hp, logits_t = _out_projection(attn, ret, x, wo[:ATTN_W], wo[ATTN_W:], g1, sh2, sc2,
                                       norm2_w[0].reshape(1, d), r_hi, r_lo, tm=TILE_OUT)

    idx_t, w_t, rank_t, cnt_col, cnt_row = _route(logits_t, router_bias[0].reshape(N_EXPERTS, 1), tb=TILE_TOKENS)
    n_blocks = -(-(t * TOP_K) // MOE_BLOCK) + N_EXPERTS
    tb = TILE_TOKENS
    dest, meta, items = _destinations(idx_t, rank_t, cnt_col, cnt_row, tb=tb, n_blocks=n_blocks)
    dest_flat = dest.reshape(-1)
    steps, _, tbe = dest.shape
    dest_win = dest.reshape(steps, TOP_K, tbe // SC_WINDOW, SC_WINDOW).transpose(0, 2, 1, 3)
    dest_win = dest_win.reshape(t // SC_WINDOW, TOP_K, SC_WINDOW)
    xs = _sc_scatter_rows(hp, dest_win, (n_blocks + ITEM_BLOCKS - 1) * MOE_BLOCK)
    xs = _pad_fill(meta, xs)
    ys = _experts(items, xs, exp_w_gate[0], exp_w_up[0], exp_w_down[0], n_blocks=n_blocks)
    yg = _sc_gather_rows(ys, dest_flat).reshape(dest.shape + (d // 2,))
    out = _combine(hp, x1.reshape(t, d), g2, w_t.T, sh_w_gate[0].astype(BF16), sh_w_up[0].astype(BF16),
                   sh_w_down[0].astype(BF16), yg, tb=tb, seq_len=l)
    return out.reshape(b, l, d)
```

```python
import functools

import jax
import jax.numpy as jnp
from jax import lax
from jax.experimental import pallas as pl
from jax.experimental.pallas import tpu as pltpu
from jax.experimental.pallas import tpu_sc as plsc

F32 = jnp.float32
BF16 = jnp.bfloat16
I32 = jnp.int32
U32 = jnp.uint32
PACKED = jnp.int32

HEAD_DIM = 64
LANES = 128
SUBLANES = 8
ATTN_HEADS = 8
ATTN_KV_HEADS = 2
GQA = ATTN_HEADS // ATTN_KV_HEADS
RET_HEADS = 8
ATTN_W = ATTN_HEADS * HEAD_DIM
KV_W = ATTN_KV_HEADS * HEAD_DIM
RET_W = RET_HEADS * HEAD_DIM
RET_PAIRS = RET_W // LANES
CHUNK = 128
GRID_W = 64
ROPE_THETA = 10000.0
N_EXPERTS = 256
TOP_K = 8
N_GROUPS = 8
GROUP_SIZE = N_EXPERTS // N_GROUPS
TOPK_GROUPS = 4
ROUTED_SCALE = 2.5
MOE_BLOCK = 128
EPS = 1e-6
QK_SCALE = HEAD_DIM ** -0.5
LOG2_E = 1.4426950408889634

OFF_AK = 0
OFF_AV = OFF_AK + KV_W
OFF_RK = OFF_AV + KV_W
OFF_RV = OFF_RK + RET_W
CTX_KV_COLS = OFF_RV + RET_W
OFF_AQ = CTX_KV_COLS
OFF_RQ = OFF_AQ + ATTN_W
OFF_RG = OFF_RQ + RET_W
IN_COLS = OFF_RG + RET_W

VMEM_LIMIT = 52 * 1024 * 1024

TILE_PROJ = 512
TILE_ATTN_Q = 256
TILE_OUT = 512
TILE_TOKENS = 256
ITEM_BLOCKS = 5
SC_GATHER_BUFS = 2
WEIGHT_SLOTS = 3
SC_WINDOW = 64
HI_MASK = 0xFFFF0000


def _cparams(*sem):
    return pltpu.CompilerParams(dimension_semantics=sem, vmem_limit_bytes=VMEM_LIMIT)


def _split(a):
    hi = a.astype(BF16)
    lo = (a - hi.astype(F32)).astype(BF16)
    return hi, lo


def _dot(a, b):
    return jnp.dot(a, b, preferred_element_type=F32)


def _dot_nt(a, b):
    return lax.dot_general(a, b, (((1,), (1,)), ((), ())), preferred_element_type=F32)


def _sigmoid(v):
    return 1.0 / (1.0 + jnp.exp(-v))


def _silu(v):
    return v * _sigmoid(v)


def _pack_halves(a, b):
    ua = lax.bitcast_convert_type(a.astype(BF16).astype(F32), U32)
    ub = lax.bitcast_convert_type(b.astype(BF16).astype(F32), U32)
    return lax.bitcast_convert_type((ua & jnp.uint32(HI_MASK)) | (ub >> 16), PACKED)


def _unpack_halves(p):
    u = lax.bitcast_convert_type(p, U32)
    a = lax.bitcast_convert_type(u & jnp.uint32(HI_MASK), F32)
    b = lax.bitcast_convert_type(u << 16, F32)
    return a, b


def _mod_kernel(c_ref, w_ref, b_ref, o_ref):
    s_hi, s_lo = _split(_silu(c_ref[...]))
    w_hi, w_lo = _split(w_ref[...])
    o_ref[...] = _dot(s_hi, w_hi) + _dot(s_hi, w_lo) + _dot(s_lo, w_hi) + b_ref[...]


def _modulation(cc, w_mod, b_mod):
    rows, d = cc.shape
    n = w_mod.shape[1]
    tn = 768
    return pl.pallas_call(
        _mod_kernel,
        grid=(n // tn,),
        in_specs=[pl.BlockSpec((rows, d), lambda j: (0, 0)),
                  pl.BlockSpec((d, tn), lambda j: (0, j)),
                  pl.BlockSpec((1, tn), lambda j: (0, j))],
        out_specs=pl.BlockSpec((rows, tn), lambda j: (0, j)),
        out_shape=jax.ShapeDtypeStruct((rows, n), F32),
        compiler_params=_cparams("arbitrary"),
        name="mod",
    )(cc, w_mod, b_mod.reshape(1, n))


def _segment_ones():
    r = lax.broadcasted_iota(I32, (LANES, LANES), 0) // HEAD_DIM
    c = lax.broadcasted_iota(I32, (LANES, LANES), 1) // HEAD_DIM
    return jnp.where(r == c, 1.0, 0.0).astype(BF16)


def _head_mean_sq(v, seg):
    hi, lo = _split(v * v)
    return (_dot(hi, seg) + _dot(lo, seg)) * (1.0 / HEAD_DIM)


def _proj_kernel(x_ref, sh_ref, sc_ref, nw_ref, wi_ref, qnw_ref, knw_ref, cos_ref, sin_ref,
                 *out_refs, rope, with_q):
    if with_q:
        klo_ref, khi_ref, vlo_ref, vhi_ref, rk_ref, rv_ref, q_ref, rq_ref, sg_ref = out_refs
    else:
        klo_ref, khi_ref, vlo_ref, vhi_ref, rk_ref, rv_ref = out_refs
    x = x_ref[0]
    h = x * lax.rsqrt(jnp.mean(x * x, axis=-1, keepdims=True) + EPS) * nw_ref[...]
    h = h * (1.0 + sc_ref[0]) + sh_ref[0]
    z = _dot(h.astype(BF16), wi_ref[...])

    seg = _segment_ones()
    lane = lax.broadcasted_iota(I32, (x.shape[0], LANES), 1)
    low_half = lane < HEAD_DIM
    even = (lane & 1) == 0

    def norm_rope(v, w128):
        v = v * lax.rsqrt(_head_mean_sq(v, seg) + EPS) * w128
        if rope:
            swapped = jnp.where(even, pltpu.roll(v, LANES - 1, 1), pltpu.roll(v, 1, 1))
            v = v * cos_ref[...] + swapped * sin_ref[...]
        return v

    def write_lo_hi(v, lo_ref, hi_ref, transposed):
        sw = pltpu.roll(v, HEAD_DIM, 1)
        fix = (lambda a: jnp.transpose(a).astype(BF16)) if transposed else (lambda a: a.astype(BF16))
        lo_ref[0, 0] = fix(jnp.where(low_half, v, 0.0))
        hi_ref[0, 0] = fix(jnp.where(low_half, 0.0, sw))
        lo_ref[0, 1] = fix(jnp.where(low_half, sw, 0.0))
        hi_ref[0, 1] = fix(jnp.where(low_half, 0.0, v))

    write_lo_hi(norm_rope(z[:, OFF_AK:OFF_AK + KV_W], knw_ref[...]), klo_ref, khi_ref, True)
    write_lo_hi(z[:, OFF_AV:OFF_AV + KV_W], vlo_ref, vhi_ref, False)
    rk_ref[0] = (z[:, OFF_RK:OFF_RK + RET_W] * QK_SCALE).astype(BF16)
    rv_ref[0] = z[:, OFF_RV:OFF_RV + RET_W].astype(BF16)
    if with_q:
        for j in range(ATTN_W // LANES):
            qj = norm_rope(z[:, OFF_AQ + j * LANES:OFF_AQ + (j + 1) * LANES], qnw_ref[...])
            q_ref[0, :, j * LANES:(j + 1) * LANES] = (qj * (QK_SCALE * LOG2_E)).astype(BF16)
        rq_ref[0] = z[:, OFF_RQ:OFF_RQ + RET_W].astype(BF16)
        sg_ref[0] = _silu(z[:, OFF_RG:OFF_RG + RET_W]).astype(BF16)


def _projection(x, shift, scale, norm_w, wi_bf16, qnw, knw, cos, sin, *, rope, with_q, tm):
    b, l, d = x.shape
    tm = min(tm, l)
    ncols = IN_COLS if with_q else CTX_KV_COLS
    per_batch = shift.shape[0] > 1
    mod_idx = (lambda bi, i: (bi, 0, 0)) if per_batch else (lambda bi, i: (0, 0, 0))
    kv_shape = jax.ShapeDtypeStruct((b, ATTN_KV_HEADS, l, LANES), BF16)
    kv_spec = pl.BlockSpec((1, ATTN_KV_HEADS, tm, LANES), lambda bi, i: (bi, 0, i, 0))
    kt_shape = jax.ShapeDtypeStruct((b, ATTN_KV_HEADS, LANES, l), BF16)
    kt_spec = pl.BlockSpec((1, ATTN_KV_HEADS, LANES, tm), lambda bi, i: (bi, 0, 0, i))
    w_shape = jax.ShapeDtypeStruct((b, l, RET_W), BF16)
    w_spec = pl.BlockSpec((1, tm, RET_W), lambda bi, i: (bi, i, 0))
    out_shape = [kt_shape] * 2 + [kv_shape] * 2 + [w_shape] * 2
    out_specs = [kt_spec] * 2 + [kv_spec] * 2 + [w_spec] * 2
    if with_q:
        out_shape += [w_shape] * 3
        out_specs += [w_spec] * 3
    return pl.pallas_call(
        functools.partial(_proj_kernel, rope=rope, with_q=with_q),
        grid=(b, l // tm),
        in_specs=[pl.BlockSpec((1, tm, d), lambda bi, i: (bi, i, 0)),
                  pl.BlockSpec((1, 1, d), mod_idx),
                  pl.BlockSpec((1, 1, d), mod_idx),
                  pl.BlockSpec((1, d), lambda bi, i: (0, 0)),
                  pl.BlockSpec((d, ncols), lambda bi, i: (0, 0)),
                  pl.BlockSpec((1, LANES), lambda bi, i: (0, 0)),
                  pl.BlockSpec((1, LANES), lambda bi, i: (0, 0)),
                  pl.BlockSpec((tm, LANES), lambda bi, i: (i, 0)),
                  pl.BlockSpec((tm, LANES), lambda bi, i: (i, 0))],
        out_specs=out_specs,
        out_shape=out_shape,
        compiler_params=_cparams("arbitrary", "arbitrary"),
        name="proj_latent" if with_q else "proj_ctx",
    )(x, shift, scale, norm_w, wi_bf16, qnw, knw, cos, sin)


def _attn_kernel(q_ref, klo_ref, khi_ref, vlo_ref, vhi_ref, cklo_ref, ckhi_ref, cvlo_ref, cvhi_ref,
                 o_ref, kl_s, kh_s, va_s, *, l, lc):
    lk = l + lc

    @pl.when(pl.program_id(2) == 0)
    def _():
        kl_s[:, 0:l] = klo_ref[0, 0]
        kl_s[:, l:lk] = cklo_ref[0, 0]
        kh_s[:, 0:l] = khi_ref[0, 0]
        kh_s[:, l:lk] = ckhi_ref[0, 0]
        lane = lax.broadcasted_iota(I32, (lk, LANES), 1)
        ones_lo = jnp.where(lane < HEAD_DIM, 1.0, 0.0).astype(BF16)
        ones_hi = jnp.where(lane < HEAD_DIM, 0.0, 1.0).astype(BF16)
        for g, (v_ref, cv_ref, ones) in enumerate(((vlo_ref, cvlo_ref, ones_lo), (vhi_ref, cvhi_ref, ones_hi),
                                                   (vlo_ref, cvlo_ref, ones_lo), (vhi_ref, cvhi_ref, ones_hi))):
            v_col, one_col = (0, LANES) if g < 2 else (LANES, 0)
            va_s[g, 0:l, v_col:v_col + LANES] = v_ref[0, 0]
            va_s[g, l:lk, v_col:v_col + LANES] = cv_ref[0, 0]
            va_s[g, :, one_col:one_col + LANES] = ones

    q = q_ref[0]
    acc = []
    for g in range(GQA):
        qp = q[:, (g // 2) * LANES:(g // 2 + 1) * LANES]
        kt = kl_s[...] if g % 2 == 0 else kh_s[...]
        s = _dot(qp, kt)
        p = jnp.exp2(s - jnp.max(s, axis=-1, keepdims=True)).astype(BF16)
        acc.append(_dot(p, va_s[g]))
    out_a = acc[0] + acc[1]
    out_b = acc[2] + acc[3]
    o_ref[0, :, 0:LANES] = (out_a[:, 0:LANES] / out_a[:, LANES:2 * LANES]).astype(BF16)
    o_ref[0, :, LANES:2 * LANES] = (out_b[:, LANES:2 * LANES] / out_b[:, 0:LANES]).astype(BF16)


def _attention(q, klo, khi, vlo, vhi, cklo, ckhi, cvlo, cvhi, *, tq):
    b, l, _ = q.shape
    lc = cvlo.shape[2]
    lk = l + lc
    tq = min(tq, l)
    gw = GQA * HEAD_DIM
    kt_spec = pl.BlockSpec((1, 1, LANES, l), lambda bi, h, i: (bi, h, 0, 0))
    kv_spec = pl.BlockSpec((1, 1, l, LANES), lambda bi, h, i: (bi, h, 0, 0))
    ckt_spec = pl.BlockSpec((1, 1, LANES, lc), lambda bi, h, i: (bi, h, 0, 0))
    ckv_spec = pl.BlockSpec((1, 1, lc, LANES), lambda bi, h, i: (bi, h, 0, 0))
    return pl.pallas_call(
        functools.partial(_attn_kernel, l=l, lc=lc),
        grid=(b, ATTN_KV_HEADS, l // tq),
        in_specs=([pl.BlockSpec((1, tq, gw), lambda bi, h, i: (bi, i, h))] + [kt_spec] * 2 + [kv_spec] * 2
                  + [ckt_spec] * 2 + [ckv_spec] * 2),
        out_specs=pl.BlockSpec((1, tq, gw), lambda bi, h, i: (bi, i, h)),
        out_shape=jax.ShapeDtypeStruct((b, l, ATTN_W), BF16),
        scratch_shapes=[pltpu.VMEM((LANES, lk), BF16), pltpu.VMEM((LANES, lk), BF16),
                        pltpu.VMEM((GQA, lk, 2 * LANES), BF16)],
        compiler_params=_cparams("arbitrary", "arbitrary", "arbitrary"),
        name="attn",
    )(q, klo, khi, vlo, vhi, cklo, ckhi, cvlo, cvhi)


def _log_sigmoid(v):
    return jnp.minimum(v, 0.0) - jnp.log(1.0 + jnp.exp(-jnp.abs(v)))


def _ret_kernel(rq_ref, rk_ref, rv_ref, sg_ref, crk_ref, crv_ref, df_ref, db_ref, o_ref,
                m_s, tab_s, sb_s, sf_s, *, l, lc):
    n = l // CHUNK
    nc = lc // CHUNK
    lgf = _log_sigmoid(df_ref[...])
    lgb = _log_sigmoid(db_ref[...])
    pos = lax.broadcasted_iota(I32, (CHUNK, RET_W), 0).astype(F32)
    tab_s[0] = jnp.exp((pos + 1.0) * lgf)
    tab_s[1] = jnp.exp((CHUNK - pos) * lgb)
    tab_s[2] = jnp.exp((CHUNK - 1.0 - pos) * lgf)
    tab_s[3] = jnp.exp(pos * lgb)
    gf_c = jnp.exp(CHUNK * lgf)
    gb_c = jnp.exp(CHUNK * lgb)
    row = lax.broadcasted_iota(I32, (CHUNK, CHUNK), 0)
    col = lax.broadcasted_iota(I32, (CHUNK, CHUNK), 1)
    diff = (row - col).astype(F32)
    for h in range(RET_HEADS):
        lf = lgf[:, h * HEAD_DIM:h * HEAD_DIM + 1]
        lb = lgb[:, h * HEAD_DIM:h * HEAD_DIM + 1]
        m_s[h] = jnp.where(diff > 0, jnp.exp(diff * lf), jnp.where(diff < 0, jnp.exp(-diff * lb), 2.0))

    lane = lax.broadcasted_iota(I32, (CHUNK, LANES), 1)
    low_half = lane < HEAD_DIM
    diag = (lax.broadcasted_iota(I32, (LANES, LANES), 0) // HEAD_DIM
            == lax.broadcasted_iota(I32, (LANES, LANES), 1) // HEAD_DIM)
    seg = jnp.where(diag, 1.0, 0.0).astype(BF16)

    def contrib(k_ref, v_ref, r0, p, zeta_idx):
        cols = slice(p * LANES, (p + 1) * LANES)
        kz = k_ref[0, pl.ds(r0, CHUNK), cols].astype(F32) * tab_s[zeta_idx, :, cols]
        kv = _dot(jnp.transpose(kz).astype(BF16), v_ref[0, pl.ds(r0, CHUNK), cols])
        return jnp.where(diag, kv, 0.0)

    for p in range(RET_PAIRS):
        cols = slice(p * LANES, (p + 1) * LANES)
        sf = jnp.zeros((LANES, LANES), F32)
        sb = jnp.zeros((LANES, LANES), F32)
        for c in range(nc):
            sf = gf_c[:, cols] * sf + contrib(crk_ref, crv_ref, c * CHUNK, p, 2)
            cb = nc - 1 - c
            sb = gb_c[:, cols] * sb + contrib(crk_ref, crv_ref, cb * CHUNK, p, 3)
        sf_s[p] = sf
        sb_s[n, p] = sb

    def bwd_body(j, carry):
        c = n - 1 - j
        r0 = pl.multiple_of(c * CHUNK, CHUNK)
        for p in range(RET_PAIRS):
            cols = slice(p * LANES, (p + 1) * LANES)
            prev = sb_s[c + 1, p]
            sb_s[c, p] = gb_c[:, cols] * prev + contrib(rk_ref, rv_ref, r0, p, 3)
        return carry

    lax.fori_loop(0, n, bwd_body, 0)

    def fwd_body(c, carry):
        r0 = pl.multiple_of(c * CHUNK, CHUNK)
        for p in range(RET_PAIRS):
            cols = slice(p * LANES, (p + 1) * LANES)
            qp = rq_ref[0, pl.ds(r0, CHUNK), cols]
            kp = rk_ref[0, pl.ds(r0, CHUNK), cols]
            vp = rv_ref[0, pl.ds(r0, CHUNK), cols]
            zero = jnp.zeros_like(kp)
            s0 = _dot_nt(qp, jnp.where(low_half, kp, zero))
            s1 = _dot_nt(qp, jnp.where(low_half, zero, kp))
            a0 = (s0 * m_s[2 * p]).astype(BF16)
            a1 = (s1 * m_s[2 * p + 1]).astype(BF16)
            y = _dot(a0, jnp.where(low_half, vp, zero)) + _dot(a1, jnp.where(low_half, zero, vp))
            qf = qp.astype(F32)
            sf = sf_s[p]
            y += _dot((qf * tab_s[0, :, cols]).astype(BF16), sf.astype(BF16))
            y += _dot((qf * tab_s[1, :, cols]).astype(BF16), sb_s[c + 1, p].astype(BF16))
            hi, lo = _split(y * y)
            ms = (_dot(hi, seg) + _dot(lo, seg)) * (1.0 / HEAD_DIM)
            out = y * lax.rsqrt(ms + EPS) * sg_ref[0, pl.ds(r0, CHUNK), cols].astype(F32)
            o_ref[0, pl.ds(r0, CHUNK), cols] = out.astype(BF16)
            sf_s[p] = gf_c[:, cols] * sf + contrib(rk_ref, rv_ref, r0, p, 2)
        return carry

    lax.fori_loop(0, n, fwd_body, 0)


def _retention(rq, rk, rv, sg, crk, crv, dec_f, dec_b):
    b, l, _ = rq.shape
    lc = crk.shape[1]
    n = l // CHUNK
    spec = pl.BlockSpec((1, l, RET_W), lambda bi: (bi, 0, 0))
    cspec = pl.BlockSpec((1, lc, RET_W), lambda bi: (bi, 0, 0))
    dspec = pl.BlockSpec((1, RET_W), lambda bi: (0, 0))
    return pl.pallas_call(
        functools.partial(_ret_kernel, l=l, lc=lc),
        grid=(b,),
        in_specs=[spec, spec, spec, spec, cspec, cspec, dspec, dspec],
        out_specs=spec,
        out_shape=jax.ShapeDtypeStruct((b, l, RET_W), BF16),
        scratch_shapes=[pltpu.VMEM((RET_HEADS, CHUNK, CHUNK), F32),
                        pltpu.VMEM((4, CHUNK, RET_W), F32),
                        pltpu.VMEM((n + 1, RET_PAIRS, LANES, LANES), F32),
                        pltpu.VMEM((RET_PAIRS, LANES, LANES), F32)],
        compiler_params=_cparams("arbitrary"),
        name="ret",
    )(rq, rk, rv, sg, crk, crv, dec_f, dec_b)


def _out_kernel(attn_ref, ret_ref, x_ref, wa_ref, wr_ref, g1_ref, sh_ref, sc_ref, nw_ref, rhi_ref, rlo_ref,
                x1_ref, hp_ref, lg_ref):
    y = _dot(attn_ref[0], wa_ref[...]) + _dot(ret_ref[0], wr_ref[...])
    x1 = x_ref[0] + g1_ref[0] * y
    x1_ref[0] = x1
    h = x1 * lax.rsqrt(jnp.mean(x1 * x1, axis=-1, keepdims=True) + EPS) * nw_ref[...]
    h = h * (1.0 + sc_ref[0]) + sh_ref[0]
    half = h.shape[1] // 2
    hp_ref[...] = _pack_halves(h[:, :half], h[:, half:])
    h_hi, h_lo = _split(h)
    lg_ref[...] = _dot_nt(rhi_ref[...], h_hi) + _dot_nt(rhi_ref[...], h_lo) + _dot_nt(rlo_ref[...], h_hi)


def _out_projection(attn, ret, x, wa, wr, g1, sh2, sc2, norm_w, r_hi, r_lo, *, tm):
    b, l, d = x.shape
    tm = min(tm, l)
    nt = l // tm
    t = b * l
    mspec = pl.BlockSpec((1, 1, d), lambda bi, i: (bi, 0, 0))
    return pl.pallas_call(
        _out_kernel,
        grid=(b, nt),
        in_specs=[pl.BlockSpec((1, tm, ATTN_W), lambda bi, i: (bi, i, 0)),
                  pl.BlockSpec((1, tm, RET_W), lambda bi, i: (bi, i, 0)),
                  pl.BlockSpec((1, tm, d), lambda bi, i: (bi, i, 0)),
                  pl.BlockSpec((ATTN_W, d), lambda bi, i: (0, 0)),
                  pl.BlockSpec((RET_W, d), lambda bi, i: (0, 0)),
                  mspec, mspec, mspec,
                  pl.BlockSpec((1, d), lambda bi, i: (0, 0)),
                  pl.BlockSpec((N_EXPERTS, d), lambda bi, i: (0, 0)),
                  pl.BlockSpec((N_EXPERTS, d), lambda bi, i: (0, 0))],
        out_specs=[pl.BlockSpec((1, tm, d), lambda bi, i: (bi, i, 0)),
                   pl.BlockSpec((tm, d // 2), lambda bi, i: (bi * nt + i, 0)),
                   pl.BlockSpec((N_EXPERTS, tm), lambda bi, i: (0, bi * nt + i))],
        out_shape=[jax.ShapeDtypeStruct((b, l, d), F32),
                   jax.ShapeDtypeStruct((t, d // 2), PACKED),
                   jax.ShapeDtypeStruct((N_EXPERTS, t), F32)],
        compiler_params=_cparams("arbitrary", "arbitrary"),
        name="out_proj",
    )(attn, ret, x, wa, wr, g1, sh2, sc2, norm_w, r_hi, r_lo)


def _route_kernel(lg_ref, bias_ref, idx_ref, w_ref, rank_ref, cnt_col_ref, cnt_row_ref, tri_s, col_s, row_s):
    tb = lg_ref.shape[1]
    step = pl.program_id(0)

    @pl.when(step == 0)
    def _():
        r = lax.broadcasted_iota(I32, (tb, tb), 0)
        c = lax.broadcasted_iota(I32, (tb, tb), 1)
        tri_s[...] = jnp.where(r <= c, 1.0, 0.0).astype(BF16)
        col_s[...] = jnp.zeros_like(col_s)
        row_s[...] = jnp.zeros_like(row_s)

    scores = _sigmoid(lg_ref[...])
    biased = scores + bias_ref[...]
    neg = -jnp.inf
    sub = lax.broadcasted_iota(I32, (GROUP_SIZE, tb), 0).astype(F32)

    gscore = []
    for g in range(N_GROUPS):
        blk = biased[g * GROUP_SIZE:(g + 1) * GROUP_SIZE]
        m1 = jnp.max(blk, axis=0, keepdims=True)
        first = jnp.min(jnp.where(blk == m1, sub, float(GROUP_SIZE)), axis=0, keepdims=True)
        m2 = jnp.max(jnp.where(sub == first, neg, blk), axis=0, keepdims=True)
        gscore.append(m1 + m2)
    gs = jnp.concatenate(gscore, axis=0)
    gsub = lax.broadcasted_iota(I32, (N_GROUPS, tb), 0).astype(F32)
    keep = jnp.zeros((N_GROUPS, tb), F32)
    for _ in range(TOPK_GROUPS):
        m = jnp.max(gs, axis=0, keepdims=True)
        first = jnp.min(jnp.where(gs == m, gsub, float(N_GROUPS)), axis=0, keepdims=True)
        sel = gsub == first
        keep = jnp.where(sel, 1.0, keep)
        gs = jnp.where(sel, neg, gs)
    masked = jnp.concatenate(
        [jnp.where(keep[g:g + 1] > 0.0, biased[g * GROUP_SIZE:(g + 1) * GROUP_SIZE], neg)
         for g in range(N_GROUPS)], axis=0)

    esub = lax.broadcasted_iota(I32, (N_EXPERTS, tb), 0).astype(F32)
    sels, idxs, ws = [], [], []
    chosen = jnp.zeros((N_EXPERTS, tb), F32)
    for _ in range(TOP_K):
        m = jnp.max(masked, axis=0, keepdims=True)
        first = jnp.min(jnp.where(masked == m, esub, float(N_EXPERTS)), axis=0, keepdims=True)
        sel = esub == first
        sels.append(sel)
        idxs.append(first)
        ws.append(jnp.sum(jnp.where(sel, scores, 0.0), axis=0, keepdims=True))
        chosen = jnp.where(sel, 1.0, chosen)
        masked = jnp.where(sel, neg, masked)
    wsum = ws[0]
    for k in range(1, TOP_K):
        wsum = wsum + ws[k]
    idx_ref[...] = jnp.concatenate(idxs, axis=0).astype(I32)
    w_ref[...] = jnp.concatenate([wk / wsum * ROUTED_SCALE for wk in ws], axis=0)

    chosen_b = chosen.astype(BF16)
    incl = _dot(chosen_b, tri_s[...])
    before = incl - chosen + col_s[...]
    rank_ref[...] = jnp.concatenate(
        [jnp.sum(jnp.where(sel, before, 0.0), axis=0, keepdims=True) for sel in sels], axis=0).astype(I32)
    col_s[...] = col_s[...] + incl[:, tb - 1:tb]
    row_s[...] = row_s[...] + _dot_nt(jnp.ones((8, tb), BF16), chosen_b)
    cnt_col_ref[...] = col_s[...].astype(I32)
    cnt_row_ref[...] = row_s[...].astype(I32)


def _route(logits_t, bias_col, *, tb):
    e, t = logits_t.shape
    tb = min(tb, t)
    kspec = pl.BlockSpec((TOP_K, tb), lambda i: (0, i))
    return pl.pallas_call(
        _route_kernel,
        grid=(t // tb,),
        in_specs=[pl.BlockSpec((e, tb), lambda i: (0, i)),
                  pl.BlockSpec((e, 1), lambda i: (0, 0))],
        out_specs=[kspec, kspec, kspec,
                   pl.BlockSpec((e, 1), lambda i: (0, 0)),
                   pl.BlockSpec((8, e), lambda i: (0, 0))],
        out_shape=[jax.ShapeDtypeStruct((TOP_K, t), I32),
                   jax.ShapeDtypeStruct((TOP_K, t), F32),
                   jax.ShapeDtypeStruct((TOP_K, t), I32),
                   jax.ShapeDtypeStruct((e, 1), I32),
                   jax.ShapeDtypeStruct((8, e), I32)],
        scratch_shapes=[pltpu.VMEM((tb, tb), BF16), pltpu.VMEM((e, 1), F32), pltpu.VMEM((8, e), F32)],
        compiler_params=_cparams("arbitrary"),
        name="route",
    )(logits_t, bias_col)


def _pad_block(cnt):
    return (cnt + (MOE_BLOCK - 1)) // MOE_BLOCK * MOE_BLOCK


def _max_items(n_blocks):
    return n_blocks // ITEM_BLOCKS + N_EXPERTS


def _dest_kernel(idx_ref, rank_ref, cnt_col_ref, cnt_row_ref, dest_ref, meta_ref, items_ref):
    tb = idx_ref.shape[1]
    nip = items_ref.shape[1]
    pad_col = _pad_block(cnt_col_ref[...])
    pad_row = _pad_block(cnt_row_ref[0:1, :])
    er = lax.broadcasted_iota(I32, (N_EXPERTS, N_EXPERTS), 0)
    ec = lax.broadcasted_iota(I32, (N_EXPERTS, N_EXPERTS), 1)
    start_col = jnp.sum(jnp.where(ec < er, pad_row, 0), axis=1, keepdims=True)
    start_row = jnp.sum(jnp.where(er < ec, pad_col, 0), axis=0, keepdims=True)

    esub = lax.broadcasted_iota(I32, (N_EXPERTS, tb), 0)
    rows = []
    for k in range(TOP_K):
        onehot = esub == idx_ref[k:k + 1, :]
        rows.append(jnp.sum(jnp.where(onehot, start_col, 0), axis=0, keepdims=True) + rank_ref[k:k + 1, :])
    dest_ref[0] = jnp.concatenate(rows, axis=0)

    used = jnp.sum(pad_row, axis=1, keepdims=True) // MOE_BLOCK
    meta_ref[...] = jnp.concatenate(
        [cnt_row_ref[0:1, :], start_row, pad_row, jnp.broadcast_to(used, (1, N_EXPERTS)),
         jnp.zeros((4, N_EXPERTS), I32)], axis=0)

    nb_col = pad_col // MOE_BLOCK
    it_col = (nb_col + (ITEM_BLOCKS - 1)) // ITEM_BLOCKS
    it_row = (pad_row // MOE_BLOCK + (ITEM_BLOCKS - 1)) // ITEM_BLOCKS
    it_start = jnp.sum(jnp.where(ec < er, it_row, 0), axis=1, keepdims=True)
    n_items = jnp.sum(it_row, axis=1, keepdims=True)
    lane = lax.broadcasted_iota(I32, (1, nip), 1)
    owner = jnp.sum(jnp.where(it_start + it_col <= lane, 1, 0), axis=0, keepdims=True)
    owner = jnp.minimum(owner, N_EXPERTS - 1)
    onehot = lax.broadcasted_iota(I32, (N_EXPERTS, nip), 0) == owner

    def pick(col):
        return jnp.sum(jnp.where(onehot, col, 0), axis=0, keepdims=True)

    j = lane - pick(it_start)
    block0 = pick(start_col) // MOE_BLOCK + ITEM_BLOCKS * j
    nvalid = jnp.clip(pick(nb_col) - ITEM_BLOCKS * j, 0, ITEM_BLOCKS)
    items_ref[...] = jnp.concatenate(
        [owner, block0, jnp.where(lane < n_items, nvalid, 0), jnp.broadcast_to(n_items, (1, nip)),
         jnp.zeros((4, nip), I32)], axis=0)


def _destinations(idx_t, rank_t, cnt_col, cnt_row, *, tb, n_blocks):
    _, t = idx_t.shape
    tb = min(tb, t)
    nip = (_max_items(n_blocks) + LANES - 1) // LANES * LANES
    kspec = pl.BlockSpec((TOP_K, tb), lambda i: (0, i))
    return pl.pallas_call(
        _dest_kernel,
        grid=(t // tb,),
        in_specs=[kspec, kspec,
                  pl.BlockSpec((N_EXPERTS, 1), lambda i: (0, 0)),
                  pl.BlockSpec((8, N_EXPERTS), lambda i: (0, 0))],
        out_specs=[pl.BlockSpec((1, TOP_K, tb), lambda i: (i, 0, 0)),
                   pl.BlockSpec((8, N_EXPERTS), lambda i: (0, 0)),
                   pl.BlockSpec((8, nip), lambda i: (0, 0))],
        out_shape=[jax.ShapeDtypeStruct((t // tb, TOP_K, tb), I32),
                   jax.ShapeDtypeStruct((8, N_EXPERTS), I32),
                   jax.ShapeDtypeStruct((8, nip), I32)],
        compiler_params=_cparams("arbitrary"),
        name="dest",
    )(idx_t, rank_t, cnt_col, cnt_row)


_PAD_BITS = (64, 32, 16, 8)


def _sc_scatter_rows(rows, dest_win, n_out_rows):
    n_win, n_slots, win = dest_win.shape
    width = rows.shape[1]
    info = plsc.get_sparse_core_info()
    n_workers = info.num_cores * info.num_subcores
    per_worker = n_win // n_workers
    assert per_worker * n_workers == n_win and win <= LANES
    mesh = plsc.VectorSubcoreMesh(core_axis_name="c", subcore_axis_name="s")

    def body(rows_hbm, dest_hbm, out_hbm, idx_v, rows_v, sem):
        wid = lax.axis_index("s") * info.num_cores + lax.axis_index("c")

        @pl.loop(0, per_worker)
        def _(j):
            w = wid * per_worker + j
            pltpu.sync_copy(dest_hbm.at[w], idx_v)
            pltpu.sync_copy(rows_hbm.at[pl.ds(w * win, win)], rows_v)
            copies = [pltpu.async_copy(rows_v, out_hbm.at[idx_v.at[k]], sem) for k in range(n_slots)]
            for cp in copies:
                cp.wait()

    return pl.kernel(
        body,
        out_type=jax.ShapeDtypeStruct((n_out_rows, width), rows.dtype),
        mesh=mesh,
        scratch_types=[pltpu.VMEM((n_slots, win), I32), pltpu.VMEM((win, width), rows.dtype),
                       pltpu.SemaphoreType.DMA],
        name="sc_scatter",
    )(rows, dest_win)


def _pad_fill_kernel(meta_ref, xs_in, xs_hbm, zero_s, sem_z, *, e_per_step):
    del xs_in
    step = pl.program_id(0)
    zero_s[...] = jnp.zeros_like(zero_s)

    def tail_copy(c):
        row0 = pl.multiple_of((meta_ref[3, 0] + c) * MOE_BLOCK, MOE_BLOCK)
        return pltpu.make_async_copy(zero_s, xs_hbm.at[pl.ds(row0, MOE_BLOCK)], sem_z)

    @pl.when(step == 0)
    def _():
        for c in range(ITEM_BLOCKS - 1):
            tail_copy(c).start()
        for c in range(ITEM_BLOCKS - 1):
            tail_copy(c).wait()

    def pad_copies(e):
        cnt = meta_ref[0, e]
        off = meta_ref[1, e] + cnt
        rem = meta_ref[2, e] - cnt
        head = rem & (SUBLANES - 1)
        out = []
        for i in range(SUBLANES - 1):
            out.append((i < head,
                        pltpu.make_async_copy(zero_s.at[pl.ds(0, 1)], xs_hbm.at[pl.ds(off + i, 1)], sem_z)))
        off = off + head
        for bit in _PAD_BITS:
            out.append(((rem & bit) != 0,
                        pltpu.make_async_copy(zero_s.at[pl.ds(0, bit)],
                                              xs_hbm.at[pl.ds(pl.multiple_of(off, SUBLANES), bit)], sem_z)))
            off = off + (rem & bit)
        return out

    for j in range(e_per_step):
        for cond, c in pad_copies(step * e_per_step + j):
            pl.when(cond)(c.start)
    for j in range(e_per_step):
        for cond, c in pad_copies(step * e_per_step + j):
            pl.when(cond)(c.wait)


PAD_FILL_EXPERTS_PER_STEP = 8


def _pad_fill(meta, xs):
    half = xs.shape[1]
    return pl.pallas_call(
        functools.partial(_pad_fill_kernel, e_per_step=PAD_FILL_EXPERTS_PER_STEP),
        grid=(N_EXPERTS // PAD_FILL_EXPERTS_PER_STEP,),
        in_specs=[pl.BlockSpec(memory_space=pltpu.SMEM),
                  pl.BlockSpec(memory_space=pl.ANY)],
        out_specs=pl.BlockSpec(memory_space=pl.ANY),
        out_shape=jax.ShapeDtypeStruct(xs.shape, xs.dtype),
        input_output_aliases={1: 0},
        scratch_shapes=[pltpu.VMEM((MOE_BLOCK, half), PACKED), pltpu.SemaphoreType.DMA],
        compiler_params=_cparams("arbitrary"),
        name="pad_fill",
    )(meta, xs)


def _experts_kernel(items_ref, xs_hbm, wg_hbm, wu_hbm, wd_hbm, ys_hbm,
                    xbuf, ybuf, wg_f, wu_f, wd_f, wg_s, wu_s, wd_s, sem_x, sem_y, sem_w):
    n_items = items_ref[3, 0]
    rows = ITEM_BLOCKS * MOE_BLOCK

    def w_copies(e, s):
        return [pltpu.make_async_copy(src.at[e], dst.at[s], sem_w.at[s])
                for src, dst in ((wg_hbm, wg_f), (wu_hbm, wu_f), (wd_hbm, wd_f))]

    def x_copy(item, s):
        row0 = pl.multiple_of(items_ref[1, item] * MOE_BLOCK, MOE_BLOCK)
        return pltpu.make_async_copy(xs_hbm.at[pl.ds(row0, rows)], xbuf.at[s], sem_x.at[s])

    def y_copies(item, s, fn):
        for k in range(ITEM_BLOCKS):
            row0 = pl.multiple_of((items_ref[1, item] + k) * MOE_BLOCK, MOE_BLOCK)
            cp = pltpu.make_async_copy(ybuf.at[s, pl.ds(k * MOE_BLOCK, MOE_BLOCK)],
                                       ys_hbm.at[pl.ds(row0, MOE_BLOCK)], sem_y.at[s])
            pl.when(k < items_ref[2, item])(functools.partial(fn, cp))

    def expert_of(item):
        return items_ref[0, jnp.minimum(item, n_items - 1)]

    def changes_at(item):
        return ((item < n_items) & (expert_of(item) != expert_of(item - 1))).astype(I32)

    @pl.when(n_items > 0)
    def _():
        x_copy(0, 0).start()
        for cp in w_copies(expert_of(0), 0):
            cp.start()

        @pl.when(changes_at(1) == 1)
        def _():
            for cp in w_copies(expert_of(1), 1):
                cp.start()

    def item_body(i, ordinal):
        slot = i % 2
        prev = jnp.maximum(i - 1, 0)
        e = expert_of(i)
        new_expert = (i == 0) | (e != expert_of(prev))
        c1 = changes_at(i + 1)
        c2 = changes_at(i + 2)

        @pl.when(i + 1 < n_items)
        def _():
            x_copy(i + 1, 1 - slot).start()

        @pl.when(new_expert)
        def _():
            wslot = ordinal % WEIGHT_SLOTS
            for cp in w_copies(e, wslot):
                cp.wait()
            wg_s[...] = wg_f[wslot].astype(BF16)
            wu_s[...] = wu_f[wslot].astype(BF16)
            wd_s[...] = wd_f[wslot].astype(BF16)

        @pl.when(c2 == 1)
        def _():
            for cp in w_copies(expert_of(i + 2), (ordinal + c1 + 1) % WEIGHT_SLOTS):
                cp.start()

        x_copy(i, slot).wait()
        xa, xb = _unpack_halves(xbuf[slot])
        xa = xa.astype(BF16)
        xb = xb.astype(BF16)
        half = xa.shape[1]
        g = _dot(xa, wg_s[0:half]) + _dot(xb, wg_s[half:])
        u = _dot(xa, wu_s[0:half]) + _dot(xb, wu_s[half:])
        y = _dot((_silu(g) * u).astype(BF16), wd_s[...])
        ybuf[slot] = _pack_halves(y[:, :half], y[:, half:])
        y_copies(i, slot, lambda cp: cp.start())

        @pl.when(i > 0)
        def _():
            y_copies(prev, 1 - slot, lambda cp: cp.wait())

        return ordinal + c1

    lax.fori_loop(0, n_items, item_body, jnp.int32(0))

    @pl.when(n_items > 0)
    def _():
        last = n_items - 1
        y_copies(last, last % 2, lambda cp: cp.wait())


def _experts(items, xs, w_gate, w_up, w_down, *, n_blocks):
    half = xs.shape[1]
    e, d, f = w_gate.shape
    rows = ITEM_BLOCKS * MOE_BLOCK
    any_spec = pl.BlockSpec(memory_space=pl.ANY)
    return pl.pallas_call(
        _experts_kernel,
        grid_spec=pltpu.PrefetchScalarGridSpec(
            num_scalar_prefetch=1,
            grid=(1,),
            in_specs=[any_spec, any_spec, any_spec, any_spec],
            out_specs=any_spec,
            scratch_shapes=[pltpu.VMEM((2, rows, half), PACKED), pltpu.VMEM((2, rows, half), PACKED),
                            pltpu.VMEM((WEIGHT_SLOTS, d, f), F32), pltpu.VMEM((WEIGHT_SLOTS, d, f), F32),
                            pltpu.VMEM((WEIGHT_SLOTS, f, d), F32),
                            pltpu.VMEM((d, f), BF16), pltpu.VMEM((d, f), BF16), pltpu.VMEM((f, d), BF16),
                            pltpu.SemaphoreType.DMA((2,)), pltpu.SemaphoreType.DMA((2,)),
                            pltpu.SemaphoreType.DMA((WEIGHT_SLOTS,))]),
        out_shape=jax.ShapeDtypeStruct((n_blocks * MOE_BLOCK, half), PACKED),
        compiler_params=_cparams("arbitrary"),
        name="experts",
    )(items, xs, w_gate, w_up, w_down)


def _sc_gather_rows(table, idx):
    n_idx = idx.shape[0]
    width = table.shape[1]
    info = plsc.get_sparse_core_info()
    n_workers = info.num_cores * info.num_subcores
    per_worker = n_idx // n_workers
    assert per_worker * n_workers == n_idx and per_worker % (SC_GATHER_BUFS * SC_WINDOW) == 0
    mesh = plsc.VectorSubcoreMesh(core_axis_name="c", subcore_axis_name="s")

    def body(table_hbm, idx_hbm, out_hbm, idx_v, rows_v, sem_g, sem_o):
        wid = lax.axis_index("s") * info.num_cores + lax.axis_index("c")
        base = wid * per_worker

        @pl.loop(0, per_worker // (SC_GATHER_BUFS * SC_WINDOW))
        def _(it):
            offs = [base + (it * SC_GATHER_BUFS + b) * SC_WINDOW for b in range(SC_GATHER_BUFS)]
            gathers = []
            for b, off in enumerate(offs):
                pltpu.sync_copy(idx_hbm.at[pl.ds(off, SC_WINDOW)], idx_v.at[b])
                gathers.append(pltpu.async_copy(table_hbm.at[idx_v.at[b]], rows_v.at[b], sem_g.at[b]))
            writes = []
            for b, off in enumerate(offs):
                gathers[b].wait()
                writes.append(pltpu.async_copy(rows_v.at[b], out_hbm.at[pl.ds(off, SC_WINDOW)], sem_o.at[b]))
            for cp in writes:
                cp.wait()

    return pl.kernel(
        body,
        out_type=jax.ShapeDtypeStruct((n_idx, width), table.dtype),
        mesh=mesh,
        scratch_types=[pltpu.VMEM((SC_GATHER_BUFS, SC_WINDOW), I32),
                       pltpu.VMEM((SC_GATHER_BUFS, SC_WINDOW, width), table.dtype),
                       pltpu.SemaphoreType.DMA((SC_GATHER_BUFS,)), pltpu.SemaphoreType.DMA((SC_GATHER_BUFS,))],
        name="sc_gather",
    )(table, idx)


def _combine_kernel(hp_ref, x1_ref, g2_ref, w_ref, sgw_ref, suw_ref, sdw_ref, yg_ref, o_ref):
    xa, xb = _unpack_halves(hp_ref[...])
    xa = xa.astype(BF16)
    xb = xb.astype(BF16)
    half = xa.shape[1]
    tb = xa.shape[0]
    g = _dot(xa, sgw_ref[0:half]) + _dot(xb, sgw_ref[half:])
    u = _dot(xa, suw_ref[0:half]) + _dot(xb, suw_ref[half:])
    shared = _dot((_silu(g) * u).astype(BF16), sdw_ref[...])

    w = w_ref[...]
    acc_a = jnp.zeros((tb, half), F32)
    acc_b = jnp.zeros((tb, half), F32)
    for k in range(TOP_K):
        ya, yb = _unpack_halves(yg_ref[0, k])
        acc_a += ya * w[:, k:k + 1]
        acc_b += yb * w[:, k:k + 1]
    g2 = g2_ref[0]
    o_ref[:, 0:half] = x1_ref[:, 0:half] + g2[:, 0:half] * (acc_a + shared[:, 0:half])
    o_ref[:, half:] = x1_ref[:, half:] + g2[:, half:] * (acc_b + shared[:, half:])


def _combine(hp, x1, g2, w_tok, sgw, suw, sdw, yg, *, tb, seq_len):
    t, half = hp.shape
    d = 2 * half
    tb = min(tb, seq_len)
    per_seq = seq_len // tb
    f = sgw.shape[1]
    return pl.pallas_call(
        _combine_kernel,
        grid=(t // tb,),
        in_specs=[pl.BlockSpec((tb, half), lambda i: (i, 0)),
                  pl.BlockSpec((tb, d), lambda i: (i, 0)),
                  pl.BlockSpec((1, 1, d), lambda i: (i // per_seq, 0, 0)),
                  pl.BlockSpec((tb, TOP_K), lambda i: (i, 0)),
                  pl.BlockSpec((d, f), lambda i: (0, 0)),
                  pl.BlockSpec((d, f), lambda i: (0, 0)),
                  pl.BlockSpec((f, d), lambda i: (0, 0)),
                  pl.BlockSpec((1, TOP_K, tb, half), lambda i: (i, 0, 0, 0))],
        out_specs=pl.BlockSpec((tb, d), lambda i: (i, 0)),
        out_shape=jax.ShapeDtypeStruct((t, d), F32),
        compiler_params=_cparams("arbitrary"),
        name="combine",
    )(hp, x1, g2, w_tok, sgw, suw, sdw, yg)


def _rope_tables(l):
    rows = l // GRID_W
    r = jnp.repeat(jnp.arange(rows), GRID_W).astype(F32)
    col = jnp.tile(jnp.arange(GRID_W), rows).astype(F32)
    n_f = HEAD_DIM // 4
    freqs = ROPE_THETA ** (-jnp.arange(n_f, dtype=F32) / n_f)
    ang = jnp.concatenate([r[:, None] * freqs, col[:, None] * freqs], axis=-1)
    ang = jnp.tile(jnp.repeat(ang, 2, axis=1), (1, LANES // HEAD_DIM))
    sign = jnp.where(jnp.arange(LANES) % 2 == 0, -1.0, 1.0).astype(F32)
    return jnp.cos(ang), jnp.sin(ang) * sign


def kernel(x, c, ctx, c_ctx, w_mod, b_mod, norm1_w, norm2_w, w_in, q_norm_w, k_norm_w, ret_decay_fwd,
           ret_decay_bwd, w_out, router_w, router_bias, exp_w_gate, exp_w_up, exp_w_down, sh_w_gate,
           sh_w_up, sh_w_down):
    b, l, d = x.shape
    lc = ctx.shape[1]
    t = b * l
    assert w_mod.shape[0] == 1, "single layer"
    assert l % CHUNK == 0 and lc % CHUNK == 0 and l % GRID_W == 0

    rows = (b + 1 + 7) // 8 * 8
    cc = jnp.zeros((rows, d), F32).at[:b].set(c).at[b].set(c_ctx)
    mod = _modulation(cc, w_mod[0], b_mod[0])
    sh1, sc1, g1, sh2, sc2, g2 = [mod[:b, i * d:(i + 1) * d].reshape(b, 1, d) for i in range(6)]
    shc = mod[b, 0:d].reshape(1, 1, d)
    scc = mod[b, d:2 * d].reshape(1, 1, d)

    wi = w_in[0].astype(BF16)
    qnw = jnp.tile(q_norm_w[0], LANES // HEAD_DIM).reshape(1, LANES)
    knw = jnp.tile(k_norm_w[0], LANES // HEAD_DIM).reshape(1, LANES)
    cos, sin = _rope_tables(l)
    n1 = norm1_w[0].reshape(1, d)

    cklo, ckhi, cvlo, cvhi, crk, crv = _projection(
        ctx, shc, scc, n1, wi, qnw, knw, cos[:lc], sin[:lc], rope=False, with_q=False, tm=TILE_PROJ)
    klo, khi, vlo, vhi, rk, rv, q, rq, sg = _projection(
        x, sh1, sc1, n1, wi, qnw, knw, cos, sin, rope=True, with_q=True, tm=TILE_PROJ)

    attn = _attention(q, klo, khi, vlo, vhi, cklo, ckhi, cvlo, cvhi, tq=TILE_ATTN_Q)
    dec_f = jnp.repeat(ret_decay_fwd[0].astype(F32), HEAD_DIM).reshape(1, RET_W)
    dec_b = jnp.repeat(ret_decay_bwd[0].astype(F32), HEAD_DIM).reshape(1, RET_W)
    ret = _retention(rq, rk, rv, sg, crk, crv, dec_f, dec_b)

    wo = w_out[0].astype(BF16)
    r_hi, r_lo = _split(router_w[0].T)
    x1, hp, logits_t = _out_projection(attn, ret, x, wo[:ATTN_W], wo[ATTN_W:], g1, sh2, sc2,
                                       norm2_w[0].reshape(1, d), r_hi, r_lo, tm=TILE_OUT)

    idx_t, w_t, rank_t, cnt_col, cnt_row = _route(logits_t, router_bias[0].reshape(N_EXPERTS, 1), tb=TILE_TOKENS)
    n_blocks = -(-(t * TOP_K) // MOE_BLOCK) + N_EXPERTS
    tb = TILE_TOKENS
    dest, meta, items = _destinations(idx_t, rank_t, cnt_col, cnt_row, tb=tb, n_blocks=n_blocks)
    dest_flat = dest.reshape(-1)
    steps, _, tbe = dest.shape
    dest_win = dest.reshape(steps, TOP_K, tbe // SC_WINDOW, SC_WINDOW).transpose(0, 2, 1, 3)
    dest_win = dest_win.reshape(t // SC_WINDOW, TOP_K, SC_WINDOW)
    xs = _sc_scatter_rows(hp, dest_win, (n_blocks + ITEM_BLOCKS - 1) * MOE_BLOCK)
    xs = _pad_fill(meta, xs)
    ys = _experts(items, xs, exp_w_gate[0], exp_w_up[0], exp_w_down[0], n_blocks=n_blocks)
    yg = _sc_gather_rows(ys, dest_flat).reshape(dest.shape + (d // 2,))
    out = _combine(hp, x1.reshape(t, d), g2, w_t.T, sh_w_gate[0].astype(BF16), sh_w_up[0].astype(BF16),
                   sh_w_down[0].astype(BF16), yg, tb=tb, seq_len=l)
    return out.reshape(b, l, d)
```

```python
import functools

import jax
import jax.numpy as jnp
from jax import lax
from jax.experimental import pallas as pl
from jax.experimental.pallas import tpu as pltpu
from jax.experimental.pallas import tpu_sc as plsc

F32 = jnp.float32
BF16 = jnp.bfloat16
I32 = jnp.int32
U32 = jnp.uint32
PACKED = jnp.int32

HEAD_DIM = 64
LANES = 128
SUBLANES = 8
ATTN_HEADS = 8
ATTN_KV_HEADS = 2
GQA = ATTN_HEADS // ATTN_KV_HEADS
RET_HEADS = 8
ATTN_W = ATTN_HEADS * HEAD_DIM
KV_W = ATTN_KV_HEADS * HEAD_DIM
RET_W = RET_HEADS * HEAD_DIM
RET_PAIRS = RET_W // LANES
CHUNK = 128
GRID_W = 64
ROPE_THETA = 10000.0
N_EXPERTS = 256
TOP_K = 8
N_GROUPS = 8
GROUP_SIZE = N_EXPERTS // N_GROUPS
TOPK_GROUPS = 4
ROUTED_SCALE = 2.5
MOE_BLOCK = 128
EPS = 1e-6
QK_SCALE = HEAD_DIM ** -0.5
LOG2_E = 1.4426950408889634

OFF_AK = 0
OFF_AV = OFF_AK + KV_W
OFF_RK = OFF_AV + KV_W
OFF_RV = OFF_RK + RET_W
CTX_KV_COLS = OFF_RV + RET_W
OFF_AQ = CTX_KV_COLS
OFF_RQ = OFF_AQ + ATTN_W
OFF_RG = OFF_RQ + RET_W
IN_COLS = OFF_RG + RET_W

VMEM_LIMIT = 52 * 1024 * 1024

TILE_PROJ = 512
TILE_ATTN_Q = 256
TILE_OUT = 512
TILE_TOKENS = 256
ITEM_BLOCKS = 5
COMBINE_PARTS = 4
SC_GATHER_BUFS = 2
WEIGHT_SLOTS = 3
SC_WINDOW = 64
HI_MASK = 0xFFFF0000


def _cparams(*sem):
    return pltpu.CompilerParams(dimension_semantics=sem, vmem_limit_bytes=VMEM_LIMIT)


def _split(a):
    hi = a.astype(BF16)
    lo = (a - hi.astype(F32)).astype(BF16)
    return hi, lo


def _dot(a, b):
    return jnp.dot(a, b, preferred_element_type=F32)


def _dot_nt(a, b):
    return lax.dot_general(a, b, (((1,), (1,)), ((), ())), preferred_element_type=F32)


def _sigmoid(v):
    return 1.0 / (1.0 + jnp.exp(-v))


def _silu(v):
    return v * _sigmoid(v)


def _pack_halves(a, b):
    ua = lax.bitcast_convert_type(a.astype(BF16).astype(F32), U32)
    ub = lax.bitcast_convert_type(b.astype(BF16).astype(F32), U32)
    return lax.bitcast_convert_type((ua & jnp.uint32(HI_MASK)) | (ub >> 16), PACKED)


def _unpack_halves(p):
    u = lax.bitcast_convert_type(p, U32)
    a = lax.bitcast_convert_type(u & jnp.uint32(HI_MASK), F32)
    b = lax.bitcast_convert_type(u << 16, F32)
    return a, b


def _mod_kernel(c_ref, w_ref, b_ref, o_ref):
    s_hi, s_lo = _split(_silu(c_ref[...]))
    w_hi, w_lo = _split(w_ref[...])
    o_ref[...] = _dot(s_hi, w_hi) + _dot(s_hi, w_lo) + _dot(s_lo, w_hi) + b_ref[...]


def _modulation(cc, w_mod, b_mod):
    rows, d = cc.shape
    n = w_mod.shape[1]
    tn = 768
    return pl.pallas_call(
        _mod_kernel,
        grid=(n // tn,),
        in_specs=[pl.BlockSpec((rows, d), lambda j: (0, 0)),
                  pl.BlockSpec((d, tn), lambda j: (0, j)),
                  pl.BlockSpec((1, tn), lambda j: (0, j))],
        out_specs=pl.BlockSpec((rows, tn), lambda j: (0, j)),
        out_shape=jax.ShapeDtypeStruct((rows, n), F32),
        compiler_params=_cparams("arbitrary"),
        name="mod",
    )(cc, w_mod, b_mod.reshape(1, n))


def _segment_ones():
    r = lax.broadcasted_iota(I32, (LANES, LANES), 0) // HEAD_DIM
    c = lax.broadcasted_iota(I32, (LANES, LANES), 1) // HEAD_DIM
    return jnp.where(r == c, 1.0, 0.0).astype(BF16)


def _head_mean_sq(v, seg):
    hi, lo = _split(v * v)
    return (_dot(hi, seg) + _dot(lo, seg)) * (1.0 / HEAD_DIM)


def _proj_kernel(x_ref, sh_ref, sc_ref, nw_ref, wi_ref, qnw_ref, knw_ref, cos_ref, sin_ref,
                 *out_refs, rope, with_q):
    if with_q:
        klo_ref, khi_ref, vlo_ref, vhi_ref, rk_ref, rv_ref, q_ref, rq_ref, sg_ref = out_refs
    else:
        klo_ref, khi_ref, vlo_ref, vhi_ref, rk_ref, rv_ref = out_refs
    x = x_ref[0]
    h = x * lax.rsqrt(jnp.mean(x * x, axis=-1, keepdims=True) + EPS) * nw_ref[...]
    h = h * (1.0 + sc_ref[0]) + sh_ref[0]
    z = _dot(h.astype(BF16), wi_ref[...])

    seg = _segment_ones()
    lane = lax.broadcasted_iota(I32, (x.shape[0], LANES), 1)
    low_half = lane < HEAD_DIM
    even = (lane & 1) == 0

    def norm_rope(v, w128):
        v = v * lax.rsqrt(_head_mean_sq(v, seg) + EPS) * w128
        if rope:
            swapped = jnp.where(even, pltpu.roll(v, LANES - 1, 1), pltpu.roll(v, 1, 1))
            v = v * cos_ref[...] + swapped * sin_ref[...]
        return v

    def write_lo_hi(v, lo_ref, hi_ref, transposed):
        sw = pltpu.roll(v, HEAD_DIM, 1)
        fix = (lambda a: jnp.transpose(a).astype(BF16)) if transposed else (lambda a: a.astype(BF16))
        lo_ref[0, 0] = fix(jnp.where(low_half, v, 0.0))
        hi_ref[0, 0] = fix(jnp.where(low_half, 0.0, sw))
        lo_ref[0, 1] = fix(jnp.where(low_half, sw, 0.0))
        hi_ref[0, 1] = fix(jnp.where(low_half, 0.0, v))

    write_lo_hi(norm_rope(z[:, OFF_AK:OFF_AK + KV_W], knw_ref[...]), klo_ref, khi_ref, True)
    write_lo_hi(z[:, OFF_AV:OFF_AV + KV_W], vlo_ref, vhi_ref, False)
    rk_ref[0] = (z[:, OFF_RK:OFF_RK + RET_W] * QK_SCALE).astype(BF16)
    rv_ref[0] = z[:, OFF_RV:OFF_RV + RET_W].astype(BF16)
    if with_q:
        for j in range(ATTN_W // LANES):
            qj = norm_rope(z[:, OFF_AQ + j * LANES:OFF_AQ + (j + 1) * LANES], qnw_ref[...])
            q_ref[0, :, j * LANES:(j + 1) * LANES] = (qj * (QK_SCALE * LOG2_E)).astype(BF16)
        rq_ref[0] = z[:, OFF_RQ:OFF_RQ + RET_W].astype(BF16)
        sg_ref[0] = _silu(z[:, OFF_RG:OFF_RG + RET_W]).astype(BF16)


def _projection(x, shift, scale, norm_w, wi_bf16, qnw, knw, cos, sin, *, rope, with_q, tm):
    b, l, d = x.shape
    tm = min(tm, l)
    ncols = IN_COLS if with_q else CTX_KV_COLS
    per_batch = shift.shape[0] > 1
    mod_idx = (lambda bi, i: (bi, 0, 0)) if per_batch else (lambda bi, i: (0, 0, 0))
    kv_shape = jax.ShapeDtypeStruct((b, ATTN_KV_HEADS, l, LANES), BF16)
    kv_spec = pl.BlockSpec((1, ATTN_KV_HEADS, tm, LANES), lambda bi, i: (bi, 0, i, 0))
    kt_shape = jax.ShapeDtypeStruct((b, ATTN_KV_HEADS, LANES, l), BF16)
    kt_spec = pl.BlockSpec((1, ATTN_KV_HEADS, LANES, tm), lambda bi, i: (bi, 0, 0, i))
    w_shape = jax.ShapeDtypeStruct((b, l, RET_W), BF16)
    w_spec = pl.BlockSpec((1, tm, RET_W), lambda bi, i: (bi, i, 0))
    out_shape = [kt_shape] * 2 + [kv_shape] * 2 + [w_shape] * 2
    out_specs = [kt_spec] * 2 + [kv_spec] * 2 + [w_spec] * 2
    if with_q:
        out_shape += [w_shape] * 3
        out_specs += [w_spec] * 3
    return pl.pallas_call(
        functools.partial(_proj_kernel, rope=rope, with_q=with_q),
        grid=(b, l // tm),
        in_specs=[pl.BlockSpec((1, tm, d), lambda bi, i: (bi, i, 0)),
                  pl.BlockSpec((1, 1, d), mod_idx),
                  pl.BlockSpec((1, 1, d), mod_idx),
                  pl.BlockSpec((1, d), lambda bi, i: (0, 0)),
                  pl.BlockSpec((d, ncols), lambda bi, i: (0, 0)),
                  pl.BlockSpec((1, LANES), lambda bi, i: (0, 0)),
                  pl.BlockSpec((1, LANES), lambda bi, i: (0, 0)),
                  pl.BlockSpec((tm, LANES), lambda bi, i: (i, 0)),
                  pl.BlockSpec((tm, LANES), lambda bi, i: (i, 0))],
        out_specs=out_specs,
        out_shape=out_shape,
        compiler_params=_cparams("arbitrary", "arbitrary"),
        name="proj_latent" if with_q else "proj_ctx",
    )(x, shift, scale, norm_w, wi_bf16, qnw, knw, cos, sin)


def _attn_kernel(q_ref, klo_ref, khi_ref, vlo_ref, vhi_ref, cklo_ref, ckhi_ref, cvlo_ref, cvhi_ref,
                 o_ref, kl_s, kh_s, va_s, *, l, lc):
    lk = l + lc

    @pl.when(pl.program_id(2) == 0)
    def _():
        kl_s[:, 0:l] = klo_ref[0, 0]
        kl_s[:, l:lk] = cklo_ref[0, 0]
        kh_s[:, 0:l] = khi_ref[0, 0]
        kh_s[:, l:lk] = ckhi_ref[0, 0]
        lane = lax.broadcasted_iota(I32, (lk, LANES), 1)
        ones_lo = jnp.where(lane < HEAD_DIM, 1.0, 0.0).astype(BF16)
        ones_hi = jnp.where(lane < HEAD_DIM, 0.0, 1.0).astype(BF16)
        for g, (v_ref, cv_ref, ones) in enumerate(((vlo_ref, cvlo_ref, ones_lo), (vhi_ref, cvhi_ref, ones_hi),
                                                   (vlo_ref, cvlo_ref, ones_lo), (vhi_ref, cvhi_ref, ones_hi))):
            v_col, one_col = (0, LANES) if g < 2 else (LANES, 0)
            va_s[g, 0:l, v_col:v_col + LANES] = v_ref[0, 0]
            va_s[g, l:lk, v_col:v_col + LANES] = cv_ref[0, 0]
            va_s[g, :, one_col:one_col + LANES] = ones

    q = q_ref[0]
    acc = []
    for g in range(GQA):
        qp = q[:, (g // 2) * LANES:(g // 2 + 1) * LANES]
        kt = kl_s[...] if g % 2 == 0 else kh_s[...]
        s = _dot(qp, kt)
        p = jnp.exp2(s - jnp.max(s, axis=-1, keepdims=True)).astype(BF16)
        acc.append(_dot(p, va_s[g]))
    out_a = acc[0] + acc[1]
    out_b = acc[2] + acc[3]
    o_ref[0, :, 0:LANES] = (out_a[:, 0:LANES] / out_a[:, LANES:2 * LANES]).astype(BF16)
    o_ref[0, :, LANES:2 * LANES] = (out_b[:, LANES:2 * LANES] / out_b[:, 0:LANES]).astype(BF16)


def _attention(q, klo, khi, vlo, vhi, cklo, ckhi, cvlo, cvhi, *, tq):
    b, l, _ = q.shape
    lc = cvlo.shape[2]
    lk = l + lc
    tq = min(tq, l)
    gw = GQA * HEAD_DIM
    kt_spec = pl.BlockSpec((1, 1, LANES, l), lambda bi, h, i: (bi, h, 0, 0))
    kv_spec = pl.BlockSpec((1, 1, l, LANES), lambda bi, h, i: (bi, h, 0, 0))
    ckt_spec = pl.BlockSpec((1, 1, LANES, lc), lambda bi, h, i: (bi, h, 0, 0))
    ckv_spec = pl.BlockSpec((1, 1, lc, LANES), lambda bi, h, i: (bi, h, 0, 0))
    return pl.pallas_call(
        functools.partial(_attn_kernel, l=l, lc=lc),
        grid=(b, ATTN_KV_HEADS, l // tq),
        in_specs=([pl.BlockSpec((1, tq, gw), lambda bi, h, i: (bi, i, h))] + [kt_spec] * 2 + [kv_spec] * 2
                  + [ckt_spec] * 2 + [ckv_spec] * 2),
        out_specs=pl.BlockSpec((1, tq, gw), lambda bi, h, i: (bi, i, h)),
        out_shape=jax.ShapeDtypeStruct((b, l, ATTN_W), BF16),
        scratch_shapes=[pltpu.VMEM((LANES, lk), BF16), pltpu.VMEM((LANES, lk), BF16),
                        pltpu.VMEM((GQA, lk, 2 * LANES), BF16)],
        compiler_params=_cparams("arbitrary", "arbitrary", "arbitrary"),
        name="attn",
    )(q, klo, khi, vlo, vhi, cklo, ckhi, cvlo, cvhi)


def _log_sigmoid(v):
    return jnp.minimum(v, 0.0) - jnp.log(1.0 + jnp.exp(-jnp.abs(v)))


def _ret_kernel(rq_ref, rk_ref, rv_ref, sg_ref, crk_ref, crv_ref, df_ref, db_ref, o_ref,
                m_s, xi_s, zeta_s, kv_s, st_s, *, l, lc):
    n = l // CHUNK
    nc = lc // CHUNK
    lgf = _log_sigmoid(df_ref[...])
    lgb = _log_sigmoid(db_ref[...])
    pos = lax.broadcasted_iota(I32, (CHUNK, LANES), 0).astype(F32)
    row = lax.broadcasted_iota(I32, (CHUNK, CHUNK), 0)
    col = lax.broadcasted_iota(I32, (CHUNK, CHUNK), 1)
    diff = (row - col).astype(F32)
    g_chunk = []
    for p in range(RET_PAIRS):
        cols = slice(p * LANES, (p + 1) * LANES)
        lf, lb = lgf[:, cols], lgb[:, cols]
        xi_s[p, :, 0:LANES] = jnp.exp((pos + 1.0) * lf)
        xi_s[p, :, LANES:] = jnp.exp((CHUNK - pos) * lb)
        zeta_s[p, :, 0:LANES] = jnp.exp((CHUNK - 1.0 - pos) * lf)
        zeta_s[p, :, LANES:] = jnp.exp(pos * lb)
        g_chunk.append((jnp.exp(CHUNK * lf), jnp.exp(CHUNK * lb)))
        for j in range(2):
            h = 2 * p + j
            hf = lgf[:, h * HEAD_DIM:h * HEAD_DIM + 1]
            hb = lgb[:, h * HEAD_DIM:h * HEAD_DIM + 1]
            m_s[p, :, j * CHUNK:(j + 1) * CHUNK] = jnp.where(
                diff > 0, jnp.exp(diff * hf), jnp.where(diff < 0, jnp.exp(-diff * hb), 2.0))

    lane = lax.broadcasted_iota(I32, (CHUNK, LANES), 1)
    low_half = lane < HEAD_DIM
    diag = (lax.broadcasted_iota(I32, (LANES, LANES), 0) // HEAD_DIM
            == lax.broadcasted_iota(I32, (LANES, LANES), 1) // HEAD_DIM)
    seg = jnp.where(diag, 1.0, 0.0).astype(BF16)
    seg2 = jnp.concatenate([seg, seg], axis=0)
    diag2 = jnp.concatenate([diag, diag], axis=0)

    def split_heads(a):
        zero = jnp.zeros_like(a)
        return jnp.concatenate([jnp.where(low_half, a, zero), jnp.where(low_half, zero, a)], axis=0)

    def contrib(k_ref, v_ref, r0, p):
        cols = slice(p * LANES, (p + 1) * LANES)
        kp = k_ref[0, pl.ds(r0, CHUNK), cols].astype(F32)
        kz = jnp.concatenate([kp, kp], axis=1) * zeta_s[p]
        kv = _dot(jnp.transpose(kz).astype(BF16), v_ref[0, pl.ds(r0, CHUNK), cols])
        return jnp.where(diag2, kv, 0.0)

    for c in range(nc):
        for p in range(RET_PAIRS):
            kv_s[c, p] = contrib(crk_ref, crv_ref, c * CHUNK, p)

    def contrib_body(c, carry):
        r0 = pl.multiple_of(c * CHUNK, CHUNK)
        for p in range(RET_PAIRS):
            kv_s[nc + c, p] = contrib(rk_ref, rv_ref, r0, p)
        return carry

    lax.fori_loop(0, n, contrib_body, 0)

    for p in range(RET_PAIRS):
        gf, gb = g_chunk[p]
        sf = jnp.zeros((LANES, LANES), F32)
        sb = jnp.zeros((LANES, LANES), F32)
        for c in range(nc):
            sf = gf * sf + kv_s[c, p, 0:LANES]
            sb = gb * sb + kv_s[nc - 1 - c, p, LANES:]

        def fwd_scan(c, s, p=p, gf=gf):
            st_s[c, p, 0:LANES] = s.astype(BF16)
            return gf * s + kv_s[nc + c, p, 0:LANES]

        def bwd_scan(j, s, p=p, gb=gb):
            c = n - 1 - j
            st_s[c, p, LANES:] = s.astype(BF16)
            return gb * s + kv_s[nc + c, p, LANES:]

        lax.fori_loop(0, n, fwd_scan, sf)
        lax.fori_loop(0, n, bwd_scan, sb)

    def out_body(c, carry):
        r0 = pl.multiple_of(c * CHUNK, CHUNK)
        for p in range(RET_PAIRS):
            cols = slice(p * LANES, (p + 1) * LANES)
            qp = rq_ref[0, pl.ds(r0, CHUNK), cols]
            kp = rk_ref[0, pl.ds(r0, CHUNK), cols]
            vp = rv_ref[0, pl.ds(r0, CHUNK), cols]
            s2 = _dot_nt(qp, split_heads(kp))
            a2 = (s2 * m_s[p]).astype(BF16)
            y = _dot(a2, split_heads(vp))
            qf = qp.astype(F32)
            qx = (jnp.concatenate([qf, qf], axis=1) * xi_s[p]).astype(BF16)
            y += _dot(qx, st_s[c, p])
            hi, lo = _split(y * y)
            ms = _dot(jnp.concatenate([hi, lo], axis=1), seg2) * (1.0 / HEAD_DIM)
            out = y * lax.rsqrt(ms + EPS) * sg_ref[0, pl.ds(r0, CHUNK), cols].astype(F32)
            o_ref[0, pl.ds(r0, CHUNK), cols] = out.astype(BF16)
        return carry

    lax.fori_loop(0, n, out_body, 0)


def _retention(rq, rk, rv, sg, crk, crv, dec_f, dec_b):
    b, l, _ = rq.shape
    lc = crk.shape[1]
    n = l // CHUNK
    nc = lc // CHUNK
    spec = pl.BlockSpec((1, l, RET_W), lambda bi: (bi, 0, 0))
    cspec = pl.BlockSpec((1, lc, RET_W), lambda bi: (bi, 0, 0))
    dspec = pl.BlockSpec((1, RET_W), lambda bi: (0, 0))
    return pl.pallas_call(
        functools.partial(_ret_kernel, l=l, lc=lc),
        grid=(b,),
        in_specs=[spec, spec, spec, spec, cspec, cspec, dspec, dspec],
        out_specs=spec,
        out_shape=jax.ShapeDtypeStruct((b, l, RET_W), BF16),
        scratch_shapes=[pltpu.VMEM((RET_PAIRS, CHUNK, 2 * CHUNK), F32),
                        pltpu.VMEM((RET_PAIRS, CHUNK, 2 * LANES), F32),
                        pltpu.VMEM((RET_PAIRS, CHUNK, 2 * LANES), F32),
                        pltpu.VMEM((nc + n, RET_PAIRS, 2 * LANES, LANES), F32),
                        pltpu.VMEM((n, RET_PAIRS, 2 * LANES, LANES), BF16)],
        compiler_params=_cparams("arbitrary"),
        name="ret",
    )(rq, rk, rv, sg, crk, crv, dec_f, dec_b)


def _out_kernel(attn_ref, ret_ref, x_ref, wa_ref, wr_ref, g1_ref, sh_ref, sc_ref, nw_ref, rhi_ref, rlo_ref,
                x1_ref, hp_ref, lg_ref):
    y = _dot(attn_ref[0], wa_ref[...]) + _dot(ret_ref[0], wr_ref[...])
    x1 = x_ref[0] + g1_ref[0] * y
    x1_ref[0] = x1
    h = x1 * lax.rsqrt(jnp.mean(x1 * x1, axis=-1, keepdims=True) + EPS) * nw_ref[...]
    h = h * (1.0 + sc_ref[0]) + sh_ref[0]
    half = h.shape[1] // 2
    hp_ref[...] = _pack_halves(h[:, :half], h[:, half:])
    h_hi, h_lo = _split(h)
    lg_ref[...] = _dot_nt(rhi_ref[...], h_hi) + _dot_nt(rhi_ref[...], h_lo) + _dot_nt(rlo_ref[...], h_hi)


def _out_projection(attn, ret, x, wa, wr, g1, sh2, sc2, norm_w, r_hi, r_lo, *, tm):
    b, l, d = x.shape
    tm = min(tm, l)
    nt = l // tm
    t = b * l
    mspec = pl.BlockSpec((1, 1, d), lambda bi, i: (bi, 0, 0))
    return pl.pallas_call(
        _out_kernel,
        grid=(b, nt),
        in_specs=[pl.BlockSpec((1, tm, ATTN_W), lambda bi, i: (bi, i, 0)),
                  pl.BlockSpec((1, tm, RET_W), lambda bi, i: (bi, i, 0)),
                  pl.BlockSpec((1, tm, d), lambda bi, i: (bi, i, 0)),
                  pl.BlockSpec((ATTN_W, d), lambda bi, i: (0, 0)),
                  pl.BlockSpec((RET_W, d), lambda bi, i: (0, 0)),
                  mspec, mspec, mspec,
                  pl.BlockSpec((1, d), lambda bi, i: (0, 0)),
                  pl.BlockSpec((N_EXPERTS, d), lambda bi, i: (0, 0)),
                  pl.BlockSpec((N_EXPERTS, d), lambda bi, i: (0, 0))],
        out_specs=[pl.BlockSpec((1, tm, d), lambda bi, i: (bi, i, 0)),
                   pl.BlockSpec((tm, d // 2), lambda bi, i: (bi * nt + i, 0)),
                   pl.BlockSpec((N_EXPERTS, tm), lambda bi, i: (0, bi * nt + i))],
        out_shape=[jax.ShapeDtypeStruct((b, l, d), F32),
                   jax.ShapeDtypeStruct((t, d // 2), PACKED),
                   jax.ShapeDtypeStruct((N_EXPERTS, t), F32)],
        compiler_params=_cparams("arbitrary", "arbitrary"),
        name="out_proj",
    )(attn, ret, x, wa, wr, g1, sh2, sc2, norm_w, r_hi, r_lo)


def _route_kernel(lg_ref, bias_ref, idx_ref, w_ref, rank_ref, cnt_col_ref, cnt_row_ref, tri_s, col_s, row_s):
    tb = lg_ref.shape[1]
    step = pl.program_id(0)

    @pl.when(step == 0)
    def _():
        r = lax.broadcasted_iota(I32, (tb, tb), 0)
        c = lax.broadcasted_iota(I32, (tb, tb), 1)
        tri_s[...] = jnp.where(r <= c, 1.0, 0.0).astype(BF16)
        col_s[...] = jnp.zeros_like(col_s)
        row_s[...] = jnp.zeros_like(row_s)

    scores = _sigmoid(lg_ref[...])
    biased = scores + bias_ref[...]
    neg = -jnp.inf
    sub = lax.broadcasted_iota(I32, (GROUP_SIZE, tb), 0).astype(F32)

    gscore = []
    for g in range(N_GROUPS):
        blk = biased[g * GROUP_SIZE:(g + 1) * GROUP_SIZE]
        m1 = jnp.max(blk, axis=0, keepdims=True)
        first = jnp.min(jnp.where(blk == m1, sub, float(GROUP_SIZE)), axis=0, keepdims=True)
        m2 = jnp.max(jnp.where(sub == first, neg, blk), axis=0, keepdims=True)
        gscore.append(m1 + m2)
    gs = jnp.concatenate(gscore, axis=0)
    gsub = lax.broadcasted_iota(I32, (N_GROUPS, tb), 0).astype(F32)
    keep = jnp.zeros((N_GROUPS, tb), F32)
    for _ in range(TOPK_GROUPS):
        m = jnp.max(gs, axis=0, keepdims=True)
        first = jnp.min(jnp.where(gs == m, gsub, float(N_GROUPS)), axis=0, keepdims=True)
        sel = gsub == first
        keep = jnp.where(sel, 1.0, keep)
        gs = jnp.where(sel, neg, gs)
    masked = jnp.concatenate(
        [jnp.where(keep[g:g + 1] > 0.0, biased[g * GROUP_SIZE:(g + 1) * GROUP_SIZE], neg)
         for g in range(N_GROUPS)], axis=0)

    esub = lax.broadcasted_iota(I32, (N_EXPERTS, tb), 0).astype(F32)
    sels, idxs, ws = [], [], []
    chosen = jnp.zeros((N_EXPERTS, tb), F32)
    for _ in range(TOP_K):
        m = jnp.max(masked, axis=0, keepdims=True)
        first = jnp.min(jnp.where(masked == m, esub, float(N_EXPERTS)), axis=0, keepdims=True)
        sel = esub == first
        sels.append(sel)
        idxs.append(first)
        ws.append(jnp.sum(jnp.where(sel, scores, 0.0), axis=0, keepdims=True))
        chosen = jnp.where(sel, 1.0, chosen)
        masked = jnp.where(sel, neg, masked)
    wsum = ws[0]
    for k in range(1, TOP_K):
        wsum = wsum + ws[k]
    idx_ref[...] = jnp.concatenate(idxs, axis=0).astype(I32)
    w_ref[...] = jnp.concatenate([wk / wsum * ROUTED_SCALE for wk in ws], axis=0)

    chosen_b = chosen.astype(BF16)
    incl = _dot(chosen_b, tri_s[...])
    before = incl - chosen + col_s[...]
    rank_ref[...] = jnp.concatenate(
        [jnp.sum(jnp.where(sel, before, 0.0), axis=0, keepdims=True) for sel in sels], axis=0).astype(I32)
    col_s[...] = col_s[...] + incl[:, tb - 1:tb]
    row_s[...] = row_s[...] + _dot_nt(jnp.ones((8, tb), BF16), chosen_b)
    cnt_col_ref[...] = col_s[...].astype(I32)
    cnt_row_ref[...] = row_s[...].astype(I32)


def _route(logits_t, bias_col, *, tb):
    e, t = logits_t.shape
    tb = min(tb, t)
    kspec = pl.BlockSpec((TOP_K, tb), lambda i: (0, i))
    return pl.pallas_call(
        _route_kernel,
        grid=(t // tb,),
        in_specs=[pl.BlockSpec((e, tb), lambda i: (0, i)),
                  pl.BlockSpec((e, 1), lambda i: (0, 0))],
        out_specs=[kspec, kspec, kspec,
                   pl.BlockSpec((e, 1), lambda i: (0, 0)),
                   pl.BlockSpec((8, e), lambda i: (0, 0))],
        out_shape=[jax.ShapeDtypeStruct((TOP_K, t), I32),
                   jax.ShapeDtypeStruct((TOP_K, t), F32),
                   jax.ShapeDtypeStruct((TOP_K, t), I32),
                   jax.ShapeDtypeStruct((e, 1), I32),
                   jax.ShapeDtypeStruct((8, e), I32)],
        scratch_shapes=[pltpu.VMEM((tb, tb), BF16), pltpu.VMEM((e, 1), F32), pltpu.VMEM((8, e), F32)],
        compiler_params=_cparams("arbitrary"),
        name="route",
    )(logits_t, bias_col)


def _pad_block(cnt):
    return (cnt + (MOE_BLOCK - 1)) // MOE_BLOCK * MOE_BLOCK


def _max_items(n_blocks):
    return n_blocks // ITEM_BLOCKS + N_EXPERTS


def _dest_kernel(idx_ref, rank_ref, cnt_col_ref, cnt_row_ref, dest_ref, meta_ref, items_ref):
    tb = idx_ref.shape[1]
    nip = items_ref.shape[1]
    pad_col = _pad_block(cnt_col_ref[...])
    pad_row = _pad_block(cnt_row_ref[0:1, :])
    er = lax.broadcasted_iota(I32, (N_EXPERTS, N_EXPERTS), 0)
    ec = lax.broadcasted_iota(I32, (N_EXPERTS, N_EXPERTS), 1)
    start_col = jnp.sum(jnp.where(ec < er, pad_row, 0), axis=1, keepdims=True)
    start_row = jnp.sum(jnp.where(er < ec, pad_col, 0), axis=0, keepdims=True)

    esub = lax.broadcasted_iota(I32, (N_EXPERTS, tb), 0)
    rows = []
    for k in range(TOP_K):
        onehot = esub == idx_ref[k:k + 1, :]
        rows.append(jnp.sum(jnp.where(onehot, start_col, 0), axis=0, keepdims=True) + rank_ref[k:k + 1, :])
    dest_ref[0] = jnp.concatenate(rows, axis=0)

    used = jnp.sum(pad_row, axis=1, keepdims=True) // MOE_BLOCK
    meta_ref[...] = jnp.concatenate(
        [cnt_row_ref[0:1, :], start_row, pad_row, jnp.broadcast_to(used, (1, N_EXPERTS)),
         jnp.zeros((4, N_EXPERTS), I32)], axis=0)

    nb_col = pad_col // MOE_BLOCK
    it_col = (nb_col + (ITEM_BLOCKS - 1)) // ITEM_BLOCKS
    it_row = (pad_row // MOE_BLOCK + (ITEM_BLOCKS - 1)) // ITEM_BLOCKS
    it_start = jnp.sum(jnp.where(ec < er, it_row, 0), axis=1, keepdims=True)
    n_items = jnp.sum(it_row, axis=1, keepdims=True)
    lane = lax.broadcasted_iota(I32, (1, nip), 1)
    owner = jnp.sum(jnp.where(it_start + it_col <= lane, 1, 0), axis=0, keepdims=True)
    owner = jnp.minimum(owner, N_EXPERTS - 1)
    onehot = lax.broadcasted_iota(I32, (N_EXPERTS, nip), 0) == owner

    def pick(col):
        return jnp.sum(jnp.where(onehot, col, 0), axis=0, keepdims=True)

    j = lane - pick(it_start)
    block0 = pick(start_col) // MOE_BLOCK + ITEM_BLOCKS * j
    nvalid = jnp.clip(pick(nb_col) - ITEM_BLOCKS * j, 0, ITEM_BLOCKS)
    items_ref[...] = jnp.concatenate(
        [owner, block0, jnp.where(lane < n_items, nvalid, 0), jnp.broadcast_to(n_items, (1, nip)),
         jnp.zeros((4, nip), I32)], axis=0)


def _destinations(idx_t, rank_t, cnt_col, cnt_row, *, tb, n_blocks):
    _, t = idx_t.shape
    tb = min(tb, t)
    nip = (_max_items(n_blocks) + LANES - 1) // LANES * LANES
    kspec = pl.BlockSpec((TOP_K, tb), lambda i: (0, i))
    return pl.pallas_call(
        _dest_kernel,
        grid=(t // tb,),
        in_specs=[kspec, kspec,
                  pl.BlockSpec((N_EXPERTS, 1), lambda i: (0, 0)),
                  pl.BlockSpec((8, N_EXPERTS), lambda i: (0, 0))],
        out_specs=[pl.BlockSpec((1, TOP_K, tb), lambda i: (i, 0, 0)),
                   pl.BlockSpec((8, N_EXPERTS), lambda i: (0, 0)),
                   pl.BlockSpec((8, nip), lambda i: (0, 0))],
        out_shape=[jax.ShapeDtypeStruct((t // tb, TOP_K, tb), I32),
                   jax.ShapeDtypeStruct((8, N_EXPERTS), I32),
                   jax.ShapeDtypeStruct((8, nip), I32)],
        compiler_params=_cparams("arbitrary"),
        name="dest",
    )(idx_t, rank_t, cnt_col, cnt_row)


_PAD_BITS = (64, 32, 16, 8)


def _sc_scatter_rows(rows, dest_win, n_out_rows):
    n_win, n_slots, win = dest_win.shape
    width = rows.shape[1]
    info = plsc.get_sparse_core_info()
    n_workers = info.num_cores * info.num_subcores
    per_worker = n_win // n_workers
    assert per_worker * n_workers == n_win and win <= LANES
    mesh = plsc.VectorSubcoreMesh(core_axis_name="c", subcore_axis_name="s")

    def body(rows_hbm, dest_hbm, out_hbm, idx_v, rows_v, sem):
        wid = lax.axis_index("s") * info.num_cores + lax.axis_index("c")

        @pl.loop(0, per_worker)
        def _(j):
            w = wid * per_worker + j
            pltpu.sync_copy(dest_hbm.at[w], idx_v)
            pltpu.sync_copy(rows_hbm.at[pl.ds(w * win, win)], rows_v)
            copies = [pltpu.async_copy(rows_v, out_hbm.at[idx_v.at[k]], sem) for k in range(n_slots)]
            for cp in copies:
                cp.wait()

    return pl.kernel(
        body,
        out_type=jax.ShapeDtypeStruct((n_out_rows, width), rows.dtype),
        mesh=mesh,
        scratch_types=[pltpu.VMEM((n_slots, win), I32), pltpu.VMEM((win, width), rows.dtype),
                       pltpu.SemaphoreType.DMA],
        name="sc_scatter",
    )(rows, dest_win)


def _pad_fill_kernel(meta_ref, xs_in, xs_hbm, zero_s, sem_z, *, e_per_step):
    del xs_in
    step = pl.program_id(0)
    zero_s[...] = jnp.zeros_like(zero_s)

    def tail_copy(c):
        row0 = pl.multiple_of((meta_ref[3, 0] + c) * MOE_BLOCK, MOE_BLOCK)
        return pltpu.make_async_copy(zero_s, xs_hbm.at[pl.ds(row0, MOE_BLOCK)], sem_z)

    @pl.when(step == 0)
    def _():
        for c in range(ITEM_BLOCKS - 1):
            tail_copy(c).start()
        for c in range(ITEM_BLOCKS - 1):
            tail_copy(c).wait()

    def pad_copies(e):
        cnt = meta_ref[0, e]
        off = meta_ref[1, e] + cnt
        rem = meta_ref[2, e] - cnt
        head = rem & (SUBLANES - 1)
        out = []
        for i in range(SUBLANES - 1):
            out.append((i < head,
                        pltpu.make_async_copy(zero_s.at[pl.ds(0, 1)], xs_hbm.at[pl.ds(off + i, 1)], sem_z)))
        off = off + head
        for bit in _PAD_BITS:
            out.append(((rem & bit) != 0,
                        pltpu.make_async_copy(zero_s.at[pl.ds(0, bit)],
                                              xs_hbm.at[pl.ds(pl.multiple_of(off, SUBLANES), bit)], sem_z)))
            off = off + (rem & bit)
        return out

    for j in range(e_per_step):
        for cond, c in pad_copies(step * e_per_step + j):
            pl.when(cond)(c.start)
    for j in range(e_per_step):
        for cond, c in pad_copies(step * e_per_step + j):
            pl.when(cond)(c.wait)


PAD_FILL_EXPERTS_PER_STEP = 8


def _pad_fill(meta, xs):
    half = xs.shape[1]
    return pl.pallas_call(
        functools.partial(_pad_fill_kernel, e_per_step=PAD_FILL_EXPERTS_PER_STEP),
        grid=(N_EXPERTS // PAD_FILL_EXPERTS_PER_STEP,),
        in_specs=[pl.BlockSpec(memory_space=pltpu.SMEM),
                  pl.BlockSpec(memory_space=pl.ANY)],
        out_specs=pl.BlockSpec(memory_space=pl.ANY),
        out_shape=jax.ShapeDtypeStruct(xs.shape, xs.dtype),
        input_output_aliases={1: 0},
        scratch_shapes=[pltpu.VMEM((MOE_BLOCK, half), PACKED), pltpu.SemaphoreType.DMA],
        compiler_params=_cparams("arbitrary"),
        name="pad_fill",
    )(meta, xs)


def _experts_kernel(items_ref, xs_hbm, wg_hbm, wu_hbm, wd_hbm, ys_hbm,
                    xbuf, ybuf, wg_f, wu_f, wd_f, wg_s, wu_s, wd_s, sem_x, sem_y, sem_w):
    n_items = items_ref[3, 0]
    rows = ITEM_BLOCKS * MOE_BLOCK

    def w_copies(e, s):
        return [pltpu.make_async_copy(src.at[e], dst.at[s], sem_w.at[s])
                for src, dst in ((wg_hbm, wg_f), (wu_hbm, wu_f), (wd_hbm, wd_f))]

    def x_copy(item, s):
        row0 = pl.multiple_of(items_ref[1, item] * MOE_BLOCK, MOE_BLOCK)
        return pltpu.make_async_copy(xs_hbm.at[pl.ds(row0, rows)], xbuf.at[s], sem_x.at[s])

    def y_copies(item, s, fn):
        for k in range(ITEM_BLOCKS):
            row0 = pl.multiple_of((items_ref[1, item] + k) * MOE_BLOCK, MOE_BLOCK)
            cp = pltpu.make_async_copy(ybuf.at[s, pl.ds(k * MOE_BLOCK, MOE_BLOCK)],
                                       ys_hbm.at[pl.ds(row0, MOE_BLOCK)], sem_y.at[s])
            pl.when(k < items_ref[2, item])(functools.partial(fn, cp))

    def expert_of(item):
        return items_ref[0, jnp.minimum(item, n_items - 1)]

    def changes_at(item):
        return ((item < n_items) & (expert_of(item) != expert_of(item - 1))).astype(I32)

    @pl.when(n_items > 0)
    def _():
        x_copy(0, 0).start()
        for cp in w_copies(expert_of(0), 0):
            cp.start()

        @pl.when(changes_at(1) == 1)
        def _():
            for cp in w_copies(expert_of(1), 1):
                cp.start()

    def item_body(i, ordinal):
        slot = i % 2
        prev = jnp.maximum(i - 1, 0)
        e = expert_of(i)
        new_expert = (i == 0) | (e != expert_of(prev))
        c1 = changes_at(i + 1)
        c2 = changes_at(i + 2)

        @pl.when(i + 1 < n_items)
        def _():
            x_copy(i + 1, 1 - slot).start()

        @pl.when(new_expert)
        def _():
            wslot = ordinal % WEIGHT_SLOTS
            for cp in w_copies(e, wslot):
                cp.wait()
            wg_s[...] = wg_f[wslot].astype(BF16)
            wu_s[...] = wu_f[wslot].astype(BF16)
            wd_s[...] = wd_f[wslot].astype(BF16)

        @pl.when(c2 == 1)
        def _():
            for cp in w_copies(expert_of(i + 2), (ordinal + c1 + 1) % WEIGHT_SLOTS):
                cp.start()

        x_copy(i, slot).wait()
        xa, xb = _unpack_halves(xbuf[slot])
        xa = xa.astype(BF16)
        xb = xb.astype(BF16)
        half = xa.shape[1]
        g = _dot(xa, wg_s[0:half]) + _dot(xb, wg_s[half:])
        u = _dot(xa, wu_s[0:half]) + _dot(xb, wu_s[half:])
        y = _dot((_silu(g) * u).astype(BF16), wd_s[...])
        ybuf[slot] = _pack_halves(y[:, :half], y[:, half:])
        y_copies(i, slot, lambda cp: cp.start())

        @pl.when(i > 0)
        def _():
            y_copies(prev, 1 - slot, lambda cp: cp.wait())

        return ordinal + c1

    lax.fori_loop(0, n_items, item_body, jnp.int32(0))

    @pl.when(n_items > 0)
    def _():
        last = n_items - 1
        y_copies(last, last % 2, lambda cp: cp.wait())


def _experts(items, xs, w_gate, w_up, w_down, *, n_blocks):
    half = xs.shape[1]
    e, d, f = w_gate.shape
    rows = ITEM_BLOCKS * MOE_BLOCK
    any_spec = pl.BlockSpec(memory_space=pl.ANY)
    return pl.pallas_call(
        _experts_kernel,
        grid_spec=pltpu.PrefetchScalarGridSpec(
            num_scalar_prefetch=1,
            grid=(1,),
            in_specs=[any_spec, any_spec, any_spec, any_spec],
            out_specs=any_spec,
            scratch_shapes=[pltpu.VMEM((2, rows, half), PACKED), pltpu.VMEM((2, rows, half), PACKED),
                            pltpu.VMEM((WEIGHT_SLOTS, d, f), F32), pltpu.VMEM((WEIGHT_SLOTS, d, f), F32),
                            pltpu.VMEM((WEIGHT_SLOTS, f, d), F32),
                            pltpu.VMEM((d, f), BF16), pltpu.VMEM((d, f), BF16), pltpu.VMEM((f, d), BF16),
                            pltpu.SemaphoreType.DMA((2,)), pltpu.SemaphoreType.DMA((2,)),
                            pltpu.SemaphoreType.DMA((WEIGHT_SLOTS,))]),
        out_shape=jax.ShapeDtypeStruct((n_blocks * MOE_BLOCK, half), PACKED),
        compiler_params=_cparams("arbitrary"),
        name="experts",
    )(items, xs, w_gate, w_up, w_down)


def _sc_gather_rows(table, idx):
    n_idx = idx.shape[0]
    width = table.shape[1]
    info = plsc.get_sparse_core_info()
    n_workers = info.num_cores * info.num_subcores
    per_worker = n_idx // n_workers
    assert per_worker * n_workers == n_idx and per_worker % (SC_GATHER_BUFS * SC_WINDOW) == 0
    mesh = plsc.VectorSubcoreMesh(core_axis_name="c", subcore_axis_name="s")

    def body(table_hbm, idx_hbm, out_hbm, idx_v, rows_v, sem_g, sem_o):
        wid = lax.axis_index("s") * info.num_cores + lax.axis_index("c")
        base = wid * per_worker

        @pl.loop(0, per_worker // (SC_GATHER_BUFS * SC_WINDOW))
        def _(it):
            offs = [base + (it * SC_GATHER_BUFS + b) * SC_WINDOW for b in range(SC_GATHER_BUFS)]
            gathers = []
            for b, off in enumerate(offs):
                pltpu.sync_copy(idx_hbm.at[pl.ds(off, SC_WINDOW)], idx_v.at[b])
                gathers.append(pltpu.async_copy(table_hbm.at[idx_v.at[b]], rows_v.at[b], sem_g.at[b]))
            writes = []
            for b, off in enumerate(offs):
                gathers[b].wait()
                writes.append(pltpu.async_copy(rows_v.at[b], out_hbm.at[pl.ds(off, SC_WINDOW)], sem_o.at[b]))
            for cp in writes:
                cp.wait()

    return pl.kernel(
        body,
        out_type=jax.ShapeDtypeStruct((n_idx, width), table.dtype),
        mesh=mesh,
        scratch_types=[pltpu.VMEM((SC_GATHER_BUFS, SC_WINDOW), I32),
                       pltpu.VMEM((SC_GATHER_BUFS, SC_WINDOW, width), table.dtype),
                       pltpu.SemaphoreType.DMA((SC_GATHER_BUFS,)), pltpu.SemaphoreType.DMA((SC_GATHER_BUFS,))],
        name="sc_gather",
    )(table, idx)


def _combine_kernel(hp_ref, x1_ref, g2_ref, w_ref, sgw_ref, suw_ref, sdw_ref, yg_ref, *rest):
    o_ref = rest[-1]
    xa, xb = _unpack_halves(hp_ref[...])
    xa = xa.astype(BF16)
    xb = xb.astype(BF16)
    half = xa.shape[1]
    tb = xa.shape[0]
    g = _dot(xa, sgw_ref[0:half]) + _dot(xb, sgw_ref[half:])
    u = _dot(xa, suw_ref[0:half]) + _dot(xb, suw_ref[half:])
    shared = _dot((_silu(g) * u).astype(BF16), sdw_ref[...])

    w = w_ref[...]
    acc_a = jnp.zeros((tb, half), F32)
    acc_b = jnp.zeros((tb, half), F32)
    for k in range(TOP_K):
        ya, yb = _unpack_halves(yg_ref[0, k])
        acc_a += ya * w[:, k:k + 1]
        acc_b += yb * w[:, k:k + 1]
    g2 = g2_ref[0]
    o_ref[:, 0:half] = x1_ref[:, 0:half] + g2[:, 0:half] * (acc_a + shared[:, 0:half])
    o_ref[:, half:] = x1_ref[:, half:] + g2[:, half:] * (acc_b + shared[:, half:])


def _combine(hp, x1, g2, w_tok, sgw, suw, sdw, yg, out_prev, *, tb, seq_len, first_step):
    t, half = hp.shape
    d = 2 * half
    per_seq = seq_len // tb
    f = sgw.shape[1]
    s0 = first_step
    in_specs = [pl.BlockSpec((tb, half), lambda i: (s0 + i, 0)),
                pl.BlockSpec((tb, d), lambda i: (s0 + i, 0)),
                pl.BlockSpec((1, 1, d), lambda i: ((s0 + i) // per_seq, 0, 0)),
                pl.BlockSpec((tb, TOP_K), lambda i: (s0 + i, 0)),
                pl.BlockSpec((d, f), lambda i: (0, 0)),
                pl.BlockSpec((d, f), lambda i: (0, 0)),
                pl.BlockSpec((f, d), lambda i: (0, 0)),
                pl.BlockSpec((1, TOP_K, tb, half), lambda i: (i, 0, 0, 0))]
    args = [hp, x1, g2, w_tok, sgw, suw, sdw, yg]
    aliases = {}
    if out_prev is not None:
        in_specs.append(pl.BlockSpec(memory_space=pl.ANY))
        args.append(out_prev)
        aliases = {len(args) - 1: 0}
    return pl.pallas_call(
        _combine_kernel,
        grid=(yg.shape[0],),
        in_specs=in_specs,
        out_specs=pl.BlockSpec((tb, d), lambda i: (s0 + i, 0)),
        out_shape=jax.ShapeDtypeStruct((t, d), F32),
        input_output_aliases=aliases,
        compiler_params=_cparams("arbitrary"),
        name="combine",
    )(*args)


def _rope_tables(l):
    rows = l // GRID_W
    r = jnp.repeat(jnp.arange(rows), GRID_W).astype(F32)
    col = jnp.tile(jnp.arange(GRID_W), rows).astype(F32)
    n_f = HEAD_DIM // 4
    freqs = ROPE_THETA ** (-jnp.arange(n_f, dtype=F32) / n_f)
    ang = jnp.concatenate([r[:, None] * freqs, col[:, None] * freqs], axis=-1)
    ang = jnp.tile(jnp.repeat(ang, 2, axis=1), (1, LANES // HEAD_DIM))
    sign = jnp.where(jnp.arange(LANES) % 2 == 0, -1.0, 1.0).astype(F32)
    return jnp.cos(ang), jnp.sin(ang) * sign


def kernel(x, c, ctx, c_ctx, w_mod, b_mod, norm1_w, norm2_w, w_in, q_norm_w, k_norm_w, ret_decay_fwd,
           ret_decay_bwd, w_out, router_w, router_bias, exp_w_gate, exp_w_up, exp_w_down, sh_w_gate,
           sh_w_up, sh_w_down):
    b, l, d = x.shape
    lc = ctx.shape[1]
    t = b * l
    assert w_mod.shape[0] == 1, "single layer"
    assert l % CHUNK == 0 and lc % CHUNK == 0 and l % GRID_W == 0

    rows = (b + 1 + 7) // 8 * 8
    cc = jnp.zeros((rows, d), F32).at[:b].set(c).at[b].set(c_ctx)
    mod = _modulation(cc, w_mod[0], b_mod[0])
    sh1, sc1, g1, sh2, sc2, g2 = [mod[:b, i * d:(i + 1) * d].reshape(b, 1, d) for i in range(6)]
    shc = mod[b, 0:d].reshape(1, 1, d)
    scc = mod[b, d:2 * d].reshape(1, 1, d)

    wi = w_in[0].astype(BF16)
    qnw = jnp.tile(q_norm_w[0], LANES // HEAD_DIM).reshape(1, LANES)
    knw = jnp.tile(k_norm_w[0], LANES // HEAD_DIM).reshape(1, LANES)
    cos, sin = _rope_tables(l)
    n1 = norm1_w[0].reshape(1, d)

    cklo, ckhi, cvlo, cvhi, crk, crv = _projection(
        ctx, shc, scc, n1, wi, qnw, knw, cos[:lc], sin[:lc], rope=False, with_q=False, tm=TILE_PROJ)
    klo, khi, vlo, vhi, rk, rv, q, rq, sg = _projection(
        x, sh1, sc1, n1, wi, qnw, knw, cos, sin, rope=True, with_q=True, tm=TILE_PROJ)

    attn = _attention(q, klo, khi, vlo, vhi, cklo, ckhi, cvlo, cvhi, tq=TILE_ATTN_Q)
    dec_f = jnp.repeat(ret_decay_fwd[0].astype(F32), HEAD_DIM).reshape(1, RET_W)
    dec_b = jnp.repeat(ret_decay_bwd[0].astype(F32), HEAD_DIM).reshape(1, RET_W)
    ret = _retention(rq, rk, rv, sg, crk, crv, dec_f, dec_b)

    wo = w_out[0].astype(BF16)
    r_hi, r_lo = _split(router_w[0].T)
    x1, hp, logits_t = _out_projection(attn, ret, x, wo[:ATTN_W], wo[ATTN_W:], g1, sh2, sc2,
                                       norm2_w[0].reshape(1, d), r_hi, r_lo, tm=TILE_OUT)

    idx_t, w_t, rank_t, cnt_col, cnt_row = _route(logits_t, router_bias[0].reshape(N_EXPERTS, 1), tb=TILE_TOKENS)
    n_blocks = -(-(t * TOP_K) // MOE_BLOCK) + N_EXPERTS
    tb = TILE_TOKENS
    dest, meta, items = _destinations(idx_t, rank_t, cnt_col, cnt_row, tb=tb, n_blocks=n_blocks)
    dest_flat = dest.reshape(-1)
    steps, _, tbe = dest.shape
    dest_win = dest.reshape(steps, TOP_K, tbe // SC_WINDOW, SC_WINDOW).transpose(0, 2, 1, 3)
    dest_win = dest_win.reshape(t // SC_WINDOW, TOP_K, SC_WINDOW)
    xs = _sc_scatter_rows(hp, dest_win, (n_blocks + ITEM_BLOCKS - 1) * MOE_BLOCK)
    xs = _pad_fill(meta, xs)
    ys = _experts(items, xs, exp_w_gate[0], exp_w_up[0], exp_w_down[0], n_blocks=n_blocks)
    parts = COMBINE_PARTS if steps % COMBINE_PARTS == 0 else 1
    steps_part = steps // parts
    x1f, w_tok = x1.reshape(t, d), w_t.T
    sgw, suw, sdw = sh_w_gate[0].astype(BF16), sh_w_up[0].astype(BF16), sh_w_down[0].astype(BF16)
    out = None
    for p in range(parts):
        idx = dest[p * steps_part:(p + 1) * steps_part].reshape(-1)
        yg = _sc_gather_rows(ys, idx).reshape(steps_part, TOP_K, tbe, d // 2)
        out = _combine(hp, x1f, g2, w_tok, sgw, suw, sdw, yg, out, tb=tbe, seq_len=l, first_step=p * steps_part)
    return out.reshape(b, l, d)
```

```python
import functools

import jax
import jax.numpy as jnp
from jax import lax
from jax.experimental import pallas as pl
from jax.experimental.pallas import tpu as pltpu
from jax.experimental.pallas import tpu_sc as plsc

F32 = jnp.float32
BF16 = jnp.bfloat16
I32 = jnp.int32
U32 = jnp.uint32
PACKED = jnp.int32

HEAD_DIM = 64
LANES = 128
SUBLANES = 8
ATTN_HEADS = 8
ATTN_KV_HEADS = 2
GQA = ATTN_HEADS // ATTN_KV_HEADS
RET_HEADS = 8
ATTN_W = ATTN_HEADS * HEAD_DIM
KV_W = ATTN_KV_HEADS * HEAD_DIM
RET_W = RET_HEADS * HEAD_DIM
RET_PAIRS = RET_W // LANES
CHUNK = 128
GRID_W = 64
ROPE_THETA = 10000.0
N_EXPERTS = 256
TOP_K = 8
N_GROUPS = 8
GROUP_SIZE = N_EXPERTS // N_GROUPS
TOPK_GROUPS = 4
ROUTED_SCALE = 2.5
MOE_BLOCK = 128
EPS = 1e-6
QK_SCALE = HEAD_DIM ** -0.5
LOG2_E = 1.4426950408889634

OFF_AK = 0
OFF_AV = OFF_AK + KV_W
OFF_RK = OFF_AV + KV_W
OFF_RV = OFF_RK + RET_W
CTX_KV_COLS = OFF_RV + RET_W
OFF_AQ = CTX_KV_COLS
OFF_RQ = OFF_AQ + ATTN_W
OFF_RG = OFF_RQ + RET_W
IN_COLS = OFF_RG + RET_W

VMEM_LIMIT = 52 * 1024 * 1024

SUB_ROWS = 256
TILE_PROJ = 1024
TILE_ATTN_Q = 512
TILE_OUT = 512
TILE_TOKENS = 256
ITEM_BLOCKS = 5
COMBINE_PARTS = 4
SC_GATHER_BUFS = 2
WEIGHT_SLOTS = 3
SC_WINDOW = 64
HI_MASK = 0xFFFF0000


def _cparams(*sem):
    return pltpu.CompilerParams(dimension_semantics=sem, vmem_limit_bytes=VMEM_LIMIT)


def _split(a):
    hi = a.astype(BF16)
    lo = (a - hi.astype(F32)).astype(BF16)
    return hi, lo


def _dot(a, b):
    return jnp.dot(a, b, preferred_element_type=F32)


def _dot_nt(a, b):
    return lax.dot_general(a, b, (((1,), (1,)), ((), ())), preferred_element_type=F32)


def _sigmoid(v):
    return 1.0 / (1.0 + jnp.exp(-v))


def _silu(v):
    return v * _sigmoid(v)


def _pack_halves(a, b):
    ua = lax.bitcast_convert_type(a.astype(BF16).astype(F32), U32)
    ub = lax.bitcast_convert_type(b.astype(BF16).astype(F32), U32)
    return lax.bitcast_convert_type((ua & jnp.uint32(HI_MASK)) | (ub >> 16), PACKED)


def _unpack_halves(p):
    u = lax.bitcast_convert_type(p, U32)
    a = lax.bitcast_convert_type(u & jnp.uint32(HI_MASK), F32)
    b = lax.bitcast_convert_type(u << 16, F32)
    return a, b


def _mod_kernel(c_ref, w_ref, b_ref, o_ref):
    s_hi, s_lo = _split(_silu(c_ref[...]))
    w_hi, w_lo = _split(w_ref[...])
    o_ref[...] = _dot(s_hi, w_hi) + _dot(s_hi, w_lo) + _dot(s_lo, w_hi) + b_ref[...]


def _modulation(cc, w_mod, b_mod):
    rows, d = cc.shape
    n = w_mod.shape[1]
    tn = 768
    return pl.pallas_call(
        _mod_kernel,
        grid=(n // tn,),
        in_specs=[pl.BlockSpec((rows, d), lambda j: (0, 0)),
                  pl.BlockSpec((d, tn), lambda j: (0, j)),
                  pl.BlockSpec((1, tn), lambda j: (0, j))],
        out_specs=pl.BlockSpec((rows, tn), lambda j: (0, j)),
        out_shape=jax.ShapeDtypeStruct((rows, n), F32),
        compiler_params=_cparams("arbitrary"),
        name="mod",
    )(cc, w_mod, b_mod.reshape(1, n))


def _segment_ones():
    r = lax.broadcasted_iota(I32, (LANES, LANES), 0) // HEAD_DIM
    c = lax.broadcasted_iota(I32, (LANES, LANES), 1) // HEAD_DIM
    return jnp.where(r == c, 1.0, 0.0).astype(BF16)


def _head_mean_sq(v, seg):
    hi, lo = _split(v * v)
    return (_dot(hi, seg) + _dot(lo, seg)) * (1.0 / HEAD_DIM)


def _proj_kernel(x_ref, sh_ref, sc_ref, nw_ref, wi_ref, qnw_ref, knw_ref, cos_ref, sin_ref,
                 *out_refs, rope, with_q):
    if with_q:
        klo_ref, khi_ref, vlo_ref, vhi_ref, rk_ref, rv_ref, q_ref, rq_ref, sg_ref = out_refs
    else:
        klo_ref, khi_ref, vlo_ref, vhi_ref, rk_ref, rv_ref = out_refs
    tm = x_ref.shape[1]
    sub = min(tm, SUB_ROWS)
    seg = _segment_ones()
    lane = lax.broadcasted_iota(I32, (sub, LANES), 1)
    low_half = lane < HEAD_DIM
    even = (lane & 1) == 0

    for r0 in range(0, tm, sub):
        rows = slice(r0, r0 + sub)
        x = x_ref[0, rows, :]
        h = x * lax.rsqrt(jnp.mean(x * x, axis=-1, keepdims=True) + EPS) * nw_ref[...]
        h = h * (1.0 + sc_ref[0]) + sh_ref[0]
        z = _dot(h.astype(BF16), wi_ref[...])

        def norm_rope(v, w128, rows=rows):
            v = v * lax.rsqrt(_head_mean_sq(v, seg) + EPS) * w128
            if rope:
                swapped = jnp.where(even, pltpu.roll(v, LANES - 1, 1), pltpu.roll(v, 1, 1))
                v = v * cos_ref[rows, :] + swapped * sin_ref[rows, :]
            return v

        k = norm_rope(z[:, OFF_AK:OFF_AK + KV_W], knw_ref[...])
        ksw = pltpu.roll(k, HEAD_DIM, 1)
        klo_ref[0, 0, :, rows] = jnp.transpose(jnp.where(low_half, k, 0.0)).astype(BF16)
        khi_ref[0, 0, :, rows] = jnp.transpose(jnp.where(low_half, 0.0, ksw)).astype(BF16)
        klo_ref[0, 1, :, rows] = jnp.transpose(jnp.where(low_half, ksw, 0.0)).astype(BF16)
        khi_ref[0, 1, :, rows] = jnp.transpose(jnp.where(low_half, 0.0, k)).astype(BF16)
        v = z[:, OFF_AV:OFF_AV + KV_W]
        vsw = pltpu.roll(v, HEAD_DIM, 1)
        vlo_ref[0, 0, rows, :] = jnp.where(low_half, v, 0.0).astype(BF16)
        vhi_ref[0, 0, rows, :] = jnp.where(low_half, 0.0, vsw).astype(BF16)
        vlo_ref[0, 1, rows, :] = jnp.where(low_half, vsw, 0.0).astype(BF16)
        vhi_ref[0, 1, rows, :] = jnp.where(low_half, 0.0, v).astype(BF16)
        rk_ref[0, rows, :] = (z[:, OFF_RK:OFF_RK + RET_W] * QK_SCALE).astype(BF16)
        rv_ref[0, rows, :] = z[:, OFF_RV:OFF_RV + RET_W].astype(BF16)
        if with_q:
            for j in range(ATTN_W // LANES):
                qj = norm_rope(z[:, OFF_AQ + j * LANES:OFF_AQ + (j + 1) * LANES], qnw_ref[...])
                q_ref[0, rows, j * LANES:(j + 1) * LANES] = (qj * (QK_SCALE * LOG2_E)).astype(BF16)
            rq_ref[0, rows, :] = z[:, OFF_RQ:OFF_RQ + RET_W].astype(BF16)
            sg_ref[0, rows, :] = _silu(z[:, OFF_RG:OFF_RG + RET_W]).astype(BF16)


def _projection(x, shift, scale, norm_w, wi_bf16, qnw, knw, cos, sin, *, rope, with_q, tm):
    b, l, d = x.shape
    tm = min(tm, l)
    ncols = IN_COLS if with_q else CTX_KV_COLS
    per_batch = shift.shape[0] > 1
    mod_idx = (lambda bi, i: (bi, 0, 0)) if per_batch else (lambda bi, i: (0, 0, 0))
    kv_shape = jax.ShapeDtypeStruct((b, ATTN_KV_HEADS, l, LANES), BF16)
    kv_spec = pl.BlockSpec((1, ATTN_KV_HEADS, tm, LANES), lambda bi, i: (bi, 0, i, 0))
    kt_shape = jax.ShapeDtypeStruct((b, ATTN_KV_HEADS, LANES, l), BF16)
    kt_spec = pl.BlockSpec((1, ATTN_KV_HEADS, LANES, tm), lambda bi, i: (bi, 0, 0, i))
    w_shape = jax.ShapeDtypeStruct((b, l, RET_W), BF16)
    w_spec = pl.BlockSpec((1, tm, RET_W), lambda bi, i: (bi, i, 0))
    out_shape = [kt_shape] * 2 + [kv_shape] * 2 + [w_shape] * 2
    out_specs = [kt_spec] * 2 + [kv_spec] * 2 + [w_spec] * 2
    if with_q:
        out_shape += [w_shape] * 3
        out_specs += [w_spec] * 3
    return pl.pallas_call(
        functools.partial(_proj_kernel, rope=rope, with_q=with_q),
        grid=(b, l // tm),
        in_specs=[pl.BlockSpec((1, tm, d), lambda bi, i: (bi, i, 0)),
                  pl.BlockSpec((1, 1, d), mod_idx),
                  pl.BlockSpec((1, 1, d), mod_idx),
                  pl.BlockSpec((1, d), lambda bi, i: (0, 0)),
                  pl.BlockSpec((d, ncols), lambda bi, i: (0, 0)),
                  pl.BlockSpec((1, LANES), lambda bi, i: (0, 0)),
                  pl.BlockSpec((1, LANES), lambda bi, i: (0, 0)),
                  pl.BlockSpec((tm, LANES), lambda bi, i: (i, 0)),
                  pl.BlockSpec((tm, LANES), lambda bi, i: (i, 0))],
        out_specs=out_specs,
        out_shape=out_shape,
        compiler_params=_cparams("arbitrary", "arbitrary"),
        name="proj_latent" if with_q else "proj_ctx",
    )(x, shift, scale, norm_w, wi_bf16, qnw, knw, cos, sin)


def _attn_kernel(q_ref, klo_ref, khi_ref, vlo_ref, vhi_ref, cklo_ref, ckhi_ref, cvlo_ref, cvhi_ref,
                 o_ref, kl_s, kh_s, va_s, *, l, lc):
    lk = l + lc

    @pl.when(pl.program_id(2) == 0)
    def _():
        kl_s[:, 0:l] = klo_ref[0, 0]
        kl_s[:, l:lk] = cklo_ref[0, 0]
        kh_s[:, 0:l] = khi_ref[0, 0]
        kh_s[:, l:lk] = ckhi_ref[0, 0]
        lane = lax.broadcasted_iota(I32, (lk, LANES), 1)
        ones_lo = jnp.where(lane < HEAD_DIM, 1.0, 0.0).astype(BF16)
        ones_hi = jnp.where(lane < HEAD_DIM, 0.0, 1.0).astype(BF16)
        for g, (v_ref, cv_ref, ones) in enumerate(((vlo_ref, cvlo_ref, ones_lo), (vhi_ref, cvhi_ref, ones_hi),
                                                   (vlo_ref, cvlo_ref, ones_lo), (vhi_ref, cvhi_ref, ones_hi))):
            v_col, one_col = (0, LANES) if g < 2 else (LANES, 0)
            va_s[g, 0:l, v_col:v_col + LANES] = v_ref[0, 0]
            va_s[g, l:lk, v_col:v_col + LANES] = cv_ref[0, 0]
            va_s[g, :, one_col:one_col + LANES] = ones

    q = q_ref[0]
    acc = []
    for g in range(GQA):
        qp = q[:, (g // 2) * LANES:(g // 2 + 1) * LANES]
        kt = kl_s[...] if g % 2 == 0 else kh_s[...]
        s = _dot(qp, kt)
        p = jnp.exp2(s - jnp.max(s, axis=-1, keepdims=True)).astype(BF16)
        acc.append(_dot(p, va_s[g]))
    out_a = acc[0] + acc[1]
    out_b = acc[2] + acc[3]
    o_ref[0, :, 0:LANES] = (out_a[:, 0:LANES] / out_a[:, LANES:2 * LANES]).astype(BF16)
    o_ref[0, :, LANES:2 * LANES] = (out_b[:, LANES:2 * LANES] / out_b[:, 0:LANES]).astype(BF16)


def _attention(q, klo, khi, vlo, vhi, cklo, ckhi, cvlo, cvhi, *, tq):
    b, l, _ = q.shape
    lc = cvlo.shape[2]
    lk = l + lc
    tq = min(tq, l)
    gw = GQA * HEAD_DIM
    kt_spec = pl.BlockSpec((1, 1, LANES, l), lambda bi, h, i: (bi, h, 0, 0))
    kv_spec = pl.BlockSpec((1, 1, l, LANES), lambda bi, h, i: (bi, h, 0, 0))
    ckt_spec = pl.BlockSpec((1, 1, LANES, lc), lambda bi, h, i: (bi, h, 0, 0))
    ckv_spec = pl.BlockSpec((1, 1, lc, LANES), lambda bi, h, i: (bi, h, 0, 0))
    return pl.pallas_call(
        functools.partial(_attn_kernel, l=l, lc=lc),
        grid=(b, ATTN_KV_HEADS, l // tq),
        in_specs=([pl.BlockSpec((1, tq, gw), lambda bi, h, i: (bi, i, h))] + [kt_spec] * 2 + [kv_spec] * 2
                  + [ckt_spec] * 2 + [ckv_spec] * 2),
        out_specs=pl.BlockSpec((1, tq, gw), lambda bi, h, i: (bi, i, h)),
        out_shape=jax.ShapeDtypeStruct((b, l, ATTN_W), BF16),
        scratch_shapes=[pltpu.VMEM((LANES, lk), BF16), pltpu.VMEM((LANES, lk), BF16),
                        pltpu.VMEM((GQA, lk, 2 * LANES), BF16)],
        compiler_params=_cparams("arbitrary", "arbitrary", "arbitrary"),
        name="attn",
    )(q, klo, khi, vlo, vhi, cklo, ckhi, cvlo, cvhi)


def _log_sigmoid(v):
    return jnp.minimum(v, 0.0) - jnp.log(1.0 + jnp.exp(-jnp.abs(v)))


def _ret_kernel(rq_ref, rk_ref, rv_ref, sg_ref, crk_ref, crv_ref, df_ref, db_ref, o_ref,
                m_s, xi_s, zeta_s, kv_s, st_s, *, l, lc):
    n = l // CHUNK
    nc = lc // CHUNK
    lgf = _log_sigmoid(df_ref[...])
    lgb = _log_sigmoid(db_ref[...])
    pos = lax.broadcasted_iota(I32, (CHUNK, LANES), 0).astype(F32)
    row = lax.broadcasted_iota(I32, (CHUNK, CHUNK), 0)
    col = lax.broadcasted_iota(I32, (CHUNK, CHUNK), 1)
    diff = (row - col).astype(F32)
    g_chunk = []
    for p in range(RET_PAIRS):
        cols = slice(p * LANES, (p + 1) * LANES)
        lf, lb = lgf[:, cols], lgb[:, cols]
        xi_s[p, :, 0:LANES] = jnp.exp((pos + 1.0) * lf)
        xi_s[p, :, LANES:] = jnp.exp((CHUNK - pos) * lb)
        zeta_s[p, :, 0:LANES] = jnp.exp((CHUNK - 1.0 - pos) * lf)
        zeta_s[p, :, LANES:] = jnp.exp(pos * lb)
        g_chunk.append((jnp.exp(CHUNK * lf), jnp.exp(CHUNK * lb)))
        for j in range(2):
            h = 2 * p + j
            hf = lgf[:, h * HEAD_DIM:h * HEAD_DIM + 1]
            hb = lgb[:, h * HEAD_DIM:h * HEAD_DIM + 1]
            m_s[p, :, j * CHUNK:(j + 1) * CHUNK] = jnp.where(
                diff > 0, jnp.exp(diff * hf), jnp.where(diff < 0, jnp.exp(-diff * hb), 2.0))

    lane = lax.broadcasted_iota(I32, (CHUNK, LANES), 1)
    low_half = lane < HEAD_DIM
    diag = (lax.broadcasted_iota(I32, (LANES, LANES), 0) // HEAD_DIM
            == lax.broadcasted_iota(I32, (LANES, LANES), 1) // HEAD_DIM)
    seg = jnp.where(diag, 1.0, 0.0).astype(BF16)
    seg2 = jnp.concatenate([seg, seg], axis=0)
    diag2 = jnp.concatenate([diag, diag], axis=0)

    def split_heads(a):
        zero = jnp.zeros_like(a)
        return jnp.concatenate([jnp.where(low_half, a, zero), jnp.where(low_half, zero, a)], axis=0)

    def contrib(k_ref, v_ref, r0, p):
        cols = slice(p * LANES, (p + 1) * LANES)
        kp = k_ref[0, pl.ds(r0, CHUNK), cols].astype(F32)
        kz = jnp.concatenate([kp, kp], axis=1) * zeta_s[p]
        kv = _dot(jnp.transpose(kz).astype(BF16), v_ref[0, pl.ds(r0, CHUNK), cols])
        return jnp.where(diag2, kv, 0.0)

    for c in range(nc):
        for p in range(RET_PAIRS):
            kv_s[c, p] = contrib(crk_ref, crv_ref, c * CHUNK, p)

    def contrib_body(c, carry):
        r0 = pl.multiple_of(c * CHUNK, CHUNK)
        for p in range(RET_PAIRS):
            kv_s[nc + c, p] = contrib(rk_ref, rv_ref, r0, p)
        return carry

    lax.fori_loop(0, n, contrib_body, 0)

    for p in range(RET_PAIRS):
        gf, gb = g_chunk[p]
        sf = jnp.zeros((LANES, LANES), F32)
        sb = jnp.zeros((LANES, LANES), F32)
        for c in range(nc):
            sf = gf * sf + kv_s[c, p, 0:LANES]
            sb = gb * sb + kv_s[nc - 1 - c, p, LANES:]

        def fwd_scan(c, s, p=p, gf=gf):
            st_s[c, p, 0:LANES] = s.astype(BF16)
            return gf * s + kv_s[nc + c, p, 0:LANES]

        def bwd_scan(j, s, p=p, gb=gb):
            c = n - 1 - j
            st_s[c, p, LANES:] = s.astype(BF16)
            return gb * s + kv_s[nc + c, p, LANES:]

        lax.fori_loop(0, n, fwd_scan, sf)
        lax.fori_loop(0, n, bwd_scan, sb)

    def out_body(c, carry):
        r0 = pl.multiple_of(c * CHUNK, CHUNK)
        for p in range(RET_PAIRS):
            cols = slice(p * LANES, (p + 1) * LANES)
            qp = rq_ref[0, pl.ds(r0, CHUNK), cols]
            kp = rk_ref[0, pl.ds(r0, CHUNK), cols]
            vp = rv_ref[0, pl.ds(r0, CHUNK), cols]
            s2 = _dot_nt(qp, split_heads(kp))
            a2 = (s2 * m_s[p]).astype(BF16)
            y = _dot(a2, split_heads(vp))
            qf = qp.astype(F32)
            qx = (jnp.concatenate([qf, qf], axis=1) * xi_s[p]).astype(BF16)
            y += _dot(qx, st_s[c, p])
            hi, lo = _split(y * y)
            ms = _dot(jnp.concatenate([hi, lo], axis=1), seg2) * (1.0 / HEAD_DIM)
            out = y * lax.rsqrt(ms + EPS) * sg_ref[0, pl.ds(r0, CHUNK), cols].astype(F32)
            o_ref[0, pl.ds(r0, CHUNK), cols] = out.astype(BF16)
        return carry

    lax.fori_loop(0, n, out_body, 0)


def _retention(rq, rk, rv, sg, crk, crv, dec_f, dec_b):
    b, l, _ = rq.shape
    lc = crk.shape[1]
    n = l // CHUNK
    nc = lc // CHUNK
    spec = pl.BlockSpec((1, l, RET_W), lambda bi: (bi, 0, 0))
    cspec = pl.BlockSpec((1, lc, RET_W), lambda bi: (bi, 0, 0))
    dspec = pl.BlockSpec((1, RET_W), lambda bi: (0, 0))
    return pl.pallas_call(
        functools.partial(_ret_kernel, l=l, lc=lc),
        grid=(b,),
        in_specs=[spec, spec, spec, spec, cspec, cspec, dspec, dspec],
        out_specs=spec,
        out_shape=jax.ShapeDtypeStruct((b, l, RET_W), BF16),
        scratch_shapes=[pltpu.VMEM((RET_PAIRS, CHUNK, 2 * CHUNK), F32),
                        pltpu.VMEM((RET_PAIRS, CHUNK, 2 * LANES), F32),
                        pltpu.VMEM((RET_PAIRS, CHUNK, 2 * LANES), F32),
                        pltpu.VMEM((nc + n, RET_PAIRS, 2 * LANES, LANES), F32),
                        pltpu.VMEM((n, RET_PAIRS, 2 * LANES, LANES), BF16)],
        compiler_params=_cparams("arbitrary"),
        name="ret",
    )(rq, rk, rv, sg, crk, crv, dec_f, dec_b)


def _out_kernel(attn_ref, ret_ref, x_ref, wa_ref, wr_ref, g1_ref, sh_ref, sc_ref, nw_ref, rhi_ref, rlo_ref,
                x1_ref, hp_ref, lg_ref):
    y = _dot(attn_ref[0], wa_ref[...]) + _dot(ret_ref[0], wr_ref[...])
    x1 = x_ref[0] + g1_ref[0] * y
    x1_ref[0] = x1
    h = x1 * lax.rsqrt(jnp.mean(x1 * x1, axis=-1, keepdims=True) + EPS) * nw_ref[...]
    h = h * (1.0 + sc_ref[0]) + sh_ref[0]
    half = h.shape[1] // 2
    hp_ref[...] = _pack_halves(h[:, :half], h[:, half:])
    h_hi, h_lo = _split(h)
    lg_ref[...] = _dot_nt(rhi_ref[...], h_hi) + _dot_nt(rhi_ref[...], h_lo) + _dot_nt(rlo_ref[...], h_hi)


def _out_projection(attn, ret, x, wa, wr, g1, sh2, sc2, norm_w, r_hi, r_lo, *, tm):
    b, l, d = x.shape
    tm = min(tm, l)
    nt = l // tm
    t = b * l
    mspec = pl.BlockSpec((1, 1, d), lambda bi, i: (bi, 0, 0))
    return pl.pallas_call(
        _out_kernel,
        grid=(b, nt),
        in_specs=[pl.BlockSpec((1, tm, ATTN_W), lambda bi, i: (bi, i, 0)),
                  pl.BlockSpec((1, tm, RET_W), lambda bi, i: (bi, i, 0)),
                  pl.BlockSpec((1, tm, d), lambda bi, i: (bi, i, 0)),
                  pl.BlockSpec((ATTN_W, d), lambda bi, i: (0, 0)),
                  pl.BlockSpec((RET_W, d), lambda bi, i: (0, 0)),
                  mspec, mspec, mspec,
                  pl.BlockSpec((1, d), lambda bi, i: (0, 0)),
                  pl.BlockSpec((N_EXPERTS, d), lambda bi, i: (0, 0)),
                  pl.BlockSpec((N_EXPERTS, d), lambda bi, i: (0, 0))],
        out_specs=[pl.BlockSpec((1, tm, d), lambda bi, i: (bi, i, 0)),
                   pl.BlockSpec((tm, d // 2), lambda bi, i: (bi * nt + i, 0)),
                   pl.BlockSpec((N_EXPERTS, tm), lambda bi, i: (0, bi * nt + i))],
        out_shape=[jax.ShapeDtypeStruct((b, l, d), F32),
                   jax.ShapeDtypeStruct((t, d // 2), PACKED),
                   jax.ShapeDtypeStruct((N_EXPERTS, t), F32)],
        compiler_params=_cparams("arbitrary", "arbitrary"),
        name="out_proj",
    )(attn, ret, x, wa, wr, g1, sh2, sc2, norm_w, r_hi, r_lo)


def _route_kernel(lg_ref, bias_ref, idx_ref, w_ref, rank_ref, cnt_col_ref, cnt_row_ref, tri_s, col_s, row_s):
    tb = lg_ref.shape[1]
    step = pl.program_id(0)

    @pl.when(step == 0)
    def _():
        r = lax.broadcasted_iota(I32, (tb, tb), 0)
        c = lax.broadcasted_iota(I32, (tb, tb), 1)
        tri_s[...] = jnp.where(r <= c, 1.0, 0.0).astype(BF16)
        col_s[...] = jnp.zeros_like(col_s)
        row_s[...] = jnp.zeros_like(row_s)

    scores = _sigmoid(lg_ref[...])
    biased = scores + bias_ref[...]
    neg = -jnp.inf
    sub = lax.broadcasted_iota(I32, (GROUP_SIZE, tb), 0).astype(F32)

    gscore = []
    for g in range(N_GROUPS):
        blk = biased[g * GROUP_SIZE:(g + 1) * GROUP_SIZE]
        m1 = jnp.max(blk, axis=0, keepdims=True)
        first = jnp.min(jnp.where(blk == m1, sub, float(GROUP_SIZE)), axis=0, keepdims=True)
        m2 = jnp.max(jnp.where(sub == first, neg, blk), axis=0, keepdims=True)
        gscore.append(m1 + m2)
    gs = jnp.concatenate(gscore, axis=0)
    gsub = lax.broadcasted_iota(I32, (N_GROUPS, tb), 0).astype(F32)
    keep = jnp.zeros((N_GROUPS, tb), F32)
    for _ in range(TOPK_GROUPS):
        m = jnp.max(gs, axis=0, keepdims=True)
        first = jnp.min(jnp.where(gs == m, gsub, float(N_GROUPS)), axis=0, keepdims=True)
        sel = gsub == first
        keep = jnp.where(sel, 1.0, keep)
        gs = jnp.where(sel, neg, gs)
    masked = jnp.concatenate(
        [jnp.where(keep[g:g + 1] > 0.0, biased[g * GROUP_SIZE:(g + 1) * GROUP_SIZE], neg)
         for g in range(N_GROUPS)], axis=0)

    esub = lax.broadcasted_iota(I32, (N_EXPERTS, tb), 0).astype(F32)
    sels, idxs, ws = [], [], []
    chosen = jnp.zeros((N_EXPERTS, tb), F32)
    for _ in range(TOP_K):
        m = jnp.max(masked, axis=0, keepdims=True)
        first = jnp.min(jnp.where(masked == m, esub, float(N_EXPERTS)), axis=0, keepdims=True)
        sel = esub == first
        sels.append(sel)
        idxs.append(first)
        ws.append(jnp.sum(jnp.where(sel, scores, 0.0), axis=0, keepdims=True))
        chosen = jnp.where(sel, 1.0, chosen)
        masked = jnp.where(sel, neg, masked)
    wsum = ws[0]
    for k in range(1, TOP_K):
        wsum = wsum + ws[k]
    idx_ref[...] = jnp.concatenate(idxs, axis=0).astype(I32)
    w_ref[...] = jnp.concatenate([wk / wsum * ROUTED_SCALE for wk in ws], axis=0)

    chosen_b = chosen.astype(BF16)
    incl = _dot(chosen_b, tri_s[...])
    before = incl - chosen + col_s[...]
    rank_ref[...] = jnp.concatenate(
        [jnp.sum(jnp.where(sel, before, 0.0), axis=0, keepdims=True) for sel in sels], axis=0).astype(I32)
    col_s[...] = col_s[...] + incl[:, tb - 1:tb]
    row_s[...] = row_s[...] + _dot_nt(jnp.ones((8, tb), BF16), chosen_b)
    cnt_col_ref[...] = col_s[...].astype(I32)
    cnt_row_ref[...] = row_s[...].astype(I32)


def _route(logits_t, bias_col, *, tb):
    e, t = logits_t.shape
    tb = min(tb, t)
    kspec = pl.BlockSpec((TOP_K, tb), lambda i: (0, i))
    return pl.pallas_call(
        _route_kernel,
        grid=(t // tb,),
        in_specs=[pl.BlockSpec((e, tb), lambda i: (0, i)),
                  pl.BlockSpec((e, 1), lambda i: (0, 0))],
        out_specs=[kspec, kspec, kspec,
                   pl.BlockSpec((e, 1), lambda i: (0, 0)),
                   pl.BlockSpec((8, e), lambda i: (0, 0))],
        out_shape=[jax.ShapeDtypeStruct((TOP_K, t), I32),
                   jax.ShapeDtypeStruct((TOP_K, t), F32),
                   jax.ShapeDtypeStruct((TOP_K, t), I32),
                   jax.ShapeDtypeStruct((e, 1), I32),
                   jax.ShapeDtypeStruct((8, e), I32)],
        scratch_shapes=[pltpu.VMEM((tb, tb), BF16), pltpu.VMEM((e, 1), F32), pltpu.VMEM((8, e), F32)],
        compiler_params=_cparams("arbitrary"),
        name="route",
    )(logits_t, bias_col)


def _pad_block(cnt):
    return (cnt + (MOE_BLOCK - 1)) // MOE_BLOCK * MOE_BLOCK


def _max_items(n_blocks):
    return n_blocks // ITEM_BLOCKS + N_EXPERTS


def _dest_kernel(idx_ref, rank_ref, cnt_col_ref, cnt_row_ref, dest_ref, meta_ref, items_ref, start_s):
    tb = idx_ref.shape[1]
    nip = items_ref.shape[1]

    @pl.when(pl.program_id(0) == 0)
    def _():
        pad_col = _pad_block(cnt_col_ref[...])
        pad_row = _pad_block(cnt_row_ref[0:1, :])
        er = lax.broadcasted_iota(I32, (N_EXPERTS, N_EXPERTS), 0)
        ec = lax.broadcasted_iota(I32, (N_EXPERTS, N_EXPERTS), 1)
        start_col = jnp.sum(jnp.where(ec < er, pad_row, 0), axis=1, keepdims=True)
        start_row = jnp.sum(jnp.where(er < ec, pad_col, 0), axis=0, keepdims=True)
        start_s[...] = start_col

        used = jnp.sum(pad_row, axis=1, keepdims=True) // MOE_BLOCK
        meta_ref[...] = jnp.concatenate(
            [cnt_row_ref[0:1, :], start_row, pad_row, jnp.broadcast_to(used, (1, N_EXPERTS)),
             jnp.zeros((4, N_EXPERTS), I32)], axis=0)

        nb_col = pad_col // MOE_BLOCK
        it_col = (nb_col + (ITEM_BLOCKS - 1)) // ITEM_BLOCKS
        it_row = (pad_row // MOE_BLOCK + (ITEM_BLOCKS - 1)) // ITEM_BLOCKS
        it_start = jnp.sum(jnp.where(ec < er, it_row, 0), axis=1, keepdims=True)
        n_items = jnp.sum(it_row, axis=1, keepdims=True)
        lane = lax.broadcasted_iota(I32, (1, nip), 1)
        owner = jnp.sum(jnp.where(it_start + it_col <= lane, 1, 0), axis=0, keepdims=True)
        owner = jnp.minimum(owner, N_EXPERTS - 1)
        onehot = lax.broadcasted_iota(I32, (N_EXPERTS, nip), 0) == owner

        def pick(col):
            return jnp.sum(jnp.where(onehot, col, 0), axis=0, keepdims=True)

        j = lane - pick(it_start)
        block0 = pick(start_col) // MOE_BLOCK + ITEM_BLOCKS * j
        nvalid = jnp.clip(pick(nb_col) - ITEM_BLOCKS * j, 0, ITEM_BLOCKS)
        items_ref[...] = jnp.concatenate(
            [owner, block0, jnp.where(lane < n_items, nvalid, 0), jnp.broadcast_to(n_items, (1, nip)),
             jnp.zeros((4, nip), I32)], axis=0)

    start_col = start_s[...]
    esub = lax.broadcasted_iota(I32, (N_EXPERTS, tb), 0)
    rows = []
    for k in range(TOP_K):
        onehot = esub == idx_ref[k:k + 1, :]
        rows.append(jnp.sum(jnp.where(onehot, start_col, 0), axis=0, keepdims=True) + rank_ref[k:k + 1, :])
    dest_ref[0] = jnp.concatenate(rows, axis=0)


def _destinations(idx_t, rank_t, cnt_col, cnt_row, *, tb, n_blocks):
    _, t = idx_t.shape
    tb = min(tb, t)
    nip = (_max_items(n_blocks) + LANES - 1) // LANES * LANES
    kspec = pl.BlockSpec((TOP_K, tb), lambda i: (0, i))
    return pl.pallas_call(
        _dest_kernel,
        grid=(t // tb,),
        in_specs=[kspec, kspec,
                  pl.BlockSpec((N_EXPERTS, 1), lambda i: (0, 0)),
                  pl.BlockSpec((8, N_EXPERTS), lambda i: (0, 0))],
        out_specs=[pl.BlockSpec((1, TOP_K, tb), lambda i: (i, 0, 0)),
                   pl.BlockSpec((8, N_EXPERTS), lambda i: (0, 0)),
                   pl.BlockSpec((8, nip), lambda i: (0, 0))],
        out_shape=[jax.ShapeDtypeStruct((t // tb, TOP_K, tb), I32),
                   jax.ShapeDtypeStruct((8, N_EXPERTS), I32),
                   jax.ShapeDtypeStruct((8, nip), I32)],
        scratch_shapes=[pltpu.VMEM((N_EXPERTS, 1), I32)],
        compiler_params=_cparams("arbitrary"),
        name="dest",
    )(idx_t, rank_t, cnt_col, cnt_row)


_PAD_BITS = (64, 32, 16, 8)


def _sc_scatter_rows(rows, dest_win, n_out_rows):
    n_win, n_slots, win = dest_win.shape
    width = rows.shape[1]
    info = plsc.get_sparse_core_info()
    n_workers = info.num_cores * info.num_subcores
    per_worker = n_win // n_workers
    assert per_worker * n_workers == n_win and win <= LANES
    mesh = plsc.VectorSubcoreMesh(core_axis_name="c", subcore_axis_name="s")

    def body(rows_hbm, dest_hbm, out_hbm, idx_v, rows_v, sem):
        wid = lax.axis_index("s") * info.num_cores + lax.axis_index("c")

        @pl.loop(0, per_worker)
        def _(j):
            w = wid * per_worker + j
            pltpu.sync_copy(dest_hbm.at[w], idx_v)
            pltpu.sync_copy(rows_hbm.at[pl.ds(w * win, win)], rows_v)
            copies = [pltpu.async_copy(rows_v, out_hbm.at[idx_v.at[k]], sem) for k in range(n_slots)]
            for cp in copies:
                cp.wait()

    return pl.kernel(
        body,
        out_type=jax.ShapeDtypeStruct((n_out_rows, width), rows.dtype),
        mesh=mesh,
        scratch_types=[pltpu.VMEM((n_slots, win), I32), pltpu.VMEM((win, width), rows.dtype),
                       pltpu.SemaphoreType.DMA],
        name="sc_scatter",
    )(rows, dest_win)


def _pad_fill_kernel(meta_ref, xs_in, xs_hbm, zero_s, sem_z, *, e_per_step):
    del xs_in
    step = pl.program_id(0)
    zero_s[...] = jnp.zeros_like(zero_s)

    def tail_copy(c):
        row0 = pl.multiple_of((meta_ref[3, 0] + c) * MOE_BLOCK, MOE_BLOCK)
        return pltpu.make_async_copy(zero_s, xs_hbm.at[pl.ds(row0, MOE_BLOCK)], sem_z)

    @pl.when(step == 0)
    def _():
        for c in range(ITEM_BLOCKS - 1):
            tail_copy(c).start()
        for c in range(ITEM_BLOCKS - 1):
            tail_copy(c).wait()

    def pad_copies(e):
        cnt = meta_ref[0, e]
        off = meta_ref[1, e] + cnt
        rem = meta_ref[2, e] - cnt
        head = rem & (SUBLANES - 1)
        out = []
        for i in range(SUBLANES - 1):
            out.append((i < head,
                        pltpu.make_async_copy(zero_s.at[pl.ds(0, 1)], xs_hbm.at[pl.ds(off + i, 1)], sem_z)))
        off = off + head
        for bit in _PAD_BITS:
            out.append(((rem & bit) != 0,
                        pltpu.make_async_copy(zero_s.at[pl.ds(0, bit)],
                                              xs_hbm.at[pl.ds(pl.multiple_of(off, SUBLANES), bit)], sem_z)))
            off = off + (rem & bit)
        return out

    for j in range(e_per_step):
        for cond, c in pad_copies(step * e_per_step + j):
            pl.when(cond)(c.start)
    for j in range(e_per_step):
        for cond, c in pad_copies(step * e_per_step + j):
            pl.when(cond)(c.wait)


PAD_FILL_EXPERTS_PER_STEP = 8


def _pad_fill(meta, xs):
    half = xs.shape[1]
    return pl.pallas_call(
        functools.partial(_pad_fill_kernel, e_per_step=PAD_FILL_EXPERTS_PER_STEP),
        grid=(N_EXPERTS // PAD_FILL_EXPERTS_PER_STEP,),
        in_specs=[pl.BlockSpec(memory_space=pltpu.SMEM),
                  pl.BlockSpec(memory_space=pl.ANY)],
        out_specs=pl.BlockSpec(memory_space=pl.ANY),
        out_shape=jax.ShapeDtypeStruct(xs.shape, xs.dtype),
        input_output_aliases={1: 0},
        scratch_shapes=[pltpu.VMEM((MOE_BLOCK, half), PACKED), pltpu.SemaphoreType.DMA],
        compiler_params=_cparams("arbitrary"),
        name="pad_fill",
    )(meta, xs)


def _experts_kernel(items_ref, xs_hbm, wg_hbm, wu_hbm, wd_hbm, ys_hbm,
                    xbuf, ybuf, wg_f, wu_f, wd_f, wg_s, wu_s, wd_s, sem_x, sem_y, sem_w):
    n_items = items_ref[3, 0]
    rows = ITEM_BLOCKS * MOE_BLOCK

    def w_copies(e, s):
        return [pltpu.make_async_copy(src.at[e], dst.at[s], sem_w.at[s])
                for src, dst in ((wg_hbm, wg_f), (wu_hbm, wu_f), (wd_hbm, wd_f))]

    def x_copy(item, s):
        row0 = pl.multiple_of(items_ref[1, item] * MOE_BLOCK, MOE_BLOCK)
        return pltpu.make_async_copy(xs_hbm.at[pl.ds(row0, rows)], xbuf.at[s], sem_x.at[s])

    def y_copies(item, s, fn):
        for k in range(ITEM_BLOCKS):
            row0 = pl.multiple_of((items_ref[1, item] + k) * MOE_BLOCK, MOE_BLOCK)
            cp = pltpu.make_async_copy(ybuf.at[s, pl.ds(k * MOE_BLOCK, MOE_BLOCK)],
                                       ys_hbm.at[pl.ds(row0, MOE_BLOCK)], sem_y.at[s])
            pl.when(k < items_ref[2, item])(functools.partial(fn, cp))

    def expert_of(item):
        return items_ref[0, jnp.minimum(item, n_items - 1)]

    def changes_at(item):
        return ((item < n_items) & (expert_of(item) != expert_of(item - 1))).astype(I32)

    @pl.when(n_items > 0)
    def _():
        x_copy(0, 0).start()
        for cp in w_copies(expert_of(0), 0):
            cp.start()

        @pl.when(changes_at(1) == 1)
        def _():
            for cp in w_copies(expert_of(1), 1):
                cp.start()

    def item_body(i, ordinal):
        slot = i % 2
        prev = jnp.maximum(i - 1, 0)
        e = expert_of(i)
        new_expert = (i == 0) | (e != expert_of(prev))
        c1 = changes_at(i + 1)
        c2 = changes_at(i + 2)

        @pl.when(i + 1 < n_items)
        def _():
            x_copy(i + 1, 1 - slot).start()

        @pl.when(new_expert)
        def _():
            wslot = ordinal % WEIGHT_SLOTS
            for cp in w_copies(e, wslot):
                cp.wait()
            wg_s[...] = wg_f[wslot].astype(BF16)
            wu_s[...] = wu_f[wslot].astype(BF16)
            wd_s[...] = wd_f[wslot].astype(BF16)

        @pl.when(c2 == 1)
        def _():
            for cp in w_copies(expert_of(i + 2), (ordinal + c1 + 1) % WEIGHT_SLOTS):
                cp.start()

        x_copy(i, slot).wait()
        xa, xb = _unpack_halves(xbuf[slot])
        xa = xa.astype(BF16)
        xb = xb.astype(BF16)
        half = xa.shape[1]
        g = _dot(xa, wg_s[0:half]) + _dot(xb, wg_s[half:])
        u = _dot(xa, wu_s[0:half]) + _dot(xb, wu_s[half:])
        y = _dot((_silu(g) * u).astype(BF16), wd_s[...])
        ybuf[slot] = _pack_halves(y[:, :half], y[:, half:])
        y_copies(i, slot, lambda cp: cp.start())

        @pl.when(i > 0)
        def _():
            y_copies(prev, 1 - slot, lambda cp: cp.wait())

        return ordinal + c1

    lax.fori_loop(0, n_items, item_body, jnp.int32(0))

    @pl.when(n_items > 0)
    def _():
        last = n_items - 1
        y_copies(last, last % 2, lambda cp: cp.wait())


def _experts(items, xs, w_gate, w_up, w_down, *, n_blocks):
    half = xs.shape[1]
    e, d, f = w_gate.shape
    rows = ITEM_BLOCKS * MOE_BLOCK
    any_spec = pl.BlockSpec(memory_space=pl.ANY)
    return pl.pallas_call(
        _experts_kernel,
        grid_spec=pltpu.PrefetchScalarGridSpec(
            num_scalar_prefetch=1,
            grid=(1,),
            in_specs=[any_spec, any_spec, any_spec, any_spec],
            out_specs=any_spec,
            scratch_shapes=[pltpu.VMEM((2, rows, half), PACKED), pltpu.VMEM((2, rows, half), PACKED),
                            pltpu.VMEM((WEIGHT_SLOTS, d, f), F32), pltpu.VMEM((WEIGHT_SLOTS, d, f), F32),
                            pltpu.VMEM((WEIGHT_SLOTS, f, d), F32),
                            pltpu.VMEM((d, f), BF16), pltpu.VMEM((d, f), BF16), pltpu.VMEM((f, d), BF16),
                            pltpu.SemaphoreType.DMA((2,)), pltpu.SemaphoreType.DMA((2,)),
                            pltpu.SemaphoreType.DMA((WEIGHT_SLOTS,))]),
        out_shape=jax.ShapeDtypeStruct((n_blocks * MOE_BLOCK, half), PACKED),
        compiler_params=_cparams("arbitrary"),
        name="experts",
    )(items, xs, w_gate, w_up, w_down)


def _sc_gather_rows(table, idx):
    n_idx = idx.shape[0]
    width = table.shape[1]
    info = plsc.get_sparse_core_info()
    n_workers = info.num_cores * info.num_subcores
    per_worker = n_idx // n_workers
    assert per_worker * n_workers == n_idx and per_worker % (SC_GATHER_BUFS * SC_WINDOW) == 0
    mesh = plsc.VectorSubcoreMesh(core_axis_name="c", subcore_axis_name="s")

    def body(table_hbm, idx_hbm, out_hbm, idx_v, rows_v, sem_g, sem_o):
        wid = lax.axis_index("s") * info.num_cores + lax.axis_index("c")
        base = wid * per_worker

        @pl.loop(0, per_worker // (SC_GATHER_BUFS * SC_WINDOW))
        def _(it):
            offs = [base + (it * SC_GATHER_BUFS + b) * SC_WINDOW for b in range(SC_GATHER_BUFS)]
            gathers = []
            for b, off in enumerate(offs):
                pltpu.sync_copy(idx_hbm.at[pl.ds(off, SC_WINDOW)], idx_v.at[b])
                gathers.append(pltpu.async_copy(table_hbm.at[idx_v.at[b]], rows_v.at[b], sem_g.at[b]))
            writes = []
            for b, off in enumerate(offs):
                gathers[b].wait()
                writes.append(pltpu.async_copy(rows_v.at[b], out_hbm.at[pl.ds(off, SC_WINDOW)], sem_o.at[b]))
            for cp in writes:
                cp.wait()

    return pl.kernel(
        body,
        out_type=jax.ShapeDtypeStruct((n_idx, width), table.dtype),
        mesh=mesh,
        scratch_types=[pltpu.VMEM((SC_GATHER_BUFS, SC_WINDOW), I32),
                       pltpu.VMEM((SC_GATHER_BUFS, SC_WINDOW, width), table.dtype),
                       pltpu.SemaphoreType.DMA((SC_GATHER_BUFS,)), pltpu.SemaphoreType.DMA((SC_GATHER_BUFS,))],
        name="sc_gather",
    )(table, idx)


def _combine_kernel(hp_ref, x1_ref, g2_ref, w_ref, sgw_ref, suw_ref, sdw_ref, yg_ref, *rest):
    o_ref = rest[-1]
    xa, xb = _unpack_halves(hp_ref[...])
    xa = xa.astype(BF16)
    xb = xb.astype(BF16)
    half = xa.shape[1]
    tb = xa.shape[0]
    g = _dot(xa, sgw_ref[0:half]) + _dot(xb, sgw_ref[half:])
    u = _dot(xa, suw_ref[0:half]) + _dot(xb, suw_ref[half:])
    shared = _dot((_silu(g) * u).astype(BF16), sdw_ref[...])

    w = w_ref[...]
    acc_a = jnp.zeros((tb, half), F32)
    acc_b = jnp.zeros((tb, half), F32)
    for k in range(TOP_K):
        ya, yb = _unpack_halves(yg_ref[0, k])
        acc_a += ya * w[:, k:k + 1]
        acc_b += yb * w[:, k:k + 1]
    g2 = g2_ref[0]
    o_ref[:, 0:half] = x1_ref[:, 0:half] + g2[:, 0:half] * (acc_a + shared[:, 0:half])
    o_ref[:, half:] = x1_ref[:, half:] + g2[:, half:] * (acc_b + shared[:, half:])


def _combine(hp, x1, g2, w_tok, sgw, suw, sdw, yg, out_prev, *, tb, seq_len, first_step):
    t, half = hp.shape
    d = 2 * half
    per_seq = seq_len // tb
    f = sgw.shape[1]
    s0 = first_step
    in_specs = [pl.BlockSpec((tb, half), lambda i: (s0 + i, 0)),
                pl.BlockSpec((tb, d), lambda i: (s0 + i, 0)),
                pl.BlockSpec((1, 1, d), lambda i: ((s0 + i) // per_seq, 0, 0)),
                pl.BlockSpec((tb, TOP_K), lambda i: (s0 + i, 0)),
                pl.BlockSpec((d, f), lambda i: (0, 0)),
                pl.BlockSpec((d, f), lambda i: (0, 0)),
                pl.BlockSpec((f, d), lambda i: (0, 0)),
                pl.BlockSpec((1, TOP_K, tb, half), lambda i: (i, 0, 0, 0))]
    args = [hp, x1, g2, w_tok, sgw, suw, sdw, yg]
    aliases = {}
    if out_prev is not None:
        in_specs.append(pl.BlockSpec(memory_space=pl.ANY))
        args.append(out_prev)
        aliases = {len(args) - 1: 0}
    return pl.pallas_call(
        _combine_kernel,
        grid=(yg.shape[0],),
        in_specs=in_specs,
        out_specs=pl.BlockSpec((tb, d), lambda i: (s0 + i, 0)),
        out_shape=jax.ShapeDtypeStruct((t, d), F32),
        input_output_aliases=aliases,
        compiler_params=_cparams("arbitrary"),
        name="combine",
    )(*args)


def _rope_tables(l):
    rows = l // GRID_W
    r = jnp.repeat(jnp.arange(rows), GRID_W).astype(F32)
    col = jnp.tile(jnp.arange(GRID_W), rows).astype(F32)
    n_f = HEAD_DIM // 4
    freqs = ROPE_THETA ** (-jnp.arange(n_f, dtype=F32) / n_f)
    ang = jnp.concatenate([r[:, None] * freqs, col[:, None] * freqs], axis=-1)
    ang = jnp.tile(jnp.repeat(ang, 2, axis=1), (1, LANES // HEAD_DIM))
    sign = jnp.where(jnp.arange(LANES) % 2 == 0, -1.0, 1.0).astype(F32)
    return jnp.cos(ang), jnp.sin(ang) * sign


def kernel(x, c, ctx, c_ctx, w_mod, b_mod, norm1_w, norm2_w, w_in, q_norm_w, k_norm_w, ret_decay_fwd,
           ret_decay_bwd, w_out, router_w, router_bias, exp_w_gate, exp_w_up, exp_w_down, sh_w_gate,
           sh_w_up, sh_w_down):
    b, l, d = x.shape
    lc = ctx.shape[1]
    t = b * l
    assert w_mod.shape[0] == 1, "single layer"
    assert l % CHUNK == 0 and lc % CHUNK == 0 and l % GRID_W == 0

    rows = (b + 1 + 7) // 8 * 8
    cc = jnp.zeros((rows, d), F32).at[:b].set(c).at[b].set(c_ctx)
    mod = _modulation(cc, w_mod[0], b_mod[0])
    sh1, sc1, g1, sh2, sc2, g2 = [mod[:b, i * d:(i + 1) * d].reshape(b, 1, d) for i in range(6)]
    shc = mod[b, 0:d].reshape(1, 1, d)
    scc = mod[b, d:2 * d].reshape(1, 1, d)

    wi = w_in[0].astype(BF16)
    qnw = jnp.tile(q_norm_w[0], LANES // HEAD_DIM).reshape(1, LANES)
    knw = jnp.tile(k_norm_w[0], LANES // HEAD_DIM).reshape(1, LANES)
    cos, sin = _rope_tables(l)
    n1 = norm1_w[0].reshape(1, d)

    cklo, ckhi, cvlo, cvhi, crk, crv = _projection(
        ctx, shc, scc, n1, wi, qnw, knw, cos[:lc], sin[:lc], rope=False, with_q=False, tm=TILE_PROJ)
    klo, khi, vlo, vhi, rk, rv, q, rq, sg = _projection(
        x, sh1, sc1, n1, wi, qnw, knw, cos, sin, rope=True, with_q=True, tm=TILE_PROJ)

    attn = _attention(q, klo, khi, vlo, vhi, cklo, ckhi, cvlo, cvhi, tq=TILE_ATTN_Q)
    dec_f = jnp.repeat(ret_decay_fwd[0].astype(F32), HEAD_DIM).reshape(1, RET_W)
    dec_b = jnp.repeat(ret_decay_bwd[0].astype(F32), HEAD_DIM).reshape(1, RET_W)
    ret = _retention(rq, rk, rv, sg, crk, crv, dec_f, dec_b)

    wo = w_out[0].astype(BF16)
    r_hi, r_lo = _split(router_w[0].T)
    x1, hp, logits_t = _out_projection(attn, ret, x, wo[:ATTN_W], wo[ATTN_W:], g1, sh2, sc2,
                                       norm2_w[0].reshape(1, d), r_hi, r_lo, tm=TILE_OUT)

    idx_t, w_t, rank_t, cnt_col, cnt_row = _route(logits_t, router_bias[0].reshape(N_EXPERTS, 1), tb=TILE_TOKENS)
    n_blocks = -(-(t * TOP_K) // MOE_BLOCK) + N_EXPERTS
    tb = TILE_TOKENS
    dest, meta, items = _destinations(idx_t, rank_t, cnt_col, cnt_row, tb=tb, n_blocks=n_blocks)
    steps, _, tbe = dest.shape
    dest_win = dest.reshape(steps, TOP_K, tbe // SC_WINDOW, SC_WINDOW).transpose(0, 2, 1, 3)
    dest_win = dest_win.reshape(t // SC_WINDOW, TOP_K, SC_WINDOW)
    xs = _sc_scatter_rows(hp, dest_win, (n_blocks + ITEM_BLOCKS - 1) * MOE_BLOCK)
    xs = _pad_fill(meta, xs)
    ys = _experts(items, xs, exp_w_gate[0], exp_w_up[0], exp_w_down[0], n_blocks=n_blocks)
    parts = COMBINE_PARTS if steps % COMBINE_PARTS == 0 else 1
    steps_part = steps // parts
    x1f, w_tok = x1.reshape(t, d), w_t.T
    sgw, suw, sdw = sh_w_gate[0].astype(BF16), sh_w_up[0].astype(BF16), sh_w_down[0].astype(BF16)
    out = None
    for p in range(parts):
        idx = dest[p * steps_part:(p + 1) * steps_part].reshape(-1)
        yg = _sc_gather_rows(ys, idx).reshape(steps_part, TOP_K, tbe, d // 2)
        out = _combine(hp, x1f, g2, w_tok, sgw, suw, sdw, yg, out, tb=tbe, seq_len=l, first_step=p * steps_part)
    return out.reshape(b, l, d)
```

```python
import functools

import jax
import jax.numpy as jnp
from jax import lax
from jax.experimental import pallas as pl
from jax.experimental.pallas import tpu as pltpu
from jax.experimental.pallas import tpu_sc as plsc

F32 = jnp.float32
BF16 = jnp.bfloat16
I32 = jnp.int32
U32 = jnp.uint32
PACKED = jnp.int32

HEAD_DIM = 64
LANES = 128
SUBLANES = 8
ATTN_HEADS = 8
ATTN_KV_HEADS = 2
GQA = ATTN_HEADS // ATTN_KV_HEADS
RET_HEADS = 8
ATTN_W = ATTN_HEADS * HEAD_DIM
KV_W = ATTN_KV_HEADS * HEAD_DIM
RET_W = RET_HEADS * HEAD_DIM
RET_PAIRS = RET_W // LANES
CHUNK = 128
GRID_W = 64
ROPE_THETA = 10000.0
N_EXPERTS = 256
TOP_K = 8
N_GROUPS = 8
GROUP_SIZE = N_EXPERTS // N_GROUPS
TOPK_GROUPS = 4
ROUTED_SCALE = 2.5
MOE_BLOCK = 128
EPS = 1e-6
QK_SCALE = HEAD_DIM ** -0.5
LOG2_E = 1.4426950408889634

OFF_AK = 0
OFF_AV = OFF_AK + KV_W
OFF_RK = OFF_AV + KV_W
OFF_RV = OFF_RK + RET_W
CTX_KV_COLS = OFF_RV + RET_W
OFF_AQ = CTX_KV_COLS
OFF_RQ = OFF_AQ + ATTN_W
OFF_RG = OFF_RQ + RET_W
IN_COLS = OFF_RG + RET_W

VMEM_LIMIT = 52 * 1024 * 1024

SUB_ROWS = 256
TILE_PROJ = 1024
TILE_ATTN_Q = 512
TILE_OUT = 512
TILE_TOKENS = 256
ITEM_BLOCKS = 5
COMBINE_PARTS = 4
SC_GATHER_BUFS = 2
WEIGHT_SLOTS = 3
SC_WINDOW = 64
HI_MASK = 0xFFFF0000


def _cparams(*sem):
    return pltpu.CompilerParams(dimension_semantics=sem, vmem_limit_bytes=VMEM_LIMIT)


def _split(a):
    hi = a.astype(BF16)
    lo = (a - hi.astype(F32)).astype(BF16)
    return hi, lo


def _dot(a, b):
    return jnp.dot(a, b, preferred_element_type=F32)


def _dot_nt(a, b):
    return lax.dot_general(a, b, (((1,), (1,)), ((), ())), preferred_element_type=F32)


def _sigmoid(v):
    return 1.0 / (1.0 + jnp.exp(-v))


def _silu(v):
    return v * _sigmoid(v)


def _pack_halves(a, b):
    ua = lax.bitcast_convert_type(a.astype(BF16).astype(F32), U32)
    ub = lax.bitcast_convert_type(b.astype(BF16).astype(F32), U32)
    return lax.bitcast_convert_type((ua & jnp.uint32(HI_MASK)) | (ub >> 16), PACKED)


def _unpack_halves(p):
    u = lax.bitcast_convert_type(p, U32)
    a = lax.bitcast_convert_type(u & jnp.uint32(HI_MASK), F32)
    b = lax.bitcast_convert_type(u << 16, F32)
    return a, b


def _mod_kernel(c_ref, w_ref, b_ref, o_ref):
    s_hi, s_lo = _split(_silu(c_ref[...]))
    w_hi, w_lo = _split(w_ref[...])
    o_ref[...] = _dot(s_hi, w_hi) + _dot(s_hi, w_lo) + _dot(s_lo, w_hi) + b_ref[...]


def _modulation(cc, w_mod, b_mod):
    rows, d = cc.shape
    n = w_mod.shape[1]
    tn = 768
    return pl.pallas_call(
        _mod_kernel,
        grid=(n // tn,),
        in_specs=[pl.BlockSpec((rows, d), lambda j: (0, 0)),
                  pl.BlockSpec((d, tn), lambda j: (0, j)),
                  pl.BlockSpec((1, tn), lambda j: (0, j))],
        out_specs=pl.BlockSpec((rows, tn), lambda j: (0, j)),
        out_shape=jax.ShapeDtypeStruct((rows, n), F32),
        compiler_params=_cparams("arbitrary"),
        name="mod",
    )(cc, w_mod, b_mod.reshape(1, n))


def _segment_ones():
    r = lax.broadcasted_iota(I32, (LANES, LANES), 0) // HEAD_DIM
    c = lax.broadcasted_iota(I32, (LANES, LANES), 1) // HEAD_DIM
    return jnp.where(r == c, 1.0, 0.0).astype(BF16)


def _head_mean_sq(v, seg):
    hi, lo = _split(v * v)
    return (_dot(hi, seg) + _dot(lo, seg)) * (1.0 / HEAD_DIM)


def _proj_kernel(x_ref, sh_ref, sc_ref, nw_ref, wi_ref, qnw_ref, knw_ref, cos_ref, sin_ref,
                 *out_refs, rope, with_q):
    if with_q:
        klo_ref, khi_ref, vlo_ref, vhi_ref, rk_ref, rv_ref, q_ref, rq_ref, sg_ref = out_refs
    else:
        klo_ref, khi_ref, vlo_ref, vhi_ref, rk_ref, rv_ref = out_refs
    tm = x_ref.shape[1]
    sub = min(tm, SUB_ROWS)
    seg = _segment_ones()
    lane = lax.broadcasted_iota(I32, (sub, LANES), 1)
    low_half = lane < HEAD_DIM
    even = (lane & 1) == 0

    for r0 in range(0, tm, sub):
        rows = slice(r0, r0 + sub)
        x = x_ref[0, rows, :]
        h = x * lax.rsqrt(jnp.mean(x * x, axis=-1, keepdims=True) + EPS) * nw_ref[...]
        h = h * (1.0 + sc_ref[0]) + sh_ref[0]
        z = _dot(h.astype(BF16), wi_ref[...])

        def norm_rope(v, w128, rows=rows):
            v = v * lax.rsqrt(_head_mean_sq(v, seg) + EPS) * w128
            if rope:
                swapped = jnp.where(even, pltpu.roll(v, LANES - 1, 1), pltpu.roll(v, 1, 1))
                v = v * cos_ref[rows, :] + swapped * sin_ref[rows, :]
            return v

        k = norm_rope(z[:, OFF_AK:OFF_AK + KV_W], knw_ref[...])
        ksw = pltpu.roll(k, HEAD_DIM, 1)
        klo_ref[0, 0, :, rows] = jnp.transpose(jnp.where(low_half, k, 0.0)).astype(BF16)
        khi_ref[0, 0, :, rows] = jnp.transpose(jnp.where(low_half, 0.0, ksw)).astype(BF16)
        klo_ref[0, 1, :, rows] = jnp.transpose(jnp.where(low_half, ksw, 0.0)).astype(BF16)
        khi_ref[0, 1, :, rows] = jnp.transpose(jnp.where(low_half, 0.0, k)).astype(BF16)
        v = z[:, OFF_AV:OFF_AV + KV_W]
        vsw = pltpu.roll(v, HEAD_DIM, 1)
        vlo_ref[0, 0, rows, :] = jnp.where(low_half, v, 0.0).astype(BF16)
        vhi_ref[0, 0, rows, :] = jnp.where(low_half, 0.0, vsw).astype(BF16)
        vlo_ref[0, 1, rows, :] = jnp.where(low_half, vsw, 0.0).astype(BF16)
        vhi_ref[0, 1, rows, :] = jnp.where(low_half, 0.0, v).astype(BF16)
        rk_ref[0, rows, :] = (z[:, OFF_RK:OFF_RK + RET_W] * QK_SCALE).astype(BF16)
        rv_ref[0, rows, :] = z[:, OFF_RV:OFF_RV + RET_W].astype(BF16)
        if with_q:
            for j in range(ATTN_W // LANES):
                qj = norm_rope(z[:, OFF_AQ + j * LANES:OFF_AQ + (j + 1) * LANES], qnw_ref[...])
                q_ref[0, rows, j * LANES:(j + 1) * LANES] = (qj * (QK_SCALE * LOG2_E)).astype(BF16)
            rq_ref[0, rows, :] = z[:, OFF_RQ:OFF_RQ + RET_W].astype(BF16)
            sg_ref[0, rows, :] = _silu(z[:, OFF_RG:OFF_RG + RET_W]).astype(BF16)


def _projection(x, shift, scale, norm_w, wi_bf16, qnw, knw, cos, sin, *, rope, with_q, tm):
    b, l, d = x.shape
    tm = min(tm, l)
    ncols = IN_COLS if with_q else CTX_KV_COLS
    per_batch = shift.shape[0] > 1
    mod_idx = (lambda bi, i: (bi, 0, 0)) if per_batch else (lambda bi, i: (0, 0, 0))
    kv_shape = jax.ShapeDtypeStruct((b, ATTN_KV_HEADS, l, LANES), BF16)
    kv_spec = pl.BlockSpec((1, ATTN_KV_HEADS, tm, LANES), lambda bi, i: (bi, 0, i, 0))
    kt_shape = jax.ShapeDtypeStruct((b, ATTN_KV_HEADS, LANES, l), BF16)
    kt_spec = pl.BlockSpec((1, ATTN_KV_HEADS, LANES, tm), lambda bi, i: (bi, 0, 0, i))
    w_shape = jax.ShapeDtypeStruct((b, l, RET_W), BF16)
    w_spec = pl.BlockSpec((1, tm, RET_W), lambda bi, i: (bi, i, 0))
    out_shape = [kt_shape] * 2 + [kv_shape] * 2 + [w_shape] * 2
    out_specs = [kt_spec] * 2 + [kv_spec] * 2 + [w_spec] * 2
    if with_q:
        out_shape += [w_shape] * 3
        out_specs += [w_spec] * 3
    return pl.pallas_call(
        functools.partial(_proj_kernel, rope=rope, with_q=with_q),
        grid=(b, l // tm),
        in_specs=[pl.BlockSpec((1, tm, d), lambda bi, i: (bi, i, 0)),
                  pl.BlockSpec((1, 1, d), mod_idx),
                  pl.BlockSpec((1, 1, d), mod_idx),
                  pl.BlockSpec((1, d), lambda bi, i: (0, 0)),
                  pl.BlockSpec((d, ncols), lambda bi, i: (0, 0)),
                  pl.BlockSpec((1, LANES), lambda bi, i: (0, 0)),
                  pl.BlockSpec((1, LANES), lambda bi, i: (0, 0)),
                  pl.BlockSpec((tm, LANES), lambda bi, i: (i, 0)),
                  pl.BlockSpec((tm, LANES), lambda bi, i: (i, 0))],
        out_specs=out_specs,
        out_shape=out_shape,
        compiler_params=_cparams("arbitrary", "arbitrary"),
        name="proj_latent" if with_q else "proj_ctx",
    )(x, shift, scale, norm_w, wi_bf16, qnw, knw, cos, sin)


def _attn_kernel(q_ref, klo_ref, khi_ref, vlo_ref, vhi_ref, cklo_ref, ckhi_ref, cvlo_ref, cvhi_ref,
                 o_ref, kl_s, kh_s, va_s, *, l, lc):
    lk = l + lc

    @pl.when(pl.program_id(2) == 0)
    def _():
        kl_s[:, 0:l] = klo_ref[0, 0]
        kl_s[:, l:lk] = cklo_ref[0, 0]
        kh_s[:, 0:l] = khi_ref[0, 0]
        kh_s[:, l:lk] = ckhi_ref[0, 0]
        lane = lax.broadcasted_iota(I32, (lk, LANES), 1)
        ones_lo = jnp.where(lane < HEAD_DIM, 1.0, 0.0).astype(BF16)
        ones_hi = jnp.where(lane < HEAD_DIM, 0.0, 1.0).astype(BF16)
        for g, (v_ref, cv_ref, ones) in enumerate(((vlo_ref, cvlo_ref, ones_lo), (vhi_ref, cvhi_ref, ones_hi),
                                                   (vlo_ref, cvlo_ref, ones_lo), (vhi_ref, cvhi_ref, ones_hi))):
            v_col, one_col = (0, LANES) if g < 2 else (LANES, 0)
            va_s[g, 0:l, v_col:v_col + LANES] = v_ref[0, 0]
            va_s[g, l:lk, v_col:v_col + LANES] = cv_ref[0, 0]
            va_s[g, :, one_col:one_col + LANES] = ones

    q = q_ref[0]
    acc = []
    for g in range(GQA):
        qp = q[:, (g // 2) * LANES:(g // 2 + 1) * LANES]
        kt = kl_s[...] if g % 2 == 0 else kh_s[...]
        s = _dot(qp, kt)
        p = jnp.exp2(s - jnp.max(s, axis=-1, keepdims=True)).astype(BF16)
        acc.append(_dot(p, va_s[g]))
    out_a = acc[0] + acc[1]
    out_b = acc[2] + acc[3]
    o_ref[0, :, 0:LANES] = (out_a[:, 0:LANES] / out_a[:, LANES:2 * LANES]).astype(BF16)
    o_ref[0, :, LANES:2 * LANES] = (out_b[:, LANES:2 * LANES] / out_b[:, 0:LANES]).astype(BF16)


def _attention(q, klo, khi, vlo, vhi, cklo, ckhi, cvlo, cvhi, *, tq):
    b, l, _ = q.shape
    lc = cvlo.shape[2]
    lk = l + lc
    tq = min(tq, l)
    gw = GQA * HEAD_DIM
    kt_spec = pl.BlockSpec((1, 1, LANES, l), lambda bi, h, i: (bi, h, 0, 0))
    kv_spec = pl.BlockSpec((1, 1, l, LANES), lambda bi, h, i: (bi, h, 0, 0))
    ckt_spec = pl.BlockSpec((1, 1, LANES, lc), lambda bi, h, i: (bi, h, 0, 0))
    ckv_spec = pl.BlockSpec((1, 1, lc, LANES), lambda bi, h, i: (bi, h, 0, 0))
    return pl.pallas_call(
        functools.partial(_attn_kernel, l=l, lc=lc),
        grid=(b, ATTN_KV_HEADS, l // tq),
        in_specs=([pl.BlockSpec((1, tq, gw), lambda bi, h, i: (bi, i, h))] + [kt_spec] * 2 + [kv_spec] * 2
                  + [ckt_spec] * 2 + [ckv_spec] * 2),
        out_specs=pl.BlockSpec((1, tq, gw), lambda bi, h, i: (bi, i, h)),
        out_shape=jax.ShapeDtypeStruct((b, l, ATTN_W), BF16),
        scratch_shapes=[pltpu.VMEM((LANES, lk), BF16), pltpu.VMEM((LANES, lk), BF16),
                        pltpu.VMEM((GQA, lk, 2 * LANES), BF16)],
        compiler_params=_cparams("arbitrary", "arbitrary", "arbitrary"),
        name="attn",
    )(q, klo, khi, vlo, vhi, cklo, ckhi, cvlo, cvhi)


def _log_sigmoid(v):
    return jnp.minimum(v, 0.0) - jnp.log(1.0 + jnp.exp(-jnp.abs(v)))


def _ret_kernel(rq_ref, rk_ref, rv_ref, sg_ref, crk_ref, crv_ref, df_ref, db_ref, o_ref,
                m_s, xi_s, zeta_s, kv_s, st_s, *, l, lc):
    n = l // CHUNK
    nc = lc // CHUNK
    lgf = _log_sigmoid(df_ref[...])
    lgb = _log_sigmoid(db_ref[...])
    pos = lax.broadcasted_iota(I32, (CHUNK, LANES), 0).astype(F32)
    row = lax.broadcasted_iota(I32, (CHUNK, CHUNK), 0)
    col = lax.broadcasted_iota(I32, (CHUNK, CHUNK), 1)
    diff = (row - col).astype(F32)
    g_chunk = []
    for p in range(RET_PAIRS):
        cols = slice(p * LANES, (p + 1) * LANES)
        lf, lb = lgf[:, cols], lgb[:, cols]
        xi_s[p, :, 0:LANES] = jnp.exp((pos + 1.0) * lf)
        xi_s[p, :, LANES:] = jnp.exp((CHUNK - pos) * lb)
        zeta_s[p, :, 0:LANES] = jnp.exp((CHUNK - 1.0 - pos) * lf)
        zeta_s[p, :, LANES:] = jnp.exp(pos * lb)
        g_chunk.append((jnp.exp(CHUNK * lf), jnp.exp(CHUNK * lb)))
        for j in range(2):
            h = 2 * p + j
            hf = lgf[:, h * HEAD_DIM:h * HEAD_DIM + 1]
            hb = lgb[:, h * HEAD_DIM:h * HEAD_DIM + 1]
            m_s[p, :, j * CHUNK:(j + 1) * CHUNK] = jnp.where(
                diff > 0, jnp.exp(diff * hf), jnp.where(diff < 0, jnp.exp(-diff * hb), 2.0))

    lane = lax.broadcasted_iota(I32, (CHUNK, LANES), 1)
    low_half = lane < HEAD_DIM
    diag = (lax.broadcasted_iota(I32, (LANES, LANES), 0) // HEAD_DIM
            == lax.broadcasted_iota(I32, (LANES, LANES), 1) // HEAD_DIM)
    seg = jnp.where(diag, 1.0, 0.0).astype(BF16)
    seg2 = jnp.concatenate([seg, seg], axis=0)
    diag2 = jnp.concatenate([diag, diag], axis=0)

    def split_heads(a):
        zero = jnp.zeros_like(a)
        return jnp.concatenate([jnp.where(low_half, a, zero), jnp.where(low_half, zero, a)], axis=0)

    def contrib(k_ref, v_ref, r0, p):
        cols = slice(p * LANES, (p + 1) * LANES)
        kp = k_ref[0, pl.ds(r0, CHUNK), cols].astype(F32)
        kz = jnp.concatenate([kp, kp], axis=1) * zeta_s[p]
        kv = _dot(jnp.transpose(kz).astype(BF16), v_ref[0, pl.ds(r0, CHUNK), cols])
        return jnp.where(diag2, kv, 0.0)

    for c in range(nc):
        for p in range(RET_PAIRS):
            kv_s[c, p] = contrib(crk_ref, crv_ref, c * CHUNK, p)

    def contrib_body(c, carry):
        r0 = pl.multiple_of(c * CHUNK, CHUNK)
        for p in range(RET_PAIRS):
            kv_s[nc + c, p] = contrib(rk_ref, rv_ref, r0, p)
        return carry

    lax.fori_loop(0, n, contrib_body, 0, unroll=4)

    for p in range(RET_PAIRS):
        gf, gb = g_chunk[p]
        sf = jnp.zeros((LANES, LANES), F32)
        sb = jnp.zeros((LANES, LANES), F32)
        for c in range(nc):
            sf = gf * sf + kv_s[c, p, 0:LANES]
            sb = gb * sb + kv_s[nc - 1 - c, p, LANES:]

        def fwd_scan(c, s, p=p, gf=gf):
            st_s[c, p, 0:LANES] = s.astype(BF16)
            return gf * s + kv_s[nc + c, p, 0:LANES]

        def bwd_scan(j, s, p=p, gb=gb):
            c = n - 1 - j
            st_s[c, p, LANES:] = s.astype(BF16)
            return gb * s + kv_s[nc + c, p, LANES:]

        lax.fori_loop(0, n, fwd_scan, sf)
        lax.fori_loop(0, n, bwd_scan, sb)

    def out_body(c, carry):
        r0 = pl.multiple_of(c * CHUNK, CHUNK)
        for p in range(RET_PAIRS):
            cols = slice(p * LANES, (p + 1) * LANES)
            qp = rq_ref[0, pl.ds(r0, CHUNK), cols]
            kp = rk_ref[0, pl.ds(r0, CHUNK), cols]
            vp = rv_ref[0, pl.ds(r0, CHUNK), cols]
            s2 = _dot_nt(qp, split_heads(kp))
            a2 = (s2 * m_s[p]).astype(BF16)
            y = _dot(a2, split_heads(vp))
            qf = qp.astype(F32)
            qx = (jnp.concatenate([qf, qf], axis=1) * xi_s[p]).astype(BF16)
            y += _dot(qx, st_s[c, p])
            hi, lo = _split(y * y)
            ms = _dot(jnp.concatenate([hi, lo], axis=1), seg2) * (1.0 / HEAD_DIM)
            out = y * lax.rsqrt(ms + EPS) * sg_ref[0, pl.ds(r0, CHUNK), cols].astype(F32)
            o_ref[0, pl.ds(r0, CHUNK), cols] = out.astype(BF16)
        return carry

    lax.fori_loop(0, n, out_body, 0, unroll=4)


def _retention(rq, rk, rv, sg, crk, crv, dec_f, dec_b):
    b, l, _ = rq.shape
    lc = crk.shape[1]
    n = l // CHUNK
    nc = lc // CHUNK
    spec = pl.BlockSpec((1, l, RET_W), lambda bi: (bi, 0, 0))
    cspec = pl.BlockSpec((1, lc, RET_W), lambda bi: (bi, 0, 0))
    dspec = pl.BlockSpec((1, RET_W), lambda bi: (0, 0))
    return pl.pallas_call(
        functools.partial(_ret_kernel, l=l, lc=lc),
        grid=(b,),
        in_specs=[spec, spec, spec, spec, cspec, cspec, dspec, dspec],
        out_specs=spec,
        out_shape=jax.ShapeDtypeStruct((b, l, RET_W), BF16),
        scratch_shapes=[pltpu.VMEM((RET_PAIRS, CHUNK, 2 * CHUNK), F32),
                        pltpu.VMEM((RET_PAIRS, CHUNK, 2 * LANES), F32),
                        pltpu.VMEM((RET_PAIRS, CHUNK, 2 * LANES), F32),
                        pltpu.VMEM((nc + n, RET_PAIRS, 2 * LANES, LANES), F32),
                        pltpu.VMEM((n, RET_PAIRS, 2 * LANES, LANES), BF16)],
        compiler_params=_cparams("arbitrary"),
        name="ret",
    )(rq, rk, rv, sg, crk, crv, dec_f, dec_b)


def _out_kernel(attn_ref, ret_ref, x_ref, wa_ref, wr_ref, g1_ref, sh_ref, sc_ref, nw_ref, rhi_ref, rlo_ref,
                x1_ref, hp_ref, lg_ref):
    y = _dot(attn_ref[0], wa_ref[...]) + _dot(ret_ref[0], wr_ref[...])
    x1 = x_ref[0] + g1_ref[0] * y
    x1_ref[0] = x1
    h = x1 * lax.rsqrt(jnp.mean(x1 * x1, axis=-1, keepdims=True) + EPS) * nw_ref[...]
    h = h * (1.0 + sc_ref[0]) + sh_ref[0]
    half = h.shape[1] // 2
    hp_ref[...] = _pack_halves(h[:, :half], h[:, half:])
    h_hi, h_lo = _split(h)
    lg_ref[...] = _dot_nt(rhi_ref[...], h_hi) + _dot_nt(rhi_ref[...], h_lo) + _dot_nt(rlo_ref[...], h_hi)


def _out_projection(attn, ret, x, wa, wr, g1, sh2, sc2, norm_w, r_hi, r_lo, *, tm):
    b, l, d = x.shape
    tm = min(tm, l)
    nt = l // tm
    t = b * l
    mspec = pl.BlockSpec((1, 1, d), lambda bi, i: (bi, 0, 0))
    return pl.pallas_call(
        _out_kernel,
        grid=(b, nt),
        in_specs=[pl.BlockSpec((1, tm, ATTN_W), lambda bi, i: (bi, i, 0)),
                  pl.BlockSpec((1, tm, RET_W), lambda bi, i: (bi, i, 0)),
                  pl.BlockSpec((1, tm, d), lambda bi, i: (bi, i, 0)),
                  pl.BlockSpec((ATTN_W, d), lambda bi, i: (0, 0)),
                  pl.BlockSpec((RET_W, d), lambda bi, i: (0, 0)),
                  mspec, mspec, mspec,
                  pl.BlockSpec((1, d), lambda bi, i: (0, 0)),
                  pl.BlockSpec((N_EXPERTS, d), lambda bi, i: (0, 0)),
                  pl.BlockSpec((N_EXPERTS, d), lambda bi, i: (0, 0))],
        out_specs=[pl.BlockSpec((1, tm, d), lambda bi, i: (bi, i, 0)),
                   pl.BlockSpec((tm, d // 2), lambda bi, i: (bi * nt + i, 0)),
                   pl.BlockSpec((N_EXPERTS, tm), lambda bi, i: (0, bi * nt + i))],
        out_shape=[jax.ShapeDtypeStruct((b, l, d), F32),
                   jax.ShapeDtypeStruct((t, d // 2), PACKED),
                   jax.ShapeDtypeStruct((N_EXPERTS, t), F32)],
        compiler_params=_cparams("arbitrary", "arbitrary"),
        name="out_proj",
    )(attn, ret, x, wa, wr, g1, sh2, sc2, norm_w, r_hi, r_lo)


def _route_kernel(lg_ref, bias_ref, idx_ref, w_ref, rank_ref, cnt_col_ref, cnt_row_ref, tri_s, col_s, row_s):
    tb = lg_ref.shape[1]
    step = pl.program_id(0)

    @pl.when(step == 0)
    def _():
        r = lax.broadcasted_iota(I32, (tb, tb), 0)
        c = lax.broadcasted_iota(I32, (tb, tb), 1)
        tri_s[...] = jnp.where(r <= c, 1.0, 0.0).astype(BF16)
        col_s[...] = jnp.zeros_like(col_s)
        row_s[...] = jnp.zeros_like(row_s)

    scores = _sigmoid(lg_ref[...])
    biased = scores + bias_ref[...]
    neg = -jnp.inf
    sub = lax.broadcasted_iota(I32, (GROUP_SIZE, tb), 0).astype(F32)

    gscore = []
    for g in range(N_GROUPS):
        blk = biased[g * GROUP_SIZE:(g + 1) * GROUP_SIZE]
        m1 = jnp.max(blk, axis=0, keepdims=True)
        first = jnp.min(jnp.where(blk == m1, sub, float(GROUP_SIZE)), axis=0, keepdims=True)
        m2 = jnp.max(jnp.where(sub == first, neg, blk), axis=0, keepdims=True)
        gscore.append(m1 + m2)
    gs = jnp.concatenate(gscore, axis=0)
    gsub = lax.broadcasted_iota(I32, (N_GROUPS, tb), 0).astype(F32)
    keep = jnp.zeros((N_GROUPS, tb), F32)
    for _ in range(TOPK_GROUPS):
        m = jnp.max(gs, axis=0, keepdims=True)
        first = jnp.min(jnp.where(gs == m, gsub, float(N_GROUPS)), axis=0, keepdims=True)
        sel = gsub == first
        keep = jnp.where(sel, 1.0, keep)
        gs = jnp.where(sel, neg, gs)
    masked = jnp.concatenate(
        [jnp.where(keep[g:g + 1] > 0.0, biased[g * GROUP_SIZE:(g + 1) * GROUP_SIZE], neg)
         for g in range(N_GROUPS)], axis=0)

    esub = lax.broadcasted_iota(I32, (N_EXPERTS, tb), 0).astype(F32)
    sels, idxs, ws = [], [], []
    chosen = jnp.zeros((N_EXPERTS, tb), F32)
    for _ in range(TOP_K):
        m = jnp.max(masked, axis=0, keepdims=True)
        first = jnp.min(jnp.where(masked == m, esub, float(N_EXPERTS)), axis=0, keepdims=True)
        sel = esub == first
        sels.append(sel)
        idxs.append(first)
        ws.append(jnp.sum(jnp.where(sel, scores, 0.0), axis=0, keepdims=True))
        chosen = jnp.where(sel, 1.0, chosen)
        masked = jnp.where(sel, neg, masked)
    wsum = ws[0]
    for k in range(1, TOP_K):
        wsum = wsum + ws[k]
    idx_ref[...] = jnp.concatenate(idxs, axis=0).astype(I32)
    w_ref[...] = jnp.concatenate([wk / wsum * ROUTED_SCALE for wk in ws], axis=0)

    chosen_b = chosen.astype(BF16)
    incl = _dot(chosen_b, tri_s[...])
    before = incl - chosen + col_s[...]
    rank_ref[...] = jnp.concatenate(
        [jnp.sum(jnp.where(sel, before, 0.0), axis=0, keepdims=True) for sel in sels], axis=0).astype(I32)
    col_s[...] = col_s[...] + incl[:, tb - 1:tb]
    row_s[...] = row_s[...] + _dot_nt(jnp.ones((8, tb), BF16), chosen_b)
    cnt_col_ref[...] = col_s[...].astype(I32)
    cnt_row_ref[...] = row_s[...].astype(I32)


def _route(logits_t, bias_col, *, tb):
    e, t = logits_t.shape
    tb = min(tb, t)
    kspec = pl.BlockSpec((TOP_K, tb), lambda i: (0, i))
    return pl.pallas_call(
        _route_kernel,
        grid=(t // tb,),
        in_specs=[pl.BlockSpec((e, tb), lambda i: (0, i)),
                  pl.BlockSpec((e, 1), lambda i: (0, 0))],
        out_specs=[kspec, kspec, kspec,
                   pl.BlockSpec((e, 1), lambda i: (0, 0)),
                   pl.BlockSpec((8, e), lambda i: (0, 0))],
        out_shape=[jax.ShapeDtypeStruct((TOP_K, t), I32),
                   jax.ShapeDtypeStruct((TOP_K, t), F32),
                   jax.ShapeDtypeStruct((TOP_K, t), I32),
                   jax.ShapeDtypeStruct((e, 1), I32),
                   jax.ShapeDtypeStruct((8, e), I32)],
        scratch_shapes=[pltpu.VMEM((tb, tb), BF16), pltpu.VMEM((e, 1), F32), pltpu.VMEM((8, e), F32)],
        compiler_params=_cparams("arbitrary"),
        name="route",
    )(logits_t, bias_col)


def _pad_block(cnt):
    return (cnt + (MOE_BLOCK - 1)) // MOE_BLOCK * MOE_BLOCK


def _max_items(n_blocks):
    return n_blocks // ITEM_BLOCKS + N_EXPERTS


def _dest_kernel(idx_ref, rank_ref, cnt_col_ref, cnt_row_ref, dest_ref, meta_ref, items_ref, start_s):
    tb = idx_ref.shape[1]
    nip = items_ref.shape[1]

    @pl.when(pl.program_id(0) == 0)
    def _():
        pad_col = _pad_block(cnt_col_ref[...])
        pad_row = _pad_block(cnt_row_ref[0:1, :])
        er = lax.broadcasted_iota(I32, (N_EXPERTS, N_EXPERTS), 0)
        ec = lax.broadcasted_iota(I32, (N_EXPERTS, N_EXPERTS), 1)
        start_col = jnp.sum(jnp.where(ec < er, pad_row, 0), axis=1, keepdims=True)
        start_row = jnp.sum(jnp.where(er < ec, pad_col, 0), axis=0, keepdims=True)
        start_s[...] = start_col

        used = jnp.sum(pad_row, axis=1, keepdims=True) // MOE_BLOCK
        meta_ref[...] = jnp.concatenate(
            [cnt_row_ref[0:1, :], start_row, pad_row, jnp.broadcast_to(used, (1, N_EXPERTS)),
             jnp.zeros((4, N_EXPERTS), I32)], axis=0)

        nb_col = pad_col // MOE_BLOCK
        it_col = (nb_col + (ITEM_BLOCKS - 1)) // ITEM_BLOCKS
        it_row = (pad_row // MOE_BLOCK + (ITEM_BLOCKS - 1)) // ITEM_BLOCKS
        it_start = jnp.sum(jnp.where(ec < er, it_row, 0), axis=1, keepdims=True)
        n_items = jnp.sum(it_row, axis=1, keepdims=True)
        lane = lax.broadcasted_iota(I32, (1, nip), 1)
        owner = jnp.sum(jnp.where(it_start + it_col <= lane, 1, 0), axis=0, keepdims=True)
        owner = jnp.minimum(owner, N_EXPERTS - 1)
        onehot = lax.broadcasted_iota(I32, (N_EXPERTS, nip), 0) == owner

        def pick(col):
            return jnp.sum(jnp.where(onehot, col, 0), axis=0, keepdims=True)

        j = lane - pick(it_start)
        block0 = pick(start_col) // MOE_BLOCK + ITEM_BLOCKS * j
        nvalid = jnp.clip(pick(nb_col) - ITEM_BLOCKS * j, 0, ITEM_BLOCKS)
        items_ref[...] = jnp.concatenate(
            [owner, block0, jnp.where(lane < n_items, nvalid, 0), jnp.broadcast_to(n_items, (1, nip)),
             jnp.zeros((4, nip), I32)], axis=0)

    start_col = start_s[...]
    esub = lax.broadcasted_iota(I32, (N_EXPERTS, tb), 0)
    rows = []
    for k in range(TOP_K):
        onehot = esub == idx_ref[k:k + 1, :]
        rows.append(jnp.sum(jnp.where(onehot, start_col, 0), axis=0, keepdims=True) + rank_ref[k:k + 1, :])
    dest_ref[0] = jnp.concatenate(rows, axis=0)


def _destinations(idx_t, rank_t, cnt_col, cnt_row, *, tb, n_blocks):
    _, t = idx_t.shape
    tb = min(tb, t)
    nip = (_max_items(n_blocks) + LANES - 1) // LANES * LANES
    kspec = pl.BlockSpec((TOP_K, tb), lambda i: (0, i))
    return pl.pallas_call(
        _dest_kernel,
        grid=(t // tb,),
        in_specs=[kspec, kspec,
                  pl.BlockSpec((N_EXPERTS, 1), lambda i: (0, 0)),
                  pl.BlockSpec((8, N_EXPERTS), lambda i: (0, 0))],
        out_specs=[pl.BlockSpec((1, TOP_K, tb), lambda i: (i, 0, 0)),
                   pl.BlockSpec((8, N_EXPERTS), lambda i: (0, 0)),
                   pl.BlockSpec((8, nip), lambda i: (0, 0))],
        out_shape=[jax.ShapeDtypeStruct((t // tb, TOP_K, tb), I32),
                   jax.ShapeDtypeStruct((8, N_EXPERTS), I32),
                   jax.ShapeDtypeStruct((8, nip), I32)],
        scratch_shapes=[pltpu.VMEM((N_EXPERTS, 1), I32)],
        compiler_params=_cparams("arbitrary"),
        name="dest",
    )(idx_t, rank_t, cnt_col, cnt_row)


_PAD_BITS = (64, 32, 16, 8)


def _sc_scatter_rows(rows, dest_win, n_out_rows):
    n_win, n_slots, win = dest_win.shape
    width = rows.shape[1]
    info = plsc.get_sparse_core_info()
    n_workers = info.num_cores * info.num_subcores
    per_worker = n_win // n_workers
    assert per_worker * n_workers == n_win and win <= LANES
    mesh = plsc.VectorSubcoreMesh(core_axis_name="c", subcore_axis_name="s")

    def body(rows_hbm, dest_hbm, out_hbm, idx_v, rows_v, sem):
        wid = lax.axis_index("s") * info.num_cores + lax.axis_index("c")

        @pl.loop(0, per_worker)
        def _(j):
            w = wid * per_worker + j
            pltpu.sync_copy(dest_hbm.at[w], idx_v)
            pltpu.sync_copy(rows_hbm.at[pl.ds(w * win, win)], rows_v)
            copies = [pltpu.async_copy(rows_v, out_hbm.at[idx_v.at[k]], sem) for k in range(n_slots)]
            for cp in copies:
                cp.wait()

    return pl.kernel(
        body,
        out_type=jax.ShapeDtypeStruct((n_out_rows, width), rows.dtype),
        mesh=mesh,
        scratch_types=[pltpu.VMEM((n_slots, win), I32), pltpu.VMEM((win, width), rows.dtype),
                       pltpu.SemaphoreType.DMA],
        name="sc_scatter",
    )(rows, dest_win)


def _pad_fill_kernel(meta_ref, xs_in, xs_hbm, zero_s, sem_z, *, e_per_step):
    del xs_in
    step = pl.program_id(0)
    zero_s[...] = jnp.zeros_like(zero_s)

    def tail_copy(c):
        row0 = pl.multiple_of((meta_ref[3, 0] + c) * MOE_BLOCK, MOE_BLOCK)
        return pltpu.make_async_copy(zero_s, xs_hbm.at[pl.ds(row0, MOE_BLOCK)], sem_z)

    @pl.when(step == 0)
    def _():
        for c in range(ITEM_BLOCKS - 1):
            tail_copy(c).start()
        for c in range(ITEM_BLOCKS - 1):
            tail_copy(c).wait()

    def pad_copies(e):
        cnt = meta_ref[0, e]
        off = meta_ref[1, e] + cnt
        rem = meta_ref[2, e] - cnt
        head = rem & (SUBLANES - 1)
        out = []
        for i in range(SUBLANES - 1):
            out.append((i < head,
                        pltpu.make_async_copy(zero_s.at[pl.ds(0, 1)], xs_hbm.at[pl.ds(off + i, 1)], sem_z)))
        off = off + head
        for bit in _PAD_BITS:
            out.append(((rem & bit) != 0,
                        pltpu.make_async_copy(zero_s.at[pl.ds(0, bit)],
                                              xs_hbm.at[pl.ds(pl.multiple_of(off, SUBLANES), bit)], sem_z)))
            off = off + (rem & bit)
        return out

    for j in range(e_per_step):
        for cond, c in pad_copies(step * e_per_step + j):
            pl.when(cond)(c.start)
    for j in range(e_per_step):
        for cond, c in pad_copies(step * e_per_step + j):
            pl.when(cond)(c.wait)


PAD_FILL_EXPERTS_PER_STEP = 8


def _pad_fill(meta, xs):
    half = xs.shape[1]
    return pl.pallas_call(
        functools.partial(_pad_fill_kernel, e_per_step=PAD_FILL_EXPERTS_PER_STEP),
        grid=(N_EXPERTS // PAD_FILL_EXPERTS_PER_STEP,),
        in_specs=[pl.BlockSpec(memory_space=pltpu.SMEM),
                  pl.BlockSpec(memory_space=pl.ANY)],
        out_specs=pl.BlockSpec(memory_space=pl.ANY),
        out_shape=jax.ShapeDtypeStruct(xs.shape, xs.dtype),
        input_output_aliases={1: 0},
        scratch_shapes=[pltpu.VMEM((MOE_BLOCK, half), PACKED), pltpu.SemaphoreType.DMA],
        compiler_params=_cparams("arbitrary"),
        name="pad_fill",
    )(meta, xs)


def _experts_kernel(items_ref, xs_hbm, wg_hbm, wu_hbm, wd_hbm, ys_hbm,
                    xbuf, ybuf, wg_f, wu_f, wd_f, wg_s, wu_s, wd_s, sem_x, sem_y, sem_w):
    n_items = items_ref[3, 0]
    rows = ITEM_BLOCKS * MOE_BLOCK

    def w_copies(e, s):
        return [pltpu.make_async_copy(src.at[e], dst.at[s], sem_w.at[s])
                for src, dst in ((wg_hbm, wg_f), (wu_hbm, wu_f), (wd_hbm, wd_f))]

    def x_copy(item, s):
        row0 = pl.multiple_of(items_ref[1, item] * MOE_BLOCK, MOE_BLOCK)
        return pltpu.make_async_copy(xs_hbm.at[pl.ds(row0, rows)], xbuf.at[s], sem_x.at[s])

    def y_copies(item, s, fn):
        for k in range(ITEM_BLOCKS):
            row0 = pl.multiple_of((items_ref[1, item] + k) * MOE_BLOCK, MOE_BLOCK)
            cp = pltpu.make_async_copy(ybuf.at[s, pl.ds(k * MOE_BLOCK, MOE_BLOCK)],
                                       ys_hbm.at[pl.ds(row0, MOE_BLOCK)], sem_y.at[s])
            pl.when(k < items_ref[2, item])(functools.partial(fn, cp))

    def expert_of(item):
        return items_ref[0, jnp.minimum(item, n_items - 1)]

    def changes_at(item):
        return ((item < n_items) & (expert_of(item) != expert_of(item - 1))).astype(I32)

    @pl.when(n_items > 0)
    def _():
        x_copy(0, 0).start()
        for cp in w_copies(expert_of(0), 0):
            cp.start()

        @pl.when(changes_at(1) == 1)
        def _():
            for cp in w_copies(expert_of(1), 1):
                cp.start()

    def item_body(i, ordinal):
        slot = i % 2
        prev = jnp.maximum(i - 1, 0)
        e = expert_of(i)
        new_expert = (i == 0) | (e != expert_of(prev))
        c1 = changes_at(i + 1)
        c2 = changes_at(i + 2)

        @pl.when(i + 1 < n_items)
        def _():
            x_copy(i + 1, 1 - slot).start()

        @pl.when(new_expert)
        def _():
            wslot = ordinal % WEIGHT_SLOTS
            for cp in w_copies(e, wslot):
                cp.wait()
            wg_s[...] = wg_f[wslot].astype(BF16)
            wu_s[...] = wu_f[wslot].astype(BF16)
            wd_s[...] = wd_f[wslot].astype(BF16)

        @pl.when(c2 == 1)
        def _():
            for cp in w_copies(expert_of(i + 2), (ordinal + c1 + 1) % WEIGHT_SLOTS):
                cp.start()

        x_copy(i, slot).wait()
        xa, xb = _unpack_halves(xbuf[slot])
        xa = xa.astype(BF16)
        xb = xb.astype(BF16)
        half = xa.shape[1]
        g = _dot(xa, wg_s[0:half]) + _dot(xb, wg_s[half:])
        u = _dot(xa, wu_s[0:half]) + _dot(xb, wu_s[half:])
        y = _dot((_silu(g) * u).astype(BF16), wd_s[...])
        ybuf[slot] = _pack_halves(y[:, :half], y[:, half:])
        y_copies(i, slot, lambda cp: cp.start())

        @pl.when(i > 0)
        def _():
            y_copies(prev, 1 - slot, lambda cp: cp.wait())

        return ordinal + c1

    lax.fori_loop(0, n_items, item_body, jnp.int32(0))

    @pl.when(n_items > 0)
    def _():
        last = n_items - 1
        y_copies(last, last % 2, lambda cp: cp.wait())


def _experts(items, xs, w_gate, w_up, w_down, *, n_blocks):
    half = xs.shape[1]
    e, d, f = w_gate.shape
    rows = ITEM_BLOCKS * MOE_BLOCK
    any_spec = pl.BlockSpec(memory_space=pl.ANY)
    return pl.pallas_call(
        _experts_kernel,
        grid_spec=pltpu.PrefetchScalarGridSpec(
            num_scalar_prefetch=1,
            grid=(1,),
            in_specs=[any_spec, any_spec, any_spec, any_spec],
            out_specs=any_spec,
            scratch_shapes=[pltpu.VMEM((2, rows, half), PACKED), pltpu.VMEM((2, rows, half), PACKED),
                            pltpu.VMEM((WEIGHT_SLOTS, d, f), F32), pltpu.VMEM((WEIGHT_SLOTS, d, f), F32),
                            pltpu.VMEM((WEIGHT_SLOTS, f, d), F32),
                            pltpu.VMEM((d, f), BF16), pltpu.VMEM((d, f), BF16), pltpu.VMEM((f, d), BF16),
                            pltpu.SemaphoreType.DMA((2,)), pltpu.SemaphoreType.DMA((2,)),
                            pltpu.SemaphoreType.DMA((WEIGHT_SLOTS,))]),
        out_shape=jax.ShapeDtypeStruct((n_blocks * MOE_BLOCK, half), PACKED),
        compiler_params=_cparams("arbitrary"),
        name="experts",
    )(items, xs, w_gate, w_up, w_down)


def _sc_gather_rows(table, idx):
    n_idx = idx.shape[0]
    width = table.shape[1]
    info = plsc.get_sparse_core_info()
    n_workers = info.num_cores * info.num_subcores
    per_worker = n_idx // n_workers
    assert per_worker * n_workers == n_idx and per_worker % (SC_GATHER_BUFS * SC_WINDOW) == 0
    mesh = plsc.VectorSubcoreMesh(core_axis_name="c", subcore_axis_name="s")

    def body(table_hbm, idx_hbm, out_hbm, idx_v, rows_v, sem_g, sem_o):
        wid = lax.axis_index("s") * info.num_cores + lax.axis_index("c")
        base = wid * per_worker

        @pl.loop(0, per_worker // (SC_GATHER_BUFS * SC_WINDOW))
        def _(it):
            offs = [base + (it * SC_GATHER_BUFS + b) * SC_WINDOW for b in range(SC_GATHER_BUFS)]
            gathers = []
            for b, off in enumerate(offs):
                pltpu.sync_copy(idx_hbm.at[pl.ds(off, SC_WINDOW)], idx_v.at[b])
                gathers.append(pltpu.async_copy(table_hbm.at[idx_v.at[b]], rows_v.at[b], sem_g.at[b]))
            writes = []
            for b, off in enumerate(offs):
                gathers[b].wait()
                writes.append(pltpu.async_copy(rows_v.at[b], out_hbm.at[pl.ds(off, SC_WINDOW)], sem_o.at[b]))
            for cp in writes:
                cp.wait()

    return pl.kernel(
        body,
        out_type=jax.ShapeDtypeStruct((n_idx, width), table.dtype),
        mesh=mesh,
        scratch_types=[pltpu.VMEM((SC_GATHER_BUFS, SC_WINDOW), I32),
                       pltpu.VMEM((SC_GATHER_BUFS, SC_WINDOW, width), table.dtype),
                       pltpu.SemaphoreType.DMA((SC_GATHER_BUFS,)), pltpu.SemaphoreType.DMA((SC_GATHER_BUFS,))],
        name="sc_gather",
    )(table, idx)


def _combine_kernel(hp_ref, x1_ref, g2_ref, w_ref, sgw_ref, suw_ref, sdw_ref, yg_ref, *rest):
    o_ref = rest[-1]
    xa, xb = _unpack_halves(hp_ref[...])
    xa = xa.astype(BF16)
    xb = xb.astype(BF16)
    half = xa.shape[1]
    tb = xa.shape[0]
    g = _dot(xa, sgw_ref[0:half]) + _dot(xb, sgw_ref[half:])
    u = _dot(xa, suw_ref[0:half]) + _dot(xb, suw_ref[half:])
    shared = _dot((_silu(g) * u).astype(BF16), sdw_ref[...])

    w = w_ref[...]
    acc_a = jnp.zeros((tb, half), F32)
    acc_b = jnp.zeros((tb, half), F32)
    for k in range(TOP_K):
        ya, yb = _unpack_halves(yg_ref[0, k])
        acc_a += ya * w[:, k:k + 1]
        acc_b += yb * w[:, k:k + 1]
    g2 = g2_ref[0]
    o_ref[:, 0:half] = x1_ref[:, 0:half] + g2[:, 0:half] * (acc_a + shared[:, 0:half])
    o_ref[:, half:] = x1_ref[:, half:] + g2[:, half:] * (acc_b + shared[:, half:])


def _combine(hp, x1, g2, w_tok, sgw, suw, sdw, yg, out_prev, *, tb, seq_len, first_step):
    t, half = hp.shape
    d = 2 * half
    per_seq = seq_len // tb
    f = sgw.shape[1]
    s0 = first_step
    in_specs = [pl.BlockSpec((tb, half), lambda i: (s0 + i, 0)),
                pl.BlockSpec((tb, d), lambda i: (s0 + i, 0)),
                pl.BlockSpec((1, 1, d), lambda i: ((s0 + i) // per_seq, 0, 0)),
                pl.BlockSpec((tb, TOP_K), lambda i: (s0 + i, 0)),
                pl.BlockSpec((d, f), lambda i: (0, 0)),
                pl.BlockSpec((d, f), lambda i: (0, 0)),
                pl.BlockSpec((f, d), lambda i: (0, 0)),
                pl.BlockSpec((1, TOP_K, tb, half), lambda i: (i, 0, 0, 0))]
    args = [hp, x1, g2, w_tok, sgw, suw, sdw, yg]
    aliases = {}
    if out_prev is not None:
        in_specs.append(pl.BlockSpec(memory_space=pl.ANY))
        args.append(out_prev)
        aliases = {len(args) - 1: 0}
    return pl.pallas_call(
        _combine_kernel,
        grid=(yg.shape[0],),
        in_specs=in_specs,
        out_specs=pl.BlockSpec((tb, d), lambda i: (s0 + i, 0)),
        out_shape=jax.ShapeDtypeStruct((t, d), F32),
        input_output_aliases=aliases,
        compiler_params=_cparams("arbitrary"),
        name="combine",
    )(*args)


def _rope_tables(l):
    rows = l // GRID_W
    r = jnp.repeat(jnp.arange(rows), GRID_W).astype(F32)
    col = jnp.tile(jnp.arange(GRID_W), rows).astype(F32)
    n_f = HEAD_DIM // 4
    freqs = ROPE_THETA ** (-jnp.arange(n_f, dtype=F32) / n_f)
    ang = jnp.concatenate([r[:, None] * freqs, col[:, None] * freqs], axis=-1)
    ang = jnp.tile(jnp.repeat(ang, 2, axis=1), (1, LANES // HEAD_DIM))
    sign = jnp.where(jnp.arange(LANES) % 2 == 0, -1.0, 1.0).astype(F32)
    return jnp.cos(ang), jnp.sin(ang) * sign


def kernel(x, c, ctx, c_ctx, w_mod, b_mod, norm1_w, norm2_w, w_in, q_norm_w, k_norm_w, ret_decay_fwd,
           ret_decay_bwd, w_out, router_w, router_bias, exp_w_gate, exp_w_up, exp_w_down, sh_w_gate,
           sh_w_up, sh_w_down):
    b, l, d = x.shape
    lc = ctx.shape[1]
    t = b * l
    assert w_mod.shape[0] == 1, "single layer"
    assert l % CHUNK == 0 and lc % CHUNK == 0 and l % GRID_W == 0

    rows = (b + 1 + 7) // 8 * 8
    cc = jnp.zeros((rows, d), F32).at[:b].set(c).at[b].set(c_ctx)
    mod = _modulation(cc, w_mod[0], b_mod[0])
    sh1, sc1, g1, sh2, sc2, g2 = [mod[:b, i * d:(i + 1) * d].reshape(b, 1, d) for i in range(6)]
    shc = mod[b, 0:d].reshape(1, 1, d)
    scc = mod[b, d:2 * d].reshape(1, 1, d)

    wi = w_in[0].astype(BF16)
    qnw = jnp.tile(q_norm_w[0], LANES // HEAD_DIM).reshape(1, LANES)
    knw = jnp.tile(k_norm_w[0], LANES // HEAD_DIM).reshape(1, LANES)
    cos, sin = _rope_tables(l)
    n1 = norm1_w[0].reshape(1, d)

    cklo, ckhi, cvlo, cvhi, crk, crv = _projection(
        ctx, shc, scc, n1, wi, qnw, knw, cos[:lc], sin[:lc], rope=False, with_q=False, tm=TILE_PROJ)
    klo, khi, vlo, vhi, rk, rv, q, rq, sg = _projection(
        x, sh1, sc1, n1, wi, qnw, knw, cos, sin, rope=True, with_q=True, tm=TILE_PROJ)

    attn = _attention(q, klo, khi, vlo, vhi, cklo, ckhi, cvlo, cvhi, tq=TILE_ATTN_Q)
    dec_f = jnp.repeat(ret_decay_fwd[0].astype(F32), HEAD_DIM).reshape(1, RET_W)
    dec_b = jnp.repeat(ret_decay_bwd[0].astype(F32), HEAD_DIM).reshape(1, RET_W)
    ret = _retention(rq, rk, rv, sg, crk, crv, dec_f, dec_b)

    wo = w_out[0].astype(BF16)
    r_hi, r_lo = _split(router_w[0].T)
    x1, hp, logits_t = _out_projection(attn, ret, x, wo[:ATTN_W], wo[ATTN_W:], g1, sh2, sc2,
                                       norm2_w[0].reshape(1, d), r_hi, r_lo, tm=TILE_OUT)

    idx_t, w_t, rank_t, cnt_col, cnt_row = _route(logits_t, router_bias[0].reshape(N_EXPERTS, 1), tb=TILE_TOKENS)
    n_blocks = -(-(t * TOP_K) // MOE_BLOCK) + N_EXPERTS
    tb = TILE_TOKENS
    dest, meta, items = _destinations(idx_t, rank_t, cnt_col, cnt_row, tb=tb, n_blocks=n_blocks)
    steps, _, tbe = dest.shape
    dest_win = dest.reshape(steps, TOP_K, tbe // SC_WINDOW, SC_WINDOW).transpose(0, 2, 1, 3)
    dest_win = dest_win.reshape(t // SC_WINDOW, TOP_K, SC_WINDOW)
    xs = _sc_scatter_rows(hp, dest_win, (n_blocks + ITEM_BLOCKS - 1) * MOE_BLOCK)
    xs = _pad_fill(meta, xs)
    ys = _experts(items, xs, exp_w_gate[0], exp_w_up[0], exp_w_down[0], n_blocks=n_blocks)
    parts = COMBINE_PARTS if steps % COMBINE_PARTS == 0 else 1
    steps_part = steps // parts
    x1f, w_tok = x1.reshape(t, d), w_t.T
    sgw, suw, sdw = sh_w_gate[0].astype(BF16), sh_w_up[0].astype(BF16), sh_w_down[0].astype(BF16)
    out = None
    for p in range(parts):
        idx = dest[p * steps_part:(p + 1) * steps_part].reshape(-1)
        yg = _sc_gather_rows(ys, idx).reshape(steps_part, TOP_K, tbe, d // 2)
        out = _combine(hp, x1f, g2, w_tok, sgw, suw, sdw, yg, out, tb=tbe, seq_len=l, first_step=p * steps_part)
    return out.reshape(b, l, d)
```

```python
import functools

import jax
import jax.numpy as jnp
from jax import lax
from jax.experimental import pallas as pl
from jax.experimental.pallas import tpu as pltpu
from jax.experimental.pallas import tpu_sc as plsc

F32 = jnp.float32
BF16 = jnp.bfloat16
I32 = jnp.int32
U32 = jnp.uint32
PACKED = jnp.int32

HEAD_DIM = 64
LANES = 128
SUBLANES = 8
ATTN_HEADS = 8
ATTN_KV_HEADS = 2
GQA = ATTN_HEADS // ATTN_KV_HEADS
RET_HEADS = 8
ATTN_W = ATTN_HEADS * HEAD_DIM
KV_W = ATTN_KV_HEADS * HEAD_DIM
RET_W = RET_HEADS * HEAD_DIM
RET_PAIRS = RET_W // LANES
CHUNK = 128
GRID_W = 64
ROPE_THETA = 10000.0
N_EXPERTS = 256
TOP_K = 8
N_GROUPS = 8
GROUP_SIZE = N_EXPERTS // N_GROUPS
TOPK_GROUPS = 4
ROUTED_SCALE = 2.5
MOE_BLOCK = 128
EPS = 1e-6
QK_SCALE = HEAD_DIM ** -0.5
LOG2_E = 1.4426950408889634

OFF_AK = 0
OFF_AV = OFF_AK + KV_W
OFF_RK = OFF_AV + KV_W
OFF_RV = OFF_RK + RET_W
CTX_KV_COLS = OFF_RV + RET_W
OFF_AQ = CTX_KV_COLS
OFF_RQ = OFF_AQ + ATTN_W
OFF_RG = OFF_RQ + RET_W
IN_COLS = OFF_RG + RET_W

VMEM_LIMIT = 52 * 1024 * 1024

SUB_ROWS = 256
TILE_PROJ = 1024
TILE_ATTN_Q = 512
TILE_OUT = 512
TILE_TOKENS = 256
ITEM_BLOCKS = 5
COMBINE_PARTS = 4
SC_GATHER_BUFS = 2
WEIGHT_SLOTS = 3
SC_WINDOW = 64
HI_MASK = 0xFFFF0000


def _cparams(*sem):
    return pltpu.CompilerParams(dimension_semantics=sem, vmem_limit_bytes=VMEM_LIMIT)


def _split(a):
    hi = a.astype(BF16)
    lo = (a - hi.astype(F32)).astype(BF16)
    return hi, lo


def _dot(a, b):
    return jnp.dot(a, b, preferred_element_type=F32)


def _dot_nt(a, b):
    return lax.dot_general(a, b, (((1,), (1,)), ((), ())), preferred_element_type=F32)


def _sigmoid(v):
    return 1.0 / (1.0 + jnp.exp(-v))


def _silu(v):
    return v * _sigmoid(v)


def _pack_halves(a, b):
    ua = lax.bitcast_convert_type(a.astype(BF16).astype(F32), U32)
    ub = lax.bitcast_convert_type(b.astype(BF16).astype(F32), U32)
    return lax.bitcast_convert_type((ua & jnp.uint32(HI_MASK)) | (ub >> 16), PACKED)


def _unpack_halves(p):
    u = lax.bitcast_convert_type(p, U32)
    a = lax.bitcast_convert_type(u & jnp.uint32(HI_MASK), F32)
    b = lax.bitcast_convert_type(u << 16, F32)
    return a, b


def _mod_kernel(c_ref, w_ref, b_ref, o_ref):
    s_hi, s_lo = _split(_silu(c_ref[...]))
    w_hi, w_lo = _split(w_ref[...])
    o_ref[...] = _dot(s_hi, w_hi) + _dot(s_hi, w_lo) + _dot(s_lo, w_hi) + b_ref[...]


def _modulation(cc, w_mod, b_mod):
    rows, d = cc.shape
    n = w_mod.shape[1]
    tn = 768
    return pl.pallas_call(
        _mod_kernel,
        grid=(n // tn,),
        in_specs=[pl.BlockSpec((rows, d), lambda j: (0, 0)),
                  pl.BlockSpec((d, tn), lambda j: (0, j)),
                  pl.BlockSpec((1, tn), lambda j: (0, j))],
        out_specs=pl.BlockSpec((rows, tn), lambda j: (0, j)),
        out_shape=jax.ShapeDtypeStruct((rows, n), F32),
        compiler_params=_cparams("arbitrary"),
        name="mod",
    )(cc, w_mod, b_mod.reshape(1, n))


def _segment_ones():
    r = lax.broadcasted_iota(I32, (LANES, LANES), 0) // HEAD_DIM
    c = lax.broadcasted_iota(I32, (LANES, LANES), 1) // HEAD_DIM
    return jnp.where(r == c, 1.0, 0.0).astype(BF16)


def _head_mean_sq(v, seg):
    hi, lo = _split(v * v)
    return (_dot(hi, seg) + _dot(lo, seg)) * (1.0 / HEAD_DIM)


def _proj_kernel(x_ref, sh_ref, sc_ref, nw_ref, wi_ref, qnw_ref, knw_ref, cos_ref, sin_ref,
                 *out_refs, rope, with_q):
    if with_q:
        klo_ref, khi_ref, vlo_ref, vhi_ref, rk_ref, rv_ref, q_ref, rq_ref, sg_ref = out_refs
    else:
        klo_ref, khi_ref, vlo_ref, vhi_ref, rk_ref, rv_ref = out_refs
    tm = x_ref.shape[1]
    sub = min(tm, SUB_ROWS)
    seg = _segment_ones()
    lane = lax.broadcasted_iota(I32, (sub, LANES), 1)
    low_half = lane < HEAD_DIM
    even = (lane & 1) == 0

    for r0 in range(0, tm, sub):
        rows = slice(r0, r0 + sub)
        x = x_ref[0, rows, :]
        h = x * lax.rsqrt(jnp.mean(x * x, axis=-1, keepdims=True) + EPS) * nw_ref[...]
        h = h * (1.0 + sc_ref[0]) + sh_ref[0]
        z = _dot(h.astype(BF16), wi_ref[...])

        def norm_rope(v, w128, rows=rows):
            v = v * lax.rsqrt(_head_mean_sq(v, seg) + EPS) * w128
            if rope:
                swapped = jnp.where(even, pltpu.roll(v, LANES - 1, 1), pltpu.roll(v, 1, 1))
                v = v * cos_ref[rows, :] + swapped * sin_ref[rows, :]
            return v

        k = norm_rope(z[:, OFF_AK:OFF_AK + KV_W], knw_ref[...])
        ksw = pltpu.roll(k, HEAD_DIM, 1)
        klo_ref[0, 0, :, rows] = jnp.transpose(jnp.where(low_half, k, 0.0)).astype(BF16)
        khi_ref[0, 0, :, rows] = jnp.transpose(jnp.where(low_half, 0.0, ksw)).astype(BF16)
        klo_ref[0, 1, :, rows] = jnp.transpose(jnp.where(low_half, ksw, 0.0)).astype(BF16)
        khi_ref[0, 1, :, rows] = jnp.transpose(jnp.where(low_half, 0.0, k)).astype(BF16)
        v = z[:, OFF_AV:OFF_AV + KV_W]
        vsw = pltpu.roll(v, HEAD_DIM, 1)
        vlo_ref[0, 0, rows, :] = jnp.where(low_half, v, 0.0).astype(BF16)
        vhi_ref[0, 0, rows, :] = jnp.where(low_half, 0.0, vsw).astype(BF16)
        vlo_ref[0, 1, rows, :] = jnp.where(low_half, vsw, 0.0).astype(BF16)
        vhi_ref[0, 1, rows, :] = jnp.where(low_half, 0.0, v).astype(BF16)
        rk_ref[0, rows, :] = (z[:, OFF_RK:OFF_RK + RET_W] * QK_SCALE).astype(BF16)
        rv_ref[0, rows, :] = z[:, OFF_RV:OFF_RV + RET_W].astype(BF16)
        if with_q:
            for j in range(ATTN_W // LANES):
                qj = norm_rope(z[:, OFF_AQ + j * LANES:OFF_AQ + (j + 1) * LANES], qnw_ref[...])
                q_ref[0, rows, j * LANES:(j + 1) * LANES] = (qj * (QK_SCALE * LOG2_E)).astype(BF16)
            rq_ref[0, rows, :] = z[:, OFF_RQ:OFF_RQ + RET_W].astype(BF16)
            sg_ref[0, rows, :] = _silu(z[:, OFF_RG:OFF_RG + RET_W]).astype(BF16)


def _projection(x, shift, scale, norm_w, wi_bf16, qnw, knw, cos, sin, *, rope, with_q, tm):
    b, l, d = x.shape
    tm = min(tm, l)
    ncols = IN_COLS if with_q else CTX_KV_COLS
    per_batch = shift.shape[0] > 1
    mod_idx = (lambda bi, i: (bi, 0, 0)) if per_batch else (lambda bi, i: (0, 0, 0))
    kv_shape = jax.ShapeDtypeStruct((b, ATTN_KV_HEADS, l, LANES), BF16)
    kv_spec = pl.BlockSpec((1, ATTN_KV_HEADS, tm, LANES), lambda bi, i: (bi, 0, i, 0))
    kt_shape = jax.ShapeDtypeStruct((b, ATTN_KV_HEADS, LANES, l), BF16)
    kt_spec = pl.BlockSpec((1, ATTN_KV_HEADS, LANES, tm), lambda bi, i: (bi, 0, 0, i))
    w_shape = jax.ShapeDtypeStruct((b, l, RET_W), BF16)
    w_spec = pl.BlockSpec((1, tm, RET_W), lambda bi, i: (bi, i, 0))
    out_shape = [kt_shape] * 2 + [kv_shape] * 2 + [w_shape] * 2
    out_specs = [kt_spec] * 2 + [kv_spec] * 2 + [w_spec] * 2
    if with_q:
        out_shape += [w_shape] * 3
        out_specs += [w_spec] * 3
    return pl.pallas_call(
        functools.partial(_proj_kernel, rope=rope, with_q=with_q),
        grid=(b, l // tm),
        in_specs=[pl.BlockSpec((1, tm, d), lambda bi, i: (bi, i, 0)),
                  pl.BlockSpec((1, 1, d), mod_idx),
                  pl.BlockSpec((1, 1, d), mod_idx),
                  pl.BlockSpec((1, d), lambda bi, i: (0, 0)),
                  pl.BlockSpec((d, ncols), lambda bi, i: (0, 0)),
                  pl.BlockSpec((1, LANES), lambda bi, i: (0, 0)),
                  pl.BlockSpec((1, LANES), lambda bi, i: (0, 0)),
                  pl.BlockSpec((tm, LANES), lambda bi, i: (i, 0)),
                  pl.BlockSpec((tm, LANES), lambda bi, i: (i, 0))],
        out_specs=out_specs,
        out_shape=out_shape,
        compiler_params=_cparams("arbitrary", "arbitrary"),
        name="proj_latent" if with_q else "proj_ctx",
    )(x, shift, scale, norm_w, wi_bf16, qnw, knw, cos, sin)


def _attn_kernel(q_ref, klo_ref, khi_ref, vlo_ref, vhi_ref, cklo_ref, ckhi_ref, cvlo_ref, cvhi_ref,
                 o_ref, kl_s, kh_s, va_s, *, l, lc):
    lk = l + lc

    @pl.when(pl.program_id(2) == 0)
    def _():
        kl_s[:, 0:l] = klo_ref[0, 0]
        kl_s[:, l:lk] = cklo_ref[0, 0]
        kh_s[:, 0:l] = khi_ref[0, 0]
        kh_s[:, l:lk] = ckhi_ref[0, 0]
        lane = lax.broadcasted_iota(I32, (lk, LANES), 1)
        ones_lo = jnp.where(lane < HEAD_DIM, 1.0, 0.0).astype(BF16)
        ones_hi = jnp.where(lane < HEAD_DIM, 0.0, 1.0).astype(BF16)
        for g, (v_ref, cv_ref, ones) in enumerate(((vlo_ref, cvlo_ref, ones_lo), (vhi_ref, cvhi_ref, ones_hi),
                                                   (vlo_ref, cvlo_ref, ones_lo), (vhi_ref, cvhi_ref, ones_hi))):
            v_col, one_col = (0, LANES) if g < 2 else (LANES, 0)
            va_s[g, 0:l, v_col:v_col + LANES] = v_ref[0, 0]
            va_s[g, l:lk, v_col:v_col + LANES] = cv_ref[0, 0]
            va_s[g, :, one_col:one_col + LANES] = ones

    q = q_ref[0]
    acc = []
    for g in range(GQA):
        qp = q[:, (g // 2) * LANES:(g // 2 + 1) * LANES]
        kt = kl_s[...] if g % 2 == 0 else kh_s[...]
        s = _dot(qp, kt)
        p = jnp.exp2(s - jnp.max(s, axis=-1, keepdims=True)).astype(BF16)
        acc.append(_dot(p, va_s[g]))
    out_a = acc[0] + acc[1]
    out_b = acc[2] + acc[3]
    o_ref[0, :, 0:LANES] = (out_a[:, 0:LANES] / out_a[:, LANES:2 * LANES]).astype(BF16)
    o_ref[0, :, LANES:2 * LANES] = (out_b[:, LANES:2 * LANES] / out_b[:, 0:LANES]).astype(BF16)


def _attention(q, klo, khi, vlo, vhi, cklo, ckhi, cvlo, cvhi, *, tq):
    b, l, _ = q.shape
    lc = cvlo.shape[2]
    lk = l + lc
    tq = min(tq, l)
    gw = GQA * HEAD_DIM
    kt_spec = pl.BlockSpec((1, 1, LANES, l), lambda bi, h, i: (bi, h, 0, 0))
    kv_spec = pl.BlockSpec((1, 1, l, LANES), lambda bi, h, i: (bi, h, 0, 0))
    ckt_spec = pl.BlockSpec((1, 1, LANES, lc), lambda bi, h, i: (bi, h, 0, 0))
    ckv_spec = pl.BlockSpec((1, 1, lc, LANES), lambda bi, h, i: (bi, h, 0, 0))
    return pl.pallas_call(
        functools.partial(_attn_kernel, l=l, lc=lc),
        grid=(b, ATTN_KV_HEADS, l // tq),
        in_specs=([pl.BlockSpec((1, tq, gw), lambda bi, h, i: (bi, i, h))] + [kt_spec] * 2 + [kv_spec] * 2
                  + [ckt_spec] * 2 + [ckv_spec] * 2),
        out_specs=pl.BlockSpec((1, tq, gw), lambda bi, h, i: (bi, i, h)),
        out_shape=jax.ShapeDtypeStruct((b, l, ATTN_W), BF16),
        scratch_shapes=[pltpu.VMEM((LANES, lk), BF16), pltpu.VMEM((LANES, lk), BF16),
                        pltpu.VMEM((GQA, lk, 2 * LANES), BF16)],
        compiler_params=_cparams("arbitrary", "arbitrary", "arbitrary"),
        name="attn",
    )(q, klo, khi, vlo, vhi, cklo, ckhi, cvlo, cvhi)


def _log_sigmoid(v):
    return jnp.minimum(v, 0.0) - jnp.log(1.0 + jnp.exp(-jnp.abs(v)))


def _ret_kernel(rq_ref, rk_ref, rv_ref, sg_ref, crk_ref, crv_ref, df_ref, db_ref, o_ref,
                m_s, xi_s, zeta_s, kv_s, st_s, *, l, lc):
    n = l // CHUNK
    nc = lc // CHUNK
    lgf = _log_sigmoid(df_ref[...])
    lgb = _log_sigmoid(db_ref[...])
    pos = lax.broadcasted_iota(I32, (CHUNK, LANES), 0).astype(F32)
    row = lax.broadcasted_iota(I32, (CHUNK, CHUNK), 0)
    col = lax.broadcasted_iota(I32, (CHUNK, CHUNK), 1)
    diff = (row - col).astype(F32)
    g_chunk = []
    for p in range(RET_PAIRS):
        cols = slice(p * LANES, (p + 1) * LANES)
        lf, lb = lgf[:, cols], lgb[:, cols]
        xi_s[p, :, 0:LANES] = jnp.exp((pos + 1.0) * lf)
        xi_s[p, :, LANES:] = jnp.exp((CHUNK - pos) * lb)
        zeta_s[p, :, 0:LANES] = jnp.exp((CHUNK - 1.0 - pos) * lf)
        zeta_s[p, :, LANES:] = jnp.exp(pos * lb)
        g_chunk.append((jnp.exp(CHUNK * lf), jnp.exp(CHUNK * lb)))
        for j in range(2):
            h = 2 * p + j
            hf = lgf[:, h * HEAD_DIM:h * HEAD_DIM + 1]
            hb = lgb[:, h * HEAD_DIM:h * HEAD_DIM + 1]
            m_s[p, :, j * CHUNK:(j + 1) * CHUNK] = jnp.where(
                diff > 0, jnp.exp(diff * hf), jnp.where(diff < 0, jnp.exp(-diff * hb), 2.0))

    lane = lax.broadcasted_iota(I32, (CHUNK, LANES), 1)
    low_half = lane < HEAD_DIM
    diag = (lax.broadcasted_iota(I32, (LANES, LANES), 0) // HEAD_DIM
            == lax.broadcasted_iota(I32, (LANES, LANES), 1) // HEAD_DIM)
    seg = jnp.where(diag, 1.0, 0.0).astype(BF16)
    seg2 = jnp.concatenate([seg, seg], axis=0)
    diag2 = jnp.concatenate([diag, diag], axis=0)

    def split_heads(a):
        zero = jnp.zeros_like(a)
        return jnp.concatenate([jnp.where(low_half, a, zero), jnp.where(low_half, zero, a)], axis=0)

    def contrib(k_ref, v_ref, r0, p):
        cols = slice(p * LANES, (p + 1) * LANES)
        kp = k_ref[0, pl.ds(r0, CHUNK), cols].astype(F32)
        kz = jnp.concatenate([kp, kp], axis=1) * zeta_s[p]
        kv = _dot(jnp.transpose(kz).astype(BF16), v_ref[0, pl.ds(r0, CHUNK), cols])
        return jnp.where(diag2, kv, 0.0)

    for c in range(nc):
        for p in range(RET_PAIRS):
            kv_s[c, p] = contrib(crk_ref, crv_ref, c * CHUNK, p)

    def contrib_body(c, carry):
        r0 = pl.multiple_of(c * CHUNK, CHUNK)
        for p in range(RET_PAIRS):
            kv_s[nc + c, p] = contrib(rk_ref, rv_ref, r0, p)
        return carry

    lax.fori_loop(0, n, contrib_body, 0, unroll=4)

    for p in range(RET_PAIRS):
        gf, gb = g_chunk[p]
        sf = jnp.zeros((LANES, LANES), F32)
        sb = jnp.zeros((LANES, LANES), F32)
        for c in range(nc):
            sf = gf * sf + kv_s[c, p, 0:LANES]
            sb = gb * sb + kv_s[nc - 1 - c, p, LANES:]

        def fwd_scan(c, s, p=p, gf=gf):
            st_s[c, p, 0:LANES] = s.astype(BF16)
            return gf * s + kv_s[nc + c, p, 0:LANES]

        def bwd_scan(j, s, p=p, gb=gb):
            c = n - 1 - j
            st_s[c, p, LANES:] = s.astype(BF16)
            return gb * s + kv_s[nc + c, p, LANES:]

        lax.fori_loop(0, n, fwd_scan, sf)
        lax.fori_loop(0, n, bwd_scan, sb)

    def out_body(c, carry):
        r0 = pl.multiple_of(c * CHUNK, CHUNK)
        for p in range(RET_PAIRS):
            cols = slice(p * LANES, (p + 1) * LANES)
            qp = rq_ref[0, pl.ds(r0, CHUNK), cols]
            kp = rk_ref[0, pl.ds(r0, CHUNK), cols]
            vp = rv_ref[0, pl.ds(r0, CHUNK), cols]
            s2 = _dot_nt(qp, split_heads(kp))
            a2 = (s2 * m_s[p]).astype(BF16)
            y = _dot(a2, split_heads(vp))
            qf = qp.astype(F32)
            qx = (jnp.concatenate([qf, qf], axis=1) * xi_s[p]).astype(BF16)
            y += _dot(qx, st_s[c, p])
            hi, lo = _split(y * y)
            ms = _dot(jnp.concatenate([hi, lo], axis=1), seg2) * (1.0 / HEAD_DIM)
            out = y * lax.rsqrt(ms + EPS) * sg_ref[0, pl.ds(r0, CHUNK), cols].astype(F32)
            o_ref[0, pl.ds(r0, CHUNK), cols] = out.astype(BF16)
        return carry

    lax.fori_loop(0, n, out_body, 0, unroll=4)


def _retention(rq, rk, rv, sg, crk, crv, dec_f, dec_b):
    b, l, _ = rq.shape
    lc = crk.shape[1]
    n = l // CHUNK
    nc = lc // CHUNK
    spec = pl.BlockSpec((1, l, RET_W), lambda bi: (bi, 0, 0))
    cspec = pl.BlockSpec((1, lc, RET_W), lambda bi: (bi, 0, 0))
    dspec = pl.BlockSpec((1, RET_W), lambda bi: (0, 0))
    return pl.pallas_call(
        functools.partial(_ret_kernel, l=l, lc=lc),
        grid=(b,),
        in_specs=[spec, spec, spec, spec, cspec, cspec, dspec, dspec],
        out_specs=spec,
        out_shape=jax.ShapeDtypeStruct((b, l, RET_W), BF16),
        scratch_shapes=[pltpu.VMEM((RET_PAIRS, CHUNK, 2 * CHUNK), F32),
                        pltpu.VMEM((RET_PAIRS, CHUNK, 2 * LANES), F32),
                        pltpu.VMEM((RET_PAIRS, CHUNK, 2 * LANES), F32),
                        pltpu.VMEM((nc + n, RET_PAIRS, 2 * LANES, LANES), F32),
                        pltpu.VMEM((n, RET_PAIRS, 2 * LANES, LANES), BF16)],
        compiler_params=_cparams("arbitrary"),
        name="ret",
    )(rq, rk, rv, sg, crk, crv, dec_f, dec_b)


def _out_kernel(attn_ref, ret_ref, x_ref, wa_ref, wr_ref, g1_ref, sh_ref, sc_ref, nw_ref, rhi_ref, rlo_ref,
                x1_ref, hp_ref, lg_ref):
    y = _dot(attn_ref[0], wa_ref[...]) + _dot(ret_ref[0], wr_ref[...])
    x1 = x_ref[0] + g1_ref[0] * y
    x1_ref[0] = x1
    h = x1 * lax.rsqrt(jnp.mean(x1 * x1, axis=-1, keepdims=True) + EPS) * nw_ref[...]
    h = h * (1.0 + sc_ref[0]) + sh_ref[0]
    half = h.shape[1] // 2
    hp_ref[...] = _pack_halves(h[:, :half], h[:, half:])
    h_hi, h_lo = _split(h)
    lg_ref[...] = _dot_nt(rhi_ref[...], h_hi) + _dot_nt(rhi_ref[...], h_lo) + _dot_nt(rlo_ref[...], h_hi)


def _out_projection(attn, ret, x, wa, wr, g1, sh2, sc2, norm_w, r_hi, r_lo, *, tm):
    b, l, d = x.shape
    tm = min(tm, l)
    nt = l // tm
    t = b * l
    mspec = pl.BlockSpec((1, 1, d), lambda bi, i: (bi, 0, 0))
    return pl.pallas_call(
        _out_kernel,
        grid=(b, nt),
        in_specs=[pl.BlockSpec((1, tm, ATTN_W), lambda bi, i: (bi, i, 0)),
                  pl.BlockSpec((1, tm, RET_W), lambda bi, i: (bi, i, 0)),
                  pl.BlockSpec((1, tm, d), lambda bi, i: (bi, i, 0)),
                  pl.BlockSpec((ATTN_W, d), lambda bi, i: (0, 0)),
                  pl.BlockSpec((RET_W, d), lambda bi, i: (0, 0)),
                  mspec, mspec, mspec,
                  pl.BlockSpec((1, d), lambda bi, i: (0, 0)),
                  pl.BlockSpec((N_EXPERTS, d), lambda bi, i: (0, 0)),
                  pl.BlockSpec((N_EXPERTS, d), lambda bi, i: (0, 0))],
        out_specs=[pl.BlockSpec((1, tm, d), lambda bi, i: (bi, i, 0)),
                   pl.BlockSpec((tm, d // 2), lambda bi, i: (bi * nt + i, 0)),
                   pl.BlockSpec((N_EXPERTS, tm), lambda bi, i: (0, bi * nt + i))],
        out_shape=[jax.ShapeDtypeStruct((b, l, d), F32),
                   jax.ShapeDtypeStruct((t, d // 2), PACKED),
                   jax.ShapeDtypeStruct((N_EXPERTS, t), F32)],
        compiler_params=_cparams("arbitrary", "arbitrary"),
        name="out_proj",
    )(attn, ret, x, wa, wr, g1, sh2, sc2, norm_w, r_hi, r_lo)


def _route_kernel(lg_ref, bias_ref, idx_ref, w_ref, rank_ref, cnt_col_ref, cnt_row_ref, tri_s, col_s, row_s):
    tb = lg_ref.shape[1]
    step = pl.program_id(0)

    @pl.when(step == 0)
    def _():
        r = lax.broadcasted_iota(I32, (tb, tb), 0)
        c = lax.broadcasted_iota(I32, (tb, tb), 1)
        tri_s[...] = jnp.where(r <= c, 1.0, 0.0).astype(BF16)
        col_s[...] = jnp.zeros_like(col_s)
        row_s[...] = jnp.zeros_like(row_s)

    scores = _sigmoid(lg_ref[...])
    biased = scores + bias_ref[...]
    neg = -jnp.inf
    sub = lax.broadcasted_iota(I32, (GROUP_SIZE, tb), 0).astype(F32)

    gscore = []
    for g in range(N_GROUPS):
        blk = biased[g * GROUP_SIZE:(g + 1) * GROUP_SIZE]
        m1 = jnp.max(blk, axis=0, keepdims=True)
        first = jnp.min(jnp.where(blk == m1, sub, float(GROUP_SIZE)), axis=0, keepdims=True)
        m2 = jnp.max(jnp.where(sub == first, neg, blk), axis=0, keepdims=True)
        gscore.append(m1 + m2)
    gs = jnp.concatenate(gscore, axis=0)
    gsub = lax.broadcasted_iota(I32, (N_GROUPS, tb), 0).astype(F32)
    keep = jnp.zeros((N_GROUPS, tb), F32)
    for _ in range(TOPK_GROUPS):
        m = jnp.max(gs, axis=0, keepdims=True)
        first = jnp.min(jnp.where(gs == m, gsub, float(N_GROUPS)), axis=0, keepdims=True)
        sel = gsub == first
        keep = jnp.where(sel, 1.0, keep)
        gs = jnp.where(sel, neg, gs)
    masked = jnp.concatenate(
        [jnp.where(keep[g:g + 1] > 0.0, biased[g * GROUP_SIZE:(g + 1) * GROUP_SIZE], neg)
         for g in range(N_GROUPS)], axis=0)

    esub = lax.broadcasted_iota(I32, (N_EXPERTS, tb), 0).astype(F32)
    sels, idxs, ws = [], [], []
    chosen = jnp.zeros((N_EXPERTS, tb), F32)
    for _ in range(TOP_K):
        m = jnp.max(masked, axis=0, keepdims=True)
        first = jnp.min(jnp.where(masked == m, esub, float(N_EXPERTS)), axis=0, keepdims=True)
        sel = esub == first
        sels.append(sel)
        idxs.append(first)
        ws.append(jnp.sum(jnp.where(sel, scores, 0.0), axis=0, keepdims=True))
        chosen = jnp.where(sel, 1.0, chosen)
        masked = jnp.where(sel, neg, masked)
    wsum = ws[0]
    for k in range(1, TOP_K):
        wsum = wsum + ws[k]
    idx_ref[...] = jnp.concatenate(idxs, axis=0).astype(I32)
    w_ref[...] = jnp.concatenate([wk / wsum * ROUTED_SCALE for wk in ws], axis=0)

    chosen_b = chosen.astype(BF16)
    incl = _dot(chosen_b, tri_s[...])
    before = incl - chosen + col_s[...]
    rank_ref[...] = jnp.concatenate(
        [jnp.sum(jnp.where(sel, before, 0.0), axis=0, keepdims=True) for sel in sels], axis=0).astype(I32)
    col_s[...] = col_s[...] + incl[:, tb - 1:tb]
    row_s[...] = row_s[...] + _dot_nt(jnp.ones((8, tb), BF16), chosen_b)
    cnt_col_ref[...] = col_s[...].astype(I32)
    cnt_row_ref[...] = row_s[...].astype(I32)


def _route(logits_t, bias_col, *, tb):
    e, t = logits_t.shape
    tb = min(tb, t)
    kspec = pl.BlockSpec((TOP_K, tb), lambda i: (0, i))
    return pl.pallas_call(
        _route_kernel,
        grid=(t // tb,),
        in_specs=[pl.BlockSpec((e, tb), lambda i: (0, i)),
                  pl.BlockSpec((e, 1), lambda i: (0, 0))],
        out_specs=[kspec, kspec, kspec,
                   pl.BlockSpec((e, 1), lambda i: (0, 0)),
                   pl.BlockSpec((8, e), lambda i: (0, 0))],
        out_shape=[jax.ShapeDtypeStruct((TOP_K, t), I32),
                   jax.ShapeDtypeStruct((TOP_K, t), F32),
                   jax.ShapeDtypeStruct((TOP_K, t), I32),
                   jax.ShapeDtypeStruct((e, 1), I32),
                   jax.ShapeDtypeStruct((8, e), I32)],
        scratch_shapes=[pltpu.VMEM((tb, tb), BF16), pltpu.VMEM((e, 1), F32), pltpu.VMEM((8, e), F32)],
        compiler_params=_cparams("arbitrary"),
        name="route",
    )(logits_t, bias_col)


def _pad_block(cnt):
    return (cnt + (MOE_BLOCK - 1)) // MOE_BLOCK * MOE_BLOCK


def _max_items(n_blocks):
    return n_blocks // ITEM_BLOCKS + N_EXPERTS


def _dest_kernel(idx_ref, rank_ref, cnt_col_ref, cnt_row_ref, dest_ref, meta_ref, items_ref, start_s):
    tb = idx_ref.shape[1]
    nip = items_ref.shape[1]

    @pl.when(pl.program_id(0) == 0)
    def _():
        pad_col = _pad_block(cnt_col_ref[...])
        pad_row = _pad_block(cnt_row_ref[0:1, :])
        er = lax.broadcasted_iota(I32, (N_EXPERTS, N_EXPERTS), 0)
        ec = lax.broadcasted_iota(I32, (N_EXPERTS, N_EXPERTS), 1)
        start_col = jnp.sum(jnp.where(ec < er, pad_row, 0), axis=1, keepdims=True)
        start_row = jnp.sum(jnp.where(er < ec, pad_col, 0), axis=0, keepdims=True)
        start_s[...] = start_col

        used = jnp.sum(pad_row, axis=1, keepdims=True) // MOE_BLOCK
        meta_ref[...] = jnp.concatenate(
            [cnt_row_ref[0:1, :], start_row, pad_row, jnp.broadcast_to(used, (1, N_EXPERTS)),
             jnp.zeros((4, N_EXPERTS), I32)], axis=0)

        nb_col = pad_col // MOE_BLOCK
        it_col = (nb_col + (ITEM_BLOCKS - 1)) // ITEM_BLOCKS
        it_row = (pad_row // MOE_BLOCK + (ITEM_BLOCKS - 1)) // ITEM_BLOCKS
        it_start = jnp.sum(jnp.where(ec < er, it_row, 0), axis=1, keepdims=True)
        n_items = jnp.sum(it_row, axis=1, keepdims=True)
        lane = lax.broadcasted_iota(I32, (1, nip), 1)
        owner = jnp.sum(jnp.where(it_start + it_col <= lane, 1, 0), axis=0, keepdims=True)
        owner = jnp.minimum(owner, N_EXPERTS - 1)
        onehot = lax.broadcasted_iota(I32, (N_EXPERTS, nip), 0) == owner

        def pick(col):
            return jnp.sum(jnp.where(onehot, col, 0), axis=0, keepdims=True)

        j = lane - pick(it_start)
        block0 = pick(start_col) // MOE_BLOCK + ITEM_BLOCKS * j
        nvalid = jnp.clip(pick(nb_col) - ITEM_BLOCKS * j, 0, ITEM_BLOCKS)
        items_ref[...] = jnp.concatenate(
            [owner, block0, jnp.where(lane < n_items, nvalid, 0), jnp.broadcast_to(n_items, (1, nip)),
             jnp.zeros((4, nip), I32)], axis=0)

    start_col = start_s[...]
    esub = lax.broadcasted_iota(I32, (N_EXPERTS, tb), 0)
    rows = []
    for k in range(TOP_K):
        onehot = esub == idx_ref[k:k + 1, :]
        rows.append(jnp.sum(jnp.where(onehot, start_col, 0), axis=0, keepdims=True) + rank_ref[k:k + 1, :])
    dest_ref[0] = jnp.concatenate(rows, axis=0)


def _destinations(idx_t, rank_t, cnt_col, cnt_row, *, tb, n_blocks):
    _, t = idx_t.shape
    tb = min(tb, t)
    nip = (_max_items(n_blocks) + LANES - 1) // LANES * LANES
    kspec = pl.BlockSpec((TOP_K, tb), lambda i: (0, i))
    return pl.pallas_call(
        _dest_kernel,
        grid=(t // tb,),
        in_specs=[kspec, kspec,
                  pl.BlockSpec((N_EXPERTS, 1), lambda i: (0, 0)),
                  pl.BlockSpec((8, N_EXPERTS), lambda i: (0, 0))],
        out_specs=[pl.BlockSpec((1, TOP_K, tb), lambda i: (i, 0, 0)),
                   pl.BlockSpec((8, N_EXPERTS), lambda i: (0, 0)),
                   pl.BlockSpec((8, nip), lambda i: (0, 0))],
        out_shape=[jax.ShapeDtypeStruct((t // tb, TOP_K, tb), I32),
                   jax.ShapeDtypeStruct((8, N_EXPERTS), I32),
                   jax.ShapeDtypeStruct((8, nip), I32)],
        scratch_shapes=[pltpu.VMEM((N_EXPERTS, 1), I32)],
        compiler_params=_cparams("arbitrary"),
        name="dest",
    )(idx_t, rank_t, cnt_col, cnt_row)


_PAD_BITS = (64, 32, 16, 8)


def _sc_scatter_rows(rows, dest_win, n_out_rows):
    n_win, n_slots, win = dest_win.shape
    width = rows.shape[1]
    info = plsc.get_sparse_core_info()
    n_workers = info.num_cores * info.num_subcores
    per_worker = n_win // n_workers
    assert per_worker * n_workers == n_win and win <= LANES
    mesh = plsc.VectorSubcoreMesh(core_axis_name="c", subcore_axis_name="s")

    def body(rows_hbm, dest_hbm, out_hbm, idx_v, rows_v, sem):
        wid = lax.axis_index("s") * info.num_cores + lax.axis_index("c")

        @pl.loop(0, per_worker)
        def _(j):
            w = wid * per_worker + j
            pltpu.sync_copy(dest_hbm.at[w], idx_v)
            pltpu.sync_copy(rows_hbm.at[pl.ds(w * win, win)], rows_v)
            copies = [pltpu.async_copy(rows_v, out_hbm.at[idx_v.at[k]], sem) for k in range(n_slots)]
            for cp in copies:
                cp.wait()

    return pl.kernel(
        body,
        out_type=jax.ShapeDtypeStruct((n_out_rows, width), rows.dtype),
        mesh=mesh,
        scratch_types=[pltpu.VMEM((n_slots, win), I32), pltpu.VMEM((win, width), rows.dtype),
                       pltpu.SemaphoreType.DMA],
        name="sc_scatter",
    )(rows, dest_win)


def _pad_fill_kernel(meta_ref, xs_in, xs_hbm, zero_s, sem_z, *, e_per_step):
    del xs_in
    step = pl.program_id(0)
    zero_s[...] = jnp.zeros_like(zero_s)

    def tail_copy(c):
        row0 = pl.multiple_of((meta_ref[3, 0] + c) * MOE_BLOCK, MOE_BLOCK)
        return pltpu.make_async_copy(zero_s, xs_hbm.at[pl.ds(row0, MOE_BLOCK)], sem_z)

    @pl.when(step == 0)
    def _():
        for c in range(ITEM_BLOCKS - 1):
            tail_copy(c).start()
        for c in range(ITEM_BLOCKS - 1):
            tail_copy(c).wait()

    def pad_copies(e):
        cnt = meta_ref[0, e]
        off = meta_ref[1, e] + cnt
        rem = meta_ref[2, e] - cnt
        head = rem & (SUBLANES - 1)
        out = []
        for i in range(SUBLANES - 1):
            out.append((i < head,
                        pltpu.make_async_copy(zero_s.at[pl.ds(0, 1)], xs_hbm.at[pl.ds(off + i, 1)], sem_z)))
        off = off + head
        for bit in _PAD_BITS:
            out.append(((rem & bit) != 0,
                        pltpu.make_async_copy(zero_s.at[pl.ds(0, bit)],
                                              xs_hbm.at[pl.ds(pl.multiple_of(off, SUBLANES), bit)], sem_z)))
            off = off + (rem & bit)
        return out

    for j in range(e_per_step):
        for cond, c in pad_copies(step * e_per_step + j):
            pl.when(cond)(c.start)
    for j in range(e_per_step):
        for cond, c in pad_copies(step * e_per_step + j):
            pl.when(cond)(c.wait)


PAD_FILL_EXPERTS_PER_STEP = 8


def _pad_fill(meta, xs):
    half = xs.shape[1]
    return pl.pallas_call(
        functools.partial(_pad_fill_kernel, e_per_step=PAD_FILL_EXPERTS_PER_STEP),
        grid=(N_EXPERTS // PAD_FILL_EXPERTS_PER_STEP,),
        in_specs=[pl.BlockSpec(memory_space=pltpu.SMEM),
                  pl.BlockSpec(memory_space=pl.ANY)],
        out_specs=pl.BlockSpec(memory_space=pl.ANY),
        out_shape=jax.ShapeDtypeStruct(xs.shape, xs.dtype),
        input_output_aliases={1: 0},
        scratch_shapes=[pltpu.VMEM((MOE_BLOCK, half), PACKED), pltpu.SemaphoreType.DMA],
        compiler_params=_cparams("arbitrary"),
        name="pad_fill",
    )(meta, xs)


SC_LANES = 16
SC_PACK_BYTES = 64 * 1024


def _sc_pack_rows(w):
    r, c = w.shape
    h = c // 2
    info = plsc.get_sparse_core_info()
    n_workers = info.num_cores * info.num_subcores
    rb = SC_PACK_BYTES // (4 * c)
    per_worker = r // n_workers
    assert per_worker * n_workers == r and per_worker % rb == 0 and h % SC_LANES == 0
    mesh = plsc.VectorSubcoreMesh(core_axis_name="c", subcore_axis_name="s")

    def body(w_hbm, out_hbm, in_v, out_v):
        wid = lax.axis_index("s") * info.num_cores + lax.axis_index("c")

        @pl.loop(0, per_worker // rb)
        def _(j):
            r0 = wid * per_worker + j * rb
            pltpu.sync_copy(w_hbm.at[pl.ds(r0, rb)], in_v)

            @pl.loop(0, rb)
            def _(row):
                for k in range(h // SC_LANES):
                    a = in_v[row, pl.ds(k * SC_LANES, SC_LANES)]
                    b = in_v[row, pl.ds(h + k * SC_LANES, SC_LANES)]
                    packed = plsc.pack(a, b, format=plsc.PackFormat.INTERLEAVED)
                    out_v[row, pl.ds(k * SC_LANES, SC_LANES)] = plsc.bitcast(packed, PACKED)

            pltpu.sync_copy(out_v, out_hbm.at[pl.ds(r0, rb)])

    return pl.kernel(
        body,
        out_type=jax.ShapeDtypeStruct((r, h), PACKED),
        mesh=mesh,
        scratch_types=[pltpu.VMEM((rb, c), F32), pltpu.VMEM((rb, h), PACKED)],
        compiler_params=pltpu.CompilerParams(needs_layout_passes=False),
        name="sc_pack",
    )(w)


def _unpack_weight(dst, packed):
    hi, lo = _unpack_halves(packed)
    h = packed.shape[1]
    dst[:, 0:h] = lo.astype(BF16)
    dst[:, h:] = hi.astype(BF16)


def _experts_kernel(items_ref, xs_hbm, wg_hbm, wu_hbm, wd_hbm, ys_hbm,
                    xbuf, ybuf, wg_f, wu_f, wd_f, wg_s, wu_s, wd_s, sem_x, sem_y, sem_w):
    n_items = items_ref[3, 0]
    rows = ITEM_BLOCKS * MOE_BLOCK

    def w_copies(e, s):
        return [pltpu.make_async_copy(src.at[e], dst.at[s], sem_w.at[s])
                for src, dst in ((wg_hbm, wg_f), (wu_hbm, wu_f), (wd_hbm, wd_f))]

    def x_copy(item, s):
        row0 = pl.multiple_of(items_ref[1, item] * MOE_BLOCK, MOE_BLOCK)
        return pltpu.make_async_copy(xs_hbm.at[pl.ds(row0, rows)], xbuf.at[s], sem_x.at[s])

    def y_copies(item, s, fn):
        for k in range(ITEM_BLOCKS):
            row0 = pl.multiple_of((items_ref[1, item] + k) * MOE_BLOCK, MOE_BLOCK)
            cp = pltpu.make_async_copy(ybuf.at[s, pl.ds(k * MOE_BLOCK, MOE_BLOCK)],
                                       ys_hbm.at[pl.ds(row0, MOE_BLOCK)], sem_y.at[s])
            pl.when(k < items_ref[2, item])(functools.partial(fn, cp))

    def expert_of(item):
        return items_ref[0, jnp.minimum(item, n_items - 1)]

    def changes_at(item):
        return ((item < n_items) & (expert_of(item) != expert_of(item - 1))).astype(I32)

    @pl.when(n_items > 0)
    def _():
        x_copy(0, 0).start()
        for cp in w_copies(expert_of(0), 0):
            cp.start()

        @pl.when(changes_at(1) == 1)
        def _():
            for cp in w_copies(expert_of(1), 1):
                cp.start()

    def item_body(i, ordinal):
        slot = i % 2
        prev = jnp.maximum(i - 1, 0)
        e = expert_of(i)
        new_expert = (i == 0) | (e != expert_of(prev))
        c1 = changes_at(i + 1)
        c2 = changes_at(i + 2)

        @pl.when(i + 1 < n_items)
        def _():
            x_copy(i + 1, 1 - slot).start()

        @pl.when(new_expert)
        def _():
            wslot = ordinal % WEIGHT_SLOTS
            for cp in w_copies(e, wslot):
                cp.wait()
            _unpack_weight(wg_s, wg_f[wslot])
            _unpack_weight(wu_s, wu_f[wslot])
            _unpack_weight(wd_s, wd_f[wslot])

        @pl.when(c2 == 1)
        def _():
            for cp in w_copies(expert_of(i + 2), (ordinal + c1 + 1) % WEIGHT_SLOTS):
                cp.start()

        x_copy(i, slot).wait()
        xa, xb = _unpack_halves(xbuf[slot])
        xa = xa.astype(BF16)
        xb = xb.astype(BF16)
        half = xa.shape[1]
        g = _dot(xa, wg_s[0:half]) + _dot(xb, wg_s[half:])
        u = _dot(xa, wu_s[0:half]) + _dot(xb, wu_s[half:])
        y = _dot((_silu(g) * u).astype(BF16), wd_s[...])
        ybuf[slot] = _pack_halves(y[:, :half], y[:, half:])
        y_copies(i, slot, lambda cp: cp.start())

        @pl.when(i > 0)
        def _():
            y_copies(prev, 1 - slot, lambda cp: cp.wait())

        return ordinal + c1

    lax.fori_loop(0, n_items, item_body, jnp.int32(0))

    @pl.when(n_items > 0)
    def _():
        last = n_items - 1
        y_copies(last, last % 2, lambda cp: cp.wait())


def _experts(items, xs, w_gate, w_up, w_down, *, n_blocks):
    half = xs.shape[1]
    e, d, f = w_gate.shape[0], w_gate.shape[1], w_down.shape[1]
    rows = ITEM_BLOCKS * MOE_BLOCK
    any_spec = pl.BlockSpec(memory_space=pl.ANY)
    return pl.pallas_call(
        _experts_kernel,
        grid_spec=pltpu.PrefetchScalarGridSpec(
            num_scalar_prefetch=1,
            grid=(1,),
            in_specs=[any_spec, any_spec, any_spec, any_spec],
            out_specs=any_spec,
            scratch_shapes=[pltpu.VMEM((2, rows, half), PACKED), pltpu.VMEM((2, rows, half), PACKED),
                            pltpu.VMEM((WEIGHT_SLOTS, d, f // 2), PACKED),
                            pltpu.VMEM((WEIGHT_SLOTS, d, f // 2), PACKED),
                            pltpu.VMEM((WEIGHT_SLOTS, f, d // 2), PACKED),
                            pltpu.VMEM((d, f), BF16), pltpu.VMEM((d, f), BF16), pltpu.VMEM((f, d), BF16),
                            pltpu.SemaphoreType.DMA((2,)), pltpu.SemaphoreType.DMA((2,)),
                            pltpu.SemaphoreType.DMA((WEIGHT_SLOTS,))]),
        out_shape=jax.ShapeDtypeStruct((n_blocks * MOE_BLOCK, half), PACKED),
        compiler_params=_cparams("arbitrary"),
        name="experts",
    )(items, xs, w_gate, w_up, w_down)


def _sc_gather_rows(table, idx):
    n_idx = idx.shape[0]
    width = table.shape[1]
    info = plsc.get_sparse_core_info()
    n_workers = info.num_cores * info.num_subcores
    per_worker = n_idx // n_workers
    assert per_worker * n_workers == n_idx and per_worker % (SC_GATHER_BUFS * SC_WINDOW) == 0
    mesh = plsc.VectorSubcoreMesh(core_axis_name="c", subcore_axis_name="s")

    def body(table_hbm, idx_hbm, out_hbm, idx_v, rows_v, sem_g, sem_o):
        wid = lax.axis_index("s") * info.num_cores + lax.axis_index("c")
        base = wid * per_worker

        @pl.loop(0, per_worker // (SC_GATHER_BUFS * SC_WINDOW))
        def _(it):
            offs = [base + (it * SC_GATHER_BUFS + b) * SC_WINDOW for b in range(SC_GATHER_BUFS)]
            gathers = []
            for b, off in enumerate(offs):
                pltpu.sync_copy(idx_hbm.at[pl.ds(off, SC_WINDOW)], idx_v.at[b])
                gathers.append(pltpu.async_copy(table_hbm.at[idx_v.at[b]], rows_v.at[b], sem_g.at[b]))
            writes = []
            for b, off in enumerate(offs):
                gathers[b].wait()
                writes.append(pltpu.async_copy(rows_v.at[b], out_hbm.at[pl.ds(off, SC_WINDOW)], sem_o.at[b]))
            for cp in writes:
                cp.wait()

    return pl.kernel(
        body,
        out_type=jax.ShapeDtypeStruct((n_idx, width), table.dtype),
        mesh=mesh,
        scratch_types=[pltpu.VMEM((SC_GATHER_BUFS, SC_WINDOW), I32),
                       pltpu.VMEM((SC_GATHER_BUFS, SC_WINDOW, width), table.dtype),
                       pltpu.SemaphoreType.DMA((SC_GATHER_BUFS,)), pltpu.SemaphoreType.DMA((SC_GATHER_BUFS,))],
        name="sc_gather",
    )(table, idx)


def _combine_kernel(hp_ref, x1_ref, g2_ref, w_ref, sgw_ref, suw_ref, sdw_ref, yg_ref, *rest):
    o_ref = rest[-1]
    xa, xb = _unpack_halves(hp_ref[...])
    xa = xa.astype(BF16)
    xb = xb.astype(BF16)
    half = xa.shape[1]
    tb = xa.shape[0]
    g = _dot(xa, sgw_ref[0:half]) + _dot(xb, sgw_ref[half:])
    u = _dot(xa, suw_ref[0:half]) + _dot(xb, suw_ref[half:])
    shared = _dot((_silu(g) * u).astype(BF16), sdw_ref[...])

    w = w_ref[...]
    acc_a = jnp.zeros((tb, half), F32)
    acc_b = jnp.zeros((tb, half), F32)
    for k in range(TOP_K):
        ya, yb = _unpack_halves(yg_ref[0, k])
        acc_a += ya * w[:, k:k + 1]
        acc_b += yb * w[:, k:k + 1]
    g2 = g2_ref[0]
    o_ref[:, 0:half] = x1_ref[:, 0:half] + g2[:, 0:half] * (acc_a + shared[:, 0:half])
    o_ref[:, half:] = x1_ref[:, half:] + g2[:, half:] * (acc_b + shared[:, half:])


def _combine(hp, x1, g2, w_tok, sgw, suw, sdw, yg, out_prev, *, tb, seq_len, first_step):
    t, half = hp.shape
    d = 2 * half
    per_seq = seq_len // tb
    f = sgw.shape[1]
    s0 = first_step
    in_specs = [pl.BlockSpec((tb, half), lambda i: (s0 + i, 0)),
                pl.BlockSpec((tb, d), lambda i: (s0 + i, 0)),
                pl.BlockSpec((1, 1, d), lambda i: ((s0 + i) // per_seq, 0, 0)),
                pl.BlockSpec((tb, TOP_K), lambda i: (s0 + i, 0)),
                pl.BlockSpec((d, f), lambda i: (0, 0)),
                pl.BlockSpec((d, f), lambda i: (0, 0)),
                pl.BlockSpec((f, d), lambda i: (0, 0)),
                pl.BlockSpec((1, TOP_K, tb, half), lambda i: (i, 0, 0, 0))]
    args = [hp, x1, g2, w_tok, sgw, suw, sdw, yg]
    aliases = {}
    if out_prev is not None:
        in_specs.append(pl.BlockSpec(memory_space=pl.ANY))
        args.append(out_prev)
        aliases = {len(args) - 1: 0}
    return pl.pallas_call(
        _combine_kernel,
        grid=(yg.shape[0],),
        in_specs=in_specs,
        out_specs=pl.BlockSpec((tb, d), lambda i: (s0 + i, 0)),
        out_shape=jax.ShapeDtypeStruct((t, d), F32),
        input_output_aliases=aliases,
        compiler_params=_cparams("arbitrary"),
        name="combine",
    )(*args)


def _rope_tables(l):
    rows = l // GRID_W
    r = jnp.repeat(jnp.arange(rows), GRID_W).astype(F32)
    col = jnp.tile(jnp.arange(GRID_W), rows).astype(F32)
    n_f = HEAD_DIM // 4
    freqs = ROPE_THETA ** (-jnp.arange(n_f, dtype=F32) / n_f)
    ang = jnp.concatenate([r[:, None] * freqs, col[:, None] * freqs], axis=-1)
    ang = jnp.tile(jnp.repeat(ang, 2, axis=1), (1, LANES // HEAD_DIM))
    sign = jnp.where(jnp.arange(LANES) % 2 == 0, -1.0, 1.0).astype(F32)
    return jnp.cos(ang), jnp.sin(ang) * sign


def kernel(x, c, ctx, c_ctx, w_mod, b_mod, norm1_w, norm2_w, w_in, q_norm_w, k_norm_w, ret_decay_fwd,
           ret_decay_bwd, w_out, router_w, router_bias, exp_w_gate, exp_w_up, exp_w_down, sh_w_gate,
           sh_w_up, sh_w_down):
    b, l, d = x.shape
    lc = ctx.shape[1]
    t = b * l
    assert w_mod.shape[0] == 1, "single layer"
    assert l % CHUNK == 0 and lc % CHUNK == 0 and l % GRID_W == 0

    n_e, _, d_ff = exp_w_gate.shape[1:]
    wg_p = _sc_pack_rows(exp_w_gate[0].reshape(n_e * d, d_ff)).reshape(n_e, d, d_ff // 2)
    wu_p = _sc_pack_rows(exp_w_up[0].reshape(n_e * d, d_ff)).reshape(n_e, d, d_ff // 2)
    wd_p = _sc_pack_rows(exp_w_down[0].reshape(n_e * d_ff, d)).reshape(n_e, d_ff, d // 2)

    rows = (b + 1 + 7) // 8 * 8
    cc = jnp.zeros((rows, d), F32).at[:b].set(c).at[b].set(c_ctx)
    mod = _modulation(cc, w_mod[0], b_mod[0])
    sh1, sc1, g1, sh2, sc2, g2 = [mod[:b, i * d:(i + 1) * d].reshape(b, 1, d) for i in range(6)]
    shc = mod[b, 0:d].reshape(1, 1, d)
    scc = mod[b, d:2 * d].reshape(1, 1, d)

    wi = w_in[0].astype(BF16)
    qnw = jnp.tile(q_norm_w[0], LANES // HEAD_DIM).reshape(1, LANES)
    knw = jnp.tile(k_norm_w[0], LANES // HEAD_DIM).reshape(1, LANES)
    cos, sin = _rope_tables(l)
    n1 = norm1_w[0].reshape(1, d)

    cklo, ckhi, cvlo, cvhi, crk, crv = _projection(
        ctx, shc, scc, n1, wi, qnw, knw, cos[:lc], sin[:lc], rope=False, with_q=False, tm=TILE_PROJ)
    klo, khi, vlo, vhi, rk, rv, q, rq, sg = _projection(
        x, sh1, sc1, n1, wi, qnw, knw, cos, sin, rope=True, with_q=True, tm=TILE_PROJ)

    attn = _attention(q, klo, khi, vlo, vhi, cklo, ckhi, cvlo, cvhi, tq=TILE_ATTN_Q)
    dec_f = jnp.repeat(ret_decay_fwd[0].astype(F32), HEAD_DIM).reshape(1, RET_W)
    dec_b = jnp.repeat(ret_decay_bwd[0].astype(F32), HEAD_DIM).reshape(1, RET_W)
    ret = _retention(rq, rk, rv, sg, crk, crv, dec_f, dec_b)

    wo = w_out[0].astype(BF16)
    r_hi, r_lo = _split(router_w[0].T)
    x1, hp, logits_t = _out_projection(attn, ret, x, wo[:ATTN_W], wo[ATTN_W:], g1, sh2, sc2,
                                       norm2_w[0].reshape(1, d), r_hi, r_lo, tm=TILE_OUT)

    idx_t, w_t, rank_t, cnt_col, cnt_row = _route(logits_t, router_bias[0].reshape(N_EXPERTS, 1), tb=TILE_TOKENS)
    n_blocks = -(-(t * TOP_K) // MOE_BLOCK) + N_EXPERTS
    tb = TILE_TOKENS
    dest, meta, items = _destinations(idx_t, rank_t, cnt_col, cnt_row, tb=tb, n_blocks=n_blocks)
    steps, _, tbe = dest.shape
    dest_win = dest.reshape(steps, TOP_K, tbe // SC_WINDOW, SC_WINDOW).transpose(0, 2, 1, 3)
    dest_win = dest_win.reshape(t // SC_WINDOW, TOP_K, SC_WINDOW)
    xs = _sc_scatter_rows(hp, dest_win, (n_blocks + ITEM_BLOCKS - 1) * MOE_BLOCK)
    xs = _pad_fill(meta, xs)
    ys = _experts(items, xs, wg_p, wu_p, wd_p, n_blocks=n_blocks)
    parts = COMBINE_PARTS if steps % COMBINE_PARTS == 0 else 1
    steps_part = steps // parts
    x1f, w_tok = x1.reshape(t, d), w_t.T
    sgw, suw, sdw = sh_w_gate[0].astype(BF16), sh_w_up[0].astype(BF16), sh_w_down[0].astype(BF16)
    out = None
    for p in range(parts):
        idx = dest[p * steps_part:(p + 1) * steps_part].reshape(-1)
        yg = _sc_gather_rows(ys, idx).reshape(steps_part, TOP_K, tbe, d // 2)
        out = _combine(hp, x1f, g2, w_tok, sgw, suw, sdw, yg, out, tb=tbe, seq_len=l, first_step=p * steps_part)
    return out.reshape(b, l, d)
```

```python
import functools

import jax
import jax.numpy as jnp
from jax import lax
from jax.experimental import pallas as pl
from jax.experimental.pallas import tpu as pltpu
from jax.experimental.pallas import tpu_sc as plsc

F32 = jnp.float32
BF16 = jnp.bfloat16
I32 = jnp.int32
U32 = jnp.uint32
PACKED = jnp.int32

HEAD_DIM = 64
LANES = 128
SUBLANES = 8
ATTN_HEADS = 8
ATTN_KV_HEADS = 2
GQA = ATTN_HEADS // ATTN_KV_HEADS
RET_HEADS = 8
ATTN_W = ATTN_HEADS * HEAD_DIM
KV_W = ATTN_KV_HEADS * HEAD_DIM
RET_W = RET_HEADS * HEAD_DIM
RET_PAIRS = RET_W // LANES
CHUNK = 128
GRID_W = 64
ROPE_THETA = 10000.0
N_EXPERTS = 256
TOP_K = 8
N_GROUPS = 8
GROUP_SIZE = N_EXPERTS // N_GROUPS
TOPK_GROUPS = 4
ROUTED_SCALE = 2.5
MOE_BLOCK = 128
EPS = 1e-6
QK_SCALE = HEAD_DIM ** -0.5
LOG2_E = 1.4426950408889634
KEY_TILE = 256
BOUND_MARGIN = 1.02
MAX_STREAM_SHIFT = 56.0

OFF_AK = 0
OFF_AV = OFF_AK + KV_W
OFF_RK = OFF_AV + KV_W
OFF_RV = OFF_RK + RET_W
CTX_KV_COLS = OFF_RV + RET_W
OFF_AQ = CTX_KV_COLS
OFF_RQ = OFF_AQ + ATTN_W
OFF_RG = OFF_RQ + RET_W
IN_COLS = OFF_RG + RET_W

VMEM_LIMIT = 52 * 1024 * 1024

SUB_ROWS = 256
TILE_PROJ = 1024
TILE_ATTN_Q = 1024
TILE_OUT = 512
TILE_TOKENS = 256
ITEM_BLOCKS = 5
COMBINE_PARTS = 4
SC_GATHER_BUFS = 2
WEIGHT_SLOTS = 4
SC_WINDOW = 64
HI_MASK = 0xFFFF0000


def _cparams(*sem):
    return pltpu.CompilerParams(dimension_semantics=sem, vmem_limit_bytes=VMEM_LIMIT)


def _split(a):
    hi = a.astype(BF16)
    lo = (a - hi.astype(F32)).astype(BF16)
    return hi, lo


def _dot(a, b):
    return jnp.dot(a, b, preferred_element_type=F32)


def _dot_nt(a, b):
    return lax.dot_general(a, b, (((1,), (1,)), ((), ())), preferred_element_type=F32)


def _sigmoid(v):
    return 1.0 / (1.0 + jnp.exp(-v))


def _silu(v):
    return v * _sigmoid(v)


def _pack_halves(a, b):
    ua = lax.bitcast_convert_type(a.astype(BF16).astype(F32), U32)
    ub = lax.bitcast_convert_type(b.astype(BF16).astype(F32), U32)
    return lax.bitcast_convert_type((ua & jnp.uint32(HI_MASK)) | (ub >> 16), PACKED)


def _unpack_halves(p):
    u = lax.bitcast_convert_type(p, U32)
    a = lax.bitcast_convert_type(u & jnp.uint32(HI_MASK), F32)
    b = lax.bitcast_convert_type(u << 16, F32)
    return a, b


def _mod_kernel(c_ref, w_ref, b_ref, o_ref):
    s_hi, s_lo = _split(_silu(c_ref[...]))
    w_hi, w_lo = _split(w_ref[...])
    o_ref[...] = _dot(s_hi, w_hi) + _dot(s_hi, w_lo) + _dot(s_lo, w_hi) + b_ref[...]


def _modulation(cc, w_mod, b_mod):
    rows, d = cc.shape
    n = w_mod.shape[1]
    tn = 768
    return pl.pallas_call(
        _mod_kernel,
        grid=(n // tn,),
        in_specs=[pl.BlockSpec((rows, d), lambda j: (0, 0)),
                  pl.BlockSpec((d, tn), lambda j: (0, j)),
                  pl.BlockSpec((1, tn), lambda j: (0, j))],
        out_specs=pl.BlockSpec((rows, tn), lambda j: (0, j)),
        out_shape=jax.ShapeDtypeStruct((rows, n), F32),
        compiler_params=_cparams("arbitrary"),
        name="mod",
    )(cc, w_mod, b_mod.reshape(1, n))


def _segment_ones():
    r = lax.broadcasted_iota(I32, (LANES, LANES), 0) // HEAD_DIM
    c = lax.broadcasted_iota(I32, (LANES, LANES), 1) // HEAD_DIM
    return jnp.where(r == c, 1.0, 0.0).astype(BF16)


def _head_mean_sq(v, seg):
    hi, lo = _split(v * v)
    return (_dot(hi, seg) + _dot(lo, seg)) * (1.0 / HEAD_DIM)


def _proj_kernel(x_ref, sh_ref, sc_ref, nw_ref, wi_ref, qnw_ref, knw_ref, cos_ref, sin_ref,
                 *out_refs, rope, with_q):
    if with_q:
        klo_ref, khi_ref, vlo_ref, vhi_ref, rk_ref, rv_ref, q_ref, rq_ref, sg_ref = out_refs
    else:
        klo_ref, khi_ref, vlo_ref, vhi_ref, rk_ref, rv_ref = out_refs
    tm = x_ref.shape[1]
    sub = min(tm, SUB_ROWS)
    seg = _segment_ones()
    lane = lax.broadcasted_iota(I32, (sub, LANES), 1)
    low_half = lane < HEAD_DIM
    even = (lane & 1) == 0

    for r0 in range(0, tm, sub):
        rows = slice(r0, r0 + sub)
        x = x_ref[0, rows, :]
        h = x * lax.rsqrt(jnp.mean(x * x, axis=-1, keepdims=True) + EPS) * nw_ref[...]
        h = h * (1.0 + sc_ref[0]) + sh_ref[0]
        z = _dot(h.astype(BF16), wi_ref[...])

        def norm_rope(v, w128, rows=rows):
            v = v * lax.rsqrt(_head_mean_sq(v, seg) + EPS) * w128
            if rope:
                swapped = jnp.where(even, pltpu.roll(v, LANES - 1, 1), pltpu.roll(v, 1, 1))
                v = v * cos_ref[rows, :] + swapped * sin_ref[rows, :]
            return v

        k = norm_rope(z[:, OFF_AK:OFF_AK + KV_W], knw_ref[...])
        ksw = pltpu.roll(k, HEAD_DIM, 1)
        klo_ref[0, 0, :, rows] = jnp.transpose(jnp.where(low_half, k, 0.0)).astype(BF16)
        khi_ref[0, 0, :, rows] = jnp.transpose(jnp.where(low_half, 0.0, ksw)).astype(BF16)
        klo_ref[0, 1, :, rows] = jnp.transpose(jnp.where(low_half, ksw, 0.0)).astype(BF16)
        khi_ref[0, 1, :, rows] = jnp.transpose(jnp.where(low_half, 0.0, k)).astype(BF16)
        v = z[:, OFF_AV:OFF_AV + KV_W]
        vsw = pltpu.roll(v, HEAD_DIM, 1)
        vlo_ref[0, 0, rows, :] = jnp.where(low_half, v, 0.0).astype(BF16)
        vhi_ref[0, 0, rows, :] = jnp.where(low_half, 0.0, vsw).astype(BF16)
        vlo_ref[0, 1, rows, :] = jnp.where(low_half, vsw, 0.0).astype(BF16)
        vhi_ref[0, 1, rows, :] = jnp.where(low_half, 0.0, v).astype(BF16)
        rk_ref[0, rows, :] = (z[:, OFF_RK:OFF_RK + RET_W] * QK_SCALE).astype(BF16)
        rv_ref[0, rows, :] = z[:, OFF_RV:OFF_RV + RET_W].astype(BF16)
        if with_q:
            for j in range(ATTN_W // LANES):
                qj = norm_rope(z[:, OFF_AQ + j * LANES:OFF_AQ + (j + 1) * LANES], qnw_ref[...])
                q_ref[0, rows, j * LANES:(j + 1) * LANES] = (qj * (QK_SCALE * LOG2_E)).astype(BF16)
            rq_ref[0, rows, :] = z[:, OFF_RQ:OFF_RQ + RET_W].astype(BF16)
            sg_ref[0, rows, :] = _silu(z[:, OFF_RG:OFF_RG + RET_W]).astype(BF16)


def _projection(x, shift, scale, norm_w, wi_bf16, qnw, knw, cos, sin, *, rope, with_q, tm):
    b, l, d = x.shape
    tm = min(tm, l)
    ncols = IN_COLS if with_q else CTX_KV_COLS
    per_batch = shift.shape[0] > 1
    mod_idx = (lambda bi, i: (bi, 0, 0)) if per_batch else (lambda bi, i: (0, 0, 0))
    kv_shape = jax.ShapeDtypeStruct((b, ATTN_KV_HEADS, l, LANES), BF16)
    kv_spec = pl.BlockSpec((1, ATTN_KV_HEADS, tm, LANES), lambda bi, i: (bi, 0, i, 0))
    kt_shape = jax.ShapeDtypeStruct((b, ATTN_KV_HEADS, LANES, l), BF16)
    kt_spec = pl.BlockSpec((1, ATTN_KV_HEADS, LANES, tm), lambda bi, i: (bi, 0, 0, i))
    w_shape = jax.ShapeDtypeStruct((b, l, RET_W), BF16)
    w_spec = pl.BlockSpec((1, tm, RET_W), lambda bi, i: (bi, i, 0))
    out_shape = [kt_shape] * 2 + [kv_shape] * 2 + [w_shape] * 2
    out_specs = [kt_spec] * 2 + [kv_spec] * 2 + [w_spec] * 2
    if with_q:
        out_shape += [w_shape] * 3
        out_specs += [w_spec] * 3
    return pl.pallas_call(
        functools.partial(_proj_kernel, rope=rope, with_q=with_q),
        grid=(b, l // tm),
        in_specs=[pl.BlockSpec((1, tm, d), lambda bi, i: (bi, i, 0)),
                  pl.BlockSpec((1, 1, d), mod_idx),
                  pl.BlockSpec((1, 1, d), mod_idx),
                  pl.BlockSpec((1, d), lambda bi, i: (0, 0)),
                  pl.BlockSpec((d, ncols), lambda bi, i: (0, 0)),
                  pl.BlockSpec((1, LANES), lambda bi, i: (0, 0)),
                  pl.BlockSpec((1, LANES), lambda bi, i: (0, 0)),
                  pl.BlockSpec((tm, LANES), lambda bi, i: (i, 0)),
                  pl.BlockSpec((tm, LANES), lambda bi, i: (i, 0))],
        out_specs=out_specs,
        out_shape=out_shape,
        compiler_params=_cparams("arbitrary", "arbitrary"),
        name="proj_latent" if with_q else "proj_ctx",
    )(x, shift, scale, norm_w, wi_bf16, qnw, knw, cos, sin)


def _attn_kernel(shift_ref, q_ref, klo_ref, khi_ref, vlo_ref, vhi_ref, cklo_ref, ckhi_ref, cvlo_ref, cvhi_ref,
                 o_ref, kl_s, kh_s, va_s, *, l, lc, streaming):
    lk = l + lc

    @pl.when(pl.program_id(2) == 0)
    def _():
        kl_s[:, 0:l] = klo_ref[0, 0]
        kl_s[:, l:lk] = cklo_ref[0, 0]
        kh_s[:, 0:l] = khi_ref[0, 0]
        kh_s[:, l:lk] = ckhi_ref[0, 0]
        lane = lax.broadcasted_iota(I32, (lk, LANES), 1)
        ones_lo = jnp.where(lane < HEAD_DIM, 1.0, 0.0).astype(BF16)
        ones_hi = jnp.where(lane < HEAD_DIM, 0.0, 1.0).astype(BF16)
        for g, (v_ref, cv_ref, ones) in enumerate(((vlo_ref, cvlo_ref, ones_lo), (vhi_ref, cvhi_ref, ones_hi),
                                                   (vlo_ref, cvlo_ref, ones_lo), (vhi_ref, cvhi_ref, ones_hi))):
            v_col, one_col = (0, LANES) if g < 2 else (LANES, 0)
            va_s[g, 0:l, v_col:v_col + LANES] = v_ref[0, 0]
            va_s[g, l:lk, v_col:v_col + LANES] = cv_ref[0, 0]
            va_s[g, :, one_col:one_col + LANES] = ones

    q = q_ref[0]
    acc = []
    for g in range(GQA):
        qp = q[:, (g // 2) * LANES:(g // 2 + 1) * LANES]
        k_s = kl_s if g % 2 == 0 else kh_s
        if streaming:
            shift = shift_ref[0]
            o = None
            for c in range(0, lk, KEY_TILE):
                hi = min(c + KEY_TILE, lk)
                p = jnp.exp2(_dot(qp, k_s[:, c:hi]) - shift).astype(BF16)
                t = _dot(p, va_s[g, c:hi])
                o = t if o is None else o + t
        else:
            s = _dot(qp, k_s[...])
            p = jnp.exp2(s - jnp.max(s, axis=-1, keepdims=True)).astype(BF16)
            o = _dot(p, va_s[g])
        acc.append(o)
    out_a = acc[0] + acc[1]
    out_b = acc[2] + acc[3]
    o_ref[0, :, 0:LANES] = (out_a[:, 0:LANES] / out_a[:, LANES:2 * LANES]).astype(BF16)
    o_ref[0, :, LANES:2 * LANES] = (out_b[:, LANES:2 * LANES] / out_b[:, 0:LANES]).astype(BF16)


def _attention(shift, q, klo, khi, vlo, vhi, cklo, ckhi, cvlo, cvhi, *, tq, streaming):
    b, l, _ = q.shape
    lc = cvlo.shape[2]
    lk = l + lc
    tq = min(tq, l)
    gw = GQA * HEAD_DIM
    kt_spec = pl.BlockSpec((1, 1, LANES, l), lambda bi, h, i: (bi, h, 0, 0))
    kv_spec = pl.BlockSpec((1, 1, l, LANES), lambda bi, h, i: (bi, h, 0, 0))
    ckt_spec = pl.BlockSpec((1, 1, LANES, lc), lambda bi, h, i: (bi, h, 0, 0))
    ckv_spec = pl.BlockSpec((1, 1, lc, LANES), lambda bi, h, i: (bi, h, 0, 0))
    return pl.pallas_call(
        functools.partial(_attn_kernel, l=l, lc=lc, streaming=streaming),
        grid=(b, ATTN_KV_HEADS, l // tq),
        in_specs=([pl.BlockSpec(memory_space=pltpu.SMEM),
                   pl.BlockSpec((1, tq, gw), lambda bi, h, i: (bi, i, h))] + [kt_spec] * 2 + [kv_spec] * 2
                  + [ckt_spec] * 2 + [ckv_spec] * 2),
        out_specs=pl.BlockSpec((1, tq, gw), lambda bi, h, i: (bi, i, h)),
        out_shape=jax.ShapeDtypeStruct((b, l, ATTN_W), BF16),
        scratch_shapes=[pltpu.VMEM((LANES, lk), BF16), pltpu.VMEM((LANES, lk), BF16),
                        pltpu.VMEM((GQA, lk, 2 * LANES), BF16)],
        compiler_params=_cparams("arbitrary", "arbitrary", "arbitrary"),
        name="attn_stream" if streaming else "attn",
    )(shift, q, klo, khi, vlo, vhi, cklo, ckhi, cvlo, cvhi)


def _log_sigmoid(v):
    return jnp.minimum(v, 0.0) - jnp.log(1.0 + jnp.exp(-jnp.abs(v)))


def _ret_kernel(rq_ref, rk_ref, rv_ref, sg_ref, crk_ref, crv_ref, df_ref, db_ref, o_ref,
                m_s, xi_s, zeta_s, kv_s, st_s, *, l, lc):
    n = l // CHUNK
    nc = lc // CHUNK
    lgf = _log_sigmoid(df_ref[...])
    lgb = _log_sigmoid(db_ref[...])
    pos = lax.broadcasted_iota(I32, (CHUNK, LANES), 0).astype(F32)
    row = lax.broadcasted_iota(I32, (CHUNK, CHUNK), 0)
    col = lax.broadcasted_iota(I32, (CHUNK, CHUNK), 1)
    diff = (row - col).astype(F32)
    g_chunk = []
    for p in range(RET_PAIRS):
        cols = slice(p * LANES, (p + 1) * LANES)
        lf, lb = lgf[:, cols], lgb[:, cols]
        xi_s[p, :, 0:LANES] = jnp.exp((pos + 1.0) * lf)
        xi_s[p, :, LANES:] = jnp.exp((CHUNK - pos) * lb)
        zeta_s[p, :, 0:LANES] = jnp.exp((CHUNK - 1.0 - pos) * lf)
        zeta_s[p, :, LANES:] = jnp.exp(pos * lb)
        g_chunk.append((jnp.exp(CHUNK * lf), jnp.exp(CHUNK * lb)))
        for j in range(2):
            h = 2 * p + j
            hf = lgf[:, h * HEAD_DIM:h * HEAD_DIM + 1]
            hb = lgb[:, h * HEAD_DIM:h * HEAD_DIM + 1]
            m_s[p, :, j * CHUNK:(j + 1) * CHUNK] = jnp.where(
                diff > 0, jnp.exp(diff * hf), jnp.where(diff < 0, jnp.exp(-diff * hb), 2.0))

    lane = lax.broadcasted_iota(I32, (CHUNK, LANES), 1)
    low_half = lane < HEAD_DIM
    diag = (lax.broadcasted_iota(I32, (LANES, LANES), 0) // HEAD_DIM
            == lax.broadcasted_iota(I32, (LANES, LANES), 1) // HEAD_DIM)
    seg = jnp.where(diag, 1.0, 0.0).astype(BF16)
    seg2 = jnp.concatenate([seg, seg], axis=0)
    diag2 = jnp.concatenate([diag, diag], axis=0)

    def split_heads(a):
        zero = jnp.zeros_like(a)
        return jnp.concatenate([jnp.where(low_half, a, zero), jnp.where(low_half, zero, a)], axis=0)

    def contrib(k_ref, v_ref, r0, p):
        cols = slice(p * LANES, (p + 1) * LANES)
        kp = k_ref[0, pl.ds(r0, CHUNK), cols].astype(F32)
        kz = jnp.concatenate([kp, kp], axis=1) * zeta_s[p]
        kv = _dot(jnp.transpose(kz).astype(BF16), v_ref[0, pl.ds(r0, CHUNK), cols])
        return jnp.where(diag2, kv, 0.0)

    for c in range(nc):
        for p in range(RET_PAIRS):
            kv_s[c, p] = contrib(crk_ref, crv_ref, c * CHUNK, p)

    def contrib_body(c, carry):
        r0 = pl.multiple_of(c * CHUNK, CHUNK)
        for p in range(RET_PAIRS):
            kv_s[nc + c, p] = contrib(rk_ref, rv_ref, r0, p)
        return carry

    lax.fori_loop(0, n, contrib_body, 0, unroll=4)

    for p in range(RET_PAIRS):
        gf, gb = g_chunk[p]
        sf = jnp.zeros((LANES, LANES), F32)
        sb = jnp.zeros((LANES, LANES), F32)
        for c in range(nc):
            sf = gf * sf + kv_s[c, p, 0:LANES]
            sb = gb * sb + kv_s[nc - 1 - c, p, LANES:]

        def fwd_scan(c, s, p=p, gf=gf):
            st_s[c, p, 0:LANES] = s.astype(BF16)
            return gf * s + kv_s[nc + c, p, 0:LANES]

        def bwd_scan(j, s, p=p, gb=gb):
            c = n - 1 - j
            st_s[c, p, LANES:] = s.astype(BF16)
            return gb * s + kv_s[nc + c, p, LANES:]

        lax.fori_loop(0, n, fwd_scan, sf)
        lax.fori_loop(0, n, bwd_scan, sb)

    def out_body(c, carry):
        r0 = pl.multiple_of(c * CHUNK, CHUNK)
        for p in range(RET_PAIRS):
            cols = slice(p * LANES, (p + 1) * LANES)
            qp = rq_ref[0, pl.ds(r0, CHUNK), cols]
            kp = rk_ref[0, pl.ds(r0, CHUNK), cols]
            vp = rv_ref[0, pl.ds(r0, CHUNK), cols]
            s2 = _dot_nt(qp, split_heads(kp))
            a2 = (s2 * m_s[p]).astype(BF16)
            y = _dot(a2, split_heads(vp))
            qf = qp.astype(F32)
            qx = (jnp.concatenate([qf, qf], axis=1) * xi_s[p]).astype(BF16)
            y += _dot(qx, st_s[c, p])
            hi, lo = _split(y * y)
            ms = _dot(jnp.concatenate([hi, lo], axis=1), seg2) * (1.0 / HEAD_DIM)
            out = y * lax.rsqrt(ms + EPS) * sg_ref[0, pl.ds(r0, CHUNK), cols].astype(F32)
            o_ref[0, pl.ds(r0, CHUNK), cols] = out.astype(BF16)
        return carry

    lax.fori_loop(0, n, out_body, 0, unroll=4)


def _retention(rq, rk, rv, sg, crk, crv, dec_f, dec_b):
    b, l, _ = rq.shape
    lc = crk.shape[1]
    n = l // CHUNK
    nc = lc // CHUNK
    spec = pl.BlockSpec((1, l, RET_W), lambda bi: (bi, 0, 0))
    cspec = pl.BlockSpec((1, lc, RET_W), lambda bi: (bi, 0, 0))
    dspec = pl.BlockSpec((1, RET_W), lambda bi: (0, 0))
    return pl.pallas_call(
        functools.partial(_ret_kernel, l=l, lc=lc),
        grid=(b,),
        in_specs=[spec, spec, spec, spec, cspec, cspec, dspec, dspec],
        out_specs=spec,
        out_shape=jax.ShapeDtypeStruct((b, l, RET_W), BF16),
        scratch_shapes=[pltpu.VMEM((RET_PAIRS, CHUNK, 2 * CHUNK), F32),
                        pltpu.VMEM((RET_PAIRS, CHUNK, 2 * LANES), F32),
                        pltpu.VMEM((RET_PAIRS, CHUNK, 2 * LANES), F32),
                        pltpu.VMEM((nc + n, RET_PAIRS, 2 * LANES, LANES), F32),
                        pltpu.VMEM((n, RET_PAIRS, 2 * LANES, LANES), BF16)],
        compiler_params=_cparams("arbitrary"),
        name="ret",
    )(rq, rk, rv, sg, crk, crv, dec_f, dec_b)


def _out_kernel(attn_ref, ret_ref, x_ref, wa_ref, wr_ref, g1_ref, sh_ref, sc_ref, nw_ref, rhi_ref, rlo_ref,
                x1_ref, hp_ref, lg_ref):
    y = _dot(attn_ref[0], wa_ref[...]) + _dot(ret_ref[0], wr_ref[...])
    x1 = x_ref[0] + g1_ref[0] * y
    x1_ref[0] = x1
    h = x1 * lax.rsqrt(jnp.mean(x1 * x1, axis=-1, keepdims=True) + EPS) * nw_ref[...]
    h = h * (1.0 + sc_ref[0]) + sh_ref[0]
    half = h.shape[1] // 2
    hp_ref[...] = _pack_halves(h[:, :half], h[:, half:])
    h_hi, h_lo = _split(h)
    lg_ref[...] = _dot_nt(rhi_ref[...], h_hi) + _dot_nt(rhi_ref[...], h_lo) + _dot_nt(rlo_ref[...], h_hi)


def _out_projection(attn, ret, x, wa, wr, g1, sh2, sc2, norm_w, r_hi, r_lo, *, tm):
    b, l, d = x.shape
    tm = min(tm, l)
    nt = l // tm
    t = b * l
    mspec = pl.BlockSpec((1, 1, d), lambda bi, i: (bi, 0, 0))
    return pl.pallas_call(
        _out_kernel,
        grid=(b, nt),
        in_specs=[pl.BlockSpec((1, tm, ATTN_W), lambda bi, i: (bi, i, 0)),
                  pl.BlockSpec((1, tm, RET_W), lambda bi, i: (bi, i, 0)),
                  pl.BlockSpec((1, tm, d), lambda bi, i: (bi, i, 0)),
                  pl.BlockSpec((ATTN_W, d), lambda bi, i: (0, 0)),
                  pl.BlockSpec((RET_W, d), lambda bi, i: (0, 0)),
                  mspec, mspec, mspec,
                  pl.BlockSpec((1, d), lambda bi, i: (0, 0)),
                  pl.BlockSpec((N_EXPERTS, d), lambda bi, i: (0, 0)),
                  pl.BlockSpec((N_EXPERTS, d), lambda bi, i: (0, 0))],
        out_specs=[pl.BlockSpec((1, tm, d), lambda bi, i: (bi, i, 0)),
                   pl.BlockSpec((tm, d // 2), lambda bi, i: (bi * nt + i, 0)),
                   pl.BlockSpec((N_EXPERTS, tm), lambda bi, i: (0, bi * nt + i))],
        out_shape=[jax.ShapeDtypeStruct((b, l, d), F32),
                   jax.ShapeDtypeStruct((t, d // 2), PACKED),
                   jax.ShapeDtypeStruct((N_EXPERTS, t), F32)],
        compiler_params=_cparams("arbitrary", "arbitrary"),
        name="out_proj",
    )(attn, ret, x, wa, wr, g1, sh2, sc2, norm_w, r_hi, r_lo)


def _route_kernel(lg_ref, bias_ref, idx_ref, w_ref, rank_ref, cnt_col_ref, cnt_row_ref, tri_s, col_s, row_s):
    tb = lg_ref.shape[1]
    step = pl.program_id(0)

    @pl.when(step == 0)
    def _():
        r = lax.broadcasted_iota(I32, (tb, tb), 0)
        c = lax.broadcasted_iota(I32, (tb, tb), 1)
        tri_s[...] = jnp.where(r <= c, 1.0, 0.0).astype(BF16)
        col_s[...] = jnp.zeros_like(col_s)
        row_s[...] = jnp.zeros_like(row_s)

    scores = _sigmoid(lg_ref[...])
    biased = scores + bias_ref[...]
    neg = -jnp.inf
    sub = lax.broadcasted_iota(I32, (GROUP_SIZE, tb), 0).astype(F32)

    gscore = []
    for g in range(N_GROUPS):
        blk = biased[g * GROUP_SIZE:(g + 1) * GROUP_SIZE]
        m1 = jnp.max(blk, axis=0, keepdims=True)
        first = jnp.min(jnp.where(blk == m1, sub, float(GROUP_SIZE)), axis=0, keepdims=True)
        m2 = jnp.max(jnp.where(sub == first, neg, blk), axis=0, keepdims=True)
        gscore.append(m1 + m2)
    gs = jnp.concatenate(gscore, axis=0)
    gsub = lax.broadcasted_iota(I32, (N_GROUPS, tb), 0).astype(F32)
    keep = jnp.zeros((N_GROUPS, tb), F32)
    for _ in range(TOPK_GROUPS):
        m = jnp.max(gs, axis=0, keepdims=True)
        first = jnp.min(jnp.where(gs == m, gsub, float(N_GROUPS)), axis=0, keepdims=True)
        sel = gsub == first
        keep = jnp.where(sel, 1.0, keep)
        gs = jnp.where(sel, neg, gs)
    masked = jnp.concatenate(
        [jnp.where(keep[g:g + 1] > 0.0, biased[g * GROUP_SIZE:(g + 1) * GROUP_SIZE], neg)
         for g in range(N_GROUPS)], axis=0)

    esub = lax.broadcasted_iota(I32, (N_EXPERTS, tb), 0).astype(F32)
    sels, idxs, ws = [], [], []
    chosen = jnp.zeros((N_EXPERTS, tb), F32)
    for _ in range(TOP_K):
        m = jnp.max(masked, axis=0, keepdims=True)
        first = jnp.min(jnp.where(masked == m, esub, float(N_EXPERTS)), axis=0, keepdims=True)
        sel = esub == first
        sels.append(sel)
        idxs.append(first)
        ws.append(jnp.sum(jnp.where(sel, scores, 0.0), axis=0, keepdims=True))
        chosen = jnp.where(sel, 1.0, chosen)
        masked = jnp.where(sel, neg, masked)
    wsum = ws[0]
    for k in range(1, TOP_K):
        wsum = wsum + ws[k]
    idx_ref[...] = jnp.concatenate(idxs, axis=0).astype(I32)
    w_ref[...] = jnp.concatenate([wk / wsum * ROUTED_SCALE for wk in ws], axis=0)

    chosen_b = chosen.astype(BF16)
    incl = _dot(chosen_b, tri_s[...])
    before = incl - chosen + col_s[...]
    rank_ref[...] = jnp.concatenate(
        [jnp.sum(jnp.where(sel, before, 0.0), axis=0, keepdims=True) for sel in sels], axis=0).astype(I32)
    col_s[...] = col_s[...] + incl[:, tb - 1:tb]
    row_s[...] = row_s[...] + _dot_nt(jnp.ones((8, tb), BF16), chosen_b)
    cnt_col_ref[...] = col_s[...].astype(I32)
    cnt_row_ref[...] = row_s[...].astype(I32)


def _route(logits_t, bias_col, *, tb):
    e, t = logits_t.shape
    tb = min(tb, t)
    kspec = pl.BlockSpec((TOP_K, tb), lambda i: (0, i))
    return pl.pallas_call(
        _route_kernel,
        grid=(t // tb,),
        in_specs=[pl.BlockSpec((e, tb), lambda i: (0, i)),
                  pl.BlockSpec((e, 1), lambda i: (0, 0))],
        out_specs=[kspec, kspec, kspec,
                   pl.BlockSpec((e, 1), lambda i: (0, 0)),
                   pl.BlockSpec((8, e), lambda i: (0, 0))],
        out_shape=[jax.ShapeDtypeStruct((TOP_K, t), I32),
                   jax.ShapeDtypeStruct((TOP_K, t), F32),
                   jax.ShapeDtypeStruct((TOP_K, t), I32),
                   jax.ShapeDtypeStruct((e, 1), I32),
                   jax.ShapeDtypeStruct((8, e), I32)],
        scratch_shapes=[pltpu.VMEM((tb, tb), BF16), pltpu.VMEM((e, 1), F32), pltpu.VMEM((8, e), F32)],
        compiler_params=_cparams("arbitrary"),
        name="route",
    )(logits_t, bias_col)


def _pad_block(cnt):
    return (cnt + (MOE_BLOCK - 1)) // MOE_BLOCK * MOE_BLOCK


def _max_items(n_blocks):
    return n_blocks // ITEM_BLOCKS + N_EXPERTS


def _dest_kernel(idx_ref, rank_ref, cnt_col_ref, cnt_row_ref, dest_ref, meta_ref, items_ref, start_s):
    tb = idx_ref.shape[1]
    nip = items_ref.shape[1]

    @pl.when(pl.program_id(0) == 0)
    def _():
        pad_col = _pad_block(cnt_col_ref[...])
        pad_row = _pad_block(cnt_row_ref[0:1, :])
        er = lax.broadcasted_iota(I32, (N_EXPERTS, N_EXPERTS), 0)
        ec = lax.broadcasted_iota(I32, (N_EXPERTS, N_EXPERTS), 1)
        start_col = jnp.sum(jnp.where(ec < er, pad_row, 0), axis=1, keepdims=True)
        start_row = jnp.sum(jnp.where(er < ec, pad_col, 0), axis=0, keepdims=True)
        start_s[...] = start_col

        used = jnp.sum(pad_row, axis=1, keepdims=True) // MOE_BLOCK
        meta_ref[...] = jnp.concatenate(
            [cnt_row_ref[0:1, :], start_row, pad_row, jnp.broadcast_to(used, (1, N_EXPERTS)),
             jnp.zeros((4, N_EXPERTS), I32)], axis=0)

        nb_col = pad_col // MOE_BLOCK
        it_col = (nb_col + (ITEM_BLOCKS - 1)) // ITEM_BLOCKS
        it_row = (pad_row // MOE_BLOCK + (ITEM_BLOCKS - 1)) // ITEM_BLOCKS
        it_start = jnp.sum(jnp.where(ec < er, it_row, 0), axis=1, keepdims=True)
        n_items = jnp.sum(it_row, axis=1, keepdims=True)
        lane = lax.broadcasted_iota(I32, (1, nip), 1)
        owner = jnp.sum(jnp.where(it_start + it_col <= lane, 1, 0), axis=0, keepdims=True)
        owner = jnp.minimum(owner, N_EXPERTS - 1)
        onehot = lax.broadcasted_iota(I32, (N_EXPERTS, nip), 0) == owner

        def pick(col):
            return jnp.sum(jnp.where(onehot, col, 0), axis=0, keepdims=True)

        j = lane - pick(it_start)
        block0 = pick(start_col) // MOE_BLOCK + ITEM_BLOCKS * j
        nvalid = jnp.clip(pick(nb_col) - ITEM_BLOCKS * j, 0, ITEM_BLOCKS)
        items_ref[...] = jnp.concatenate(
            [owner, block0, jnp.where(lane < n_items, nvalid, 0), jnp.broadcast_to(n_items, (1, nip)),
             jnp.zeros((4, nip), I32)], axis=0)

    start_col = start_s[...]
    esub = lax.broadcasted_iota(I32, (N_EXPERTS, tb), 0)
    rows = []
    for k in range(TOP_K):
        onehot = esub == idx_ref[k:k + 1, :]
        rows.append(jnp.sum(jnp.where(onehot, start_col, 0), axis=0, keepdims=True) + rank_ref[k:k + 1, :])
    dest_ref[0] = jnp.concatenate(rows, axis=0)


def _destinations(idx_t, rank_t, cnt_col, cnt_row, *, tb, n_blocks):
    _, t = idx_t.shape
    tb = min(tb, t)
    nip = (_max_items(n_blocks) + LANES - 1) // LANES * LANES
    kspec = pl.BlockSpec((TOP_K, tb), lambda i: (0, i))
    return pl.pallas_call(
        _dest_kernel,
        grid=(t // tb,),
        in_specs=[kspec, kspec,
                  pl.BlockSpec((N_EXPERTS, 1), lambda i: (0, 0)),
                  pl.BlockSpec((8, N_EXPERTS), lambda i: (0, 0))],
        out_specs=[pl.BlockSpec((1, TOP_K, tb), lambda i: (i, 0, 0)),
                   pl.BlockSpec((8, N_EXPERTS), lambda i: (0, 0)),
                   pl.BlockSpec((8, nip), lambda i: (0, 0))],
        out_shape=[jax.ShapeDtypeStruct((t // tb, TOP_K, tb), I32),
                   jax.ShapeDtypeStruct((8, N_EXPERTS), I32),
                   jax.ShapeDtypeStruct((8, nip), I32)],
        scratch_shapes=[pltpu.VMEM((N_EXPERTS, 1), I32)],
        compiler_params=_cparams("arbitrary"),
        name="dest",
    )(idx_t, rank_t, cnt_col, cnt_row)


_PAD_BITS = (64, 32, 16, 8)


def _sc_scatter_rows(rows, dest_win, n_out_rows):
    n_win, n_slots, win = dest_win.shape
    width = rows.shape[1]
    info = plsc.get_sparse_core_info()
    n_workers = info.num_cores * info.num_subcores
    per_worker = n_win // n_workers
    assert per_worker * n_workers == n_win and win <= LANES
    mesh = plsc.VectorSubcoreMesh(core_axis_name="c", subcore_axis_name="s")

    def body(rows_hbm, dest_hbm, out_hbm, idx_v, rows_v, sem):
        wid = lax.axis_index("s") * info.num_cores + lax.axis_index("c")

        @pl.loop(0, per_worker)
        def _(j):
            w = wid * per_worker + j
            pltpu.sync_copy(dest_hbm.at[w], idx_v)
            pltpu.sync_copy(rows_hbm.at[pl.ds(w * win, win)], rows_v)
            copies = [pltpu.async_copy(rows_v, out_hbm.at[idx_v.at[k]], sem) for k in range(n_slots)]
            for cp in copies:
                cp.wait()

    return pl.kernel(
        body,
        out_type=jax.ShapeDtypeStruct((n_out_rows, width), rows.dtype),
        mesh=mesh,
        scratch_types=[pltpu.VMEM((n_slots, win), I32), pltpu.VMEM((win, width), rows.dtype),
                       pltpu.SemaphoreType.DMA],
        name="sc_scatter",
    )(rows, dest_win)


def _pad_fill_kernel(meta_ref, xs_in, xs_hbm, zero_s, sem_z, *, e_per_step):
    del xs_in
    step = pl.program_id(0)
    zero_s[...] = jnp.zeros_like(zero_s)

    def tail_copy(c):
        row0 = pl.multiple_of((meta_ref[3, 0] + c) * MOE_BLOCK, MOE_BLOCK)
        return pltpu.make_async_copy(zero_s, xs_hbm.at[pl.ds(row0, MOE_BLOCK)], sem_z)

    @pl.when(step == 0)
    def _():
        for c in range(ITEM_BLOCKS - 1):
            tail_copy(c).start()
        for c in range(ITEM_BLOCKS - 1):
            tail_copy(c).wait()

    def pad_copies(e):
        cnt = meta_ref[0, e]
        off = meta_ref[1, e] + cnt
        rem = meta_ref[2, e] - cnt
        head = rem & (SUBLANES - 1)
        out = []
        for i in range(SUBLANES - 1):
            out.append((i < head,
                        pltpu.make_async_copy(zero_s.at[pl.ds(0, 1)], xs_hbm.at[pl.ds(off + i, 1)], sem_z)))
        off = off + head
        for bit in _PAD_BITS:
            out.append(((rem & bit) != 0,
                        pltpu.make_async_copy(zero_s.at[pl.ds(0, bit)],
                                              xs_hbm.at[pl.ds(pl.multiple_of(off, SUBLANES), bit)], sem_z)))
            off = off + (rem & bit)
        return out

    for j in range(e_per_step):
        for cond, c in pad_copies(step * e_per_step + j):
            pl.when(cond)(c.start)
    for j in range(e_per_step):
        for cond, c in pad_copies(step * e_per_step + j):
            pl.when(cond)(c.wait)


PAD_FILL_EXPERTS_PER_STEP = 8


def _pad_fill(meta, xs):
    half = xs.shape[1]
    return pl.pallas_call(
        functools.partial(_pad_fill_kernel, e_per_step=PAD_FILL_EXPERTS_PER_STEP),
        grid=(N_EXPERTS // PAD_FILL_EXPERTS_PER_STEP,),
        in_specs=[pl.BlockSpec(memory_space=pltpu.SMEM),
                  pl.BlockSpec(memory_space=pl.ANY)],
        out_specs=pl.BlockSpec(memory_space=pl.ANY),
        out_shape=jax.ShapeDtypeStruct(xs.shape, xs.dtype),
        input_output_aliases={1: 0},
        scratch_shapes=[pltpu.VMEM((MOE_BLOCK, half), PACKED), pltpu.SemaphoreType.DMA],
        compiler_params=_cparams("arbitrary"),
        name="pad_fill",
    )(meta, xs)


def _experts_kernel(items_ref, xs_hbm, wg_hbm, wu_hbm, wd_hbm, ys_hbm,
                    xbuf, ybuf, wg_f, wu_f, wd_f, wg_a, wu_a, wd_a, wg_b, wu_b, wd_b, sem_x, sem_y, sem_w):
    n_items = items_ref[3, 0]
    rows = ITEM_BLOCKS * MOE_BLOCK

    def w_copies(e, s):
        return [pltpu.make_async_copy(src.at[e], dst.at[s], sem_w.at[s])
                for src, dst in ((wg_hbm, wg_f), (wu_hbm, wu_f), (wd_hbm, wd_f))]

    def x_copy(item, s):
        row0 = pl.multiple_of(items_ref[1, item] * MOE_BLOCK, MOE_BLOCK)
        return pltpu.make_async_copy(xs_hbm.at[pl.ds(row0, rows)], xbuf.at[s], sem_x.at[s])

    def y_copies(item, s, fn):
        for k in range(ITEM_BLOCKS):
            row0 = pl.multiple_of((items_ref[1, item] + k) * MOE_BLOCK, MOE_BLOCK)
            cp = pltpu.make_async_copy(ybuf.at[s, pl.ds(k * MOE_BLOCK, MOE_BLOCK)],
                                       ys_hbm.at[pl.ds(row0, MOE_BLOCK)], sem_y.at[s])
            pl.when(k < items_ref[2, item])(functools.partial(fn, cp))

    def expert_of(item):
        return items_ref[0, jnp.minimum(item, n_items - 1)]

    def changes_at(item):
        return ((item < n_items) & (expert_of(item) != expert_of(item - 1))).astype(I32)

    bf16_sets = ((wg_a, wu_a, wd_a), (wg_b, wu_b, wd_b))

    def cast_weights(dst, wslot):
        for ref, src in zip(dst, (wg_f, wu_f, wd_f)):
            ref[...] = src[wslot].astype(BF16)

    @pl.when(n_items > 0)
    def _():
        x_copy(0, 0).start()
        for cp in w_copies(expert_of(0), 0):
            cp.start()
        c1 = changes_at(1)

        @pl.when(c1 == 1)
        def _():
            for cp in w_copies(expert_of(1), 1):
                cp.start()

        @pl.when(changes_at(2) == 1)
        def _():
            for cp in w_copies(expert_of(2), c1 + 1):
                cp.start()

        for cp in w_copies(expert_of(0), 0):
            cp.wait()
        cast_weights(bf16_sets[0], 0)

    def item_body(i, ordinal):
        slot = i % 2
        prev = jnp.maximum(i - 1, 0)
        c1 = changes_at(i + 1)
        c2 = changes_at(i + 2)
        c3 = changes_at(i + 3)

        @pl.when(i + 1 < n_items)
        def _():
            x_copy(i + 1, 1 - slot).start()

        @pl.when(c3 == 1)
        def _():
            for cp in w_copies(expert_of(i + 3), (ordinal + c1 + c2 + 1) % WEIGHT_SLOTS):
                cp.start()

        @pl.when(c1 == 1)
        def _():
            for cp in w_copies(expert_of(i + 1), (ordinal + 1) % WEIGHT_SLOTS):
                cp.wait()

        x_copy(i, slot).wait()
        next_slot = (ordinal + c1) % WEIGHT_SLOTS

        def run(cur, nxt):
            wg_s, wu_s, wd_s = cur
            cast_weights(nxt, next_slot)
            xa, xb = _unpack_halves(xbuf[slot])
            xa = xa.astype(BF16)
            xb = xb.astype(BF16)
            half = xa.shape[1]
            g = _dot(xa, wg_s[0:half]) + _dot(xb, wg_s[half:])
            u = _dot(xa, wu_s[0:half]) + _dot(xb, wu_s[half:])
            y = _dot((_silu(g) * u).astype(BF16), wd_s[...])
            ybuf[slot] = _pack_halves(y[:, :half], y[:, half:])

        parity = ordinal % 2
        pl.when(parity == 0)(functools.partial(run, bf16_sets[0], bf16_sets[1]))
        pl.when(parity == 1)(functools.partial(run, bf16_sets[1], bf16_sets[0]))
        y_copies(i, slot, lambda cp: cp.start())

        @pl.when(i > 0)
        def _():
            y_copies(prev, 1 - slot, lambda cp: cp.wait())

        return ordinal + c1

    lax.fori_loop(0, n_items, item_body, jnp.int32(0))

    @pl.when(n_items > 0)
    def _():
        last = n_items - 1
        y_copies(last, last % 2, lambda cp: cp.wait())


def _experts(items, xs, w_gate, w_up, w_down, *, n_blocks):
    half = xs.shape[1]
    e, d, f = w_gate.shape
    rows = ITEM_BLOCKS * MOE_BLOCK
    any_spec = pl.BlockSpec(memory_space=pl.ANY)
    return pl.pallas_call(
        _experts_kernel,
        grid_spec=pltpu.PrefetchScalarGridSpec(
            num_scalar_prefetch=1,
            grid=(1,),
            in_specs=[any_spec, any_spec, any_spec, any_spec],
            out_specs=any_spec,
            scratch_shapes=[pltpu.VMEM((2, rows, half), PACKED), pltpu.VMEM((2, rows, half), PACKED),
                            pltpu.VMEM((WEIGHT_SLOTS, d, f), F32), pltpu.VMEM((WEIGHT_SLOTS, d, f), F32),
                            pltpu.VMEM((WEIGHT_SLOTS, f, d), F32),
                            pltpu.VMEM((d, f), BF16), pltpu.VMEM((d, f), BF16), pltpu.VMEM((f, d), BF16),
                            pltpu.VMEM((d, f), BF16), pltpu.VMEM((d, f), BF16), pltpu.VMEM((f, d), BF16),
                            pltpu.SemaphoreType.DMA((2,)), pltpu.SemaphoreType.DMA((2,)),
                            pltpu.SemaphoreType.DMA((WEIGHT_SLOTS,))]),
        out_shape=jax.ShapeDtypeStruct((n_blocks * MOE_BLOCK, half), PACKED),
        compiler_params=_cparams("arbitrary"),
        name="experts",
    )(items, xs, w_gate, w_up, w_down)


def _sc_gather_rows(table, idx):
    n_idx = idx.shape[0]
    width = table.shape[1]
    info = plsc.get_sparse_core_info()
    n_workers = info.num_cores * info.num_subcores
    per_worker = n_idx // n_workers
    assert per_worker * n_workers == n_idx and per_worker % (SC_GATHER_BUFS * SC_WINDOW) == 0
    mesh = plsc.VectorSubcoreMesh(core_axis_name="c", subcore_axis_name="s")

    def body(table_hbm, idx_hbm, out_hbm, idx_v, rows_v, sem_g, sem_o):
        wid = lax.axis_index("s") * info.num_cores + lax.axis_index("c")
        base = wid * per_worker

        @pl.loop(0, per_worker // (SC_GATHER_BUFS * SC_WINDOW))
        def _(it):
            offs = [base + (it * SC_GATHER_BUFS + b) * SC_WINDOW for b in range(SC_GATHER_BUFS)]
            gathers = []
            for b, off in enumerate(offs):
                pltpu.sync_copy(idx_hbm.at[pl.ds(off, SC_WINDOW)], idx_v.at[b])
                gathers.append(pltpu.async_copy(table_hbm.at[idx_v.at[b]], rows_v.at[b], sem_g.at[b]))
            writes = []
            for b, off in enumerate(offs):
                gathers[b].wait()
                writes.append(pltpu.async_copy(rows_v.at[b], out_hbm.at[pl.ds(off, SC_WINDOW)], sem_o.at[b]))
            for cp in writes:
                cp.wait()

    return pl.kernel(
        body,
        out_type=jax.ShapeDtypeStruct((n_idx, width), table.dtype),
        mesh=mesh,
        scratch_types=[pltpu.VMEM((SC_GATHER_BUFS, SC_WINDOW), I32),
                       pltpu.VMEM((SC_GATHER_BUFS, SC_WINDOW, width), table.dtype),
                       pltpu.SemaphoreType.DMA((SC_GATHER_BUFS,)), pltpu.SemaphoreType.DMA((SC_GATHER_BUFS,))],
        name="sc_gather",
    )(table, idx)


def _combine_kernel(hp_ref, x1_ref, g2_ref, w_ref, sgw_ref, suw_ref, sdw_ref, yg_ref, *rest):
    o_ref = rest[-1]
    xa, xb = _unpack_halves(hp_ref[...])
    xa = xa.astype(BF16)
    xb = xb.astype(BF16)
    half = xa.shape[1]
    tb = xa.shape[0]
    g = _dot(xa, sgw_ref[0:half]) + _dot(xb, sgw_ref[half:])
    u = _dot(xa, suw_ref[0:half]) + _dot(xb, suw_ref[half:])
    shared = _dot((_silu(g) * u).astype(BF16), sdw_ref[...])

    w = w_ref[...]
    acc_a = jnp.zeros((tb, half), F32)
    acc_b = jnp.zeros((tb, half), F32)
    for k in range(TOP_K):
        ya, yb = _unpack_halves(yg_ref[0, k])
        acc_a += ya * w[:, k:k + 1]
        acc_b += yb * w[:, k:k + 1]
    g2 = g2_ref[0]
    o_ref[:, 0:half] = x1_ref[:, 0:half] + g2[:, 0:half] * (acc_a + shared[:, 0:half])
    o_ref[:, half:] = x1_ref[:, half:] + g2[:, half:] * (acc_b + shared[:, half:])


def _combine(hp, x1, g2, w_tok, sgw, suw, sdw, yg, out_prev, *, tb, seq_len, first_step):
    t, half = hp.shape
    d = 2 * half
    per_seq = seq_len // tb
    f = sgw.shape[1]
    s0 = first_step
    in_specs = [pl.BlockSpec((tb, half), lambda i: (s0 + i, 0)),
                pl.BlockSpec((tb, d), lambda i: (s0 + i, 0)),
                pl.BlockSpec((1, 1, d), lambda i: ((s0 + i) // per_seq, 0, 0)),
                pl.BlockSpec((tb, TOP_K), lambda i: (s0 + i, 0)),
                pl.BlockSpec((d, f), lambda i: (0, 0)),
                pl.BlockSpec((d, f), lambda i: (0, 0)),
                pl.BlockSpec((f, d), lambda i: (0, 0)),
                pl.BlockSpec((1, TOP_K, tb, half), lambda i: (i, 0, 0, 0))]
    args = [hp, x1, g2, w_tok, sgw, suw, sdw, yg]
    aliases = {}
    if out_prev is not None:
        in_specs.append(pl.BlockSpec(memory_space=pl.ANY))
        args.append(out_prev)
        aliases = {len(args) - 1: 0}
    return pl.pallas_call(
        _combine_kernel,
        grid=(yg.shape[0],),
        in_specs=in_specs,
        out_specs=pl.BlockSpec((tb, d), lambda i: (s0 + i, 0)),
        out_shape=jax.ShapeDtypeStruct((t, d), F32),
        input_output_aliases=aliases,
        compiler_params=_cparams("arbitrary"),
        name="combine",
    )(*args)


def _rope_tables(l):
    rows = l // GRID_W
    r = jnp.repeat(jnp.arange(rows), GRID_W).astype(F32)
    col = jnp.tile(jnp.arange(GRID_W), rows).astype(F32)
    n_f = HEAD_DIM // 4
    freqs = ROPE_THETA ** (-jnp.arange(n_f, dtype=F32) / n_f)
    ang = jnp.concatenate([r[:, None] * freqs, col[:, None] * freqs], axis=-1)
    ang = jnp.tile(jnp.repeat(ang, 2, axis=1), (1, LANES // HEAD_DIM))
    sign = jnp.where(jnp.arange(LANES) % 2 == 0, -1.0, 1.0).astype(F32)
    return jnp.cos(ang), jnp.sin(ang) * sign


def kernel(x, c, ctx, c_ctx, w_mod, b_mod, norm1_w, norm2_w, w_in, q_norm_w, k_norm_w, ret_decay_fwd,
           ret_decay_bwd, w_out, router_w, router_bias, exp_w_gate, exp_w_up, exp_w_down, sh_w_gate,
           sh_w_up, sh_w_down):
    b, l, d = x.shape
    lc = ctx.shape[1]
    t = b * l
    assert w_mod.shape[0] == 1, "single layer"
    assert l % CHUNK == 0 and lc % CHUNK == 0 and l % GRID_W == 0

    rows = (b + 1 + 7) // 8 * 8
    cc = jnp.zeros((rows, d), F32).at[:b].set(c).at[b].set(c_ctx)
    mod = _modulation(cc, w_mod[0], b_mod[0])
    sh1, sc1, g1, sh2, sc2, g2 = [mod[:b, i * d:(i + 1) * d].reshape(b, 1, d) for i in range(6)]
    shc = mod[b, 0:d].reshape(1, 1, d)
    scc = mod[b, d:2 * d].reshape(1, 1, d)

    wi = w_in[0].astype(BF16)
    qnw = jnp.tile(q_norm_w[0], LANES // HEAD_DIM).reshape(1, LANES)
    knw = jnp.tile(k_norm_w[0], LANES // HEAD_DIM).reshape(1, LANES)
    cos, sin = _rope_tables(l)
    n1 = norm1_w[0].reshape(1, d)

    cklo, ckhi, cvlo, cvhi, crk, crv = _projection(
        ctx, shc, scc, n1, wi, qnw, knw, cos[:lc], sin[:lc], rope=False, with_q=False, tm=TILE_PROJ)
    klo, khi, vlo, vhi, rk, rv, q, rq, sg = _projection(
        x, sh1, sc1, n1, wi, qnw, knw, cos, sin, rope=True, with_q=True, tm=TILE_PROJ)

    bound = (HEAD_DIM * QK_SCALE * LOG2_E * BOUND_MARGIN
             * jnp.max(jnp.abs(q_norm_w[0])) * jnp.max(jnp.abs(k_norm_w[0]))).astype(F32)
    attn_args = (bound.reshape(1), q, klo, khi, vlo, vhi, cklo, ckhi, cvlo, cvhi)
    attn = lax.cond(bound <= MAX_STREAM_SHIFT,
                    functools.partial(_attention, tq=TILE_ATTN_Q, streaming=True),
                    functools.partial(_attention, tq=TILE_ATTN_Q, streaming=False), *attn_args)
    dec_f = jnp.repeat(ret_decay_fwd[0].astype(F32), HEAD_DIM).reshape(1, RET_W)
    dec_b = jnp.repeat(ret_decay_bwd[0].astype(F32), HEAD_DIM).reshape(1, RET_W)
    ret = _retention(rq, rk, rv, sg, crk, crv, dec_f, dec_b)

    wo = w_out[0].astype(BF16)
    r_hi, r_lo = _split(router_w[0].T)
    x1, hp, logits_t = _out_projection(attn, ret, x, wo[:ATTN_W], wo[ATTN_W:], g1, sh2, sc2,
                                       norm2_w[0].reshape(1, d), r_hi, r_lo, tm=TILE_OUT)

    idx_t, w_t, rank_t, cnt_col, cnt_row = _route(logits_t, router_bias[0].reshape(N_EXPERTS, 1), tb=TILE_TOKENS)
    n_blocks = -(-(t * TOP_K) // MOE_BLOCK) + N_EXPERTS
    tb = TILE_TOKENS
    dest, meta, items = _destinations(idx_t, rank_t, cnt_col, cnt_row, tb=tb, n_blocks=n_blocks)
    steps, _, tbe = dest.shape
    dest_win = dest.reshape(steps, TOP_K, tbe // SC_WINDOW, SC_WINDOW).transpose(0, 2, 1, 3)
    dest_win = dest_win.reshape(t // SC_WINDOW, TOP_K, SC_WINDOW)
    xs = _sc_scatter_rows(hp, dest_win, (n_blocks + ITEM_BLOCKS - 1) * MOE_BLOCK)
    xs = _pad_fill(meta, xs)
    ys = _experts(items, xs, exp_w_gate[0], exp_w_up[0], exp_w_down[0], n_blocks=n_blocks)
    parts = COMBINE_PARTS if steps % COMBINE_PARTS == 0 else 1
    steps_part = steps // parts
    x1f, w_tok = x1.reshape(t, d), w_t.T
    sgw, suw, sdw = sh_w_gate[0].astype(BF16), sh_w_up[0].astype(BF16), sh_w_down[0].astype(BF16)
    out = None
    for p in range(parts):
        idx = dest[p * steps_part:(p + 1) * steps_part].reshape(-1)
        yg = _sc_gather_rows(ys, idx).reshape(steps_part, TOP_K, tbe, d // 2)
        out = _combine(hp, x1f, g2, w_tok, sgw, suw, sdw, yg, out, tb=tbe, seq_len=l, first_step=p * steps_part)
    return out.reshape(b, l, d)
```

```python
import functools

import jax
import jax.numpy as jnp
from jax import lax
from jax.experimental import pallas as pl
from jax.experimental.pallas import tpu as pltpu
from jax.experimental.pallas import tpu_sc as plsc

F32 = jnp.float32
BF16 = jnp.bfloat16
I32 = jnp.int32
U32 = jnp.uint32
PACKED = jnp.int32

HEAD_DIM = 64
LANES = 128
SUBLANES = 8
ATTN_HEADS = 8
ATTN_KV_HEADS = 2
GQA = ATTN_HEADS // ATTN_KV_HEADS
RET_HEADS = 8
ATTN_W = ATTN_HEADS * HEAD_DIM
KV_W = ATTN_KV_HEADS * HEAD_DIM
RET_W = RET_HEADS * HEAD_DIM
RET_PAIRS = RET_W // LANES
CHUNK = 128
GRID_W = 64
ROPE_THETA = 10000.0
N_EXPERTS = 256
TOP_K = 8
N_GROUPS = 8
GROUP_SIZE = N_EXPERTS // N_GROUPS
TOPK_GROUPS = 4
ROUTED_SCALE = 2.5
MOE_BLOCK = 128
EPS = 1e-6
QK_SCALE = HEAD_DIM ** -0.5
LOG2_E = 1.4426950408889634
KEY_TILE = 256
BOUND_MARGIN = 1.02
MAX_STREAM_SHIFT = 56.0

OFF_AK = 0
OFF_AV = OFF_AK + KV_W
OFF_RK = OFF_AV + KV_W
OFF_RV = OFF_RK + RET_W
CTX_KV_COLS = OFF_RV + RET_W
OFF_AQ = CTX_KV_COLS
OFF_RQ = OFF_AQ + ATTN_W
OFF_RG = OFF_RQ + RET_W
IN_COLS = OFF_RG + RET_W

VMEM_LIMIT = 52 * 1024 * 1024

SUB_ROWS = 256
TILE_PROJ = 1024
TILE_ATTN_Q = 1024
TILE_OUT = 512
TILE_TOKENS = 512
ITEM_BLOCKS = 5
COMBINE_PARTS = 4
SC_GATHER_BUFS = 2
WEIGHT_SLOTS = 3
SC_WINDOW = 64
HI_MASK = 0xFFFF0000


def _cparams(*sem):
    return pltpu.CompilerParams(dimension_semantics=sem, vmem_limit_bytes=VMEM_LIMIT)


def _split(a):
    hi = a.astype(BF16)
    lo = (a - hi.astype(F32)).astype(BF16)
    return hi, lo


def _dot(a, b):
    return jnp.dot(a, b, preferred_element_type=F32)


def _dot_nt(a, b):
    return lax.dot_general(a, b, (((1,), (1,)), ((), ())), preferred_element_type=F32)


def _sigmoid(v):
    return 1.0 / (1.0 + jnp.exp(-v))


def _silu(v):
    return v * _sigmoid(v)


def _pack_halves(a, b):
    ua = lax.bitcast_convert_type(a.astype(BF16).astype(F32), U32)
    ub = lax.bitcast_convert_type(b.astype(BF16).astype(F32), U32)
    return lax.bitcast_convert_type((ua & jnp.uint32(HI_MASK)) | (ub >> 16), PACKED)


def _unpack_halves(p):
    u = lax.bitcast_convert_type(p, U32)
    a = lax.bitcast_convert_type(u & jnp.uint32(HI_MASK), F32)
    b = lax.bitcast_convert_type(u << 16, F32)
    return a, b


def _mod_kernel(c_ref, w_ref, b_ref, o_ref):
    s_hi, s_lo = _split(_silu(c_ref[...]))
    w_hi, w_lo = _split(w_ref[...])
    o_ref[...] = _dot(s_hi, w_hi) + _dot(s_hi, w_lo) + _dot(s_lo, w_hi) + b_ref[...]


def _modulation(cc, w_mod, b_mod):
    rows, d = cc.shape
    n = w_mod.shape[1]
    tn = 768
    return pl.pallas_call(
        _mod_kernel,
        grid=(n // tn,),
        in_specs=[pl.BlockSpec((rows, d), lambda j: (0, 0)),
                  pl.BlockSpec((d, tn), lambda j: (0, j)),
                  pl.BlockSpec((1, tn), lambda j: (0, j))],
        out_specs=pl.BlockSpec((rows, tn), lambda j: (0, j)),
        out_shape=jax.ShapeDtypeStruct((rows, n), F32),
        compiler_params=_cparams("arbitrary"),
        name="mod",
    )(cc, w_mod, b_mod.reshape(1, n))


def _segment_ones():
    r = lax.broadcasted_iota(I32, (LANES, LANES), 0) // HEAD_DIM
    c = lax.broadcasted_iota(I32, (LANES, LANES), 1) // HEAD_DIM
    return jnp.where(r == c, 1.0, 0.0).astype(BF16)


def _head_mean_sq(v, seg):
    hi, lo = _split(v * v)
    return (_dot(hi, seg) + _dot(lo, seg)) * (1.0 / HEAD_DIM)


def _proj_kernel(x_ref, sh_ref, sc_ref, nw_ref, wi_ref, qnw_ref, knw_ref, cos_ref, sin_ref,
                 *out_refs, rope, with_q):
    if with_q:
        klo_ref, khi_ref, vlo_ref, vhi_ref, rk_ref, rv_ref, q_ref, rq_ref, sg_ref = out_refs
    else:
        klo_ref, khi_ref, vlo_ref, vhi_ref, rk_ref, rv_ref = out_refs
    tm = x_ref.shape[1]
    sub = min(tm, SUB_ROWS)
    seg = _segment_ones()
    lane = lax.broadcasted_iota(I32, (sub, LANES), 1)
    low_half = lane < HEAD_DIM
    even = (lane & 1) == 0

    for r0 in range(0, tm, sub):
        rows = slice(r0, r0 + sub)
        x = x_ref[0, rows, :]
        h = x * lax.rsqrt(jnp.mean(x * x, axis=-1, keepdims=True) + EPS) * nw_ref[...]
        h = h * (1.0 + sc_ref[0]) + sh_ref[0]
        z = _dot(h.astype(BF16), wi_ref[...])

        def norm_rope(v, w128, rows=rows):
            v = v * lax.rsqrt(_head_mean_sq(v, seg) + EPS) * w128
            if rope:
                swapped = jnp.where(even, pltpu.roll(v, LANES - 1, 1), pltpu.roll(v, 1, 1))
                v = v * cos_ref[rows, :] + swapped * sin_ref[rows, :]
            return v

        k = norm_rope(z[:, OFF_AK:OFF_AK + KV_W], knw_ref[...])
        ksw = pltpu.roll(k, HEAD_DIM, 1)
        klo_ref[0, 0, :, rows] = jnp.transpose(jnp.where(low_half, k, 0.0)).astype(BF16)
        khi_ref[0, 0, :, rows] = jnp.transpose(jnp.where(low_half, 0.0, ksw)).astype(BF16)
        klo_ref[0, 1, :, rows] = jnp.transpose(jnp.where(low_half, ksw, 0.0)).astype(BF16)
        khi_ref[0, 1, :, rows] = jnp.transpose(jnp.where(low_half, 0.0, k)).astype(BF16)
        v = z[:, OFF_AV:OFF_AV + KV_W]
        vsw = pltpu.roll(v, HEAD_DIM, 1)
        vlo_ref[0, 0, rows, :] = jnp.where(low_half, v, 0.0).astype(BF16)
        vhi_ref[0, 0, rows, :] = jnp.where(low_half, 0.0, vsw).astype(BF16)
        vlo_ref[0, 1, rows, :] = jnp.where(low_half, vsw, 0.0).astype(BF16)
        vhi_ref[0, 1, rows, :] = jnp.where(low_half, 0.0, v).astype(BF16)
        rk_ref[0, rows, :] = (z[:, OFF_RK:OFF_RK + RET_W] * QK_SCALE).astype(BF16)
        rv_ref[0, rows, :] = z[:, OFF_RV:OFF_RV + RET_W].astype(BF16)
        if with_q:
            for j in range(ATTN_W // LANES):
                qj = norm_rope(z[:, OFF_AQ + j * LANES:OFF_AQ + (j + 1) * LANES], qnw_ref[...])
                q_ref[0, rows, j * LANES:(j + 1) * LANES] = (qj * (QK_SCALE * LOG2_E)).astype(BF16)
            rq_ref[0, rows, :] = z[:, OFF_RQ:OFF_RQ + RET_W].astype(BF16)
            sg_ref[0, rows, :] = _silu(z[:, OFF_RG:OFF_RG + RET_W]).astype(BF16)


def _projection(x, shift, scale, norm_w, wi_bf16, qnw, knw, cos, sin, *, rope, with_q, tm):
    b, l, d = x.shape
    tm = min(tm, l)
    ncols = IN_COLS if with_q else CTX_KV_COLS
    per_batch = shift.shape[0] > 1
    mod_idx = (lambda bi, i: (bi, 0, 0)) if per_batch else (lambda bi, i: (0, 0, 0))
    kv_shape = jax.ShapeDtypeStruct((b, ATTN_KV_HEADS, l, LANES), BF16)
    kv_spec = pl.BlockSpec((1, ATTN_KV_HEADS, tm, LANES), lambda bi, i: (bi, 0, i, 0))
    kt_shape = jax.ShapeDtypeStruct((b, ATTN_KV_HEADS, LANES, l), BF16)
    kt_spec = pl.BlockSpec((1, ATTN_KV_HEADS, LANES, tm), lambda bi, i: (bi, 0, 0, i))
    w_shape = jax.ShapeDtypeStruct((b, l, RET_W), BF16)
    w_spec = pl.BlockSpec((1, tm, RET_W), lambda bi, i: (bi, i, 0))
    out_shape = [kt_shape] * 2 + [kv_shape] * 2 + [w_shape] * 2
    out_specs = [kt_spec] * 2 + [kv_spec] * 2 + [w_spec] * 2
    if with_q:
        out_shape += [w_shape] * 3
        out_specs += [w_spec] * 3
    return pl.pallas_call(
        functools.partial(_proj_kernel, rope=rope, with_q=with_q),
        grid=(b, l // tm),
        in_specs=[pl.BlockSpec((1, tm, d), lambda bi, i: (bi, i, 0)),
                  pl.BlockSpec((1, 1, d), mod_idx),
                  pl.BlockSpec((1, 1, d), mod_idx),
                  pl.BlockSpec((1, d), lambda bi, i: (0, 0)),
                  pl.BlockSpec((d, ncols), lambda bi, i: (0, 0)),
                  pl.BlockSpec((1, LANES), lambda bi, i: (0, 0)),
                  pl.BlockSpec((1, LANES), lambda bi, i: (0, 0)),
                  pl.BlockSpec((tm, LANES), lambda bi, i: (i, 0)),
                  pl.BlockSpec((tm, LANES), lambda bi, i: (i, 0))],
        out_specs=out_specs,
        out_shape=out_shape,
        compiler_params=_cparams("arbitrary", "arbitrary"),
        name="proj_latent" if with_q else "proj_ctx",
    )(x, shift, scale, norm_w, wi_bf16, qnw, knw, cos, sin)


def _attn_kernel(shift_ref, q_ref, klo_ref, khi_ref, vlo_ref, vhi_ref, cklo_ref, ckhi_ref, cvlo_ref, cvhi_ref,
                 o_ref, kl_s, kh_s, va_s, *, l, lc, streaming):
    lk = l + lc

    @pl.when(pl.program_id(2) == 0)
    def _():
        kl_s[:, 0:l] = klo_ref[0, 0]
        kl_s[:, l:lk] = cklo_ref[0, 0]
        kh_s[:, 0:l] = khi_ref[0, 0]
        kh_s[:, l:lk] = ckhi_ref[0, 0]
        lane = lax.broadcasted_iota(I32, (lk, LANES), 1)
        ones_lo = jnp.where(lane < HEAD_DIM, 1.0, 0.0).astype(BF16)
        ones_hi = jnp.where(lane < HEAD_DIM, 0.0, 1.0).astype(BF16)
        for g, (v_ref, cv_ref, ones) in enumerate(((vlo_ref, cvlo_ref, ones_lo), (vhi_ref, cvhi_ref, ones_hi),
                                                   (vlo_ref, cvlo_ref, ones_lo), (vhi_ref, cvhi_ref, ones_hi))):
            v_col, one_col = (0, LANES) if g < 2 else (LANES, 0)
            va_s[g, 0:l, v_col:v_col + LANES] = v_ref[0, 0]
            va_s[g, l:lk, v_col:v_col + LANES] = cv_ref[0, 0]
            va_s[g, :, one_col:one_col + LANES] = ones

    q = q_ref[0]
    acc = []
    for g in range(GQA):
        qp = q[:, (g // 2) * LANES:(g // 2 + 1) * LANES]
        k_s = kl_s if g % 2 == 0 else kh_s
        if streaming:
            shift = shift_ref[0]
            o = None
            for c in range(0, lk, KEY_TILE):
                hi = min(c + KEY_TILE, lk)
                p = jnp.exp2(_dot(qp, k_s[:, c:hi]) - shift).astype(BF16)
                t = _dot(p, va_s[g, c:hi])
                o = t if o is None else o + t
        else:
            s = _dot(qp, k_s[...])
            p = jnp.exp2(s - jnp.max(s, axis=-1, keepdims=True)).astype(BF16)
            o = _dot(p, va_s[g])
        acc.append(o)
    out_a = acc[0] + acc[1]
    out_b = acc[2] + acc[3]
    o_ref[0, :, 0:LANES] = (out_a[:, 0:LANES] / out_a[:, LANES:2 * LANES]).astype(BF16)
    o_ref[0, :, LANES:2 * LANES] = (out_b[:, LANES:2 * LANES] / out_b[:, 0:LANES]).astype(BF16)


def _attention(shift, q, klo, khi, vlo, vhi, cklo, ckhi, cvlo, cvhi, *, tq, streaming):
    b, l, _ = q.shape
    lc = cvlo.shape[2]
    lk = l + lc
    tq = min(tq, l)
    gw = GQA * HEAD_DIM
    kt_spec = pl.BlockSpec((1, 1, LANES, l), lambda bi, h, i: (bi, h, 0, 0))
    kv_spec = pl.BlockSpec((1, 1, l, LANES), lambda bi, h, i: (bi, h, 0, 0))
    ckt_spec = pl.BlockSpec((1, 1, LANES, lc), lambda bi, h, i: (bi, h, 0, 0))
    ckv_spec = pl.BlockSpec((1, 1, lc, LANES), lambda bi, h, i: (bi, h, 0, 0))
    return pl.pallas_call(
        functools.partial(_attn_kernel, l=l, lc=lc, streaming=streaming),
        grid=(b, ATTN_KV_HEADS, l // tq),
        in_specs=([pl.BlockSpec(memory_space=pltpu.SMEM),
                   pl.BlockSpec((1, tq, gw), lambda bi, h, i: (bi, i, h))] + [kt_spec] * 2 + [kv_spec] * 2
                  + [ckt_spec] * 2 + [ckv_spec] * 2),
        out_specs=pl.BlockSpec((1, tq, gw), lambda bi, h, i: (bi, i, h)),
        out_shape=jax.ShapeDtypeStruct((b, l, ATTN_W), BF16),
        scratch_shapes=[pltpu.VMEM((LANES, lk), BF16), pltpu.VMEM((LANES, lk), BF16),
                        pltpu.VMEM((GQA, lk, 2 * LANES), BF16)],
        compiler_params=_cparams("arbitrary", "arbitrary", "arbitrary"),
        name="attn_stream" if streaming else "attn",
    )(shift, q, klo, khi, vlo, vhi, cklo, ckhi, cvlo, cvhi)


def _log_sigmoid(v):
    return jnp.minimum(v, 0.0) - jnp.log(1.0 + jnp.exp(-jnp.abs(v)))


def _ret_kernel(rq_ref, rk_ref, rv_ref, sg_ref, crk_ref, crv_ref, df_ref, db_ref, o_ref,
                m_s, xi_s, zeta_s, kv_s, st_s, *, l, lc):
    n = l // CHUNK
    nc = lc // CHUNK
    lgf = _log_sigmoid(df_ref[...])
    lgb = _log_sigmoid(db_ref[...])
    pos = lax.broadcasted_iota(I32, (CHUNK, LANES), 0).astype(F32)
    row = lax.broadcasted_iota(I32, (CHUNK, CHUNK), 0)
    col = lax.broadcasted_iota(I32, (CHUNK, CHUNK), 1)
    diff = (row - col).astype(F32)
    g_chunk = []
    for p in range(RET_PAIRS):
        cols = slice(p * LANES, (p + 1) * LANES)
        lf, lb = lgf[:, cols], lgb[:, cols]
        xi_s[p, :, 0:LANES] = jnp.exp((pos + 1.0) * lf)
        xi_s[p, :, LANES:] = jnp.exp((CHUNK - pos) * lb)
        zeta_s[p, :, 0:LANES] = jnp.exp((CHUNK - 1.0 - pos) * lf)
        zeta_s[p, :, LANES:] = jnp.exp(pos * lb)
        g_chunk.append((jnp.exp(CHUNK * lf), jnp.exp(CHUNK * lb)))
        for j in range(2):
            h = 2 * p + j
            hf = lgf[:, h * HEAD_DIM:h * HEAD_DIM + 1]
            hb = lgb[:, h * HEAD_DIM:h * HEAD_DIM + 1]
            m_s[p, :, j * CHUNK:(j + 1) * CHUNK] = jnp.where(
                diff > 0, jnp.exp(diff * hf), jnp.where(diff < 0, jnp.exp(-diff * hb), 2.0))

    lane = lax.broadcasted_iota(I32, (CHUNK, LANES), 1)
    low_half = lane < HEAD_DIM
    diag = (lax.broadcasted_iota(I32, (LANES, LANES), 0) // HEAD_DIM
            == lax.broadcasted_iota(I32, (LANES, LANES), 1) // HEAD_DIM)
    seg = jnp.where(diag, 1.0, 0.0).astype(BF16)
    seg2 = jnp.concatenate([seg, seg], axis=0)
    diag2 = jnp.concatenate([diag, diag], axis=0)

    def split_heads(a):
        zero = jnp.zeros_like(a)
        return jnp.concatenate([jnp.where(low_half, a, zero), jnp.where(low_half, zero, a)], axis=0)

    def contrib(k_ref, v_ref, r0, p):
        cols = slice(p * LANES, (p + 1) * LANES)
        kp = k_ref[0, pl.ds(r0, CHUNK), cols].astype(F32)
        kz = jnp.concatenate([kp, kp], axis=1) * zeta_s[p]
        kv = _dot(jnp.transpose(kz).astype(BF16), v_ref[0, pl.ds(r0, CHUNK), cols])
        return jnp.where(diag2, kv, 0.0)

    for c in range(nc):
        for p in range(RET_PAIRS):
            kv_s[c, p] = contrib(crk_ref, crv_ref, c * CHUNK, p)

    def contrib_body(c, carry):
        r0 = pl.multiple_of(c * CHUNK, CHUNK)
        for p in range(RET_PAIRS):
            kv_s[nc + c, p] = contrib(rk_ref, rv_ref, r0, p)
        return carry

    lax.fori_loop(0, n, contrib_body, 0, unroll=4)

    for p in range(RET_PAIRS):
        gf, gb = g_chunk[p]
        sf = jnp.zeros((LANES, LANES), F32)
        sb = jnp.zeros((LANES, LANES), F32)
        for c in range(nc):
            sf = gf * sf + kv_s[c, p, 0:LANES]
            sb = gb * sb + kv_s[nc - 1 - c, p, LANES:]

        def fwd_scan(c, s, p=p, gf=gf):
            st_s[c, p, 0:LANES] = s.astype(BF16)
            return gf * s + kv_s[nc + c, p, 0:LANES]

        def bwd_scan(j, s, p=p, gb=gb):
            c = n - 1 - j
            st_s[c, p, LANES:] = s.astype(BF16)
            return gb * s + kv_s[nc + c, p, LANES:]

        lax.fori_loop(0, n, fwd_scan, sf)
        lax.fori_loop(0, n, bwd_scan, sb)

    def out_body(c, carry):
        r0 = pl.multiple_of(c * CHUNK, CHUNK)
        for p in range(RET_PAIRS):
            cols = slice(p * LANES, (p + 1) * LANES)
            qp = rq_ref[0, pl.ds(r0, CHUNK), cols]
            kp = rk_ref[0, pl.ds(r0, CHUNK), cols]
            vp = rv_ref[0, pl.ds(r0, CHUNK), cols]
            s2 = _dot_nt(qp, split_heads(kp))
            a2 = (s2 * m_s[p]).astype(BF16)
            y = _dot(a2, split_heads(vp))
            qf = qp.astype(F32)
            qx = (jnp.concatenate([qf, qf], axis=1) * xi_s[p]).astype(BF16)
            y += _dot(qx, st_s[c, p])
            hi, lo = _split(y * y)
            ms = _dot(jnp.concatenate([hi, lo], axis=1), seg2) * (1.0 / HEAD_DIM)
            out = y * lax.rsqrt(ms + EPS) * sg_ref[0, pl.ds(r0, CHUNK), cols].astype(F32)
            o_ref[0, pl.ds(r0, CHUNK), cols] = out.astype(BF16)
        return carry

    lax.fori_loop(0, n, out_body, 0, unroll=4)


def _retention(rq, rk, rv, sg, crk, crv, dec_f, dec_b):
    b, l, _ = rq.shape
    lc = crk.shape[1]
    n = l // CHUNK
    nc = lc // CHUNK
    spec = pl.BlockSpec((1, l, RET_W), lambda bi: (bi, 0, 0))
    cspec = pl.BlockSpec((1, lc, RET_W), lambda bi: (bi, 0, 0))
    dspec = pl.BlockSpec((1, RET_W), lambda bi: (0, 0))
    return pl.pallas_call(
        functools.partial(_ret_kernel, l=l, lc=lc),
        grid=(b,),
        in_specs=[spec, spec, spec, spec, cspec, cspec, dspec, dspec],
        out_specs=spec,
        out_shape=jax.ShapeDtypeStruct((b, l, RET_W), BF16),
        scratch_shapes=[pltpu.VMEM((RET_PAIRS, CHUNK, 2 * CHUNK), F32),
                        pltpu.VMEM((RET_PAIRS, CHUNK, 2 * LANES), F32),
                        pltpu.VMEM((RET_PAIRS, CHUNK, 2 * LANES), F32),
                        pltpu.VMEM((nc + n, RET_PAIRS, 2 * LANES, LANES), F32),
                        pltpu.VMEM((n, RET_PAIRS, 2 * LANES, LANES), BF16)],
        compiler_params=_cparams("arbitrary"),
        name="ret",
    )(rq, rk, rv, sg, crk, crv, dec_f, dec_b)


def _out_kernel(attn_ref, ret_ref, x_ref, wa_ref, wr_ref, g1_ref, sh_ref, sc_ref, nw_ref, rhi_ref, rlo_ref,
                x1_ref, hp_ref, lg_ref):
    y = _dot(attn_ref[0], wa_ref[...]) + _dot(ret_ref[0], wr_ref[...])
    x1 = x_ref[0] + g1_ref[0] * y
    x1_ref[0] = x1
    h = x1 * lax.rsqrt(jnp.mean(x1 * x1, axis=-1, keepdims=True) + EPS) * nw_ref[...]
    h = h * (1.0 + sc_ref[0]) + sh_ref[0]
    half = h.shape[1] // 2
    hp_ref[...] = _pack_halves(h[:, :half], h[:, half:])
    h_hi, h_lo = _split(h)
    lg_ref[...] = _dot_nt(rhi_ref[...], h_hi) + _dot_nt(rhi_ref[...], h_lo) + _dot_nt(rlo_ref[...], h_hi)


def _out_projection(attn, ret, x, wa, wr, g1, sh2, sc2, norm_w, r_hi, r_lo, *, tm):
    b, l, d = x.shape
    tm = min(tm, l)
    nt = l // tm
    t = b * l
    mspec = pl.BlockSpec((1, 1, d), lambda bi, i: (bi, 0, 0))
    return pl.pallas_call(
        _out_kernel,
        grid=(b, nt),
        in_specs=[pl.BlockSpec((1, tm, ATTN_W), lambda bi, i: (bi, i, 0)),
                  pl.BlockSpec((1, tm, RET_W), lambda bi, i: (bi, i, 0)),
                  pl.BlockSpec((1, tm, d), lambda bi, i: (bi, i, 0)),
                  pl.BlockSpec((ATTN_W, d), lambda bi, i: (0, 0)),
                  pl.BlockSpec((RET_W, d), lambda bi, i: (0, 0)),
                  mspec, mspec, mspec,
                  pl.BlockSpec((1, d), lambda bi, i: (0, 0)),
                  pl.BlockSpec((N_EXPERTS, d), lambda bi, i: (0, 0)),
                  pl.BlockSpec((N_EXPERTS, d), lambda bi, i: (0, 0))],
        out_specs=[pl.BlockSpec((1, tm, d), lambda bi, i: (bi, i, 0)),
                   pl.BlockSpec((tm, d // 2), lambda bi, i: (bi * nt + i, 0)),
                   pl.BlockSpec((N_EXPERTS, tm), lambda bi, i: (0, bi * nt + i))],
        out_shape=[jax.ShapeDtypeStruct((b, l, d), F32),
                   jax.ShapeDtypeStruct((t, d // 2), PACKED),
                   jax.ShapeDtypeStruct((N_EXPERTS, t), F32)],
        compiler_params=_cparams("arbitrary", "arbitrary"),
        name="out_proj",
    )(attn, ret, x, wa, wr, g1, sh2, sc2, norm_w, r_hi, r_lo)


def _route_kernel(lg_ref, bias_ref, idx_ref, w_ref, rank_ref, cnt_col_ref, cnt_row_ref, tri_s, col_s, row_s):
    tb = lg_ref.shape[1]
    step = pl.program_id(0)

    @pl.when(step == 0)
    def _():
        r = lax.broadcasted_iota(I32, (tb, tb), 0)
        c = lax.broadcasted_iota(I32, (tb, tb), 1)
        tri_s[...] = jnp.where(r <= c, 1.0, 0.0).astype(BF16)
        col_s[...] = jnp.zeros_like(col_s)
        row_s[...] = jnp.zeros_like(row_s)

    scores = _sigmoid(lg_ref[...])
    biased = scores + bias_ref[...]
    neg = -jnp.inf
    sub = lax.broadcasted_iota(I32, (GROUP_SIZE, tb), 0).astype(F32)

    gscore = []
    for g in range(N_GROUPS):
        blk = biased[g * GROUP_SIZE:(g + 1) * GROUP_SIZE]
        m1 = jnp.max(blk, axis=0, keepdims=True)
        first = jnp.min(jnp.where(blk == m1, sub, float(GROUP_SIZE)), axis=0, keepdims=True)
        m2 = jnp.max(jnp.where(sub == first, neg, blk), axis=0, keepdims=True)
        gscore.append(m1 + m2)
    gs = jnp.concatenate(gscore, axis=0)
    gsub = lax.broadcasted_iota(I32, (N_GROUPS, tb), 0).astype(F32)
    keep = jnp.zeros((N_GROUPS, tb), F32)
    for _ in range(TOPK_GROUPS):
        m = jnp.max(gs, axis=0, keepdims=True)
        first = jnp.min(jnp.where(gs == m, gsub, float(N_GROUPS)), axis=0, keepdims=True)
        sel = gsub == first
        keep = jnp.where(sel, 1.0, keep)
        gs = jnp.where(sel, neg, gs)
    masked = jnp.concatenate(
        [jnp.where(keep[g:g + 1] > 0.0, biased[g * GROUP_SIZE:(g + 1) * GROUP_SIZE], neg)
         for g in range(N_GROUPS)], axis=0)

    esub = lax.broadcasted_iota(I32, (N_EXPERTS, tb), 0).astype(F32)
    sels, idxs, ws = [], [], []
    chosen = jnp.zeros((N_EXPERTS, tb), F32)
    for _ in range(TOP_K):
        m = jnp.max(masked, axis=0, keepdims=True)
        first = jnp.min(jnp.where(masked == m, esub, float(N_EXPERTS)), axis=0, keepdims=True)
        sel = esub == first
        sels.append(sel)
        idxs.append(first)
        ws.append(jnp.sum(jnp.where(sel, scores, 0.0), axis=0, keepdims=True))
        chosen = jnp.where(sel, 1.0, chosen)
        masked = jnp.where(sel, neg, masked)
    wsum = ws[0]
    for k in range(1, TOP_K):
        wsum = wsum + ws[k]
    idx_ref[...] = jnp.concatenate(idxs, axis=0).astype(I32)
    w_ref[...] = jnp.concatenate([wk / wsum * ROUTED_SCALE for wk in ws], axis=0)

    chosen_b = chosen.astype(BF16)
    incl = _dot(chosen_b, tri_s[...])
    before = incl - chosen + col_s[...]
    rank_ref[...] = jnp.concatenate(
        [jnp.sum(jnp.where(sel, before, 0.0), axis=0, keepdims=True) for sel in sels], axis=0).astype(I32)
    col_s[...] = col_s[...] + incl[:, tb - 1:tb]
    row_s[...] = row_s[...] + _dot_nt(jnp.ones((8, tb), BF16), chosen_b)
    cnt_col_ref[...] = col_s[...].astype(I32)
    cnt_row_ref[...] = row_s[...].astype(I32)


def _route(logits_t, bias_col, *, tb):
    e, t = logits_t.shape
    tb = min(tb, t)
    kspec = pl.BlockSpec((TOP_K, tb), lambda i: (0, i))
    return pl.pallas_call(
        _route_kernel,
        grid=(t // tb,),
        in_specs=[pl.BlockSpec((e, tb), lambda i: (0, i)),
                  pl.BlockSpec((e, 1), lambda i: (0, 0))],
        out_specs=[kspec, kspec, kspec,
                   pl.BlockSpec((e, 1), lambda i: (0, 0)),
                   pl.BlockSpec((8, e), lambda i: (0, 0))],
        out_shape=[jax.ShapeDtypeStruct((TOP_K, t), I32),
                   jax.ShapeDtypeStruct((TOP_K, t), F32),
                   jax.ShapeDtypeStruct((TOP_K, t), I32),
                   jax.ShapeDtypeStruct((e, 1), I32),
                   jax.ShapeDtypeStruct((8, e), I32)],
        scratch_shapes=[pltpu.VMEM((tb, tb), BF16), pltpu.VMEM((e, 1), F32), pltpu.VMEM((8, e), F32)],
        compiler_params=_cparams("arbitrary"),
        name="route",
    )(logits_t, bias_col)


def _pad_block(cnt):
    return (cnt + (MOE_BLOCK - 1)) // MOE_BLOCK * MOE_BLOCK


def _max_items(n_blocks):
    return n_blocks // ITEM_BLOCKS + N_EXPERTS


def _dest_kernel(idx_ref, rank_ref, cnt_col_ref, cnt_row_ref, dest_ref, meta_ref, items_ref, start_s):
    tb = idx_ref.shape[1]
    nip = items_ref.shape[1]

    @pl.when(pl.program_id(0) == 0)
    def _():
        pad_col = _pad_block(cnt_col_ref[...])
        pad_row = _pad_block(cnt_row_ref[0:1, :])
        er = lax.broadcasted_iota(I32, (N_EXPERTS, N_EXPERTS), 0)
        ec = lax.broadcasted_iota(I32, (N_EXPERTS, N_EXPERTS), 1)
        start_col = jnp.sum(jnp.where(ec < er, pad_row, 0), axis=1, keepdims=True)
        start_row = jnp.sum(jnp.where(er < ec, pad_col, 0), axis=0, keepdims=True)
        start_s[...] = start_col

        used = jnp.sum(pad_row, axis=1, keepdims=True) // MOE_BLOCK
        meta_ref[...] = jnp.concatenate(
            [cnt_row_ref[0:1, :], start_row, pad_row, jnp.broadcast_to(used, (1, N_EXPERTS)),
             jnp.zeros((4, N_EXPERTS), I32)], axis=0)

        nb_col = pad_col // MOE_BLOCK
        it_col = (nb_col + (ITEM_BLOCKS - 1)) // ITEM_BLOCKS
        it_row = (pad_row // MOE_BLOCK + (ITEM_BLOCKS - 1)) // ITEM_BLOCKS
        it_start = jnp.sum(jnp.where(ec < er, it_row, 0), axis=1, keepdims=True)
        n_items = jnp.sum(it_row, axis=1, keepdims=True)
        lane = lax.broadcasted_iota(I32, (1, nip), 1)
        owner = jnp.sum(jnp.where(it_start + it_col <= lane, 1, 0), axis=0, keepdims=True)
        owner = jnp.minimum(owner, N_EXPERTS - 1)
        onehot = lax.broadcasted_iota(I32, (N_EXPERTS, nip), 0) == owner

        def pick(col):
            return jnp.sum(jnp.where(onehot, col, 0), axis=0, keepdims=True)

        j = lane - pick(it_start)
        block0 = pick(start_col) // MOE_BLOCK + ITEM_BLOCKS * j
        nvalid = jnp.clip(pick(nb_col) - ITEM_BLOCKS * j, 0, ITEM_BLOCKS)
        items_ref[...] = jnp.concatenate(
            [owner, block0, jnp.where(lane < n_items, nvalid, 0), jnp.broadcast_to(n_items, (1, nip)),
             jnp.zeros((4, nip), I32)], axis=0)

    start_col = start_s[...]
    esub = lax.broadcasted_iota(I32, (N_EXPERTS, tb), 0)
    rows = []
    for k in range(TOP_K):
        onehot = esub == idx_ref[k:k + 1, :]
        rows.append(jnp.sum(jnp.where(onehot, start_col, 0), axis=0, keepdims=True) + rank_ref[k:k + 1, :])
    dest_ref[0] = jnp.concatenate(rows, axis=0)


def _destinations(idx_t, rank_t, cnt_col, cnt_row, *, tb, n_blocks):
    _, t = idx_t.shape
    tb = min(tb, t)
    nip = (_max_items(n_blocks) + LANES - 1) // LANES * LANES
    kspec = pl.BlockSpec((TOP_K, tb), lambda i: (0, i))
    return pl.pallas_call(
        _dest_kernel,
        grid=(t // tb,),
        in_specs=[kspec, kspec,
                  pl.BlockSpec((N_EXPERTS, 1), lambda i: (0, 0)),
                  pl.BlockSpec((8, N_EXPERTS), lambda i: (0, 0))],
        out_specs=[pl.BlockSpec((1, TOP_K, tb), lambda i: (i, 0, 0)),
                   pl.BlockSpec((8, N_EXPERTS), lambda i: (0, 0)),
                   pl.BlockSpec((8, nip), lambda i: (0, 0))],
        out_shape=[jax.ShapeDtypeStruct((t // tb, TOP_K, tb), I32),
                   jax.ShapeDtypeStruct((8, N_EXPERTS), I32),
                   jax.ShapeDtypeStruct((8, nip), I32)],
        scratch_shapes=[pltpu.VMEM((N_EXPERTS, 1), I32)],
        compiler_params=_cparams("arbitrary"),
        name="dest",
    )(idx_t, rank_t, cnt_col, cnt_row)


_PAD_BITS = (64, 32, 16, 8)


def _sc_scatter_rows(rows, dest_win, n_out_rows):
    n_win, n_slots, win = dest_win.shape
    width = rows.shape[1]
    info = plsc.get_sparse_core_info()
    n_workers = info.num_cores * info.num_subcores
    per_worker = n_win // n_workers
    assert per_worker * n_workers == n_win and win <= LANES
    mesh = plsc.VectorSubcoreMesh(core_axis_name="c", subcore_axis_name="s")

    def body(rows_hbm, dest_hbm, out_hbm, idx_a, idx_b, rows_a, rows_b, sem_a, sem_b):
        wid = lax.axis_index("s") * info.num_cores + lax.axis_index("c")
        bufs = ((idx_a, rows_a, sem_a), (idx_b, rows_b, sem_b))

        def load(j):
            idx_v, rows_v, _ = bufs[j % 2]
            w = wid * per_worker + j
            pltpu.sync_copy(dest_hbm.at[w], idx_v)
            pltpu.sync_copy(rows_hbm.at[pl.ds(w * win, win)], rows_v)

        load(0)
        in_flight = []
        for j in range(per_worker):
            idx_v, rows_v, sem = bufs[j % 2]
            copies = [pltpu.async_copy(rows_v, out_hbm.at[idx_v.at[k]], sem) for k in range(n_slots)]
            for cp in in_flight:
                cp.wait()
            if j + 1 < per_worker:
                load(j + 1)
            in_flight = copies
        for cp in in_flight:
            cp.wait()

    return pl.kernel(
        body,
        out_type=jax.ShapeDtypeStruct((n_out_rows, width), rows.dtype),
        mesh=mesh,
        scratch_types=[pltpu.VMEM((n_slots, win), I32), pltpu.VMEM((n_slots, win), I32),
                       pltpu.VMEM((win, width), rows.dtype), pltpu.VMEM((win, width), rows.dtype),
                       pltpu.SemaphoreType.DMA, pltpu.SemaphoreType.DMA],
        name="sc_scatter",
    )(rows, dest_win)


def _pad_fill_kernel(meta_ref, xs_in, xs_hbm, zero_s, sem_z, *, e_per_step):
    del xs_in
    step = pl.program_id(0)
    zero_s[...] = jnp.zeros_like(zero_s)

    def tail_copy(c):
        row0 = pl.multiple_of((meta_ref[3, 0] + c) * MOE_BLOCK, MOE_BLOCK)
        return pltpu.make_async_copy(zero_s, xs_hbm.at[pl.ds(row0, MOE_BLOCK)], sem_z)

    @pl.when(step == 0)
    def _():
        for c in range(ITEM_BLOCKS - 1):
            tail_copy(c).start()
        for c in range(ITEM_BLOCKS - 1):
            tail_copy(c).wait()

    def pad_copies(e):
        cnt = meta_ref[0, e]
        off = meta_ref[1, e] + cnt
        rem = meta_ref[2, e] - cnt
        head = rem & (SUBLANES - 1)
        out = []
        for i in range(SUBLANES - 1):
            out.append((i < head,
                        pltpu.make_async_copy(zero_s.at[pl.ds(0, 1)], xs_hbm.at[pl.ds(off + i, 1)], sem_z)))
        off = off + head
        for bit in _PAD_BITS:
            out.append(((rem & bit) != 0,
                        pltpu.make_async_copy(zero_s.at[pl.ds(0, bit)],
                                              xs_hbm.at[pl.ds(pl.multiple_of(off, SUBLANES), bit)], sem_z)))
            off = off + (rem & bit)
        return out

    for j in range(e_per_step):
        for cond, c in pad_copies(step * e_per_step + j):
            pl.when(cond)(c.start)
    for j in range(e_per_step):
        for cond, c in pad_copies(step * e_per_step + j):
            pl.when(cond)(c.wait)


PAD_FILL_EXPERTS_PER_STEP = 8


def _pad_fill(meta, xs):
    half = xs.shape[1]
    return pl.pallas_call(
        functools.partial(_pad_fill_kernel, e_per_step=PAD_FILL_EXPERTS_PER_STEP),
        grid=(N_EXPERTS // PAD_FILL_EXPERTS_PER_STEP,),
        in_specs=[pl.BlockSpec(memory_space=pltpu.SMEM),
                  pl.BlockSpec(memory_space=pl.ANY)],
        out_specs=pl.BlockSpec(memory_space=pl.ANY),
        out_shape=jax.ShapeDtypeStruct(xs.shape, xs.dtype),
        input_output_aliases={1: 0},
        scratch_shapes=[pltpu.VMEM((MOE_BLOCK, half), PACKED), pltpu.SemaphoreType.DMA],
        compiler_params=_cparams("arbitrary"),
        name="pad_fill",
    )(meta, xs)


def _experts_kernel(items_ref, xs_hbm, wg_hbm, wu_hbm, wd_hbm, ys_hbm,
                    xbuf, ybuf, wg_f, wu_f, wd_f, wg_s, wu_s, wd_s, sem_x, sem_y, sem_w):
    n_items = items_ref[3, 0]
    rows = ITEM_BLOCKS * MOE_BLOCK

    def w_copies(e, s):
        return [pltpu.make_async_copy(src.at[e], dst.at[s], sem_w.at[s])
                for src, dst in ((wg_hbm, wg_f), (wu_hbm, wu_f), (wd_hbm, wd_f))]

    def x_copy(item, s):
        row0 = pl.multiple_of(items_ref[1, item] * MOE_BLOCK, MOE_BLOCK)
        return pltpu.make_async_copy(xs_hbm.at[pl.ds(row0, rows)], xbuf.at[s], sem_x.at[s])

    def y_copies(item, s, fn):
        for k in range(ITEM_BLOCKS):
            row0 = pl.multiple_of((items_ref[1, item] + k) * MOE_BLOCK, MOE_BLOCK)
            cp = pltpu.make_async_copy(ybuf.at[s, pl.ds(k * MOE_BLOCK, MOE_BLOCK)],
                                       ys_hbm.at[pl.ds(row0, MOE_BLOCK)], sem_y.at[s])
            pl.when(k < items_ref[2, item])(functools.partial(fn, cp))

    def expert_of(item):
        return items_ref[0, jnp.minimum(item, n_items - 1)]

    def changes_at(item):
        return ((item < n_items) & (expert_of(item) != expert_of(item - 1))).astype(I32)

    @pl.when(n_items > 0)
    def _():
        x_copy(0, 0).start()
        for cp in w_copies(expert_of(0), 0):
            cp.start()

        @pl.when(changes_at(1) == 1)
        def _():
            for cp in w_copies(expert_of(1), 1):
                cp.start()

    def item_body(i, ordinal):
        slot = i % 2
        prev = jnp.maximum(i - 1, 0)
        e = expert_of(i)
        new_expert = (i == 0) | (e != expert_of(prev))
        c1 = changes_at(i + 1)
        c2 = changes_at(i + 2)

        @pl.when(i + 1 < n_items)
        def _():
            x_copy(i + 1, 1 - slot).start()

        @pl.when(new_expert)
        def _():
            wslot = ordinal % WEIGHT_SLOTS
            for cp in w_copies(e, wslot):
                cp.wait()
            wg_s[...] = wg_f[wslot].astype(BF16)
            wu_s[...] = wu_f[wslot].astype(BF16)
            wd_s[...] = wd_f[wslot].astype(BF16)

        @pl.when(c2 == 1)
        def _():
            for cp in w_copies(expert_of(i + 2), (ordinal + c1 + 1) % WEIGHT_SLOTS):
                cp.start()

        x_copy(i, slot).wait()
        xa, xb = _unpack_halves(xbuf[slot])
        xa = xa.astype(BF16)
        xb = xb.astype(BF16)
        half = xa.shape[1]
        g = _dot(xa, wg_s[0:half]) + _dot(xb, wg_s[half:])
        u = _dot(xa, wu_s[0:half]) + _dot(xb, wu_s[half:])
        y = _dot((_silu(g) * u).astype(BF16), wd_s[...])
        ybuf[slot] = _pack_halves(y[:, :half], y[:, half:])
        y_copies(i, slot, lambda cp: cp.start())

        @pl.when(i > 0)
        def _():
            y_copies(prev, 1 - slot, lambda cp: cp.wait())

        return ordinal + c1

    lax.fori_loop(0, n_items, item_body, jnp.int32(0))

    @pl.when(n_items > 0)
    def _():
        last = n_items - 1
        y_copies(last, last % 2, lambda cp: cp.wait())


def _experts(items, xs, w_gate, w_up, w_down, *, n_blocks):
    half = xs.shape[1]
    e, d, f = w_gate.shape
    rows = ITEM_BLOCKS * MOE_BLOCK
    any_spec = pl.BlockSpec(memory_space=pl.ANY)
    return pl.pallas_call(
        _experts_kernel,
        grid_spec=pltpu.PrefetchScalarGridSpec(
            num_scalar_prefetch=1,
            grid=(1,),
            in_specs=[any_spec, any_spec, any_spec, any_spec],
            out_specs=any_spec,
            scratch_shapes=[pltpu.VMEM((2, rows, half), PACKED), pltpu.VMEM((2, rows, half), PACKED),
                            pltpu.VMEM((WEIGHT_SLOTS, d, f), F32), pltpu.VMEM((WEIGHT_SLOTS, d, f), F32),
                            pltpu.VMEM((WEIGHT_SLOTS, f, d), F32),
                            pltpu.VMEM((d, f), BF16), pltpu.VMEM((d, f), BF16), pltpu.VMEM((f, d), BF16),
                            pltpu.SemaphoreType.DMA((2,)), pltpu.SemaphoreType.DMA((2,)),
                            pltpu.SemaphoreType.DMA((WEIGHT_SLOTS,))]),
        out_shape=jax.ShapeDtypeStruct((n_blocks * MOE_BLOCK, half), PACKED),
        compiler_params=_cparams("arbitrary"),
        name="experts",
    )(items, xs, w_gate, w_up, w_down)


def _sc_gather_rows(table, idx):
    n_idx = idx.shape[0]
    width = table.shape[1]
    info = plsc.get_sparse_core_info()
    n_workers = info.num_cores * info.num_subcores
    per_worker = n_idx // n_workers
    assert per_worker * n_workers == n_idx and per_worker % (SC_GATHER_BUFS * SC_WINDOW) == 0
    mesh = plsc.VectorSubcoreMesh(core_axis_name="c", subcore_axis_name="s")

    def body(table_hbm, idx_hbm, out_hbm, idx_v, rows_v, sem_g, sem_o):
        wid = lax.axis_index("s") * info.num_cores + lax.axis_index("c")
        base = wid * per_worker

        @pl.loop(0, per_worker // (SC_GATHER_BUFS * SC_WINDOW))
        def _(it):
            offs = [base + (it * SC_GATHER_BUFS + b) * SC_WINDOW for b in range(SC_GATHER_BUFS)]
            gathers = []
            for b, off in enumerate(offs):
                pltpu.sync_copy(idx_hbm.at[pl.ds(off, SC_WINDOW)], idx_v.at[b])
                gathers.append(pltpu.async_copy(table_hbm.at[idx_v.at[b]], rows_v.at[b], sem_g.at[b]))
            writes = []
            for b, off in enumerate(offs):
                gathers[b].wait()
                writes.append(pltpu.async_copy(rows_v.at[b], out_hbm.at[pl.ds(off, SC_WINDOW)], sem_o.at[b]))
            for cp in writes:
                cp.wait()

    return pl.kernel(
        body,
        out_type=jax.ShapeDtypeStruct((n_idx, width), table.dtype),
        mesh=mesh,
        scratch_types=[pltpu.VMEM((SC_GATHER_BUFS, SC_WINDOW), I32),
                       pltpu.VMEM((SC_GATHER_BUFS, SC_WINDOW, width), table.dtype),
                       pltpu.SemaphoreType.DMA((SC_GATHER_BUFS,)), pltpu.SemaphoreType.DMA((SC_GATHER_BUFS,))],
        name="sc_gather",
    )(table, idx)


def _combine_kernel(hp_ref, x1_ref, g2_ref, w_ref, sgw_ref, suw_ref, sdw_ref, yg_ref, *rest):
    o_ref = rest[-1]
    xa, xb = _unpack_halves(hp_ref[...])
    xa = xa.astype(BF16)
    xb = xb.astype(BF16)
    half = xa.shape[1]
    tb = xa.shape[0]
    g = _dot(xa, sgw_ref[0:half]) + _dot(xb, sgw_ref[half:])
    u = _dot(xa, suw_ref[0:half]) + _dot(xb, suw_ref[half:])
    shared = _dot((_silu(g) * u).astype(BF16), sdw_ref[...])

    w = w_ref[...]
    acc_a = jnp.zeros((tb, half), F32)
    acc_b = jnp.zeros((tb, half), F32)
    for k in range(TOP_K):
        ya, yb = _unpack_halves(yg_ref[0, k])
        acc_a += ya * w[:, k:k + 1]
        acc_b += yb * w[:, k:k + 1]
    g2 = g2_ref[0]
    o_ref[:, 0:half] = x1_ref[:, 0:half] + g2[:, 0:half] * (acc_a + shared[:, 0:half])
    o_ref[:, half:] = x1_ref[:, half:] + g2[:, half:] * (acc_b + shared[:, half:])


def _combine(hp, x1, g2, w_tok, sgw, suw, sdw, yg, out_prev, *, tb, seq_len, first_step):
    t, half = hp.shape
    d = 2 * half
    per_seq = seq_len // tb
    f = sgw.shape[1]
    s0 = first_step
    in_specs = [pl.BlockSpec((tb, half), lambda i: (s0 + i, 0)),
                pl.BlockSpec((tb, d), lambda i: (s0 + i, 0)),
                pl.BlockSpec((1, 1, d), lambda i: ((s0 + i) // per_seq, 0, 0)),
                pl.BlockSpec((tb, TOP_K), lambda i: (s0 + i, 0)),
                pl.BlockSpec((d, f), lambda i: (0, 0)),
                pl.BlockSpec((d, f), lambda i: (0, 0)),
                pl.BlockSpec((f, d), lambda i: (0, 0)),
                pl.BlockSpec((1, TOP_K, tb, half), lambda i: (i, 0, 0, 0))]
    args = [hp, x1, g2, w_tok, sgw, suw, sdw, yg]
    aliases = {}
    if out_prev is not None:
        in_specs.append(pl.BlockSpec(memory_space=pl.ANY))
        args.append(out_prev)
        aliases = {len(args) - 1: 0}
    return pl.pallas_call(
        _combine_kernel,
        grid=(yg.shape[0],),
        in_specs=in_specs,
        out_specs=pl.BlockSpec((tb, d), lambda i: (s0 + i, 0)),
        out_shape=jax.ShapeDtypeStruct((t, d), F32),
        input_output_aliases=aliases,
        compiler_params=_cparams("arbitrary"),
        name="combine",
    )(*args)


def _rope_tables(l):
    rows = l // GRID_W
    r = jnp.repeat(jnp.arange(rows), GRID_W).astype(F32)
    col = jnp.tile(jnp.arange(GRID_W), rows).astype(F32)
    n_f = HEAD_DIM // 4
    freqs = ROPE_THETA ** (-jnp.arange(n_f, dtype=F32) / n_f)
    ang = jnp.concatenate([r[:, None] * freqs, col[:, None] * freqs], axis=-1)
    ang = jnp.tile(jnp.repeat(ang, 2, axis=1), (1, LANES // HEAD_DIM))
    sign = jnp.where(jnp.arange(LANES) % 2 == 0, -1.0, 1.0).astype(F32)
    return jnp.cos(ang), jnp.sin(ang) * sign


def kernel(x, c, ctx, c_ctx, w_mod, b_mod, norm1_w, norm2_w, w_in, q_norm_w, k_norm_w, ret_decay_fwd,
           ret_decay_bwd, w_out, router_w, router_bias, exp_w_gate, exp_w_up, exp_w_down, sh_w_gate,
           sh_w_up, sh_w_down):
    b, l, d = x.shape
    lc = ctx.shape[1]
    t = b * l
    assert w_mod.shape[0] == 1, "single layer"
    assert l % CHUNK == 0 and lc % CHUNK == 0 and l % GRID_W == 0

    rows = (b + 1 + 7) // 8 * 8
    cc = jnp.zeros((rows, d), F32).at[:b].set(c).at[b].set(c_ctx)
    mod = _modulation(cc, w_mod[0], b_mod[0])
    sh1, sc1, g1, sh2, sc2, g2 = [mod[:b, i * d:(i + 1) * d].reshape(b, 1, d) for i in range(6)]
    shc = mod[b, 0:d].reshape(1, 1, d)
    scc = mod[b, d:2 * d].reshape(1, 1, d)

    wi = w_in[0].astype(BF16)
    qnw = jnp.tile(q_norm_w[0], LANES // HEAD_DIM).reshape(1, LANES)
    knw = jnp.tile(k_norm_w[0], LANES // HEAD_DIM).reshape(1, LANES)
    cos, sin = _rope_tables(l)
    n1 = norm1_w[0].reshape(1, d)

    cklo, ckhi, cvlo, cvhi, crk, crv = _projection(
        ctx, shc, scc, n1, wi, qnw, knw, cos[:lc], sin[:lc], rope=False, with_q=False, tm=TILE_PROJ)
    klo, khi, vlo, vhi, rk, rv, q, rq, sg = _projection(
        x, sh1, sc1, n1, wi, qnw, knw, cos, sin, rope=True, with_q=True, tm=TILE_PROJ)

    bound = (HEAD_DIM * QK_SCALE * LOG2_E * BOUND_MARGIN
             * jnp.max(jnp.abs(q_norm_w[0])) * jnp.max(jnp.abs(k_norm_w[0]))).astype(F32)
    attn_args = (bound.reshape(1), q, klo, khi, vlo, vhi, cklo, ckhi, cvlo, cvhi)
    attn = lax.cond(bound <= MAX_STREAM_SHIFT,
                    functools.partial(_attention, tq=TILE_ATTN_Q, streaming=True),
                    functools.partial(_attention, tq=TILE_ATTN_Q, streaming=False), *attn_args)
    dec_f = jnp.repeat(ret_decay_fwd[0].astype(F32), HEAD_DIM).reshape(1, RET_W)
    dec_b = jnp.repeat(ret_decay_bwd[0].astype(F32), HEAD_DIM).reshape(1, RET_W)
    ret = _retention(rq, rk, rv, sg, crk, crv, dec_f, dec_b)

    wo = w_out[0].astype(BF16)
    r_hi, r_lo = _split(router_w[0].T)
    x1, hp, logits_t = _out_projection(attn, ret, x, wo[:ATTN_W], wo[ATTN_W:], g1, sh2, sc2,
                                       norm2_w[0].reshape(1, d), r_hi, r_lo, tm=TILE_OUT)

    idx_t, w_t, rank_t, cnt_col, cnt_row = _route(logits_t, router_bias[0].reshape(N_EXPERTS, 1), tb=TILE_TOKENS)
    n_blocks = -(-(t * TOP_K) // MOE_BLOCK) + N_EXPERTS
    tb = TILE_TOKENS
    dest, meta, items = _destinations(idx_t, rank_t, cnt_col, cnt_row, tb=tb, n_blocks=n_blocks)
    steps, _, tbe = dest.shape
    dest_win = dest.reshape(steps, TOP_K, tbe // SC_WINDOW, SC_WINDOW).transpose(0, 2, 1, 3)
    dest_win = dest_win.reshape(t // SC_WINDOW, TOP_K, SC_WINDOW)
    xs = _sc_scatter_rows(hp, dest_win, (n_blocks + ITEM_BLOCKS - 1) * MOE_BLOCK)
    xs = _pad_fill(meta, xs)
    ys = _experts(items, xs, exp_w_gate[0], exp_w_up[0], exp_w_down[0], n_blocks=n_blocks)
    parts = COMBINE_PARTS if steps % COMBINE_PARTS == 0 else 1
    steps_part = steps // parts
    x1f, w_tok = x1.reshape(t, d), w_t.T
    sgw, suw, sdw = sh_w_gate[0].astype(BF16), sh_w_up[0].astype(BF16), sh_w_down[0].astype(BF16)
    out = None
    for p in range(parts):
        idx = dest[p * steps_part:(p + 1) * steps_part].reshape(-1)
        yg = _sc_gather_rows(ys, idx).reshape(steps_part, TOP_K, tbe, d // 2)
        out = _combine(hp, x1f, g2, w_tok, sgw, suw, sdw, yg, out, tb=tbe, seq_len=l, first_step=p * steps_part)
    return out.reshape(b, l, d)
```

```python
import functools

import jax
import jax.numpy as jnp
from jax import lax
from jax.experimental import pallas as pl
from jax.experimental.pallas import tpu as pltpu
from jax.experimental.pallas import tpu_sc as plsc

F32 = jnp.float32
BF16 = jnp.bfloat16
I32 = jnp.int32
U32 = jnp.uint32
PACKED = jnp.int32

HEAD_DIM = 64
LANES = 128
SUBLANES = 8
ATTN_HEADS = 8
ATTN_KV_HEADS = 2
GQA = ATTN_HEADS // ATTN_KV_HEADS
RET_HEADS = 8
ATTN_W = ATTN_HEADS * HEAD_DIM
KV_W = ATTN_KV_HEADS * HEAD_DIM
RET_W = RET_HEADS * HEAD_DIM
RET_PAIRS = RET_W // LANES
CHUNK = 128
GRID_W = 64
ROPE_THETA = 10000.0
N_EXPERTS = 256
TOP_K = 8
N_GROUPS = 8
GROUP_SIZE = N_EXPERTS // N_GROUPS
TOPK_GROUPS = 4
ROUTED_SCALE = 2.5
MOE_BLOCK = 128
EPS = 1e-6
QK_SCALE = HEAD_DIM ** -0.5
LOG2_E = 1.4426950408889634
KEY_TILE = 256
BOUND_MARGIN = 1.02
MAX_STREAM_SHIFT = 56.0

OFF_AK = 0
OFF_AV = OFF_AK + KV_W
OFF_RK = OFF_AV + KV_W
OFF_RV = OFF_RK + RET_W
CTX_KV_COLS = OFF_RV + RET_W
OFF_AQ = CTX_KV_COLS
OFF_RQ = OFF_AQ + ATTN_W
OFF_RG = OFF_RQ + RET_W
IN_COLS = OFF_RG + RET_W

VMEM_LIMIT = 52 * 1024 * 1024

SUB_ROWS = 256
TILE_PROJ = 1024
TILE_ATTN_Q = 1024
TILE_OUT = 512
TILE_TOKENS = 512
ITEM_BLOCKS = 5
COMBINE_PARTS = 4
SC_GATHER_BUFS = 2
WEIGHT_SLOTS = 3
SC_WINDOW = 64
HI_MASK = 0xFFFF0000


def _cparams(*sem):
    return pltpu.CompilerParams(dimension_semantics=sem, vmem_limit_bytes=VMEM_LIMIT)


def _split(a):
    hi = a.astype(BF16)
    lo = (a - hi.astype(F32)).astype(BF16)
    return hi, lo


def _dot(a, b):
    return jnp.dot(a, b, preferred_element_type=F32)


def _dot_nt(a, b):
    return lax.dot_general(a, b, (((1,), (1,)), ((), ())), preferred_element_type=F32)


def _sigmoid(v):
    return 1.0 / (1.0 + jnp.exp(-v))


def _silu(v):
    return v * _sigmoid(v)


def _pack_halves(a, b):
    ua = lax.bitcast_convert_type(a.astype(BF16).astype(F32), U32)
    ub = lax.bitcast_convert_type(b.astype(BF16).astype(F32), U32)
    return lax.bitcast_convert_type((ua & jnp.uint32(HI_MASK)) | (ub >> 16), PACKED)


def _unpack_halves(p):
    u = lax.bitcast_convert_type(p, U32)
    a = lax.bitcast_convert_type(u & jnp.uint32(HI_MASK), F32)
    b = lax.bitcast_convert_type(u << 16, F32)
    return a, b


def _mod_kernel(c_ref, w_ref, b_ref, o_ref):
    s_hi, s_lo = _split(_silu(c_ref[...]))
    w_hi, w_lo = _split(w_ref[...])
    o_ref[...] = _dot(s_hi, w_hi) + _dot(s_hi, w_lo) + _dot(s_lo, w_hi) + b_ref[...]


def _modulation(cc, w_mod, b_mod):
    rows, d = cc.shape
    n = w_mod.shape[1]
    tn = 768
    return pl.pallas_call(
        _mod_kernel,
        grid=(n // tn,),
        in_specs=[pl.BlockSpec((rows, d), lambda j: (0, 0)),
                  pl.BlockSpec((d, tn), lambda j: (0, j)),
                  pl.BlockSpec((1, tn), lambda j: (0, j))],
        out_specs=pl.BlockSpec((rows, tn), lambda j: (0, j)),
        out_shape=jax.ShapeDtypeStruct((rows, n), F32),
        compiler_params=_cparams("arbitrary"),
        name="mod",
    )(cc, w_mod, b_mod.reshape(1, n))


def _segment_ones():
    r = lax.broadcasted_iota(I32, (LANES, LANES), 0) // HEAD_DIM
    c = lax.broadcasted_iota(I32, (LANES, LANES), 1) // HEAD_DIM
    return jnp.where(r == c, 1.0, 0.0).astype(BF16)


def _head_mean_sq(v, seg):
    hi, lo = _split(v * v)
    return (_dot(hi, seg) + _dot(lo, seg)) * (1.0 / HEAD_DIM)


def _proj_kernel(x_ref, sh_ref, sc_ref, nw_ref, wi_ref, qnw_ref, knw_ref, cos_ref, sin_ref,
                 *out_refs, rope, with_q):
    if with_q:
        klo_ref, khi_ref, vlo_ref, vhi_ref, rk_ref, rv_ref, q_ref, rq_ref, sg_ref = out_refs
    else:
        klo_ref, khi_ref, vlo_ref, vhi_ref, rk_ref, rv_ref = out_refs
    tm = x_ref.shape[1]
    sub = min(tm, SUB_ROWS)
    seg = _segment_ones()
    lane = lax.broadcasted_iota(I32, (sub, LANES), 1)
    low_half = lane < HEAD_DIM
    even = (lane & 1) == 0

    for r0 in range(0, tm, sub):
        rows = slice(r0, r0 + sub)
        x = x_ref[0, rows, :]
        h = x * lax.rsqrt(jnp.mean(x * x, axis=-1, keepdims=True) + EPS) * nw_ref[...]
        h = h * (1.0 + sc_ref[0]) + sh_ref[0]
        z = _dot(h.astype(BF16), wi_ref[...])

        def norm_rope(v, w128, rows=rows):
            v = v * lax.rsqrt(_head_mean_sq(v, seg) + EPS) * w128
            if rope:
                swapped = jnp.where(even, pltpu.roll(v, LANES - 1, 1), pltpu.roll(v, 1, 1))
                v = v * cos_ref[rows, :] + swapped * sin_ref[rows, :]
            return v

        k = norm_rope(z[:, OFF_AK:OFF_AK + KV_W], knw_ref[...])
        ksw = pltpu.roll(k, HEAD_DIM, 1)
        klo_ref[0, 0, :, rows] = jnp.transpose(jnp.where(low_half, k, 0.0)).astype(BF16)
        khi_ref[0, 0, :, rows] = jnp.transpose(jnp.where(low_half, 0.0, ksw)).astype(BF16)
        klo_ref[0, 1, :, rows] = jnp.transpose(jnp.where(low_half, ksw, 0.0)).astype(BF16)
        khi_ref[0, 1, :, rows] = jnp.transpose(jnp.where(low_half, 0.0, k)).astype(BF16)
        v = z[:, OFF_AV:OFF_AV + KV_W]
        vsw = pltpu.roll(v, HEAD_DIM, 1)
        vlo_ref[0, 0, rows, :] = jnp.where(low_half, v, 0.0).astype(BF16)
        vhi_ref[0, 0, rows, :] = jnp.where(low_half, 0.0, vsw).astype(BF16)
        vlo_ref[0, 1, rows, :] = jnp.where(low_half, vsw, 0.0).astype(BF16)
        vhi_ref[0, 1, rows, :] = jnp.where(low_half, 0.0, v).astype(BF16)
        rk_ref[0, rows, :] = (z[:, OFF_RK:OFF_RK + RET_W] * QK_SCALE).astype(BF16)
        rv_ref[0, rows, :] = z[:, OFF_RV:OFF_RV + RET_W].astype(BF16)
        if with_q:
            for j in range(ATTN_W // LANES):
                qj = norm_rope(z[:, OFF_AQ + j * LANES:OFF_AQ + (j + 1) * LANES], qnw_ref[...])
                q_ref[0, rows, j * LANES:(j + 1) * LANES] = (qj * (QK_SCALE * LOG2_E)).astype(BF16)
            rq_ref[0, rows, :] = z[:, OFF_RQ:OFF_RQ + RET_W].astype(BF16)
            sg_ref[0, rows, :] = _silu(z[:, OFF_RG:OFF_RG + RET_W]).astype(BF16)


def _projection(x, shift, scale, norm_w, wi_bf16, qnw, knw, cos, sin, *, rope, with_q, tm):
    b, l, d = x.shape
    tm = min(tm, l)
    ncols = IN_COLS if with_q else CTX_KV_COLS
    per_batch = shift.shape[0] > 1
    mod_idx = (lambda bi, i: (bi, 0, 0)) if per_batch else (lambda bi, i: (0, 0, 0))
    kv_shape = jax.ShapeDtypeStruct((b, ATTN_KV_HEADS, l, LANES), BF16)
    kv_spec = pl.BlockSpec((1, ATTN_KV_HEADS, tm, LANES), lambda bi, i: (bi, 0, i, 0))
    kt_shape = jax.ShapeDtypeStruct((b, ATTN_KV_HEADS, LANES, l), BF16)
    kt_spec = pl.BlockSpec((1, ATTN_KV_HEADS, LANES, tm), lambda bi, i: (bi, 0, 0, i))
    w_shape = jax.ShapeDtypeStruct((b, l, RET_W), BF16)
    w_spec = pl.BlockSpec((1, tm, RET_W), lambda bi, i: (bi, i, 0))
    out_shape = [kt_shape] * 2 + [kv_shape] * 2 + [w_shape] * 2
    out_specs = [kt_spec] * 2 + [kv_spec] * 2 + [w_spec] * 2
    if with_q:
        out_shape += [w_shape] * 3
        out_specs += [w_spec] * 3
    return pl.pallas_call(
        functools.partial(_proj_kernel, rope=rope, with_q=with_q),
        grid=(b, l // tm),
        in_specs=[pl.BlockSpec((1, tm, d), lambda bi, i: (bi, i, 0)),
                  pl.BlockSpec((1, 1, d), mod_idx),
                  pl.BlockSpec((1, 1, d), mod_idx),
                  pl.BlockSpec((1, d), lambda bi, i: (0, 0)),
                  pl.BlockSpec((d, ncols), lambda bi, i: (0, 0)),
                  pl.BlockSpec((1, LANES), lambda bi, i: (0, 0)),
                  pl.BlockSpec((1, LANES), lambda bi, i: (0, 0)),
                  pl.BlockSpec((tm, LANES), lambda bi, i: (i, 0)),
                  pl.BlockSpec((tm, LANES), lambda bi, i: (i, 0))],
        out_specs=out_specs,
        out_shape=out_shape,
        compiler_params=_cparams("arbitrary", "arbitrary"),
        name="proj_latent" if with_q else "proj_ctx",
    )(x, shift, scale, norm_w, wi_bf16, qnw, knw, cos, sin)


def _attn_kernel(shift_ref, q_ref, klo_ref, khi_ref, vlo_ref, vhi_ref, cklo_ref, ckhi_ref, cvlo_ref, cvhi_ref,
                 o_ref, kl_s, kh_s, va_s, *, l, lc):
    lk = l + lc

    @pl.when(pl.program_id(2) == 0)
    def _():
        kl_s[:, 0:l] = klo_ref[0, 0]
        kl_s[:, l:lk] = cklo_ref[0, 0]
        kh_s[:, 0:l] = khi_ref[0, 0]
        kh_s[:, l:lk] = ckhi_ref[0, 0]
        lane = lax.broadcasted_iota(I32, (lk, LANES), 1)
        ones_lo = jnp.where(lane < HEAD_DIM, 1.0, 0.0).astype(BF16)
        ones_hi = jnp.where(lane < HEAD_DIM, 0.0, 1.0).astype(BF16)
        for g, (v_ref, cv_ref, ones) in enumerate(((vlo_ref, cvlo_ref, ones_lo), (vhi_ref, cvhi_ref, ones_hi),
                                                   (vlo_ref, cvlo_ref, ones_lo), (vhi_ref, cvhi_ref, ones_hi))):
            v_col, one_col = (0, LANES) if g < 2 else (LANES, 0)
            va_s[g, 0:l, v_col:v_col + LANES] = v_ref[0, 0]
            va_s[g, l:lk, v_col:v_col + LANES] = cv_ref[0, 0]
            va_s[g, :, one_col:one_col + LANES] = ones

    def attend(streaming):
        q = q_ref[0]
        acc = []
        for g in range(GQA):
            qp = q[:, (g // 2) * LANES:(g // 2 + 1) * LANES]
            k_s = kl_s if g % 2 == 0 else kh_s
            if streaming:
                shift = shift_ref[0]
                o = None
                for c in range(0, lk, KEY_TILE):
                    hi = min(c + KEY_TILE, lk)
                    p = jnp.exp2(_dot(qp, k_s[:, c:hi]) - shift).astype(BF16)
                    t = _dot(p, va_s[g, c:hi])
                    o = t if o is None else o + t
            else:
                s = _dot(qp, k_s[...])
                p = jnp.exp2(s - jnp.max(s, axis=-1, keepdims=True)).astype(BF16)
                o = _dot(p, va_s[g])
            acc.append(o)
        out_a = acc[0] + acc[1]
        out_b = acc[2] + acc[3]
        o_ref[0, :, 0:LANES] = (out_a[:, 0:LANES] / out_a[:, LANES:2 * LANES]).astype(BF16)
        o_ref[0, :, LANES:2 * LANES] = (out_b[:, LANES:2 * LANES] / out_b[:, 0:LANES]).astype(BF16)

    can_stream = shift_ref[0] <= MAX_STREAM_SHIFT
    pl.when(can_stream)(functools.partial(attend, True))
    pl.when(jnp.logical_not(can_stream))(functools.partial(attend, False))


def _attention(shift, q, klo, khi, vlo, vhi, cklo, ckhi, cvlo, cvhi, *, tq):
    b, l, _ = q.shape
    lc = cvlo.shape[2]
    lk = l + lc
    tq = min(tq, l)
    gw = GQA * HEAD_DIM
    kt_spec = pl.BlockSpec((1, 1, LANES, l), lambda bi, h, i: (bi, h, 0, 0))
    kv_spec = pl.BlockSpec((1, 1, l, LANES), lambda bi, h, i: (bi, h, 0, 0))
    ckt_spec = pl.BlockSpec((1, 1, LANES, lc), lambda bi, h, i: (bi, h, 0, 0))
    ckv_spec = pl.BlockSpec((1, 1, lc, LANES), lambda bi, h, i: (bi, h, 0, 0))
    return pl.pallas_call(
        functools.partial(_attn_kernel, l=l, lc=lc),
        grid=(b, ATTN_KV_HEADS, l // tq),
        in_specs=([pl.BlockSpec(memory_space=pltpu.SMEM),
                   pl.BlockSpec((1, tq, gw), lambda bi, h, i: (bi, i, h))] + [kt_spec] * 2 + [kv_spec] * 2
                  + [ckt_spec] * 2 + [ckv_spec] * 2),
        out_specs=pl.BlockSpec((1, tq, gw), lambda bi, h, i: (bi, i, h)),
        out_shape=jax.ShapeDtypeStruct((b, l, ATTN_W), BF16),
        scratch_shapes=[pltpu.VMEM((LANES, lk), BF16), pltpu.VMEM((LANES, lk), BF16),
                        pltpu.VMEM((GQA, lk, 2 * LANES), BF16)],
        compiler_params=_cparams("arbitrary", "arbitrary", "arbitrary"),
        name="attn",
    )(shift, q, klo, khi, vlo, vhi, cklo, ckhi, cvlo, cvhi)


def _log_sigmoid(v):
    return jnp.minimum(v, 0.0) - jnp.log(1.0 + jnp.exp(-jnp.abs(v)))


def _ret_kernel(rq_ref, rk_ref, rv_ref, sg_ref, crk_ref, crv_ref, df_ref, db_ref, o_ref,
                m_s, xi_s, zeta_s, kv_s, st_s, *, l, lc):
    n = l // CHUNK
    nc = lc // CHUNK
    lgf = _log_sigmoid(df_ref[...])
    lgb = _log_sigmoid(db_ref[...])
    pos = lax.broadcasted_iota(I32, (CHUNK, LANES), 0).astype(F32)
    row = lax.broadcasted_iota(I32, (CHUNK, CHUNK), 0)
    col = lax.broadcasted_iota(I32, (CHUNK, CHUNK), 1)
    diff = (row - col).astype(F32)
    g_chunk = []
    for p in range(RET_PAIRS):
        cols = slice(p * LANES, (p + 1) * LANES)
        lf, lb = lgf[:, cols], lgb[:, cols]
        xi_s[p, :, 0:LANES] = jnp.exp((pos + 1.0) * lf)
        xi_s[p, :, LANES:] = jnp.exp((CHUNK - pos) * lb)
        zeta_s[p, :, 0:LANES] = jnp.exp((CHUNK - 1.0 - pos) * lf)
        zeta_s[p, :, LANES:] = jnp.exp(pos * lb)
        g_chunk.append((jnp.exp(CHUNK * lf), jnp.exp(CHUNK * lb)))
        for j in range(2):
            h = 2 * p + j
            hf = lgf[:, h * HEAD_DIM:h * HEAD_DIM + 1]
            hb = lgb[:, h * HEAD_DIM:h * HEAD_DIM + 1]
            m_s[p, :, j * CHUNK:(j + 1) * CHUNK] = jnp.where(
                diff > 0, jnp.exp(diff * hf), jnp.where(diff < 0, jnp.exp(-diff * hb), 2.0))

    lane = lax.broadcasted_iota(I32, (CHUNK, LANES), 1)
    low_half = lane < HEAD_DIM
    diag = (lax.broadcasted_iota(I32, (LANES, LANES), 0) // HEAD_DIM
            == lax.broadcasted_iota(I32, (LANES, LANES), 1) // HEAD_DIM)
    seg = jnp.where(diag, 1.0, 0.0).astype(BF16)
    seg2 = jnp.concatenate([seg, seg], axis=0)
    diag2 = jnp.concatenate([diag, diag], axis=0)

    def split_heads(a):
        zero = jnp.zeros_like(a)
        return jnp.concatenate([jnp.where(low_half, a, zero), jnp.where(low_half, zero, a)], axis=0)

    def contrib(k_ref, v_ref, r0, p):
        cols = slice(p * LANES, (p + 1) * LANES)
        kp = k_ref[0, pl.ds(r0, CHUNK), cols].astype(F32)
        kz = jnp.concatenate([kp, kp], axis=1) * zeta_s[p]
        kv = _dot(jnp.transpose(kz).astype(BF16), v_ref[0, pl.ds(r0, CHUNK), cols])
        return jnp.where(diag2, kv, 0.0)

    for c in range(nc):
        for p in range(RET_PAIRS):
            kv_s[c, p] = contrib(crk_ref, crv_ref, c * CHUNK, p)

    def contrib_body(c, carry):
        r0 = pl.multiple_of(c * CHUNK, CHUNK)
        for p in range(RET_PAIRS):
            kv_s[nc + c, p] = contrib(rk_ref, rv_ref, r0, p)
        return carry

    lax.fori_loop(0, n, contrib_body, 0, unroll=4)

    for p in range(RET_PAIRS):
        gf, gb = g_chunk[p]
        sf = jnp.zeros((LANES, LANES), F32)
        sb = jnp.zeros((LANES, LANES), F32)
        for c in range(nc):
            sf = gf * sf + kv_s[c, p, 0:LANES]
            sb = gb * sb + kv_s[nc - 1 - c, p, LANES:]

        def fwd_scan(c, s, p=p, gf=gf):
            st_s[c, p, 0:LANES] = s.astype(BF16)
            return gf * s + kv_s[nc + c, p, 0:LANES]

        def bwd_scan(j, s, p=p, gb=gb):
            c = n - 1 - j
            st_s[c, p, LANES:] = s.astype(BF16)
            return gb * s + kv_s[nc + c, p, LANES:]

        lax.fori_loop(0, n, fwd_scan, sf)
        lax.fori_loop(0, n, bwd_scan, sb)

    def out_body(c, carry):
        r0 = pl.multiple_of(c * CHUNK, CHUNK)
        for p in range(RET_PAIRS):
            cols = slice(p * LANES, (p + 1) * LANES)
            qp = rq_ref[0, pl.ds(r0, CHUNK), cols]
            kp = rk_ref[0, pl.ds(r0, CHUNK), cols]
            vp = rv_ref[0, pl.ds(r0, CHUNK), cols]
            s2 = _dot_nt(qp, split_heads(kp))
            a2 = (s2 * m_s[p]).astype(BF16)
            y = _dot(a2, split_heads(vp))
            qf = qp.astype(F32)
            qx = (jnp.concatenate([qf, qf], axis=1) * xi_s[p]).astype(BF16)
            y += _dot(qx, st_s[c, p])
            hi, lo = _split(y * y)
            ms = _dot(jnp.concatenate([hi, lo], axis=1), seg2) * (1.0 / HEAD_DIM)
            out = y * lax.rsqrt(ms + EPS) * sg_ref[0, pl.ds(r0, CHUNK), cols].astype(F32)
            o_ref[0, pl.ds(r0, CHUNK), cols] = out.astype(BF16)
        return carry

    lax.fori_loop(0, n, out_body, 0, unroll=4)


def _retention(rq, rk, rv, sg, crk, crv, dec_f, dec_b):
    b, l, _ = rq.shape
    lc = crk.shape[1]
    n = l // CHUNK
    nc = lc // CHUNK
    spec = pl.BlockSpec((1, l, RET_W), lambda bi: (bi, 0, 0))
    cspec = pl.BlockSpec((1, lc, RET_W), lambda bi: (bi, 0, 0))
    dspec = pl.BlockSpec((1, RET_W), lambda bi: (0, 0))
    return pl.pallas_call(
        functools.partial(_ret_kernel, l=l, lc=lc),
        grid=(b,),
        in_specs=[spec, spec, spec, spec, cspec, cspec, dspec, dspec],
        out_specs=spec,
        out_shape=jax.ShapeDtypeStruct((b, l, RET_W), BF16),
        scratch_shapes=[pltpu.VMEM((RET_PAIRS, CHUNK, 2 * CHUNK), F32),
                        pltpu.VMEM((RET_PAIRS, CHUNK, 2 * LANES), F32),
                        pltpu.VMEM((RET_PAIRS, CHUNK, 2 * LANES), F32),
                        pltpu.VMEM((nc + n, RET_PAIRS, 2 * LANES, LANES), F32),
                        pltpu.VMEM((n, RET_PAIRS, 2 * LANES, LANES), BF16)],
        compiler_params=_cparams("arbitrary"),
        name="ret",
    )(rq, rk, rv, sg, crk, crv, dec_f, dec_b)


def _out_kernel(attn_ref, ret_ref, x_ref, wa_ref, wr_ref, g1_ref, sh_ref, sc_ref, nw_ref, rhi_ref, rlo_ref,
                x1_ref, hp_ref, lg_ref):
    y = _dot(attn_ref[0], wa_ref[...]) + _dot(ret_ref[0], wr_ref[...])
    x1 = x_ref[0] + g1_ref[0] * y
    x1_ref[0] = x1
    h = x1 * lax.rsqrt(jnp.mean(x1 * x1, axis=-1, keepdims=True) + EPS) * nw_ref[...]
    h = h * (1.0 + sc_ref[0]) + sh_ref[0]
    half = h.shape[1] // 2
    hp_ref[...] = _pack_halves(h[:, :half], h[:, half:])
    h_hi, h_lo = _split(h)
    lg_ref[...] = _dot_nt(rhi_ref[...], h_hi) + _dot_nt(rhi_ref[...], h_lo) + _dot_nt(rlo_ref[...], h_hi)


def _out_projection(attn, ret, x, wa, wr, g1, sh2, sc2, norm_w, r_hi, r_lo, *, tm):
    b, l, d = x.shape
    tm = min(tm, l)
    nt = l // tm
    t = b * l
    mspec = pl.BlockSpec((1, 1, d), lambda bi, i: (bi, 0, 0))
    return pl.pallas_call(
        _out_kernel,
        grid=(b, nt),
        in_specs=[pl.BlockSpec((1, tm, ATTN_W), lambda bi, i: (bi, i, 0)),
                  pl.BlockSpec((1, tm, RET_W), lambda bi, i: (bi, i, 0)),
                  pl.BlockSpec((1, tm, d), lambda bi, i: (bi, i, 0)),
                  pl.BlockSpec((ATTN_W, d), lambda bi, i: (0, 0)),
                  pl.BlockSpec((RET_W, d), lambda bi, i: (0, 0)),
                  mspec, mspec, mspec,
                  pl.BlockSpec((1, d), lambda bi, i: (0, 0)),
                  pl.BlockSpec((N_EXPERTS, d), lambda bi, i: (0, 0)),
                  pl.BlockSpec((N_EXPERTS, d), lambda bi, i: (0, 0))],
        out_specs=[pl.BlockSpec((1, tm, d), lambda bi, i: (bi, i, 0)),
                   pl.BlockSpec((tm, d // 2), lambda bi, i: (bi * nt + i, 0)),
                   pl.BlockSpec((N_EXPERTS, tm), lambda bi, i: (0, bi * nt + i))],
        out_shape=[jax.ShapeDtypeStruct((b, l, d), F32),
                   jax.ShapeDtypeStruct((t, d // 2), PACKED),
                   jax.ShapeDtypeStruct((N_EXPERTS, t), F32)],
        compiler_params=_cparams("arbitrary", "arbitrary"),
        name="out_proj",
    )(attn, ret, x, wa, wr, g1, sh2, sc2, norm_w, r_hi, r_lo)


def _route_kernel(lg_ref, bias_ref, idx_ref, w_ref, rank_ref, cnt_col_ref, cnt_row_ref, tri_s, col_s, row_s):
    tb = lg_ref.shape[1]
    step = pl.program_id(0)

    @pl.when(step == 0)
    def _():
        r = lax.broadcasted_iota(I32, (tb, tb), 0)
        c = lax.broadcasted_iota(I32, (tb, tb), 1)
        tri_s[...] = jnp.where(r <= c, 1.0, 0.0).astype(BF16)
        col_s[...] = jnp.zeros_like(col_s)
        row_s[...] = jnp.zeros_like(row_s)

    scores = _sigmoid(lg_ref[...])
    biased = scores + bias_ref[...]
    neg = -jnp.inf
    sub = lax.broadcasted_iota(I32, (GROUP_SIZE, tb), 0).astype(F32)

    gscore = []
    for g in range(N_GROUPS):
        blk = biased[g * GROUP_SIZE:(g + 1) * GROUP_SIZE]
        m1 = jnp.max(blk, axis=0, keepdims=True)
        first = jnp.min(jnp.where(blk == m1, sub, float(GROUP_SIZE)), axis=0, keepdims=True)
        m2 = jnp.max(jnp.where(sub == first, neg, blk), axis=0, keepdims=True)
        gscore.append(m1 + m2)
    gs = jnp.concatenate(gscore, axis=0)
    gsub = lax.broadcasted_iota(I32, (N_GROUPS, tb), 0).astype(F32)
    keep = jnp.zeros((N_GROUPS, tb), F32)
    for _ in range(TOPK_GROUPS):
        m = jnp.max(gs, axis=0, keepdims=True)
        first = jnp.min(jnp.where(gs == m, gsub, float(N_GROUPS)), axis=0, keepdims=True)
        sel = gsub == first
        keep = jnp.where(sel, 1.0, keep)
        gs = jnp.where(sel, neg, gs)
    masked = jnp.concatenate(
        [jnp.where(keep[g:g + 1] > 0.0, biased[g * GROUP_SIZE:(g + 1) * GROUP_SIZE], neg)
         for g in range(N_GROUPS)], axis=0)

    esub = lax.broadcasted_iota(I32, (N_EXPERTS, tb), 0).astype(F32)
    sels, idxs, ws = [], [], []
    chosen = jnp.zeros((N_EXPERTS, tb), F32)
    for _ in range(TOP_K):
        m = jnp.max(masked, axis=0, keepdims=True)
        first = jnp.min(jnp.where(masked == m, esub, float(N_EXPERTS)), axis=0, keepdims=True)
        sel = esub == first
        sels.append(sel)
        idxs.append(first)
        ws.append(jnp.sum(jnp.where(sel, scores, 0.0), axis=0, keepdims=True))
        chosen = jnp.where(sel, 1.0, chosen)
        masked = jnp.where(sel, neg, masked)
    wsum = ws[0]
    for k in range(1, TOP_K):
        wsum = wsum + ws[k]
    idx_ref[...] = jnp.concatenate(idxs, axis=0).astype(I32)
    w_ref[...] = jnp.concatenate([wk / wsum * ROUTED_SCALE for wk in ws], axis=0)

    chosen_b = chosen.astype(BF16)
    incl = _dot(chosen_b, tri_s[...])
    before = incl - chosen + col_s[...]
    rank_ref[...] = jnp.concatenate(
        [jnp.sum(jnp.where(sel, before, 0.0), axis=0, keepdims=True) for sel in sels], axis=0).astype(I32)
    col_s[...] = col_s[...] + incl[:, tb - 1:tb]
    row_s[...] = row_s[...] + _dot_nt(jnp.ones((8, tb), BF16), chosen_b)
    cnt_col_ref[...] = col_s[...].astype(I32)
    cnt_row_ref[...] = row_s[...].astype(I32)


def _route(logits_t, bias_col, *, tb):
    e, t = logits_t.shape
    tb = min(tb, t)
    kspec = pl.BlockSpec((TOP_K, tb), lambda i: (0, i))
    return pl.pallas_call(
        _route_kernel,
        grid=(t // tb,),
        in_specs=[pl.BlockSpec((e, tb), lambda i: (0, i)),
                  pl.BlockSpec((e, 1), lambda i: (0, 0))],
        out_specs=[kspec, kspec, kspec,
                   pl.BlockSpec((e, 1), lambda i: (0, 0)),
                   pl.BlockSpec((8, e), lambda i: (0, 0))],
        out_shape=[jax.ShapeDtypeStruct((TOP_K, t), I32),
                   jax.ShapeDtypeStruct((TOP_K, t), F32),
                   jax.ShapeDtypeStruct((TOP_K, t), I32),
                   jax.ShapeDtypeStruct((e, 1), I32),
                   jax.ShapeDtypeStruct((8, e), I32)],
        scratch_shapes=[pltpu.VMEM((tb, tb), BF16), pltpu.VMEM((e, 1), F32), pltpu.VMEM((8, e), F32)],
        compiler_params=_cparams("arbitrary"),
        name="route",
    )(logits_t, bias_col)


def _pad_block(cnt):
    return (cnt + (MOE_BLOCK - 1)) // MOE_BLOCK * MOE_BLOCK


def _max_items(n_blocks):
    return n_blocks // ITEM_BLOCKS + N_EXPERTS


def _dest_kernel(idx_ref, rank_ref, cnt_col_ref, cnt_row_ref, dest_ref, meta_ref, items_ref, start_s):
    tb = idx_ref.shape[1]
    nip = items_ref.shape[1]

    @pl.when(pl.program_id(0) == 0)
    def _():
        pad_col = _pad_block(cnt_col_ref[...])
        pad_row = _pad_block(cnt_row_ref[0:1, :])
        er = lax.broadcasted_iota(I32, (N_EXPERTS, N_EXPERTS), 0)
        ec = lax.broadcasted_iota(I32, (N_EXPERTS, N_EXPERTS), 1)
        start_col = jnp.sum(jnp.where(ec < er, pad_row, 0), axis=1, keepdims=True)
        start_row = jnp.sum(jnp.where(er < ec, pad_col, 0), axis=0, keepdims=True)
        start_s[...] = start_col

        used = jnp.sum(pad_row, axis=1, keepdims=True) // MOE_BLOCK
        meta_ref[...] = jnp.concatenate(
            [cnt_row_ref[0:1, :], start_row, pad_row, jnp.broadcast_to(used, (1, N_EXPERTS)),
             jnp.zeros((4, N_EXPERTS), I32)], axis=0)

        nb_col = pad_col // MOE_BLOCK
        it_col = (nb_col + (ITEM_BLOCKS - 1)) // ITEM_BLOCKS
        it_row = (pad_row // MOE_BLOCK + (ITEM_BLOCKS - 1)) // ITEM_BLOCKS
        it_start = jnp.sum(jnp.where(ec < er, it_row, 0), axis=1, keepdims=True)
        n_items = jnp.sum(it_row, axis=1, keepdims=True)
        lane = lax.broadcasted_iota(I32, (1, nip), 1)
        owner = jnp.sum(jnp.where(it_start + it_col <= lane, 1, 0), axis=0, keepdims=True)
        owner = jnp.minimum(owner, N_EXPERTS - 1)
        onehot = lax.broadcasted_iota(I32, (N_EXPERTS, nip), 0) == owner

        def pick(col):
            return jnp.sum(jnp.where(onehot, col, 0), axis=0, keepdims=True)

        j = lane - pick(it_start)
        block0 = pick(start_col) // MOE_BLOCK + ITEM_BLOCKS * j
        nvalid = jnp.clip(pick(nb_col) - ITEM_BLOCKS * j, 0, ITEM_BLOCKS)
        items_ref[...] = jnp.concatenate(
            [owner, block0, jnp.where(lane < n_items, nvalid, 0), jnp.broadcast_to(n_items, (1, nip)),
             jnp.zeros((4, nip), I32)], axis=0)

    start_col = start_s[...]
    esub = lax.broadcasted_iota(I32, (N_EXPERTS, tb), 0)
    rows = []
    for k in range(TOP_K):
        onehot = esub == idx_ref[k:k + 1, :]
        rows.append(jnp.sum(jnp.where(onehot, start_col, 0), axis=0, keepdims=True) + rank_ref[k:k + 1, :])
    dest_ref[0] = jnp.concatenate(rows, axis=0)


def _destinations(idx_t, rank_t, cnt_col, cnt_row, *, tb, n_blocks):
    _, t = idx_t.shape
    tb = min(tb, t)
    nip = (_max_items(n_blocks) + LANES - 1) // LANES * LANES
    kspec = pl.BlockSpec((TOP_K, tb), lambda i: (0, i))
    return pl.pallas_call(
        _dest_kernel,
        grid=(t // tb,),
        in_specs=[kspec, kspec,
                  pl.BlockSpec((N_EXPERTS, 1), lambda i: (0, 0)),
                  pl.BlockSpec((8, N_EXPERTS), lambda i: (0, 0))],
        out_specs=[pl.BlockSpec((1, TOP_K, tb), lambda i: (i, 0, 0)),
                   pl.BlockSpec((8, N_EXPERTS), lambda i: (0, 0)),
                   pl.BlockSpec((8, nip), lambda i: (0, 0))],
        out_shape=[jax.ShapeDtypeStruct((t // tb, TOP_K, tb), I32),
                   jax.ShapeDtypeStruct((8, N_EXPERTS), I32),
                   jax.ShapeDtypeStruct((8, nip), I32)],
        scratch_shapes=[pltpu.VMEM((N_EXPERTS, 1), I32)],
        compiler_params=_cparams("arbitrary"),
        name="dest",
    )(idx_t, rank_t, cnt_col, cnt_row)


_PAD_BITS = (64, 32, 16, 8)


def _sc_scatter_rows(rows, dest_win, n_out_rows):
    n_win, n_slots, win = dest_win.shape
    width = rows.shape[1]
    info = plsc.get_sparse_core_info()
    n_workers = info.num_cores * info.num_subcores
    per_worker = n_win // n_workers
    assert per_worker * n_workers == n_win and win <= LANES
    mesh = plsc.VectorSubcoreMesh(core_axis_name="c", subcore_axis_name="s")

    def body(rows_hbm, dest_hbm, out_hbm, idx_v, rows_v, sem):
        wid = lax.axis_index("s") * info.num_cores + lax.axis_index("c")

        @pl.loop(0, per_worker)
        def _(j):
            w = wid * per_worker + j
            pltpu.sync_copy(dest_hbm.at[w], idx_v)
            pltpu.sync_copy(rows_hbm.at[pl.ds(w * win, win)], rows_v)
            copies = [pltpu.async_copy(rows_v, out_hbm.at[idx_v.at[k]], sem) for k in range(n_slots)]
            for cp in copies:
                cp.wait()

    return pl.kernel(
        body,
        out_type=jax.ShapeDtypeStruct((n_out_rows, width), rows.dtype),
        mesh=mesh,
        scratch_types=[pltpu.VMEM((n_slots, win), I32), pltpu.VMEM((win, width), rows.dtype),
                       pltpu.SemaphoreType.DMA],
        name="sc_scatter",
    )(rows, dest_win)


def _pad_fill_kernel(meta_ref, xs_in, xs_hbm, zero_s, sem_z, *, e_per_step):
    del xs_in
    step = pl.program_id(0)
    zero_s[...] = jnp.zeros_like(zero_s)

    def tail_copy(c):
        row0 = pl.multiple_of((meta_ref[3, 0] + c) * MOE_BLOCK, MOE_BLOCK)
        return pltpu.make_async_copy(zero_s, xs_hbm.at[pl.ds(row0, MOE_BLOCK)], sem_z)

    @pl.when(step == 0)
    def _():
        for c in range(ITEM_BLOCKS - 1):
            tail_copy(c).start()
        for c in range(ITEM_BLOCKS - 1):
            tail_copy(c).wait()

    def pad_copies(e):
        cnt = meta_ref[0, e]
        off = meta_ref[1, e] + cnt
        rem = meta_ref[2, e] - cnt
        head = rem & (SUBLANES - 1)
        out = []
        for i in range(SUBLANES - 1):
            out.append((i < head,
                        pltpu.make_async_copy(zero_s.at[pl.ds(0, 1)], xs_hbm.at[pl.ds(off + i, 1)], sem_z)))
        off = off + head
        for bit in _PAD_BITS:
            out.append(((rem & bit) != 0,
                        pltpu.make_async_copy(zero_s.at[pl.ds(0, bit)],
                                              xs_hbm.at[pl.ds(pl.multiple_of(off, SUBLANES), bit)], sem_z)))
            off = off + (rem & bit)
        return out

    for j in range(e_per_step):
        for cond, c in pad_copies(step * e_per_step + j):
            pl.when(cond)(c.start)
    for j in range(e_per_step):
        for cond, c in pad_copies(step * e_per_step + j):
            pl.when(cond)(c.wait)


PAD_FILL_EXPERTS_PER_STEP = 8


def _pad_fill(meta, xs):
    half = xs.shape[1]
    return pl.pallas_call(
        functools.partial(_pad_fill_kernel, e_per_step=PAD_FILL_EXPERTS_PER_STEP),
        grid=(N_EXPERTS // PAD_FILL_EXPERTS_PER_STEP,),
        in_specs=[pl.BlockSpec(memory_space=pltpu.SMEM),
                  pl.BlockSpec(memory_space=pl.ANY)],
        out_specs=pl.BlockSpec(memory_space=pl.ANY),
        out_shape=jax.ShapeDtypeStruct(xs.shape, xs.dtype),
        input_output_aliases={1: 0},
        scratch_shapes=[pltpu.VMEM((MOE_BLOCK, half), PACKED), pltpu.SemaphoreType.DMA],
        compiler_params=_cparams("arbitrary"),
        name="pad_fill",
    )(meta, xs)


def _experts_kernel(items_ref, xs_hbm, wg_hbm, wu_hbm, wd_hbm, ys_hbm,
                    xbuf, ybuf, wg_f, wu_f, wd_f, wg_s, wu_s, wd_s, sem_x, sem_y, sem_w):
    n_items = items_ref[3, 0]
    rows = ITEM_BLOCKS * MOE_BLOCK

    def w_copies(e, s):
        return [pltpu.make_async_copy(src.at[e], dst.at[s], sem_w.at[s])
                for src, dst in ((wg_hbm, wg_f), (wu_hbm, wu_f), (wd_hbm, wd_f))]

    def x_copy(item, s):
        row0 = pl.multiple_of(items_ref[1, item] * MOE_BLOCK, MOE_BLOCK)
        return pltpu.make_async_copy(xs_hbm.at[pl.ds(row0, rows)], xbuf.at[s], sem_x.at[s])

    def y_copies(item, s, fn):
        for k in range(ITEM_BLOCKS):
            row0 = pl.multiple_of((items_ref[1, item] + k) * MOE_BLOCK, MOE_BLOCK)
            cp = pltpu.make_async_copy(ybuf.at[s, pl.ds(k * MOE_BLOCK, MOE_BLOCK)],
                                       ys_hbm.at[pl.ds(row0, MOE_BLOCK)], sem_y.at[s])
            pl.when(k < items_ref[2, item])(functools.partial(fn, cp))

    def expert_of(item):
        return items_ref[0, jnp.minimum(item, n_items - 1)]

    def changes_at(item):
        return ((item < n_items) & (expert_of(item) != expert_of(item - 1))).astype(I32)

    @pl.when(n_items > 0)
    def _():
        x_copy(0, 0).start()
        for cp in w_copies(expert_of(0), 0):
            cp.start()

        @pl.when(changes_at(1) == 1)
        def _():
            for cp in w_copies(expert_of(1), 1):
                cp.start()

    def item_body(i, ordinal):
        slot = i % 2
        prev = jnp.maximum(i - 1, 0)
        e = expert_of(i)
        new_expert = (i == 0) | (e != expert_of(prev))
        c1 = changes_at(i + 1)
        c2 = changes_at(i + 2)

        @pl.when(i + 1 < n_items)
        def _():
            x_copy(i + 1, 1 - slot).start()

        @pl.when(new_expert)
        def _():
            wslot = ordinal % WEIGHT_SLOTS
            for cp in w_copies(e, wslot):
                cp.wait()
            wg_s[...] = wg_f[wslot].astype(BF16)
            wu_s[...] = wu_f[wslot].astype(BF16)
            wd_s[...] = wd_f[wslot].astype(BF16)

        @pl.when(c2 == 1)
        def _():
            for cp in w_copies(expert_of(i + 2), (ordinal + c1 + 1) % WEIGHT_SLOTS):
                cp.start()

        x_copy(i, slot).wait()
        xa, xb = _unpack_halves(xbuf[slot])
        xa = xa.astype(BF16)
        xb = xb.astype(BF16)
        half = xa.shape[1]
        g = _dot(xa, wg_s[0:half]) + _dot(xb, wg_s[half:])
        u = _dot(xa, wu_s[0:half]) + _dot(xb, wu_s[half:])
        y = _dot((_silu(g) * u).astype(BF16), wd_s[...])
        ybuf[slot] = _pack_halves(y[:, :half], y[:, half:])
        y_copies(i, slot, lambda cp: cp.start())

        @pl.when(i > 0)
        def _():
            y_copies(prev, 1 - slot, lambda cp: cp.wait())

        return ordinal + c1

    lax.fori_loop(0, n_items, item_body, jnp.int32(0))

    @pl.when(n_items > 0)
    def _():
        last = n_items - 1
        y_copies(last, last % 2, lambda cp: cp.wait())


def _experts(items, xs, w_gate, w_up, w_down, *, n_blocks):
    half = xs.shape[1]
    e, d, f = w_gate.shape
    rows = ITEM_BLOCKS * MOE_BLOCK
    any_spec = pl.BlockSpec(memory_space=pl.ANY)
    return pl.pallas_call(
        _experts_kernel,
        grid_spec=pltpu.PrefetchScalarGridSpec(
            num_scalar_prefetch=1,
            grid=(1,),
            in_specs=[any_spec, any_spec, any_spec, any_spec],
            out_specs=any_spec,
            scratch_shapes=[pltpu.VMEM((2, rows, half), PACKED), pltpu.VMEM((2, rows, half), PACKED),
                            pltpu.VMEM((WEIGHT_SLOTS, d, f), F32), pltpu.VMEM((WEIGHT_SLOTS, d, f), F32),
                            pltpu.VMEM((WEIGHT_SLOTS, f, d), F32),
                            pltpu.VMEM((d, f), BF16), pltpu.VMEM((d, f), BF16), pltpu.VMEM((f, d), BF16),
                            pltpu.SemaphoreType.DMA((2,)), pltpu.SemaphoreType.DMA((2,)),
                            pltpu.SemaphoreType.DMA((WEIGHT_SLOTS,))]),
        out_shape=jax.ShapeDtypeStruct((n_blocks * MOE_BLOCK, half), PACKED),
        compiler_params=_cparams("arbitrary"),
        name="experts",
    )(items, xs, w_gate, w_up, w_down)


def _sc_gather_rows(table, idx):
    n_idx = idx.shape[0]
    width = table.shape[1]
    info = plsc.get_sparse_core_info()
    n_workers = info.num_cores * info.num_subcores
    per_worker = n_idx // n_workers
    assert per_worker * n_workers == n_idx and per_worker % (SC_GATHER_BUFS * SC_WINDOW) == 0
    mesh = plsc.VectorSubcoreMesh(core_axis_name="c", subcore_axis_name="s")

    def body(table_hbm, idx_hbm, out_hbm, idx_v, rows_v, sem_g, sem_o):
        wid = lax.axis_index("s") * info.num_cores + lax.axis_index("c")
        base = wid * per_worker

        @pl.loop(0, per_worker // (SC_GATHER_BUFS * SC_WINDOW))
        def _(it):
            offs = [base + (it * SC_GATHER_BUFS + b) * SC_WINDOW for b in range(SC_GATHER_BUFS)]
            gathers = []
            for b, off in enumerate(offs):
                pltpu.sync_copy(idx_hbm.at[pl.ds(off, SC_WINDOW)], idx_v.at[b])
                gathers.append(pltpu.async_copy(table_hbm.at[idx_v.at[b]], rows_v.at[b], sem_g.at[b]))
            writes = []
            for b, off in enumerate(offs):
                gathers[b].wait()
                writes.append(pltpu.async_copy(rows_v.at[b], out_hbm.at[pl.ds(off, SC_WINDOW)], sem_o.at[b]))
            for cp in writes:
                cp.wait()

    return pl.kernel(
        body,
        out_type=jax.ShapeDtypeStruct((n_idx, width), table.dtype),
        mesh=mesh,
        scratch_types=[pltpu.VMEM((SC_GATHER_BUFS, SC_WINDOW), I32),
                       pltpu.VMEM((SC_GATHER_BUFS, SC_WINDOW, width), table.dtype),
                       pltpu.SemaphoreType.DMA((SC_GATHER_BUFS,)), pltpu.SemaphoreType.DMA((SC_GATHER_BUFS,))],
        name="sc_gather",
    )(table, idx)


def _combine_kernel(hp_ref, x1_ref, g2_ref, w_ref, sgw_ref, suw_ref, sdw_ref, yg_ref, *rest):
    o_ref = rest[-1]
    xa, xb = _unpack_halves(hp_ref[...])
    xa = xa.astype(BF16)
    xb = xb.astype(BF16)
    half = xa.shape[1]
    tb = xa.shape[0]
    g = _dot(xa, sgw_ref[0:half]) + _dot(xb, sgw_ref[half:])
    u = _dot(xa, suw_ref[0:half]) + _dot(xb, suw_ref[half:])
    shared = _dot((_silu(g) * u).astype(BF16), sdw_ref[...])

    w = w_ref[...]
    acc_a = jnp.zeros((tb, half), F32)
    acc_b = jnp.zeros((tb, half), F32)
    for k in range(TOP_K):
        ya, yb = _unpack_halves(yg_ref[0, k])
        acc_a += ya * w[:, k:k + 1]
        acc_b += yb * w[:, k:k + 1]
    g2 = g2_ref[0]
    o_ref[:, 0:half] = x1_ref[:, 0:half] + g2[:, 0:half] * (acc_a + shared[:, 0:half])
    o_ref[:, half:] = x1_ref[:, half:] + g2[:, half:] * (acc_b + shared[:, half:])


def _combine(hp, x1, g2, w_tok, sgw, suw, sdw, yg, out_prev, *, tb, seq_len, first_step):
    t, half = hp.shape
    d = 2 * half
    per_seq = seq_len // tb
    f = sgw.shape[1]
    s0 = first_step
    in_specs = [pl.BlockSpec((tb, half), lambda i: (s0 + i, 0)),
                pl.BlockSpec((tb, d), lambda i: (s0 + i, 0)),
                pl.BlockSpec((1, 1, d), lambda i: ((s0 + i) // per_seq, 0, 0)),
                pl.BlockSpec((tb, TOP_K), lambda i: (s0 + i, 0)),
                pl.BlockSpec((d, f), lambda i: (0, 0)),
                pl.BlockSpec((d, f), lambda i: (0, 0)),
                pl.BlockSpec((f, d), lambda i: (0, 0)),
                pl.BlockSpec((1, TOP_K, tb, half), lambda i: (i, 0, 0, 0))]
    args = [hp, x1, g2, w_tok, sgw, suw, sdw, yg]
    aliases = {}
    if out_prev is not None:
        in_specs.append(pl.BlockSpec(memory_space=pl.ANY))
        args.append(out_prev)
        aliases = {len(args) - 1: 0}
    return pl.pallas_call(
        _combine_kernel,
        grid=(yg.shape[0],),
        in_specs=in_specs,
        out_specs=pl.BlockSpec((tb, d), lambda i: (s0 + i, 0)),
        out_shape=jax.ShapeDtypeStruct((t, d), F32),
        input_output_aliases=aliases,
        compiler_params=_cparams("arbitrary"),
        name="combine",
    )(*args)


def _rope_tables(l):
    rows = l // GRID_W
    r = jnp.repeat(jnp.arange(rows), GRID_W).astype(F32)
    col = jnp.tile(jnp.arange(GRID_W), rows).astype(F32)
    n_f = HEAD_DIM // 4
    freqs = ROPE_THETA ** (-jnp.arange(n_f, dtype=F32) / n_f)
    ang = jnp.concatenate([r[:, None] * freqs, col[:, None] * freqs], axis=-1)
    ang = jnp.tile(jnp.repeat(ang, 2, axis=1), (1, LANES // HEAD_DIM))
    sign = jnp.where(jnp.arange(LANES) % 2 == 0, -1.0, 1.0).astype(F32)
    return jnp.cos(ang), jnp.sin(ang) * sign


def kernel(x, c, ctx, c_ctx, w_mod, b_mod, norm1_w, norm2_w, w_in, q_norm_w, k_norm_w, ret_decay_fwd,
           ret_decay_bwd, w_out, router_w, router_bias, exp_w_gate, exp_w_up, exp_w_down, sh_w_gate,
           sh_w_up, sh_w_down):
    b, l, d = x.shape
    lc = ctx.shape[1]
    t = b * l
    assert w_mod.shape[0] == 1, "single layer"
    assert l % CHUNK == 0 and lc % CHUNK == 0 and l % GRID_W == 0

    rows = (b + 1 + 7) // 8 * 8
    cc = jnp.zeros((rows, d), F32).at[:b].set(c).at[b].set(c_ctx)
    mod = _modulation(cc, w_mod[0], b_mod[0])
    sh1, sc1, g1, sh2, sc2, g2 = [mod[:b, i * d:(i + 1) * d].reshape(b, 1, d) for i in range(6)]
    shc = mod[b, 0:d].reshape(1, 1, d)
    scc = mod[b, d:2 * d].reshape(1, 1, d)

    wi = w_in[0].astype(BF16)
    qnw = jnp.tile(q_norm_w[0], LANES // HEAD_DIM).reshape(1, LANES)
    knw = jnp.tile(k_norm_w[0], LANES // HEAD_DIM).reshape(1, LANES)
    cos, sin = _rope_tables(l)
    n1 = norm1_w[0].reshape(1, d)

    cklo, ckhi, cvlo, cvhi, crk, crv = _projection(
        ctx, shc, scc, n1, wi, qnw, knw, cos[:lc], sin[:lc], rope=False, with_q=False, tm=TILE_PROJ)
    klo, khi, vlo, vhi, rk, rv, q, rq, sg = _projection(
        x, sh1, sc1, n1, wi, qnw, knw, cos, sin, rope=True, with_q=True, tm=TILE_PROJ)

    bound = (HEAD_DIM * QK_SCALE * LOG2_E * BOUND_MARGIN
             * jnp.max(jnp.abs(q_norm_w[0])) * jnp.max(jnp.abs(k_norm_w[0]))).astype(F32)
    attn = _attention(bound.reshape(1), q, klo, khi, vlo, vhi, cklo, ckhi, cvlo, cvhi, tq=TILE_ATTN_Q)
    dec_f = jnp.repeat(ret_decay_fwd[0].astype(F32), HEAD_DIM).reshape(1, RET_W)
    dec_b = jnp.repeat(ret_decay_bwd[0].astype(F32), HEAD_DIM).reshape(1, RET_W)
    ret = _retention(rq, rk, rv, sg, crk, crv, dec_f, dec_b)

    wo = w_out[0].astype(BF16)
    r_hi, r_lo = _split(router_w[0].T)
    x1, hp, logits_t = _out_projection(attn, ret, x, wo[:ATTN_W], wo[ATTN_W:], g1, sh2, sc2,
                                       norm2_w[0].reshape(1, d), r_hi, r_lo, tm=TILE_OUT)

    idx_t, w_t, rank_t, cnt_col, cnt_row = _route(logits_t, router_bias[0].reshape(N_EXPERTS, 1), tb=TILE_TOKENS)
    n_blocks = -(-(t * TOP_K) // MOE_BLOCK) + N_EXPERTS
    tb = TILE_TOKENS
    dest, meta, items = _destinations(idx_t, rank_t, cnt_col, cnt_row, tb=tb, n_blocks=n_blocks)
    steps, _, tbe = dest.shape
    dest_win = dest.reshape(steps, TOP_K, tbe // SC_WINDOW, SC_WINDOW).transpose(0, 2, 1, 3)
    dest_win = dest_win.reshape(t // SC_WINDOW, TOP_K, SC_WINDOW)
    xs = _sc_scatter_rows(hp, dest_win, (n_blocks + ITEM_BLOCKS - 1) * MOE_BLOCK)
    xs = _pad_fill(meta, xs)
    ys = _experts(items, xs, exp_w_gate[0], exp_w_up[0], exp_w_down[0], n_blocks=n_blocks)
    parts = COMBINE_PARTS if steps % COMBINE_PARTS == 0 else 1
    steps_part = steps // parts
    x1f, w_tok = x1.reshape(t, d), w_t.T
    sgw, suw, sdw = sh_w_gate[0].astype(BF16), sh_w_up[0].astype(BF16), sh_w_down[0].astype(BF16)
    out = None
    for p in range(parts):
        idx = dest[p * steps_part:(p + 1) * steps_part].reshape(-1)
        yg = _sc_gather_rows(ys, idx).reshape(steps_part, TOP_K, tbe, d // 2)
        out = _combine(hp, x1f, g2, w_tok, sgw, suw, sdw, yg, out, tb=tbe, seq_len=l, first_step=p * steps_part)
    return out.reshape(b, l, d)
```

```python
import functools

import jax
import jax.numpy as jnp
from jax import lax
from jax.experimental import pallas as pl
from jax.experimental.pallas import tpu as pltpu
from jax.experimental.pallas import tpu_sc as plsc

F32 = jnp.float32
BF16 = jnp.bfloat16
I32 = jnp.int32
U32 = jnp.uint32
PACKED = jnp.int32

HEAD_DIM = 64
LANES = 128
SUBLANES = 8
ATTN_HEADS = 8
ATTN_KV_HEADS = 2
GQA = ATTN_HEADS // ATTN_KV_HEADS
RET_HEADS = 8
ATTN_W = ATTN_HEADS * HEAD_DIM
KV_W = ATTN_KV_HEADS * HEAD_DIM
RET_W = RET_HEADS * HEAD_DIM
RET_PAIRS = RET_W // LANES
CHUNK = 128
GRID_W = 64
ROPE_THETA = 10000.0
N_EXPERTS = 256
TOP_K = 8
N_GROUPS = 8
GROUP_SIZE = N_EXPERTS // N_GROUPS
TOPK_GROUPS = 4
ROUTED_SCALE = 2.5
MOE_BLOCK = 128
EPS = 1e-6
QK_SCALE = HEAD_DIM ** -0.5
LOG2_E = 1.4426950408889634
KEY_TILE = 256
BOUND_MARGIN = 1.02
MAX_STREAM_SHIFT = 56.0

OFF_AK = 0
OFF_AV = OFF_AK + KV_W
OFF_RK = OFF_AV + KV_W
OFF_RV = OFF_RK + RET_W
CTX_KV_COLS = OFF_RV + RET_W
OFF_AQ = CTX_KV_COLS
OFF_RQ = OFF_AQ + ATTN_W
OFF_RG = OFF_RQ + RET_W
IN_COLS = OFF_RG + RET_W

V7X_VMEM_BYTES = 64 * 1024 * 1024
VMEM_LIMIT = V7X_VMEM_BYTES * 13 // 16

SUB_ROWS = 256
TILE_PROJ = 1024
TILE_ATTN_Q = 1024
TILE_OUT = 512
TILE_TOKENS = 512
ITEM_BLOCKS = 5
COMBINE_PARTS = 4
SC_GATHER_BUFS = 2
WEIGHT_SLOTS = 3
SC_WINDOW = 64
HI_MASK = 0xFFFF0000


def _cparams(*sem):
    return pltpu.CompilerParams(dimension_semantics=sem, vmem_limit_bytes=VMEM_LIMIT)


def _split(a):
    hi = a.astype(BF16)
    lo = (a - hi.astype(F32)).astype(BF16)
    return hi, lo


def _dot(a, b):
    return jnp.dot(a, b, preferred_element_type=F32)


def _dot_nt(a, b):
    return lax.dot_general(a, b, (((1,), (1,)), ((), ())), preferred_element_type=F32)


def _sigmoid(v):
    return 1.0 / (1.0 + jnp.exp(-v))


def _silu(v):
    return v * _sigmoid(v)


def _pack_halves(a, b):
    ua = lax.bitcast_convert_type(a.astype(BF16).astype(F32), U32)
    ub = lax.bitcast_convert_type(b.astype(BF16).astype(F32), U32)
    return lax.bitcast_convert_type((ua & jnp.uint32(HI_MASK)) | (ub >> 16), PACKED)


def _unpack_halves(p):
    u = lax.bitcast_convert_type(p, U32)
    a = lax.bitcast_convert_type(u & jnp.uint32(HI_MASK), F32)
    b = lax.bitcast_convert_type(u << 16, F32)
    return a, b


def _mod_kernel(c_ref, w_ref, b_ref, o_ref):
    s_hi, s_lo = _split(_silu(c_ref[...]))
    w_hi, w_lo = _split(w_ref[...])
    o_ref[...] = _dot(s_hi, w_hi) + _dot(s_hi, w_lo) + _dot(s_lo, w_hi) + b_ref[...]


def _modulation(cc, w_mod, b_mod):
    rows, d = cc.shape
    n = w_mod.shape[1]
    tn = 768
    return pl.pallas_call(
        _mod_kernel,
        grid=(n // tn,),
        in_specs=[pl.BlockSpec((rows, d), lambda j: (0, 0)),
                  pl.BlockSpec((d, tn), lambda j: (0, j)),
                  pl.BlockSpec((1, tn), lambda j: (0, j))],
        out_specs=pl.BlockSpec((rows, tn), lambda j: (0, j)),
        out_shape=jax.ShapeDtypeStruct((rows, n), F32),
        compiler_params=_cparams("arbitrary"),
        name="mod",
    )(cc, w_mod, b_mod.reshape(1, n))


def _segment_ones():
    r = lax.broadcasted_iota(I32, (LANES, LANES), 0) // HEAD_DIM
    c = lax.broadcasted_iota(I32, (LANES, LANES), 1) // HEAD_DIM
    return jnp.where(r == c, 1.0, 0.0).astype(BF16)


def _head_mean_sq(v, seg):
    hi, lo = _split(v * v)
    return (_dot(hi, seg) + _dot(lo, seg)) * (1.0 / HEAD_DIM)


def _proj_kernel(x_ref, sh_ref, sc_ref, nw_ref, wi_ref, qnw_ref, knw_ref, cos_ref, sin_ref,
                 *out_refs, rope, with_q):
    if with_q:
        klo_ref, khi_ref, vlo_ref, vhi_ref, rk_ref, rv_ref, q_ref, rq_ref, sg_ref = out_refs
    else:
        klo_ref, khi_ref, vlo_ref, vhi_ref, rk_ref, rv_ref = out_refs
    tm = x_ref.shape[1]
    sub = min(tm, SUB_ROWS)
    seg = _segment_ones()
    lane = lax.broadcasted_iota(I32, (sub, LANES), 1)
    low_half = lane < HEAD_DIM
    even = (lane & 1) == 0

    for r0 in range(0, tm, sub):
        rows = slice(r0, r0 + sub)
        x = x_ref[0, rows, :]
        h = x * lax.rsqrt(jnp.mean(x * x, axis=-1, keepdims=True) + EPS) * nw_ref[...]
        h = h * (1.0 + sc_ref[0]) + sh_ref[0]
        z = _dot(h.astype(BF16), wi_ref[...])

        def norm_rope(v, w128, rows=rows):
            v = v * lax.rsqrt(_head_mean_sq(v, seg) + EPS) * w128
            if rope:
                swapped = jnp.where(even, pltpu.roll(v, LANES - 1, 1), pltpu.roll(v, 1, 1))
                v = v * cos_ref[rows, :] + swapped * sin_ref[rows, :]
            return v

        k = norm_rope(z[:, OFF_AK:OFF_AK + KV_W], knw_ref[...])
        ksw = pltpu.roll(k, HEAD_DIM, 1)
        klo_ref[0, 0, :, rows] = jnp.transpose(jnp.where(low_half, k, 0.0)).astype(BF16)
        khi_ref[0, 0, :, rows] = jnp.transpose(jnp.where(low_half, 0.0, ksw)).astype(BF16)
        klo_ref[0, 1, :, rows] = jnp.transpose(jnp.where(low_half, ksw, 0.0)).astype(BF16)
        khi_ref[0, 1, :, rows] = jnp.transpose(jnp.where(low_half, 0.0, k)).astype(BF16)
        v = z[:, OFF_AV:OFF_AV + KV_W]
        vsw = pltpu.roll(v, HEAD_DIM, 1)
        vlo_ref[0, 0, rows, :] = jnp.where(low_half, v, 0.0).astype(BF16)
        vhi_ref[0, 0, rows, :] = jnp.where(low_half, 0.0, vsw).astype(BF16)
        vlo_ref[0, 1, rows, :] = jnp.where(low_half, vsw, 0.0).astype(BF16)
        vhi_ref[0, 1, rows, :] = jnp.where(low_half, 0.0, v).astype(BF16)
        rk_ref[0, rows, :] = (z[:, OFF_RK:OFF_RK + RET_W] * QK_SCALE).astype(BF16)
        rv_ref[0, rows, :] = z[:, OFF_RV:OFF_RV + RET_W].astype(BF16)
        if with_q:
            for j in range(ATTN_W // LANES):
                qj = norm_rope(z[:, OFF_AQ + j * LANES:OFF_AQ + (j + 1) * LANES], qnw_ref[...])
                q_ref[0, rows, j * LANES:(j + 1) * LANES] = (qj * (QK_SCALE * LOG2_E)).astype(BF16)
            rq_ref[0, rows, :] = z[:, OFF_RQ:OFF_RQ + RET_W].astype(BF16)
            sg_ref[0, rows, :] = _silu(z[:, OFF_RG:OFF_RG + RET_W]).astype(BF16)


def _projection(x, shift, scale, norm_w, wi_bf16, qnw, knw, cos, sin, *, rope, with_q, tm):
    b, l, d = x.shape
    tm = min(tm, l)
    ncols = IN_COLS if with_q else CTX_KV_COLS
    per_batch = shift.shape[0] > 1
    mod_idx = (lambda bi, i: (bi, 0, 0)) if per_batch else (lambda bi, i: (0, 0, 0))
    kv_shape = jax.ShapeDtypeStruct((b, ATTN_KV_HEADS, l, LANES), BF16)
    kv_spec = pl.BlockSpec((1, ATTN_KV_HEADS, tm, LANES), lambda bi, i: (bi, 0, i, 0))
    kt_shape = jax.ShapeDtypeStruct((b, ATTN_KV_HEADS, LANES, l), BF16)
    kt_spec = pl.BlockSpec((1, ATTN_KV_HEADS, LANES, tm), lambda bi, i: (bi, 0, 0, i))
    w_shape = jax.ShapeDtypeStruct((b, l, RET_W), BF16)
    w_spec = pl.BlockSpec((1, tm, RET_W), lambda bi, i: (bi, i, 0))
    out_shape = [kt_shape] * 2 + [kv_shape] * 2 + [w_shape] * 2
    out_specs = [kt_spec] * 2 + [kv_spec] * 2 + [w_spec] * 2
    if with_q:
        out_shape += [w_shape] * 3
        out_specs += [w_spec] * 3
    return pl.pallas_call(
        functools.partial(_proj_kernel, rope=rope, with_q=with_q),
        grid=(b, l // tm),
        in_specs=[pl.BlockSpec((1, tm, d), lambda bi, i: (bi, i, 0)),
                  pl.BlockSpec((1, 1, d), mod_idx),
                  pl.BlockSpec((1, 1, d), mod_idx),
                  pl.BlockSpec((1, d), lambda bi, i: (0, 0)),
                  pl.BlockSpec((d, ncols), lambda bi, i: (0, 0)),
                  pl.BlockSpec((1, LANES), lambda bi, i: (0, 0)),
                  pl.BlockSpec((1, LANES), lambda bi, i: (0, 0)),
                  pl.BlockSpec((tm, LANES), lambda bi, i: (i, 0)),
                  pl.BlockSpec((tm, LANES), lambda bi, i: (i, 0))],
        out_specs=out_specs,
        out_shape=out_shape,
        compiler_params=_cparams("arbitrary", "arbitrary"),
        name="proj_latent" if with_q else "proj_ctx",
    )(x, shift, scale, norm_w, wi_bf16, qnw, knw, cos, sin)


def _attn_kernel(shift_ref, q_ref, klo_ref, khi_ref, vlo_ref, vhi_ref, cklo_ref, ckhi_ref, cvlo_ref, cvhi_ref,
                 rq_ref, rk_ref, rv_ref, sg_ref, st_ref, m_ref, xi_ref,
                 o_ref, ret_ref, kl_s, kh_s, va_s, *, l, lc, streaming):
    lk = l + lc

    for c in range(st_ref.shape[1]):
        rows = slice(c * CHUNK, (c + 1) * CHUNK)
        for p in range(RET_PAIRS):
            cols = slice(p * LANES, (p + 1) * LANES)
            ret_ref[0, rows, cols] = _ret_chunk_out(rq_ref[0, rows, cols], rk_ref[0, rows, cols],
                                                    rv_ref[0, rows, cols], sg_ref[0, rows, cols],
                                                    m_ref[p], xi_ref[p], st_ref[0, c, p])

    @pl.when(pl.program_id(2) == 0)
    def _():
        kl_s[:, 0:l] = klo_ref[0, 0]
        kl_s[:, l:lk] = cklo_ref[0, 0]
        kh_s[:, 0:l] = khi_ref[0, 0]
        kh_s[:, l:lk] = ckhi_ref[0, 0]
        lane = lax.broadcasted_iota(I32, (lk, LANES), 1)
        ones_lo = jnp.where(lane < HEAD_DIM, 1.0, 0.0).astype(BF16)
        ones_hi = jnp.where(lane < HEAD_DIM, 0.0, 1.0).astype(BF16)
        for g, (v_ref, cv_ref, ones) in enumerate(((vlo_ref, cvlo_ref, ones_lo), (vhi_ref, cvhi_ref, ones_hi),
                                                   (vlo_ref, cvlo_ref, ones_lo), (vhi_ref, cvhi_ref, ones_hi))):
            v_col, one_col = (0, LANES) if g < 2 else (LANES, 0)
            va_s[g, 0:l, v_col:v_col + LANES] = v_ref[0, 0]
            va_s[g, l:lk, v_col:v_col + LANES] = cv_ref[0, 0]
            va_s[g, :, one_col:one_col + LANES] = ones

    q = q_ref[0]
    acc = []
    for g in range(GQA):
        qp = q[:, (g // 2) * LANES:(g // 2 + 1) * LANES]
        k_s = kl_s if g % 2 == 0 else kh_s
        if streaming:
            shift = shift_ref[0]
            o = None
            for c in range(0, lk, KEY_TILE):
                hi = min(c + KEY_TILE, lk)
                p = jnp.exp2(_dot(qp, k_s[:, c:hi]) - shift).astype(BF16)
                t = _dot(p, va_s[g, c:hi])
                o = t if o is None else o + t
        else:
            s = _dot(qp, k_s[...])
            p = jnp.exp2(s - jnp.max(s, axis=-1, keepdims=True)).astype(BF16)
            o = _dot(p, va_s[g])
        acc.append(o)
    out_a = acc[0] + acc[1]
    out_b = acc[2] + acc[3]
    o_ref[0, :, 0:LANES] = (out_a[:, 0:LANES] / out_a[:, LANES:2 * LANES]).astype(BF16)
    o_ref[0, :, LANES:2 * LANES] = (out_b[:, LANES:2 * LANES] / out_b[:, 0:LANES]).astype(BF16)


def _attention(shift, q, klo, khi, vlo, vhi, cklo, ckhi, cvlo, cvhi, rq, rk, rv, sg, st, m_tab, xi_tab,
               *, tq, streaming):
    b, l, _ = q.shape
    lc = cvlo.shape[2]
    lk = l + lc
    tq = min(tq, l)
    gw = GQA * HEAD_DIM
    nq = l // tq
    n_chunks = st.shape[1]
    rc = n_chunks // (ATTN_KV_HEADS * nq)
    assert rc * ATTN_KV_HEADS * nq == n_chunks
    kt_spec = pl.BlockSpec((1, 1, LANES, l), lambda bi, h, i: (bi, h, 0, 0))
    kv_spec = pl.BlockSpec((1, 1, l, LANES), lambda bi, h, i: (bi, h, 0, 0))
    ckt_spec = pl.BlockSpec((1, 1, LANES, lc), lambda bi, h, i: (bi, h, 0, 0))
    ckv_spec = pl.BlockSpec((1, 1, lc, LANES), lambda bi, h, i: (bi, h, 0, 0))
    ret_spec = pl.BlockSpec((1, rc * CHUNK, RET_W), lambda bi, h, i: (bi, h * nq + i, 0))
    st_spec = pl.BlockSpec((1, rc, RET_PAIRS, 2 * LANES, LANES), lambda bi, h, i: (bi, h * nq + i, 0, 0, 0))
    tab_spec = pl.BlockSpec((RET_PAIRS, CHUNK, 2 * LANES), lambda bi, h, i: (0, 0, 0))
    return pl.pallas_call(
        functools.partial(_attn_kernel, l=l, lc=lc, streaming=streaming),
        grid=(b, ATTN_KV_HEADS, nq),
        in_specs=([pl.BlockSpec(memory_space=pltpu.SMEM),
                   pl.BlockSpec((1, tq, gw), lambda bi, h, i: (bi, i, h))] + [kt_spec] * 2 + [kv_spec] * 2
                  + [ckt_spec] * 2 + [ckv_spec] * 2 + [ret_spec] * 4 + [st_spec, tab_spec, tab_spec]),
        out_specs=[pl.BlockSpec((1, tq, gw), lambda bi, h, i: (bi, i, h)), ret_spec],
        out_shape=[jax.ShapeDtypeStruct((b, l, ATTN_W), BF16), jax.ShapeDtypeStruct((b, l, RET_W), BF16)],
        scratch_shapes=[pltpu.VMEM((LANES, lk), BF16), pltpu.VMEM((LANES, lk), BF16),
                        pltpu.VMEM((GQA, lk, 2 * LANES), BF16)],
        compiler_params=_cparams("arbitrary", "arbitrary", "arbitrary"),
        name="attn_stream" if streaming else "attn",
    )(shift, q, klo, khi, vlo, vhi, cklo, ckhi, cvlo, cvhi, rq, rk, rv, sg, st, m_tab, xi_tab)


def _log_sigmoid(v):
    return jnp.minimum(v, 0.0) - jnp.log(1.0 + jnp.exp(-jnp.abs(v)))


def _pair_masks():
    lane = lax.broadcasted_iota(I32, (CHUNK, LANES), 1)
    diag = (lax.broadcasted_iota(I32, (LANES, LANES), 0) // HEAD_DIM
            == lax.broadcasted_iota(I32, (LANES, LANES), 1) // HEAD_DIM)
    return lane < HEAD_DIM, diag


def _ret_chunk_out(qp, kp, vp, gate, m2, xi2, st2):
    low_half, diag = _pair_masks()
    seg = jnp.where(diag, 1.0, 0.0).astype(BF16)
    seg2 = jnp.concatenate([seg, seg], axis=0)

    def split_heads(a):
        zero = jnp.zeros_like(a)
        return jnp.concatenate([jnp.where(low_half, a, zero), jnp.where(low_half, zero, a)], axis=0)

    s2 = _dot_nt(qp, split_heads(kp))
    y = _dot((s2 * m2).astype(BF16), split_heads(vp))
    qf = qp.astype(F32)
    y += _dot((jnp.concatenate([qf, qf], axis=1) * xi2).astype(BF16), st2)
    hi, lo = _split(y * y)
    ms = _dot(jnp.concatenate([hi, lo], axis=1), seg2) * (1.0 / HEAD_DIM)
    return (y * lax.rsqrt(ms + EPS) * gate.astype(F32)).astype(BF16)


def _ret_state_kernel(rk_ref, rv_ref, crk_ref, crv_ref, df_ref, db_ref, st_ref, m_ref, xi_ref,
                      zeta_s, kv_s, *, l, lc):
    m_s, xi_s, st_s = m_ref, xi_ref, st_ref.at[0]
    n = l // CHUNK
    nc = lc // CHUNK
    lgf = _log_sigmoid(df_ref[...])
    lgb = _log_sigmoid(db_ref[...])
    pos = lax.broadcasted_iota(I32, (CHUNK, LANES), 0).astype(F32)
    row = lax.broadcasted_iota(I32, (CHUNK, CHUNK), 0)
    col = lax.broadcasted_iota(I32, (CHUNK, CHUNK), 1)
    diff = (row - col).astype(F32)
    g_chunk = []
    for p in range(RET_PAIRS):
        cols = slice(p * LANES, (p + 1) * LANES)
        lf, lb = lgf[:, cols], lgb[:, cols]
        xi_s[p, :, 0:LANES] = jnp.exp((pos + 1.0) * lf)
        xi_s[p, :, LANES:] = jnp.exp((CHUNK - pos) * lb)
        zeta_s[p, :, 0:LANES] = jnp.exp((CHUNK - 1.0 - pos) * lf)
        zeta_s[p, :, LANES:] = jnp.exp(pos * lb)
        g_chunk.append((jnp.exp(CHUNK * lf), jnp.exp(CHUNK * lb)))
        for j in range(2):
            h = 2 * p + j
            hf = lgf[:, h * HEAD_DIM:h * HEAD_DIM + 1]
            hb = lgb[:, h * HEAD_DIM:h * HEAD_DIM + 1]
            m_s[p, :, j * CHUNK:(j + 1) * CHUNK] = jnp.where(
                diff > 0, jnp.exp(diff * hf), jnp.where(diff < 0, jnp.exp(-diff * hb), 2.0))

    _, diag = _pair_masks()
    diag2 = jnp.concatenate([diag, diag], axis=0)

    def contrib(k_ref, v_ref, r0, p):
        cols = slice(p * LANES, (p + 1) * LANES)
        kp = k_ref[0, pl.ds(r0, CHUNK), cols].astype(F32)
        kz = jnp.concatenate([kp, kp], axis=1) * zeta_s[p]
        kv = _dot(jnp.transpose(kz).astype(BF16), v_ref[0, pl.ds(r0, CHUNK), cols])
        return jnp.where(diag2, kv, 0.0)

    for c in range(nc):
        for p in range(RET_PAIRS):
            kv_s[c, p] = contrib(crk_ref, crv_ref, c * CHUNK, p)

    def contrib_body(c, carry):
        r0 = pl.multiple_of(c * CHUNK, CHUNK)
        for p in range(RET_PAIRS):
            kv_s[nc + c, p] = contrib(rk_ref, rv_ref, r0, p)
        return carry

    lax.fori_loop(0, n, contrib_body, 0, unroll=4)

    for p in range(RET_PAIRS):
        gf, gb = g_chunk[p]
        sf = jnp.zeros((LANES, LANES), F32)
        sb = jnp.zeros((LANES, LANES), F32)
        for c in range(nc):
            sf = gf * sf + kv_s[c, p, 0:LANES]
            sb = gb * sb + kv_s[nc - 1 - c, p, LANES:]

        def fwd_scan(c, s, p=p, gf=gf):
            st_s[c, p, 0:LANES] = s.astype(BF16)
            return gf * s + kv_s[nc + c, p, 0:LANES]

        def bwd_scan(j, s, p=p, gb=gb):
            c = n - 1 - j
            st_s[c, p, LANES:] = s.astype(BF16)
            return gb * s + kv_s[nc + c, p, LANES:]

        lax.fori_loop(0, n, fwd_scan, sf)
        lax.fori_loop(0, n, bwd_scan, sb)


def _retention_states(rk, rv, crk, crv, dec_f, dec_b):
    b, l, _ = rk.shape
    lc = crk.shape[1]
    n = l // CHUNK
    nc = lc // CHUNK
    spec = pl.BlockSpec((1, l, RET_W), lambda bi: (bi, 0, 0))
    cspec = pl.BlockSpec((1, lc, RET_W), lambda bi: (bi, 0, 0))
    dspec = pl.BlockSpec((1, RET_W), lambda bi: (0, 0))
    tab_shape = jax.ShapeDtypeStruct((RET_PAIRS, CHUNK, 2 * LANES), F32)
    tab_spec = pl.BlockSpec((RET_PAIRS, CHUNK, 2 * LANES), lambda bi: (0, 0, 0))
    return pl.pallas_call(
        functools.partial(_ret_state_kernel, l=l, lc=lc),
        grid=(b,),
        in_specs=[spec, spec, cspec, cspec, dspec, dspec],
        out_specs=[pl.BlockSpec((1, n, RET_PAIRS, 2 * LANES, LANES), lambda bi: (bi, 0, 0, 0, 0)),
                   tab_spec, tab_spec],
        out_shape=[jax.ShapeDtypeStruct((b, n, RET_PAIRS, 2 * LANES, LANES), BF16), tab_shape, tab_shape],
        scratch_shapes=[pltpu.VMEM((RET_PAIRS, CHUNK, 2 * LANES), F32),
                        pltpu.VMEM((nc + n, RET_PAIRS, 2 * LANES, LANES), F32)],
        compiler_params=_cparams("arbitrary"),
        name="ret_state",
    )(rk, rv, crk, crv, dec_f, dec_b)


def _out_kernel(attn_ref, ret_ref, x_ref, wa_ref, wr_ref, g1_ref, sh_ref, sc_ref, nw_ref, rhi_ref, rlo_ref,
                x1_ref, hp_ref, lg_ref):
    y = _dot(attn_ref[0], wa_ref[...]) + _dot(ret_ref[0], wr_ref[...])
    x1 = x_ref[0] + g1_ref[0] * y
    x1_ref[0] = x1
    h = x1 * lax.rsqrt(jnp.mean(x1 * x1, axis=-1, keepdims=True) + EPS) * nw_ref[...]
    h = h * (1.0 + sc_ref[0]) + sh_ref[0]
    half = h.shape[1] // 2
    hp_ref[...] = _pack_halves(h[:, :half], h[:, half:])
    h_hi, h_lo = _split(h)
    lg_ref[...] = _dot_nt(rhi_ref[...], h_hi) + _dot_nt(rhi_ref[...], h_lo) + _dot_nt(rlo_ref[...], h_hi)


def _out_projection(attn, ret, x, wa, wr, g1, sh2, sc2, norm_w, r_hi, r_lo, *, tm):
    b, l, d = x.shape
    tm = min(tm, l)
    nt = l // tm
    t = b * l
    mspec = pl.BlockSpec((1, 1, d), lambda bi, i: (bi, 0, 0))
    return pl.pallas_call(
        _out_kernel,
        grid=(b, nt),
        in_specs=[pl.BlockSpec((1, tm, ATTN_W), lambda bi, i: (bi, i, 0)),
                  pl.BlockSpec((1, tm, RET_W), lambda bi, i: (bi, i, 0)),
                  pl.BlockSpec((1, tm, d), lambda bi, i: (bi, i, 0)),
                  pl.BlockSpec((ATTN_W, d), lambda bi, i: (0, 0)),
                  pl.BlockSpec((RET_W, d), lambda bi, i: (0, 0)),
                  mspec, mspec, mspec,
                  pl.BlockSpec((1, d), lambda bi, i: (0, 0)),
                  pl.BlockSpec((N_EXPERTS, d), lambda bi, i: (0, 0)),
                  pl.BlockSpec((N_EXPERTS, d), lambda bi, i: (0, 0))],
        out_specs=[pl.BlockSpec((1, tm, d), lambda bi, i: (bi, i, 0)),
                   pl.BlockSpec((tm, d // 2), lambda bi, i: (bi * nt + i, 0)),
                   pl.BlockSpec((N_EXPERTS, tm), lambda bi, i: (0, bi * nt + i))],
        out_shape=[jax.ShapeDtypeStruct((b, l, d), F32),
                   jax.ShapeDtypeStruct((t, d // 2), PACKED),
                   jax.ShapeDtypeStruct((N_EXPERTS, t), F32)],
        compiler_params=_cparams("arbitrary", "arbitrary"),
        name="out_proj",
    )(attn, ret, x, wa, wr, g1, sh2, sc2, norm_w, r_hi, r_lo)


def _route_kernel(lg_ref, bias_ref, idx_ref, w_ref, rank_ref, cnt_col_ref, cnt_row_ref, tri_s, col_s, row_s):
    tb = lg_ref.shape[1]
    step = pl.program_id(0)

    @pl.when(step == 0)
    def _():
        r = lax.broadcasted_iota(I32, (tb, tb), 0)
        c = lax.broadcasted_iota(I32, (tb, tb), 1)
        tri_s[...] = jnp.where(r <= c, 1.0, 0.0).astype(BF16)
        col_s[...] = jnp.zeros_like(col_s)
        row_s[...] = jnp.zeros_like(row_s)

    scores = _sigmoid(lg_ref[...])
    biased = scores + bias_ref[...]
    neg = -jnp.inf
    sub = lax.broadcasted_iota(I32, (GROUP_SIZE, tb), 0).astype(F32)

    gscore = []
    for g in range(N_GROUPS):
        blk = biased[g * GROUP_SIZE:(g + 1) * GROUP_SIZE]
        m1 = jnp.max(blk, axis=0, keepdims=True)
        first = jnp.min(jnp.where(blk == m1, sub, float(GROUP_SIZE)), axis=0, keepdims=True)
        m2 = jnp.max(jnp.where(sub == first, neg, blk), axis=0, keepdims=True)
        gscore.append(m1 + m2)
    gs = jnp.concatenate(gscore, axis=0)
    gsub = lax.broadcasted_iota(I32, (N_GROUPS, tb), 0).astype(F32)
    keep = jnp.zeros((N_GROUPS, tb), F32)
    for _ in range(TOPK_GROUPS):
        m = jnp.max(gs, axis=0, keepdims=True)
        first = jnp.min(jnp.where(gs == m, gsub, float(N_GROUPS)), axis=0, keepdims=True)
        sel = gsub == first
        keep = jnp.where(sel, 1.0, keep)
        gs = jnp.where(sel, neg, gs)
    masked = jnp.concatenate(
        [jnp.where(keep[g:g + 1] > 0.0, biased[g * GROUP_SIZE:(g + 1) * GROUP_SIZE], neg)
         for g in range(N_GROUPS)], axis=0)

    esub = lax.broadcasted_iota(I32, (N_EXPERTS, tb), 0).astype(F32)
    sels, idxs, ws = [], [], []
    chosen = jnp.zeros((N_EXPERTS, tb), F32)
    for _ in range(TOP_K):
        m = jnp.max(masked, axis=0, keepdims=True)
        first = jnp.min(jnp.where(masked == m, esub, float(N_EXPERTS)), axis=0, keepdims=True)
        sel = esub == first
        sels.append(sel)
        idxs.append(first)
        ws.append(jnp.sum(jnp.where(sel, scores, 0.0), axis=0, keepdims=True))
        chosen = jnp.where(sel, 1.0, chosen)
        masked = jnp.where(sel, neg, masked)
    wsum = ws[0]
    for k in range(1, TOP_K):
        wsum = wsum + ws[k]
    idx_ref[...] = jnp.concatenate(idxs, axis=0).astype(I32)
    w_ref[...] = jnp.concatenate([wk / wsum * ROUTED_SCALE for wk in ws], axis=0)

    chosen_b = chosen.astype(BF16)
    incl = _dot(chosen_b, tri_s[...])
    before = incl - chosen + col_s[...]
    rank_ref[...] = jnp.concatenate(
        [jnp.sum(jnp.where(sel, before, 0.0), axis=0, keepdims=True) for sel in sels], axis=0).astype(I32)
    col_s[...] = col_s[...] + incl[:, tb - 1:tb]
    row_s[...] = row_s[...] + _dot_nt(jnp.ones((8, tb), BF16), chosen_b)
    cnt_col_ref[...] = col_s[...].astype(I32)
    cnt_row_ref[...] = row_s[...].astype(I32)


def _route(logits_t, bias_col, *, tb):
    e, t = logits_t.shape
    tb = min(tb, t)
    kspec = pl.BlockSpec((TOP_K, tb), lambda i: (0, i))
    return pl.pallas_call(
        _route_kernel,
        grid=(t // tb,),
        in_specs=[pl.BlockSpec((e, tb), lambda i: (0, i)),
                  pl.BlockSpec((e, 1), lambda i: (0, 0))],
        out_specs=[kspec, kspec, kspec,
                   pl.BlockSpec((e, 1), lambda i: (0, 0)),
                   pl.BlockSpec((8, e), lambda i: (0, 0))],
        out_shape=[jax.ShapeDtypeStruct((TOP_K, t), I32),
                   jax.ShapeDtypeStruct((TOP_K, t), F32),
                   jax.ShapeDtypeStruct((TOP_K, t), I32),
                   jax.ShapeDtypeStruct((e, 1), I32),
                   jax.ShapeDtypeStruct((8, e), I32)],
        scratch_shapes=[pltpu.VMEM((tb, tb), BF16), pltpu.VMEM((e, 1), F32), pltpu.VMEM((8, e), F32)],
        compiler_params=_cparams("arbitrary"),
        name="route",
    )(logits_t, bias_col)


def _pad_block(cnt):
    return (cnt + (MOE_BLOCK - 1)) // MOE_BLOCK * MOE_BLOCK


def _max_items(n_blocks):
    return n_blocks // ITEM_BLOCKS + N_EXPERTS


def _dest_kernel(idx_ref, rank_ref, cnt_col_ref, cnt_row_ref, dest_ref, meta_ref, items_ref, start_s):
    tb = idx_ref.shape[1]
    nip = items_ref.shape[1]

    @pl.when(pl.program_id(0) == 0)
    def _():
        pad_col = _pad_block(cnt_col_ref[...])
        pad_row = _pad_block(cnt_row_ref[0:1, :])
        er = lax.broadcasted_iota(I32, (N_EXPERTS, N_EXPERTS), 0)
        ec = lax.broadcasted_iota(I32, (N_EXPERTS, N_EXPERTS), 1)
        start_col = jnp.sum(jnp.where(ec < er, pad_row, 0), axis=1, keepdims=True)
        start_row = jnp.sum(jnp.where(er < ec, pad_col, 0), axis=0, keepdims=True)
        start_s[...] = start_col

        used = jnp.sum(pad_row, axis=1, keepdims=True) // MOE_BLOCK
        meta_ref[...] = jnp.concatenate(
            [cnt_row_ref[0:1, :], start_row, pad_row, jnp.broadcast_to(used, (1, N_EXPERTS)),
             jnp.zeros((4, N_EXPERTS), I32)], axis=0)

        nb_col = pad_col // MOE_BLOCK
        it_col = (nb_col + (ITEM_BLOCKS - 1)) // ITEM_BLOCKS
        it_row = (pad_row // MOE_BLOCK + (ITEM_BLOCKS - 1)) // ITEM_BLOCKS
        it_start = jnp.sum(jnp.where(ec < er, it_row, 0), axis=1, keepdims=True)
        n_items = jnp.sum(it_row, axis=1, keepdims=True)
        lane = lax.broadcasted_iota(I32, (1, nip), 1)
        owner = jnp.sum(jnp.where(it_start + it_col <= lane, 1, 0), axis=0, keepdims=True)
        owner = jnp.minimum(owner, N_EXPERTS - 1)
        onehot = lax.broadcasted_iota(I32, (N_EXPERTS, nip), 0) == owner

        def pick(col):
            return jnp.sum(jnp.where(onehot, col, 0), axis=0, keepdims=True)

        j = lane - pick(it_start)
        block0 = pick(start_col) // MOE_BLOCK + ITEM_BLOCKS * j
        nvalid = jnp.clip(pick(nb_col) - ITEM_BLOCKS * j, 0, ITEM_BLOCKS)
        items_ref[...] = jnp.concatenate(
            [owner, block0, jnp.where(lane < n_items, nvalid, 0), jnp.broadcast_to(n_items, (1, nip)),
             jnp.zeros((4, nip), I32)], axis=0)

    start_col = start_s[...]
    esub = lax.broadcasted_iota(I32, (N_EXPERTS, tb), 0)
    rows = []
    for k in range(TOP_K):
        onehot = esub == idx_ref[k:k + 1, :]
        rows.append(jnp.sum(jnp.where(onehot, start_col, 0), axis=0, keepdims=True) + rank_ref[k:k + 1, :])
    dest_ref[0] = jnp.concatenate(rows, axis=0)


def _destinations(idx_t, rank_t, cnt_col, cnt_row, *, tb, n_blocks):
    _, t = idx_t.shape
    tb = min(tb, t)
    nip = (_max_items(n_blocks) + LANES - 1) // LANES * LANES
    kspec = pl.BlockSpec((TOP_K, tb), lambda i: (0, i))
    return pl.pallas_call(
        _dest_kernel,
        grid=(t // tb,),
        in_specs=[kspec, kspec,
                  pl.BlockSpec((N_EXPERTS, 1), lambda i: (0, 0)),
                  pl.BlockSpec((8, N_EXPERTS), lambda i: (0, 0))],
        out_specs=[pl.BlockSpec((1, TOP_K, tb), lambda i: (i, 0, 0)),
                   pl.BlockSpec((8, N_EXPERTS), lambda i: (0, 0)),
                   pl.BlockSpec((8, nip), lambda i: (0, 0))],
        out_shape=[jax.ShapeDtypeStruct((t // tb, TOP_K, tb), I32),
                   jax.ShapeDtypeStruct((8, N_EXPERTS), I32),
                   jax.ShapeDtypeStruct((8, nip), I32)],
        scratch_shapes=[pltpu.VMEM((N_EXPERTS, 1), I32)],
        compiler_params=_cparams("arbitrary"),
        name="dest",
    )(idx_t, rank_t, cnt_col, cnt_row)


_PAD_BITS = (64, 32, 16, 8)


def _sc_scatter_rows(rows, dest_win, n_out_rows):
    n_win, n_slots, win = dest_win.shape
    width = rows.shape[1]
    info = plsc.get_sparse_core_info()
    n_workers = info.num_cores * info.num_subcores
    per_worker = n_win // n_workers
    assert per_worker * n_workers == n_win and win <= LANES
    mesh = plsc.VectorSubcoreMesh(core_axis_name="c", subcore_axis_name="s")

    def body(rows_hbm, dest_hbm, out_hbm, idx_v, rows_v, sem):
        wid = lax.axis_index("s") * info.num_cores + lax.axis_index("c")

        @pl.loop(0, per_worker)
        def _(j):
            w = wid * per_worker + j
            pltpu.sync_copy(dest_hbm.at[w], idx_v)
            pltpu.sync_copy(rows_hbm.at[pl.ds(w * win, win)], rows_v)
            copies = [pltpu.async_copy(rows_v, out_hbm.at[idx_v.at[k]], sem) for k in range(n_slots)]
            for cp in copies:
                cp.wait()

    return pl.kernel(
        body,
        out_type=jax.ShapeDtypeStruct((n_out_rows, width), rows.dtype),
        mesh=mesh,
        scratch_types=[pltpu.VMEM((n_slots, win), I32), pltpu.VMEM((win, width), rows.dtype),
                       pltpu.SemaphoreType.DMA],
        name="sc_scatter",
    )(rows, dest_win)


def _pad_fill_kernel(meta_ref, xs_in, xs_hbm, zero_s, sem_z, *, e_per_step):
    del xs_in
    step = pl.program_id(0)
    zero_s[...] = jnp.zeros_like(zero_s)

    def tail_copy(c):
        row0 = pl.multiple_of((meta_ref[3, 0] + c) * MOE_BLOCK, MOE_BLOCK)
        return pltpu.make_async_copy(zero_s, xs_hbm.at[pl.ds(row0, MOE_BLOCK)], sem_z)

    @pl.when(step == 0)
    def _():
        for c in range(ITEM_BLOCKS - 1):
            tail_copy(c).start()
        for c in range(ITEM_BLOCKS - 1):
            tail_copy(c).wait()

    def pad_copies(e):
        cnt = meta_ref[0, e]
        off = meta_ref[1, e] + cnt
        rem = meta_ref[2, e] - cnt
        head = rem & (SUBLANES - 1)
        out = []
        for i in range(SUBLANES - 1):
            out.append((i < head,
                        pltpu.make_async_copy(zero_s.at[pl.ds(0, 1)], xs_hbm.at[pl.ds(off + i, 1)], sem_z)))
        off = off + head
        for bit in _PAD_BITS:
            out.append(((rem & bit) != 0,
                        pltpu.make_async_copy(zero_s.at[pl.ds(0, bit)],
                                              xs_hbm.at[pl.ds(pl.multiple_of(off, SUBLANES), bit)], sem_z)))
            off = off + (rem & bit)
        return out

    for j in range(e_per_step):
        for cond, c in pad_copies(step * e_per_step + j):
            pl.when(cond)(c.start)
    for j in range(e_per_step):
        for cond, c in pad_copies(step * e_per_step + j):
            pl.when(cond)(c.wait)


PAD_FILL_EXPERTS_PER_STEP = 8


def _pad_fill(meta, xs):
    half = xs.shape[1]
    return pl.pallas_call(
        functools.partial(_pad_fill_kernel, e_per_step=PAD_FILL_EXPERTS_PER_STEP),
        grid=(N_EXPERTS // PAD_FILL_EXPERTS_PER_STEP,),
        in_specs=[pl.BlockSpec(memory_space=pltpu.SMEM),
                  pl.BlockSpec(memory_space=pl.ANY)],
        out_specs=pl.BlockSpec(memory_space=pl.ANY),
        out_shape=jax.ShapeDtypeStruct(xs.shape, xs.dtype),
        input_output_aliases={1: 0},
        scratch_shapes=[pltpu.VMEM((MOE_BLOCK, half), PACKED), pltpu.SemaphoreType.DMA],
        compiler_params=_cparams("arbitrary"),
        name="pad_fill",
    )(meta, xs)


def _experts_kernel(items_ref, xs_hbm, wg_hbm, wu_hbm, wd_hbm, ys_hbm,
                    xbuf, ybuf, wg_f, wu_f, wd_f, wg_s, wu_s, wd_s, sem_x, sem_y, sem_w):
    n_items = items_ref[3, 0]
    rows = ITEM_BLOCKS * MOE_BLOCK

    def w_copies(e, s):
        return [pltpu.make_async_copy(src.at[e], dst.at[s], sem_w.at[s])
                for src, dst in ((wg_hbm, wg_f), (wu_hbm, wu_f), (wd_hbm, wd_f))]

    def x_copy(item, s):
        row0 = pl.multiple_of(items_ref[1, item] * MOE_BLOCK, MOE_BLOCK)
        return pltpu.make_async_copy(xs_hbm.at[pl.ds(row0, rows)], xbuf.at[s], sem_x.at[s])

    def y_copies(item, s, fn):
        for k in range(ITEM_BLOCKS):
            row0 = pl.multiple_of((items_ref[1, item] + k) * MOE_BLOCK, MOE_BLOCK)
            cp = pltpu.make_async_copy(ybuf.at[s, pl.ds(k * MOE_BLOCK, MOE_BLOCK)],
                                       ys_hbm.at[pl.ds(row0, MOE_BLOCK)], sem_y.at[s])
            pl.when(k < items_ref[2, item])(functools.partial(fn, cp))

    def expert_of(item):
        return items_ref[0, jnp.minimum(item, n_items - 1)]

    def changes_at(item):
        return ((item < n_items) & (expert_of(item) != expert_of(item - 1))).astype(I32)

    @pl.when(n_items > 0)
    def _():
        x_copy(0, 0).start()
        for cp in w_copies(expert_of(0), 0):
            cp.start()

        @pl.when(changes_at(1) == 1)
        def _():
            for cp in w_copies(expert_of(1), 1):
                cp.start()

    def item_body(i, ordinal):
        slot = i % 2
        prev = jnp.maximum(i - 1, 0)
        e = expert_of(i)
        new_expert = (i == 0) | (e != expert_of(prev))
        c1 = changes_at(i + 1)
        c2 = changes_at(i + 2)

        @pl.when(i + 1 < n_items)
        def _():
            x_copy(i + 1, 1 - slot).start()

        @pl.when(new_expert)
        def _():
            wslot = ordinal % WEIGHT_SLOTS
            for cp in w_copies(e, wslot):
                cp.wait()
            wg_s[...] = wg_f[wslot].astype(BF16)
            wu_s[...] = wu_f[wslot].astype(BF16)
            wd_s[...] = wd_f[wslot].astype(BF16)

        @pl.when(c2 == 1)
        def _():
            for cp in w_copies(expert_of(i + 2), (ordinal + c1 + 1) % WEIGHT_SLOTS):
                cp.start()

        x_copy(i, slot).wait()
        xa, xb = _unpack_halves(xbuf[slot])
        xa = xa.astype(BF16)
        xb = xb.astype(BF16)
        half = xa.shape[1]
        g = _dot(xa, wg_s[0:half]) + _dot(xb, wg_s[half:])
        u = _dot(xa, wu_s[0:half]) + _dot(xb, wu_s[half:])
        y = _dot((_silu(g) * u).astype(BF16), wd_s[...])
        ybuf[slot] = _pack_halves(y[:, :half], y[:, half:])
        y_copies(i, slot, lambda cp: cp.start())

        @pl.when(i > 0)
        def _():
            y_copies(prev, 1 - slot, lambda cp: cp.wait())

        return ordinal + c1

    lax.fori_loop(0, n_items, item_body, jnp.int32(0))

    @pl.when(n_items > 0)
    def _():
        last = n_items - 1
        y_copies(last, last % 2, lambda cp: cp.wait())


def _experts(items, xs, w_gate, w_up, w_down, *, n_blocks):
    half = xs.shape[1]
    e, d, f = w_gate.shape
    rows = ITEM_BLOCKS * MOE_BLOCK
    any_spec = pl.BlockSpec(memory_space=pl.ANY)
    return pl.pallas_call(
        _experts_kernel,
        grid_spec=pltpu.PrefetchScalarGridSpec(
            num_scalar_prefetch=1,
            grid=(1,),
            in_specs=[any_spec, any_spec, any_spec, any_spec],
            out_specs=any_spec,
            scratch_shapes=[pltpu.VMEM((2, rows, half), PACKED), pltpu.VMEM((2, rows, half), PACKED),
                            pltpu.VMEM((WEIGHT_SLOTS, d, f), F32), pltpu.VMEM((WEIGHT_SLOTS, d, f), F32),
                            pltpu.VMEM((WEIGHT_SLOTS, f, d), F32),
                            pltpu.VMEM((d, f), BF16), pltpu.VMEM((d, f), BF16), pltpu.VMEM((f, d), BF16),
                            pltpu.SemaphoreType.DMA((2,)), pltpu.SemaphoreType.DMA((2,)),
                            pltpu.SemaphoreType.DMA((WEIGHT_SLOTS,))]),
        out_shape=jax.ShapeDtypeStruct((n_blocks * MOE_BLOCK, half), PACKED),
        compiler_params=_cparams("arbitrary"),
        name="experts",
    )(items, xs, w_gate, w_up, w_down)


def _sc_gather_rows(table, idx):
    n_idx = idx.shape[0]
    width = table.shape[1]
    info = plsc.get_sparse_core_info()
    n_workers = info.num_cores * info.num_subcores
    per_worker = n_idx // n_workers
    assert per_worker * n_workers == n_idx and per_worker % (SC_GATHER_BUFS * SC_WINDOW) == 0
    mesh = plsc.VectorSubcoreMesh(core_axis_name="c", subcore_axis_name="s")

    def body(table_hbm, idx_hbm, out_hbm, idx_v, rows_v, sem_g, sem_o):
        wid = lax.axis_index("s") * info.num_cores + lax.axis_index("c")
        base = wid * per_worker

        @pl.loop(0, per_worker // (SC_GATHER_BUFS * SC_WINDOW))
        def _(it):
            offs = [base + (it * SC_GATHER_BUFS + b) * SC_WINDOW for b in range(SC_GATHER_BUFS)]
            gathers = []
            for b, off in enumerate(offs):
                pltpu.sync_copy(idx_hbm.at[pl.ds(off, SC_WINDOW)], idx_v.at[b])
                gathers.append(pltpu.async_copy(table_hbm.at[idx_v.at[b]], rows_v.at[b], sem_g.at[b]))
            writes = []
            for b, off in enumerate(offs):
                gathers[b].wait()
                writes.append(pltpu.async_copy(rows_v.at[b], out_hbm.at[pl.ds(off, SC_WINDOW)], sem_o.at[b]))
            for cp in writes:
                cp.wait()

    return pl.kernel(
        body,
        out_type=jax.ShapeDtypeStruct((n_idx, width), table.dtype),
        mesh=mesh,
        scratch_types=[pltpu.VMEM((SC_GATHER_BUFS, SC_WINDOW), I32),
                       pltpu.VMEM((SC_GATHER_BUFS, SC_WINDOW, width), table.dtype),
                       pltpu.SemaphoreType.DMA((SC_GATHER_BUFS,)), pltpu.SemaphoreType.DMA((SC_GATHER_BUFS,))],
        name="sc_gather",
    )(table, idx)


def _combine_kernel(hp_ref, x1_ref, g2_ref, w_ref, sgw_ref, suw_ref, sdw_ref, yg_ref, *rest):
    o_ref = rest[-1]
    xa, xb = _unpack_halves(hp_ref[...])
    xa = xa.astype(BF16)
    xb = xb.astype(BF16)
    half = xa.shape[1]
    tb = xa.shape[0]
    g = _dot(xa, sgw_ref[0:half]) + _dot(xb, sgw_ref[half:])
    u = _dot(xa, suw_ref[0:half]) + _dot(xb, suw_ref[half:])
    shared = _dot((_silu(g) * u).astype(BF16), sdw_ref[...])

    w = w_ref[...]
    acc_a = jnp.zeros((tb, half), F32)
    acc_b = jnp.zeros((tb, half), F32)
    for k in range(TOP_K):
        ya, yb = _unpack_halves(yg_ref[0, k])
        acc_a += ya * w[:, k:k + 1]
        acc_b += yb * w[:, k:k + 1]
    g2 = g2_ref[0]
    o_ref[:, 0:half] = x1_ref[:, 0:half] + g2[:, 0:half] * (acc_a + shared[:, 0:half])
    o_ref[:, half:] = x1_ref[:, half:] + g2[:, half:] * (acc_b + shared[:, half:])


def _combine(hp, x1, g2, w_tok, sgw, suw, sdw, yg, out_prev, *, tb, seq_len, first_step):
    t, half = hp.shape
    d = 2 * half
    per_seq = seq_len // tb
    f = sgw.shape[1]
    s0 = first_step
    in_specs = [pl.BlockSpec((tb, half), lambda i: (s0 + i, 0)),
                pl.BlockSpec((tb, d), lambda i: (s0 + i, 0)),
                pl.BlockSpec((1, 1, d), lambda i: ((s0 + i) // per_seq, 0, 0)),
                pl.BlockSpec((tb, TOP_K), lambda i: (s0 + i, 0)),
                pl.BlockSpec((d, f), lambda i: (0, 0)),
                pl.BlockSpec((d, f), lambda i: (0, 0)),
                pl.BlockSpec((f, d), lambda i: (0, 0)),
                pl.BlockSpec((1, TOP_K, tb, half), lambda i: (i, 0, 0, 0))]
    args = [hp, x1, g2, w_tok, sgw, suw, sdw, yg]
    aliases = {}
    if out_prev is not None:
        in_specs.append(pl.BlockSpec(memory_space=pl.ANY))
        args.append(out_prev)
        aliases = {len(args) - 1: 0}
    return pl.pallas_call(
        _combine_kernel,
        grid=(yg.shape[0],),
        in_specs=in_specs,
        out_specs=pl.BlockSpec((tb, d), lambda i: (s0 + i, 0)),
        out_shape=jax.ShapeDtypeStruct((t, d), F32),
        input_output_aliases=aliases,
        compiler_params=_cparams("arbitrary"),
        name="combine",
    )(*args)


def _rope_tables(l):
    rows = l // GRID_W
    r = jnp.repeat(jnp.arange(rows), GRID_W).astype(F32)
    col = jnp.tile(jnp.arange(GRID_W), rows).astype(F32)
    n_f = HEAD_DIM // 4
    freqs = ROPE_THETA ** (-jnp.arange(n_f, dtype=F32) / n_f)
    ang = jnp.concatenate([r[:, None] * freqs, col[:, None] * freqs], axis=-1)
    ang = jnp.tile(jnp.repeat(ang, 2, axis=1), (1, LANES // HEAD_DIM))
    sign = jnp.where(jnp.arange(LANES) % 2 == 0, -1.0, 1.0).astype(F32)
    return jnp.cos(ang), jnp.sin(ang) * sign


def kernel(x, c, ctx, c_ctx, w_mod, b_mod, norm1_w, norm2_w, w_in, q_norm_w, k_norm_w, ret_decay_fwd,
           ret_decay_bwd, w_out, router_w, router_bias, exp_w_gate, exp_w_up, exp_w_down, sh_w_gate,
           sh_w_up, sh_w_down):
    b, l, d = x.shape
    lc = ctx.shape[1]
    t = b * l
    assert w_mod.shape[0] == 1, "single layer"
    assert l % CHUNK == 0 and lc % CHUNK == 0 and l % GRID_W == 0

    rows = (b + 1 + 7) // 8 * 8
    cc = jnp.zeros((rows, d), F32).at[:b].set(c).at[b].set(c_ctx)
    mod = _modulation(cc, w_mod[0], b_mod[0])
    sh1, sc1, g1, sh2, sc2, g2 = [mod[:b, i * d:(i + 1) * d].reshape(b, 1, d) for i in range(6)]
    shc = mod[b, 0:d].reshape(1, 1, d)
    scc = mod[b, d:2 * d].reshape(1, 1, d)

    wi = w_in[0].astype(BF16)
    qnw = jnp.tile(q_norm_w[0], LANES // HEAD_DIM).reshape(1, LANES)
    knw = jnp.tile(k_norm_w[0], LANES // HEAD_DIM).reshape(1, LANES)
    cos, sin = _rope_tables(l)
    n1 = norm1_w[0].reshape(1, d)

    cklo, ckhi, cvlo, cvhi, crk, crv = _projection(
        ctx, shc, scc, n1, wi, qnw, knw, cos[:lc], sin[:lc], rope=False, with_q=False, tm=TILE_PROJ)
    klo, khi, vlo, vhi, rk, rv, q, rq, sg = _projection(
        x, sh1, sc1, n1, wi, qnw, knw, cos, sin, rope=True, with_q=True, tm=TILE_PROJ)

    bound = (HEAD_DIM * QK_SCALE * LOG2_E * BOUND_MARGIN
             * jnp.max(jnp.abs(q_norm_w[0])) * jnp.max(jnp.abs(k_norm_w[0]))).astype(F32)
    dec_f = jnp.repeat(ret_decay_fwd[0].astype(F32), HEAD_DIM).reshape(1, RET_W)
    dec_b = jnp.repeat(ret_decay_bwd[0].astype(F32), HEAD_DIM).reshape(1, RET_W)
    st, m_tab, xi_tab = _retention_states(rk, rv, crk, crv, dec_f, dec_b)
    attn_args = (bound.reshape(1), q, klo, khi, vlo, vhi, cklo, ckhi, cvlo, cvhi, rq, rk, rv, sg, st, m_tab, xi_tab)
    attn, ret = lax.cond(bound <= MAX_STREAM_SHIFT,
                         functools.partial(_attention, tq=TILE_ATTN_Q, streaming=True),
                         functools.partial(_attention, tq=TILE_ATTN_Q, streaming=False), *attn_args)

    wo = w_out[0].astype(BF16)
    r_hi, r_lo = _split(router_w[0].T)
    x1, hp, logits_t = _out_projection(attn, ret, x, wo[:ATTN_W], wo[ATTN_W:], g1, sh2, sc2,
                                       norm2_w[0].reshape(1, d), r_hi, r_lo, tm=TILE_OUT)

    idx_t, w_t, rank_t, cnt_col, cnt_row = _route(logits_t, router_bias[0].reshape(N_EXPERTS, 1), tb=TILE_TOKENS)
    n_blocks = -(-(t * TOP_K) // MOE_BLOCK) + N_EXPERTS
    tb = TILE_TOKENS
    dest, meta, items = _destinations(idx_t, rank_t, cnt_col, cnt_row, tb=tb, n_blocks=n_blocks)
    steps, _, tbe = dest.shape
    dest_win = dest.reshape(steps, TOP_K, tbe // SC_WINDOW, SC_WINDOW).transpose(0, 2, 1, 3)
    dest_win = dest_win.reshape(t // SC_WINDOW, TOP_K, SC_WINDOW)
    xs = _sc_scatter_rows(hp, dest_win, (n_blocks + ITEM_BLOCKS - 1) * MOE_BLOCK)
    xs = _pad_fill(meta, xs)
    ys = _experts(items, xs, exp_w_gate[0], exp_w_up[0], exp_w_down[0], n_blocks=n_blocks)
    parts = COMBINE_PARTS if steps % COMBINE_PARTS == 0 else 1
    steps_part = steps // parts
    x1f, w_tok = x1.reshape(t, d), w_t.T
    sgw, suw, sdw = sh_w_gate[0].astype(BF16), sh_w_up[0].astype(BF16), sh_w_down[0].astype(BF16)
    out = None
    for p in range(parts):
        idx = dest[p * steps_part:(p + 1) * steps_part].reshape(-1)
        yg = _sc_gather_rows(ys, idx).reshape(steps_part, TOP_K, tbe, d // 2)
        out = _combine(hp, x1f, g2, w_tok, sgw, suw, sdw, yg, out, tb=tbe, seq_len=l, first_step=p * steps_part)
    return out.reshape(b, l, d)
```

```python
import functools

import jax
import jax.numpy as jnp
from jax import lax
from jax.experimental import pallas as pl
from jax.experimental.pallas import tpu as pltpu
from jax.experimental.pallas import tpu_sc as plsc

F32 = jnp.float32
BF16 = jnp.bfloat16
I32 = jnp.int32
U32 = jnp.uint32
PACKED = jnp.int32

HEAD_DIM = 64
LANES = 128
SUBLANES = 8
ATTN_HEADS = 8
ATTN_KV_HEADS = 2
GQA = ATTN_HEADS // ATTN_KV_HEADS
RET_HEADS = 8
ATTN_W = ATTN_HEADS * HEAD_DIM
KV_W = ATTN_KV_HEADS * HEAD_DIM
RET_W = RET_HEADS * HEAD_DIM
RET_PAIRS = RET_W // LANES
CHUNK = 128
GRID_W = 64
ROPE_THETA = 10000.0
N_EXPERTS = 256
TOP_K = 8
N_GROUPS = 8
GROUP_SIZE = N_EXPERTS // N_GROUPS
TOPK_GROUPS = 4
ROUTED_SCALE = 2.5
MOE_BLOCK = 128
EPS = 1e-6
QK_SCALE = HEAD_DIM ** -0.5
LOG2_E = 1.4426950408889634
KEY_TILE = 256
BOUND_MARGIN = 1.02
MAX_STREAM_SHIFT = 56.0

OFF_AK = 0
OFF_AV = OFF_AK + KV_W
OFF_RK = OFF_AV + KV_W
OFF_RV = OFF_RK + RET_W
CTX_KV_COLS = OFF_RV + RET_W
OFF_AQ = CTX_KV_COLS
OFF_RQ = OFF_AQ + ATTN_W
OFF_RG = OFF_RQ + RET_W
IN_COLS = OFF_RG + RET_W

V7X_VMEM_BYTES = 64 * 1024 * 1024
VMEM_LIMIT = V7X_VMEM_BYTES * 13 // 16

SUB_ROWS = 256
TILE_PROJ = 1024
TILE_ATTN_Q = 1024
TILE_OUT = 1024
TILE_TOKENS = 512
ITEM_BLOCKS = 5
COMBINE_PARTS = 4
SC_GATHER_BUFS = 2
WEIGHT_SLOTS = 3
SC_WINDOW = 64
HI_MASK = 0xFFFF0000


def _cparams(*sem):
    return pltpu.CompilerParams(dimension_semantics=sem, vmem_limit_bytes=VMEM_LIMIT)


def _split(a):
    hi = a.astype(BF16)
    lo = (a - hi.astype(F32)).astype(BF16)
    return hi, lo


def _dot(a, b):
    return jnp.dot(a, b, preferred_element_type=F32)


def _dot_nt(a, b):
    return lax.dot_general(a, b, (((1,), (1,)), ((), ())), preferred_element_type=F32)


def _sigmoid(v):
    return 1.0 / (1.0 + jnp.exp(-v))


def _silu(v):
    return v * _sigmoid(v)


def _pack_halves(a, b):
    ua = lax.bitcast_convert_type(a.astype(BF16).astype(F32), U32)
    ub = lax.bitcast_convert_type(b.astype(BF16).astype(F32), U32)
    return lax.bitcast_convert_type((ua & jnp.uint32(HI_MASK)) | (ub >> 16), PACKED)


def _unpack_halves(p):
    u = lax.bitcast_convert_type(p, U32)
    a = lax.bitcast_convert_type(u & jnp.uint32(HI_MASK), F32)
    b = lax.bitcast_convert_type(u << 16, F32)
    return a, b


def _mod_kernel(c_ref, w_ref, b_ref, o_ref):
    s_hi, s_lo = _split(_silu(c_ref[...]))
    w_hi, w_lo = _split(w_ref[...])
    o_ref[...] = _dot(s_hi, w_hi) + _dot(s_hi, w_lo) + _dot(s_lo, w_hi) + b_ref[...]


def _modulation(cc, w_mod, b_mod):
    rows, d = cc.shape
    n = w_mod.shape[1]
    tn = 768
    return pl.pallas_call(
        _mod_kernel,
        grid=(n // tn,),
        in_specs=[pl.BlockSpec((rows, d), lambda j: (0, 0)),
                  pl.BlockSpec((d, tn), lambda j: (0, j)),
                  pl.BlockSpec((1, tn), lambda j: (0, j))],
        out_specs=pl.BlockSpec((rows, tn), lambda j: (0, j)),
        out_shape=jax.ShapeDtypeStruct((rows, n), F32),
        compiler_params=_cparams("arbitrary"),
        name="mod",
    )(cc, w_mod, b_mod.reshape(1, n))


def _segment_ones():
    r = lax.broadcasted_iota(I32, (LANES, LANES), 0) // HEAD_DIM
    c = lax.broadcasted_iota(I32, (LANES, LANES), 1) // HEAD_DIM
    return jnp.where(r == c, 1.0, 0.0).astype(BF16)


def _head_mean_sq(v, seg):
    hi, lo = _split(v * v)
    return (_dot(hi, seg) + _dot(lo, seg)) * (1.0 / HEAD_DIM)


def _proj_kernel(x_ref, sh_ref, sc_ref, nw_ref, wi_ref, qnw_ref, knw_ref, cos_ref, sin_ref,
                 *out_refs, rope, with_q):
    if with_q:
        klo_ref, khi_ref, vlo_ref, vhi_ref, rk_ref, rv_ref, q_ref, rq_ref, sg_ref = out_refs
    else:
        klo_ref, khi_ref, vlo_ref, vhi_ref, rk_ref, rv_ref = out_refs
    tm = x_ref.shape[1]
    sub = min(tm, SUB_ROWS)
    seg = _segment_ones()
    lane = lax.broadcasted_iota(I32, (sub, LANES), 1)
    low_half = lane < HEAD_DIM
    even = (lane & 1) == 0

    for r0 in range(0, tm, sub):
        rows = slice(r0, r0 + sub)
        x = x_ref[0, rows, :]
        h = x * lax.rsqrt(jnp.mean(x * x, axis=-1, keepdims=True) + EPS) * nw_ref[...]
        h = h * (1.0 + sc_ref[0]) + sh_ref[0]
        z = _dot(h.astype(BF16), wi_ref[...])

        def norm_rope(v, w128, rows=rows):
            v = v * lax.rsqrt(_head_mean_sq(v, seg) + EPS) * w128
            if rope:
                swapped = jnp.where(even, pltpu.roll(v, LANES - 1, 1), pltpu.roll(v, 1, 1))
                v = v * cos_ref[rows, :] + swapped * sin_ref[rows, :]
            return v

        k = norm_rope(z[:, OFF_AK:OFF_AK + KV_W], knw_ref[...])
        ksw = pltpu.roll(k, HEAD_DIM, 1)
        klo_ref[0, 0, :, rows] = jnp.transpose(jnp.where(low_half, k, 0.0)).astype(BF16)
        khi_ref[0, 0, :, rows] = jnp.transpose(jnp.where(low_half, 0.0, ksw)).astype(BF16)
        klo_ref[0, 1, :, rows] = jnp.transpose(jnp.where(low_half, ksw, 0.0)).astype(BF16)
        khi_ref[0, 1, :, rows] = jnp.transpose(jnp.where(low_half, 0.0, k)).astype(BF16)
        v = z[:, OFF_AV:OFF_AV + KV_W]
        vsw = pltpu.roll(v, HEAD_DIM, 1)
        vlo_ref[0, 0, rows, :] = jnp.where(low_half, v, 0.0).astype(BF16)
        vhi_ref[0, 0, rows, :] = jnp.where(low_half, 0.0, vsw).astype(BF16)
        vlo_ref[0, 1, rows, :] = jnp.where(low_half, vsw, 0.0).astype(BF16)
        vhi_ref[0, 1, rows, :] = jnp.where(low_half, 0.0, v).astype(BF16)
        rk_ref[0, rows, :] = (z[:, OFF_RK:OFF_RK + RET_W] * QK_SCALE).astype(BF16)
        rv_ref[0, rows, :] = z[:, OFF_RV:OFF_RV + RET_W].astype(BF16)
        if with_q:
            for j in range(ATTN_W // LANES):
                qj = norm_rope(z[:, OFF_AQ + j * LANES:OFF_AQ + (j + 1) * LANES], qnw_ref[...])
                q_ref[0, rows, j * LANES:(j + 1) * LANES] = (qj * (QK_SCALE * LOG2_E)).astype(BF16)
            rq_ref[0, rows, :] = z[:, OFF_RQ:OFF_RQ + RET_W].astype(BF16)
            sg_ref[0, rows, :] = _silu(z[:, OFF_RG:OFF_RG + RET_W]).astype(BF16)


def _projection(x, shift, scale, norm_w, wi_bf16, qnw, knw, cos, sin, *, rope, with_q, tm):
    b, l, d = x.shape
    tm = min(tm, l)
    ncols = IN_COLS if with_q else CTX_KV_COLS
    per_batch = shift.shape[0] > 1
    mod_idx = (lambda bi, i: (bi, 0, 0)) if per_batch else (lambda bi, i: (0, 0, 0))
    kv_shape = jax.ShapeDtypeStruct((b, ATTN_KV_HEADS, l, LANES), BF16)
    kv_spec = pl.BlockSpec((1, ATTN_KV_HEADS, tm, LANES), lambda bi, i: (bi, 0, i, 0))
    kt_shape = jax.ShapeDtypeStruct((b, ATTN_KV_HEADS, LANES, l), BF16)
    kt_spec = pl.BlockSpec((1, ATTN_KV_HEADS, LANES, tm), lambda bi, i: (bi, 0, 0, i))
    w_shape = jax.ShapeDtypeStruct((b, l, RET_W), BF16)
    w_spec = pl.BlockSpec((1, tm, RET_W), lambda bi, i: (bi, i, 0))
    out_shape = [kt_shape] * 2 + [kv_shape] * 2 + [w_shape] * 2
    out_specs = [kt_spec] * 2 + [kv_spec] * 2 + [w_spec] * 2
    if with_q:
        out_shape += [w_shape] * 3
        out_specs += [w_spec] * 3
    return pl.pallas_call(
        functools.partial(_proj_kernel, rope=rope, with_q=with_q),
        grid=(b, l // tm),
        in_specs=[pl.BlockSpec((1, tm, d), lambda bi, i: (bi, i, 0)),
                  pl.BlockSpec((1, 1, d), mod_idx),
                  pl.BlockSpec((1, 1, d), mod_idx),
                  pl.BlockSpec((1, d), lambda bi, i: (0, 0)),
                  pl.BlockSpec((d, ncols), lambda bi, i: (0, 0)),
                  pl.BlockSpec((1, LANES), lambda bi, i: (0, 0)),
                  pl.BlockSpec((1, LANES), lambda bi, i: (0, 0)),
                  pl.BlockSpec((tm, LANES), lambda bi, i: (i, 0)),
                  pl.BlockSpec((tm, LANES), lambda bi, i: (i, 0))],
        out_specs=out_specs,
        out_shape=out_shape,
        compiler_params=_cparams("arbitrary", "arbitrary"),
        name="proj_latent" if with_q else "proj_ctx",
    )(x, shift, scale, norm_w, wi_bf16, qnw, knw, cos, sin)


def _attn_kernel(shift_ref, q_ref, klo_ref, khi_ref, vlo_ref, vhi_ref, cklo_ref, ckhi_ref, cvlo_ref, cvhi_ref,
                 o_ref, kl_s, kh_s, va_s, *, l, lc, streaming):
    lk = l + lc

    @pl.when(pl.program_id(2) == 0)
    def _():
        kl_s[:, 0:l] = klo_ref[0, 0]
        kl_s[:, l:lk] = cklo_ref[0, 0]
        kh_s[:, 0:l] = khi_ref[0, 0]
        kh_s[:, l:lk] = ckhi_ref[0, 0]
        lane = lax.broadcasted_iota(I32, (lk, LANES), 1)
        ones_lo = jnp.where(lane < HEAD_DIM, 1.0, 0.0).astype(BF16)
        ones_hi = jnp.where(lane < HEAD_DIM, 0.0, 1.0).astype(BF16)
        for g, (v_ref, cv_ref, ones) in enumerate(((vlo_ref, cvlo_ref, ones_lo), (vhi_ref, cvhi_ref, ones_hi),
                                                   (vlo_ref, cvlo_ref, ones_lo), (vhi_ref, cvhi_ref, ones_hi))):
            v_col, one_col = (0, LANES) if g < 2 else (LANES, 0)
            va_s[g, 0:l, v_col:v_col + LANES] = v_ref[0, 0]
            va_s[g, l:lk, v_col:v_col + LANES] = cv_ref[0, 0]
            va_s[g, :, one_col:one_col + LANES] = ones

    q = q_ref[0]
    acc = []
    for g in range(GQA):
        qp = q[:, (g // 2) * LANES:(g // 2 + 1) * LANES]
        k_s = kl_s if g % 2 == 0 else kh_s
        if streaming:
            shift = shift_ref[0]
            o = None
            for c in range(0, lk, KEY_TILE):
                hi = min(c + KEY_TILE, lk)
                p = jnp.exp2(_dot(qp, k_s[:, c:hi]) - shift).astype(BF16)
                t = _dot(p, va_s[g, c:hi])
                o = t if o is None else o + t
        else:
            s = _dot(qp, k_s[...])
            p = jnp.exp2(s - jnp.max(s, axis=-1, keepdims=True)).astype(BF16)
            o = _dot(p, va_s[g])
        acc.append(o)
    out_a = acc[0] + acc[1]
    out_b = acc[2] + acc[3]
    o_ref[0, :, 0:LANES] = (out_a[:, 0:LANES] / out_a[:, LANES:2 * LANES]).astype(BF16)
    o_ref[0, :, LANES:2 * LANES] = (out_b[:, LANES:2 * LANES] / out_b[:, 0:LANES]).astype(BF16)


def _attention(shift, q, klo, khi, vlo, vhi, cklo, ckhi, cvlo, cvhi, *, tq, streaming):
    b, l, _ = q.shape
    lc = cvlo.shape[2]
    lk = l + lc
    tq = min(tq, l)
    gw = GQA * HEAD_DIM
    kt_spec = pl.BlockSpec((1, 1, LANES, l), lambda bi, h, i: (bi, h, 0, 0))
    kv_spec = pl.BlockSpec((1, 1, l, LANES), lambda bi, h, i: (bi, h, 0, 0))
    ckt_spec = pl.BlockSpec((1, 1, LANES, lc), lambda bi, h, i: (bi, h, 0, 0))
    ckv_spec = pl.BlockSpec((1, 1, lc, LANES), lambda bi, h, i: (bi, h, 0, 0))
    return pl.pallas_call(
        functools.partial(_attn_kernel, l=l, lc=lc, streaming=streaming),
        grid=(b, ATTN_KV_HEADS, l // tq),
        in_specs=([pl.BlockSpec(memory_space=pltpu.SMEM),
                   pl.BlockSpec((1, tq, gw), lambda bi, h, i: (bi, i, h))] + [kt_spec] * 2 + [kv_spec] * 2
                  + [ckt_spec] * 2 + [ckv_spec] * 2),
        out_specs=pl.BlockSpec((1, tq, gw), lambda bi, h, i: (bi, i, h)),
        out_shape=jax.ShapeDtypeStruct((b, l, ATTN_W), BF16),
        scratch_shapes=[pltpu.VMEM((LANES, lk), BF16), pltpu.VMEM((LANES, lk), BF16),
                        pltpu.VMEM((GQA, lk, 2 * LANES), BF16)],
        compiler_params=_cparams("arbitrary", "arbitrary", "arbitrary"),
        name="attn_stream" if streaming else "attn",
    )(shift, q, klo, khi, vlo, vhi, cklo, ckhi, cvlo, cvhi)


def _log_sigmoid(v):
    return jnp.minimum(v, 0.0) - jnp.log(1.0 + jnp.exp(-jnp.abs(v)))


def _ret_kernel(rq_ref, rk_ref, rv_ref, sg_ref, crk_ref, crv_ref, df_ref, db_ref, o_ref,
                m_s, xi_s, zeta_s, kv_s, st_s, *, l, lc):
    n = l // CHUNK
    nc = lc // CHUNK
    lgf = _log_sigmoid(df_ref[...])
    lgb = _log_sigmoid(db_ref[...])
    pos = lax.broadcasted_iota(I32, (CHUNK, LANES), 0).astype(F32)
    row = lax.broadcasted_iota(I32, (CHUNK, CHUNK), 0)
    col = lax.broadcasted_iota(I32, (CHUNK, CHUNK), 1)
    diff = (row - col).astype(F32)
    g_chunk = []
    for p in range(RET_PAIRS):
        cols = slice(p * LANES, (p + 1) * LANES)
        lf, lb = lgf[:, cols], lgb[:, cols]
        xi_s[p, :, 0:LANES] = jnp.exp((pos + 1.0) * lf)
        xi_s[p, :, LANES:] = jnp.exp((CHUNK - pos) * lb)
        zeta_s[p, :, 0:LANES] = jnp.exp((CHUNK - 1.0 - pos) * lf)
        zeta_s[p, :, LANES:] = jnp.exp(pos * lb)
        g_chunk.append((jnp.exp(CHUNK * lf), jnp.exp(CHUNK * lb)))
        for j in range(2):
            h = 2 * p + j
            hf = lgf[:, h * HEAD_DIM:h * HEAD_DIM + 1]
            hb = lgb[:, h * HEAD_DIM:h * HEAD_DIM + 1]
            m_s[p, :, j * CHUNK:(j + 1) * CHUNK] = jnp.where(
                diff > 0, jnp.exp(diff * hf), jnp.where(diff < 0, jnp.exp(-diff * hb), 2.0))

    lane = lax.broadcasted_iota(I32, (CHUNK, LANES), 1)
    low_half = lane < HEAD_DIM
    diag = (lax.broadcasted_iota(I32, (LANES, LANES), 0) // HEAD_DIM
            == lax.broadcasted_iota(I32, (LANES, LANES), 1) // HEAD_DIM)
    seg = jnp.where(diag, 1.0, 0.0).astype(BF16)
    seg2 = jnp.concatenate([seg, seg], axis=0)
    diag2 = jnp.concatenate([diag, diag], axis=0)

    def split_heads(a):
        zero = jnp.zeros_like(a)
        return jnp.concatenate([jnp.where(low_half, a, zero), jnp.where(low_half, zero, a)], axis=0)

    def contrib(k_ref, v_ref, r0, p):
        cols = slice(p * LANES, (p + 1) * LANES)
        kp = k_ref[0, pl.ds(r0, CHUNK), cols].astype(F32)
        kz = jnp.concatenate([kp, kp], axis=1) * zeta_s[p]
        kv = _dot(jnp.transpose(kz).astype(BF16), v_ref[0, pl.ds(r0, CHUNK), cols])
        return jnp.where(diag2, kv, 0.0)

    for c in range(nc):
        for p in range(RET_PAIRS):
            kv_s[c, p] = contrib(crk_ref, crv_ref, c * CHUNK, p)

    def contrib_body(c, carry):
        r0 = pl.multiple_of(c * CHUNK, CHUNK)
        for p in range(RET_PAIRS):
            kv_s[nc + c, p] = contrib(rk_ref, rv_ref, r0, p)
        return carry

    lax.fori_loop(0, n, contrib_body, 0, unroll=4)

    for p in range(RET_PAIRS):
        gf, gb = g_chunk[p]
        sf = jnp.zeros((LANES, LANES), F32)
        sb = jnp.zeros((LANES, LANES), F32)
        for c in range(nc):
            sf = gf * sf + kv_s[c, p, 0:LANES]
            sb = gb * sb + kv_s[nc - 1 - c, p, LANES:]

        def fwd_scan(c, s, p=p, gf=gf):
            st_s[c, p, 0:LANES] = s.astype(BF16)
            return gf * s + kv_s[nc + c, p, 0:LANES]

        def bwd_scan(j, s, p=p, gb=gb):
            c = n - 1 - j
            st_s[c, p, LANES:] = s.astype(BF16)
            return gb * s + kv_s[nc + c, p, LANES:]

        lax.fori_loop(0, n, fwd_scan, sf)
        lax.fori_loop(0, n, bwd_scan, sb)

    def out_body(c, carry):
        r0 = pl.multiple_of(c * CHUNK, CHUNK)
        for p in range(RET_PAIRS):
            cols = slice(p * LANES, (p + 1) * LANES)
            qp = rq_ref[0, pl.ds(r0, CHUNK), cols]
            kp = rk_ref[0, pl.ds(r0, CHUNK), cols]
            vp = rv_ref[0, pl.ds(r0, CHUNK), cols]
            s2 = _dot_nt(qp, split_heads(kp))
            a2 = (s2 * m_s[p]).astype(BF16)
            y = _dot(a2, split_heads(vp))
            qf = qp.astype(F32)
            qx = (jnp.concatenate([qf, qf], axis=1) * xi_s[p]).astype(BF16)
            y += _dot(qx, st_s[c, p])
            hi, lo = _split(y * y)
            ms = _dot(jnp.concatenate([hi, lo], axis=1), seg2) * (1.0 / HEAD_DIM)
            out = y * lax.rsqrt(ms + EPS) * sg_ref[0, pl.ds(r0, CHUNK), cols].astype(F32)
            o_ref[0, pl.ds(r0, CHUNK), cols] = out.astype(BF16)
        return carry

    lax.fori_loop(0, n, out_body, 0, unroll=4)


def _retention(rq, rk, rv, sg, crk, crv, dec_f, dec_b):
    b, l, _ = rq.shape
    lc = crk.shape[1]
    n = l // CHUNK
    nc = lc // CHUNK
    spec = pl.BlockSpec((1, l, RET_W), lambda bi: (bi, 0, 0))
    cspec = pl.BlockSpec((1, lc, RET_W), lambda bi: (bi, 0, 0))
    dspec = pl.BlockSpec((1, RET_W), lambda bi: (0, 0))
    return pl.pallas_call(
        functools.partial(_ret_kernel, l=l, lc=lc),
        grid=(b,),
        in_specs=[spec, spec, spec, spec, cspec, cspec, dspec, dspec],
        out_specs=spec,
        out_shape=jax.ShapeDtypeStruct((b, l, RET_W), BF16),
        scratch_shapes=[pltpu.VMEM((RET_PAIRS, CHUNK, 2 * CHUNK), F32),
                        pltpu.VMEM((RET_PAIRS, CHUNK, 2 * LANES), F32),
                        pltpu.VMEM((RET_PAIRS, CHUNK, 2 * LANES), F32),
                        pltpu.VMEM((nc + n, RET_PAIRS, 2 * LANES, LANES), F32),
                        pltpu.VMEM((n, RET_PAIRS, 2 * LANES, LANES), BF16)],
        compiler_params=_cparams("arbitrary"),
        name="ret",
    )(rq, rk, rv, sg, crk, crv, dec_f, dec_b)


def _out_kernel(attn_ref, ret_ref, x_ref, wa_ref, wr_ref, g1_ref, sh_ref, sc_ref, nw_ref, rhi_ref, rlo_ref,
                x1_ref, hp_ref, lg_ref):
    y = _dot(attn_ref[0], wa_ref[...]) + _dot(ret_ref[0], wr_ref[...])
    x1 = x_ref[0] + g1_ref[0] * y
    x1_ref[0] = x1
    h = x1 * lax.rsqrt(jnp.mean(x1 * x1, axis=-1, keepdims=True) + EPS) * nw_ref[...]
    h = h * (1.0 + sc_ref[0]) + sh_ref[0]
    half = h.shape[1] // 2
    hp_ref[...] = _pack_halves(h[:, :half], h[:, half:])
    h_hi, h_lo = _split(h)
    lg_ref[...] = _dot_nt(rhi_ref[...], h_hi) + _dot_nt(rhi_ref[...], h_lo) + _dot_nt(rlo_ref[...], h_hi)


def _out_projection(attn, ret, x, wa, wr, g1, sh2, sc2, norm_w, r_hi, r_lo, *, tm):
    b, l, d = x.shape
    tm = min(tm, l)
    nt = l // tm
    t = b * l
    mspec = pl.BlockSpec((1, 1, d), lambda bi, i: (bi, 0, 0))
    return pl.pallas_call(
        _out_kernel,
        grid=(b, nt),
        in_specs=[pl.BlockSpec((1, tm, ATTN_W), lambda bi, i: (bi, i, 0)),
                  pl.BlockSpec((1, tm, RET_W), lambda bi, i: (bi, i, 0)),
                  pl.BlockSpec((1, tm, d), lambda bi, i: (bi, i, 0)),
                  pl.BlockSpec((ATTN_W, d), lambda bi, i: (0, 0)),
                  pl.BlockSpec((RET_W, d), lambda bi, i: (0, 0)),
                  mspec, mspec, mspec,
                  pl.BlockSpec((1, d), lambda bi, i: (0, 0)),
                  pl.BlockSpec((N_EXPERTS, d), lambda bi, i: (0, 0)),
                  pl.BlockSpec((N_EXPERTS, d), lambda bi, i: (0, 0))],
        out_specs=[pl.BlockSpec((1, tm, d), lambda bi, i: (bi, i, 0)),
                   pl.BlockSpec((tm, d // 2), lambda bi, i: (bi * nt + i, 0)),
                   pl.BlockSpec((N_EXPERTS, tm), lambda bi, i: (0, bi * nt + i))],
        out_shape=[jax.ShapeDtypeStruct((b, l, d), F32),
                   jax.ShapeDtypeStruct((t, d // 2), PACKED),
                   jax.ShapeDtypeStruct((N_EXPERTS, t), F32)],
        compiler_params=_cparams("arbitrary", "arbitrary"),
        name="out_proj",
    )(attn, ret, x, wa, wr, g1, sh2, sc2, norm_w, r_hi, r_lo)


def _route_kernel(lg_ref, bias_ref, idx_ref, w_ref, rank_ref, cnt_col_ref, cnt_row_ref, tri_s, col_s, row_s):
    tb = lg_ref.shape[1]
    step = pl.program_id(0)

    @pl.when(step == 0)
    def _():
        r = lax.broadcasted_iota(I32, (tb, tb), 0)
        c = lax.broadcasted_iota(I32, (tb, tb), 1)
        tri_s[...] = jnp.where(r <= c, 1.0, 0.0).astype(BF16)
        col_s[...] = jnp.zeros_like(col_s)
        row_s[...] = jnp.zeros_like(row_s)

    scores = _sigmoid(lg_ref[...])
    biased = scores + bias_ref[...]
    neg = -jnp.inf
    sub = lax.broadcasted_iota(I32, (GROUP_SIZE, tb), 0).astype(F32)

    gscore = []
    for g in range(N_GROUPS):
        blk = biased[g * GROUP_SIZE:(g + 1) * GROUP_SIZE]
        m1 = jnp.max(blk, axis=0, keepdims=True)
        first = jnp.min(jnp.where(blk == m1, sub, float(GROUP_SIZE)), axis=0, keepdims=True)
        m2 = jnp.max(jnp.where(sub == first, neg, blk), axis=0, keepdims=True)
        gscore.append(m1 + m2)
    gs = jnp.concatenate(gscore, axis=0)
    gsub = lax.broadcasted_iota(I32, (N_GROUPS, tb), 0).astype(F32)
    keep = jnp.zeros((N_GROUPS, tb), F32)
    for _ in range(TOPK_GROUPS):
        m = jnp.max(gs, axis=0, keepdims=True)
        first = jnp.min(jnp.where(gs == m, gsub, float(N_GROUPS)), axis=0, keepdims=True)
        sel = gsub == first
        keep = jnp.where(sel, 1.0, keep)
        gs = jnp.where(sel, neg, gs)
    masked = jnp.concatenate(
        [jnp.where(keep[g:g + 1] > 0.0, biased[g * GROUP_SIZE:(g + 1) * GROUP_SIZE], neg)
         for g in range(N_GROUPS)], axis=0)

    esub = lax.broadcasted_iota(I32, (N_EXPERTS, tb), 0).astype(F32)
    sels, idxs, ws = [], [], []
    chosen = jnp.zeros((N_EXPERTS, tb), F32)
    for _ in range(TOP_K):
        m = jnp.max(masked, axis=0, keepdims=True)
        first = jnp.min(jnp.where(masked == m, esub, float(N_EXPERTS)), axis=0, keepdims=True)
        sel = esub == first
        sels.append(sel)
        idxs.append(first)
        ws.append(jnp.sum(jnp.where(sel, scores, 0.0), axis=0, keepdims=True))
        chosen = jnp.where(sel, 1.0, chosen)
        masked = jnp.where(sel, neg, masked)
    wsum = ws[0]
    for k in range(1, TOP_K):
        wsum = wsum + ws[k]
    idx_ref[...] = jnp.concatenate(idxs, axis=0).astype(I32)
    w_ref[...] = jnp.concatenate([wk / wsum * ROUTED_SCALE for wk in ws], axis=0)

    chosen_b = chosen.astype(BF16)
    incl = _dot(chosen_b, tri_s[...])
    before = incl - chosen + col_s[...]
    rank_ref[...] = jnp.concatenate(
        [jnp.sum(jnp.where(sel, before, 0.0), axis=0, keepdims=True) for sel in sels], axis=0).astype(I32)
    col_s[...] = col_s[...] + incl[:, tb - 1:tb]
    row_s[...] = row_s[...] + _dot_nt(jnp.ones((8, tb), BF16), chosen_b)
    cnt_col_ref[...] = col_s[...].astype(I32)
    cnt_row_ref[...] = row_s[...].astype(I32)


def _route(logits_t, bias_col, *, tb):
    e, t = logits_t.shape
    tb = min(tb, t)
    kspec = pl.BlockSpec((TOP_K, tb), lambda i: (0, i))
    return pl.pallas_call(
        _route_kernel,
        grid=(t // tb,),
        in_specs=[pl.BlockSpec((e, tb), lambda i: (0, i)),
                  pl.BlockSpec((e, 1), lambda i: (0, 0))],
        out_specs=[kspec, kspec, kspec,
                   pl.BlockSpec((e, 1), lambda i: (0, 0)),
                   pl.BlockSpec((8, e), lambda i: (0, 0))],
        out_shape=[jax.ShapeDtypeStruct((TOP_K, t), I32),
                   jax.ShapeDtypeStruct((TOP_K, t), F32),
                   jax.ShapeDtypeStruct((TOP_K, t), I32),
                   jax.ShapeDtypeStruct((e, 1), I32),
                   jax.ShapeDtypeStruct((8, e), I32)],
        scratch_shapes=[pltpu.VMEM((tb, tb), BF16), pltpu.VMEM((e, 1), F32), pltpu.VMEM((8, e), F32)],
        compiler_params=_cparams("arbitrary"),
        name="route",
    )(logits_t, bias_col)


def _pad_block(cnt):
    return (cnt + (MOE_BLOCK - 1)) // MOE_BLOCK * MOE_BLOCK


def _max_items(n_blocks):
    return n_blocks // ITEM_BLOCKS + N_EXPERTS


def _dest_kernel(idx_ref, rank_ref, cnt_col_ref, cnt_row_ref, dest_ref, meta_ref, items_ref, start_s):
    tb = idx_ref.shape[1]
    nip = items_ref.shape[1]

    @pl.when(pl.program_id(0) == 0)
    def _():
        pad_col = _pad_block(cnt_col_ref[...])
        pad_row = _pad_block(cnt_row_ref[0:1, :])
        er = lax.broadcasted_iota(I32, (N_EXPERTS, N_EXPERTS), 0)
        ec = lax.broadcasted_iota(I32, (N_EXPERTS, N_EXPERTS), 1)
        start_col = jnp.sum(jnp.where(ec < er, pad_row, 0), axis=1, keepdims=True)
        start_row = jnp.sum(jnp.where(er < ec, pad_col, 0), axis=0, keepdims=True)
        start_s[...] = start_col

        used = jnp.sum(pad_row, axis=1, keepdims=True) // MOE_BLOCK
        meta_ref[...] = jnp.concatenate(
            [cnt_row_ref[0:1, :], start_row, pad_row, jnp.broadcast_to(used, (1, N_EXPERTS)),
             jnp.zeros((4, N_EXPERTS), I32)], axis=0)

        nb_col = pad_col // MOE_BLOCK
        it_col = (nb_col + (ITEM_BLOCKS - 1)) // ITEM_BLOCKS
        it_row = (pad_row // MOE_BLOCK + (ITEM_BLOCKS - 1)) // ITEM_BLOCKS
        it_start = jnp.sum(jnp.where(ec < er, it_row, 0), axis=1, keepdims=True)
        n_items = jnp.sum(it_row, axis=1, keepdims=True)
        lane = lax.broadcasted_iota(I32, (1, nip), 1)
        owner = jnp.sum(jnp.where(it_start + it_col <= lane, 1, 0), axis=0, keepdims=True)
        owner = jnp.minimum(owner, N_EXPERTS - 1)
        onehot = lax.broadcasted_iota(I32, (N_EXPERTS, nip), 0) == owner

        def pick(col):
            return jnp.sum(jnp.where(onehot, col, 0), axis=0, keepdims=True)

        j = lane - pick(it_start)
        block0 = pick(start_col) // MOE_BLOCK + ITEM_BLOCKS * j
        nvalid = jnp.clip(pick(nb_col) - ITEM_BLOCKS * j, 0, ITEM_BLOCKS)
        items_ref[...] = jnp.concatenate(
            [owner, block0, jnp.where(lane < n_items, nvalid, 0), jnp.broadcast_to(n_items, (1, nip)),
             jnp.zeros((4, nip), I32)], axis=0)

    start_col = start_s[...]
    esub = lax.broadcasted_iota(I32, (N_EXPERTS, tb), 0)
    rows = []
    for k in range(TOP_K):
        onehot = esub == idx_ref[k:k + 1, :]
        rows.append(jnp.sum(jnp.where(onehot, start_col, 0), axis=0, keepdims=True) + rank_ref[k:k + 1, :])
    dest_ref[0] = jnp.concatenate(rows, axis=0)


def _destinations(idx_t, rank_t, cnt_col, cnt_row, *, tb, n_blocks):
    _, t = idx_t.shape
    tb = min(tb, t)
    nip = (_max_items(n_blocks) + LANES - 1) // LANES * LANES
    kspec = pl.BlockSpec((TOP_K, tb), lambda i: (0, i))
    return pl.pallas_call(
        _dest_kernel,
        grid=(t // tb,),
        in_specs=[kspec, kspec,
                  pl.BlockSpec((N_EXPERTS, 1), lambda i: (0, 0)),
                  pl.BlockSpec((8, N_EXPERTS), lambda i: (0, 0))],
        out_specs=[pl.BlockSpec((1, TOP_K, tb), lambda i: (i, 0, 0)),
                   pl.BlockSpec((8, N_EXPERTS), lambda i: (0, 0)),
                   pl.BlockSpec((8, nip), lambda i: (0, 0))],
        out_shape=[jax.ShapeDtypeStruct((t // tb, TOP_K, tb), I32),
                   jax.ShapeDtypeStruct((8, N_EXPERTS), I32),
                   jax.ShapeDtypeStruct((8, nip), I32)],
        scratch_shapes=[pltpu.VMEM((N_EXPERTS, 1), I32)],
        compiler_params=_cparams("arbitrary"),
        name="dest",
    )(idx_t, rank_t, cnt_col, cnt_row)


_PAD_BITS = (64, 32, 16, 8)


def _sc_scatter_rows(rows, dest_win, n_out_rows):
    n_win, n_slots, win = dest_win.shape
    width = rows.shape[1]
    info = plsc.get_sparse_core_info()
    n_workers = info.num_cores * info.num_subcores
    per_worker = n_win // n_workers
    assert per_worker * n_workers == n_win and win <= LANES
    mesh = plsc.VectorSubcoreMesh(core_axis_name="c", subcore_axis_name="s")

    def body(rows_hbm, dest_hbm, out_hbm, idx_v, rows_v, sem):
        wid = lax.axis_index("s") * info.num_cores + lax.axis_index("c")

        @pl.loop(0, per_worker)
        def _(j):
            w = wid * per_worker + j
            pltpu.sync_copy(dest_hbm.at[w], idx_v)
            pltpu.sync_copy(rows_hbm.at[pl.ds(w * win, win)], rows_v)
            copies = [pltpu.async_copy(rows_v, out_hbm.at[idx_v.at[k]], sem) for k in range(n_slots)]
            for cp in copies:
                cp.wait()

    return pl.kernel(
        body,
        out_type=jax.ShapeDtypeStruct((n_out_rows, width), rows.dtype),
        mesh=mesh,
        scratch_types=[pltpu.VMEM((n_slots, win), I32), pltpu.VMEM((win, width), rows.dtype),
                       pltpu.SemaphoreType.DMA],
        name="sc_scatter",
    )(rows, dest_win)


def _pad_fill_kernel(meta_ref, xs_in, xs_hbm, zero_s, sem_z, *, e_per_step):
    del xs_in
    step = pl.program_id(0)
    zero_s[...] = jnp.zeros_like(zero_s)

    def tail_copy(c):
        row0 = pl.multiple_of((meta_ref[3, 0] + c) * MOE_BLOCK, MOE_BLOCK)
        return pltpu.make_async_copy(zero_s, xs_hbm.at[pl.ds(row0, MOE_BLOCK)], sem_z)

    @pl.when(step == 0)
    def _():
        for c in range(ITEM_BLOCKS - 1):
            tail_copy(c).start()
        for c in range(ITEM_BLOCKS - 1):
            tail_copy(c).wait()

    def pad_copies(e):
        cnt = meta_ref[0, e]
        off = meta_ref[1, e] + cnt
        rem = meta_ref[2, e] - cnt
        head = rem & (SUBLANES - 1)
        out = []
        for i in range(SUBLANES - 1):
            out.append((i < head,
                        pltpu.make_async_copy(zero_s.at[pl.ds(0, 1)], xs_hbm.at[pl.ds(off + i, 1)], sem_z)))
        off = off + head
        for bit in _PAD_BITS:
            out.append(((rem & bit) != 0,
                        pltpu.make_async_copy(zero_s.at[pl.ds(0, bit)],
                                              xs_hbm.at[pl.ds(pl.multiple_of(off, SUBLANES), bit)], sem_z)))
            off = off + (rem & bit)
        return out

    for j in range(e_per_step):
        for cond, c in pad_copies(step * e_per_step + j):
            pl.when(cond)(c.start)
    for j in range(e_per_step):
        for cond, c in pad_copies(step * e_per_step + j):
            pl.when(cond)(c.wait)


PAD_FILL_EXPERTS_PER_STEP = 8


def _pad_fill(meta, xs):
    half = xs.shape[1]
    return pl.pallas_call(
        functools.partial(_pad_fill_kernel, e_per_step=PAD_FILL_EXPERTS_PER_STEP),
        grid=(N_EXPERTS // PAD_FILL_EXPERTS_PER_STEP,),
        in_specs=[pl.BlockSpec(memory_space=pltpu.SMEM),
                  pl.BlockSpec(memory_space=pl.ANY)],
        out_specs=pl.BlockSpec(memory_space=pl.ANY),
        out_shape=jax.ShapeDtypeStruct(xs.shape, xs.dtype),
        input_output_aliases={1: 0},
        scratch_shapes=[pltpu.VMEM((MOE_BLOCK, half), PACKED), pltpu.SemaphoreType.DMA],
        compiler_params=_cparams("arbitrary"),
        name="pad_fill",
    )(meta, xs)


def _experts_kernel(items_ref, xs_hbm, wg_hbm, wu_hbm, wd_hbm, ys_hbm,
                    xbuf, ybuf, wg_f, wu_f, wd_f, wg_s, wu_s, wd_s, sem_x, sem_y, sem_w):
    n_items = items_ref[3, 0]
    rows = ITEM_BLOCKS * MOE_BLOCK

    def w_copies(e, s):
        return [pltpu.make_async_copy(src.at[e], dst.at[s], sem_w.at[s])
                for src, dst in ((wg_hbm, wg_f), (wu_hbm, wu_f), (wd_hbm, wd_f))]

    def x_copy(item, s):
        row0 = pl.multiple_of(items_ref[1, item] * MOE_BLOCK, MOE_BLOCK)
        return pltpu.make_async_copy(xs_hbm.at[pl.ds(row0, rows)], xbuf.at[s], sem_x.at[s])

    def y_copies(item, s, fn):
        for k in range(ITEM_BLOCKS):
            row0 = pl.multiple_of((items_ref[1, item] + k) * MOE_BLOCK, MOE_BLOCK)
            cp = pltpu.make_async_copy(ybuf.at[s, pl.ds(k * MOE_BLOCK, MOE_BLOCK)],
                                       ys_hbm.at[pl.ds(row0, MOE_BLOCK)], sem_y.at[s])
            pl.when(k < items_ref[2, item])(functools.partial(fn, cp))

    def expert_of(item):
        return items_ref[0, jnp.minimum(item, n_items - 1)]

    def changes_at(item):
        return ((item < n_items) & (expert_of(item) != expert_of(item - 1))).astype(I32)

    @pl.when(n_items > 0)
    def _():
        x_copy(0, 0).start()
        for cp in w_copies(expert_of(0), 0):
            cp.start()

        @pl.when(changes_at(1) == 1)
        def _():
            for cp in w_copies(expert_of(1), 1):
                cp.start()

    def item_body(i, ordinal):
        slot = i % 2
        prev = jnp.maximum(i - 1, 0)
        e = expert_of(i)
        new_expert = (i == 0) | (e != expert_of(prev))
        c1 = changes_at(i + 1)
        c2 = changes_at(i + 2)

        @pl.when(i + 1 < n_items)
        def _():
            x_copy(i + 1, 1 - slot).start()

        @pl.when(new_expert)
        def _():
            wslot = ordinal % WEIGHT_SLOTS
            for cp in w_copies(e, wslot):
                cp.wait()
            wg_s[...] = wg_f[wslot].astype(BF16)
            wu_s[...] = wu_f[wslot].astype(BF16)
            wd_s[...] = wd_f[wslot].astype(BF16)

        @pl.when(c2 == 1)
        def _():
            for cp in w_copies(expert_of(i + 2), (ordinal + c1 + 1) % WEIGHT_SLOTS):
                cp.start()

        x_copy(i, slot).wait()
        xa, xb = _unpack_halves(xbuf[slot])
        xa = xa.astype(BF16)
        xb = xb.astype(BF16)
        half = xa.shape[1]
        g = _dot(xa, wg_s[0:half]) + _dot(xb, wg_s[half:])
        u = _dot(xa, wu_s[0:half]) + _dot(xb, wu_s[half:])
        y = _dot((_silu(g) * u).astype(BF16), wd_s[...])
        ybuf[slot] = _pack_halves(y[:, :half], y[:, half:])
        y_copies(i, slot, lambda cp: cp.start())

        @pl.when(i > 0)
        def _():
            y_copies(prev, 1 - slot, lambda cp: cp.wait())

        return ordinal + c1

    lax.fori_loop(0, n_items, item_body, jnp.int32(0))

    @pl.when(n_items > 0)
    def _():
        last = n_items - 1
        y_copies(last, last % 2, lambda cp: cp.wait())


def _experts(items, xs, w_gate, w_up, w_down, *, n_blocks):
    half = xs.shape[1]
    e, d, f = w_gate.shape
    rows = ITEM_BLOCKS * MOE_BLOCK
    any_spec = pl.BlockSpec(memory_space=pl.ANY)
    return pl.pallas_call(
        _experts_kernel,
        grid_spec=pltpu.PrefetchScalarGridSpec(
            num_scalar_prefetch=1,
            grid=(1,),
            in_specs=[any_spec, any_spec, any_spec, any_spec],
            out_specs=any_spec,
            scratch_shapes=[pltpu.VMEM((2, rows, half), PACKED), pltpu.VMEM((2, rows, half), PACKED),
                            pltpu.VMEM((WEIGHT_SLOTS, d, f), F32), pltpu.VMEM((WEIGHT_SLOTS, d, f), F32),
                            pltpu.VMEM((WEIGHT_SLOTS, f, d), F32),
                            pltpu.VMEM((d, f), BF16), pltpu.VMEM((d, f), BF16), pltpu.VMEM((f, d), BF16),
                            pltpu.SemaphoreType.DMA((2,)), pltpu.SemaphoreType.DMA((2,)),
                            pltpu.SemaphoreType.DMA((WEIGHT_SLOTS,))]),
        out_shape=jax.ShapeDtypeStruct((n_blocks * MOE_BLOCK, half), PACKED),
        compiler_params=_cparams("arbitrary"),
        name="experts",
    )(items, xs, w_gate, w_up, w_down)


def _sc_gather_rows(table, idx):
    n_idx = idx.shape[0]
    width = table.shape[1]
    info = plsc.get_sparse_core_info()
    n_workers = info.num_cores * info.num_subcores
    per_worker = n_idx // n_workers
    assert per_worker * n_workers == n_idx and per_worker % (SC_GATHER_BUFS * SC_WINDOW) == 0
    mesh = plsc.VectorSubcoreMesh(core_axis_name="c", subcore_axis_name="s")

    def body(table_hbm, idx_hbm, out_hbm, idx_v, rows_v, sem_g, sem_o):
        wid = lax.axis_index("s") * info.num_cores + lax.axis_index("c")
        base = wid * per_worker

        @pl.loop(0, per_worker // (SC_GATHER_BUFS * SC_WINDOW))
        def _(it):
            offs = [base + (it * SC_GATHER_BUFS + b) * SC_WINDOW for b in range(SC_GATHER_BUFS)]
            gathers = []
            for b, off in enumerate(offs):
                pltpu.sync_copy(idx_hbm.at[pl.ds(off, SC_WINDOW)], idx_v.at[b])
                gathers.append(pltpu.async_copy(table_hbm.at[idx_v.at[b]], rows_v.at[b], sem_g.at[b]))
            writes = []
            for b, off in enumerate(offs):
                gathers[b].wait()
                writes.append(pltpu.async_copy(rows_v.at[b], out_hbm.at[pl.ds(off, SC_WINDOW)], sem_o.at[b]))
            for cp in writes:
                cp.wait()

    return pl.kernel(
        body,
        out_type=jax.ShapeDtypeStruct((n_idx, width), table.dtype),
        mesh=mesh,
        scratch_types=[pltpu.VMEM((SC_GATHER_BUFS, SC_WINDOW), I32),
                       pltpu.VMEM((SC_GATHER_BUFS, SC_WINDOW, width), table.dtype),
                       pltpu.SemaphoreType.DMA((SC_GATHER_BUFS,)), pltpu.SemaphoreType.DMA((SC_GATHER_BUFS,))],
        name="sc_gather",
    )(table, idx)


def _combine_kernel(hp_ref, x1_ref, g2_ref, w_ref, sgw_ref, suw_ref, sdw_ref, yg_ref, *rest):
    o_ref = rest[-1]
    xa, xb = _unpack_halves(hp_ref[...])
    xa = xa.astype(BF16)
    xb = xb.astype(BF16)
    half = xa.shape[1]
    tb = xa.shape[0]
    g = _dot(xa, sgw_ref[0:half]) + _dot(xb, sgw_ref[half:])
    u = _dot(xa, suw_ref[0:half]) + _dot(xb, suw_ref[half:])
    shared = _dot((_silu(g) * u).astype(BF16), sdw_ref[...])

    w = w_ref[...]
    acc_a = jnp.zeros((tb, half), F32)
    acc_b = jnp.zeros((tb, half), F32)
    for k in range(TOP_K):
        ya, yb = _unpack_halves(yg_ref[0, k])
        acc_a += ya * w[:, k:k + 1]
        acc_b += yb * w[:, k:k + 1]
    g2 = g2_ref[0]
    o_ref[:, 0:half] = x1_ref[:, 0:half] + g2[:, 0:half] * (acc_a + shared[:, 0:half])
    o_ref[:, half:] = x1_ref[:, half:] + g2[:, half:] * (acc_b + shared[:, half:])


def _combine(hp, x1, g2, w_tok, sgw, suw, sdw, yg, out_prev, *, tb, seq_len, first_step):
    t, half = hp.shape
    d = 2 * half
    per_seq = seq_len // tb
    f = sgw.shape[1]
    s0 = first_step
    in_specs = [pl.BlockSpec((tb, half), lambda i: (s0 + i, 0)),
                pl.BlockSpec((tb, d), lambda i: (s0 + i, 0)),
                pl.BlockSpec((1, 1, d), lambda i: ((s0 + i) // per_seq, 0, 0)),
                pl.BlockSpec((tb, TOP_K), lambda i: (s0 + i, 0)),
                pl.BlockSpec((d, f), lambda i: (0, 0)),
                pl.BlockSpec((d, f), lambda i: (0, 0)),
                pl.BlockSpec((f, d), lambda i: (0, 0)),
                pl.BlockSpec((1, TOP_K, tb, half), lambda i: (i, 0, 0, 0))]
    args = [hp, x1, g2, w_tok, sgw, suw, sdw, yg]
    aliases = {}
    if out_prev is not None:
        in_specs.append(pl.BlockSpec(memory_space=pl.ANY))
        args.append(out_prev)
        aliases = {len(args) - 1: 0}
    return pl.pallas_call(
        _combine_kernel,
        grid=(yg.shape[0],),
        in_specs=in_specs,
        out_specs=pl.BlockSpec((tb, d), lambda i: (s0 + i, 0)),
        out_shape=jax.ShapeDtypeStruct((t, d), F32),
        input_output_aliases=aliases,
        compiler_params=_cparams("arbitrary"),
        name="combine",
    )(*args)


def _rope_tables(l):
    rows = l // GRID_W
    r = jnp.repeat(jnp.arange(rows), GRID_W).astype(F32)
    col = jnp.tile(jnp.arange(GRID_W), rows).astype(F32)
    n_f = HEAD_DIM // 4
    freqs = ROPE_THETA ** (-jnp.arange(n_f, dtype=F32) / n_f)
    ang = jnp.concatenate([r[:, None] * freqs, col[:, None] * freqs], axis=-1)
    ang = jnp.tile(jnp.repeat(ang, 2, axis=1), (1, LANES // HEAD_DIM))
    sign = jnp.where(jnp.arange(LANES) % 2 == 0, -1.0, 1.0).astype(F32)
    return jnp.cos(ang), jnp.sin(ang) * sign


def kernel(x, c, ctx, c_ctx, w_mod, b_mod, norm1_w, norm2_w, w_in, q_norm_w, k_norm_w, ret_decay_fwd,
           ret_decay_bwd, w_out, router_w, router_bias, exp_w_gate, exp_w_up, exp_w_down, sh_w_gate,
           sh_w_up, sh_w_down):
    b, l, d = x.shape
    lc = ctx.shape[1]
    t = b * l
    assert w_mod.shape[0] == 1, "single layer"
    assert l % CHUNK == 0 and lc % CHUNK == 0 and l % GRID_W == 0

    rows = (b + 1 + 7) // 8 * 8
    cc = jnp.zeros((rows, d), F32).at[:b].set(c).at[b].set(c_ctx)
    mod = _modulation(cc, w_mod[0], b_mod[0])
    sh1, sc1, g1, sh2, sc2, g2 = [mod[:b, i * d:(i + 1) * d].reshape(b, 1, d) for i in range(6)]
    shc = mod[b, 0:d].reshape(1, 1, d)
    scc = mod[b, d:2 * d].reshape(1, 1, d)

    wi = w_in[0].astype(BF16)
    qnw = jnp.tile(q_norm_w[0], LANES // HEAD_DIM).reshape(1, LANES)
    knw = jnp.tile(k_norm_w[0], LANES // HEAD_DIM).reshape(1, LANES)
    cos, sin = _rope_tables(l)
    n1 = norm1_w[0].reshape(1, d)

    cklo, ckhi, cvlo, cvhi, crk, crv = _projection(
        ctx, shc, scc, n1, wi, qnw, knw, cos[:lc], sin[:lc], rope=False, with_q=False, tm=TILE_PROJ)
    klo, khi, vlo, vhi, rk, rv, q, rq, sg = _projection(
        x, sh1, sc1, n1, wi, qnw, knw, cos, sin, rope=True, with_q=True, tm=TILE_PROJ)

    bound = (HEAD_DIM * QK_SCALE * LOG2_E * BOUND_MARGIN
             * jnp.max(jnp.abs(q_norm_w[0])) * jnp.max(jnp.abs(k_norm_w[0]))).astype(F32)
    attn_args = (bound.reshape(1), q, klo, khi, vlo, vhi, cklo, ckhi, cvlo, cvhi)
    attn = lax.cond(bound <= MAX_STREAM_SHIFT,
                    functools.partial(_attention, tq=TILE_ATTN_Q, streaming=True),
                    functools.partial(_attention, tq=TILE_ATTN_Q, streaming=False), *attn_args)
    dec_f = jnp.repeat(ret_decay_fwd[0].astype(F32), HEAD_DIM).reshape(1, RET_W)
    dec_b = jnp.repeat(ret_decay_bwd[0].astype(F32), HEAD_DIM).reshape(1, RET_W)
    ret = _retention(rq, rk, rv, sg, crk, crv, dec_f, dec_b)

    wo = w_out[0].astype(BF16)
    r_hi, r_lo = _split(router_w[0].T)
    x1, hp, logits_t = _out_projection(attn, ret, x, wo[:ATTN_W], wo[ATTN_W:], g1, sh2, sc2,
                                       norm2_w[0].reshape(1, d), r_hi, r_lo, tm=TILE_OUT)

    idx_t, w_t, rank_t, cnt_col, cnt_row = _route(logits_t, router_bias[0].reshape(N_EXPERTS, 1), tb=TILE_TOKENS)
    n_blocks = -(-(t * TOP_K) // MOE_BLOCK) + N_EXPERTS
    tb = TILE_TOKENS
    dest, meta, items = _destinations(idx_t, rank_t, cnt_col, cnt_row, tb=tb, n_blocks=n_blocks)
    steps, _, tbe = dest.shape
    dest_win = dest.reshape(steps, TOP_K, tbe // SC_WINDOW, SC_WINDOW).transpose(0, 2, 1, 3)
    dest_win = dest_win.reshape(t // SC_WINDOW, TOP_K, SC_WINDOW)
    xs = _sc_scatter_rows(hp, dest_win, (n_blocks + ITEM_BLOCKS - 1) * MOE_BLOCK)
    xs = _pad_fill(meta, xs)
    ys = _experts(items, xs, exp_w_gate[0], exp_w_up[0], exp_w_down[0], n_blocks=n_blocks)
    parts = COMBINE_PARTS if steps % COMBINE_PARTS == 0 else 1
    steps_part = steps // parts
    x1f, w_tok = x1.reshape(t, d), w_t.T
    sgw, suw, sdw = sh_w_gate[0].astype(BF16), sh_w_up[0].astype(BF16), sh_w_down[0].astype(BF16)
    out = None
    for p in range(parts):
        idx = dest[p * steps_part:(p + 1) * steps_part].reshape(-1)
        yg = _sc_gather_rows(ys, idx).reshape(steps_part, TOP_K, tbe, d // 2)
        out = _combine(hp, x1f, g2, w_tok, sgw, suw, sdw, yg, out, tb=tbe, seq_len=l, first_step=p * steps_part)
    return out.reshape(b, l, d)
```

```python
import functools

import jax
import jax.numpy as jnp
from jax import lax
from jax.experimental import pallas as pl
from jax.experimental.pallas import tpu as pltpu
from jax.experimental.pallas import tpu_sc as plsc

F32 = jnp.float32
BF16 = jnp.bfloat16
I32 = jnp.int32
U32 = jnp.uint32
PACKED = jnp.int32

HEAD_DIM = 64
LANES = 128
SUBLANES = 8
ATTN_HEADS = 8
ATTN_KV_HEADS = 2
GQA = ATTN_HEADS // ATTN_KV_HEADS
RET_HEADS = 8
ATTN_W = ATTN_HEADS * HEAD_DIM
KV_W = ATTN_KV_HEADS * HEAD_DIM
RET_W = RET_HEADS * HEAD_DIM
RET_PAIRS = RET_W // LANES
CHUNK = 128
GRID_W = 64
ROPE_THETA = 10000.0
N_EXPERTS = 256
TOP_K = 8
N_GROUPS = 8
GROUP_SIZE = N_EXPERTS // N_GROUPS
TOPK_GROUPS = 4
ROUTED_SCALE = 2.5
MOE_BLOCK = 128
EPS = 1e-6
QK_SCALE = HEAD_DIM ** -0.5
LOG2_E = 1.4426950408889634
KEY_TILE = 256
BOUND_MARGIN = 1.02
MAX_STREAM_SHIFT = 56.0

OFF_AK = 0
OFF_AV = OFF_AK + KV_W
OFF_RK = OFF_AV + KV_W
OFF_RV = OFF_RK + RET_W
CTX_KV_COLS = OFF_RV + RET_W
OFF_AQ = CTX_KV_COLS
OFF_RQ = OFF_AQ + ATTN_W
OFF_RG = OFF_RQ + RET_W
IN_COLS = OFF_RG + RET_W

V7X_VMEM_BYTES = 64 * 1024 * 1024
VMEM_LIMIT = V7X_VMEM_BYTES * 13 // 16

SUB_ROWS = 256
TILE_PROJ = 1024
TILE_ATTN_Q = 1024
TILE_OUT = 512
TILE_TOKENS = 512
ITEM_BLOCKS = 5
COMBINE_PARTS = 4
SC_GATHER_BUFS = 2
WEIGHT_SLOTS = 4
SC_WINDOW = 64
HI_MASK = 0xFFFF0000


def _cparams(*sem):
    return pltpu.CompilerParams(dimension_semantics=sem, vmem_limit_bytes=VMEM_LIMIT)


def _split(a):
    hi = a.astype(BF16)
    lo = (a - hi.astype(F32)).astype(BF16)
    return hi, lo


def _dot(a, b):
    return jnp.dot(a, b, preferred_element_type=F32)


def _dot_nt(a, b):
    return lax.dot_general(a, b, (((1,), (1,)), ((), ())), preferred_element_type=F32)


def _sigmoid(v):
    return 1.0 / (1.0 + jnp.exp(-v))


def _silu(v):
    return v * _sigmoid(v)


def _pack_halves(a, b):
    ua = lax.bitcast_convert_type(a.astype(BF16).astype(F32), U32)
    ub = lax.bitcast_convert_type(b.astype(BF16).astype(F32), U32)
    return lax.bitcast_convert_type((ua & jnp.uint32(HI_MASK)) | (ub >> 16), PACKED)


def _unpack_halves(p):
    u = lax.bitcast_convert_type(p, U32)
    a = lax.bitcast_convert_type(u & jnp.uint32(HI_MASK), F32)
    b = lax.bitcast_convert_type(u << 16, F32)
    return a, b


def _mod_kernel(c_ref, w_ref, b_ref, o_ref):
    s_hi, s_lo = _split(_silu(c_ref[...]))
    w_hi, w_lo = _split(w_ref[...])
    o_ref[...] = _dot(s_hi, w_hi) + _dot(s_hi, w_lo) + _dot(s_lo, w_hi) + b_ref[...]


def _modulation(cc, w_mod, b_mod):
    rows, d = cc.shape
    n = w_mod.shape[1]
    tn = 768
    return pl.pallas_call(
        _mod_kernel,
        grid=(n // tn,),
        in_specs=[pl.BlockSpec((rows, d), lambda j: (0, 0)),
                  pl.BlockSpec((d, tn), lambda j: (0, j)),
                  pl.BlockSpec((1, tn), lambda j: (0, j))],
        out_specs=pl.BlockSpec((rows, tn), lambda j: (0, j)),
        out_shape=jax.ShapeDtypeStruct((rows, n), F32),
        compiler_params=_cparams("arbitrary"),
        name="mod",
    )(cc, w_mod, b_mod.reshape(1, n))


def _segment_ones():
    r = lax.broadcasted_iota(I32, (LANES, LANES), 0) // HEAD_DIM
    c = lax.broadcasted_iota(I32, (LANES, LANES), 1) // HEAD_DIM
    return jnp.where(r == c, 1.0, 0.0).astype(BF16)


def _head_mean_sq(v, seg):
    hi, lo = _split(v * v)
    return (_dot(hi, seg) + _dot(lo, seg)) * (1.0 / HEAD_DIM)


def _proj_kernel(x_ref, sh_ref, sc_ref, nw_ref, wi_ref, qnw_ref, knw_ref, cos_ref, sin_ref,
                 *out_refs, rope, with_q):
    if with_q:
        klo_ref, khi_ref, vlo_ref, vhi_ref, rk_ref, rv_ref, q_ref, rq_ref, sg_ref = out_refs
    else:
        klo_ref, khi_ref, vlo_ref, vhi_ref, rk_ref, rv_ref = out_refs
    tm = x_ref.shape[1]
    sub = min(tm, SUB_ROWS)
    seg = _segment_ones()
    lane = lax.broadcasted_iota(I32, (sub, LANES), 1)
    low_half = lane < HEAD_DIM
    even = (lane & 1) == 0

    for r0 in range(0, tm, sub):
        rows = slice(r0, r0 + sub)
        x = x_ref[0, rows, :]
        h = x * lax.rsqrt(jnp.mean(x * x, axis=-1, keepdims=True) + EPS) * nw_ref[...]
        h = h * (1.0 + sc_ref[0]) + sh_ref[0]
        z = _dot(h.astype(BF16), wi_ref[...])

        def norm_rope(v, w128, rows=rows):
            v = v * lax.rsqrt(_head_mean_sq(v, seg) + EPS) * w128
            if rope:
                swapped = jnp.where(even, pltpu.roll(v, LANES - 1, 1), pltpu.roll(v, 1, 1))
                v = v * cos_ref[rows, :] + swapped * sin_ref[rows, :]
            return v

        k = norm_rope(z[:, OFF_AK:OFF_AK + KV_W], knw_ref[...])
        ksw = pltpu.roll(k, HEAD_DIM, 1)
        klo_ref[0, 0, :, rows] = jnp.transpose(jnp.where(low_half, k, 0.0)).astype(BF16)
        khi_ref[0, 0, :, rows] = jnp.transpose(jnp.where(low_half, 0.0, ksw)).astype(BF16)
        klo_ref[0, 1, :, rows] = jnp.transpose(jnp.where(low_half, ksw, 0.0)).astype(BF16)
        khi_ref[0, 1, :, rows] = jnp.transpose(jnp.where(low_half, 0.0, k)).astype(BF16)
        v = z[:, OFF_AV:OFF_AV + KV_W]
        vsw = pltpu.roll(v, HEAD_DIM, 1)
        vlo_ref[0, 0, rows, :] = jnp.where(low_half, v, 0.0).astype(BF16)
        vhi_ref[0, 0, rows, :] = jnp.where(low_half, 0.0, vsw).astype(BF16)
        vlo_ref[0, 1, rows, :] = jnp.where(low_half, vsw, 0.0).astype(BF16)
        vhi_ref[0, 1, rows, :] = jnp.where(low_half, 0.0, v).astype(BF16)
        rk_ref[0, rows, :] = (z[:, OFF_RK:OFF_RK + RET_W] * QK_SCALE).astype(BF16)
        rv_ref[0, rows, :] = z[:, OFF_RV:OFF_RV + RET_W].astype(BF16)
        if with_q:
            for j in range(ATTN_W // LANES):
                qj = norm_rope(z[:, OFF_AQ + j * LANES:OFF_AQ + (j + 1) * LANES], qnw_ref[...])
                q_ref[0, rows, j * LANES:(j + 1) * LANES] = (qj * (QK_SCALE * LOG2_E)).astype(BF16)
            rq_ref[0, rows, :] = z[:, OFF_RQ:OFF_RQ + RET_W].astype(BF16)
            sg_ref[0, rows, :] = _silu(z[:, OFF_RG:OFF_RG + RET_W]).astype(BF16)


def _projection(x, shift, scale, norm_w, wi_bf16, qnw, knw, cos, sin, *, rope, with_q, tm):
    b, l, d = x.shape
    tm = min(tm, l)
    ncols = IN_COLS if with_q else CTX_KV_COLS
    per_batch = shift.shape[0] > 1
    mod_idx = (lambda bi, i: (bi, 0, 0)) if per_batch else (lambda bi, i: (0, 0, 0))
    kv_shape = jax.ShapeDtypeStruct((b, ATTN_KV_HEADS, l, LANES), BF16)
    kv_spec = pl.BlockSpec((1, ATTN_KV_HEADS, tm, LANES), lambda bi, i: (bi, 0, i, 0))
    kt_shape = jax.ShapeDtypeStruct((b, ATTN_KV_HEADS, LANES, l), BF16)
    kt_spec = pl.BlockSpec((1, ATTN_KV_HEADS, LANES, tm), lambda bi, i: (bi, 0, 0, i))
    w_shape = jax.ShapeDtypeStruct((b, l, RET_W), BF16)
    w_spec = pl.BlockSpec((1, tm, RET_W), lambda bi, i: (bi, i, 0))
    out_shape = [kt_shape] * 2 + [kv_shape] * 2 + [w_shape] * 2
    out_specs = [kt_spec] * 2 + [kv_spec] * 2 + [w_spec] * 2
    if with_q:
        out_shape += [w_shape] * 3
        out_specs += [w_spec] * 3
    return pl.pallas_call(
        functools.partial(_proj_kernel, rope=rope, with_q=with_q),
        grid=(b, l // tm),
        in_specs=[pl.BlockSpec((1, tm, d), lambda bi, i: (bi, i, 0)),
                  pl.BlockSpec((1, 1, d), mod_idx),
                  pl.BlockSpec((1, 1, d), mod_idx),
                  pl.BlockSpec((1, d), lambda bi, i: (0, 0)),
                  pl.BlockSpec((d, ncols), lambda bi, i: (0, 0)),
                  pl.BlockSpec((1, LANES), lambda bi, i: (0, 0)),
                  pl.BlockSpec((1, LANES), lambda bi, i: (0, 0)),
                  pl.BlockSpec((tm, LANES), lambda bi, i: (i, 0)),
                  pl.BlockSpec((tm, LANES), lambda bi, i: (i, 0))],
        out_specs=out_specs,
        out_shape=out_shape,
        compiler_params=_cparams("arbitrary", "arbitrary"),
        name="proj_latent" if with_q else "proj_ctx",
    )(x, shift, scale, norm_w, wi_bf16, qnw, knw, cos, sin)


def _attn_kernel(shift_ref, q_ref, klo_ref, khi_ref, vlo_ref, vhi_ref, cklo_ref, ckhi_ref, cvlo_ref, cvhi_ref,
                 o_ref, kl_s, kh_s, va_s, *, l, lc, streaming):
    lk = l + lc

    @pl.when(pl.program_id(2) == 0)
    def _():
        kl_s[:, 0:l] = klo_ref[0, 0]
        kl_s[:, l:lk] = cklo_ref[0, 0]
        kh_s[:, 0:l] = khi_ref[0, 0]
        kh_s[:, l:lk] = ckhi_ref[0, 0]
        lane = lax.broadcasted_iota(I32, (lk, LANES), 1)
        ones_lo = jnp.where(lane < HEAD_DIM, 1.0, 0.0).astype(BF16)
        ones_hi = jnp.where(lane < HEAD_DIM, 0.0, 1.0).astype(BF16)
        for g, (v_ref, cv_ref, ones) in enumerate(((vlo_ref, cvlo_ref, ones_lo), (vhi_ref, cvhi_ref, ones_hi),
                                                   (vlo_ref, cvlo_ref, ones_lo), (vhi_ref, cvhi_ref, ones_hi))):
            v_col, one_col = (0, LANES) if g < 2 else (LANES, 0)
            va_s[g, 0:l, v_col:v_col + LANES] = v_ref[0, 0]
            va_s[g, l:lk, v_col:v_col + LANES] = cv_ref[0, 0]
            va_s[g, :, one_col:one_col + LANES] = ones

    q = q_ref[0]
    acc = []
    for g in range(GQA):
        qp = q[:, (g // 2) * LANES:(g // 2 + 1) * LANES]
        k_s = kl_s if g % 2 == 0 else kh_s
        if streaming:
            shift = shift_ref[0]
            o = None
            for c in range(0, lk, KEY_TILE):
                hi = min(c + KEY_TILE, lk)
                p = jnp.exp2(_dot(qp, k_s[:, c:hi]) - shift).astype(BF16)
                t = _dot(p, va_s[g, c:hi])
                o = t if o is None else o + t
        else:
            s = _dot(qp, k_s[...])
            p = jnp.exp2(s - jnp.max(s, axis=-1, keepdims=True)).astype(BF16)
            o = _dot(p, va_s[g])
        acc.append(o)
    out_a = acc[0] + acc[1]
    out_b = acc[2] + acc[3]
    o_ref[0, :, 0:LANES] = (out_a[:, 0:LANES] / out_a[:, LANES:2 * LANES]).astype(BF16)
    o_ref[0, :, LANES:2 * LANES] = (out_b[:, LANES:2 * LANES] / out_b[:, 0:LANES]).astype(BF16)


def _attention(shift, q, klo, khi, vlo, vhi, cklo, ckhi, cvlo, cvhi, *, tq, streaming):
    b, l, _ = q.shape
    lc = cvlo.shape[2]
    lk = l + lc
    tq = min(tq, l)
    gw = GQA * HEAD_DIM
    kt_spec = pl.BlockSpec((1, 1, LANES, l), lambda bi, h, i: (bi, h, 0, 0))
    kv_spec = pl.BlockSpec((1, 1, l, LANES), lambda bi, h, i: (bi, h, 0, 0))
    ckt_spec = pl.BlockSpec((1, 1, LANES, lc), lambda bi, h, i: (bi, h, 0, 0))
    ckv_spec = pl.BlockSpec((1, 1, lc, LANES), lambda bi, h, i: (bi, h, 0, 0))
    return pl.pallas_call(
        functools.partial(_attn_kernel, l=l, lc=lc, streaming=streaming),
        grid=(b, ATTN_KV_HEADS, l // tq),
        in_specs=([pl.BlockSpec(memory_space=pltpu.SMEM),
                   pl.BlockSpec((1, tq, gw), lambda bi, h, i: (bi, i, h))] + [kt_spec] * 2 + [kv_spec] * 2
                  + [ckt_spec] * 2 + [ckv_spec] * 2),
        out_specs=pl.BlockSpec((1, tq, gw), lambda bi, h, i: (bi, i, h)),
        out_shape=jax.ShapeDtypeStruct((b, l, ATTN_W), BF16),
        scratch_shapes=[pltpu.VMEM((LANES, lk), BF16), pltpu.VMEM((LANES, lk), BF16),
                        pltpu.VMEM((GQA, lk, 2 * LANES), BF16)],
        compiler_params=_cparams("arbitrary", "arbitrary", "arbitrary"),
        name="attn_stream" if streaming else "attn",
    )(shift, q, klo, khi, vlo, vhi, cklo, ckhi, cvlo, cvhi)


def _log_sigmoid(v):
    return jnp.minimum(v, 0.0) - jnp.log(1.0 + jnp.exp(-jnp.abs(v)))


def _ret_kernel(rq_ref, rk_ref, rv_ref, sg_ref, crk_ref, crv_ref, df_ref, db_ref, o_ref,
                m_s, xi_s, zeta_s, kv_s, st_s, *, l, lc):
    n = l // CHUNK
    nc = lc // CHUNK
    lgf = _log_sigmoid(df_ref[...])
    lgb = _log_sigmoid(db_ref[...])
    pos = lax.broadcasted_iota(I32, (CHUNK, LANES), 0).astype(F32)
    row = lax.broadcasted_iota(I32, (CHUNK, CHUNK), 0)
    col = lax.broadcasted_iota(I32, (CHUNK, CHUNK), 1)
    diff = (row - col).astype(F32)
    g_chunk = []
    for p in range(RET_PAIRS):
        cols = slice(p * LANES, (p + 1) * LANES)
        lf, lb = lgf[:, cols], lgb[:, cols]
        xi_s[p, :, 0:LANES] = jnp.exp((pos + 1.0) * lf)
        xi_s[p, :, LANES:] = jnp.exp((CHUNK - pos) * lb)
        zeta_s[p, :, 0:LANES] = jnp.exp((CHUNK - 1.0 - pos) * lf)
        zeta_s[p, :, LANES:] = jnp.exp(pos * lb)
        g_chunk.append((jnp.exp(CHUNK * lf), jnp.exp(CHUNK * lb)))
        for j in range(2):
            h = 2 * p + j
            hf = lgf[:, h * HEAD_DIM:h * HEAD_DIM + 1]
            hb = lgb[:, h * HEAD_DIM:h * HEAD_DIM + 1]
            m_s[p, :, j * CHUNK:(j + 1) * CHUNK] = jnp.where(
                diff > 0, jnp.exp(diff * hf), jnp.where(diff < 0, jnp.exp(-diff * hb), 2.0))

    lane = lax.broadcasted_iota(I32, (CHUNK, LANES), 1)
    low_half = lane < HEAD_DIM
    diag = (lax.broadcasted_iota(I32, (LANES, LANES), 0) // HEAD_DIM
            == lax.broadcasted_iota(I32, (LANES, LANES), 1) // HEAD_DIM)
    seg = jnp.where(diag, 1.0, 0.0).astype(BF16)
    seg2 = jnp.concatenate([seg, seg], axis=0)
    diag2 = jnp.concatenate([diag, diag], axis=0)

    def split_heads(a):
        zero = jnp.zeros_like(a)
        return jnp.concatenate([jnp.where(low_half, a, zero), jnp.where(low_half, zero, a)], axis=0)

    def contrib(k_ref, v_ref, r0, p):
        cols = slice(p * LANES, (p + 1) * LANES)
        kp = k_ref[0, pl.ds(r0, CHUNK), cols].astype(F32)
        kz = jnp.concatenate([kp, kp], axis=1) * zeta_s[p]
        kv = _dot(jnp.transpose(kz).astype(BF16), v_ref[0, pl.ds(r0, CHUNK), cols])
        return jnp.where(diag2, kv, 0.0)

    for c in range(nc):
        for p in range(RET_PAIRS):
            kv_s[c, p] = contrib(crk_ref, crv_ref, c * CHUNK, p)

    def contrib_body(c, carry):
        r0 = pl.multiple_of(c * CHUNK, CHUNK)
        for p in range(RET_PAIRS):
            kv_s[nc + c, p] = contrib(rk_ref, rv_ref, r0, p)
        return carry

    lax.fori_loop(0, n, contrib_body, 0, unroll=4)

    for p in range(RET_PAIRS):
        gf, gb = g_chunk[p]
        sf = jnp.zeros((LANES, LANES), F32)
        sb = jnp.zeros((LANES, LANES), F32)
        for c in range(nc):
            sf = gf * sf + kv_s[c, p, 0:LANES]
            sb = gb * sb + kv_s[nc - 1 - c, p, LANES:]

        def fwd_scan(c, s, p=p, gf=gf):
            st_s[c, p, 0:LANES] = s.astype(BF16)
            return gf * s + kv_s[nc + c, p, 0:LANES]

        def bwd_scan(j, s, p=p, gb=gb):
            c = n - 1 - j
            st_s[c, p, LANES:] = s.astype(BF16)
            return gb * s + kv_s[nc + c, p, LANES:]

        lax.fori_loop(0, n, fwd_scan, sf)
        lax.fori_loop(0, n, bwd_scan, sb)

    def out_body(c, carry):
        r0 = pl.multiple_of(c * CHUNK, CHUNK)
        for p in range(RET_PAIRS):
            cols = slice(p * LANES, (p + 1) * LANES)
            qp = rq_ref[0, pl.ds(r0, CHUNK), cols]
            kp = rk_ref[0, pl.ds(r0, CHUNK), cols]
            vp = rv_ref[0, pl.ds(r0, CHUNK), cols]
            s2 = _dot_nt(qp, split_heads(kp))
            a2 = (s2 * m_s[p]).astype(BF16)
            y = _dot(a2, split_heads(vp))
            qf = qp.astype(F32)
            qx = (jnp.concatenate([qf, qf], axis=1) * xi_s[p]).astype(BF16)
            y += _dot(qx, st_s[c, p])
            hi, lo = _split(y * y)
            ms = _dot(jnp.concatenate([hi, lo], axis=1), seg2) * (1.0 / HEAD_DIM)
            out = y * lax.rsqrt(ms + EPS) * sg_ref[0, pl.ds(r0, CHUNK), cols].astype(F32)
            o_ref[0, pl.ds(r0, CHUNK), cols] = out.astype(BF16)
        return carry

    lax.fori_loop(0, n, out_body, 0, unroll=4)


def _retention(rq, rk, rv, sg, crk, crv, dec_f, dec_b):
    b, l, _ = rq.shape
    lc = crk.shape[1]
    n = l // CHUNK
    nc = lc // CHUNK
    spec = pl.BlockSpec((1, l, RET_W), lambda bi: (bi, 0, 0))
    cspec = pl.BlockSpec((1, lc, RET_W), lambda bi: (bi, 0, 0))
    dspec = pl.BlockSpec((1, RET_W), lambda bi: (0, 0))
    return pl.pallas_call(
        functools.partial(_ret_kernel, l=l, lc=lc),
        grid=(b,),
        in_specs=[spec, spec, spec, spec, cspec, cspec, dspec, dspec],
        out_specs=spec,
        out_shape=jax.ShapeDtypeStruct((b, l, RET_W), BF16),
        scratch_shapes=[pltpu.VMEM((RET_PAIRS, CHUNK, 2 * CHUNK), F32),
                        pltpu.VMEM((RET_PAIRS, CHUNK, 2 * LANES), F32),
                        pltpu.VMEM((RET_PAIRS, CHUNK, 2 * LANES), F32),
                        pltpu.VMEM((nc + n, RET_PAIRS, 2 * LANES, LANES), F32),
                        pltpu.VMEM((n, RET_PAIRS, 2 * LANES, LANES), BF16)],
        compiler_params=_cparams("arbitrary"),
        name="ret",
    )(rq, rk, rv, sg, crk, crv, dec_f, dec_b)


def _out_kernel(attn_ref, ret_ref, x_ref, wa_ref, wr_ref, g1_ref, sh_ref, sc_ref, nw_ref, rhi_ref, rlo_ref,
                x1_ref, hp_ref, lg_ref):
    y = _dot(attn_ref[0], wa_ref[...]) + _dot(ret_ref[0], wr_ref[...])
    x1 = x_ref[0] + g1_ref[0] * y
    x1_ref[0] = x1
    h = x1 * lax.rsqrt(jnp.mean(x1 * x1, axis=-1, keepdims=True) + EPS) * nw_ref[...]
    h = h * (1.0 + sc_ref[0]) + sh_ref[0]
    half = h.shape[1] // 2
    hp_ref[...] = _pack_halves(h[:, :half], h[:, half:])
    h_hi, h_lo = _split(h)
    lg_ref[...] = _dot_nt(rhi_ref[...], h_hi) + _dot_nt(rhi_ref[...], h_lo) + _dot_nt(rlo_ref[...], h_hi)


def _out_projection(attn, ret, x, wa, wr, g1, sh2, sc2, norm_w, r_hi, r_lo, *, tm):
    b, l, d = x.shape
    tm = min(tm, l)
    nt = l // tm
    t = b * l
    mspec = pl.BlockSpec((1, 1, d), lambda bi, i: (bi, 0, 0))
    return pl.pallas_call(
        _out_kernel,
        grid=(b, nt),
        in_specs=[pl.BlockSpec((1, tm, ATTN_W), lambda bi, i: (bi, i, 0)),
                  pl.BlockSpec((1, tm, RET_W), lambda bi, i: (bi, i, 0)),
                  pl.BlockSpec((1, tm, d), lambda bi, i: (bi, i, 0)),
                  pl.BlockSpec((ATTN_W, d), lambda bi, i: (0, 0)),
                  pl.BlockSpec((RET_W, d), lambda bi, i: (0, 0)),
                  mspec, mspec, mspec,
                  pl.BlockSpec((1, d), lambda bi, i: (0, 0)),
                  pl.BlockSpec((N_EXPERTS, d), lambda bi, i: (0, 0)),
                  pl.BlockSpec((N_EXPERTS, d), lambda bi, i: (0, 0))],
        out_specs=[pl.BlockSpec((1, tm, d), lambda bi, i: (bi, i, 0)),
                   pl.BlockSpec((tm, d // 2), lambda bi, i: (bi * nt + i, 0)),
                   pl.BlockSpec((N_EXPERTS, tm), lambda bi, i: (0, bi * nt + i))],
        out_shape=[jax.ShapeDtypeStruct((b, l, d), F32),
                   jax.ShapeDtypeStruct((t, d // 2), PACKED),
                   jax.ShapeDtypeStruct((N_EXPERTS, t), F32)],
        compiler_params=_cparams("arbitrary", "arbitrary"),
        name="out_proj",
    )(attn, ret, x, wa, wr, g1, sh2, sc2, norm_w, r_hi, r_lo)


def _route_kernel(lg_ref, bias_ref, idx_ref, w_ref, rank_ref, cnt_col_ref, cnt_row_ref, tri_s, col_s, row_s):
    tb = lg_ref.shape[1]
    step = pl.program_id(0)

    @pl.when(step == 0)
    def _():
        r = lax.broadcasted_iota(I32, (tb, tb), 0)
        c = lax.broadcasted_iota(I32, (tb, tb), 1)
        tri_s[...] = jnp.where(r <= c, 1.0, 0.0).astype(BF16)
        col_s[...] = jnp.zeros_like(col_s)
        row_s[...] = jnp.zeros_like(row_s)

    scores = _sigmoid(lg_ref[...])
    biased = scores + bias_ref[...]
    neg = -jnp.inf
    sub = lax.broadcasted_iota(I32, (GROUP_SIZE, tb), 0).astype(F32)

    gscore = []
    for g in range(N_GROUPS):
        blk = biased[g * GROUP_SIZE:(g + 1) * GROUP_SIZE]
        m1 = jnp.max(blk, axis=0, keepdims=True)
        first = jnp.min(jnp.where(blk == m1, sub, float(GROUP_SIZE)), axis=0, keepdims=True)
        m2 = jnp.max(jnp.where(sub == first, neg, blk), axis=0, keepdims=True)
        gscore.append(m1 + m2)
    gs = jnp.concatenate(gscore, axis=0)
    gsub = lax.broadcasted_iota(I32, (N_GROUPS, tb), 0).astype(F32)
    keep = jnp.zeros((N_GROUPS, tb), F32)
    for _ in range(TOPK_GROUPS):
        m = jnp.max(gs, axis=0, keepdims=True)
        first = jnp.min(jnp.where(gs == m, gsub, float(N_GROUPS)), axis=0, keepdims=True)
        sel = gsub == first
        keep = jnp.where(sel, 1.0, keep)
        gs = jnp.where(sel, neg, gs)
    masked = jnp.concatenate(
        [jnp.where(keep[g:g + 1] > 0.0, biased[g * GROUP_SIZE:(g + 1) * GROUP_SIZE], neg)
         for g in range(N_GROUPS)], axis=0)

    esub = lax.broadcasted_iota(I32, (N_EXPERTS, tb), 0).astype(F32)
    sels, idxs, ws = [], [], []
    chosen = jnp.zeros((N_EXPERTS, tb), F32)
    for _ in range(TOP_K):
        m = jnp.max(masked, axis=0, keepdims=True)
        first = jnp.min(jnp.where(masked == m, esub, float(N_EXPERTS)), axis=0, keepdims=True)
        sel = esub == first
        sels.append(sel)
        idxs.append(first)
        ws.append(jnp.sum(jnp.where(sel, scores, 0.0), axis=0, keepdims=True))
        chosen = jnp.where(sel, 1.0, chosen)
        masked = jnp.where(sel, neg, masked)
    wsum = ws[0]
    for k in range(1, TOP_K):
        wsum = wsum + ws[k]
    idx_ref[...] = jnp.concatenate(idxs, axis=0).astype(I32)
    w_ref[...] = jnp.concatenate([wk / wsum * ROUTED_SCALE for wk in ws], axis=0)

    chosen_b = chosen.astype(BF16)
    incl = _dot(chosen_b, tri_s[...])
    before = incl - chosen + col_s[...]
    rank_ref[...] = jnp.concatenate(
        [jnp.sum(jnp.where(sel, before, 0.0), axis=0, keepdims=True) for sel in sels], axis=0).astype(I32)
    col_s[...] = col_s[...] + incl[:, tb - 1:tb]
    row_s[...] = row_s[...] + _dot_nt(jnp.ones((8, tb), BF16), chosen_b)
    cnt_col_ref[...] = col_s[...].astype(I32)
    cnt_row_ref[...] = row_s[...].astype(I32)


def _route(logits_t, bias_col, *, tb):
    e, t = logits_t.shape
    tb = min(tb, t)
    kspec = pl.BlockSpec((TOP_K, tb), lambda i: (0, i))
    return pl.pallas_call(
        _route_kernel,
        grid=(t // tb,),
        in_specs=[pl.BlockSpec((e, tb), lambda i: (0, i)),
                  pl.BlockSpec((e, 1), lambda i: (0, 0))],
        out_specs=[kspec, kspec, kspec,
                   pl.BlockSpec((e, 1), lambda i: (0, 0)),
                   pl.BlockSpec((8, e), lambda i: (0, 0))],
        out_shape=[jax.ShapeDtypeStruct((TOP_K, t), I32),
                   jax.ShapeDtypeStruct((TOP_K, t), F32),
                   jax.ShapeDtypeStruct((TOP_K, t), I32),
                   jax.ShapeDtypeStruct((e, 1), I32),
                   jax.ShapeDtypeStruct((8, e), I32)],
        scratch_shapes=[pltpu.VMEM((tb, tb), BF16), pltpu.VMEM((e, 1), F32), pltpu.VMEM((8, e), F32)],
        compiler_params=_cparams("arbitrary"),
        name="route",
    )(logits_t, bias_col)


def _pad_block(cnt):
    return (cnt + (MOE_BLOCK - 1)) // MOE_BLOCK * MOE_BLOCK


def _max_items(n_blocks):
    return n_blocks // ITEM_BLOCKS + N_EXPERTS


def _dest_kernel(idx_ref, rank_ref, cnt_col_ref, cnt_row_ref, dest_ref, meta_ref, items_ref, start_s):
    tb = idx_ref.shape[1]
    nip = items_ref.shape[1]

    @pl.when(pl.program_id(0) == 0)
    def _():
        pad_col = _pad_block(cnt_col_ref[...])
        pad_row = _pad_block(cnt_row_ref[0:1, :])
        er = lax.broadcasted_iota(I32, (N_EXPERTS, N_EXPERTS), 0)
        ec = lax.broadcasted_iota(I32, (N_EXPERTS, N_EXPERTS), 1)
        start_col = jnp.sum(jnp.where(ec < er, pad_row, 0), axis=1, keepdims=True)
        start_row = jnp.sum(jnp.where(er < ec, pad_col, 0), axis=0, keepdims=True)
        start_s[...] = start_col

        used = jnp.sum(pad_row, axis=1, keepdims=True) // MOE_BLOCK
        meta_ref[...] = jnp.concatenate(
            [cnt_row_ref[0:1, :], start_row, pad_row, jnp.broadcast_to(used, (1, N_EXPERTS)),
             jnp.zeros((4, N_EXPERTS), I32)], axis=0)

        nb_col = pad_col // MOE_BLOCK
        it_col = (nb_col + (ITEM_BLOCKS - 1)) // ITEM_BLOCKS
        it_row = (pad_row // MOE_BLOCK + (ITEM_BLOCKS - 1)) // ITEM_BLOCKS
        it_start = jnp.sum(jnp.where(ec < er, it_row, 0), axis=1, keepdims=True)
        n_items = jnp.sum(it_row, axis=1, keepdims=True)
        lane = lax.broadcasted_iota(I32, (1, nip), 1)
        owner = jnp.sum(jnp.where(it_start + it_col <= lane, 1, 0), axis=0, keepdims=True)
        owner = jnp.minimum(owner, N_EXPERTS - 1)
        onehot = lax.broadcasted_iota(I32, (N_EXPERTS, nip), 0) == owner

        def pick(col):
            return jnp.sum(jnp.where(onehot, col, 0), axis=0, keepdims=True)

        j = lane - pick(it_start)
        block0 = pick(start_col) // MOE_BLOCK + ITEM_BLOCKS * j
        nvalid = jnp.clip(pick(nb_col) - ITEM_BLOCKS * j, 0, ITEM_BLOCKS)
        items_ref[...] = jnp.concatenate(
            [owner, block0, jnp.where(lane < n_items, nvalid, 0), jnp.broadcast_to(n_items, (1, nip)),
             jnp.zeros((4, nip), I32)], axis=0)

    start_col = start_s[...]
    esub = lax.broadcasted_iota(I32, (N_EXPERTS, tb), 0)
    rows = []
    for k in range(TOP_K):
        onehot = esub == idx_ref[k:k + 1, :]
        rows.append(jnp.sum(jnp.where(onehot, start_col, 0), axis=0, keepdims=True) + rank_ref[k:k + 1, :])
    dest_ref[0] = jnp.concatenate(rows, axis=0)


def _destinations(idx_t, rank_t, cnt_col, cnt_row, *, tb, n_blocks):
    _, t = idx_t.shape
    tb = min(tb, t)
    nip = (_max_items(n_blocks) + LANES - 1) // LANES * LANES
    kspec = pl.BlockSpec((TOP_K, tb), lambda i: (0, i))
    return pl.pallas_call(
        _dest_kernel,
        grid=(t // tb,),
        in_specs=[kspec, kspec,
                  pl.BlockSpec((N_EXPERTS, 1), lambda i: (0, 0)),
                  pl.BlockSpec((8, N_EXPERTS), lambda i: (0, 0))],
        out_specs=[pl.BlockSpec((1, TOP_K, tb), lambda i: (i, 0, 0)),
                   pl.BlockSpec((8, N_EXPERTS), lambda i: (0, 0)),
                   pl.BlockSpec((8, nip), lambda i: (0, 0))],
        out_shape=[jax.ShapeDtypeStruct((t // tb, TOP_K, tb), I32),
                   jax.ShapeDtypeStruct((8, N_EXPERTS), I32),
                   jax.ShapeDtypeStruct((8, nip), I32)],
        scratch_shapes=[pltpu.VMEM((N_EXPERTS, 1), I32)],
        compiler_params=_cparams("arbitrary"),
        name="dest",
    )(idx_t, rank_t, cnt_col, cnt_row)


_PAD_BITS = (64, 32, 16, 8)


def _sc_scatter_rows(rows, dest_win, n_out_rows):
    n_win, n_slots, win = dest_win.shape
    width = rows.shape[1]
    info = plsc.get_sparse_core_info()
    n_workers = info.num_cores * info.num_subcores
    per_worker = n_win // n_workers
    assert per_worker * n_workers == n_win and win <= LANES
    mesh = plsc.VectorSubcoreMesh(core_axis_name="c", subcore_axis_name="s")

    def body(rows_hbm, dest_hbm, out_hbm, idx_v, rows_v, sem):
        wid = lax.axis_index("s") * info.num_cores + lax.axis_index("c")

        @pl.loop(0, per_worker)
        def _(j):
            w = wid * per_worker + j
            pltpu.sync_copy(dest_hbm.at[w], idx_v)
            pltpu.sync_copy(rows_hbm.at[pl.ds(w * win, win)], rows_v)
            copies = [pltpu.async_copy(rows_v, out_hbm.at[idx_v.at[k]], sem) for k in range(n_slots)]
            for cp in copies:
                cp.wait()

    return pl.kernel(
        body,
        out_type=jax.ShapeDtypeStruct((n_out_rows, width), rows.dtype),
        mesh=mesh,
        scratch_types=[pltpu.VMEM((n_slots, win), I32), pltpu.VMEM((win, width), rows.dtype),
                       pltpu.SemaphoreType.DMA],
        name="sc_scatter",
    )(rows, dest_win)


def _pad_fill_kernel(meta_ref, xs_in, xs_hbm, zero_s, sem_z, *, e_per_step):
    del xs_in
    step = pl.program_id(0)
    zero_s[...] = jnp.zeros_like(zero_s)

    def tail_copy(c):
        row0 = pl.multiple_of((meta_ref[3, 0] + c) * MOE_BLOCK, MOE_BLOCK)
        return pltpu.make_async_copy(zero_s, xs_hbm.at[pl.ds(row0, MOE_BLOCK)], sem_z)

    @pl.when(step == 0)
    def _():
        for c in range(ITEM_BLOCKS - 1):
            tail_copy(c).start()
        for c in range(ITEM_BLOCKS - 1):
            tail_copy(c).wait()

    def pad_copies(e):
        cnt = meta_ref[0, e]
        off = meta_ref[1, e] + cnt
        rem = meta_ref[2, e] - cnt
        head = rem & (SUBLANES - 1)
        out = []
        for i in range(SUBLANES - 1):
            out.append((i < head,
                        pltpu.make_async_copy(zero_s.at[pl.ds(0, 1)], xs_hbm.at[pl.ds(off + i, 1)], sem_z)))
        off = off + head
        for bit in _PAD_BITS:
            out.append(((rem & bit) != 0,
                        pltpu.make_async_copy(zero_s.at[pl.ds(0, bit)],
                                              xs_hbm.at[pl.ds(pl.multiple_of(off, SUBLANES), bit)], sem_z)))
            off = off + (rem & bit)
        return out

    for j in range(e_per_step):
        for cond, c in pad_copies(step * e_per_step + j):
            pl.when(cond)(c.start)
    for j in range(e_per_step):
        for cond, c in pad_copies(step * e_per_step + j):
            pl.when(cond)(c.wait)


PAD_FILL_EXPERTS_PER_STEP = 8


def _pad_fill(meta, xs):
    half = xs.shape[1]
    return pl.pallas_call(
        functools.partial(_pad_fill_kernel, e_per_step=PAD_FILL_EXPERTS_PER_STEP),
        grid=(N_EXPERTS // PAD_FILL_EXPERTS_PER_STEP,),
        in_specs=[pl.BlockSpec(memory_space=pltpu.SMEM),
                  pl.BlockSpec(memory_space=pl.ANY)],
        out_specs=pl.BlockSpec(memory_space=pl.ANY),
        out_shape=jax.ShapeDtypeStruct(xs.shape, xs.dtype),
        input_output_aliases={1: 0},
        scratch_shapes=[pltpu.VMEM((MOE_BLOCK, half), PACKED), pltpu.SemaphoreType.DMA],
        compiler_params=_cparams("arbitrary"),
        name="pad_fill",
    )(meta, xs)


def _experts_kernel(items_ref, xs_hbm, wg_hbm, wu_hbm, wd_hbm, ys_hbm,
                    xbuf, ybuf, wg_f, wu_f, wd_f, wg_s, wu_s, wd_s, sem_x, sem_y, sem_w):
    n_items = items_ref[3, 0]
    rows = ITEM_BLOCKS * MOE_BLOCK

    def w_copies(e, s):
        return [pltpu.make_async_copy(src.at[e], dst.at[s], sem_w.at[s])
                for src, dst in ((wg_hbm, wg_f), (wu_hbm, wu_f), (wd_hbm, wd_f))]

    def x_copy(item, s):
        row0 = pl.multiple_of(items_ref[1, item] * MOE_BLOCK, MOE_BLOCK)
        return pltpu.make_async_copy(xs_hbm.at[pl.ds(row0, rows)], xbuf.at[s], sem_x.at[s])

    def y_copies(item, s, fn):
        for k in range(ITEM_BLOCKS):
            row0 = pl.multiple_of((items_ref[1, item] + k) * MOE_BLOCK, MOE_BLOCK)
            cp = pltpu.make_async_copy(ybuf.at[s, pl.ds(k * MOE_BLOCK, MOE_BLOCK)],
                                       ys_hbm.at[pl.ds(row0, MOE_BLOCK)], sem_y.at[s])
            pl.when(k < items_ref[2, item])(functools.partial(fn, cp))

    def expert_of(item):
        return items_ref[0, jnp.minimum(item, n_items - 1)]

    def changes_at(item):
        return ((item < n_items) & (expert_of(item) != expert_of(item - 1))).astype(I32)

    @pl.when(n_items > 0)
    def _():
        x_copy(0, 0).start()
        for cp in w_copies(expert_of(0), 0):
            cp.start()

        first_change = changes_at(1)

        @pl.when(first_change == 1)
        def _():
            for cp in w_copies(expert_of(1), 1):
                cp.start()

        @pl.when(changes_at(2) == 1)
        def _():
            for cp in w_copies(expert_of(2), first_change + 1):
                cp.start()

    def item_body(i, ordinal):
        slot = i % 2
        prev = jnp.maximum(i - 1, 0)
        e = expert_of(i)
        new_expert = (i == 0) | (e != expert_of(prev))
        c1 = changes_at(i + 1)
        c2 = changes_at(i + 2)
        c3 = changes_at(i + 3)

        @pl.when(i + 1 < n_items)
        def _():
            x_copy(i + 1, 1 - slot).start()

        @pl.when(new_expert)
        def _():
            wslot = ordinal % WEIGHT_SLOTS
            for cp in w_copies(e, wslot):
                cp.wait()
            wg_s[...] = wg_f[wslot].astype(BF16)
            wu_s[...] = wu_f[wslot].astype(BF16)
            wd_s[...] = wd_f[wslot].astype(BF16)

        @pl.when(c3 == 1)
        def _():
            for cp in w_copies(expert_of(i + 3), (ordinal + c1 + c2 + 1) % WEIGHT_SLOTS):
                cp.start()

        x_copy(i, slot).wait()
        xa, xb = _unpack_halves(xbuf[slot])
        xa = xa.astype(BF16)
        xb = xb.astype(BF16)
        half = xa.shape[1]
        g = _dot(xa, wg_s[0:half]) + _dot(xb, wg_s[half:])
        u = _dot(xa, wu_s[0:half]) + _dot(xb, wu_s[half:])
        y = _dot((_silu(g) * u).astype(BF16), wd_s[...])
        ybuf[slot] = _pack_halves(y[:, :half], y[:, half:])
        y_copies(i, slot, lambda cp: cp.start())

        @pl.when(i > 0)
        def _():
            y_copies(prev, 1 - slot, lambda cp: cp.wait())

        return ordinal + c1

    lax.fori_loop(0, n_items, item_body, jnp.int32(0))

    @pl.when(n_items > 0)
    def _():
        last = n_items - 1
        y_copies(last, last % 2, lambda cp: cp.wait())


def _experts(items, xs, w_gate, w_up, w_down, *, n_blocks):
    half = xs.shape[1]
    e, d, f = w_gate.shape
    rows = ITEM_BLOCKS * MOE_BLOCK
    any_spec = pl.BlockSpec(memory_space=pl.ANY)
    return pl.pallas_call(
        _experts_kernel,
        grid_spec=pltpu.PrefetchScalarGridSpec(
            num_scalar_prefetch=1,
            grid=(1,),
            in_specs=[any_spec, any_spec, any_spec, any_spec],
            out_specs=any_spec,
            scratch_shapes=[pltpu.VMEM((2, rows, half), PACKED), pltpu.VMEM((2, rows, half), PACKED),
                            pltpu.VMEM((WEIGHT_SLOTS, d, f), F32), pltpu.VMEM((WEIGHT_SLOTS, d, f), F32),
                            pltpu.VMEM((WEIGHT_SLOTS, f, d), F32),
                            pltpu.VMEM((d, f), BF16), pltpu.VMEM((d, f), BF16), pltpu.VMEM((f, d), BF16),
                            pltpu.SemaphoreType.DMA((2,)), pltpu.SemaphoreType.DMA((2,)),
                            pltpu.SemaphoreType.DMA((WEIGHT_SLOTS,))]),
        out_shape=jax.ShapeDtypeStruct((n_blocks * MOE_BLOCK, half), PACKED),
        compiler_params=_cparams("arbitrary"),
        name="experts",
    )(items, xs, w_gate, w_up, w_down)


def _sc_gather_rows(table, idx):
    n_idx = idx.shape[0]
    width = table.shape[1]
    info = plsc.get_sparse_core_info()
    n_workers = info.num_cores * info.num_subcores
    per_worker = n_idx // n_workers
    assert per_worker * n_workers == n_idx and per_worker % (SC_GATHER_BUFS * SC_WINDOW) == 0
    mesh = plsc.VectorSubcoreMesh(core_axis_name="c", subcore_axis_name="s")

    def body(table_hbm, idx_hbm, out_hbm, idx_v, rows_v, sem_g, sem_o):
        wid = lax.axis_index("s") * info.num_cores + lax.axis_index("c")
        base = wid * per_worker

        @pl.loop(0, per_worker // (SC_GATHER_BUFS * SC_WINDOW))
        def _(it):
            offs = [base + (it * SC_GATHER_BUFS + b) * SC_WINDOW for b in range(SC_GATHER_BUFS)]
            gathers = []
            for b, off in enumerate(offs):
                pltpu.sync_copy(idx_hbm.at[pl.ds(off, SC_WINDOW)], idx_v.at[b])
                gathers.append(pltpu.async_copy(table_hbm.at[idx_v.at[b]], rows_v.at[b], sem_g.at[b]))
            writes = []
            for b, off in enumerate(offs):
                gathers[b].wait()
                writes.append(pltpu.async_copy(rows_v.at[b], out_hbm.at[pl.ds(off, SC_WINDOW)], sem_o.at[b]))
            for cp in writes:
                cp.wait()

    return pl.kernel(
        body,
        out_type=jax.ShapeDtypeStruct((n_idx, width), table.dtype),
        mesh=mesh,
        scratch_types=[pltpu.VMEM((SC_GATHER_BUFS, SC_WINDOW), I32),
                       pltpu.VMEM((SC_GATHER_BUFS, SC_WINDOW, width), table.dtype),
                       pltpu.SemaphoreType.DMA((SC_GATHER_BUFS,)), pltpu.SemaphoreType.DMA((SC_GATHER_BUFS,))],
        name="sc_gather",
    )(table, idx)


def _combine_kernel(hp_ref, x1_ref, g2_ref, w_ref, sgw_ref, suw_ref, sdw_ref, yg_ref, *rest):
    o_ref = rest[-1]
    xa, xb = _unpack_halves(hp_ref[...])
    xa = xa.astype(BF16)
    xb = xb.astype(BF16)
    half = xa.shape[1]
    tb = xa.shape[0]
    g = _dot(xa, sgw_ref[0:half]) + _dot(xb, sgw_ref[half:])
    u = _dot(xa, suw_ref[0:half]) + _dot(xb, suw_ref[half:])
    shared = _dot((_silu(g) * u).astype(BF16), sdw_ref[...])

    w = w_ref[...]
    acc_a = jnp.zeros((tb, half), F32)
    acc_b = jnp.zeros((tb, half), F32)
    for k in range(TOP_K):
        ya, yb = _unpack_halves(yg_ref[0, k])
        acc_a += ya * w[:, k:k + 1]
        acc_b += yb * w[:, k:k + 1]
    g2 = g2_ref[0]
    o_ref[:, 0:half] = x1_ref[:, 0:half] + g2[:, 0:half] * (acc_a + shared[:, 0:half])
    o_ref[:, half:] = x1_ref[:, half:] + g2[:, half:] * (acc_b + shared[:, half:])


def _combine(hp, x1, g2, w_tok, sgw, suw, sdw, yg, out_prev, *, tb, seq_len, first_step):
    t, half = hp.shape
    d = 2 * half
    per_seq = seq_len // tb
    f = sgw.shape[1]
    s0 = first_step
    in_specs = [pl.BlockSpec((tb, half), lambda i: (s0 + i, 0)),
                pl.BlockSpec((tb, d), lambda i: (s0 + i, 0)),
                pl.BlockSpec((1, 1, d), lambda i: ((s0 + i) // per_seq, 0, 0)),
                pl.BlockSpec((tb, TOP_K), lambda i: (s0 + i, 0)),
                pl.BlockSpec((d, f), lambda i: (0, 0)),
                pl.BlockSpec((d, f), lambda i: (0, 0)),
                pl.BlockSpec((f, d), lambda i: (0, 0)),
                pl.BlockSpec((1, TOP_K, tb, half), lambda i: (i, 0, 0, 0))]
    args = [hp, x1, g2, w_tok, sgw, suw, sdw, yg]
    aliases = {}
    if out_prev is not None:
        in_specs.append(pl.BlockSpec(memory_space=pl.ANY))
        args.append(out_prev)
        aliases = {len(args) - 1: 0}
    return pl.pallas_call(
        _combine_kernel,
        grid=(yg.shape[0],),
        in_specs=in_specs,
        out_specs=pl.BlockSpec((tb, d), lambda i: (s0 + i, 0)),
        out_shape=jax.ShapeDtypeStruct((t, d), F32),
        input_output_aliases=aliases,
        compiler_params=_cparams("arbitrary"),
        name="combine",
    )(*args)


def _rope_tables(l):
    rows = l // GRID_W
    r = jnp.repeat(jnp.arange(rows), GRID_W).astype(F32)
    col = jnp.tile(jnp.arange(GRID_W), rows).astype(F32)
    n_f = HEAD_DIM // 4
    freqs = ROPE_THETA ** (-jnp.arange(n_f, dtype=F32) / n_f)
    ang = jnp.concatenate([r[:, None] * freqs, col[:, None] * freqs], axis=-1)
    ang = jnp.tile(jnp.repeat(ang, 2, axis=1), (1, LANES // HEAD_DIM))
    sign = jnp.where(jnp.arange(LANES) % 2 == 0, -1.0, 1.0).astype(F32)
    return jnp.cos(ang), jnp.sin(ang) * sign


def kernel(x, c, ctx, c_ctx, w_mod, b_mod, norm1_w, norm2_w, w_in, q_norm_w, k_norm_w, ret_decay_fwd,
           ret_decay_bwd, w_out, router_w, router_bias, exp_w_gate, exp_w_up, exp_w_down, sh_w_gate,
           sh_w_up, sh_w_down):
    b, l, d = x.shape
    lc = ctx.shape[1]
    t = b * l
    assert w_mod.shape[0] == 1, "single layer"
    assert l % CHUNK == 0 and lc % CHUNK == 0 and l % GRID_W == 0

    rows = (b + 1 + 7) // 8 * 8
    cc = jnp.zeros((rows, d), F32).at[:b].set(c).at[b].set(c_ctx)
    mod = _modulation(cc, w_mod[0], b_mod[0])
    sh1, sc1, g1, sh2, sc2, g2 = [mod[:b, i * d:(i + 1) * d].reshape(b, 1, d) for i in range(6)]
    shc = mod[b, 0:d].reshape(1, 1, d)
    scc = mod[b, d:2 * d].reshape(1, 1, d)

    wi = w_in[0].astype(BF16)
    qnw = jnp.tile(q_norm_w[0], LANES // HEAD_DIM).reshape(1, LANES)
    knw = jnp.tile(k_norm_w[0], LANES // HEAD_DIM).reshape(1, LANES)
    cos, sin = _rope_tables(l)
    n1 = norm1_w[0].reshape(1, d)

    cklo, ckhi, cvlo, cvhi, crk, crv = _projection(
        ctx, shc, scc, n1, wi, qnw, knw, cos[:lc], sin[:lc], rope=False, with_q=False, tm=TILE_PROJ)
    klo, khi, vlo, vhi, rk, rv, q, rq, sg = _projection(
        x, sh1, sc1, n1, wi, qnw, knw, cos, sin, rope=True, with_q=True, tm=TILE_PROJ)

    bound = (HEAD_DIM * QK_SCALE * LOG2_E * BOUND_MARGIN
             * jnp.max(jnp.abs(q_norm_w[0])) * jnp.max(jnp.abs(k_norm_w[0]))).astype(F32)
    attn_args = (bound.reshape(1), q, klo, khi, vlo, vhi, cklo, ckhi, cvlo, cvhi)
    attn = lax.cond(bound <= MAX_STREAM_SHIFT,
                    functools.partial(_attention, tq=TILE_ATTN_Q, streaming=True),
                    functools.partial(_attention, tq=TILE_ATTN_Q, streaming=False), *attn_args)
    dec_f = jnp.repeat(ret_decay_fwd[0].astype(F32), HEAD_DIM).reshape(1, RET_W)
    dec_b = jnp.repeat(ret_decay_bwd[0].astype(F32), HEAD_DIM).reshape(1, RET_W)
    ret = _retention(rq, rk, rv, sg, crk, crv, dec_f, dec_b)

    wo = w_out[0].astype(BF16)
    r_hi, r_lo = _split(router_w[0].T)
    x1, hp, logits_t = _out_projection(attn, ret, x, wo[:ATTN_W], wo[ATTN_W:], g1, sh2, sc2,
                                       norm2_w[0].reshape(1, d), r_hi, r_lo, tm=TILE_OUT)

    idx_t, w_t, rank_t, cnt_col, cnt_row = _route(logits_t, router_bias[0].reshape(N_EXPERTS, 1), tb=TILE_TOKENS)
    n_blocks = -(-(t * TOP_K) // MOE_BLOCK) + N_EXPERTS
    tb = TILE_TOKENS
    dest, meta, items = _destinations(idx_t, rank_t, cnt_col, cnt_row, tb=tb, n_blocks=n_blocks)
    steps, _, tbe = dest.shape
    dest_win = dest.reshape(steps, TOP_K, tbe // SC_WINDOW, SC_WINDOW).transpose(0, 2, 1, 3)
    dest_win = dest_win.reshape(t // SC_WINDOW, TOP_K, SC_WINDOW)
    xs = _sc_scatter_rows(hp, dest_win, (n_blocks + ITEM_BLOCKS - 1) * MOE_BLOCK)
    xs = _pad_fill(meta, xs)
    ys = _experts(items, xs, exp_w_gate[0], exp_w_up[0], exp_w_down[0], n_blocks=n_blocks)
    parts = COMBINE_PARTS if steps % COMBINE_PARTS == 0 else 1
    steps_part = steps // parts
    x1f, w_tok = x1.reshape(t, d), w_t.T
    sgw, suw, sdw = sh_w_gate[0].astype(BF16), sh_w_up[0].astype(BF16), sh_w_down[0].astype(BF16)
    out = None
    for p in range(parts):
        idx = dest[p * steps_part:(p + 1) * steps_part].reshape(-1)
        yg = _sc_gather_rows(ys, idx).reshape(steps_part, TOP_K, tbe, d // 2)
        out = _combine(hp, x1f, g2, w_tok, sgw, suw, sdw, yg, out, tb=tbe, seq_len=l, first_step=p * steps_part)
    return out.reshape(b, l, d)
```

```python
import functools

import jax
import jax.numpy as jnp
from jax import lax
from jax.experimental import pallas as pl
from jax.experimental.pallas import tpu as pltpu
from jax.experimental.pallas import tpu_sc as plsc

F32 = jnp.float32
BF16 = jnp.bfloat16
I32 = jnp.int32
U32 = jnp.uint32
PACKED = jnp.int32

HEAD_DIM = 64
LANES = 128
SUBLANES = 8
ATTN_HEADS = 8
ATTN_KV_HEADS = 2
GQA = ATTN_HEADS // ATTN_KV_HEADS
RET_HEADS = 8
ATTN_W = ATTN_HEADS * HEAD_DIM
KV_W = ATTN_KV_HEADS * HEAD_DIM
RET_W = RET_HEADS * HEAD_DIM
RET_PAIRS = RET_W // LANES
CHUNK = 128
GRID_W = 64
ROPE_THETA = 10000.0
N_EXPERTS = 256
TOP_K = 8
N_GROUPS = 8
GROUP_SIZE = N_EXPERTS // N_GROUPS
TOPK_GROUPS = 4
ROUTED_SCALE = 2.5
MOE_BLOCK = 128
EPS = 1e-6
QK_SCALE = HEAD_DIM ** -0.5
LOG2_E = 1.4426950408889634
KEY_TILE = 256
BOUND_MARGIN = 1.02
MAX_STREAM_SHIFT = 56.0

OFF_AK = 0
OFF_AV = OFF_AK + KV_W
OFF_RK = OFF_AV + KV_W
OFF_RV = OFF_RK + RET_W
CTX_KV_COLS = OFF_RV + RET_W
OFF_AQ = CTX_KV_COLS
OFF_RQ = OFF_AQ + ATTN_W
OFF_RG = OFF_RQ + RET_W
IN_COLS = OFF_RG + RET_W

V7X_VMEM_BYTES = 64 * 1024 * 1024
VMEM_LIMIT = V7X_VMEM_BYTES * 13 // 16

SUB_ROWS = 256
TILE_PROJ = 1024
TILE_ATTN_Q = 1024
TILE_OUT = 512
TILE_TOKENS = 512
ITEM_BLOCKS = 5
COMBINE_PARTS = 4
SC_GATHER_BUFS = 2
WEIGHT_SLOTS = 3
SC_WINDOW = 64
HI_MASK = 0xFFFF0000


def _cparams(*sem):
    return pltpu.CompilerParams(dimension_semantics=sem, vmem_limit_bytes=VMEM_LIMIT)


def _split(a):
    hi = a.astype(BF16)
    lo = (a - hi.astype(F32)).astype(BF16)
    return hi, lo


def _dot(a, b):
    return jnp.dot(a, b, preferred_element_type=F32)


def _dot_nt(a, b):
    return lax.dot_general(a, b, (((1,), (1,)), ((), ())), preferred_element_type=F32)


def _sigmoid(v):
    return 1.0 / (1.0 + jnp.exp(-v))


def _silu(v):
    return v * _sigmoid(v)


def _pack_halves(a, b):
    ua = lax.bitcast_convert_type(a.astype(BF16).astype(F32), U32)
    ub = lax.bitcast_convert_type(b.astype(BF16).astype(F32), U32)
    return lax.bitcast_convert_type((ua & jnp.uint32(HI_MASK)) | (ub >> 16), PACKED)


def _unpack_halves(p):
    u = lax.bitcast_convert_type(p, U32)
    a = lax.bitcast_convert_type(u & jnp.uint32(HI_MASK), F32)
    b = lax.bitcast_convert_type(u << 16, F32)
    return a, b


def _mod_kernel(c_ref, w_ref, b_ref, o_ref):
    s_hi, s_lo = _split(_silu(c_ref[...]))
    w_hi, w_lo = _split(w_ref[...])
    o_ref[...] = _dot(s_hi, w_hi) + _dot(s_hi, w_lo) + _dot(s_lo, w_hi) + b_ref[...]


def _modulation(cc, w_mod, b_mod):
    rows, d = cc.shape
    n = w_mod.shape[1]
    tn = 768
    return pl.pallas_call(
        _mod_kernel,
        grid=(n // tn,),
        in_specs=[pl.BlockSpec((rows, d), lambda j: (0, 0)),
                  pl.BlockSpec((d, tn), lambda j: (0, j)),
                  pl.BlockSpec((1, tn), lambda j: (0, j))],
        out_specs=pl.BlockSpec((rows, tn), lambda j: (0, j)),
        out_shape=jax.ShapeDtypeStruct((rows, n), F32),
        compiler_params=_cparams("arbitrary"),
        name="mod",
    )(cc, w_mod, b_mod.reshape(1, n))


def _segment_ones():
    r = lax.broadcasted_iota(I32, (LANES, LANES), 0) // HEAD_DIM
    c = lax.broadcasted_iota(I32, (LANES, LANES), 1) // HEAD_DIM
    return jnp.where(r == c, 1.0, 0.0).astype(BF16)


def _head_mean_sq(v, seg):
    hi, lo = _split(v * v)
    return (_dot(hi, seg) + _dot(lo, seg)) * (1.0 / HEAD_DIM)


def _proj_kernel(x_ref, sh_ref, sc_ref, nw_ref, wi_ref, qnw_ref, knw_ref, cos_ref, sin_ref,
                 *out_refs, rope, with_q):
    if with_q:
        klo_ref, khi_ref, vlo_ref, vhi_ref, rk_ref, rv_ref, q_ref, rq_ref, sg_ref = out_refs
    else:
        klo_ref, khi_ref, vlo_ref, vhi_ref, rk_ref, rv_ref = out_refs
    tm = x_ref.shape[1]
    sub = min(tm, SUB_ROWS)
    seg = _segment_ones()
    lane = lax.broadcasted_iota(I32, (sub, LANES), 1)
    low_half = lane < HEAD_DIM
    even = (lane & 1) == 0

    for r0 in range(0, tm, sub):
        rows = slice(r0, r0 + sub)
        x = x_ref[0, rows, :]
        h = x * lax.rsqrt(jnp.mean(x * x, axis=-1, keepdims=True) + EPS) * nw_ref[...]
        h = h * (1.0 + sc_ref[0]) + sh_ref[0]
        z = _dot(h.astype(BF16), wi_ref[...])

        def norm_rope(v, w128, rows=rows):
            v = v * lax.rsqrt(_head_mean_sq(v, seg) + EPS) * w128
            if rope:
                swapped = jnp.where(even, pltpu.roll(v, LANES - 1, 1), pltpu.roll(v, 1, 1))
                v = v * cos_ref[rows, :] + swapped * sin_ref[rows, :]
            return v

        k = norm_rope(z[:, OFF_AK:OFF_AK + KV_W], knw_ref[...])
        ksw = pltpu.roll(k, HEAD_DIM, 1)
        klo_ref[0, 0, :, rows] = jnp.transpose(jnp.where(low_half, k, 0.0)).astype(BF16)
        khi_ref[0, 0, :, rows] = jnp.transpose(jnp.where(low_half, 0.0, ksw)).astype(BF16)
        klo_ref[0, 1, :, rows] = jnp.transpose(jnp.where(low_half, ksw, 0.0)).astype(BF16)
        khi_ref[0, 1, :, rows] = jnp.transpose(jnp.where(low_half, 0.0, k)).astype(BF16)
        v = z[:, OFF_AV:OFF_AV + KV_W]
        vsw = pltpu.roll(v, HEAD_DIM, 1)
        vlo_ref[0, 0, rows, :] = jnp.where(low_half, v, 0.0).astype(BF16)
        vhi_ref[0, 0, rows, :] = jnp.where(low_half, 0.0, vsw).astype(BF16)
        vlo_ref[0, 1, rows, :] = jnp.where(low_half, vsw, 0.0).astype(BF16)
        vhi_ref[0, 1, rows, :] = jnp.where(low_half, 0.0, v).astype(BF16)
        rk_ref[0, rows, :] = (z[:, OFF_RK:OFF_RK + RET_W] * QK_SCALE).astype(BF16)
        rv_ref[0, rows, :] = z[:, OFF_RV:OFF_RV + RET_W].astype(BF16)
        if with_q:
            for j in range(ATTN_W // LANES):
                qj = norm_rope(z[:, OFF_AQ + j * LANES:OFF_AQ + (j + 1) * LANES], qnw_ref[...])
                q_ref[0, rows, j * LANES:(j + 1) * LANES] = (qj * (QK_SCALE * LOG2_E)).astype(BF16)
            rq_ref[0, rows, :] = z[:, OFF_RQ:OFF_RQ + RET_W].astype(BF16)
            sg_ref[0, rows, :] = _silu(z[:, OFF_RG:OFF_RG + RET_W]).astype(BF16)


def _projection(x, shift, scale, norm_w, wi_bf16, qnw, knw, cos, sin, *, rope, with_q, tm):
    b, l, d = x.shape
    tm = min(tm, l)
    ncols = IN_COLS if with_q else CTX_KV_COLS
    per_batch = shift.shape[0] > 1
    mod_idx = (lambda bi, i: (bi, 0, 0)) if per_batch else (lambda bi, i: (0, 0, 0))
    kv_shape = jax.ShapeDtypeStruct((b, ATTN_KV_HEADS, l, LANES), BF16)
    kv_spec = pl.BlockSpec((1, ATTN_KV_HEADS, tm, LANES), lambda bi, i: (bi, 0, i, 0))
    kt_shape = jax.ShapeDtypeStruct((b, ATTN_KV_HEADS, LANES, l), BF16)
    kt_spec = pl.BlockSpec((1, ATTN_KV_HEADS, LANES, tm), lambda bi, i: (bi, 0, 0, i))
    w_shape = jax.ShapeDtypeStruct((b, l, RET_W), BF16)
    w_spec = pl.BlockSpec((1, tm, RET_W), lambda bi, i: (bi, i, 0))
    out_shape = [kt_shape] * 2 + [kv_shape] * 2 + [w_shape] * 2
    out_specs = [kt_spec] * 2 + [kv_spec] * 2 + [w_spec] * 2
    if with_q:
        out_shape += [w_shape] * 3
        out_specs += [w_spec] * 3
    return pl.pallas_call(
        functools.partial(_proj_kernel, rope=rope, with_q=with_q),
        grid=(b, l // tm),
        in_specs=[pl.BlockSpec((1, tm, d), lambda bi, i: (bi, i, 0)),
                  pl.BlockSpec((1, 1, d), mod_idx),
                  pl.BlockSpec((1, 1, d), mod_idx),
                  pl.BlockSpec((1, d), lambda bi, i: (0, 0)),
                  pl.BlockSpec((d, ncols), lambda bi, i: (0, 0)),
                  pl.BlockSpec((1, LANES), lambda bi, i: (0, 0)),
                  pl.BlockSpec((1, LANES), lambda bi, i: (0, 0)),
                  pl.BlockSpec((tm, LANES), lambda bi, i: (i, 0)),
                  pl.BlockSpec((tm, LANES), lambda bi, i: (i, 0))],
        out_specs=out_specs,
        out_shape=out_shape,
        compiler_params=_cparams("arbitrary", "arbitrary"),
        name="proj_latent" if with_q else "proj_ctx",
    )(x, shift, scale, norm_w, wi_bf16, qnw, knw, cos, sin)


def _attn_kernel(shift_ref, q_ref, klo_ref, khi_ref, vlo_ref, vhi_ref, cklo_ref, ckhi_ref, cvlo_ref, cvhi_ref,
                 o_ref, kl_s, kh_s, va_s, *, l, lc, streaming):
    lk = l + lc

    @pl.when(pl.program_id(2) == 0)
    def _():
        kl_s[:, 0:l] = klo_ref[0, 0]
        kl_s[:, l:lk] = cklo_ref[0, 0]
        kh_s[:, 0:l] = khi_ref[0, 0]
        kh_s[:, l:lk] = ckhi_ref[0, 0]
        lane = lax.broadcasted_iota(I32, (lk, LANES), 1)
        ones_lo = jnp.where(lane < HEAD_DIM, 1.0, 0.0).astype(BF16)
        ones_hi = jnp.where(lane < HEAD_DIM, 0.0, 1.0).astype(BF16)
        for g, (v_ref, cv_ref, ones) in enumerate(((vlo_ref, cvlo_ref, ones_lo), (vhi_ref, cvhi_ref, ones_hi),
                                                   (vlo_ref, cvlo_ref, ones_lo), (vhi_ref, cvhi_ref, ones_hi))):
            v_col, one_col = (0, LANES) if g < 2 else (LANES, 0)
            va_s[g, 0:l, v_col:v_col + LANES] = v_ref[0, 0]
            va_s[g, l:lk, v_col:v_col + LANES] = cv_ref[0, 0]
            va_s[g, :, one_col:one_col + LANES] = ones

    q = q_ref[0]
    acc = []
    for g in range(GQA):
        qp = q[:, (g // 2) * LANES:(g // 2 + 1) * LANES]
        k_s = kl_s if g % 2 == 0 else kh_s
        if streaming:
            shift = shift_ref[0]
            o = None
            for c in range(0, lk, KEY_TILE):
                hi = min(c + KEY_TILE, lk)
                p = jnp.exp2(_dot(qp, k_s[:, c:hi]) - shift).astype(BF16)
                t = _dot(p, va_s[g, c:hi])
                o = t if o is None else o + t
        else:
            s = _dot(qp, k_s[...])
            p = jnp.exp2(s - jnp.max(s, axis=-1, keepdims=True)).astype(BF16)
            o = _dot(p, va_s[g])
        acc.append(o)
    out_a = acc[0] + acc[1]
    out_b = acc[2] + acc[3]
    o_ref[0, :, 0:LANES] = (out_a[:, 0:LANES] / out_a[:, LANES:2 * LANES]).astype(BF16)
    o_ref[0, :, LANES:2 * LANES] = (out_b[:, LANES:2 * LANES] / out_b[:, 0:LANES]).astype(BF16)


def _attention(shift, q, klo, khi, vlo, vhi, cklo, ckhi, cvlo, cvhi, *, tq, streaming):
    b, l, _ = q.shape
    lc = cvlo.shape[2]
    lk = l + lc
    tq = min(tq, l)
    gw = GQA * HEAD_DIM
    kt_spec = pl.BlockSpec((1, 1, LANES, l), lambda bi, h, i: (bi, h, 0, 0))
    kv_spec = pl.BlockSpec((1, 1, l, LANES), lambda bi, h, i: (bi, h, 0, 0))
    ckt_spec = pl.BlockSpec((1, 1, LANES, lc), lambda bi, h, i: (bi, h, 0, 0))
    ckv_spec = pl.BlockSpec((1, 1, lc, LANES), lambda bi, h, i: (bi, h, 0, 0))
    return pl.pallas_call(
        functools.partial(_attn_kernel, l=l, lc=lc, streaming=streaming),
        grid=(b, ATTN_KV_HEADS, l // tq),
        in_specs=([pl.BlockSpec(memory_space=pltpu.SMEM),
                   pl.BlockSpec((1, tq, gw), lambda bi, h, i: (bi, i, h))] + [kt_spec] * 2 + [kv_spec] * 2
                  + [ckt_spec] * 2 + [ckv_spec] * 2),
        out_specs=pl.BlockSpec((1, tq, gw), lambda bi, h, i: (bi, i, h)),
        out_shape=jax.ShapeDtypeStruct((b, l, ATTN_W), BF16),
        scratch_shapes=[pltpu.VMEM((LANES, lk), BF16), pltpu.VMEM((LANES, lk), BF16),
                        pltpu.VMEM((GQA, lk, 2 * LANES), BF16)],
        compiler_params=_cparams("arbitrary", "arbitrary", "arbitrary"),
        name="attn_stream" if streaming else "attn",
    )(shift, q, klo, khi, vlo, vhi, cklo, ckhi, cvlo, cvhi)


def _log_sigmoid(v):
    return jnp.minimum(v, 0.0) - jnp.log(1.0 + jnp.exp(-jnp.abs(v)))


def _ret_kernel(rq_ref, rk_ref, rv_ref, sg_ref, crk_ref, crv_ref, df_ref, db_ref, o_ref,
                m_s, xi_s, zeta_s, kv_s, st_s, *, l, lc):
    n = l // CHUNK
    nc = lc // CHUNK
    lgf = _log_sigmoid(df_ref[...])
    lgb = _log_sigmoid(db_ref[...])
    pos = lax.broadcasted_iota(I32, (CHUNK, LANES), 0).astype(F32)
    row = lax.broadcasted_iota(I32, (CHUNK, CHUNK), 0)
    col = lax.broadcasted_iota(I32, (CHUNK, CHUNK), 1)
    diff = (row - col).astype(F32)
    g_chunk = []
    for p in range(RET_PAIRS):
        cols = slice(p * LANES, (p + 1) * LANES)
        lf, lb = lgf[:, cols], lgb[:, cols]
        xi_s[p, :, 0:LANES] = jnp.exp((pos + 1.0) * lf)
        xi_s[p, :, LANES:] = jnp.exp((CHUNK - pos) * lb)
        zeta_s[p, :, 0:LANES] = jnp.exp((CHUNK - 1.0 - pos) * lf)
        zeta_s[p, :, LANES:] = jnp.exp(pos * lb)
        g_chunk.append((jnp.exp(CHUNK * lf), jnp.exp(CHUNK * lb)))
        for j in range(2):
            h = 2 * p + j
            hf = lgf[:, h * HEAD_DIM:h * HEAD_DIM + 1]
            hb = lgb[:, h * HEAD_DIM:h * HEAD_DIM + 1]
            m_s[p, :, j * CHUNK:(j + 1) * CHUNK] = jnp.where(
                diff > 0, jnp.exp(diff * hf), jnp.where(diff < 0, jnp.exp(-diff * hb), 2.0))

    lane = lax.broadcasted_iota(I32, (CHUNK, LANES), 1)
    low_half = lane < HEAD_DIM
    diag = (lax.broadcasted_iota(I32, (LANES, LANES), 0) // HEAD_DIM
            == lax.broadcasted_iota(I32, (LANES, LANES), 1) // HEAD_DIM)
    seg = jnp.where(diag, 1.0, 0.0).astype(BF16)
    seg2 = jnp.concatenate([seg, seg], axis=0)
    diag2 = jnp.concatenate([diag, diag], axis=0)

    def split_heads(a):
        zero = jnp.zeros_like(a)
        return jnp.concatenate([jnp.where(low_half, a, zero), jnp.where(low_half, zero, a)], axis=0)

    def contrib(k_ref, v_ref, r0, p):
        cols = slice(p * LANES, (p + 1) * LANES)
        kp = k_ref[0, pl.ds(r0, CHUNK), cols].astype(F32)
        kz = jnp.concatenate([kp, kp], axis=1) * zeta_s[p]
        kv = _dot(jnp.transpose(kz).astype(BF16), v_ref[0, pl.ds(r0, CHUNK), cols])
        return jnp.where(diag2, kv, 0.0)

    for c in range(nc):
        for p in range(RET_PAIRS):
            kv_s[c, p] = contrib(crk_ref, crv_ref, c * CHUNK, p)

    def contrib_body(c, carry):
        r0 = pl.multiple_of(c * CHUNK, CHUNK)
        for p in range(RET_PAIRS):
            kv_s[nc + c, p] = contrib(rk_ref, rv_ref, r0, p)
        return carry

    lax.fori_loop(0, n, contrib_body, 0, unroll=4)

    for p in range(RET_PAIRS):
        gf, gb = g_chunk[p]
        sf = jnp.zeros((LANES, LANES), F32)
        sb = jnp.zeros((LANES, LANES), F32)
        for c in range(nc):
            sf = gf * sf + kv_s[c, p, 0:LANES]
            sb = gb * sb + kv_s[nc - 1 - c, p, LANES:]

        def fwd_scan(c, s, p=p, gf=gf):
            st_s[c, p, 0:LANES] = s.astype(BF16)
            return gf * s + kv_s[nc + c, p, 0:LANES]

        def bwd_scan(j, s, p=p, gb=gb):
            c = n - 1 - j
            st_s[c, p, LANES:] = s.astype(BF16)
            return gb * s + kv_s[nc + c, p, LANES:]

        lax.fori_loop(0, n, fwd_scan, sf)
        lax.fori_loop(0, n, bwd_scan, sb)

    def out_body(c, carry):
        r0 = pl.multiple_of(c * CHUNK, CHUNK)
        for p in range(RET_PAIRS):
            cols = slice(p * LANES, (p + 1) * LANES)
            qp = rq_ref[0, pl.ds(r0, CHUNK), cols]
            kp = rk_ref[0, pl.ds(r0, CHUNK), cols]
            vp = rv_ref[0, pl.ds(r0, CHUNK), cols]
            s2 = _dot_nt(qp, split_heads(kp))
            a2 = (s2 * m_s[p]).astype(BF16)
            y = _dot(a2, split_heads(vp))
            qf = qp.astype(F32)
            qx = (jnp.concatenate([qf, qf], axis=1) * xi_s[p]).astype(BF16)
            y += _dot(qx, st_s[c, p])
            hi, lo = _split(y * y)
            ms = _dot(jnp.concatenate([hi, lo], axis=1), seg2) * (1.0 / HEAD_DIM)
            out = y * lax.rsqrt(ms + EPS) * sg_ref[0, pl.ds(r0, CHUNK), cols].astype(F32)
            o_ref[0, pl.ds(r0, CHUNK), cols] = out.astype(BF16)
        return carry

    lax.fori_loop(0, n, out_body, 0, unroll=4)


def _retention(rq, rk, rv, sg, crk, crv, dec_f, dec_b):
    b, l, _ = rq.shape
    lc = crk.shape[1]
    n = l // CHUNK
    nc = lc // CHUNK
    spec = pl.BlockSpec((1, l, RET_W), lambda bi: (bi, 0, 0))
    cspec = pl.BlockSpec((1, lc, RET_W), lambda bi: (bi, 0, 0))
    dspec = pl.BlockSpec((1, RET_W), lambda bi: (0, 0))
    return pl.pallas_call(
        functools.partial(_ret_kernel, l=l, lc=lc),
        grid=(b,),
        in_specs=[spec, spec, spec, spec, cspec, cspec, dspec, dspec],
        out_specs=spec,
        out_shape=jax.ShapeDtypeStruct((b, l, RET_W), BF16),
        scratch_shapes=[pltpu.VMEM((RET_PAIRS, CHUNK, 2 * CHUNK), F32),
                        pltpu.VMEM((RET_PAIRS, CHUNK, 2 * LANES), F32),
                        pltpu.VMEM((RET_PAIRS, CHUNK, 2 * LANES), F32),
                        pltpu.VMEM((nc + n, RET_PAIRS, 2 * LANES, LANES), F32),
                        pltpu.VMEM((n, RET_PAIRS, 2 * LANES, LANES), BF16)],
        compiler_params=_cparams("arbitrary"),
        name="ret",
    )(rq, rk, rv, sg, crk, crv, dec_f, dec_b)


def _out_kernel(attn_ref, ret_ref, x_ref, wa_ref, wr_ref, g1_ref, sh_ref, sc_ref, nw_ref, rhi_ref, rlo_ref,
                x1_ref, hp_ref, lg_ref):
    y = _dot(attn_ref[0], wa_ref[...]) + _dot(ret_ref[0], wr_ref[...])
    x1 = x_ref[0] + g1_ref[0] * y
    x1_ref[0] = x1
    h = x1 * lax.rsqrt(jnp.mean(x1 * x1, axis=-1, keepdims=True) + EPS) * nw_ref[...]
    h = h * (1.0 + sc_ref[0]) + sh_ref[0]
    half = h.shape[1] // 2
    hp_ref[...] = _pack_halves(h[:, :half], h[:, half:])
    h_hi, h_lo = _split(h)
    lg_ref[...] = _dot_nt(rhi_ref[...], h_hi) + _dot_nt(rhi_ref[...], h_lo) + _dot_nt(rlo_ref[...], h_hi)


def _out_projection(attn, ret, x, w_out, g1, sh2, sc2, norm_w, r_hi, r_lo, *, tm):
    assert ATTN_W % RET_W == 0
    wa = wr = w_out
    b, l, d = x.shape
    tm = min(tm, l)
    nt = l // tm
    t = b * l
    mspec = pl.BlockSpec((1, 1, d), lambda bi, i: (bi, 0, 0))
    return pl.pallas_call(
        _out_kernel,
        grid=(b, nt),
        in_specs=[pl.BlockSpec((1, tm, ATTN_W), lambda bi, i: (bi, i, 0)),
                  pl.BlockSpec((1, tm, RET_W), lambda bi, i: (bi, i, 0)),
                  pl.BlockSpec((1, tm, d), lambda bi, i: (bi, i, 0)),
                  pl.BlockSpec((ATTN_W, d), lambda bi, i: (0, 0)),
                  pl.BlockSpec((RET_W, d), lambda bi, i: (ATTN_W // RET_W, 0)),
                  mspec, mspec, mspec,
                  pl.BlockSpec((1, d), lambda bi, i: (0, 0)),
                  pl.BlockSpec((N_EXPERTS, d), lambda bi, i: (0, 0)),
                  pl.BlockSpec((N_EXPERTS, d), lambda bi, i: (0, 0))],
        out_specs=[pl.BlockSpec((1, tm, d), lambda bi, i: (bi, i, 0)),
                   pl.BlockSpec((tm, d // 2), lambda bi, i: (bi * nt + i, 0)),
                   pl.BlockSpec((N_EXPERTS, tm), lambda bi, i: (0, bi * nt + i))],
        out_shape=[jax.ShapeDtypeStruct((b, l, d), F32),
                   jax.ShapeDtypeStruct((t, d // 2), PACKED),
                   jax.ShapeDtypeStruct((N_EXPERTS, t), F32)],
        compiler_params=_cparams("arbitrary", "arbitrary"),
        name="out_proj",
    )(attn, ret, x, wa, wr, g1, sh2, sc2, norm_w, r_hi, r_lo)


def _route_kernel(lg_ref, bias_ref, idx_ref, w_ref, rank_ref, cnt_col_ref, cnt_row_ref, tri_s, col_s, row_s):
    tb = lg_ref.shape[1]
    step = pl.program_id(0)

    @pl.when(step == 0)
    def _():
        r = lax.broadcasted_iota(I32, (tb, tb), 0)
        c = lax.broadcasted_iota(I32, (tb, tb), 1)
        tri_s[...] = jnp.where(r <= c, 1.0, 0.0).astype(BF16)
        col_s[...] = jnp.zeros_like(col_s)
        row_s[...] = jnp.zeros_like(row_s)

    scores = _sigmoid(lg_ref[...])
    biased = scores + bias_ref[...]
    neg = -jnp.inf
    sub = lax.broadcasted_iota(I32, (GROUP_SIZE, tb), 0).astype(F32)

    gscore = []
    for g in range(N_GROUPS):
        blk = biased[g * GROUP_SIZE:(g + 1) * GROUP_SIZE]
        m1 = jnp.max(blk, axis=0, keepdims=True)
        first = jnp.min(jnp.where(blk == m1, sub, float(GROUP_SIZE)), axis=0, keepdims=True)
        m2 = jnp.max(jnp.where(sub == first, neg, blk), axis=0, keepdims=True)
        gscore.append(m1 + m2)
    gs = jnp.concatenate(gscore, axis=0)
    gsub = lax.broadcasted_iota(I32, (N_GROUPS, tb), 0).astype(F32)
    keep = jnp.zeros((N_GROUPS, tb), F32)
    for _ in range(TOPK_GROUPS):
        m = jnp.max(gs, axis=0, keepdims=True)
        first = jnp.min(jnp.where(gs == m, gsub, float(N_GROUPS)), axis=0, keepdims=True)
        sel = gsub == first
        keep = jnp.where(sel, 1.0, keep)
        gs = jnp.where(sel, neg, gs)
    masked = jnp.concatenate(
        [jnp.where(keep[g:g + 1] > 0.0, biased[g * GROUP_SIZE:(g + 1) * GROUP_SIZE], neg)
         for g in range(N_GROUPS)], axis=0)

    esub = lax.broadcasted_iota(I32, (N_EXPERTS, tb), 0).astype(F32)
    sels, idxs, ws = [], [], []
    chosen = jnp.zeros((N_EXPERTS, tb), F32)
    for _ in range(TOP_K):
        m = jnp.max(masked, axis=0, keepdims=True)
        first = jnp.min(jnp.where(masked == m, esub, float(N_EXPERTS)), axis=0, keepdims=True)
        sel = esub == first
        sels.append(sel)
        idxs.append(first)
        ws.append(jnp.sum(jnp.where(sel, scores, 0.0), axis=0, keepdims=True))
        chosen = jnp.where(sel, 1.0, chosen)
        masked = jnp.where(sel, neg, masked)
    wsum = ws[0]
    for k in range(1, TOP_K):
        wsum = wsum + ws[k]
    idx_ref[...] = jnp.concatenate(idxs, axis=0).astype(I32)
    w_ref[...] = jnp.concatenate([wk / wsum * ROUTED_SCALE for wk in ws], axis=0)

    chosen_b = chosen.astype(BF16)
    incl = _dot(chosen_b, tri_s[...])
    before = incl - chosen + col_s[...]
    rank_ref[...] = jnp.concatenate(
        [jnp.sum(jnp.where(sel, before, 0.0), axis=0, keepdims=True) for sel in sels], axis=0).astype(I32)
    col_s[...] = col_s[...] + incl[:, tb - 1:tb]
    row_s[...] = row_s[...] + _dot_nt(jnp.ones((8, tb), BF16), chosen_b)
    cnt_col_ref[...] = col_s[...].astype(I32)
    cnt_row_ref[...] = row_s[...].astype(I32)


def _route(logits_t, bias_col, *, tb):
    e, t = logits_t.shape
    tb = min(tb, t)
    kspec = pl.BlockSpec((TOP_K, tb), lambda i: (0, i))
    return pl.pallas_call(
        _route_kernel,
        grid=(t // tb,),
        in_specs=[pl.BlockSpec((e, tb), lambda i: (0, i)),
                  pl.BlockSpec((e, 1), lambda i: (0, 0))],
        out_specs=[kspec, kspec, kspec,
                   pl.BlockSpec((e, 1), lambda i: (0, 0)),
                   pl.BlockSpec((8, e), lambda i: (0, 0))],
        out_shape=[jax.ShapeDtypeStruct((TOP_K, t), I32),
                   jax.ShapeDtypeStruct((TOP_K, t), F32),
                   jax.ShapeDtypeStruct((TOP_K, t), I32),
                   jax.ShapeDtypeStruct((e, 1), I32),
                   jax.ShapeDtypeStruct((8, e), I32)],
        scratch_shapes=[pltpu.VMEM((tb, tb), BF16), pltpu.VMEM((e, 1), F32), pltpu.VMEM((8, e), F32)],
        compiler_params=_cparams("arbitrary"),
        name="route",
    )(logits_t, bias_col)


def _pad_block(cnt):
    return (cnt + (MOE_BLOCK - 1)) // MOE_BLOCK * MOE_BLOCK


def _max_items(n_blocks):
    return n_blocks // ITEM_BLOCKS + N_EXPERTS


def _dest_kernel(idx_ref, rank_ref, cnt_col_ref, cnt_row_ref, dest_ref, meta_ref, items_ref, start_s):
    tb = idx_ref.shape[1]
    nip = items_ref.shape[1]

    @pl.when(pl.program_id(0) == 0)
    def _():
        pad_col = _pad_block(cnt_col_ref[...])
        pad_row = _pad_block(cnt_row_ref[0:1, :])
        er = lax.broadcasted_iota(I32, (N_EXPERTS, N_EXPERTS), 0)
        ec = lax.broadcasted_iota(I32, (N_EXPERTS, N_EXPERTS), 1)
        start_col = jnp.sum(jnp.where(ec < er, pad_row, 0), axis=1, keepdims=True)
        start_row = jnp.sum(jnp.where(er < ec, pad_col, 0), axis=0, keepdims=True)
        start_s[...] = start_col

        used = jnp.sum(pad_row, axis=1, keepdims=True) // MOE_BLOCK
        meta_ref[...] = jnp.concatenate(
            [cnt_row_ref[0:1, :], start_row, pad_row, jnp.broadcast_to(used, (1, N_EXPERTS)),
             jnp.zeros((4, N_EXPERTS), I32)], axis=0)

        nb_col = pad_col // MOE_BLOCK
        it_col = (nb_col + (ITEM_BLOCKS - 1)) // ITEM_BLOCKS
        it_row = (pad_row // MOE_BLOCK + (ITEM_BLOCKS - 1)) // ITEM_BLOCKS
        it_start = jnp.sum(jnp.where(ec < er, it_row, 0), axis=1, keepdims=True)
        n_items = jnp.sum(it_row, axis=1, keepdims=True)
        lane = lax.broadcasted_iota(I32, (1, nip), 1)
        owner = jnp.sum(jnp.where(it_start + it_col <= lane, 1, 0), axis=0, keepdims=True)
        owner = jnp.minimum(owner, N_EXPERTS - 1)
        onehot = lax.broadcasted_iota(I32, (N_EXPERTS, nip), 0) == owner

        def pick(col):
            return jnp.sum(jnp.where(onehot, col, 0), axis=0, keepdims=True)

        j = lane - pick(it_start)
        block0 = pick(start_col) // MOE_BLOCK + ITEM_BLOCKS * j
        nvalid = jnp.clip(pick(nb_col) - ITEM_BLOCKS * j, 0, ITEM_BLOCKS)
        items_ref[...] = jnp.concatenate(
            [owner, block0, jnp.where(lane < n_items, nvalid, 0), jnp.broadcast_to(n_items, (1, nip)),
             jnp.zeros((4, nip), I32)], axis=0)

    start_col = start_s[...]
    esub = lax.broadcasted_iota(I32, (N_EXPERTS, tb), 0)
    rows = []
    for k in range(TOP_K):
        onehot = esub == idx_ref[k:k + 1, :]
        rows.append(jnp.sum(jnp.where(onehot, start_col, 0), axis=0, keepdims=True) + rank_ref[k:k + 1, :])
    dest_ref[0] = jnp.concatenate(rows, axis=0)


def _destinations(idx_t, rank_t, cnt_col, cnt_row, *, tb, n_blocks):
    _, t = idx_t.shape
    tb = min(tb, t)
    nip = (_max_items(n_blocks) + LANES - 1) // LANES * LANES
    kspec = pl.BlockSpec((TOP_K, tb), lambda i: (0, i))
    return pl.pallas_call(
        _dest_kernel,
        grid=(t // tb,),
        in_specs=[kspec, kspec,
                  pl.BlockSpec((N_EXPERTS, 1), lambda i: (0, 0)),
                  pl.BlockSpec((8, N_EXPERTS), lambda i: (0, 0))],
        out_specs=[pl.BlockSpec((1, TOP_K, tb), lambda i: (i, 0, 0)),
                   pl.BlockSpec((8, N_EXPERTS), lambda i: (0, 0)),
                   pl.BlockSpec((8, nip), lambda i: (0, 0))],
        out_shape=[jax.ShapeDtypeStruct((t // tb, TOP_K, tb), I32),
                   jax.ShapeDtypeStruct((8, N_EXPERTS), I32),
                   jax.ShapeDtypeStruct((8, nip), I32)],
        scratch_shapes=[pltpu.VMEM((N_EXPERTS, 1), I32)],
        compiler_params=_cparams("arbitrary"),
        name="dest",
    )(idx_t, rank_t, cnt_col, cnt_row)


_PAD_BITS = (64, 32, 16, 8)


def _sc_scatter_rows(rows, dest_win, n_out_rows):
    n_win, n_slots, win = dest_win.shape
    width = rows.shape[1]
    info = plsc.get_sparse_core_info()
    n_workers = info.num_cores * info.num_subcores
    per_worker = n_win // n_workers
    assert per_worker * n_workers == n_win and win <= LANES
    mesh = plsc.VectorSubcoreMesh(core_axis_name="c", subcore_axis_name="s")

    def body(rows_hbm, dest_hbm, out_hbm, idx_v, rows_v, sem):
        wid = lax.axis_index("s") * info.num_cores + lax.axis_index("c")

        @pl.loop(0, per_worker)
        def _(j):
            w = wid * per_worker + j
            pltpu.sync_copy(dest_hbm.at[w], idx_v)
            pltpu.sync_copy(rows_hbm.at[pl.ds(w * win, win)], rows_v)
            copies = [pltpu.async_copy(rows_v, out_hbm.at[idx_v.at[k]], sem) for k in range(n_slots)]
            for cp in copies:
                cp.wait()

    return pl.kernel(
        body,
        out_type=jax.ShapeDtypeStruct((n_out_rows, width), rows.dtype),
        mesh=mesh,
        scratch_types=[pltpu.VMEM((n_slots, win), I32), pltpu.VMEM((win, width), rows.dtype),
                       pltpu.SemaphoreType.DMA],
        name="sc_scatter",
    )(rows, dest_win)


def _pad_fill_kernel(meta_ref, xs_in, xs_hbm, zero_s, sem_z, *, e_per_step):
    del xs_in
    step = pl.program_id(0)
    zero_s[...] = jnp.zeros_like(zero_s)

    def tail_copy(c):
        row0 = pl.multiple_of((meta_ref[3, 0] + c) * MOE_BLOCK, MOE_BLOCK)
        return pltpu.make_async_copy(zero_s, xs_hbm.at[pl.ds(row0, MOE_BLOCK)], sem_z)

    @pl.when(step == 0)
    def _():
        for c in range(ITEM_BLOCKS - 1):
            tail_copy(c).start()
        for c in range(ITEM_BLOCKS - 1):
            tail_copy(c).wait()

    def pad_copies(e):
        cnt = meta_ref[0, e]
        off = meta_ref[1, e] + cnt
        rem = meta_ref[2, e] - cnt
        head = rem & (SUBLANES - 1)
        out = []
        for i in range(SUBLANES - 1):
            out.append((i < head,
                        pltpu.make_async_copy(zero_s.at[pl.ds(0, 1)], xs_hbm.at[pl.ds(off + i, 1)], sem_z)))
        off = off + head
        for bit in _PAD_BITS:
            out.append(((rem & bit) != 0,
                        pltpu.make_async_copy(zero_s.at[pl.ds(0, bit)],
                                              xs_hbm.at[pl.ds(pl.multiple_of(off, SUBLANES), bit)], sem_z)))
            off = off + (rem & bit)
        return out

    for j in range(e_per_step):
        for cond, c in pad_copies(step * e_per_step + j):
            pl.when(cond)(c.start)
    for j in range(e_per_step):
        for cond, c in pad_copies(step * e_per_step + j):
            pl.when(cond)(c.wait)


PAD_FILL_EXPERTS_PER_STEP = 8


def _pad_fill(meta, xs):
    half = xs.shape[1]
    return pl.pallas_call(
        functools.partial(_pad_fill_kernel, e_per_step=PAD_FILL_EXPERTS_PER_STEP),
        grid=(N_EXPERTS // PAD_FILL_EXPERTS_PER_STEP,),
        in_specs=[pl.BlockSpec(memory_space=pltpu.SMEM),
                  pl.BlockSpec(memory_space=pl.ANY)],
        out_specs=pl.BlockSpec(memory_space=pl.ANY),
        out_shape=jax.ShapeDtypeStruct(xs.shape, xs.dtype),
        input_output_aliases={1: 0},
        scratch_shapes=[pltpu.VMEM((MOE_BLOCK, half), PACKED), pltpu.SemaphoreType.DMA],
        compiler_params=_cparams("arbitrary"),
        name="pad_fill",
    )(meta, xs)


def _experts_kernel(items_ref, xs_hbm, wg_hbm, wu_hbm, wd_hbm, ys_hbm,
                    xbuf, ybuf, wg_f, wu_f, wd_f, wg_s, wu_s, wd_s, sem_x, sem_y, sem_w):
    n_items = items_ref[3, 0]
    rows = ITEM_BLOCKS * MOE_BLOCK

    def w_copies(e, s):
        return [pltpu.make_async_copy(src.at[e], dst.at[s], sem_w.at[s])
                for src, dst in ((wg_hbm, wg_f), (wu_hbm, wu_f), (wd_hbm, wd_f))]

    def x_copy(item, s):
        row0 = pl.multiple_of(items_ref[1, item] * MOE_BLOCK, MOE_BLOCK)
        return pltpu.make_async_copy(xs_hbm.at[pl.ds(row0, rows)], xbuf.at[s], sem_x.at[s])

    def y_copies(item, s, fn):
        for k in range(ITEM_BLOCKS):
            row0 = pl.multiple_of((items_ref[1, item] + k) * MOE_BLOCK, MOE_BLOCK)
            cp = pltpu.make_async_copy(ybuf.at[s, pl.ds(k * MOE_BLOCK, MOE_BLOCK)],
                                       ys_hbm.at[pl.ds(row0, MOE_BLOCK)], sem_y.at[s])
            pl.when(k < items_ref[2, item])(functools.partial(fn, cp))

    def expert_of(item):
        return items_ref[0, jnp.minimum(item, n_items - 1)]

    def changes_at(item):
        return ((item < n_items) & (expert_of(item) != expert_of(item - 1))).astype(I32)

    @pl.when(n_items > 0)
    def _():
        x_copy(0, 0).start()
        for cp in w_copies(expert_of(0), 0):
            cp.start()

        @pl.when(changes_at(1) == 1)
        def _():
            for cp in w_copies(expert_of(1), 1):
                cp.start()

    def item_body(i, ordinal):
        slot = i % 2
        prev = jnp.maximum(i - 1, 0)
        e = expert_of(i)
        new_expert = (i == 0) | (e != expert_of(prev))
        c1 = changes_at(i + 1)
        c2 = changes_at(i + 2)

        @pl.when(i + 1 < n_items)
        def _():
            x_copy(i + 1, 1 - slot).start()

        @pl.when(new_expert)
        def _():
            wslot = ordinal % WEIGHT_SLOTS
            for cp in w_copies(e, wslot):
                cp.wait()
            wg_s[...] = wg_f[wslot].astype(BF16)
            wu_s[...] = wu_f[wslot].astype(BF16)
            wd_s[...] = wd_f[wslot].astype(BF16)

        @pl.when(c2 == 1)
        def _():
            for cp in w_copies(expert_of(i + 2), (ordinal + c1 + 1) % WEIGHT_SLOTS):
                cp.start()

        x_copy(i, slot).wait()
        xa, xb = _unpack_halves(xbuf[slot])
        xa = xa.astype(BF16)
        xb = xb.astype(BF16)
        half = xa.shape[1]
        g = _dot(xa, wg_s[0:half]) + _dot(xb, wg_s[half:])
        u = _dot(xa, wu_s[0:half]) + _dot(xb, wu_s[half:])
        y = _dot((_silu(g) * u).astype(BF16), wd_s[...])
        ybuf[slot] = _pack_halves(y[:, :half], y[:, half:])
        y_copies(i, slot, lambda cp: cp.start())

        @pl.when(i > 0)
        def _():
            y_copies(prev, 1 - slot, lambda cp: cp.wait())

        return ordinal + c1

    lax.fori_loop(0, n_items, item_body, jnp.int32(0))

    @pl.when(n_items > 0)
    def _():
        last = n_items - 1
        y_copies(last, last % 2, lambda cp: cp.wait())


def _experts(items, xs, w_gate, w_up, w_down, *, n_blocks):
    half = xs.shape[1]
    e, d, f = w_gate.shape
    rows = ITEM_BLOCKS * MOE_BLOCK
    any_spec = pl.BlockSpec(memory_space=pl.ANY)
    return pl.pallas_call(
        _experts_kernel,
        grid_spec=pltpu.PrefetchScalarGridSpec(
            num_scalar_prefetch=1,
            grid=(1,),
            in_specs=[any_spec, any_spec, any_spec, any_spec],
            out_specs=any_spec,
            scratch_shapes=[pltpu.VMEM((2, rows, half), PACKED), pltpu.VMEM((2, rows, half), PACKED),
                            pltpu.VMEM((WEIGHT_SLOTS, d, f), F32), pltpu.VMEM((WEIGHT_SLOTS, d, f), F32),
                            pltpu.VMEM((WEIGHT_SLOTS, f, d), F32),
                            pltpu.VMEM((d, f), BF16), pltpu.VMEM((d, f), BF16), pltpu.VMEM((f, d), BF16),
                            pltpu.SemaphoreType.DMA((2,)), pltpu.SemaphoreType.DMA((2,)),
                            pltpu.SemaphoreType.DMA((WEIGHT_SLOTS,))]),
        out_shape=jax.ShapeDtypeStruct((n_blocks * MOE_BLOCK, half), PACKED),
        compiler_params=_cparams("arbitrary"),
        name="experts",
    )(items, xs, w_gate, w_up, w_down)


def _sc_gather_rows(table, idx):
    n_idx = idx.shape[0]
    width = table.shape[1]
    info = plsc.get_sparse_core_info()
    n_workers = info.num_cores * info.num_subcores
    per_worker = n_idx // n_workers
    assert per_worker * n_workers == n_idx and per_worker % (SC_GATHER_BUFS * SC_WINDOW) == 0
    mesh = plsc.VectorSubcoreMesh(core_axis_name="c", subcore_axis_name="s")

    def body(table_hbm, idx_hbm, out_hbm, idx_v, rows_v, sem_g, sem_o):
        wid = lax.axis_index("s") * info.num_cores + lax.axis_index("c")
        base = wid * per_worker

        @pl.loop(0, per_worker // (SC_GATHER_BUFS * SC_WINDOW))
        def _(it):
            offs = [base + (it * SC_GATHER_BUFS + b) * SC_WINDOW for b in range(SC_GATHER_BUFS)]
            gathers = []
            for b, off in enumerate(offs):
                pltpu.sync_copy(idx_hbm.at[pl.ds(off, SC_WINDOW)], idx_v.at[b])
                gathers.append(pltpu.async_copy(table_hbm.at[idx_v.at[b]], rows_v.at[b], sem_g.at[b]))
            writes = []
            for b, off in enumerate(offs):
                gathers[b].wait()
                writes.append(pltpu.async_copy(rows_v.at[b], out_hbm.at[pl.ds(off, SC_WINDOW)], sem_o.at[b]))
            for cp in writes:
                cp.wait()

    return pl.kernel(
        body,
        out_type=jax.ShapeDtypeStruct((n_idx, width), table.dtype),
        mesh=mesh,
        scratch_types=[pltpu.VMEM((SC_GATHER_BUFS, SC_WINDOW), I32),
                       pltpu.VMEM((SC_GATHER_BUFS, SC_WINDOW, width), table.dtype),
                       pltpu.SemaphoreType.DMA((SC_GATHER_BUFS,)), pltpu.SemaphoreType.DMA((SC_GATHER_BUFS,))],
        name="sc_gather",
    )(table, idx)


def _combine_kernel(hp_ref, x1_ref, g2_ref, w_ref, sgw_ref, suw_ref, sdw_ref, yg_ref, *rest):
    o_ref = rest[-1]
    xa, xb = _unpack_halves(hp_ref[...])
    xa = xa.astype(BF16)
    xb = xb.astype(BF16)
    half = xa.shape[1]
    tb = xa.shape[0]
    g = _dot(xa, sgw_ref[0:half]) + _dot(xb, sgw_ref[half:])
    u = _dot(xa, suw_ref[0:half]) + _dot(xb, suw_ref[half:])
    shared = _dot((_silu(g) * u).astype(BF16), sdw_ref[...])

    w = jnp.transpose(w_ref[...])
    acc_a = jnp.zeros((tb, half), F32)
    acc_b = jnp.zeros((tb, half), F32)
    for k in range(TOP_K):
        ya, yb = _unpack_halves(yg_ref[0, k])
        acc_a += ya * w[:, k:k + 1]
        acc_b += yb * w[:, k:k + 1]
    g2 = g2_ref[0]
    o_ref[:, 0:half] = x1_ref[:, 0:half] + g2[:, 0:half] * (acc_a + shared[:, 0:half])
    o_ref[:, half:] = x1_ref[:, half:] + g2[:, half:] * (acc_b + shared[:, half:])


def _combine(hp, x1, g2, w_tok, sgw, suw, sdw, yg, out_prev, *, tb, seq_len, first_step):
    t, half = hp.shape
    d = 2 * half
    per_seq = seq_len // tb
    f = sgw.shape[1]
    s0 = first_step
    in_specs = [pl.BlockSpec((tb, half), lambda i: (s0 + i, 0)),
                pl.BlockSpec((tb, d), lambda i: (s0 + i, 0)),
                pl.BlockSpec((1, 1, d), lambda i: ((s0 + i) // per_seq, 0, 0)),
                pl.BlockSpec((TOP_K, tb), lambda i: (0, s0 + i)),
                pl.BlockSpec((d, f), lambda i: (0, 0)),
                pl.BlockSpec((d, f), lambda i: (0, 0)),
                pl.BlockSpec((f, d), lambda i: (0, 0)),
                pl.BlockSpec((1, TOP_K, tb, half), lambda i: (i, 0, 0, 0))]
    args = [hp, x1, g2, w_tok, sgw, suw, sdw, yg]
    aliases = {}
    if out_prev is not None:
        in_specs.append(pl.BlockSpec(memory_space=pl.ANY))
        args.append(out_prev)
        aliases = {len(args) - 1: 0}
    return pl.pallas_call(
        _combine_kernel,
        grid=(yg.shape[0],),
        in_specs=in_specs,
        out_specs=pl.BlockSpec((tb, d), lambda i: (s0 + i, 0)),
        out_shape=jax.ShapeDtypeStruct((t, d), F32),
        input_output_aliases=aliases,
        compiler_params=_cparams("arbitrary"),
        name="combine",
    )(*args)


def _rope_tables(l):
    rows = l // GRID_W
    r = jnp.repeat(jnp.arange(rows), GRID_W).astype(F32)
    col = jnp.tile(jnp.arange(GRID_W), rows).astype(F32)
    n_f = HEAD_DIM // 4
    freqs = ROPE_THETA ** (-jnp.arange(n_f, dtype=F32) / n_f)
    ang = jnp.concatenate([r[:, None] * freqs, col[:, None] * freqs], axis=-1)
    ang = jnp.tile(jnp.repeat(ang, 2, axis=1), (1, LANES // HEAD_DIM))
    sign = jnp.where(jnp.arange(LANES) % 2 == 0, -1.0, 1.0).astype(F32)
    return jnp.cos(ang), jnp.sin(ang) * sign


def kernel(x, c, ctx, c_ctx, w_mod, b_mod, norm1_w, norm2_w, w_in, q_norm_w, k_norm_w, ret_decay_fwd,
           ret_decay_bwd, w_out, router_w, router_bias, exp_w_gate, exp_w_up, exp_w_down, sh_w_gate,
           sh_w_up, sh_w_down):
    b, l, d = x.shape
    lc = ctx.shape[1]
    t = b * l
    assert w_mod.shape[0] == 1, "single layer"
    assert l % CHUNK == 0 and lc % CHUNK == 0 and l % GRID_W == 0

    rows = (b + 1 + 7) // 8 * 8
    cc = jnp.zeros((rows, d), F32).at[:b].set(c).at[b].set(c_ctx)
    mod = _modulation(cc, w_mod[0], b_mod[0])
    sh1, sc1, g1, sh2, sc2, g2 = [mod[:b, i * d:(i + 1) * d].reshape(b, 1, d) for i in range(6)]
    shc = mod[b, 0:d].reshape(1, 1, d)
    scc = mod[b, d:2 * d].reshape(1, 1, d)

    wi = w_in[0].astype(BF16)
    qnw = jnp.tile(q_norm_w[0], LANES // HEAD_DIM).reshape(1, LANES)
    knw = jnp.tile(k_norm_w[0], LANES // HEAD_DIM).reshape(1, LANES)
    cos, sin = _rope_tables(l)
    n1 = norm1_w[0].reshape(1, d)

    cklo, ckhi, cvlo, cvhi, crk, crv = _projection(
        ctx, shc, scc, n1, wi, qnw, knw, cos[:lc], sin[:lc], rope=False, with_q=False, tm=TILE_PROJ)
    klo, khi, vlo, vhi, rk, rv, q, rq, sg = _projection(
        x, sh1, sc1, n1, wi, qnw, knw, cos, sin, rope=True, with_q=True, tm=TILE_PROJ)

    bound = (HEAD_DIM * QK_SCALE * LOG2_E * BOUND_MARGIN
             * jnp.max(jnp.abs(q_norm_w[0])) * jnp.max(jnp.abs(k_norm_w[0]))).astype(F32)
    attn_args = (bound.reshape(1), q, klo, khi, vlo, vhi, cklo, ckhi, cvlo, cvhi)
    attn = lax.cond(bound <= MAX_STREAM_SHIFT,
                    functools.partial(_attention, tq=TILE_ATTN_Q, streaming=True),
                    functools.partial(_attention, tq=TILE_ATTN_Q, streaming=False), *attn_args)
    dec_f = jnp.repeat(ret_decay_fwd[0].astype(F32), HEAD_DIM).reshape(1, RET_W)
    dec_b = jnp.repeat(ret_decay_bwd[0].astype(F32), HEAD_DIM).reshape(1, RET_W)
    ret = _retention(rq, rk, rv, sg, crk, crv, dec_f, dec_b)

    wo = w_out[0].astype(BF16)
    r_hi, r_lo = _split(router_w[0].T)
    x1, hp, logits_t = _out_projection(attn, ret, x, wo, g1, sh2, sc2,
                                       norm2_w[0].reshape(1, d), r_hi, r_lo, tm=TILE_OUT)

    idx_t, w_t, rank_t, cnt_col, cnt_row = _route(logits_t, router_bias[0].reshape(N_EXPERTS, 1), tb=TILE_TOKENS)
    n_blocks = -(-(t * TOP_K) // MOE_BLOCK) + N_EXPERTS
    tb = TILE_TOKENS
    dest, meta, items = _destinations(idx_t, rank_t, cnt_col, cnt_row, tb=tb, n_blocks=n_blocks)
    steps, _, tbe = dest.shape
    dest_win = dest.reshape(steps, TOP_K, tbe // SC_WINDOW, SC_WINDOW).transpose(0, 2, 1, 3)
    dest_win = dest_win.reshape(t // SC_WINDOW, TOP_K, SC_WINDOW)
    xs = _sc_scatter_rows(hp, dest_win, (n_blocks + ITEM_BLOCKS - 1) * MOE_BLOCK)
    xs = _pad_fill(meta, xs)
    ys = _experts(items, xs, exp_w_gate[0], exp_w_up[0], exp_w_down[0], n_blocks=n_blocks)
    parts = COMBINE_PARTS if steps % COMBINE_PARTS == 0 else 1
    steps_part = steps // parts
    x1f, w_tok = x1.reshape(t, d), w_t
    sgw, suw, sdw = sh_w_gate[0].astype(BF16), sh_w_up[0].astype(BF16), sh_w_down[0].astype(BF16)
    out = None
    for p in range(parts):
        idx = dest[p * steps_part:(p + 1) * steps_part].reshape(-1)
        yg = _sc_gather_rows(ys, idx).reshape(steps_part, TOP_K, tbe, d // 2)
        out = _combine(hp, x1f, g2, w_tok, sgw, suw, sdw, yg, out, tb=tbe, seq_len=l, first_step=p * steps_part)
    return out.reshape(b, l, d)
```

```python
import functools

import jax
import jax.numpy as jnp
from jax import lax
from jax.experimental import pallas as pl
from jax.experimental.pallas import tpu as pltpu
from jax.experimental.pallas import tpu_sc as plsc

F32 = jnp.float32
BF16 = jnp.bfloat16
I32 = jnp.int32
U32 = jnp.uint32
PACKED = jnp.int32

HEAD_DIM = 64
LANES = 128
SUBLANES = 8
ATTN_HEADS = 8
ATTN_KV_HEADS = 2
GQA = ATTN_HEADS // ATTN_KV_HEADS
RET_HEADS = 8
ATTN_W = ATTN_HEADS * HEAD_DIM
KV_W = ATTN_KV_HEADS * HEAD_DIM
RET_W = RET_HEADS * HEAD_DIM
RET_PAIRS = RET_W // LANES
CHUNK = 128
GRID_W = 64
ROPE_THETA = 10000.0
N_EXPERTS = 256
TOP_K = 8
N_GROUPS = 8
GROUP_SIZE = N_EXPERTS // N_GROUPS
TOPK_GROUPS = 4
ROUTED_SCALE = 2.5
MOE_BLOCK = 128
EPS = 1e-6
QK_SCALE = HEAD_DIM ** -0.5
MOD_SHIFT1, MOD_SCALE1, MOD_GATE1, MOD_SHIFT2, MOD_SCALE2, MOD_GATE2 = range(6)
LOG2_E = 1.4426950408889634
KEY_TILE = 256
BOUND_MARGIN = 1.02
MAX_STREAM_SHIFT = 56.0

OFF_AK = 0
OFF_AV = OFF_AK + KV_W
OFF_RK = OFF_AV + KV_W
OFF_RV = OFF_RK + RET_W
CTX_KV_COLS = OFF_RV + RET_W
OFF_AQ = CTX_KV_COLS
OFF_RQ = OFF_AQ + ATTN_W
OFF_RG = OFF_RQ + RET_W
IN_COLS = OFF_RG + RET_W

V7X_VMEM_BYTES = 64 * 1024 * 1024
VMEM_LIMIT = V7X_VMEM_BYTES * 13 // 16

SUB_ROWS = 256
TILE_PROJ = 1024
TILE_ATTN_Q = 1024
TILE_OUT = 512
TILE_TOKENS = 512
ITEM_BLOCKS = 5
COMBINE_PARTS = 4
SC_GATHER_BUFS = 2
WEIGHT_SLOTS = 3
SC_WINDOW = 64
HI_MASK = 0xFFFF0000


def _cparams(*sem):
    return pltpu.CompilerParams(dimension_semantics=sem, vmem_limit_bytes=VMEM_LIMIT)


def _split(a):
    hi = a.astype(BF16)
    lo = (a - hi.astype(F32)).astype(BF16)
    return hi, lo


def _dot(a, b):
    return jnp.dot(a, b, preferred_element_type=F32)


def _dot_nt(a, b):
    return lax.dot_general(a, b, (((1,), (1,)), ((), ())), preferred_element_type=F32)


def _sigmoid(v):
    return 1.0 / (1.0 + jnp.exp(-v))


def _silu(v):
    return v * _sigmoid(v)


def _pack_halves(a, b):
    ua = lax.bitcast_convert_type(a.astype(BF16).astype(F32), U32)
    ub = lax.bitcast_convert_type(b.astype(BF16).astype(F32), U32)
    return lax.bitcast_convert_type((ua & jnp.uint32(HI_MASK)) | (ub >> 16), PACKED)


def _unpack_halves(p):
    u = lax.bitcast_convert_type(p, U32)
    a = lax.bitcast_convert_type(u & jnp.uint32(HI_MASK), F32)
    b = lax.bitcast_convert_type(u << 16, F32)
    return a, b


def _mod_kernel(c_ref, w_ref, b_ref, o_ref):
    s_hi, s_lo = _split(_silu(c_ref[...]))
    w_hi, w_lo = _split(w_ref[...])
    o_ref[...] = _dot(s_hi, w_hi) + _dot(s_hi, w_lo) + _dot(s_lo, w_hi) + b_ref[...]


def _modulation(cc, w_mod, b_mod):
    rows, d = cc.shape
    n = w_mod.shape[1]
    tn = 768
    return pl.pallas_call(
        _mod_kernel,
        grid=(n // tn,),
        in_specs=[pl.BlockSpec((rows, d), lambda j: (0, 0)),
                  pl.BlockSpec((d, tn), lambda j: (0, j)),
                  pl.BlockSpec((1, tn), lambda j: (0, j))],
        out_specs=pl.BlockSpec((rows, tn), lambda j: (0, j)),
        out_shape=jax.ShapeDtypeStruct((rows, n), F32),
        compiler_params=_cparams("arbitrary"),
        name="mod",
    )(cc, w_mod, b_mod.reshape(1, n))


def _segment_ones():
    r = lax.broadcasted_iota(I32, (LANES, LANES), 0) // HEAD_DIM
    c = lax.broadcasted_iota(I32, (LANES, LANES), 1) // HEAD_DIM
    return jnp.where(r == c, 1.0, 0.0).astype(BF16)


def _head_mean_sq(v, seg):
    hi, lo = _split(v * v)
    return (_dot(hi, seg) + _dot(lo, seg)) * (1.0 / HEAD_DIM)


def _proj_kernel(x_ref, sh_ref, sc_ref, nw_ref, wi_ref, qnw_ref, knw_ref, cos_ref, sin_ref,
                 *out_refs, rope, with_q):
    if with_q:
        klo_ref, khi_ref, vlo_ref, vhi_ref, rk_ref, rv_ref, q_ref, rq_ref, sg_ref = out_refs
    else:
        klo_ref, khi_ref, vlo_ref, vhi_ref, rk_ref, rv_ref = out_refs
    tm = x_ref.shape[1]
    sub = min(tm, SUB_ROWS)
    seg = _segment_ones()
    lane = lax.broadcasted_iota(I32, (sub, LANES), 1)
    low_half = lane < HEAD_DIM
    even = (lane & 1) == 0

    for r0 in range(0, tm, sub):
        rows = slice(r0, r0 + sub)
        x = x_ref[0, rows, :]
        h = x * lax.rsqrt(jnp.mean(x * x, axis=-1, keepdims=True) + EPS) * nw_ref[...]
        h = h * (1.0 + sc_ref[0, 0]) + sh_ref[0, 0]
        z = _dot(h.astype(BF16), wi_ref[...])

        def norm_rope(v, w128, rows=rows):
            v = v * lax.rsqrt(_head_mean_sq(v, seg) + EPS) * w128
            if rope:
                swapped = jnp.where(even, pltpu.roll(v, LANES - 1, 1), pltpu.roll(v, 1, 1))
                v = v * cos_ref[rows, :] + swapped * sin_ref[rows, :]
            return v

        k = norm_rope(z[:, OFF_AK:OFF_AK + KV_W], knw_ref[...])
        ksw = pltpu.roll(k, HEAD_DIM, 1)
        klo_ref[0, 0, :, rows] = jnp.transpose(jnp.where(low_half, k, 0.0)).astype(BF16)
        khi_ref[0, 0, :, rows] = jnp.transpose(jnp.where(low_half, 0.0, ksw)).astype(BF16)
        klo_ref[0, 1, :, rows] = jnp.transpose(jnp.where(low_half, ksw, 0.0)).astype(BF16)
        khi_ref[0, 1, :, rows] = jnp.transpose(jnp.where(low_half, 0.0, k)).astype(BF16)
        v = z[:, OFF_AV:OFF_AV + KV_W]
        vsw = pltpu.roll(v, HEAD_DIM, 1)
        vlo_ref[0, 0, rows, :] = jnp.where(low_half, v, 0.0).astype(BF16)
        vhi_ref[0, 0, rows, :] = jnp.where(low_half, 0.0, vsw).astype(BF16)
        vlo_ref[0, 1, rows, :] = jnp.where(low_half, vsw, 0.0).astype(BF16)
        vhi_ref[0, 1, rows, :] = jnp.where(low_half, 0.0, v).astype(BF16)
        rk_ref[0, rows, :] = (z[:, OFF_RK:OFF_RK + RET_W] * QK_SCALE).astype(BF16)
        rv_ref[0, rows, :] = z[:, OFF_RV:OFF_RV + RET_W].astype(BF16)
        if with_q:
            for j in range(ATTN_W // LANES):
                qj = norm_rope(z[:, OFF_AQ + j * LANES:OFF_AQ + (j + 1) * LANES], qnw_ref[...])
                q_ref[0, rows, j * LANES:(j + 1) * LANES] = (qj * (QK_SCALE * LOG2_E)).astype(BF16)
            rq_ref[0, rows, :] = z[:, OFF_RQ:OFF_RQ + RET_W].astype(BF16)
            sg_ref[0, rows, :] = _silu(z[:, OFF_RG:OFF_RG + RET_W]).astype(BF16)


def _projection(x, modv, mod_row, norm_w, wi_bf16, qnw, knw, cos, sin, *, rope, with_q, tm):
    b, l, d = x.shape
    tm = min(tm, l)
    ncols = IN_COLS if with_q else CTX_KV_COLS
    def mod_spec(j):
        if mod_row is None:
            return pl.BlockSpec((1, 1, 1, d), lambda bi, i: (bi, j, 0, 0))
        return pl.BlockSpec((1, 1, 1, d), lambda bi, i: (mod_row, j, 0, 0))

    kv_shape = jax.ShapeDtypeStruct((b, ATTN_KV_HEADS, l, LANES), BF16)
    kv_spec = pl.BlockSpec((1, ATTN_KV_HEADS, tm, LANES), lambda bi, i: (bi, 0, i, 0))
    kt_shape = jax.ShapeDtypeStruct((b, ATTN_KV_HEADS, LANES, l), BF16)
    kt_spec = pl.BlockSpec((1, ATTN_KV_HEADS, LANES, tm), lambda bi, i: (bi, 0, 0, i))
    w_shape = jax.ShapeDtypeStruct((b, l, RET_W), BF16)
    w_spec = pl.BlockSpec((1, tm, RET_W), lambda bi, i: (bi, i, 0))
    out_shape = [kt_shape] * 2 + [kv_shape] * 2 + [w_shape] * 2
    out_specs = [kt_spec] * 2 + [kv_spec] * 2 + [w_spec] * 2
    if with_q:
        out_shape += [w_shape] * 3
        out_specs += [w_spec] * 3
    return pl.pallas_call(
        functools.partial(_proj_kernel, rope=rope, with_q=with_q),
        grid=(b, l // tm),
        in_specs=[pl.BlockSpec((1, tm, d), lambda bi, i: (bi, i, 0)),
                  mod_spec(MOD_SHIFT1),
                  mod_spec(MOD_SCALE1),
                  pl.BlockSpec((1, d), lambda bi, i: (0, 0)),
                  pl.BlockSpec((d, ncols), lambda bi, i: (0, 0)),
                  pl.BlockSpec((1, LANES), lambda bi, i: (0, 0)),
                  pl.BlockSpec((1, LANES), lambda bi, i: (0, 0)),
                  pl.BlockSpec((tm, LANES), lambda bi, i: (i, 0)),
                  pl.BlockSpec((tm, LANES), lambda bi, i: (i, 0))],
        out_specs=out_specs,
        out_shape=out_shape,
        compiler_params=_cparams("arbitrary", "arbitrary"),
        name="proj_latent" if with_q else "proj_ctx",
    )(x, modv, modv, norm_w, wi_bf16, qnw, knw, cos, sin)


def _attn_kernel(shift_ref, q_ref, klo_ref, khi_ref, vlo_ref, vhi_ref, cklo_ref, ckhi_ref, cvlo_ref, cvhi_ref,
                 o_ref, kl_s, kh_s, va_s, *, l, lc, streaming):
    lk = l + lc

    @pl.when(pl.program_id(2) == 0)
    def _():
        kl_s[:, 0:l] = klo_ref[0, 0]
        kl_s[:, l:lk] = cklo_ref[0, 0]
        kh_s[:, 0:l] = khi_ref[0, 0]
        kh_s[:, l:lk] = ckhi_ref[0, 0]
        lane = lax.broadcasted_iota(I32, (lk, LANES), 1)
        ones_lo = jnp.where(lane < HEAD_DIM, 1.0, 0.0).astype(BF16)
        ones_hi = jnp.where(lane < HEAD_DIM, 0.0, 1.0).astype(BF16)
        for g, (v_ref, cv_ref, ones) in enumerate(((vlo_ref, cvlo_ref, ones_lo), (vhi_ref, cvhi_ref, ones_hi),
                                                   (vlo_ref, cvlo_ref, ones_lo), (vhi_ref, cvhi_ref, ones_hi))):
            v_col, one_col = (0, LANES) if g < 2 else (LANES, 0)
            va_s[g, 0:l, v_col:v_col + LANES] = v_ref[0, 0]
            va_s[g, l:lk, v_col:v_col + LANES] = cv_ref[0, 0]
            va_s[g, :, one_col:one_col + LANES] = ones

    q = q_ref[0]
    acc = []
    for g in range(GQA):
        qp = q[:, (g // 2) * LANES:(g // 2 + 1) * LANES]
        k_s = kl_s if g % 2 == 0 else kh_s
        if streaming:
            shift = shift_ref[0]
            o = None
            for c in range(0, lk, KEY_TILE):
                hi = min(c + KEY_TILE, lk)
                p = jnp.exp2(_dot(qp, k_s[:, c:hi]) - shift).astype(BF16)
                t = _dot(p, va_s[g, c:hi])
                o = t if o is None else o + t
        else:
            s = _dot(qp, k_s[...])
            p = jnp.exp2(s - jnp.max(s, axis=-1, keepdims=True)).astype(BF16)
            o = _dot(p, va_s[g])
        acc.append(o)
    out_a = acc[0] + acc[1]
    out_b = acc[2] + acc[3]
    o_ref[0, :, 0:LANES] = (out_a[:, 0:LANES] / out_a[:, LANES:2 * LANES]).astype(BF16)
    o_ref[0, :, LANES:2 * LANES] = (out_b[:, LANES:2 * LANES] / out_b[:, 0:LANES]).astype(BF16)


def _attention(shift, q, klo, khi, vlo, vhi, cklo, ckhi, cvlo, cvhi, *, tq, streaming):
    b, l, _ = q.shape
    lc = cvlo.shape[2]
    lk = l + lc
    tq = min(tq, l)
    gw = GQA * HEAD_DIM
    kt_spec = pl.BlockSpec((1, 1, LANES, l), lambda bi, h, i: (bi, h, 0, 0))
    kv_spec = pl.BlockSpec((1, 1, l, LANES), lambda bi, h, i: (bi, h, 0, 0))
    ckt_spec = pl.BlockSpec((1, 1, LANES, lc), lambda bi, h, i: (bi, h, 0, 0))
    ckv_spec = pl.BlockSpec((1, 1, lc, LANES), lambda bi, h, i: (bi, h, 0, 0))
    return pl.pallas_call(
        functools.partial(_attn_kernel, l=l, lc=lc, streaming=streaming),
        grid=(b, ATTN_KV_HEADS, l // tq),
        in_specs=([pl.BlockSpec(memory_space=pltpu.SMEM),
                   pl.BlockSpec((1, tq, gw), lambda bi, h, i: (bi, i, h))] + [kt_spec] * 2 + [kv_spec] * 2
                  + [ckt_spec] * 2 + [ckv_spec] * 2),
        out_specs=pl.BlockSpec((1, tq, gw), lambda bi, h, i: (bi, i, h)),
        out_shape=jax.ShapeDtypeStruct((b, l, ATTN_W), BF16),
        scratch_shapes=[pltpu.VMEM((LANES, lk), BF16), pltpu.VMEM((LANES, lk), BF16),
                        pltpu.VMEM((GQA, lk, 2 * LANES), BF16)],
        compiler_params=_cparams("arbitrary", "arbitrary", "arbitrary"),
        name="attn_stream" if streaming else "attn",
    )(shift, q, klo, khi, vlo, vhi, cklo, ckhi, cvlo, cvhi)


def _log_sigmoid(v):
    return jnp.minimum(v, 0.0) - jnp.log(1.0 + jnp.exp(-jnp.abs(v)))


def _ret_kernel(rq_ref, rk_ref, rv_ref, sg_ref, crk_ref, crv_ref, df_ref, db_ref, o_ref,
                m_s, xi_s, zeta_s, kv_s, st_s, *, l, lc):
    n = l // CHUNK
    nc = lc // CHUNK
    lgf = _log_sigmoid(df_ref[...])
    lgb = _log_sigmoid(db_ref[...])
    pos = lax.broadcasted_iota(I32, (CHUNK, LANES), 0).astype(F32)
    row = lax.broadcasted_iota(I32, (CHUNK, CHUNK), 0)
    col = lax.broadcasted_iota(I32, (CHUNK, CHUNK), 1)
    diff = (row - col).astype(F32)
    g_chunk = []
    for p in range(RET_PAIRS):
        cols = slice(p * LANES, (p + 1) * LANES)
        lf, lb = lgf[:, cols], lgb[:, cols]
        xi_s[p, :, 0:LANES] = jnp.exp((pos + 1.0) * lf)
        xi_s[p, :, LANES:] = jnp.exp((CHUNK - pos) * lb)
        zeta_s[p, :, 0:LANES] = jnp.exp((CHUNK - 1.0 - pos) * lf)
        zeta_s[p, :, LANES:] = jnp.exp(pos * lb)
        g_chunk.append((jnp.exp(CHUNK * lf), jnp.exp(CHUNK * lb)))
        for j in range(2):
            h = 2 * p + j
            hf = lgf[:, h * HEAD_DIM:h * HEAD_DIM + 1]
            hb = lgb[:, h * HEAD_DIM:h * HEAD_DIM + 1]
            m_s[p, :, j * CHUNK:(j + 1) * CHUNK] = jnp.where(
                diff > 0, jnp.exp(diff * hf), jnp.where(diff < 0, jnp.exp(-diff * hb), 2.0))

    lane = lax.broadcasted_iota(I32, (CHUNK, LANES), 1)
    low_half = lane < HEAD_DIM
    diag = (lax.broadcasted_iota(I32, (LANES, LANES), 0) // HEAD_DIM
            == lax.broadcasted_iota(I32, (LANES, LANES), 1) // HEAD_DIM)
    seg = jnp.where(diag, 1.0, 0.0).astype(BF16)
    seg2 = jnp.concatenate([seg, seg], axis=0)
    diag2 = jnp.concatenate([diag, diag], axis=0)

    def split_heads(a):
        zero = jnp.zeros_like(a)
        return jnp.concatenate([jnp.where(low_half, a, zero), jnp.where(low_half, zero, a)], axis=0)

    def contrib(k_ref, v_ref, r0, p):
        cols = slice(p * LANES, (p + 1) * LANES)
        kp = k_ref[0, pl.ds(r0, CHUNK), cols].astype(F32)
        kz = jnp.concatenate([kp, kp], axis=1) * zeta_s[p]
        kv = _dot(jnp.transpose(kz).astype(BF16), v_ref[0, pl.ds(r0, CHUNK), cols])
        return jnp.where(diag2, kv, 0.0)

    for c in range(nc):
        for p in range(RET_PAIRS):
            kv_s[c, p] = contrib(crk_ref, crv_ref, c * CHUNK, p)

    def contrib_body(c, carry):
        r0 = pl.multiple_of(c * CHUNK, CHUNK)
        for p in range(RET_PAIRS):
            kv_s[nc + c, p] = contrib(rk_ref, rv_ref, r0, p)
        return carry

    lax.fori_loop(0, n, contrib_body, 0, unroll=4)

    for p in range(RET_PAIRS):
        gf, gb = g_chunk[p]
        sf = jnp.zeros((LANES, LANES), F32)
        sb = jnp.zeros((LANES, LANES), F32)
        for c in range(nc):
            sf = gf * sf + kv_s[c, p, 0:LANES]
            sb = gb * sb + kv_s[nc - 1 - c, p, LANES:]

        def fwd_scan(c, s, p=p, gf=gf):
            st_s[c, p, 0:LANES] = s.astype(BF16)
            return gf * s + kv_s[nc + c, p, 0:LANES]

        def bwd_scan(j, s, p=p, gb=gb):
            c = n - 1 - j
            st_s[c, p, LANES:] = s.astype(BF16)
            return gb * s + kv_s[nc + c, p, LANES:]

        lax.fori_loop(0, n, fwd_scan, sf)
        lax.fori_loop(0, n, bwd_scan, sb)

    def out_body(c, carry):
        r0 = pl.multiple_of(c * CHUNK, CHUNK)
        for p in range(RET_PAIRS):
            cols = slice(p * LANES, (p + 1) * LANES)
            qp = rq_ref[0, pl.ds(r0, CHUNK), cols]
            kp = rk_ref[0, pl.ds(r0, CHUNK), cols]
            vp = rv_ref[0, pl.ds(r0, CHUNK), cols]
            s2 = _dot_nt(qp, split_heads(kp))
            a2 = (s2 * m_s[p]).astype(BF16)
            y = _dot(a2, split_heads(vp))
            qf = qp.astype(F32)
            qx = (jnp.concatenate([qf, qf], axis=1) * xi_s[p]).astype(BF16)
            y += _dot(qx, st_s[c, p])
            hi, lo = _split(y * y)
            ms = _dot(jnp.concatenate([hi, lo], axis=1), seg2) * (1.0 / HEAD_DIM)
            out = y * lax.rsqrt(ms + EPS) * sg_ref[0, pl.ds(r0, CHUNK), cols].astype(F32)
            o_ref[0, pl.ds(r0, CHUNK), cols] = out.astype(BF16)
        return carry

    lax.fori_loop(0, n, out_body, 0, unroll=4)


def _retention(rq, rk, rv, sg, crk, crv, dec_f, dec_b):
    b, l, _ = rq.shape
    lc = crk.shape[1]
    n = l // CHUNK
    nc = lc // CHUNK
    spec = pl.BlockSpec((1, l, RET_W), lambda bi: (bi, 0, 0))
    cspec = pl.BlockSpec((1, lc, RET_W), lambda bi: (bi, 0, 0))
    dspec = pl.BlockSpec((1, RET_W), lambda bi: (0, 0))
    return pl.pallas_call(
        functools.partial(_ret_kernel, l=l, lc=lc),
        grid=(b,),
        in_specs=[spec, spec, spec, spec, cspec, cspec, dspec, dspec],
        out_specs=spec,
        out_shape=jax.ShapeDtypeStruct((b, l, RET_W), BF16),
        scratch_shapes=[pltpu.VMEM((RET_PAIRS, CHUNK, 2 * CHUNK), F32),
                        pltpu.VMEM((RET_PAIRS, CHUNK, 2 * LANES), F32),
                        pltpu.VMEM((RET_PAIRS, CHUNK, 2 * LANES), F32),
                        pltpu.VMEM((nc + n, RET_PAIRS, 2 * LANES, LANES), F32),
                        pltpu.VMEM((n, RET_PAIRS, 2 * LANES, LANES), BF16)],
        compiler_params=_cparams("arbitrary"),
        name="ret",
    )(rq, rk, rv, sg, crk, crv, dec_f, dec_b)


def _out_kernel(attn_ref, ret_ref, x_ref, wa_ref, wr_ref, g1_ref, sh_ref, sc_ref, nw_ref, rhi_ref, rlo_ref,
                x1_ref, hp_ref, lg_ref):
    y = _dot(attn_ref[0], wa_ref[...]) + _dot(ret_ref[0], wr_ref[...])
    x1 = x_ref[0] + g1_ref[0, 0] * y
    x1_ref[0] = x1
    h = x1 * lax.rsqrt(jnp.mean(x1 * x1, axis=-1, keepdims=True) + EPS) * nw_ref[...]
    h = h * (1.0 + sc_ref[0, 0]) + sh_ref[0, 0]
    half = h.shape[1] // 2
    hp_ref[...] = _pack_halves(h[:, :half], h[:, half:])
    h_hi, h_lo = _split(h)
    lg_ref[...] = _dot_nt(rhi_ref[...], h_hi) + _dot_nt(rhi_ref[...], h_lo) + _dot_nt(rlo_ref[...], h_hi)


def _out_projection(attn, ret, x, w_out, modv, norm_w, r_hi, r_lo, *, tm):
    assert ATTN_W % RET_W == 0
    wa = wr = w_out
    g1 = sh2 = sc2 = modv
    b, l, d = x.shape
    tm = min(tm, l)
    nt = l // tm
    t = b * l

    def mod_spec(j):
        return pl.BlockSpec((1, 1, 1, d), lambda bi, i: (bi, j, 0, 0))

    return pl.pallas_call(
        _out_kernel,
        grid=(b, nt),
        in_specs=[pl.BlockSpec((1, tm, ATTN_W), lambda bi, i: (bi, i, 0)),
                  pl.BlockSpec((1, tm, RET_W), lambda bi, i: (bi, i, 0)),
                  pl.BlockSpec((1, tm, d), lambda bi, i: (bi, i, 0)),
                  pl.BlockSpec((ATTN_W, d), lambda bi, i: (0, 0)),
                  pl.BlockSpec((RET_W, d), lambda bi, i: (ATTN_W // RET_W, 0)),
                  mod_spec(MOD_GATE1), mod_spec(MOD_SHIFT2), mod_spec(MOD_SCALE2),
                  pl.BlockSpec((1, d), lambda bi, i: (0, 0)),
                  pl.BlockSpec((N_EXPERTS, d), lambda bi, i: (0, 0)),
                  pl.BlockSpec((N_EXPERTS, d), lambda bi, i: (0, 0))],
        out_specs=[pl.BlockSpec((1, tm, d), lambda bi, i: (bi, i, 0)),
                   pl.BlockSpec((tm, d // 2), lambda bi, i: (bi * nt + i, 0)),
                   pl.BlockSpec((N_EXPERTS, tm), lambda bi, i: (0, bi * nt + i))],
        out_shape=[jax.ShapeDtypeStruct((b, l, d), F32),
                   jax.ShapeDtypeStruct((t, d // 2), PACKED),
                   jax.ShapeDtypeStruct((N_EXPERTS, t), F32)],
        compiler_params=_cparams("arbitrary", "arbitrary"),
        name="out_proj",
    )(attn, ret, x, wa, wr, g1, sh2, sc2, norm_w, r_hi, r_lo)


def _route_kernel(lg_ref, bias_ref, idx_ref, w_ref, rank_ref, cnt_col_ref, cnt_row_ref, tri_s, col_s, row_s):
    tb = lg_ref.shape[1]
    step = pl.program_id(0)

    @pl.when(step == 0)
    def _():
        r = lax.broadcasted_iota(I32, (tb, tb), 0)
        c = lax.broadcasted_iota(I32, (tb, tb), 1)
        tri_s[...] = jnp.where(r <= c, 1.0, 0.0).astype(BF16)
        col_s[...] = jnp.zeros_like(col_s)
        row_s[...] = jnp.zeros_like(row_s)

    scores = _sigmoid(lg_ref[...])
    biased = scores + bias_ref[...]
    neg = -jnp.inf
    sub = lax.broadcasted_iota(I32, (GROUP_SIZE, tb), 0).astype(F32)

    gscore = []
    for g in range(N_GROUPS):
        blk = biased[g * GROUP_SIZE:(g + 1) * GROUP_SIZE]
        m1 = jnp.max(blk, axis=0, keepdims=True)
        first = jnp.min(jnp.where(blk == m1, sub, float(GROUP_SIZE)), axis=0, keepdims=True)
        m2 = jnp.max(jnp.where(sub == first, neg, blk), axis=0, keepdims=True)
        gscore.append(m1 + m2)
    gs = jnp.concatenate(gscore, axis=0)
    gsub = lax.broadcasted_iota(I32, (N_GROUPS, tb), 0).astype(F32)
    keep = jnp.zeros((N_GROUPS, tb), F32)
    for _ in range(TOPK_GROUPS):
        m = jnp.max(gs, axis=0, keepdims=True)
        first = jnp.min(jnp.where(gs == m, gsub, float(N_GROUPS)), axis=0, keepdims=True)
        sel = gsub == first
        keep = jnp.where(sel, 1.0, keep)
        gs = jnp.where(sel, neg, gs)
    masked = jnp.concatenate(
        [jnp.where(keep[g:g + 1] > 0.0, biased[g * GROUP_SIZE:(g + 1) * GROUP_SIZE], neg)
         for g in range(N_GROUPS)], axis=0)

    esub = lax.broadcasted_iota(I32, (N_EXPERTS, tb), 0).astype(F32)
    sels, idxs, ws = [], [], []
    chosen = jnp.zeros((N_EXPERTS, tb), F32)
    for _ in range(TOP_K):
        m = jnp.max(masked, axis=0, keepdims=True)
        first = jnp.min(jnp.where(masked == m, esub, float(N_EXPERTS)), axis=0, keepdims=True)
        sel = esub == first
        sels.append(sel)
        idxs.append(first)
        ws.append(jnp.sum(jnp.where(sel, scores, 0.0), axis=0, keepdims=True))
        chosen = jnp.where(sel, 1.0, chosen)
        masked = jnp.where(sel, neg, masked)
    wsum = ws[0]
    for k in range(1, TOP_K):
        wsum = wsum + ws[k]
    idx_ref[...] = jnp.concatenate(idxs, axis=0).astype(I32)
    w_ref[...] = jnp.concatenate([wk / wsum * ROUTED_SCALE for wk in ws], axis=0)

    chosen_b = chosen.astype(BF16)
    incl = _dot(chosen_b, tri_s[...])
    before = incl - chosen + col_s[...]
    rank_ref[...] = jnp.concatenate(
        [jnp.sum(jnp.where(sel, before, 0.0), axis=0, keepdims=True) for sel in sels], axis=0).astype(I32)
    col_s[...] = col_s[...] + incl[:, tb - 1:tb]
    row_s[...] = row_s[...] + _dot_nt(jnp.ones((8, tb), BF16), chosen_b)
    cnt_col_ref[...] = col_s[...].astype(I32)
    cnt_row_ref[...] = row_s[...].astype(I32)


def _route(logits_t, bias_col, *, tb):
    e, t = logits_t.shape
    tb = min(tb, t)
    kspec = pl.BlockSpec((TOP_K, tb), lambda i: (0, i))
    return pl.pallas_call(
        _route_kernel,
        grid=(t // tb,),
        in_specs=[pl.BlockSpec((e, tb), lambda i: (0, i)),
                  pl.BlockSpec((e, 1), lambda i: (0, 0))],
        out_specs=[kspec, kspec, kspec,
                   pl.BlockSpec((e, 1), lambda i: (0, 0)),
                   pl.BlockSpec((8, e), lambda i: (0, 0))],
        out_shape=[jax.ShapeDtypeStruct((TOP_K, t), I32),
                   jax.ShapeDtypeStruct((TOP_K, t), F32),
                   jax.ShapeDtypeStruct((TOP_K, t), I32),
                   jax.ShapeDtypeStruct((e, 1), I32),
                   jax.ShapeDtypeStruct((8, e), I32)],
        scratch_shapes=[pltpu.VMEM((tb, tb), BF16), pltpu.VMEM((e, 1), F32), pltpu.VMEM((8, e), F32)],
        compiler_params=_cparams("arbitrary"),
        name="route",
    )(logits_t, bias_col)


def _pad_block(cnt):
    return (cnt + (MOE_BLOCK - 1)) // MOE_BLOCK * MOE_BLOCK


def _max_items(n_blocks):
    return n_blocks // ITEM_BLOCKS + N_EXPERTS


def _dest_kernel(idx_ref, rank_ref, cnt_col_ref, cnt_row_ref, dest_ref, meta_ref, items_ref, start_s):
    tb = idx_ref.shape[1]
    nip = items_ref.shape[1]

    @pl.when(pl.program_id(0) == 0)
    def _():
        pad_col = _pad_block(cnt_col_ref[...])
        pad_row = _pad_block(cnt_row_ref[0:1, :])
        er = lax.broadcasted_iota(I32, (N_EXPERTS, N_EXPERTS), 0)
        ec = lax.broadcasted_iota(I32, (N_EXPERTS, N_EXPERTS), 1)
        start_col = jnp.sum(jnp.where(ec < er, pad_row, 0), axis=1, keepdims=True)
        start_row = jnp.sum(jnp.where(er < ec, pad_col, 0), axis=0, keepdims=True)
        start_s[...] = start_col

        used = jnp.sum(pad_row, axis=1, keepdims=True) // MOE_BLOCK
        meta_ref[...] = jnp.concatenate(
            [cnt_row_ref[0:1, :], start_row, pad_row, jnp.broadcast_to(used, (1, N_EXPERTS)),
             jnp.zeros((4, N_EXPERTS), I32)], axis=0)

        nb_col = pad_col // MOE_BLOCK
        it_col = (nb_col + (ITEM_BLOCKS - 1)) // ITEM_BLOCKS
        it_row = (pad_row // MOE_BLOCK + (ITEM_BLOCKS - 1)) // ITEM_BLOCKS
        it_start = jnp.sum(jnp.where(ec < er, it_row, 0), axis=1, keepdims=True)
        n_items = jnp.sum(it_row, axis=1, keepdims=True)
        lane = lax.broadcasted_iota(I32, (1, nip), 1)
        owner = jnp.sum(jnp.where(it_start + it_col <= lane, 1, 0), axis=0, keepdims=True)
        owner = jnp.minimum(owner, N_EXPERTS - 1)
        onehot = lax.broadcasted_iota(I32, (N_EXPERTS, nip), 0) == owner

        def pick(col):
            return jnp.sum(jnp.where(onehot, col, 0), axis=0, keepdims=True)

        j = lane - pick(it_start)
        block0 = pick(start_col) // MOE_BLOCK + ITEM_BLOCKS * j
        nvalid = jnp.clip(pick(nb_col) - ITEM_BLOCKS * j, 0, ITEM_BLOCKS)
        items_ref[...] = jnp.concatenate(
            [owner, block0, jnp.where(lane < n_items, nvalid, 0), jnp.broadcast_to(n_items, (1, nip)),
             jnp.zeros((4, nip), I32)], axis=0)

    start_col = start_s[...]
    esub = lax.broadcasted_iota(I32, (N_EXPERTS, tb), 0)
    rows = []
    for k in range(TOP_K):
        onehot = esub == idx_ref[k:k + 1, :]
        rows.append(jnp.sum(jnp.where(onehot, start_col, 0), axis=0, keepdims=True) + rank_ref[k:k + 1, :])
    dest_ref[0] = jnp.concatenate(rows, axis=0)


def _destinations(idx_t, rank_t, cnt_col, cnt_row, *, tb, n_blocks):
    _, t = idx_t.shape
    tb = min(tb, t)
    nip = (_max_items(n_blocks) + LANES - 1) // LANES * LANES
    kspec = pl.BlockSpec((TOP_K, tb), lambda i: (0, i))
    return pl.pallas_call(
        _dest_kernel,
        grid=(t // tb,),
        in_specs=[kspec, kspec,
                  pl.BlockSpec((N_EXPERTS, 1), lambda i: (0, 0)),
                  pl.BlockSpec((8, N_EXPERTS), lambda i: (0, 0))],
        out_specs=[pl.BlockSpec((1, TOP_K, tb), lambda i: (i, 0, 0)),
                   pl.BlockSpec((8, N_EXPERTS), lambda i: (0, 0)),
                   pl.BlockSpec((8, nip), lambda i: (0, 0))],
        out_shape=[jax.ShapeDtypeStruct((t // tb, TOP_K, tb), I32),
                   jax.ShapeDtypeStruct((8, N_EXPERTS), I32),
                   jax.ShapeDtypeStruct((8, nip), I32)],
        scratch_shapes=[pltpu.VMEM((N_EXPERTS, 1), I32)],
        compiler_params=_cparams("arbitrary"),
        name="dest",
    )(idx_t, rank_t, cnt_col, cnt_row)


_PAD_BITS = (64, 32, 16, 8)


def _sc_scatter_rows(rows, dest_win, n_out_rows):
    n_win, n_slots, win = dest_win.shape
    width = rows.shape[1]
    info = plsc.get_sparse_core_info()
    n_workers = info.num_cores * info.num_subcores
    per_worker = n_win // n_workers
    assert per_worker * n_workers == n_win and win <= LANES
    mesh = plsc.VectorSubcoreMesh(core_axis_name="c", subcore_axis_name="s")

    def body(rows_hbm, dest_hbm, out_hbm, idx_v, rows_v, sem):
        wid = lax.axis_index("s") * info.num_cores + lax.axis_index("c")

        @pl.loop(0, per_worker)
        def _(j):
            w = wid * per_worker + j
            pltpu.sync_copy(dest_hbm.at[w], idx_v)
            pltpu.sync_copy(rows_hbm.at[pl.ds(w * win, win)], rows_v)
            copies = [pltpu.async_copy(rows_v, out_hbm.at[idx_v.at[k]], sem) for k in range(n_slots)]
            for cp in copies:
                cp.wait()

    return pl.kernel(
        body,
        out_type=jax.ShapeDtypeStruct((n_out_rows, width), rows.dtype),
        mesh=mesh,
        scratch_types=[pltpu.VMEM((n_slots, win), I32), pltpu.VMEM((win, width), rows.dtype),
                       pltpu.SemaphoreType.DMA],
        name="sc_scatter",
    )(rows, dest_win)


def _pad_fill_kernel(meta_ref, xs_in, xs_hbm, zero_s, sem_z, *, e_per_step):
    del xs_in
    step = pl.program_id(0)
    zero_s[...] = jnp.zeros_like(zero_s)

    def tail_copy(c):
        row0 = pl.multiple_of((meta_ref[3, 0] + c) * MOE_BLOCK, MOE_BLOCK)
        return pltpu.make_async_copy(zero_s, xs_hbm.at[pl.ds(row0, MOE_BLOCK)], sem_z)

    @pl.when(step == 0)
    def _():
        for c in range(ITEM_BLOCKS - 1):
            tail_copy(c).start()
        for c in range(ITEM_BLOCKS - 1):
            tail_copy(c).wait()

    def pad_copies(e):
        cnt = meta_ref[0, e]
        off = meta_ref[1, e] + cnt
        rem = meta_ref[2, e] - cnt
        head = rem & (SUBLANES - 1)
        out = []
        for i in range(SUBLANES - 1):
            out.append((i < head,
                        pltpu.make_async_copy(zero_s.at[pl.ds(0, 1)], xs_hbm.at[pl.ds(off + i, 1)], sem_z)))
        off = off + head
        for bit in _PAD_BITS:
            out.append(((rem & bit) != 0,
                        pltpu.make_async_copy(zero_s.at[pl.ds(0, bit)],
                                              xs_hbm.at[pl.ds(pl.multiple_of(off, SUBLANES), bit)], sem_z)))
            off = off + (rem & bit)
        return out

    for j in range(e_per_step):
        for cond, c in pad_copies(step * e_per_step + j):
            pl.when(cond)(c.start)
    for j in range(e_per_step):
        for cond, c in pad_copies(step * e_per_step + j):
            pl.when(cond)(c.wait)


PAD_FILL_EXPERTS_PER_STEP = 8


def _pad_fill(meta, xs):
    half = xs.shape[1]
    return pl.pallas_call(
        functools.partial(_pad_fill_kernel, e_per_step=PAD_FILL_EXPERTS_PER_STEP),
        grid=(N_EXPERTS // PAD_FILL_EXPERTS_PER_STEP,),
        in_specs=[pl.BlockSpec(memory_space=pltpu.SMEM),
                  pl.BlockSpec(memory_space=pl.ANY)],
        out_specs=pl.BlockSpec(memory_space=pl.ANY),
        out_shape=jax.ShapeDtypeStruct(xs.shape, xs.dtype),
        input_output_aliases={1: 0},
        scratch_shapes=[pltpu.VMEM((MOE_BLOCK, half), PACKED), pltpu.SemaphoreType.DMA],
        compiler_params=_cparams("arbitrary"),
        name="pad_fill",
    )(meta, xs)


def _experts_kernel(items_ref, xs_hbm, wg_hbm, wu_hbm, wd_hbm, ys_hbm,
                    xbuf, ybuf, wg_f, wu_f, wd_f, wg_s, wu_s, wd_s, sem_x, sem_y, sem_w):
    n_items = items_ref[3, 0]
    rows = ITEM_BLOCKS * MOE_BLOCK

    def w_copies(e, s):
        return [pltpu.make_async_copy(src.at[e], dst.at[s], sem_w.at[s])
                for src, dst in ((wg_hbm, wg_f), (wu_hbm, wu_f), (wd_hbm, wd_f))]

    def x_copy(item, s):
        row0 = pl.multiple_of(items_ref[1, item] * MOE_BLOCK, MOE_BLOCK)
        return pltpu.make_async_copy(xs_hbm.at[pl.ds(row0, rows)], xbuf.at[s], sem_x.at[s])

    def y_copies(item, s, fn):
        for k in range(ITEM_BLOCKS):
            row0 = pl.multiple_of((items_ref[1, item] + k) * MOE_BLOCK, MOE_BLOCK)
            cp = pltpu.make_async_copy(ybuf.at[s, pl.ds(k * MOE_BLOCK, MOE_BLOCK)],
                                       ys_hbm.at[pl.ds(row0, MOE_BLOCK)], sem_y.at[s])
            pl.when(k < items_ref[2, item])(functools.partial(fn, cp))

    def expert_of(item):
        return items_ref[0, jnp.minimum(item, n_items - 1)]

    def changes_at(item):
        return ((item < n_items) & (expert_of(item) != expert_of(item - 1))).astype(I32)

    @pl.when(n_items > 0)
    def _():
        x_copy(0, 0).start()
        for cp in w_copies(expert_of(0), 0):
            cp.start()

        @pl.when(changes_at(1) == 1)
        def _():
            for cp in w_copies(expert_of(1), 1):
                cp.start()

    def item_body(i, ordinal):
        slot = i % 2
        prev = jnp.maximum(i - 1, 0)
        e = expert_of(i)
        new_expert = (i == 0) | (e != expert_of(prev))
        c1 = changes_at(i + 1)
        c2 = changes_at(i + 2)

        @pl.when(i + 1 < n_items)
        def _():
            x_copy(i + 1, 1 - slot).start()

        @pl.when(new_expert)
        def _():
            wslot = ordinal % WEIGHT_SLOTS
            for cp in w_copies(e, wslot):
                cp.wait()
            wg_s[...] = wg_f[wslot].astype(BF16)
            wu_s[...] = wu_f[wslot].astype(BF16)
            wd_s[...] = wd_f[wslot].astype(BF16)

        @pl.when(c2 == 1)
        def _():
            for cp in w_copies(expert_of(i + 2), (ordinal + c1 + 1) % WEIGHT_SLOTS):
                cp.start()

        x_copy(i, slot).wait()
        xa, xb = _unpack_halves(xbuf[slot])
        xa = xa.astype(BF16)
        xb = xb.astype(BF16)
        half = xa.shape[1]
        g = _dot(xa, wg_s[0:half]) + _dot(xb, wg_s[half:])
        u = _dot(xa, wu_s[0:half]) + _dot(xb, wu_s[half:])
        y = _dot((_silu(g) * u).astype(BF16), wd_s[...])
        ybuf[slot] = _pack_halves(y[:, :half], y[:, half:])
        y_copies(i, slot, lambda cp: cp.start())

        @pl.when(i > 0)
        def _():
            y_copies(prev, 1 - slot, lambda cp: cp.wait())

        return ordinal + c1

    lax.fori_loop(0, n_items, item_body, jnp.int32(0))

    @pl.when(n_items > 0)
    def _():
        last = n_items - 1
        y_copies(last, last % 2, lambda cp: cp.wait())


def _experts(items, xs, w_gate, w_up, w_down, *, n_blocks):
    half = xs.shape[1]
    e, d, f = w_gate.shape
    rows = ITEM_BLOCKS * MOE_BLOCK
    any_spec = pl.BlockSpec(memory_space=pl.ANY)
    return pl.pallas_call(
        _experts_kernel,
        grid_spec=pltpu.PrefetchScalarGridSpec(
            num_scalar_prefetch=1,
            grid=(1,),
            in_specs=[any_spec, any_spec, any_spec, any_spec],
            out_specs=any_spec,
            scratch_shapes=[pltpu.VMEM((2, rows, half), PACKED), pltpu.VMEM((2, rows, half), PACKED),
                            pltpu.VMEM((WEIGHT_SLOTS, d, f), F32), pltpu.VMEM((WEIGHT_SLOTS, d, f), F32),
                            pltpu.VMEM((WEIGHT_SLOTS, f, d), F32),
                            pltpu.VMEM((d, f), BF16), pltpu.VMEM((d, f), BF16), pltpu.VMEM((f, d), BF16),
                            pltpu.SemaphoreType.DMA((2,)), pltpu.SemaphoreType.DMA((2,)),
                            pltpu.SemaphoreType.DMA((WEIGHT_SLOTS,))]),
        out_shape=jax.ShapeDtypeStruct((n_blocks * MOE_BLOCK, half), PACKED),
        compiler_params=_cparams("arbitrary"),
        name="experts",
    )(items, xs, w_gate, w_up, w_down)


def _sc_gather_rows(table, idx):
    n_idx = idx.shape[0]
    width = table.shape[1]
    info = plsc.get_sparse_core_info()
    n_workers = info.num_cores * info.num_subcores
    per_worker = n_idx // n_workers
    assert per_worker * n_workers == n_idx and per_worker % (SC_GATHER_BUFS * SC_WINDOW) == 0
    mesh = plsc.VectorSubcoreMesh(core_axis_name="c", subcore_axis_name="s")

    def body(table_hbm, idx_hbm, out_hbm, idx_v, rows_v, sem_g, sem_o):
        wid = lax.axis_index("s") * info.num_cores + lax.axis_index("c")
        base = wid * per_worker

        @pl.loop(0, per_worker // (SC_GATHER_BUFS * SC_WINDOW))
        def _(it):
            offs = [base + (it * SC_GATHER_BUFS + b) * SC_WINDOW for b in range(SC_GATHER_BUFS)]
            gathers = []
            for b, off in enumerate(offs):
                pltpu.sync_copy(idx_hbm.at[pl.ds(off, SC_WINDOW)], idx_v.at[b])
                gathers.append(pltpu.async_copy(table_hbm.at[idx_v.at[b]], rows_v.at[b], sem_g.at[b]))
            writes = []
            for b, off in enumerate(offs):
                gathers[b].wait()
                writes.append(pltpu.async_copy(rows_v.at[b], out_hbm.at[pl.ds(off, SC_WINDOW)], sem_o.at[b]))
            for cp in writes:
                cp.wait()

    return pl.kernel(
        body,
        out_type=jax.ShapeDtypeStruct((n_idx, width), table.dtype),
        mesh=mesh,
        scratch_types=[pltpu.VMEM((SC_GATHER_BUFS, SC_WINDOW), I32),
                       pltpu.VMEM((SC_GATHER_BUFS, SC_WINDOW, width), table.dtype),
                       pltpu.SemaphoreType.DMA((SC_GATHER_BUFS,)), pltpu.SemaphoreType.DMA((SC_GATHER_BUFS,))],
        name="sc_gather",
    )(table, idx)


def _combine_kernel(hp_ref, x1_ref, g2_ref, w_ref, sgw_ref, suw_ref, sdw_ref, yg_ref, *rest):
    o_ref = rest[-1]
    xa, xb = _unpack_halves(hp_ref[...])
    xa = xa.astype(BF16)
    xb = xb.astype(BF16)
    half = xa.shape[1]
    tb = xa.shape[0]
    g = _dot(xa, sgw_ref[0:half]) + _dot(xb, sgw_ref[half:])
    u = _dot(xa, suw_ref[0:half]) + _dot(xb, suw_ref[half:])
    shared = _dot((_silu(g) * u).astype(BF16), sdw_ref[...])

    w = jnp.transpose(w_ref[...])
    acc_a = jnp.zeros((tb, half), F32)
    acc_b = jnp.zeros((tb, half), F32)
    for k in range(TOP_K):
        ya, yb = _unpack_halves(yg_ref[0, k])
        acc_a += ya * w[:, k:k + 1]
        acc_b += yb * w[:, k:k + 1]
    g2 = g2_ref[0, 0]
    o_ref[:, 0:half] = x1_ref[:, 0:half] + g2[:, 0:half] * (acc_a + shared[:, 0:half])
    o_ref[:, half:] = x1_ref[:, half:] + g2[:, half:] * (acc_b + shared[:, half:])


def _combine(hp, x1, modv, w_tok, sgw, suw, sdw, yg, out_prev, *, tb, seq_len, first_step):
    t, half = hp.shape
    d = 2 * half
    per_seq = seq_len // tb
    f = sgw.shape[1]
    s0 = first_step
    in_specs = [pl.BlockSpec((tb, half), lambda i: (s0 + i, 0)),
                pl.BlockSpec((tb, d), lambda i: (s0 + i, 0)),
                pl.BlockSpec((1, 1, 1, d), lambda i: ((s0 + i) // per_seq, MOD_GATE2, 0, 0)),
                pl.BlockSpec((TOP_K, tb), lambda i: (0, s0 + i)),
                pl.BlockSpec((d, f), lambda i: (0, 0)),
                pl.BlockSpec((d, f), lambda i: (0, 0)),
                pl.BlockSpec((f, d), lambda i: (0, 0)),
                pl.BlockSpec((1, TOP_K, tb, half), lambda i: (i, 0, 0, 0))]
    args = [hp, x1, modv, w_tok, sgw, suw, sdw, yg]
    aliases = {}
    if out_prev is not None:
        in_specs.append(pl.BlockSpec(memory_space=pl.ANY))
        args.append(out_prev)
        aliases = {len(args) - 1: 0}
    return pl.pallas_call(
        _combine_kernel,
        grid=(yg.shape[0],),
        in_specs=in_specs,
        out_specs=pl.BlockSpec((tb, d), lambda i: (s0 + i, 0)),
        out_shape=jax.ShapeDtypeStruct((t, d), F32),
        input_output_aliases=aliases,
        compiler_params=_cparams("arbitrary"),
        name="combine",
    )(*args)


def _rope_tables(l):
    rows = l // GRID_W
    r = jnp.repeat(jnp.arange(rows), GRID_W).astype(F32)
    col = jnp.tile(jnp.arange(GRID_W), rows).astype(F32)
    n_f = HEAD_DIM // 4
    freqs = ROPE_THETA ** (-jnp.arange(n_f, dtype=F32) / n_f)
    ang = jnp.concatenate([r[:, None] * freqs, col[:, None] * freqs], axis=-1)
    ang = jnp.tile(jnp.repeat(ang, 2, axis=1), (1, LANES // HEAD_DIM))
    sign = jnp.where(jnp.arange(LANES) % 2 == 0, -1.0, 1.0).astype(F32)
    return jnp.cos(ang), jnp.sin(ang) * sign


def kernel(x, c, ctx, c_ctx, w_mod, b_mod, norm1_w, norm2_w, w_in, q_norm_w, k_norm_w, ret_decay_fwd,
           ret_decay_bwd, w_out, router_w, router_bias, exp_w_gate, exp_w_up, exp_w_down, sh_w_gate,
           sh_w_up, sh_w_down):
    b, l, d = x.shape
    lc = ctx.shape[1]
    t = b * l
    assert w_mod.shape[0] == 1, "single layer"
    assert l % CHUNK == 0 and lc % CHUNK == 0 and l % GRID_W == 0

    rows = (b + 1 + 7) // 8 * 8
    cc = jnp.zeros((rows, d), F32).at[:b].set(c).at[b].set(c_ctx)
    mod = _modulation(cc, w_mod[0], b_mod[0])
    modv = mod.reshape(rows, 6, 1, d)

    wi = w_in[0].astype(BF16)
    qnw = jnp.tile(q_norm_w[0], LANES // HEAD_DIM).reshape(1, LANES)
    knw = jnp.tile(k_norm_w[0], LANES // HEAD_DIM).reshape(1, LANES)
    cos, sin = _rope_tables(l)
    n1 = norm1_w[0].reshape(1, d)

    cklo, ckhi, cvlo, cvhi, crk, crv = _projection(
        ctx, modv, b, n1, wi, qnw, knw, cos[:lc], sin[:lc], rope=False, with_q=False, tm=TILE_PROJ)
    klo, khi, vlo, vhi, rk, rv, q, rq, sg = _projection(
        x, modv, None, n1, wi, qnw, knw, cos, sin, rope=True, with_q=True, tm=TILE_PROJ)

    bound = (HEAD_DIM * QK_SCALE * LOG2_E * BOUND_MARGIN
             * jnp.max(jnp.abs(q_norm_w[0])) * jnp.max(jnp.abs(k_norm_w[0]))).astype(F32)
    attn_args = (bound.reshape(1), q, klo, khi, vlo, vhi, cklo, ckhi, cvlo, cvhi)
    attn = lax.cond(bound <= MAX_STREAM_SHIFT,
                    functools.partial(_attention, tq=TILE_ATTN_Q, streaming=True),
                    functools.partial(_attention, tq=TILE_ATTN_Q, streaming=False), *attn_args)
    dec_f = jnp.repeat(ret_decay_fwd[0].astype(F32), HEAD_DIM).reshape(1, RET_W)
    dec_b = jnp.repeat(ret_decay_bwd[0].astype(F32), HEAD_DIM).reshape(1, RET_W)
    ret = _retention(rq, rk, rv, sg, crk, crv, dec_f, dec_b)

    wo = w_out[0].astype(BF16)
    r_hi, r_lo = _split(router_w[0].T)
    x1, hp, logits_t = _out_projection(attn, ret, x, wo, modv, norm2_w[0].reshape(1, d), r_hi, r_lo, tm=TILE_OUT)

    idx_t, w_t, rank_t, cnt_col, cnt_row = _route(logits_t, router_bias[0].reshape(N_EXPERTS, 1), tb=TILE_TOKENS)
    n_blocks = -(-(t * TOP_K) // MOE_BLOCK) + N_EXPERTS
    tb = TILE_TOKENS
    dest, meta, items = _destinations(idx_t, rank_t, cnt_col, cnt_row, tb=tb, n_blocks=n_blocks)
    steps, _, tbe = dest.shape
    dest_win = dest.reshape(steps, TOP_K, tbe // SC_WINDOW, SC_WINDOW).transpose(0, 2, 1, 3)
    dest_win = dest_win.reshape(t // SC_WINDOW, TOP_K, SC_WINDOW)
    xs = _sc_scatter_rows(hp, dest_win, (n_blocks + ITEM_BLOCKS - 1) * MOE_BLOCK)
    xs = _pad_fill(meta, xs)
    ys = _experts(items, xs, exp_w_gate[0], exp_w_up[0], exp_w_down[0], n_blocks=n_blocks)
    parts = COMBINE_PARTS if steps % COMBINE_PARTS == 0 else 1
    steps_part = steps // parts
    x1f, w_tok = x1.reshape(t, d), w_t
    sgw, suw, sdw = sh_w_gate[0].astype(BF16), sh_w_up[0].astype(BF16), sh_w_down[0].astype(BF16)
    out = None
    for p in range(parts):
        idx = dest[p * steps_part:(p + 1) * steps_part].reshape(-1)
        yg = _sc_gather_rows(ys, idx).reshape(steps_part, TOP_K, tbe, d // 2)
        out = _combine(hp, x1f, modv, w_tok, sgw, suw, sdw, yg, out, tb=tbe, seq_len=l, first_step=p * steps_part)
    return out.reshape(b, l, d)
```

```python
import functools

import jax
import jax.numpy as jnp
from jax import lax
from jax.experimental import pallas as pl
from jax.experimental.pallas import tpu as pltpu
from jax.experimental.pallas import tpu_sc as plsc

F32 = jnp.float32
BF16 = jnp.bfloat16
I32 = jnp.int32
U32 = jnp.uint32
PACKED = jnp.int32

HEAD_DIM = 64
LANES = 128
SUBLANES = 8
ATTN_HEADS = 8
ATTN_KV_HEADS = 2
GQA = ATTN_HEADS // ATTN_KV_HEADS
RET_HEADS = 8
ATTN_W = ATTN_HEADS * HEAD_DIM
KV_W = ATTN_KV_HEADS * HEAD_DIM
RET_W = RET_HEADS * HEAD_DIM
RET_PAIRS = RET_W // LANES
CHUNK = 128
GRID_W = 64
ROPE_THETA = 10000.0
N_EXPERTS = 256
TOP_K = 8
N_GROUPS = 8
GROUP_SIZE = N_EXPERTS // N_GROUPS
TOPK_GROUPS = 4
ROUTED_SCALE = 2.5
MOE_BLOCK = 128
EPS = 1e-6
QK_SCALE = HEAD_DIM ** -0.5
LOG2_E = 1.4426950408889634
KEY_TILE = 256
BOUND_MARGIN = 1.02
MAX_STREAM_SHIFT = 56.0

OFF_AK = 0
OFF_AV = OFF_AK + KV_W
OFF_RK = OFF_AV + KV_W
OFF_RV = OFF_RK + RET_W
CTX_KV_COLS = OFF_RV + RET_W
OFF_AQ = CTX_KV_COLS
OFF_RQ = OFF_AQ + ATTN_W
OFF_RG = OFF_RQ + RET_W
IN_COLS = OFF_RG + RET_W

V7X_VMEM_BYTES = 64 * 1024 * 1024
VMEM_LIMIT = V7X_VMEM_BYTES * 13 // 16

SUB_ROWS = 256
TILE_PROJ = 1024
TILE_ATTN_Q = 1024
TILE_OUT = 512
TILE_TOKENS = 512
ITEM_BLOCKS = 5
COMBINE_PARTS = 4
SC_GATHER_BUFS = 2
WEIGHT_SLOTS = 3
SC_WINDOW = 64
HI_MASK = 0xFFFF0000


def _cparams(*sem):
    return pltpu.CompilerParams(dimension_semantics=sem, vmem_limit_bytes=VMEM_LIMIT)


def _split(a):
    hi = a.astype(BF16)
    lo = (a - hi.astype(F32)).astype(BF16)
    return hi, lo


def _dot(a, b):
    return jnp.dot(a, b, preferred_element_type=F32)


def _dot_nt(a, b):
    return lax.dot_general(a, b, (((1,), (1,)), ((), ())), preferred_element_type=F32)


def _sigmoid(v):
    return 1.0 / (1.0 + jnp.exp(-v))


def _silu(v):
    return v * _sigmoid(v)


def _pack_halves(a, b):
    ua = lax.bitcast_convert_type(a.astype(BF16).astype(F32), U32)
    ub = lax.bitcast_convert_type(b.astype(BF16).astype(F32), U32)
    return lax.bitcast_convert_type((ua & jnp.uint32(HI_MASK)) | (ub >> 16), PACKED)


def _unpack_halves(p):
    u = lax.bitcast_convert_type(p, U32)
    a = lax.bitcast_convert_type(u & jnp.uint32(HI_MASK), F32)
    b = lax.bitcast_convert_type(u << 16, F32)
    return a, b


def _mod_kernel(c_ref, w_ref, b_ref, o_ref):
    s_hi, s_lo = _split(_silu(c_ref[...]))
    w_hi, w_lo = _split(w_ref[...])
    o_ref[...] = _dot(s_hi, w_hi) + _dot(s_hi, w_lo) + _dot(s_lo, w_hi) + b_ref[...]


def _modulation(cc, w_mod, b_mod):
    rows, d = cc.shape
    n = w_mod.shape[1]
    tn = 768
    return pl.pallas_call(
        _mod_kernel,
        grid=(n // tn,),
        in_specs=[pl.BlockSpec((rows, d), lambda j: (0, 0)),
                  pl.BlockSpec((d, tn), lambda j: (0, j)),
                  pl.BlockSpec((1, tn), lambda j: (0, j))],
        out_specs=pl.BlockSpec((rows, tn), lambda j: (0, j)),
        out_shape=jax.ShapeDtypeStruct((rows, n), F32),
        compiler_params=_cparams("arbitrary"),
        name="mod",
    )(cc, w_mod, b_mod.reshape(1, n))


def _segment_ones():
    r = lax.broadcasted_iota(I32, (LANES, LANES), 0) // HEAD_DIM
    c = lax.broadcasted_iota(I32, (LANES, LANES), 1) // HEAD_DIM
    return jnp.where(r == c, 1.0, 0.0).astype(BF16)


def _head_mean_sq(v, seg):
    hi, lo = _split(v * v)
    return (_dot(hi, seg) + _dot(lo, seg)) * (1.0 / HEAD_DIM)


def _proj_kernel(x_ref, sh_ref, sc_ref, nw_ref, wi_ref, qnw_ref, knw_ref, cos_ref, sin_ref,
                 *out_refs, rope, with_q):
    if with_q:
        klo_ref, khi_ref, vlo_ref, vhi_ref, rk_ref, rv_ref, q_ref, rq_ref, sg_ref = out_refs
    else:
        klo_ref, khi_ref, vlo_ref, vhi_ref, rk_ref, rv_ref = out_refs
    tm = x_ref.shape[1]
    sub = min(tm, SUB_ROWS)
    seg = _segment_ones()
    lane = lax.broadcasted_iota(I32, (sub, LANES), 1)
    low_half = lane < HEAD_DIM
    even = (lane & 1) == 0

    for r0 in range(0, tm, sub):
        rows = slice(r0, r0 + sub)
        x = x_ref[0, rows, :]
        h = x * lax.rsqrt(jnp.mean(x * x, axis=-1, keepdims=True) + EPS) * nw_ref[...]
        h = h * (1.0 + sc_ref[0]) + sh_ref[0]
        z = _dot(h.astype(BF16), wi_ref[...])

        def norm_rope(v, w128, rows=rows):
            v = v * lax.rsqrt(_head_mean_sq(v, seg) + EPS) * w128
            if rope:
                swapped = jnp.where(even, pltpu.roll(v, LANES - 1, 1), pltpu.roll(v, 1, 1))
                v = v * cos_ref[rows, :] + swapped * sin_ref[rows, :]
            return v

        k = norm_rope(z[:, OFF_AK:OFF_AK + KV_W], knw_ref[...])
        ksw = pltpu.roll(k, HEAD_DIM, 1)
        klo_ref[0, 0, :, rows] = jnp.transpose(jnp.where(low_half, k, 0.0)).astype(BF16)
        khi_ref[0, 0, :, rows] = jnp.transpose(jnp.where(low_half, 0.0, ksw)).astype(BF16)
        klo_ref[0, 1, :, rows] = jnp.transpose(jnp.where(low_half, ksw, 0.0)).astype(BF16)
        khi_ref[0, 1, :, rows] = jnp.transpose(jnp.where(low_half, 0.0, k)).astype(BF16)
        v = z[:, OFF_AV:OFF_AV + KV_W]
        vsw = pltpu.roll(v, HEAD_DIM, 1)
        vlo_ref[0, 0, rows, :] = jnp.where(low_half, v, 0.0).astype(BF16)
        vhi_ref[0, 0, rows, :] = jnp.where(low_half, 0.0, vsw).astype(BF16)
        vlo_ref[0, 1, rows, :] = jnp.where(low_half, vsw, 0.0).astype(BF16)
        vhi_ref[0, 1, rows, :] = jnp.where(low_half, 0.0, v).astype(BF16)
        rk_ref[0, rows, :] = (z[:, OFF_RK:OFF_RK + RET_W] * QK_SCALE).astype(BF16)
        rv_ref[0, rows, :] = z[:, OFF_RV:OFF_RV + RET_W].astype(BF16)
        if with_q:
            for j in range(ATTN_W // LANES):
                qj = norm_rope(z[:, OFF_AQ + j * LANES:OFF_AQ + (j + 1) * LANES], qnw_ref[...])
                q_ref[0, rows, j * LANES:(j + 1) * LANES] = (qj * (QK_SCALE * LOG2_E)).astype(BF16)
            rq_ref[0, rows, :] = z[:, OFF_RQ:OFF_RQ + RET_W].astype(BF16)
            sg_ref[0, rows, :] = _silu(z[:, OFF_RG:OFF_RG + RET_W]).astype(BF16)


def _projection(x, shift, scale, norm_w, wi_bf16, qnw, knw, cos, sin, *, rope, with_q, tm):
    b, l, d = x.shape
    tm = min(tm, l)
    ncols = IN_COLS if with_q else CTX_KV_COLS
    per_batch = shift.shape[0] > 1
    mod_idx = (lambda bi, i: (bi, 0, 0)) if per_batch else (lambda bi, i: (0, 0, 0))
    kv_shape = jax.ShapeDtypeStruct((b, ATTN_KV_HEADS, l, LANES), BF16)
    kv_spec = pl.BlockSpec((1, ATTN_KV_HEADS, tm, LANES), lambda bi, i: (bi, 0, i, 0))
    kt_shape = jax.ShapeDtypeStruct((b, ATTN_KV_HEADS, LANES, l), BF16)
    kt_spec = pl.BlockSpec((1, ATTN_KV_HEADS, LANES, tm), lambda bi, i: (bi, 0, 0, i))
    w_shape = jax.ShapeDtypeStruct((b, l, RET_W), BF16)
    w_spec = pl.BlockSpec((1, tm, RET_W), lambda bi, i: (bi, i, 0))
    out_shape = [kt_shape] * 2 + [kv_shape] * 2 + [w_shape] * 2
    out_specs = [kt_spec] * 2 + [kv_spec] * 2 + [w_spec] * 2
    if with_q:
        out_shape += [w_shape] * 3
        out_specs += [w_spec] * 3
    return pl.pallas_call(
        functools.partial(_proj_kernel, rope=rope, with_q=with_q),
        grid=(b, l // tm),
        in_specs=[pl.BlockSpec((1, tm, d), lambda bi, i: (bi, i, 0)),
                  pl.BlockSpec((1, 1, d), mod_idx),
                  pl.BlockSpec((1, 1, d), mod_idx),
                  pl.BlockSpec((1, d), lambda bi, i: (0, 0)),
                  pl.BlockSpec((d, ncols), lambda bi, i: (0, 0)),
                  pl.BlockSpec((1, LANES), lambda bi, i: (0, 0)),
                  pl.BlockSpec((1, LANES), lambda bi, i: (0, 0)),
                  pl.BlockSpec((tm, LANES), lambda bi, i: (i, 0)),
                  pl.BlockSpec((tm, LANES), lambda bi, i: (i, 0))],
        out_specs=out_specs,
        out_shape=out_shape,
        compiler_params=_cparams("arbitrary", "arbitrary"),
        name="proj_latent" if with_q else "proj_ctx",
    )(x, shift, scale, norm_w, wi_bf16, qnw, knw, cos, sin)


def _attn_kernel(shift_ref, q_ref, klo_ref, khi_ref, vlo_ref, vhi_ref, cklo_ref, ckhi_ref, cvlo_ref, cvhi_ref,
                 o_ref, kl_s, kh_s, va_s, *, l, lc, streaming):
    lk = l + lc

    @pl.when(pl.program_id(2) == 0)
    def _():
        kl_s[:, 0:l] = klo_ref[0, 0]
        kl_s[:, l:lk] = cklo_ref[0, 0]
        kh_s[:, 0:l] = khi_ref[0, 0]
        kh_s[:, l:lk] = ckhi_ref[0, 0]
        lane = lax.broadcasted_iota(I32, (lk, LANES), 1)
        ones_lo = jnp.where(lane < HEAD_DIM, 1.0, 0.0).astype(BF16)
        ones_hi = jnp.where(lane < HEAD_DIM, 0.0, 1.0).astype(BF16)
        for g, (v_ref, cv_ref, ones) in enumerate(((vlo_ref, cvlo_ref, ones_lo), (vhi_ref, cvhi_ref, ones_hi),
                                                   (vlo_ref, cvlo_ref, ones_lo), (vhi_ref, cvhi_ref, ones_hi))):
            v_col, one_col = (0, LANES) if g < 2 else (LANES, 0)
            va_s[g, 0:l, v_col:v_col + LANES] = v_ref[0, 0]
            va_s[g, l:lk, v_col:v_col + LANES] = cv_ref[0, 0]
            va_s[g, :, one_col:one_col + LANES] = ones

    q = q_ref[0]
    acc = []
    for g in range(GQA):
        qp = q[:, (g // 2) * LANES:(g // 2 + 1) * LANES]
        k_s = kl_s if g % 2 == 0 else kh_s
        if streaming:
            shift = shift_ref[0]
            o = None
            for c in range(0, lk, KEY_TILE):
                hi = min(c + KEY_TILE, lk)
                p = jnp.exp2(_dot(qp, k_s[:, c:hi]) - shift).astype(BF16)
                t = _dot(p, va_s[g, c:hi])
                o = t if o is None else o + t
        else:
            s = _dot(qp, k_s[...])
            p = jnp.exp2(s - jnp.max(s, axis=-1, keepdims=True)).astype(BF16)
            o = _dot(p, va_s[g])
        acc.append(o)
    out_a = acc[0] + acc[1]
    out_b = acc[2] + acc[3]
    o_ref[0, :, 0:LANES] = (out_a[:, 0:LANES] / out_a[:, LANES:2 * LANES]).astype(BF16)
    o_ref[0, :, LANES:2 * LANES] = (out_b[:, LANES:2 * LANES] / out_b[:, 0:LANES]).astype(BF16)


def _attention(shift, q, klo, khi, vlo, vhi, cklo, ckhi, cvlo, cvhi, *, tq, streaming):
    b, l, _ = q.shape
    lc = cvlo.shape[2]
    lk = l + lc
    tq = min(tq, l)
    gw = GQA * HEAD_DIM
    kt_spec = pl.BlockSpec((1, 1, LANES, l), lambda bi, h, i: (bi, h, 0, 0))
    kv_spec = pl.BlockSpec((1, 1, l, LANES), lambda bi, h, i: (bi, h, 0, 0))
    ckt_spec = pl.BlockSpec((1, 1, LANES, lc), lambda bi, h, i: (bi, h, 0, 0))
    ckv_spec = pl.BlockSpec((1, 1, lc, LANES), lambda bi, h, i: (bi, h, 0, 0))
    return pl.pallas_call(
        functools.partial(_attn_kernel, l=l, lc=lc, streaming=streaming),
        grid=(b, ATTN_KV_HEADS, l // tq),
        in_specs=([pl.BlockSpec(memory_space=pltpu.SMEM),
                   pl.BlockSpec((1, tq, gw), lambda bi, h, i: (bi, i, h))] + [kt_spec] * 2 + [kv_spec] * 2
                  + [ckt_spec] * 2 + [ckv_spec] * 2),
        out_specs=pl.BlockSpec((1, tq, gw), lambda bi, h, i: (bi, i, h)),
        out_shape=jax.ShapeDtypeStruct((b, l, ATTN_W), BF16),
        scratch_shapes=[pltpu.VMEM((LANES, lk), BF16), pltpu.VMEM((LANES, lk), BF16),
                        pltpu.VMEM((GQA, lk, 2 * LANES), BF16)],
        compiler_params=_cparams("arbitrary", "arbitrary", "arbitrary"),
        name="attn_stream" if streaming else "attn",
    )(shift, q, klo, khi, vlo, vhi, cklo, ckhi, cvlo, cvhi)


def _log_sigmoid(v):
    return jnp.minimum(v, 0.0) - jnp.log(1.0 + jnp.exp(-jnp.abs(v)))


def _ret_kernel(rq_ref, rk_ref, rv_ref, sg_ref, crk_ref, crv_ref, df_ref, db_ref, o_ref,
                m_s, xi_s, zeta_s, kv_s, st_s, *, l, lc):
    n = l // CHUNK
    nc = lc // CHUNK
    lgf = _log_sigmoid(df_ref[...])
    lgb = _log_sigmoid(db_ref[...])
    pos = lax.broadcasted_iota(I32, (CHUNK, LANES), 0).astype(F32)
    row = lax.broadcasted_iota(I32, (CHUNK, CHUNK), 0)
    col = lax.broadcasted_iota(I32, (CHUNK, CHUNK), 1)
    diff = (row - col).astype(F32)
    g_chunk = []
    for p in range(RET_PAIRS):
        cols = slice(p * LANES, (p + 1) * LANES)
        lf, lb = lgf[:, cols], lgb[:, cols]
        xi_s[p, :, 0:LANES] = jnp.exp((pos + 1.0) * lf)
        xi_s[p, :, LANES:] = jnp.exp((CHUNK - pos) * lb)
        zeta_s[p, :, 0:LANES] = jnp.exp((CHUNK - 1.0 - pos) * lf)
        zeta_s[p, :, LANES:] = jnp.exp(pos * lb)
        g_chunk.append((jnp.exp(CHUNK * lf), jnp.exp(CHUNK * lb)))
        for j in range(2):
            h = 2 * p + j
            hf = lgf[:, h * HEAD_DIM:h * HEAD_DIM + 1]
            hb = lgb[:, h * HEAD_DIM:h * HEAD_DIM + 1]
            m_s[p, :, j * CHUNK:(j + 1) * CHUNK] = jnp.where(
                diff > 0, jnp.exp(diff * hf), jnp.where(diff < 0, jnp.exp(-diff * hb), 2.0))

    lane = lax.broadcasted_iota(I32, (CHUNK, LANES), 1)
    low_half = lane < HEAD_DIM
    diag = (lax.broadcasted_iota(I32, (LANES, LANES), 0) // HEAD_DIM
            == lax.broadcasted_iota(I32, (LANES, LANES), 1) // HEAD_DIM)
    seg = jnp.where(diag, 1.0, 0.0).astype(BF16)
    seg2 = jnp.concatenate([seg, seg], axis=0)
    diag2 = jnp.concatenate([diag, diag], axis=0)

    def split_heads(a):
        zero = jnp.zeros_like(a)
        return jnp.concatenate([jnp.where(low_half, a, zero), jnp.where(low_half, zero, a)], axis=0)

    def contrib(k_ref, v_ref, r0, p):
        cols = slice(p * LANES, (p + 1) * LANES)
        kp = k_ref[0, pl.ds(r0, CHUNK), cols].astype(F32)
        kz = jnp.concatenate([kp, kp], axis=1) * zeta_s[p]
        kv = _dot(jnp.transpose(kz).astype(BF16), v_ref[0, pl.ds(r0, CHUNK), cols])
        return jnp.where(diag2, kv, 0.0)

    for c in range(nc):
        for p in range(RET_PAIRS):
            kv_s[c, p] = contrib(crk_ref, crv_ref, c * CHUNK, p)

    def contrib_body(c, carry):
        r0 = pl.multiple_of(c * CHUNK, CHUNK)
        for p in range(RET_PAIRS):
            kv_s[nc + c, p] = contrib(rk_ref, rv_ref, r0, p)
        return carry

    lax.fori_loop(0, n, contrib_body, 0, unroll=4)

    for p in range(RET_PAIRS):
        gf, gb = g_chunk[p]
        sf = jnp.zeros((LANES, LANES), F32)
        sb = jnp.zeros((LANES, LANES), F32)
        for c in range(nc):
            sf = gf * sf + kv_s[c, p, 0:LANES]
            sb = gb * sb + kv_s[nc - 1 - c, p, LANES:]

        def fwd_scan(c, s, p=p, gf=gf):
            st_s[c, p, 0:LANES] = s.astype(BF16)
            return gf * s + kv_s[nc + c, p, 0:LANES]

        def bwd_scan(j, s, p=p, gb=gb):
            c = n - 1 - j
            st_s[c, p, LANES:] = s.astype(BF16)
            return gb * s + kv_s[nc + c, p, LANES:]

        lax.fori_loop(0, n, fwd_scan, sf)
        lax.fori_loop(0, n, bwd_scan, sb)

    def out_body(c, carry):
        r0 = pl.multiple_of(c * CHUNK, CHUNK)
        for p in range(RET_PAIRS):
            cols = slice(p * LANES, (p + 1) * LANES)
            qp = rq_ref[0, pl.ds(r0, CHUNK), cols]
            kp = rk_ref[0, pl.ds(r0, CHUNK), cols]
            vp = rv_ref[0, pl.ds(r0, CHUNK), cols]
            s2 = _dot_nt(qp, split_heads(kp))
            a2 = (s2 * m_s[p]).astype(BF16)
            y = _dot(a2, split_heads(vp))
            qf = qp.astype(F32)
            qx = (jnp.concatenate([qf, qf], axis=1) * xi_s[p]).astype(BF16)
            y += _dot(qx, st_s[c, p])
            hi, lo = _split(y * y)
            ms = _dot(jnp.concatenate([hi, lo], axis=1), seg2) * (1.0 / HEAD_DIM)
            out = y * lax.rsqrt(ms + EPS) * sg_ref[0, pl.ds(r0, CHUNK), cols].astype(F32)
            o_ref[0, pl.ds(r0, CHUNK), cols] = out.astype(BF16)
        return carry

    lax.fori_loop(0, n, out_body, 0, unroll=4)


def _retention(rq, rk, rv, sg, crk, crv, dec_f, dec_b):
    b, l, _ = rq.shape
    lc = crk.shape[1]
    n = l // CHUNK
    nc = lc // CHUNK
    spec = pl.BlockSpec((1, l, RET_W), lambda bi: (bi, 0, 0))
    cspec = pl.BlockSpec((1, lc, RET_W), lambda bi: (bi, 0, 0))
    dspec = pl.BlockSpec((1, RET_W), lambda bi: (0, 0))
    return pl.pallas_call(
        functools.partial(_ret_kernel, l=l, lc=lc),
        grid=(b,),
        in_specs=[spec, spec, spec, spec, cspec, cspec, dspec, dspec],
        out_specs=spec,
        out_shape=jax.ShapeDtypeStruct((b, l, RET_W), BF16),
        scratch_shapes=[pltpu.VMEM((RET_PAIRS, CHUNK, 2 * CHUNK), F32),
                        pltpu.VMEM((RET_PAIRS, CHUNK, 2 * LANES), F32),
                        pltpu.VMEM((RET_PAIRS, CHUNK, 2 * LANES), F32),
                        pltpu.VMEM((nc + n, RET_PAIRS, 2 * LANES, LANES), F32),
                        pltpu.VMEM((n, RET_PAIRS, 2 * LANES, LANES), BF16)],
        compiler_params=_cparams("arbitrary"),
        name="ret",
    )(rq, rk, rv, sg, crk, crv, dec_f, dec_b)


def _out_kernel(attn_ref, ret_ref, x_ref, wa_ref, wr_ref, g1_ref, sh_ref, sc_ref, nw_ref, rhi_ref, rlo_ref,
                x1_ref, hp_ref, lg_ref):
    y = _dot(attn_ref[0], wa_ref[...]) + _dot(ret_ref[0], wr_ref[...])
    x1 = x_ref[0] + g1_ref[0] * y
    x1_ref[0] = x1
    h = x1 * lax.rsqrt(jnp.mean(x1 * x1, axis=-1, keepdims=True) + EPS) * nw_ref[...]
    h = h * (1.0 + sc_ref[0]) + sh_ref[0]
    half = h.shape[1] // 2
    hp_ref[...] = _pack_halves(h[:, :half], h[:, half:])
    h_hi, h_lo = _split(h)
    lg_ref[...] = _dot_nt(rhi_ref[...], h_hi) + _dot_nt(rhi_ref[...], h_lo) + _dot_nt(rlo_ref[...], h_hi)


def _out_projection(attn, ret, x, w_out, g1, sh2, sc2, norm_w, r_hi, r_lo, *, tm):
    assert ATTN_W % RET_W == 0
    wa = wr = w_out
    b, l, d = x.shape
    tm = min(tm, l)
    nt = l // tm
    t = b * l
    mspec = pl.BlockSpec((1, 1, d), lambda bi, i: (bi, 0, 0))
    return pl.pallas_call(
        _out_kernel,
        grid=(b, nt),
        in_specs=[pl.BlockSpec((1, tm, ATTN_W), lambda bi, i: (bi, i, 0)),
                  pl.BlockSpec((1, tm, RET_W), lambda bi, i: (bi, i, 0)),
                  pl.BlockSpec((1, tm, d), lambda bi, i: (bi, i, 0)),
                  pl.BlockSpec((ATTN_W, d), lambda bi, i: (0, 0)),
                  pl.BlockSpec((RET_W, d), lambda bi, i: (ATTN_W // RET_W, 0)),
                  mspec, mspec, mspec,
                  pl.BlockSpec((1, d), lambda bi, i: (0, 0)),
                  pl.BlockSpec((N_EXPERTS, d), lambda bi, i: (0, 0)),
                  pl.BlockSpec((N_EXPERTS, d), lambda bi, i: (0, 0))],
        out_specs=[pl.BlockSpec((1, tm, d), lambda bi, i: (bi, i, 0)),
                   pl.BlockSpec((tm, d // 2), lambda bi, i: (bi * nt + i, 0)),
                   pl.BlockSpec((N_EXPERTS, tm), lambda bi, i: (0, bi * nt + i))],
        out_shape=[jax.ShapeDtypeStruct((b, l, d), F32),
                   jax.ShapeDtypeStruct((t, d // 2), PACKED),
                   jax.ShapeDtypeStruct((N_EXPERTS, t), F32)],
        compiler_params=_cparams("arbitrary", "arbitrary"),
        name="out_proj",
    )(attn, ret, x, wa, wr, g1, sh2, sc2, norm_w, r_hi, r_lo)


def _route_kernel(lg_ref, bias_ref, idx_ref, w_ref, rank_ref, cnt_col_ref, cnt_row_ref, tri_s, col_s, row_s):
    tb = lg_ref.shape[1]
    step = pl.program_id(0)

    @pl.when(step == 0)
    def _():
        r = lax.broadcasted_iota(I32, (tb, tb), 0)
        c = lax.broadcasted_iota(I32, (tb, tb), 1)
        tri_s[...] = jnp.where(r <= c, 1.0, 0.0).astype(BF16)
        col_s[...] = jnp.zeros_like(col_s)
        row_s[...] = jnp.zeros_like(row_s)

    scores = _sigmoid(lg_ref[...])
    biased = scores + bias_ref[...]
    neg = -jnp.inf
    sub = lax.broadcasted_iota(I32, (GROUP_SIZE, tb), 0).astype(F32)

    gscore = []
    for g in range(N_GROUPS):
        blk = biased[g * GROUP_SIZE:(g + 1) * GROUP_SIZE]
        m1 = jnp.max(blk, axis=0, keepdims=True)
        first = jnp.min(jnp.where(blk == m1, sub, float(GROUP_SIZE)), axis=0, keepdims=True)
        m2 = jnp.max(jnp.where(sub == first, neg, blk), axis=0, keepdims=True)
        gscore.append(m1 + m2)
    gs = jnp.concatenate(gscore, axis=0)
    gsub = lax.broadcasted_iota(I32, (N_GROUPS, tb), 0).astype(F32)
    keep = jnp.zeros((N_GROUPS, tb), F32)
    for _ in range(TOPK_GROUPS):
        m = jnp.max(gs, axis=0, keepdims=True)
        first = jnp.min(jnp.where(gs == m, gsub, float(N_GROUPS)), axis=0, keepdims=True)
        sel = gsub == first
        keep = jnp.where(sel, 1.0, keep)
        gs = jnp.where(sel, neg, gs)
    masked = jnp.concatenate(
        [jnp.where(keep[g:g + 1] > 0.0, biased[g * GROUP_SIZE:(g + 1) * GROUP_SIZE], neg)
         for g in range(N_GROUPS)], axis=0)

    esub = lax.broadcasted_iota(I32, (N_EXPERTS, tb), 0).astype(F32)
    sels, idxs, ws = [], [], []
    chosen = jnp.zeros((N_EXPERTS, tb), F32)
    for _ in range(TOP_K):
        m = jnp.max(masked, axis=0, keepdims=True)
        first = jnp.min(jnp.where(masked == m, esub, float(N_EXPERTS)), axis=0, keepdims=True)
        sel = esub == first
        sels.append(sel)
        idxs.append(first)
        ws.append(jnp.sum(jnp.where(sel, scores, 0.0), axis=0, keepdims=True))
        chosen = jnp.where(sel, 1.0, chosen)
        masked = jnp.where(sel, neg, masked)
    wsum = ws[0]
    for k in range(1, TOP_K):
        wsum = wsum + ws[k]
    idx_ref[...] = jnp.concatenate(idxs, axis=0).astype(I32)
    w_ref[...] = jnp.concatenate([wk / wsum * ROUTED_SCALE for wk in ws], axis=0)

    chosen_b = chosen.astype(BF16)
    incl = _dot(chosen_b, tri_s[...])
    before = incl - chosen + col_s[...]
    rank_ref[...] = jnp.concatenate(
        [jnp.sum(jnp.where(sel, before, 0.0), axis=0, keepdims=True) for sel in sels], axis=0).astype(I32)
    col_s[...] = col_s[...] + incl[:, tb - 1:tb]
    row_s[...] = row_s[...] + _dot_nt(jnp.ones((8, tb), BF16), chosen_b)
    cnt_col_ref[...] = col_s[...].astype(I32)
    cnt_row_ref[...] = row_s[...].astype(I32)


def _route(logits_t, bias_col, *, tb):
    e, t = logits_t.shape
    tb = min(tb, t)
    kspec = pl.BlockSpec((TOP_K, tb), lambda i: (0, i))
    return pl.pallas_call(
        _route_kernel,
        grid=(t // tb,),
        in_specs=[pl.BlockSpec((e, tb), lambda i: (0, i)),
                  pl.BlockSpec((e, 1), lambda i: (0, 0))],
        out_specs=[kspec, kspec, kspec,
                   pl.BlockSpec((e, 1), lambda i: (0, 0)),
                   pl.BlockSpec((8, e), lambda i: (0, 0))],
        out_shape=[jax.ShapeDtypeStruct((TOP_K, t), I32),
                   jax.ShapeDtypeStruct((TOP_K, t), F32),
                   jax.ShapeDtypeStruct((TOP_K, t), I32),
                   jax.ShapeDtypeStruct((e, 1), I32),
                   jax.ShapeDtypeStruct((8, e), I32)],
        scratch_shapes=[pltpu.VMEM((tb, tb), BF16), pltpu.VMEM((e, 1), F32), pltpu.VMEM((8, e), F32)],
        compiler_params=_cparams("arbitrary"),
        name="route",
    )(logits_t, bias_col)


def _pad_block(cnt):
    return (cnt + (MOE_BLOCK - 1)) // MOE_BLOCK * MOE_BLOCK


def _max_items(n_blocks):
    return n_blocks // ITEM_BLOCKS + N_EXPERTS


def _dest_kernel(idx_ref, rank_ref, cnt_col_ref, cnt_row_ref, dest_ref, meta_ref, items_ref, start_s):
    tb = idx_ref.shape[1]
    nip = items_ref.shape[1]

    @pl.when(pl.program_id(0) == 0)
    def _():
        pad_col = _pad_block(cnt_col_ref[...])
        pad_row = _pad_block(cnt_row_ref[0:1, :])
        er = lax.broadcasted_iota(I32, (N_EXPERTS, N_EXPERTS), 0)
        ec = lax.broadcasted_iota(I32, (N_EXPERTS, N_EXPERTS), 1)
        start_col = jnp.sum(jnp.where(ec < er, pad_row, 0), axis=1, keepdims=True)
        start_row = jnp.sum(jnp.where(er < ec, pad_col, 0), axis=0, keepdims=True)
        start_s[...] = start_col

        used = jnp.sum(pad_row, axis=1, keepdims=True) // MOE_BLOCK
        meta_ref[...] = jnp.concatenate(
            [cnt_row_ref[0:1, :], start_row, pad_row, jnp.broadcast_to(used, (1, N_EXPERTS)),
             jnp.zeros((4, N_EXPERTS), I32)], axis=0)

        nb_col = pad_col // MOE_BLOCK
        it_col = (nb_col + (ITEM_BLOCKS - 1)) // ITEM_BLOCKS
        it_row = (pad_row // MOE_BLOCK + (ITEM_BLOCKS - 1)) // ITEM_BLOCKS
        it_start = jnp.sum(jnp.where(ec < er, it_row, 0), axis=1, keepdims=True)
        n_items = jnp.sum(it_row, axis=1, keepdims=True)
        lane = lax.broadcasted_iota(I32, (1, nip), 1)
        owner = jnp.sum(jnp.where(it_start + it_col <= lane, 1, 0), axis=0, keepdims=True)
        owner = jnp.minimum(owner, N_EXPERTS - 1)
        onehot = lax.broadcasted_iota(I32, (N_EXPERTS, nip), 0) == owner

        def pick(col):
            return jnp.sum(jnp.where(onehot, col, 0), axis=0, keepdims=True)

        j = lane - pick(it_start)
        block0 = pick(start_col) // MOE_BLOCK + ITEM_BLOCKS * j
        nvalid = jnp.clip(pick(nb_col) - ITEM_BLOCKS * j, 0, ITEM_BLOCKS)
        items_ref[...] = jnp.concatenate(
            [owner, block0, jnp.where(lane < n_items, nvalid, 0), jnp.broadcast_to(n_items, (1, nip)),
             jnp.zeros((4, nip), I32)], axis=0)

    start_col = start_s[...]
    esub = lax.broadcasted_iota(I32, (N_EXPERTS, tb), 0)
    rows = []
    for k in range(TOP_K):
        onehot = esub == idx_ref[k:k + 1, :]
        rows.append(jnp.sum(jnp.where(onehot, start_col, 0), axis=0, keepdims=True) + rank_ref[k:k + 1, :])
    dest_ref[0] = jnp.concatenate(rows, axis=0)


def _destinations(idx_t, rank_t, cnt_col, cnt_row, *, tb, n_blocks):
    _, t = idx_t.shape
    tb = min(tb, t)
    nip = (_max_items(n_blocks) + LANES - 1) // LANES * LANES
    kspec = pl.BlockSpec((TOP_K, tb), lambda i: (0, i))
    return pl.pallas_call(
        _dest_kernel,
        grid=(t // tb,),
        in_specs=[kspec, kspec,
                  pl.BlockSpec((N_EXPERTS, 1), lambda i: (0, 0)),
                  pl.BlockSpec((8, N_EXPERTS), lambda i: (0, 0))],
        out_specs=[pl.BlockSpec((1, TOP_K, tb), lambda i: (i, 0, 0)),
                   pl.BlockSpec((8, N_EXPERTS), lambda i: (0, 0)),
                   pl.BlockSpec((8, nip), lambda i: (0, 0))],
        out_shape=[jax.ShapeDtypeStruct((t // tb, TOP_K, tb), I32),
                   jax.ShapeDtypeStruct((8, N_EXPERTS), I32),
                   jax.ShapeDtypeStruct((8, nip), I32)],
        scratch_shapes=[pltpu.VMEM((N_EXPERTS, 1), I32)],
        compiler_params=_cparams("arbitrary"),
        name="dest",
    )(idx_t, rank_t, cnt_col, cnt_row)


_PAD_BITS = (64, 32, 16, 8)


def _sc_scatter_rows(rows, dest_win, n_out_rows):
    n_win, n_slots, win = dest_win.shape
    width = rows.shape[1]
    info = plsc.get_sparse_core_info()
    n_workers = info.num_cores * info.num_subcores
    per_worker = n_win // n_workers
    assert per_worker * n_workers == n_win and win <= LANES
    mesh = plsc.VectorSubcoreMesh(core_axis_name="c", subcore_axis_name="s")

    def body(rows_hbm, dest_hbm, out_hbm, idx_v, rows_v, sem):
        wid = lax.axis_index("s") * info.num_cores + lax.axis_index("c")

        @pl.loop(0, per_worker)
        def _(j):
            w = wid * per_worker + j
            pltpu.sync_copy(dest_hbm.at[w], idx_v)
            pltpu.sync_copy(rows_hbm.at[pl.ds(w * win, win)], rows_v)
            copies = [pltpu.async_copy(rows_v, out_hbm.at[idx_v.at[k]], sem) for k in range(n_slots)]
            for cp in copies:
                cp.wait()

    return pl.kernel(
        body,
        out_type=jax.ShapeDtypeStruct((n_out_rows, width), rows.dtype),
        mesh=mesh,
        scratch_types=[pltpu.VMEM((n_slots, win), I32), pltpu.VMEM((win, width), rows.dtype),
                       pltpu.SemaphoreType.DMA],
        name="sc_scatter",
    )(rows, dest_win)


def _pad_fill_kernel(meta_ref, xs_in, xs_hbm, zero_s, sem_z, *, e_per_step):
    del xs_in
    step = pl.program_id(0)
    zero_s[...] = jnp.zeros_like(zero_s)

    def tail_copy(c):
        row0 = pl.multiple_of((meta_ref[3, 0] + c) * MOE_BLOCK, MOE_BLOCK)
        return pltpu.make_async_copy(zero_s, xs_hbm.at[pl.ds(row0, MOE_BLOCK)], sem_z)

    @pl.when(step == 0)
    def _():
        for c in range(ITEM_BLOCKS - 1):
            tail_copy(c).start()
        for c in range(ITEM_BLOCKS - 1):
            tail_copy(c).wait()

    def pad_copies(e):
        cnt = meta_ref[0, e]
        off = meta_ref[1, e] + cnt
        rem = meta_ref[2, e] - cnt
        head = rem & (SUBLANES - 1)
        out = []
        for i in range(SUBLANES - 1):
            out.append((i < head,
                        pltpu.make_async_copy(zero_s.at[pl.ds(0, 1)], xs_hbm.at[pl.ds(off + i, 1)], sem_z)))
        off = off + head
        for bit in _PAD_BITS:
            out.append(((rem & bit) != 0,
                        pltpu.make_async_copy(zero_s.at[pl.ds(0, bit)],
                                              xs_hbm.at[pl.ds(pl.multiple_of(off, SUBLANES), bit)], sem_z)))
            off = off + (rem & bit)
        return out

    for j in range(e_per_step):
        for cond, c in pad_copies(step * e_per_step + j):
            pl.when(cond)(c.start)
    for j in range(e_per_step):
        for cond, c in pad_copies(step * e_per_step + j):
            pl.when(cond)(c.wait)


PAD_FILL_EXPERTS_PER_STEP = 8


def _pad_fill(meta, xs):
    half = xs.shape[1]
    return pl.pallas_call(
        functools.partial(_pad_fill_kernel, e_per_step=PAD_FILL_EXPERTS_PER_STEP),
        grid=(N_EXPERTS // PAD_FILL_EXPERTS_PER_STEP,),
        in_specs=[pl.BlockSpec(memory_space=pltpu.SMEM),
                  pl.BlockSpec(memory_space=pl.ANY)],
        out_specs=pl.BlockSpec(memory_space=pl.ANY),
        out_shape=jax.ShapeDtypeStruct(xs.shape, xs.dtype),
        input_output_aliases={1: 0},
        scratch_shapes=[pltpu.VMEM((MOE_BLOCK, half), PACKED), pltpu.SemaphoreType.DMA],
        compiler_params=_cparams("arbitrary"),
        name="pad_fill",
    )(meta, xs)


def _experts_kernel(items_ref, xs_hbm, wg_hbm, wu_hbm, wd_hbm, ys_hbm,
                    xbuf, ybuf, wg_f, wu_f, wd_f, wg_s, wu_s, wd_s, sem_x, sem_y, sem_w):
    n_items = items_ref[3, 0]
    rows = ITEM_BLOCKS * MOE_BLOCK

    def w_copies(e, s):
        return [pltpu.make_async_copy(src.at[e], dst.at[s], sem_w.at[s])
                for src, dst in ((wg_hbm, wg_f), (wu_hbm, wu_f), (wd_hbm, wd_f))]

    def x_copy(item, s):
        row0 = pl.multiple_of(items_ref[1, item] * MOE_BLOCK, MOE_BLOCK)
        return pltpu.make_async_copy(xs_hbm.at[pl.ds(row0, rows)], xbuf.at[s], sem_x.at[s])

    def y_copies(item, s, fn):
        for k in range(ITEM_BLOCKS):
            row0 = pl.multiple_of((items_ref[1, item] + k) * MOE_BLOCK, MOE_BLOCK)
            cp = pltpu.make_async_copy(ybuf.at[s, pl.ds(k * MOE_BLOCK, MOE_BLOCK)],
                                       ys_hbm.at[pl.ds(row0, MOE_BLOCK)], sem_y.at[s])
            pl.when(k < items_ref[2, item])(functools.partial(fn, cp))

    def expert_of(item):
        return items_ref[0, jnp.minimum(item, n_items - 1)]

    def changes_at(item):
        return ((item < n_items) & (expert_of(item) != expert_of(item - 1))).astype(I32)

    @pl.when(n_items > 0)
    def _():
        x_copy(0, 0).start()
        for cp in w_copies(expert_of(0), 0):
            cp.start()

        @pl.when(changes_at(1) == 1)
        def _():
            for cp in w_copies(expert_of(1), 1):
                cp.start()

    def item_body(i, ordinal):
        slot = i % 2
        prev = jnp.maximum(i - 1, 0)
        e = expert_of(i)
        new_expert = (i == 0) | (e != expert_of(prev))
        c1 = changes_at(i + 1)
        c2 = changes_at(i + 2)

        @pl.when(i + 1 < n_items)
        def _():
            x_copy(i + 1, 1 - slot).start()

        @pl.when(new_expert)
        def _():
            wslot = ordinal % WEIGHT_SLOTS
            for cp in w_copies(e, wslot):
                cp.wait()
            wg_s[...] = wg_f[wslot].astype(BF16)
            wu_s[...] = wu_f[wslot].astype(BF16)
            wd_s[...] = wd_f[wslot].astype(BF16)

        @pl.when(c2 == 1)
        def _():
            for cp in w_copies(expert_of(i + 2), (ordinal + c1 + 1) % WEIGHT_SLOTS):
                cp.start()

        x_copy(i, slot).wait()
        xa, xb = _unpack_halves(xbuf[slot])
        xa = xa.astype(BF16)
        xb = xb.astype(BF16)
        half = xa.shape[1]
        g = _dot(xa, wg_s[0:half]) + _dot(xb, wg_s[half:])
        u = _dot(xa, wu_s[0:half]) + _dot(xb, wu_s[half:])
        y = _dot((_silu(g) * u).astype(BF16), wd_s[...])
        ybuf[slot] = _pack_halves(y[:, :half], y[:, half:])
        y_copies(i, slot, lambda cp: cp.start(priority=1))

        @pl.when(i > 0)
        def _():
            y_copies(prev, 1 - slot, lambda cp: cp.wait())

        return ordinal + c1

    lax.fori_loop(0, n_items, item_body, jnp.int32(0))

    @pl.when(n_items > 0)
    def _():
        last = n_items - 1
        y_copies(last, last % 2, lambda cp: cp.wait())


def _experts(items, xs, w_gate, w_up, w_down, *, n_blocks):
    half = xs.shape[1]
    e, d, f = w_gate.shape
    rows = ITEM_BLOCKS * MOE_BLOCK
    any_spec = pl.BlockSpec(memory_space=pl.ANY)
    return pl.pallas_call(
        _experts_kernel,
        grid_spec=pltpu.PrefetchScalarGridSpec(
            num_scalar_prefetch=1,
            grid=(1,),
            in_specs=[any_spec, any_spec, any_spec, any_spec],
            out_specs=any_spec,
            scratch_shapes=[pltpu.VMEM((2, rows, half), PACKED), pltpu.VMEM((2, rows, half), PACKED),
                            pltpu.VMEM((WEIGHT_SLOTS, d, f), F32), pltpu.VMEM((WEIGHT_SLOTS, d, f), F32),
                            pltpu.VMEM((WEIGHT_SLOTS, f, d), F32),
                            pltpu.VMEM((d, f), BF16), pltpu.VMEM((d, f), BF16), pltpu.VMEM((f, d), BF16),
                            pltpu.SemaphoreType.DMA((2,)), pltpu.SemaphoreType.DMA((2,)),
                            pltpu.SemaphoreType.DMA((WEIGHT_SLOTS,))]),
        out_shape=jax.ShapeDtypeStruct((n_blocks * MOE_BLOCK, half), PACKED),
        compiler_params=_cparams("arbitrary"),
        name="experts",
    )(items, xs, w_gate, w_up, w_down)


def _sc_gather_rows(table, idx):
    n_idx = idx.shape[0]
    width = table.shape[1]
    info = plsc.get_sparse_core_info()
    n_workers = info.num_cores * info.num_subcores
    per_worker = n_idx // n_workers
    assert per_worker * n_workers == n_idx and per_worker % (SC_GATHER_BUFS * SC_WINDOW) == 0
    mesh = plsc.VectorSubcoreMesh(core_axis_name="c", subcore_axis_name="s")

    def body(table_hbm, idx_hbm, out_hbm, idx_v, rows_v, sem_g, sem_o):
        wid = lax.axis_index("s") * info.num_cores + lax.axis_index("c")
        base = wid * per_worker

        @pl.loop(0, per_worker // (SC_GATHER_BUFS * SC_WINDOW))
        def _(it):
            offs = [base + (it * SC_GATHER_BUFS + b) * SC_WINDOW for b in range(SC_GATHER_BUFS)]
            gathers = []
            for b, off in enumerate(offs):
                pltpu.sync_copy(idx_hbm.at[pl.ds(off, SC_WINDOW)], idx_v.at[b])
                gathers.append(pltpu.async_copy(table_hbm.at[idx_v.at[b]], rows_v.at[b], sem_g.at[b]))
            writes = []
            for b, off in enumerate(offs):
                gathers[b].wait()
                writes.append(pltpu.async_copy(rows_v.at[b], out_hbm.at[pl.ds(off, SC_WINDOW)], sem_o.at[b]))
            for cp in writes:
                cp.wait()

    return pl.kernel(
        body,
        out_type=jax.ShapeDtypeStruct((n_idx, width), table.dtype),
        mesh=mesh,
        scratch_types=[pltpu.VMEM((SC_GATHER_BUFS, SC_WINDOW), I32),
                       pltpu.VMEM((SC_GATHER_BUFS, SC_WINDOW, width), table.dtype),
                       pltpu.SemaphoreType.DMA((SC_GATHER_BUFS,)), pltpu.SemaphoreType.DMA((SC_GATHER_BUFS,))],
        name="sc_gather",
    )(table, idx)


def _combine_kernel(hp_ref, x1_ref, g2_ref, w_ref, sgw_ref, suw_ref, sdw_ref, yg_ref, *rest):
    o_ref = rest[-1]
    xa, xb = _unpack_halves(hp_ref[...])
    xa = xa.astype(BF16)
    xb = xb.astype(BF16)
    half = xa.shape[1]
    tb = xa.shape[0]
    g = _dot(xa, sgw_ref[0:half]) + _dot(xb, sgw_ref[half:])
    u = _dot(xa, suw_ref[0:half]) + _dot(xb, suw_ref[half:])
    shared = _dot((_silu(g) * u).astype(BF16), sdw_ref[...])

    w = jnp.transpose(w_ref[...])
    acc_a = jnp.zeros((tb, half), F32)
    acc_b = jnp.zeros((tb, half), F32)
    for k in range(TOP_K):
        ya, yb = _unpack_halves(yg_ref[0, k])
        acc_a += ya * w[:, k:k + 1]
        acc_b += yb * w[:, k:k + 1]
    g2 = g2_ref[0]
    o_ref[:, 0:half] = x1_ref[:, 0:half] + g2[:, 0:half] * (acc_a + shared[:, 0:half])
    o_ref[:, half:] = x1_ref[:, half:] + g2[:, half:] * (acc_b + shared[:, half:])


def _combine(hp, x1, g2, w_tok, sgw, suw, sdw, yg, out_prev, *, tb, seq_len, first_step):
    t, half = hp.shape
    d = 2 * half
    per_seq = seq_len // tb
    f = sgw.shape[1]
    s0 = first_step
    in_specs = [pl.BlockSpec((tb, half), lambda i: (s0 + i, 0)),
                pl.BlockSpec((tb, d), lambda i: (s0 + i, 0)),
                pl.BlockSpec((1, 1, d), lambda i: ((s0 + i) // per_seq, 0, 0)),
                pl.BlockSpec((TOP_K, tb), lambda i: (0, s0 + i)),
                pl.BlockSpec((d, f), lambda i: (0, 0)),
                pl.BlockSpec((d, f), lambda i: (0, 0)),
                pl.BlockSpec((f, d), lambda i: (0, 0)),
                pl.BlockSpec((1, TOP_K, tb, half), lambda i: (i, 0, 0, 0))]
    args = [hp, x1, g2, w_tok, sgw, suw, sdw, yg]
    aliases = {}
    if out_prev is not None:
        in_specs.append(pl.BlockSpec(memory_space=pl.ANY))
        args.append(out_prev)
        aliases = {len(args) - 1: 0}
    return pl.pallas_call(
        _combine_kernel,
        grid=(yg.shape[0],),
        in_specs=in_specs,
        out_specs=pl.BlockSpec((tb, d), lambda i: (s0 + i, 0)),
        out_shape=jax.ShapeDtypeStruct((t, d), F32),
        input_output_aliases=aliases,
        compiler_params=_cparams("arbitrary"),
        name="combine",
    )(*args)


def _rope_tables(l):
    rows = l // GRID_W
    r = jnp.repeat(jnp.arange(rows), GRID_W).astype(F32)
    col = jnp.tile(jnp.arange(GRID_W), rows).astype(F32)
    n_f = HEAD_DIM // 4
    freqs = ROPE_THETA ** (-jnp.arange(n_f, dtype=F32) / n_f)
    ang = jnp.concatenate([r[:, None] * freqs, col[:, None] * freqs], axis=-1)
    ang = jnp.tile(jnp.repeat(ang, 2, axis=1), (1, LANES // HEAD_DIM))
    sign = jnp.where(jnp.arange(LANES) % 2 == 0, -1.0, 1.0).astype(F32)
    return jnp.cos(ang), jnp.sin(ang) * sign


def kernel(x, c, ctx, c_ctx, w_mod, b_mod, norm1_w, norm2_w, w_in, q_norm_w, k_norm_w, ret_decay_fwd,
           ret_decay_bwd, w_out, router_w, router_bias, exp_w_gate, exp_w_up, exp_w_down, sh_w_gate,
           sh_w_up, sh_w_down):
    b, l, d = x.shape
    lc = ctx.shape[1]
    t = b * l
    assert w_mod.shape[0] == 1, "single layer"
    assert l % CHUNK == 0 and lc % CHUNK == 0 and l % GRID_W == 0

    rows = (b + 1 + 7) // 8 * 8
    cc = jnp.zeros((rows, d), F32).at[:b].set(c).at[b].set(c_ctx)
    mod = _modulation(cc, w_mod[0], b_mod[0])
    sh1, sc1, g1, sh2, sc2, g2 = [mod[:b, i * d:(i + 1) * d].reshape(b, 1, d) for i in range(6)]
    shc = mod[b, 0:d].reshape(1, 1, d)
    scc = mod[b, d:2 * d].reshape(1, 1, d)

    wi = w_in[0].astype(BF16)
    qnw = jnp.tile(q_norm_w[0], LANES // HEAD_DIM).reshape(1, LANES)
    knw = jnp.tile(k_norm_w[0], LANES // HEAD_DIM).reshape(1, LANES)
    cos, sin = _rope_tables(l)
    n1 = norm1_w[0].reshape(1, d)

    cklo, ckhi, cvlo, cvhi, crk, crv = _projection(
        ctx, shc, scc, n1, wi, qnw, knw, cos[:lc], sin[:lc], rope=False, with_q=False, tm=TILE_PROJ)
    klo, khi, vlo, vhi, rk, rv, q, rq, sg = _projection(
        x, sh1, sc1, n1, wi, qnw, knw, cos, sin, rope=True, with_q=True, tm=TILE_PROJ)

    bound = (HEAD_DIM * QK_SCALE * LOG2_E * BOUND_MARGIN
             * jnp.max(jnp.abs(q_norm_w[0])) * jnp.max(jnp.abs(k_norm_w[0]))).astype(F32)
    attn_args = (bound.reshape(1), q, klo, khi, vlo, vhi, cklo, ckhi, cvlo, cvhi)
    attn = lax.cond(bound <= MAX_STREAM_SHIFT,
                    functools.partial(_attention, tq=TILE_ATTN_Q, streaming=True),
                    functools.partial(_attention, tq=TILE_ATTN_Q, streaming=False), *attn_args)
    dec_f = jnp.repeat(ret_decay_fwd[0].astype(F32), HEAD_DIM).reshape(1, RET_W)
    dec_b = jnp.repeat(ret_decay_bwd[0].astype(F32), HEAD_DIM).reshape(1, RET_W)
    ret = _retention(rq, rk, rv, sg, crk, crv, dec_f, dec_b)

    wo = w_out[0].astype(BF16)
    r_hi, r_lo = _split(router_w[0].T)
    x1, hp, logits_t = _out_projection(attn, ret, x, wo, g1, sh2, sc2,
                                       norm2_w[0].reshape(1, d), r_hi, r_lo, tm=TILE_OUT)

    idx_t, w_t, rank_t, cnt_col, cnt_row = _route(logits_t, router_bias[0].reshape(N_EXPERTS, 1), tb=TILE_TOKENS)
    n_blocks = -(-(t * TOP_K) // MOE_BLOCK) + N_EXPERTS
    tb = TILE_TOKENS
    dest, meta, items = _destinations(idx_t, rank_t, cnt_col, cnt_row, tb=tb, n_blocks=n_blocks)
    steps, _, tbe = dest.shape
    dest_win = dest.reshape(steps, TOP_K, tbe // SC_WINDOW, SC_WINDOW).transpose(0, 2, 1, 3)
    dest_win = dest_win.reshape(t // SC_WINDOW, TOP_K, SC_WINDOW)
    xs = _sc_scatter_rows(hp, dest_win, (n_blocks + ITEM_BLOCKS - 1) * MOE_BLOCK)
    xs = _pad_fill(meta, xs)
    ys = _experts(items, xs, exp_w_gate[0], exp_w_up[0], exp_w_down[0], n_blocks=n_blocks)
    parts = COMBINE_PARTS if steps % COMBINE_PARTS == 0 else 1
    steps_part = steps // parts
    x1f, w_tok = x1.reshape(t, d), w_t
    sgw, suw, sdw = sh_w_gate[0].astype(BF16), sh_w_up[0].astype(BF16), sh_w_down[0].astype(BF16)
    out = None
    for p in range(parts):
        idx = dest[p * steps_part:(p + 1) * steps_part].reshape(-1)
        yg = _sc_gather_rows(ys, idx).reshape(steps_part, TOP_K, tbe, d // 2)
        out = _combine(hp, x1f, g2, w_tok, sgw, suw, sdw, yg, out, tb=tbe, seq_len=l, first_step=p * steps_part)
    return out.reshape(b, l, d)
```

```python
import functools

import jax
import jax.numpy as jnp
from jax import lax
from jax.experimental import pallas as pl
from jax.experimental.pallas import tpu as pltpu
from jax.experimental.pallas import tpu_sc as plsc

F32 = jnp.float32
BF16 = jnp.bfloat16
I32 = jnp.int32
U32 = jnp.uint32
PACKED = jnp.int32

HEAD_DIM = 64
LANES = 128
SUBLANES = 8
ATTN_HEADS = 8
ATTN_KV_HEADS = 2
GQA = ATTN_HEADS // ATTN_KV_HEADS
RET_HEADS = 8
ATTN_W = ATTN_HEADS * HEAD_DIM
KV_W = ATTN_KV_HEADS * HEAD_DIM
RET_W = RET_HEADS * HEAD_DIM
RET_PAIRS = RET_W // LANES
CHUNK = 128
GRID_W = 64
ROPE_THETA = 10000.0
N_EXPERTS = 256
TOP_K = 8
N_GROUPS = 8
GROUP_SIZE = N_EXPERTS // N_GROUPS
TOPK_GROUPS = 4
ROUTED_SCALE = 2.5
MOE_BLOCK = 128
EPS = 1e-6
QK_SCALE = HEAD_DIM ** -0.5
LOG2_E = 1.4426950408889634
KEY_TILE = 256
BOUND_MARGIN = 1.02
MAX_STREAM_SHIFT = 56.0

OFF_AK = 0
OFF_AV = OFF_AK + KV_W
OFF_RK = OFF_AV + KV_W
OFF_RV = OFF_RK + RET_W
CTX_KV_COLS = OFF_RV + RET_W
OFF_AQ = CTX_KV_COLS
OFF_RQ = OFF_AQ + ATTN_W
OFF_RG = OFF_RQ + RET_W
IN_COLS = OFF_RG + RET_W

V7X_VMEM_BYTES = 64 * 1024 * 1024
VMEM_LIMIT = V7X_VMEM_BYTES * 13 // 16

SUB_ROWS = 256
TILE_PROJ = 1024
TILE_ATTN_Q = 1024
TILE_OUT = 512
TILE_TOKENS = 512
ITEM_BLOCKS = 5
COMBINE_PARTS = 4
SC_GATHER_BUFS = 2
WEIGHT_SLOTS = 3
SC_WINDOW = 64
HI_MASK = 0xFFFF0000


def _cparams(*sem):
    return pltpu.CompilerParams(dimension_semantics=sem, vmem_limit_bytes=VMEM_LIMIT)


def _split(a):
    hi = a.astype(BF16)
    lo = (a - hi.astype(F32)).astype(BF16)
    return hi, lo


def _dot(a, b):
    return jnp.dot(a, b, preferred_element_type=F32)


def _dot_nt(a, b):
    return lax.dot_general(a, b, (((1,), (1,)), ((), ())), preferred_element_type=F32)


def _sigmoid(v):
    return 1.0 / (1.0 + jnp.exp(-v))


def _silu(v):
    return v * _sigmoid(v)


def _pack_halves(a, b):
    ua = lax.bitcast_convert_type(a.astype(BF16).astype(F32), U32)
    ub = lax.bitcast_convert_type(b.astype(BF16).astype(F32), U32)
    return lax.bitcast_convert_type((ua & jnp.uint32(HI_MASK)) | (ub >> 16), PACKED)


def _unpack_halves(p):
    u = lax.bitcast_convert_type(p, U32)
    a = lax.bitcast_convert_type(u & jnp.uint32(HI_MASK), F32)
    b = lax.bitcast_convert_type(u << 16, F32)
    return a, b


def _mod_kernel(c_ref, w_ref, b_ref, o_ref):
    s_hi, s_lo = _split(_silu(c_ref[...]))
    w_hi, w_lo = _split(w_ref[...])
    o_ref[...] = _dot(s_hi, w_hi) + _dot(s_hi, w_lo) + _dot(s_lo, w_hi) + b_ref[...]


def _modulation(cc, w_mod, b_mod):
    rows, d = cc.shape
    n = w_mod.shape[1]
    tn = 768
    return pl.pallas_call(
        _mod_kernel,
        grid=(n // tn,),
        in_specs=[pl.BlockSpec((rows, d), lambda j: (0, 0)),
                  pl.BlockSpec((d, tn), lambda j: (0, j)),
                  pl.BlockSpec((1, tn), lambda j: (0, j))],
        out_specs=pl.BlockSpec((rows, tn), lambda j: (0, j)),
        out_shape=jax.ShapeDtypeStruct((rows, n), F32),
        compiler_params=_cparams("arbitrary"),
        name="mod",
    )(cc, w_mod, b_mod.reshape(1, n))


def _segment_ones():
    r = lax.broadcasted_iota(I32, (LANES, LANES), 0) // HEAD_DIM
    c = lax.broadcasted_iota(I32, (LANES, LANES), 1) // HEAD_DIM
    return jnp.where(r == c, 1.0, 0.0).astype(BF16)


def _head_mean_sq(v, seg):
    hi, lo = _split(v * v)
    return (_dot(hi, seg) + _dot(lo, seg)) * (1.0 / HEAD_DIM)


def _proj_kernel(x_ref, sh_ref, sc_ref, nw_ref, wi_ref, qnw_ref, knw_ref, cos_ref, sin_ref,
                 *out_refs, rope, with_q):
    if with_q:
        klo_ref, khi_ref, vlo_ref, vhi_ref, rk_ref, rv_ref, q_ref, rq_ref, sg_ref = out_refs
    else:
        klo_ref, khi_ref, vlo_ref, vhi_ref, rk_ref, rv_ref = out_refs
    tm = x_ref.shape[1]
    sub = min(tm, SUB_ROWS)
    seg = _segment_ones()
    lane = lax.broadcasted_iota(I32, (sub, LANES), 1)
    low_half = lane < HEAD_DIM
    even = (lane & 1) == 0

    for r0 in range(0, tm, sub):
        rows = slice(r0, r0 + sub)
        x = x_ref[0, rows, :]
        h = x * lax.rsqrt(jnp.mean(x * x, axis=-1, keepdims=True) + EPS) * nw_ref[...]
        h = h * (1.0 + sc_ref[0]) + sh_ref[0]
        z = _dot(h.astype(BF16), wi_ref[...])

        def norm_rope(v, w128, rows=rows):
            v = v * lax.rsqrt(_head_mean_sq(v, seg) + EPS) * w128
            if rope:
                swapped = jnp.where(even, pltpu.roll(v, LANES - 1, 1), pltpu.roll(v, 1, 1))
                v = v * cos_ref[rows, :] + swapped * sin_ref[rows, :]
            return v

        k = norm_rope(z[:, OFF_AK:OFF_AK + KV_W], knw_ref[...])
        ksw = pltpu.roll(k, HEAD_DIM, 1)
        klo_ref[0, 0, :, rows] = jnp.transpose(jnp.where(low_half, k, 0.0)).astype(BF16)
        khi_ref[0, 0, :, rows] = jnp.transpose(jnp.where(low_half, 0.0, ksw)).astype(BF16)
        klo_ref[0, 1, :, rows] = jnp.transpose(jnp.where(low_half, ksw, 0.0)).astype(BF16)
        khi_ref[0, 1, :, rows] = jnp.transpose(jnp.where(low_half, 0.0, k)).astype(BF16)
        v = z[:, OFF_AV:OFF_AV + KV_W]
        vsw = pltpu.roll(v, HEAD_DIM, 1)
        vlo_ref[0, 0, rows, :] = jnp.where(low_half, v, 0.0).astype(BF16)
        vhi_ref[0, 0, rows, :] = jnp.where(low_half, 0.0, vsw).astype(BF16)
        vlo_ref[0, 1, rows, :] = jnp.where(low_half, vsw, 0.0).astype(BF16)
        vhi_ref[0, 1, rows, :] = jnp.where(low_half, 0.0, v).astype(BF16)
        rk_ref[0, rows, :] = (z[:, OFF_RK:OFF_RK + RET_W] * QK_SCALE).astype(BF16)
        rv_ref[0, rows, :] = z[:, OFF_RV:OFF_RV + RET_W].astype(BF16)
        if with_q:
            for j in range(ATTN_W // LANES):
                qj = norm_rope(z[:, OFF_AQ + j * LANES:OFF_AQ + (j + 1) * LANES], qnw_ref[...])
                q_ref[0, rows, j * LANES:(j + 1) * LANES] = (qj * (QK_SCALE * LOG2_E)).astype(BF16)
            rq_ref[0, rows, :] = z[:, OFF_RQ:OFF_RQ + RET_W].astype(BF16)
            sg_ref[0, rows, :] = _silu(z[:, OFF_RG:OFF_RG + RET_W]).astype(BF16)


def _projection(x, shift, scale, norm_w, wi_bf16, qnw, knw, cos, sin, *, rope, with_q, tm):
    b, l, d = x.shape
    tm = min(tm, l)
    ncols = IN_COLS if with_q else CTX_KV_COLS
    per_batch = shift.shape[0] > 1
    mod_idx = (lambda bi, i: (bi, 0, 0)) if per_batch else (lambda bi, i: (0, 0, 0))
    kv_shape = jax.ShapeDtypeStruct((b, ATTN_KV_HEADS, l, LANES), BF16)
    kv_spec = pl.BlockSpec((1, ATTN_KV_HEADS, tm, LANES), lambda bi, i: (bi, 0, i, 0))
    kt_shape = jax.ShapeDtypeStruct((b, ATTN_KV_HEADS, LANES, l), BF16)
    kt_spec = pl.BlockSpec((1, ATTN_KV_HEADS, LANES, tm), lambda bi, i: (bi, 0, 0, i))
    w_shape = jax.ShapeDtypeStruct((b, l, RET_W), BF16)
    w_spec = pl.BlockSpec((1, tm, RET_W), lambda bi, i: (bi, i, 0))
    out_shape = [kt_shape] * 2 + [kv_shape] * 2 + [w_shape] * 2
    out_specs = [kt_spec] * 2 + [kv_spec] * 2 + [w_spec] * 2
    if with_q:
        out_shape += [w_shape] * 3
        out_specs += [w_spec] * 3
    return pl.pallas_call(
        functools.partial(_proj_kernel, rope=rope, with_q=with_q),
        grid=(b, l // tm),
        in_specs=[pl.BlockSpec((1, tm, d), lambda bi, i: (bi, i, 0)),
                  pl.BlockSpec((1, 1, d), mod_idx),
                  pl.BlockSpec((1, 1, d), mod_idx),
                  pl.BlockSpec((1, d), lambda bi, i: (0, 0)),
                  pl.BlockSpec((d, ncols), lambda bi, i: (0, 0)),
                  pl.BlockSpec((1, LANES), lambda bi, i: (0, 0)),
                  pl.BlockSpec((1, LANES), lambda bi, i: (0, 0)),
                  pl.BlockSpec((tm, LANES), lambda bi, i: (i, 0)),
                  pl.BlockSpec((tm, LANES), lambda bi, i: (i, 0))],
        out_specs=out_specs,
        out_shape=out_shape,
        compiler_params=_cparams("arbitrary", "arbitrary"),
        name="proj_latent" if with_q else "proj_ctx",
    )(x, shift, scale, norm_w, wi_bf16, qnw, knw, cos, sin)


def _attn_kernel(shift_ref, q_ref, klo_ref, khi_ref, vlo_ref, vhi_ref, cklo_ref, ckhi_ref, cvlo_ref, cvhi_ref,
                 o_ref, kl_s, kh_s, va_s, *, l, lc, streaming):
    lk = l + lc

    @pl.when(pl.program_id(2) == 0)
    def _():
        kl_s[:, 0:l] = klo_ref[0, 0]
        kl_s[:, l:lk] = cklo_ref[0, 0]
        kh_s[:, 0:l] = khi_ref[0, 0]
        kh_s[:, l:lk] = ckhi_ref[0, 0]
        lane = lax.broadcasted_iota(I32, (lk, LANES), 1)
        ones_lo = jnp.where(lane < HEAD_DIM, 1.0, 0.0).astype(BF16)
        ones_hi = jnp.where(lane < HEAD_DIM, 0.0, 1.0).astype(BF16)
        for g, (v_ref, cv_ref, ones) in enumerate(((vlo_ref, cvlo_ref, ones_lo), (vhi_ref, cvhi_ref, ones_hi),
                                                   (vlo_ref, cvlo_ref, ones_lo), (vhi_ref, cvhi_ref, ones_hi))):
            v_col, one_col = (0, LANES) if g < 2 else (LANES, 0)
            va_s[g, 0:l, v_col:v_col + LANES] = v_ref[0, 0]
            va_s[g, l:lk, v_col:v_col + LANES] = cv_ref[0, 0]
            va_s[g, :, one_col:one_col + LANES] = ones

    q = q_ref[0]
    acc = []
    for g in range(GQA):
        qp = q[:, (g // 2) * LANES:(g // 2 + 1) * LANES]
        k_s = kl_s if g % 2 == 0 else kh_s
        if streaming:
            shift = shift_ref[0]
            o = None
            for c in range(0, lk, KEY_TILE):
                hi = min(c + KEY_TILE, lk)
                p = jnp.exp2(_dot(qp, k_s[:, c:hi]) - shift).astype(BF16)
                t = _dot(p, va_s[g, c:hi])
                o = t if o is None else o + t
        else:
            s = _dot(qp, k_s[...])
            p = jnp.exp2(s - jnp.max(s, axis=-1, keepdims=True)).astype(BF16)
            o = _dot(p, va_s[g])
        acc.append(o)
    out_a = acc[0] + acc[1]
    out_b = acc[2] + acc[3]
    o_ref[0, :, 0:LANES] = (out_a[:, 0:LANES] / out_a[:, LANES:2 * LANES]).astype(BF16)
    o_ref[0, :, LANES:2 * LANES] = (out_b[:, LANES:2 * LANES] / out_b[:, 0:LANES]).astype(BF16)


def _attention(shift, q, klo, khi, vlo, vhi, cklo, ckhi, cvlo, cvhi, *, tq, streaming):
    b, l, _ = q.shape
    lc = cvlo.shape[2]
    lk = l + lc
    tq = min(tq, l)
    gw = GQA * HEAD_DIM
    kt_spec = pl.BlockSpec((1, 1, LANES, l), lambda bi, h, i: (bi, h, 0, 0))
    kv_spec = pl.BlockSpec((1, 1, l, LANES), lambda bi, h, i: (bi, h, 0, 0))
    ckt_spec = pl.BlockSpec((1, 1, LANES, lc), lambda bi, h, i: (bi, h, 0, 0))
    ckv_spec = pl.BlockSpec((1, 1, lc, LANES), lambda bi, h, i: (bi, h, 0, 0))
    return pl.pallas_call(
        functools.partial(_attn_kernel, l=l, lc=lc, streaming=streaming),
        grid=(b, ATTN_KV_HEADS, l // tq),
        in_specs=([pl.BlockSpec(memory_space=pltpu.SMEM),
                   pl.BlockSpec((1, tq, gw), lambda bi, h, i: (bi, i, h))] + [kt_spec] * 2 + [kv_spec] * 2
                  + [ckt_spec] * 2 + [ckv_spec] * 2),
        out_specs=pl.BlockSpec((1, tq, gw), lambda bi, h, i: (bi, i, h)),
        out_shape=jax.ShapeDtypeStruct((b, l, ATTN_W), BF16),
        scratch_shapes=[pltpu.VMEM((LANES, lk), BF16), pltpu.VMEM((LANES, lk), BF16),
                        pltpu.VMEM((GQA, lk, 2 * LANES), BF16)],
        compiler_params=_cparams("arbitrary", "arbitrary", "arbitrary"),
        name="attn_stream" if streaming else "attn",
    )(shift, q, klo, khi, vlo, vhi, cklo, ckhi, cvlo, cvhi)


def _log_sigmoid(v):
    return jnp.minimum(v, 0.0) - jnp.log(1.0 + jnp.exp(-jnp.abs(v)))


def _ret_kernel(rq_ref, rk_ref, rv_ref, sg_ref, crk_ref, crv_ref, df_ref, db_ref, o_ref,
                m_s, xi_s, zeta_s, kv_s, st_s, *, l, lc):
    n = l // CHUNK
    nc = lc // CHUNK
    lgf = _log_sigmoid(df_ref[...])
    lgb = _log_sigmoid(db_ref[...])
    pos = lax.broadcasted_iota(I32, (CHUNK, LANES), 0).astype(F32)
    row = lax.broadcasted_iota(I32, (CHUNK, CHUNK), 0)
    col = lax.broadcasted_iota(I32, (CHUNK, CHUNK), 1)
    diff = (row - col).astype(F32)
    g_chunk = []
    for p in range(RET_PAIRS):
        cols = slice(p * LANES, (p + 1) * LANES)
        lf, lb = lgf[:, cols], lgb[:, cols]
        xi_s[p, :, 0:LANES] = jnp.exp((pos + 1.0) * lf)
        xi_s[p, :, LANES:] = jnp.exp((CHUNK - pos) * lb)
        zeta_s[p, :, 0:LANES] = jnp.exp((CHUNK - 1.0 - pos) * lf)
        zeta_s[p, :, LANES:] = jnp.exp(pos * lb)
        g_chunk.append((jnp.exp(CHUNK * lf), jnp.exp(CHUNK * lb)))
        for j in range(2):
            h = 2 * p + j
            hf = lgf[:, h * HEAD_DIM:h * HEAD_DIM + 1]
            hb = lgb[:, h * HEAD_DIM:h * HEAD_DIM + 1]
            m_s[p, :, j * CHUNK:(j + 1) * CHUNK] = jnp.where(
                diff > 0, jnp.exp(diff * hf), jnp.where(diff < 0, jnp.exp(-diff * hb), 2.0))

    lane = lax.broadcasted_iota(I32, (CHUNK, LANES), 1)
    low_half = lane < HEAD_DIM
    diag = (lax.broadcasted_iota(I32, (LANES, LANES), 0) // HEAD_DIM
            == lax.broadcasted_iota(I32, (LANES, LANES), 1) // HEAD_DIM)
    seg = jnp.where(diag, 1.0, 0.0).astype(BF16)
    seg2 = jnp.concatenate([seg, seg], axis=0)
    diag2 = jnp.concatenate([diag, diag], axis=0)

    def split_heads(a):
        zero = jnp.zeros_like(a)
        return jnp.concatenate([jnp.where(low_half, a, zero), jnp.where(low_half, zero, a)], axis=0)

    def contrib(k_ref, v_ref, r0, p):
        cols = slice(p * LANES, (p + 1) * LANES)
        kp = k_ref[0, pl.ds(r0, CHUNK), cols].astype(F32)
        kz = jnp.concatenate([kp, kp], axis=1) * zeta_s[p]
        kv = _dot(jnp.transpose(kz).astype(BF16), v_ref[0, pl.ds(r0, CHUNK), cols])
        return jnp.where(diag2, kv, 0.0)

    for c in range(nc):
        for p in range(RET_PAIRS):
            kv_s[c, p] = contrib(crk_ref, crv_ref, c * CHUNK, p)

    def contrib_body(c, carry):
        r0 = pl.multiple_of(c * CHUNK, CHUNK)
        for p in range(RET_PAIRS):
            kv_s[nc + c, p] = contrib(rk_ref, rv_ref, r0, p)
        return carry

    lax.fori_loop(0, n, contrib_body, 0, unroll=4)

    for p in range(RET_PAIRS):
        gf, gb = g_chunk[p]
        sf = jnp.zeros((LANES, LANES), F32)
        sb = jnp.zeros((LANES, LANES), F32)
        for c in range(nc):
            sf = gf * sf + kv_s[c, p, 0:LANES]
            sb = gb * sb + kv_s[nc - 1 - c, p, LANES:]

        def fwd_scan(c, s, p=p, gf=gf):
            st_s[c, p, 0:LANES] = s.astype(BF16)
            return gf * s + kv_s[nc + c, p, 0:LANES]

        def bwd_scan(j, s, p=p, gb=gb):
            c = n - 1 - j
            st_s[c, p, LANES:] = s.astype(BF16)
            return gb * s + kv_s[nc + c, p, LANES:]

        lax.fori_loop(0, n, fwd_scan, sf)
        lax.fori_loop(0, n, bwd_scan, sb)

    def out_body(c, carry):
        r0 = pl.multiple_of(c * CHUNK, CHUNK)
        for p in range(RET_PAIRS):
            cols = slice(p * LANES, (p + 1) * LANES)
            qp = rq_ref[0, pl.ds(r0, CHUNK), cols]
            kp = rk_ref[0, pl.ds(r0, CHUNK), cols]
            vp = rv_ref[0, pl.ds(r0, CHUNK), cols]
            s2 = _dot_nt(qp, split_heads(kp))
            a2 = (s2 * m_s[p]).astype(BF16)
            y = _dot(a2, split_heads(vp))
            qf = qp.astype(F32)
            qx = (jnp.concatenate([qf, qf], axis=1) * xi_s[p]).astype(BF16)
            y += _dot(qx, st_s[c, p])
            hi, lo = _split(y * y)
            ms = _dot(jnp.concatenate([hi, lo], axis=1), seg2) * (1.0 / HEAD_DIM)
            out = y * lax.rsqrt(ms + EPS) * sg_ref[0, pl.ds(r0, CHUNK), cols].astype(F32)
            o_ref[0, pl.ds(r0, CHUNK), cols] = out.astype(BF16)
        return carry

    lax.fori_loop(0, n, out_body, 0, unroll=4)


def _retention(rq, rk, rv, sg, crk, crv, dec_f, dec_b):
    b, l, _ = rq.shape
    lc = crk.shape[1]
    n = l // CHUNK
    nc = lc // CHUNK
    spec = pl.BlockSpec((1, l, RET_W), lambda bi: (bi, 0, 0))
    cspec = pl.BlockSpec((1, lc, RET_W), lambda bi: (bi, 0, 0))
    dspec = pl.BlockSpec((1, RET_W), lambda bi: (0, 0))
    return pl.pallas_call(
        functools.partial(_ret_kernel, l=l, lc=lc),
        grid=(b,),
        in_specs=[spec, spec, spec, spec, cspec, cspec, dspec, dspec],
        out_specs=spec,
        out_shape=jax.ShapeDtypeStruct((b, l, RET_W), BF16),
        scratch_shapes=[pltpu.VMEM((RET_PAIRS, CHUNK, 2 * CHUNK), F32),
                        pltpu.VMEM((RET_PAIRS, CHUNK, 2 * LANES), F32),
                        pltpu.VMEM((RET_PAIRS, CHUNK, 2 * LANES), F32),
                        pltpu.VMEM((nc + n, RET_PAIRS, 2 * LANES, LANES), F32),
                        pltpu.VMEM((n, RET_PAIRS, 2 * LANES, LANES), BF16)],
        compiler_params=_cparams("arbitrary"),
        name="ret",
    )(rq, rk, rv, sg, crk, crv, dec_f, dec_b)


def _out_kernel(attn_ref, ret_ref, x_ref, wa_ref, wr_ref, g1_ref, sh_ref, sc_ref, nw_ref, rhi_ref, rlo_ref,
                x1_ref, hp_ref, lg_ref):
    y = _dot(attn_ref[0], wa_ref[...]) + _dot(ret_ref[0], wr_ref[...])
    x1 = x_ref[0] + g1_ref[0] * y
    x1_ref[0] = x1
    h = x1 * lax.rsqrt(jnp.mean(x1 * x1, axis=-1, keepdims=True) + EPS) * nw_ref[...]
    h = h * (1.0 + sc_ref[0]) + sh_ref[0]
    half = h.shape[1] // 2
    hp_ref[...] = _pack_halves(h[:, :half], h[:, half:])
    h_hi, h_lo = _split(h)
    lg_ref[...] = _dot_nt(rhi_ref[...], h_hi) + _dot_nt(rhi_ref[...], h_lo) + _dot_nt(rlo_ref[...], h_hi)


def _out_projection(attn, ret, x, w_out, g1, sh2, sc2, norm_w, r_hi, r_lo, *, tm):
    assert ATTN_W % RET_W == 0
    wa = wr = w_out
    b, l, d = x.shape
    tm = min(tm, l)
    nt = l // tm
    t = b * l
    mspec = pl.BlockSpec((1, 1, d), lambda bi, i: (bi, 0, 0))
    return pl.pallas_call(
        _out_kernel,
        grid=(b, nt),
        in_specs=[pl.BlockSpec((1, tm, ATTN_W), lambda bi, i: (bi, i, 0)),
                  pl.BlockSpec((1, tm, RET_W), lambda bi, i: (bi, i, 0)),
                  pl.BlockSpec((1, tm, d), lambda bi, i: (bi, i, 0)),
                  pl.BlockSpec((ATTN_W, d), lambda bi, i: (0, 0)),
                  pl.BlockSpec((RET_W, d), lambda bi, i: (ATTN_W // RET_W, 0)),
                  mspec, mspec, mspec,
                  pl.BlockSpec((1, d), lambda bi, i: (0, 0)),
                  pl.BlockSpec((N_EXPERTS, d), lambda bi, i: (0, 0)),
                  pl.BlockSpec((N_EXPERTS, d), lambda bi, i: (0, 0))],
        out_specs=[pl.BlockSpec((1, tm, d), lambda bi, i: (bi, i, 0)),
                   pl.BlockSpec((tm, d // 2), lambda bi, i: (bi * nt + i, 0)),
                   pl.BlockSpec((N_EXPERTS, tm), lambda bi, i: (0, bi * nt + i))],
        out_shape=[jax.ShapeDtypeStruct((b, l, d), F32),
                   jax.ShapeDtypeStruct((t, d // 2), PACKED),
                   jax.ShapeDtypeStruct((N_EXPERTS, t), F32)],
        compiler_params=_cparams("arbitrary", "arbitrary"),
        name="out_proj",
    )(attn, ret, x, wa, wr, g1, sh2, sc2, norm_w, r_hi, r_lo)


def _route_kernel(lg_ref, bias_ref, idx_ref, w_ref, rank_ref, cnt_col_ref, cnt_row_ref, tri_s, col_s, row_s):
    tb = lg_ref.shape[1]
    step = pl.program_id(0)

    @pl.when(step == 0)
    def _():
        r = lax.broadcasted_iota(I32, (tb, tb), 0)
        c = lax.broadcasted_iota(I32, (tb, tb), 1)
        tri_s[...] = jnp.where(r <= c, 1.0, 0.0).astype(BF16)
        col_s[...] = jnp.zeros_like(col_s)
        row_s[...] = jnp.zeros_like(row_s)

    scores = _sigmoid(lg_ref[...])
    biased = scores + bias_ref[...]
    neg = -jnp.inf
    sub = lax.broadcasted_iota(I32, (GROUP_SIZE, tb), 0).astype(F32)

    gscore = []
    for g in range(N_GROUPS):
        blk = biased[g * GROUP_SIZE:(g + 1) * GROUP_SIZE]
        m1 = jnp.max(blk, axis=0, keepdims=True)
        first = jnp.min(jnp.where(blk == m1, sub, float(GROUP_SIZE)), axis=0, keepdims=True)
        m2 = jnp.max(jnp.where(sub == first, neg, blk), axis=0, keepdims=True)
        gscore.append(m1 + m2)
    gs = jnp.concatenate(gscore, axis=0)
    gsub = lax.broadcasted_iota(I32, (N_GROUPS, tb), 0).astype(F32)
    keep = jnp.zeros((N_GROUPS, tb), F32)
    for _ in range(TOPK_GROUPS):
        m = jnp.max(gs, axis=0, keepdims=True)
        first = jnp.min(jnp.where(gs == m, gsub, float(N_GROUPS)), axis=0, keepdims=True)
        sel = gsub == first
        keep = jnp.where(sel, 1.0, keep)
        gs = jnp.where(sel, neg, gs)
    masked = jnp.concatenate(
        [jnp.where(keep[g:g + 1] > 0.0, biased[g * GROUP_SIZE:(g + 1) * GROUP_SIZE], neg)
         for g in range(N_GROUPS)], axis=0)

    esub = lax.broadcasted_iota(I32, (N_EXPERTS, tb), 0).astype(F32)
    sels, idxs, ws = [], [], []
    chosen = jnp.zeros((N_EXPERTS, tb), F32)
    for _ in range(TOP_K):
        m = jnp.max(masked, axis=0, keepdims=True)
        first = jnp.min(jnp.where(masked == m, esub, float(N_EXPERTS)), axis=0, keepdims=True)
        sel = esub == first
        sels.append(sel)
        idxs.append(first)
        ws.append(jnp.sum(jnp.where(sel, scores, 0.0), axis=0, keepdims=True))
        chosen = jnp.where(sel, 1.0, chosen)
        masked = jnp.where(sel, neg, masked)
    wsum = ws[0]
    for k in range(1, TOP_K):
        wsum = wsum + ws[k]
    idx_ref[...] = jnp.concatenate(idxs, axis=0).astype(I32)
    w_ref[...] = jnp.concatenate([wk / wsum * ROUTED_SCALE for wk in ws], axis=0)

    chosen_b = chosen.astype(BF16)
    incl = _dot(chosen_b, tri_s[...])
    before = incl - chosen + col_s[...]
    rank_ref[...] = jnp.concatenate(
        [jnp.sum(jnp.where(sel, before, 0.0), axis=0, keepdims=True) for sel in sels], axis=0).astype(I32)
    col_s[...] = col_s[...] + incl[:, tb - 1:tb]
    row_s[...] = row_s[...] + _dot_nt(jnp.ones((8, tb), BF16), chosen_b)
    cnt_col_ref[...] = col_s[...].astype(I32)
    cnt_row_ref[...] = row_s[...].astype(I32)


def _route(logits_t, bias_col, *, tb):
    e, t = logits_t.shape
    tb = min(tb, t)
    kspec = pl.BlockSpec((TOP_K, tb), lambda i: (0, i))
    return pl.pallas_call(
        _route_kernel,
        grid=(t // tb,),
        in_specs=[pl.BlockSpec((e, tb), lambda i: (0, i)),
                  pl.BlockSpec((e, 1), lambda i: (0, 0))],
        out_specs=[kspec, kspec, kspec,
                   pl.BlockSpec((e, 1), lambda i: (0, 0)),
                   pl.BlockSpec((8, e), lambda i: (0, 0))],
        out_shape=[jax.ShapeDtypeStruct((TOP_K, t), I32),
                   jax.ShapeDtypeStruct((TOP_K, t), F32),
                   jax.ShapeDtypeStruct((TOP_K, t), I32),
                   jax.ShapeDtypeStruct((e, 1), I32),
                   jax.ShapeDtypeStruct((8, e), I32)],
        scratch_shapes=[pltpu.VMEM((tb, tb), BF16), pltpu.VMEM((e, 1), F32), pltpu.VMEM((8, e), F32)],
        compiler_params=_cparams("arbitrary"),
        name="route",
    )(logits_t, bias_col)


def _pad_block(cnt):
    return (cnt + (MOE_BLOCK - 1)) // MOE_BLOCK * MOE_BLOCK


def _max_items(n_blocks):
    return n_blocks // ITEM_BLOCKS + N_EXPERTS


def _dest_kernel(idx_ref, rank_ref, cnt_col_ref, cnt_row_ref, dest_ref, meta_ref, items_ref, start_s):
    tb = idx_ref.shape[1]
    nip = items_ref.shape[1]

    @pl.when(pl.program_id(0) == 0)
    def _():
        pad_col = _pad_block(cnt_col_ref[...])
        pad_row = _pad_block(cnt_row_ref[0:1, :])
        er = lax.broadcasted_iota(I32, (N_EXPERTS, N_EXPERTS), 0)
        ec = lax.broadcasted_iota(I32, (N_EXPERTS, N_EXPERTS), 1)
        start_col = jnp.sum(jnp.where(ec < er, pad_row, 0), axis=1, keepdims=True)
        start_row = jnp.sum(jnp.where(er < ec, pad_col, 0), axis=0, keepdims=True)
        start_s[...] = start_col

        used = jnp.sum(pad_row, axis=1, keepdims=True) // MOE_BLOCK
        meta_ref[...] = jnp.concatenate(
            [cnt_row_ref[0:1, :], start_row, pad_row, jnp.broadcast_to(used, (1, N_EXPERTS)),
             jnp.zeros((4, N_EXPERTS), I32)], axis=0)

        nb_col = pad_col // MOE_BLOCK
        it_col = (nb_col + (ITEM_BLOCKS - 1)) // ITEM_BLOCKS
        it_row = (pad_row // MOE_BLOCK + (ITEM_BLOCKS - 1)) // ITEM_BLOCKS
        it_start = jnp.sum(jnp.where(ec < er, it_row, 0), axis=1, keepdims=True)
        n_items = jnp.sum(it_row, axis=1, keepdims=True)
        lane = lax.broadcasted_iota(I32, (1, nip), 1)
        owner = jnp.sum(jnp.where(it_start + it_col <= lane, 1, 0), axis=0, keepdims=True)
        owner = jnp.minimum(owner, N_EXPERTS - 1)
        onehot = lax.broadcasted_iota(I32, (N_EXPERTS, nip), 0) == owner

        def pick(col):
            return jnp.sum(jnp.where(onehot, col, 0), axis=0, keepdims=True)

        j = lane - pick(it_start)
        block0 = pick(start_col) // MOE_BLOCK + ITEM_BLOCKS * j
        nvalid = jnp.clip(pick(nb_col) - ITEM_BLOCKS * j, 0, ITEM_BLOCKS)
        items_ref[...] = jnp.concatenate(
            [owner, block0, jnp.where(lane < n_items, nvalid, 0), jnp.broadcast_to(n_items, (1, nip)),
             jnp.zeros((4, nip), I32)], axis=0)

    start_col = start_s[...]
    esub = lax.broadcasted_iota(I32, (N_EXPERTS, tb), 0)
    rows = []
    for k in range(TOP_K):
        onehot = esub == idx_ref[k:k + 1, :]
        rows.append(jnp.sum(jnp.where(onehot, start_col, 0), axis=0, keepdims=True) + rank_ref[k:k + 1, :])
    dest_ref[0] = jnp.concatenate(rows, axis=0)


def _destinations(idx_t, rank_t, cnt_col, cnt_row, *, tb, n_blocks):
    _, t = idx_t.shape
    tb = min(tb, t)
    nip = (_max_items(n_blocks) + LANES - 1) // LANES * LANES
    kspec = pl.BlockSpec((TOP_K, tb), lambda i: (0, i))
    return pl.pallas_call(
        _dest_kernel,
        grid=(t // tb,),
        in_specs=[kspec, kspec,
                  pl.BlockSpec((N_EXPERTS, 1), lambda i: (0, 0)),
                  pl.BlockSpec((8, N_EXPERTS), lambda i: (0, 0))],
        out_specs=[pl.BlockSpec((1, TOP_K, tb), lambda i: (i, 0, 0)),
                   pl.BlockSpec((8, N_EXPERTS), lambda i: (0, 0)),
                   pl.BlockSpec((8, nip), lambda i: (0, 0))],
        out_shape=[jax.ShapeDtypeStruct((t // tb, TOP_K, tb), I32),
                   jax.ShapeDtypeStruct((8, N_EXPERTS), I32),
                   jax.ShapeDtypeStruct((8, nip), I32)],
        scratch_shapes=[pltpu.VMEM((N_EXPERTS, 1), I32)],
        compiler_params=_cparams("arbitrary"),
        name="dest",
    )(idx_t, rank_t, cnt_col, cnt_row)


_PAD_BITS = (64, 32, 16, 8)


def _sc_scatter_rows(rows, dest_win, n_out_rows):
    n_win, n_slots, win = dest_win.shape
    width = rows.shape[1]
    info = plsc.get_sparse_core_info()
    n_workers = info.num_cores * info.num_subcores
    per_worker = n_win // n_workers
    assert per_worker * n_workers == n_win and win <= LANES
    mesh = plsc.VectorSubcoreMesh(core_axis_name="c", subcore_axis_name="s")

    def body(rows_hbm, dest_hbm, out_hbm, idx_v, rows_v, sem):
        wid = lax.axis_index("s") * info.num_cores + lax.axis_index("c")

        @pl.loop(0, per_worker)
        def _(j):
            w = wid * per_worker + j
            pltpu.sync_copy(dest_hbm.at[w], idx_v)
            pltpu.sync_copy(rows_hbm.at[pl.ds(w * win, win)], rows_v)
            copies = [pltpu.async_copy(rows_v, out_hbm.at[idx_v.at[k]], sem) for k in range(n_slots)]
            for cp in copies:
                cp.wait()

    return pl.kernel(
        body,
        out_type=jax.ShapeDtypeStruct((n_out_rows, width), rows.dtype),
        mesh=mesh,
        scratch_types=[pltpu.VMEM((n_slots, win), I32), pltpu.VMEM((win, width), rows.dtype),
                       pltpu.SemaphoreType.DMA],
        name="sc_scatter",
    )(rows, dest_win)


def _pad_fill_kernel(meta_ref, xs_in, xs_hbm, zero_s, sem_z, *, e_per_step):
    del xs_in
    step = pl.program_id(0)
    zero_s[...] = jnp.zeros_like(zero_s)

    def tail_copy(c):
        row0 = pl.multiple_of((meta_ref[3, 0] + c) * MOE_BLOCK, MOE_BLOCK)
        return pltpu.make_async_copy(zero_s, xs_hbm.at[pl.ds(row0, MOE_BLOCK)], sem_z)

    @pl.when(step == 0)
    def _():
        for c in range(ITEM_BLOCKS - 1):
            tail_copy(c).start()
        for c in range(ITEM_BLOCKS - 1):
            tail_copy(c).wait()

    def pad_copies(e):
        cnt = meta_ref[0, e]
        off = meta_ref[1, e] + cnt
        rem = meta_ref[2, e] - cnt
        head = rem & (SUBLANES - 1)
        out = []
        for i in range(SUBLANES - 1):
            out.append((i < head,
                        pltpu.make_async_copy(zero_s.at[pl.ds(0, 1)], xs_hbm.at[pl.ds(off + i, 1)], sem_z)))
        off = off + head
        for bit in _PAD_BITS:
            out.append(((rem & bit) != 0,
                        pltpu.make_async_copy(zero_s.at[pl.ds(0, bit)],
                                              xs_hbm.at[pl.ds(pl.multiple_of(off, SUBLANES), bit)], sem_z)))
            off = off + (rem & bit)
        return out

    for j in range(e_per_step):
        for cond, c in pad_copies(step * e_per_step + j):
            pl.when(cond)(c.start)
    for j in range(e_per_step):
        for cond, c in pad_copies(step * e_per_step + j):
            pl.when(cond)(c.wait)


PAD_FILL_EXPERTS_PER_STEP = 8


def _pad_fill(meta, xs):
    half = xs.shape[1]
    return pl.pallas_call(
        functools.partial(_pad_fill_kernel, e_per_step=PAD_FILL_EXPERTS_PER_STEP),
        grid=(N_EXPERTS // PAD_FILL_EXPERTS_PER_STEP,),
        in_specs=[pl.BlockSpec(memory_space=pltpu.SMEM),
                  pl.BlockSpec(memory_space=pl.ANY)],
        out_specs=pl.BlockSpec(memory_space=pl.ANY),
        out_shape=jax.ShapeDtypeStruct(xs.shape, xs.dtype),
        input_output_aliases={1: 0},
        scratch_shapes=[pltpu.VMEM((MOE_BLOCK, half), PACKED), pltpu.SemaphoreType.DMA],
        compiler_params=_cparams("arbitrary"),
        name="pad_fill",
    )(meta, xs)


def _experts_kernel(items_ref, xs_hbm, wg_hbm, wu_hbm, wd_hbm, ys_hbm,
                    xbuf, ybuf, wg_f, wu_f, wd_f, wg_s, wu_s, wd_s, sem_x, sem_y, sem_w):
    n_items = items_ref[3, 0]
    rows = ITEM_BLOCKS * MOE_BLOCK

    def w_copies(e, s):
        return [pltpu.make_async_copy(src.at[e], dst.at[s], sem_w.at[s])
                for src, dst in ((wg_hbm, wg_f), (wu_hbm, wu_f), (wd_hbm, wd_f))]

    def x_copy(item, s):
        row0 = pl.multiple_of(items_ref[1, item] * MOE_BLOCK, MOE_BLOCK)
        return pltpu.make_async_copy(xs_hbm.at[pl.ds(row0, rows)], xbuf.at[s], sem_x.at[s])

    def y_copies(item, s, fn):
        for k in range(ITEM_BLOCKS):
            row0 = pl.multiple_of((items_ref[1, item] + k) * MOE_BLOCK, MOE_BLOCK)
            cp = pltpu.make_async_copy(ybuf.at[s, pl.ds(k * MOE_BLOCK, MOE_BLOCK)],
                                       ys_hbm.at[pl.ds(row0, MOE_BLOCK)], sem_y.at[s])
            pl.when(k < items_ref[2, item])(functools.partial(fn, cp))

    def expert_of(item):
        return items_ref[0, jnp.minimum(item, n_items - 1)]

    def changes_at(item):
        return ((item < n_items) & (expert_of(item) != expert_of(item - 1))).astype(I32)

    @pl.when(n_items > 0)
    def _():
        x_copy(0, 0).start()
        for cp in w_copies(expert_of(0), 0):
            cp.start()

        @pl.when(changes_at(1) == 1)
        def _():
            for cp in w_copies(expert_of(1), 1):
                cp.start()

    def item_body(i, ordinal):
        slot = i % 2
        prev = jnp.maximum(i - 1, 0)
        e = expert_of(i)
        new_expert = (i == 0) | (e != expert_of(prev))
        c1 = changes_at(i + 1)
        c2 = changes_at(i + 2)

        @pl.when(i + 1 < n_items)
        def _():
            x_copy(i + 1, 1 - slot).start()

        @pl.when(new_expert)
        def _():
            wslot = ordinal % WEIGHT_SLOTS
            for cp in w_copies(e, wslot):
                cp.wait()
            wg_s[...] = wg_f[wslot].astype(BF16)
            wu_s[...] = wu_f[wslot].astype(BF16)
            wd_s[...] = wd_f[wslot].astype(BF16)

        @pl.when(c2 == 1)
        def _():
            for cp in w_copies(expert_of(i + 2), (ordinal + c1 + 1) % WEIGHT_SLOTS):
                cp.start()

        x_copy(i, slot).wait()
        xa, xb = _unpack_halves(xbuf[slot])
        xa = xa.astype(BF16)
        xb = xb.astype(BF16)
        half = xa.shape[1]
        g = _dot(xa, wg_s[0:half]) + _dot(xb, wg_s[half:])
        u = _dot(xa, wu_s[0:half]) + _dot(xb, wu_s[half:])
        y = _dot((_silu(g) * u).astype(BF16), wd_s[...])
        ybuf[slot] = _pack_halves(y[:, :half], y[:, half:])
        y_copies(i, slot, lambda cp: cp.start())

        @pl.when(i > 0)
        def _():
            y_copies(prev, 1 - slot, lambda cp: cp.wait())

        return ordinal + c1

    lax.fori_loop(0, n_items, item_body, jnp.int32(0))

    @pl.when(n_items > 0)
    def _():
        last = n_items - 1
        y_copies(last, last % 2, lambda cp: cp.wait())


def _experts(items, xs, w_gate, w_up, w_down, *, n_blocks):
    half = xs.shape[1]
    e, d, f = w_gate.shape
    rows = ITEM_BLOCKS * MOE_BLOCK
    any_spec = pl.BlockSpec(memory_space=pl.ANY)
    return pl.pallas_call(
        _experts_kernel,
        grid_spec=pltpu.PrefetchScalarGridSpec(
            num_scalar_prefetch=1,
            grid=(1,),
            in_specs=[any_spec, any_spec, any_spec, any_spec],
            out_specs=any_spec,
            scratch_shapes=[pltpu.VMEM((2, rows, half), PACKED), pltpu.VMEM((2, rows, half), PACKED),
                            pltpu.VMEM((WEIGHT_SLOTS, d, f), F32), pltpu.VMEM((WEIGHT_SLOTS, d, f), F32),
                            pltpu.VMEM((WEIGHT_SLOTS, f, d), F32),
                            pltpu.VMEM((d, f), BF16), pltpu.VMEM((d, f), BF16), pltpu.VMEM((f, d), BF16),
                            pltpu.SemaphoreType.DMA((2,)), pltpu.SemaphoreType.DMA((2,)),
                            pltpu.SemaphoreType.DMA((WEIGHT_SLOTS,))]),
        out_shape=jax.ShapeDtypeStruct((n_blocks * MOE_BLOCK, half), PACKED),
        compiler_params=_cparams("arbitrary"),
        name="experts",
    )(items, xs, w_gate, w_up, w_down)


def _sc_gather_rows(table, idx):
    n_idx = idx.shape[0]
    width = table.shape[1]
    info = plsc.get_sparse_core_info()
    n_workers = info.num_cores * info.num_subcores
    per_worker = n_idx // n_workers
    assert per_worker * n_workers == n_idx and per_worker % (SC_GATHER_BUFS * SC_WINDOW) == 0
    mesh = plsc.VectorSubcoreMesh(core_axis_name="c", subcore_axis_name="s")

    def body(table_hbm, idx_hbm, out_hbm, idx_v, rows_v, sem_g, sem_o):
        wid = lax.axis_index("s") * info.num_cores + lax.axis_index("c")
        base = wid * per_worker

        @pl.loop(0, per_worker // (SC_GATHER_BUFS * SC_WINDOW))
        def _(it):
            offs = [base + (it * SC_GATHER_BUFS + b) * SC_WINDOW for b in range(SC_GATHER_BUFS)]
            gathers = []
            for b, off in enumerate(offs):
                pltpu.sync_copy(idx_hbm.at[pl.ds(off, SC_WINDOW)], idx_v.at[b])
                gathers.append(pltpu.async_copy(table_hbm.at[idx_v.at[b]], rows_v.at[b], sem_g.at[b]))
            writes = []
            for b, off in enumerate(offs):
                gathers[b].wait()
                writes.append(pltpu.async_copy(rows_v.at[b], out_hbm.at[pl.ds(off, SC_WINDOW)], sem_o.at[b]))
            for cp in writes:
                cp.wait()

    return pl.kernel(
        body,
        out_type=jax.ShapeDtypeStruct((n_idx, width), table.dtype),
        mesh=mesh,
        scratch_types=[pltpu.VMEM((SC_GATHER_BUFS, SC_WINDOW), I32),
                       pltpu.VMEM((SC_GATHER_BUFS, SC_WINDOW, width), table.dtype),
                       pltpu.SemaphoreType.DMA((SC_GATHER_BUFS,)), pltpu.SemaphoreType.DMA((SC_GATHER_BUFS,))],
        name="sc_gather",
    )(table, idx)


def _shared_base_kernel(hp_ref, x1_ref, g2_ref, sgw_ref, suw_ref, sdw_ref, order_ref, o_ref):
    del order_ref
    xa, xb = _unpack_halves(hp_ref[...])
    xa = xa.astype(BF16)
    xb = xb.astype(BF16)
    half = xa.shape[1]
    g = _dot(xa, sgw_ref[0:half]) + _dot(xb, sgw_ref[half:])
    u = _dot(xa, suw_ref[0:half]) + _dot(xb, suw_ref[half:])
    shared = _dot((_silu(g) * u).astype(BF16), sdw_ref[...])
    o_ref[...] = x1_ref[...] + g2_ref[0] * shared


def _shared_base(hp, x1, g2, sgw, suw, sdw, order_after, *, tb, seq_len):
    t, half = hp.shape
    d = 2 * half
    per_seq = seq_len // tb
    f = sgw.shape[1]
    return pl.pallas_call(
        _shared_base_kernel,
        grid=(t // tb,),
        in_specs=[pl.BlockSpec((tb, half), lambda i: (i, 0)),
                  pl.BlockSpec((tb, d), lambda i: (i, 0)),
                  pl.BlockSpec((1, 1, d), lambda i: (i // per_seq, 0, 0)),
                  pl.BlockSpec((d, f), lambda i: (0, 0)),
                  pl.BlockSpec((d, f), lambda i: (0, 0)),
                  pl.BlockSpec((f, d), lambda i: (0, 0)),
                  pl.BlockSpec(memory_space=pl.ANY)],
        out_specs=pl.BlockSpec((tb, d), lambda i: (i, 0)),
        out_shape=jax.ShapeDtypeStruct((t, d), F32),
        compiler_params=_cparams("arbitrary"),
        name="shared_base",
    )(hp, x1, g2, sgw, suw, sdw, order_after)


def _combine_kernel(base_ref, g2_ref, w_ref, yg_ref, *rest):
    o_ref = rest[-1]
    tb, d = base_ref.shape
    half = d // 2
    w = jnp.transpose(w_ref[...])
    acc_a = jnp.zeros((tb, half), F32)
    acc_b = jnp.zeros((tb, half), F32)
    for k in range(TOP_K):
        ya, yb = _unpack_halves(yg_ref[0, k])
        acc_a += ya * w[:, k:k + 1]
        acc_b += yb * w[:, k:k + 1]
    g2 = g2_ref[0]
    o_ref[:, 0:half] = base_ref[:, 0:half] + g2[:, 0:half] * acc_a
    o_ref[:, half:] = base_ref[:, half:] + g2[:, half:] * acc_b


def _combine(base, g2, w_tok, yg, out_prev, *, tb, seq_len, first_step):
    t, d = base.shape
    half = d // 2
    per_seq = seq_len // tb
    s0 = first_step
    in_specs = [pl.BlockSpec((tb, d), lambda i: (s0 + i, 0)),
                pl.BlockSpec((1, 1, d), lambda i: ((s0 + i) // per_seq, 0, 0)),
                pl.BlockSpec((TOP_K, tb), lambda i: (0, s0 + i)),
                pl.BlockSpec((1, TOP_K, tb, half), lambda i: (i, 0, 0, 0))]
    args = [base, g2, w_tok, yg]
    aliases = {}
    if out_prev is not None:
        in_specs.append(pl.BlockSpec(memory_space=pl.ANY))
        args.append(out_prev)
        aliases = {len(args) - 1: 0}
    return pl.pallas_call(
        _combine_kernel,
        grid=(yg.shape[0],),
        in_specs=in_specs,
        out_specs=pl.BlockSpec((tb, d), lambda i: (s0 + i, 0)),
        out_shape=jax.ShapeDtypeStruct((t, d), F32),
        input_output_aliases=aliases,
        compiler_params=_cparams("arbitrary"),
        name="combine",
    )(*args)


def _rope_tables(l):
    rows = l // GRID_W
    r = jnp.repeat(jnp.arange(rows), GRID_W).astype(F32)
    col = jnp.tile(jnp.arange(GRID_W), rows).astype(F32)
    n_f = HEAD_DIM // 4
    freqs = ROPE_THETA ** (-jnp.arange(n_f, dtype=F32) / n_f)
    ang = jnp.concatenate([r[:, None] * freqs, col[:, None] * freqs], axis=-1)
    ang = jnp.tile(jnp.repeat(ang, 2, axis=1), (1, LANES // HEAD_DIM))
    sign = jnp.where(jnp.arange(LANES) % 2 == 0, -1.0, 1.0).astype(F32)
    return jnp.cos(ang), jnp.sin(ang) * sign


def kernel(x, c, ctx, c_ctx, w_mod, b_mod, norm1_w, norm2_w, w_in, q_norm_w, k_norm_w, ret_decay_fwd,
           ret_decay_bwd, w_out, router_w, router_bias, exp_w_gate, exp_w_up, exp_w_down, sh_w_gate,
           sh_w_up, sh_w_down):
    b, l, d = x.shape
    lc = ctx.shape[1]
    t = b * l
    assert w_mod.shape[0] == 1, "single layer"
    assert l % CHUNK == 0 and lc % CHUNK == 0 and l % GRID_W == 0

    rows = (b + 1 + 7) // 8 * 8
    cc = jnp.zeros((rows, d), F32).at[:b].set(c).at[b].set(c_ctx)
    mod = _modulation(cc, w_mod[0], b_mod[0])
    sh1, sc1, g1, sh2, sc2, g2 = [mod[:b, i * d:(i + 1) * d].reshape(b, 1, d) for i in range(6)]
    shc = mod[b, 0:d].reshape(1, 1, d)
    scc = mod[b, d:2 * d].reshape(1, 1, d)

    wi = w_in[0].astype(BF16)
    qnw = jnp.tile(q_norm_w[0], LANES // HEAD_DIM).reshape(1, LANES)
    knw = jnp.tile(k_norm_w[0], LANES // HEAD_DIM).reshape(1, LANES)
    cos, sin = _rope_tables(l)
    n1 = norm1_w[0].reshape(1, d)

    cklo, ckhi, cvlo, cvhi, crk, crv = _projection(
        ctx, shc, scc, n1, wi, qnw, knw, cos[:lc], sin[:lc], rope=False, with_q=False, tm=TILE_PROJ)
    klo, khi, vlo, vhi, rk, rv, q, rq, sg = _projection(
        x, sh1, sc1, n1, wi, qnw, knw, cos, sin, rope=True, with_q=True, tm=TILE_PROJ)

    bound = (HEAD_DIM * QK_SCALE * LOG2_E * BOUND_MARGIN
             * jnp.max(jnp.abs(q_norm_w[0])) * jnp.max(jnp.abs(k_norm_w[0]))).astype(F32)
    attn_args = (bound.reshape(1), q, klo, khi, vlo, vhi, cklo, ckhi, cvlo, cvhi)
    attn = lax.cond(bound <= MAX_STREAM_SHIFT,
                    functools.partial(_attention, tq=TILE_ATTN_Q, streaming=True),
                    functools.partial(_attention, tq=TILE_ATTN_Q, streaming=False), *attn_args)
    dec_f = jnp.repeat(ret_decay_fwd[0].astype(F32), HEAD_DIM).reshape(1, RET_W)
    dec_b = jnp.repeat(ret_decay_bwd[0].astype(F32), HEAD_DIM).reshape(1, RET_W)
    ret = _retention(rq, rk, rv, sg, crk, crv, dec_f, dec_b)

    wo = w_out[0].astype(BF16)
    r_hi, r_lo = _split(router_w[0].T)
    x1, hp, logits_t = _out_projection(attn, ret, x, wo, g1, sh2, sc2,
                                       norm2_w[0].reshape(1, d), r_hi, r_lo, tm=TILE_OUT)

    idx_t, w_t, rank_t, cnt_col, cnt_row = _route(logits_t, router_bias[0].reshape(N_EXPERTS, 1), tb=TILE_TOKENS)
    n_blocks = -(-(t * TOP_K) // MOE_BLOCK) + N_EXPERTS
    tb = TILE_TOKENS
    dest, meta, items = _destinations(idx_t, rank_t, cnt_col, cnt_row, tb=tb, n_blocks=n_blocks)
    steps, _, tbe = dest.shape
    dest_win = dest.reshape(steps, TOP_K, tbe // SC_WINDOW, SC_WINDOW).transpose(0, 2, 1, 3)
    dest_win = dest_win.reshape(t // SC_WINDOW, TOP_K, SC_WINDOW)
    xs = _sc_scatter_rows(hp, dest_win, (n_blocks + ITEM_BLOCKS - 1) * MOE_BLOCK)
    sgw, suw, sdw = sh_w_gate[0].astype(BF16), sh_w_up[0].astype(BF16), sh_w_down[0].astype(BF16)
    base = _shared_base(hp, x1.reshape(t, d), g2, sgw, suw, sdw, dest_win, tb=tbe, seq_len=l)
    xs = _pad_fill(meta, xs)
    ys = _experts(items, xs, exp_w_gate[0], exp_w_up[0], exp_w_down[0], n_blocks=n_blocks)
    parts = COMBINE_PARTS if steps % COMBINE_PARTS == 0 else 1
    steps_part = steps // parts
    out = None
    for p in range(parts):
        idx = dest[p * steps_part:(p + 1) * steps_part].reshape(-1)
        yg = _sc_gather_rows(ys, idx).reshape(steps_part, TOP_K, tbe, d // 2)
        out = _combine(base, g2, w_t, yg, out, tb=tbe, seq_len=l, first_step=p * steps_part)
    return out.reshape(b, l, d)
```
